```python
import math
import jax, jax.numpy as jnp
from jax import lax
import numpy as np

D_MODEL = 1024
BATCH = 8
SEQ = 4096
DEPTH = 2

N_MIXERS = 4
MIXER_WIDTH = D_MODEL // N_MIXERS
D_MIX = N_MIXERS * MIXER_WIDTH
GROUPS_PER_MIXER = 4
GROUP_DIM = MIXER_WIDTH // GROUPS_PER_MIXER
D_IN_PROJ = 8 * MIXER_WIDTH
SCONV_K = 3
SGU_CHUNK = 128
CCONV_K = 31
POOL_WINDOWS = (2, 4, 8, 16)
N_XATTN_HEADS = 4
XATTN_HEAD_DIM = D_MODEL // N_XATTN_HEADS
N_MEM = 256
D_FF = 2816
EPS = 1e-6

kernel_name = "hybrid_parallel_group_macaron_decoder"


def rmsnorm(x, g):
    xf = x.astype(jnp.float32)
    y = xf * lax.rsqrt(jnp.mean(xf * xf, axis=-1, keepdims=True) + EPS)
    return (y * g.astype(jnp.float32)).astype(x.dtype)


def layernorm(x, g, b=None):
    xf = x.astype(jnp.float32)
    mu = jnp.mean(xf, axis=-1, keepdims=True)
    var = jnp.mean(jnp.square(xf - mu), axis=-1, keepdims=True)
    y = (xf - mu) * lax.rsqrt(var + EPS) * g.astype(jnp.float32)
    if b is not None:
        y = y + b.astype(jnp.float32)
    return y.astype(x.dtype)


def swiglu_ffn(h, w_in, w_out):
    g, u = jnp.split(h @ w_in, 2, axis=-1)
    return (jax.nn.silu(g) * u) @ w_out


def causal_depthwise_conv(x, w):
    k = w.shape[0]
    return lax.conv_general_dilated(
        x, w[:, None, :].astype(x.dtype), window_strides=(1,), padding=((k - 1, 0),),
        dimension_numbers=("NWC", "WIO", "NWC"), feature_group_count=x.shape[-1])


def mixer_short_conv(bg, cg, xt, conv_w):
    return bg * causal_depthwise_conv(cg * xt, conv_w)


def mixer_spatial_gating(u, v, norm_g, w_s, b_s):
    bsz, s, _ = v.shape
    vn = layernorm(v, norm_g)
    vr = vn.reshape(bsz, s // SGU_CHUNK, SGU_CHUNK, GROUPS_PER_MIXER, GROUP_DIM)
    w_causal = jnp.tril(w_s)
    mixed = jnp.einsum("hts,bcshd->bcthd", w_causal.astype(vr.dtype), vr)
    mixed = mixed + b_s.T[None, None, :, :, None].astype(vr.dtype)
    return u * mixed.reshape(bsz, s, MIXER_WIDTH)


def mixer_conformer_conv(a, g, conv_w, ln_g, ln_b):
    y = a * jax.nn.sigmoid(g)
    y = causal_depthwise_conv(y, conv_w)
    y = layernorm(y, ln_g, ln_b)
    return jax.nn.silu(y)


def mixer_multiscale_pool(w, pool_w, pool_scale):
    bsz, s, _ = w.shape
    wf = w.astype(jnp.float32)
    cs = jnp.cumsum(wf, axis=1)
    pos = jnp.arange(s, dtype=jnp.int32)
    outs = []
    for gi, k in enumerate(POOL_WINDOWS):
        sl = slice(gi * GROUP_DIM, (gi + 1) * GROUP_DIM)
        csg = cs[:, :, sl]
        shifted = jnp.pad(csg, ((0, 0), (k, 0), (0, 0)))[:, :s]
        count = jnp.minimum(pos + 1, k).astype(jnp.float32)[None, :, None]
        outs.append((csg - shifted) / count - wf[:, :, sl])
    pooled = jnp.stack(outs, axis=2).astype(w.dtype)
    y = jnp.einsum("bsgc,gcd->bsgd", pooled, pool_w)
    return y.reshape(bsz, s, MIXER_WIDTH) * pool_scale


def cross_attention(h, m, wq, wkv, wo):
    bsz, s, _ = h.shape
    q = (h @ wq).reshape(bsz, s, N_XATTN_HEADS, XATTN_HEAD_DIM)
    k, v = jnp.split(m @ wkv, 2, axis=-1)
    k = k.reshape(bsz, N_MEM, N_XATTN_HEADS, XATTN_HEAD_DIM)
    v = v.reshape(bsz, N_MEM, N_XATTN_HEADS, XATTN_HEAD_DIM)
    scores = jnp.einsum("bshd,bmhd->bhsm", q, k).astype(jnp.float32) / math.sqrt(XATTN_HEAD_DIM)
    p = jax.nn.softmax(scores, axis=-1).astype(v.dtype)
    o = jnp.einsum("bhsm,bmhd->bshd", p, v).reshape(bsz, s, D_MODEL)
    return o @ wo


def _fwd_setup_inputs(seed: int = 0) -> dict:
    key = jax.random.key(seed)
    ks = iter(jax.random.split(key, 32))
    L, D, W = DEPTH, D_MODEL, MIXER_WIDTH

    def nrm(shape, fan_in):
        return jax.random.normal(next(ks), shape, jnp.float32) * (fan_in ** -0.5)

    def gain(shape):
        return 1.0 + 0.02 * jax.random.normal(next(ks), shape, jnp.float32)

    def small(shape):
        return 0.02 * jax.random.normal(next(ks), shape, jnp.float32)

    return {
        "x": jax.random.normal(next(ks), (BATCH, SEQ, D), jnp.float32),
        "mem": jax.random.normal(next(ks), (BATCH, N_MEM, D), jnp.float32),
        "norm_ffn1": gain((L, D)),
        "ffn1_w_in": nrm((L, D, 2 * D_FF), D),
        "ffn1_w_out": nrm((L, D_FF, D), D_FF),
        "norm_mix": gain((L, D)),
        "mix_w_in": nrm((L, D, D_IN_PROJ), D),
        "sconv_w": nrm((L, SCONV_K, W), SCONV_K),
        "sgu_norm_g": gain((L, W)),
        "sgu_w": nrm((L, GROUPS_PER_MIXER, SGU_CHUNK, SGU_CHUNK), SGU_CHUNK),
        "sgu_b": gain((L, GROUPS_PER_MIXER, SGU_CHUNK)),
        "cconv_w": nrm((L, CCONV_K, W), CCONV_K),
        "cconv_ln_g": gain((L, W)),
        "cconv_ln_b": small((L, W)),
        "pool_w": nrm((L, len(POOL_WINDOWS), GROUP_DIM, GROUP_DIM), GROUP_DIM),
        "pool_scale": gain((L, W)),
        "mix_w_out": nrm((L, D_MIX, D), D_MIX),
        "norm_xattn": gain((L, D)),
        "norm_mem": gain((L, D)),
        "xattn_wq": nrm((L, D, D), D),
        "xattn_wkv": nrm((L, D, 2 * D), D),
        "xattn_wo": nrm((L, D, D), D),
        "norm_ffn2": gain((L, D)),
        "ffn2_w_in": nrm((L, D, 2 * D_FF), D),
        "ffn2_w_out": nrm((L, D_FF, D), D_FF),
        "norm_final": gain((D,)),
    }


def _fwd_reference(x, mem, norm_ffn1, ffn1_w_in, ffn1_w_out, norm_mix, mix_w_in, sconv_w,
              sgu_norm_g, sgu_w, sgu_b, cconv_w, cconv_ln_g, cconv_ln_b, pool_w, pool_scale,
              mix_w_out, norm_xattn, norm_mem, xattn_wq, xattn_wkv, xattn_wo,
              norm_ffn2, ffn2_w_in, ffn2_w_out, norm_final):
    W = MIXER_WIDTH
    split_points = [W, 2 * W, 3 * W, 4 * W, 5 * W, 6 * W, 7 * W]
    for l in range(DEPTH):
        x = x + 0.5 * swiglu_ffn(rmsnorm(x, norm_ffn1[l]), ffn1_w_in[l], ffn1_w_out[l])

        h = rmsnorm(x, norm_mix[l])
        z = h @ mix_w_in[l]
        a_b, a_c, a_x, b_u, b_v, c_a, c_g, d_w = jnp.split(z, split_points, axis=-1)
        y_a = mixer_short_conv(a_b, a_c, a_x, sconv_w[l])
        y_b = mixer_spatial_gating(b_u, b_v, sgu_norm_g[l], sgu_w[l], sgu_b[l])
        y_c = mixer_conformer_conv(c_a, c_g, cconv_w[l], cconv_ln_g[l], cconv_ln_b[l])
        y_d = mixer_multiscale_pool(d_w, pool_w[l], pool_scale[l])
        y = jnp.concatenate([y_a, y_b, y_c, y_d], axis=-1)
        x = x + y @ mix_w_out[l]

        x = x + cross_attention(rmsnorm(x, norm_xattn[l]), rmsnorm(mem, norm_mem[l]),
                                xattn_wq[l], xattn_wkv[l], xattn_wo[l])

        x = x + 0.5 * swiglu_ffn(rmsnorm(x, norm_ffn2[l]), ffn2_w_in[l], ffn2_w_out[l])
    return rmsnorm(x, norm_final)


import jax as _jax
import jax.numpy as _jnp

TWIN_FORMAT = 'train_step'
FWD_PARAMS = ['x', 'mem', 'norm_ffn1', 'ffn1_w_in', 'ffn1_w_out', 'norm_mix', 'mix_w_in', 'sconv_w', 'sgu_norm_g', 'sgu_w', 'sgu_b', 'cconv_w', 'cconv_ln_g', 'cconv_ln_b', 'pool_w', 'pool_scale', 'mix_w_out', 'norm_xattn', 'norm_mem', 'xattn_wq', 'xattn_wkv', 'xattn_wo', 'norm_ffn2', 'ffn2_w_in', 'ffn2_w_out', 'norm_final']
TWIN_WEIGHTS = ['norm_ffn1', 'ffn1_w_in', 'ffn1_w_out', 'norm_mix', 'mix_w_in', 'sconv_w', 'sgu_norm_g', 'sgu_w', 'sgu_b', 'cconv_w', 'cconv_ln_g', 'cconv_ln_b', 'pool_w', 'pool_scale', 'mix_w_out', 'norm_xattn', 'norm_mem', 'xattn_wq', 'xattn_wkv', 'xattn_wo', 'norm_ffn2', 'ffn2_w_in', 'ffn2_w_out', 'norm_final']
TWIN_DIFF_INPUT = 'x'
TWIN_INPUTS = ['x', 'mem', 'norm_ffn1', 'ffn1_w_in', 'ffn1_w_out', 'norm_mix', 'mix_w_in', 'sconv_w', 'sgu_norm_g', 'sgu_w', 'sgu_b', 'cconv_w', 'cconv_ln_g', 'cconv_ln_b', 'pool_w', 'pool_scale', 'mix_w_out', 'norm_xattn', 'norm_mem', 'xattn_wq', 'xattn_wkv', 'xattn_wo', 'norm_ffn2', 'ffn2_w_in', 'ffn2_w_out', 'norm_final', 'loss_target', 'm_norm_ffn1', 'm_ffn1_w_in', 'm_ffn1_w_out', 'm_norm_mix', 'm_mix_w_in', 'm_sconv_w', 'm_sgu_norm_g', 'm_sgu_w', 'm_sgu_b', 'm_cconv_w', 'm_cconv_ln_g', 'm_cconv_ln_b', 'm_pool_w', 'm_pool_scale', 'm_mix_w_out', 'm_norm_xattn', 'm_norm_mem', 'm_xattn_wq', 'm_xattn_wkv', 'm_xattn_wo', 'm_norm_ffn2', 'm_ffn2_w_in', 'm_ffn2_w_out', 'm_norm_final', 'v_norm_ffn1', 'v_ffn1_w_in', 'v_ffn1_w_out', 'v_norm_mix', 'v_mix_w_in', 'v_sconv_w', 'v_sgu_norm_g', 'v_sgu_w', 'v_sgu_b', 'v_cconv_w', 'v_cconv_ln_g', 'v_cconv_ln_b', 'v_pool_w', 'v_pool_scale', 'v_mix_w_out', 'v_norm_xattn', 'v_norm_mem', 'v_xattn_wq', 'v_xattn_wkv', 'v_xattn_wo', 'v_norm_ffn2', 'v_ffn2_w_in', 'v_ffn2_w_out', 'v_norm_final']
TWIN_OUTPUTS = ['loss', 'grad_x', 'grad_norm_ffn1', 'grad_ffn1_w_in', 'grad_ffn1_w_out', 'grad_norm_mix', 'grad_mix_w_in', 'grad_sconv_w', 'grad_sgu_norm_g', 'grad_sgu_w', 'grad_sgu_b', 'grad_cconv_w', 'grad_cconv_ln_g', 'grad_cconv_ln_b', 'grad_pool_w', 'grad_pool_scale', 'grad_mix_w_out', 'grad_norm_xattn', 'grad_norm_mem', 'grad_xattn_wq', 'grad_xattn_wkv', 'grad_xattn_wo', 'grad_norm_ffn2', 'grad_ffn2_w_in', 'grad_ffn2_w_out', 'grad_norm_final', 'delta_norm_ffn1', 'delta_ffn1_w_in', 'delta_ffn1_w_out', 'delta_norm_mix', 'delta_mix_w_in', 'delta_sconv_w', 'delta_sgu_norm_g', 'delta_sgu_w', 'delta_sgu_b', 'delta_cconv_w', 'delta_cconv_ln_g', 'delta_cconv_ln_b', 'delta_pool_w', 'delta_pool_scale', 'delta_mix_w_out', 'delta_norm_xattn', 'delta_norm_mem', 'delta_xattn_wq', 'delta_xattn_wkv', 'delta_xattn_wo', 'delta_norm_ffn2', 'delta_ffn2_w_in', 'delta_ffn2_w_out', 'delta_norm_final', 'new_m_norm_ffn1', 'new_m_ffn1_w_in', 'new_m_ffn1_w_out', 'new_m_norm_mix', 'new_m_mix_w_in', 'new_m_sconv_w', 'new_m_sgu_norm_g', 'new_m_sgu_w', 'new_m_sgu_b', 'new_m_cconv_w', 'new_m_cconv_ln_g', 'new_m_cconv_ln_b', 'new_m_pool_w', 'new_m_pool_scale', 'new_m_mix_w_out', 'new_m_norm_xattn', 'new_m_norm_mem', 'new_m_xattn_wq', 'new_m_xattn_wkv', 'new_m_xattn_wo', 'new_m_norm_ffn2', 'new_m_ffn2_w_in', 'new_m_ffn2_w_out', 'new_m_norm_final', 'new_v_norm_ffn1', 'new_v_ffn1_w_in', 'new_v_ffn1_w_out', 'new_v_norm_mix', 'new_v_mix_w_in', 'new_v_sconv_w', 'new_v_sgu_norm_g', 'new_v_sgu_w', 'new_v_sgu_b', 'new_v_cconv_w', 'new_v_cconv_ln_g', 'new_v_cconv_ln_b', 'new_v_pool_w', 'new_v_pool_scale', 'new_v_mix_w_out', 'new_v_norm_xattn', 'new_v_norm_mem', 'new_v_xattn_wq', 'new_v_xattn_wkv', 'new_v_xattn_wo', 'new_v_norm_ffn2', 'new_v_ffn2_w_in', 'new_v_ffn2_w_out', 'new_v_norm_final']
TWIN_LEAF_KINDS = {'loss': 'loss', 'grad_x': 'grad_x', 'grad_norm_ffn1': 'grad_w', 'grad_ffn1_w_in': 'grad_w', 'grad_ffn1_w_out': 'grad_w', 'grad_norm_mix': 'grad_w', 'grad_mix_w_in': 'grad_w', 'grad_sconv_w': 'grad_w', 'grad_sgu_norm_g': 'grad_w', 'grad_sgu_w': 'grad_w', 'grad_sgu_b': 'grad_w', 'grad_cconv_w': 'grad_w', 'grad_cconv_ln_g': 'grad_w', 'grad_cconv_ln_b': 'grad_w', 'grad_pool_w': 'grad_w', 'grad_pool_scale': 'grad_w', 'grad_mix_w_out': 'grad_w', 'grad_norm_xattn': 'grad_w', 'grad_norm_mem': 'grad_w', 'grad_xattn_wq': 'grad_w', 'grad_xattn_wkv': 'grad_w', 'grad_xattn_wo': 'grad_w', 'grad_norm_ffn2': 'grad_w', 'grad_ffn2_w_in': 'grad_w', 'grad_ffn2_w_out': 'grad_w', 'grad_norm_final': 'grad_w', 'delta_norm_ffn1': 'delta_w', 'delta_ffn1_w_in': 'delta_w', 'delta_ffn1_w_out': 'delta_w', 'delta_norm_mix': 'delta_w', 'delta_mix_w_in': 'delta_w', 'delta_sconv_w': 'delta_w', 'delta_sgu_norm_g': 'delta_w', 'delta_sgu_w': 'delta_w', 'delta_sgu_b': 'delta_w', 'delta_cconv_w': 'delta_w', 'delta_cconv_ln_g': 'delta_w', 'delta_cconv_ln_b': 'delta_w', 'delta_pool_w': 'delta_w', 'delta_pool_scale': 'delta_w', 'delta_mix_w_out': 'delta_w', 'delta_norm_xattn': 'delta_w', 'delta_norm_mem': 'delta_w', 'delta_xattn_wq': 'delta_w', 'delta_xattn_wkv': 'delta_w', 'delta_xattn_wo': 'delta_w', 'delta_norm_ffn2': 'delta_w', 'delta_ffn2_w_in': 'delta_w', 'delta_ffn2_w_out': 'delta_w', 'delta_norm_final': 'delta_w', 'new_m_norm_ffn1': 'new_m', 'new_m_ffn1_w_in': 'new_m', 'new_m_ffn1_w_out': 'new_m', 'new_m_norm_mix': 'new_m', 'new_m_mix_w_in': 'new_m', 'new_m_sconv_w': 'new_m', 'new_m_sgu_norm_g': 'new_m', 'new_m_sgu_w': 'new_m', 'new_m_sgu_b': 'new_m', 'new_m_cconv_w': 'new_m', 'new_m_cconv_ln_g': 'new_m', 'new_m_cconv_ln_b': 'new_m', 'new_m_pool_w': 'new_m', 'new_m_pool_scale': 'new_m', 'new_m_mix_w_out': 'new_m', 'new_m_norm_xattn': 'new_m', 'new_m_norm_mem': 'new_m', 'new_m_xattn_wq': 'new_m', 'new_m_xattn_wkv': 'new_m', 'new_m_xattn_wo': 'new_m', 'new_m_norm_ffn2': 'new_m', 'new_m_ffn2_w_in': 'new_m', 'new_m_ffn2_w_out': 'new_m', 'new_m_norm_final': 'new_m', 'new_v_norm_ffn1': 'new_v', 'new_v_ffn1_w_in': 'new_v', 'new_v_ffn1_w_out': 'new_v', 'new_v_norm_mix': 'new_v', 'new_v_mix_w_in': 'new_v', 'new_v_sconv_w': 'new_v', 'new_v_sgu_norm_g': 'new_v', 'new_v_sgu_w': 'new_v', 'new_v_sgu_b': 'new_v', 'new_v_cconv_w': 'new_v', 'new_v_cconv_ln_g': 'new_v', 'new_v_cconv_ln_b': 'new_v', 'new_v_pool_w': 'new_v', 'new_v_pool_scale': 'new_v', 'new_v_mix_w_out': 'new_v', 'new_v_norm_xattn': 'new_v', 'new_v_norm_mem': 'new_v', 'new_v_xattn_wq': 'new_v', 'new_v_xattn_wkv': 'new_v', 'new_v_xattn_wo': 'new_v', 'new_v_norm_ffn2': 'new_v', 'new_v_ffn2_w_in': 'new_v', 'new_v_ffn2_w_out': 'new_v', 'new_v_norm_final': 'new_v'}


def _forward(args):
    return _fwd_reference(*[args[k] for k in FWD_PARAMS])


def _output_shape():
    def fwd():
        inp = _fwd_setup_inputs(0)
        return _fwd_reference(*[inp[k] for k in FWD_PARAMS])
    out = _jax.eval_shape(fwd)
    return out.shape, out.dtype

N_MICROBATCH = 1
ADAM_LR = 0.001
ADAM_B1 = 0.9
ADAM_B2 = 0.999
ADAM_EPS = 1e-08
ADAM_WD = 0.01
ADAM_STEP = 10
PER_EXAMPLE_BATCH_AXIS = {'x': 0, 'mem': 0, 'loss_target': 0}
SHARED_INPUTS = []
_WEIGHT_DTYPES = {'norm_ffn1': _jnp.float32, 'ffn1_w_in': _jnp.float32, 'ffn1_w_out': _jnp.float32, 'norm_mix': _jnp.float32, 'mix_w_in': _jnp.float32, 'sconv_w': _jnp.float32, 'sgu_norm_g': _jnp.float32, 'sgu_w': _jnp.float32, 'sgu_b': _jnp.float32, 'cconv_w': _jnp.float32, 'cconv_ln_g': _jnp.float32, 'cconv_ln_b': _jnp.float32, 'pool_w': _jnp.float32, 'pool_scale': _jnp.float32, 'mix_w_out': _jnp.float32, 'norm_xattn': _jnp.float32, 'norm_mem': _jnp.float32, 'xattn_wq': _jnp.float32, 'xattn_wkv': _jnp.float32, 'xattn_wo': _jnp.float32, 'norm_ffn2': _jnp.float32, 'ffn2_w_in': _jnp.float32, 'ffn2_w_out': _jnp.float32, 'norm_final': _jnp.float32}
MOMENT_SCALE = {'norm_ffn1': 8.282438e-02, 'ffn1_w_in': 3.410784e-02, 'ffn1_w_out': 5.569248e-02, 'norm_mix': 1.630646e-01, 'mix_w_in': 1.107535e-01, 'sconv_w': 1.319822e-01, 'sgu_norm_g': 8.945187e-02, 'sgu_w': 6.263579e-02, 'sgu_b': 9.013947e-02, 'cconv_w': 7.888026e-02, 'cconv_ln_g': 9.050912e-02, 'cconv_ln_b': 7.814541e-02, 'pool_w': 1.140287e-01, 'pool_scale': 1.190213e-01, 'mix_w_out': 1.208405e-01, 'norm_xattn': 1.314980e-02, 'norm_mem': 1.953121e-02, 'xattn_wq': 1.327325e-02, 'xattn_wkv': 1.334976e-02, 'xattn_wo': 1.341217e-02, 'norm_ffn2': 5.294108e-02, 'ffn2_w_in': 2.225293e-02, 'ffn2_w_out': 3.629547e-02, 'norm_final': 3.202575e+01}


def _to_microbatches(a, axis):
    t = _jnp.moveaxis(a, axis, 0)
    t = t.reshape((N_MICROBATCH, t.shape[0] // N_MICROBATCH) + t.shape[1:])
    return _jnp.moveaxis(t, 1, axis + 1)


def setup_inputs(seed: int = 0) -> dict:
    inp = _fwd_setup_inputs(seed)
    key = _jax.random.fold_in(_jax.random.key(seed), 7919)
    shape, _ = _output_shape()
    out = dict(inp)
    out["loss_target"] = _jax.random.normal(_jax.random.fold_in(key, 0), shape, _jnp.float32)
    for i, name in enumerate(TWIN_WEIGHTS):
        w = inp[name].astype(_jnp.float32)
        if MOMENT_SCALE is None:
            s = _jnp.sqrt(_jnp.mean(_jnp.square(w)) + 1e-30)
        else:
            s = MOMENT_SCALE[name]
        km, kv = _jax.random.split(_jax.random.fold_in(key, i + 1))
        out[name] = w
        out["m_" + name] = s * _jax.random.normal(km, w.shape, _jnp.float32)
        out["v_" + name] = (s * s) * _jax.random.uniform(kv, w.shape, _jnp.float32, 0.5, 1.5)
    if N_MICROBATCH > 1:
        for name, axis in PER_EXAMPLE_BATCH_AXIS.items():
            out[name] = _to_microbatches(out[name], axis)
    return {'x': out['x'], 'mem': out['mem'], 'norm_ffn1': out['norm_ffn1'], 'ffn1_w_in': out['ffn1_w_in'], 'ffn1_w_out': out['ffn1_w_out'], 'norm_mix': out['norm_mix'], 'mix_w_in': out['mix_w_in'], 'sconv_w': out['sconv_w'], 'sgu_norm_g': out['sgu_norm_g'], 'sgu_w': out['sgu_w'], 'sgu_b': out['sgu_b'], 'cconv_w': out['cconv_w'], 'cconv_ln_g': out['cconv_ln_g'], 'cconv_ln_b': out['cconv_ln_b'], 'pool_w': out['pool_w'], 'pool_scale': out['pool_scale'], 'mix_w_out': out['mix_w_out'], 'norm_xattn': out['norm_xattn'], 'norm_mem': out['norm_mem'], 'xattn_wq': out['xattn_wq'], 'xattn_wkv': out['xattn_wkv'], 'xattn_wo': out['xattn_wo'], 'norm_ffn2': out['norm_ffn2'], 'ffn2_w_in': out['ffn2_w_in'], 'ffn2_w_out': out['ffn2_w_out'], 'norm_final': out['norm_final'], 'loss_target': out['loss_target'], 'm_norm_ffn1': out['m_norm_ffn1'], 'm_ffn1_w_in': out['m_ffn1_w_in'], 'm_ffn1_w_out': out['m_ffn1_w_out'], 'm_norm_mix': out['m_norm_mix'], 'm_mix_w_in': out['m_mix_w_in'], 'm_sconv_w': out['m_sconv_w'], 'm_sgu_norm_g': out['m_sgu_norm_g'], 'm_sgu_w': out['m_sgu_w'], 'm_sgu_b': out['m_sgu_b'], 'm_cconv_w': out['m_cconv_w'], 'm_cconv_ln_g': out['m_cconv_ln_g'], 'm_cconv_ln_b': out['m_cconv_ln_b'], 'm_pool_w': out['m_pool_w'], 'm_pool_scale': out['m_pool_scale'], 'm_mix_w_out': out['m_mix_w_out'], 'm_norm_xattn': out['m_norm_xattn'], 'm_norm_mem': out['m_norm_mem'], 'm_xattn_wq': out['m_xattn_wq'], 'm_xattn_wkv': out['m_xattn_wkv'], 'm_xattn_wo': out['m_xattn_wo'], 'm_norm_ffn2': out['m_norm_ffn2'], 'm_ffn2_w_in': out['m_ffn2_w_in'], 'm_ffn2_w_out': out['m_ffn2_w_out'], 'm_norm_final': out['m_norm_final'], 'v_norm_ffn1': out['v_norm_ffn1'], 'v_ffn1_w_in': out['v_ffn1_w_in'], 'v_ffn1_w_out': out['v_ffn1_w_out'], 'v_norm_mix': out['v_norm_mix'], 'v_mix_w_in': out['v_mix_w_in'], 'v_sconv_w': out['v_sconv_w'], 'v_sgu_norm_g': out['v_sgu_norm_g'], 'v_sgu_w': out['v_sgu_w'], 'v_sgu_b': out['v_sgu_b'], 'v_cconv_w': out['v_cconv_w'], 'v_cconv_ln_g': out['v_cconv_ln_g'], 'v_cconv_ln_b': out['v_cconv_ln_b'], 'v_pool_w': out['v_pool_w'], 'v_pool_scale': out['v_pool_scale'], 'v_mix_w_out': out['v_mix_w_out'], 'v_norm_xattn': out['v_norm_xattn'], 'v_norm_mem': out['v_norm_mem'], 'v_xattn_wq': out['v_xattn_wq'], 'v_xattn_wkv': out['v_xattn_wkv'], 'v_xattn_wo': out['v_xattn_wo'], 'v_norm_ffn2': out['v_norm_ffn2'], 'v_ffn2_w_in': out['v_ffn2_w_in'], 'v_ffn2_w_out': out['v_ffn2_w_out'], 'v_norm_final': out['v_norm_final']}


def _loss(weights, diff, rest, loss_target):
    with _jax.named_scope("forward"):
        args = {**rest, TWIN_DIFF_INPUT: diff, **{k: w.astype(_WEIGHT_DTYPES[k]) for k, w in weights.items()}}
        y = _forward(args)
    with _jax.named_scope("loss_head"):
        err = _jnp.square(y.astype(_jnp.float32) - loss_target)
        return 0.5 * _jnp.sum(_jnp.mean(err, axis=-1)) if err.ndim else 0.5 * err


def _adamw(w, g, m, v):
    m = ADAM_B1 * m + (1.0 - ADAM_B1) * g
    v = ADAM_B2 * v + (1.0 - ADAM_B2) * _jnp.square(g)
    m_hat = m / (1.0 - ADAM_B1 ** ADAM_STEP)
    v_hat = v / (1.0 - ADAM_B2 ** ADAM_STEP)
    delta = -ADAM_LR * (m_hat / (_jnp.sqrt(v_hat) + ADAM_EPS) + ADAM_WD * w)
    return delta, m, v


def reference(x, mem, norm_ffn1, ffn1_w_in, ffn1_w_out, norm_mix, mix_w_in, sconv_w, sgu_norm_g, sgu_w, sgu_b, cconv_w, cconv_ln_g, cconv_ln_b, pool_w, pool_scale, mix_w_out, norm_xattn, norm_mem, xattn_wq, xattn_wkv, xattn_wo, norm_ffn2, ffn2_w_in, ffn2_w_out, norm_final, loss_target, m_norm_ffn1, m_ffn1_w_in, m_ffn1_w_out, m_norm_mix, m_mix_w_in, m_sconv_w, m_sgu_norm_g, m_sgu_w, m_sgu_b, m_cconv_w, m_cconv_ln_g, m_cconv_ln_b, m_pool_w, m_pool_scale, m_mix_w_out, m_norm_xattn, m_norm_mem, m_xattn_wq, m_xattn_wkv, m_xattn_wo, m_norm_ffn2, m_ffn2_w_in, m_ffn2_w_out, m_norm_final, v_norm_ffn1, v_ffn1_w_in, v_ffn1_w_out, v_norm_mix, v_mix_w_in, v_sconv_w, v_sgu_norm_g, v_sgu_w, v_sgu_b, v_cconv_w, v_cconv_ln_g, v_cconv_ln_b, v_pool_w, v_pool_scale, v_mix_w_out, v_norm_xattn, v_norm_mem, v_xattn_wq, v_xattn_wkv, v_xattn_wo, v_norm_ffn2, v_ffn2_w_in, v_ffn2_w_out, v_norm_final):
    given = dict(x=x, mem=mem, norm_ffn1=norm_ffn1, ffn1_w_in=ffn1_w_in, ffn1_w_out=ffn1_w_out, norm_mix=norm_mix, mix_w_in=mix_w_in, sconv_w=sconv_w, sgu_norm_g=sgu_norm_g, sgu_w=sgu_w, sgu_b=sgu_b, cconv_w=cconv_w, cconv_ln_g=cconv_ln_g, cconv_ln_b=cconv_ln_b, pool_w=pool_w, pool_scale=pool_scale, mix_w_out=mix_w_out, norm_xattn=norm_xattn, norm_mem=norm_mem, xattn_wq=xattn_wq, xattn_wkv=xattn_wkv, xattn_wo=xattn_wo, norm_ffn2=norm_ffn2, ffn2_w_in=ffn2_w_in, ffn2_w_out=ffn2_w_out, norm_final=norm_final, loss_target=loss_target, m_norm_ffn1=m_norm_ffn1, m_ffn1_w_in=m_ffn1_w_in, m_ffn1_w_out=m_ffn1_w_out, m_norm_mix=m_norm_mix, m_mix_w_in=m_mix_w_in, m_sconv_w=m_sconv_w, m_sgu_norm_g=m_sgu_norm_g, m_sgu_w=m_sgu_w, m_sgu_b=m_sgu_b, m_cconv_w=m_cconv_w, m_cconv_ln_g=m_cconv_ln_g, m_cconv_ln_b=m_cconv_ln_b, m_pool_w=m_pool_w, m_pool_scale=m_pool_scale, m_mix_w_out=m_mix_w_out, m_norm_xattn=m_norm_xattn, m_norm_mem=m_norm_mem, m_xattn_wq=m_xattn_wq, m_xattn_wkv=m_xattn_wkv, m_xattn_wo=m_xattn_wo, m_norm_ffn2=m_norm_ffn2, m_ffn2_w_in=m_ffn2_w_in, m_ffn2_w_out=m_ffn2_w_out, m_norm_final=m_norm_final, v_norm_ffn1=v_norm_ffn1, v_ffn1_w_in=v_ffn1_w_in, v_ffn1_w_out=v_ffn1_w_out, v_norm_mix=v_norm_mix, v_mix_w_in=v_mix_w_in, v_sconv_w=v_sconv_w, v_sgu_norm_g=v_sgu_norm_g, v_sgu_w=v_sgu_w, v_sgu_b=v_sgu_b, v_cconv_w=v_cconv_w, v_cconv_ln_g=v_cconv_ln_g, v_cconv_ln_b=v_cconv_ln_b, v_pool_w=v_pool_w, v_pool_scale=v_pool_scale, v_mix_w_out=v_mix_w_out, v_norm_xattn=v_norm_xattn, v_norm_mem=v_norm_mem, v_xattn_wq=v_xattn_wq, v_xattn_wkv=v_xattn_wkv, v_xattn_wo=v_xattn_wo, v_norm_ffn2=v_norm_ffn2, v_ffn2_w_in=v_ffn2_w_in, v_ffn2_w_out=v_ffn2_w_out, v_norm_final=v_norm_final)
    weights = {n: given[n] for n in TWIN_WEIGHTS}
    shared = {n: given[n] for n in SHARED_INPUTS}
    per_example = {n: given[n] for n in ['x', 'mem']}
    grad_fn = _jax.value_and_grad(_loss, argnums=(0, 1))

    def one_microbatch(ex, loss_target):
        ex = dict(ex)
        diff = ex.pop(TWIN_DIFF_INPUT)
        return grad_fn(weights, diff, {**shared, **ex}, loss_target)

    if N_MICROBATCH == 1:
        loss, (grad_w, grad_x) = one_microbatch(per_example, given["loss_target"])
    else:
        def body(carry, xs):
            loss_sum, grad_sum = carry
            l_k, (gw_k, gx_k) = one_microbatch(xs[0], xs[1])
            with _jax.named_scope("update"):
                return (loss_sum + l_k, _jax.tree.map(_jnp.add, grad_sum, gw_k)), gx_k

        init = (_jnp.zeros((), _jnp.float32), _jax.tree.map(_jnp.zeros_like, weights))
        (loss, grad_w), grad_x = _jax.lax.scan(body, init, (per_example, given["loss_target"]))
    with _jax.named_scope("update"):
        delta_w, new_m, new_v = {}, {}, {}
        for n in TWIN_WEIGHTS:
            delta_w[n], new_m[n], new_v[n] = _adamw(weights[n], grad_w[n], given["m_" + n], given["v_" + n])
    return (loss, grad_x, *[grad_w[n] for n in TWIN_WEIGHTS], *[delta_w[n] for n in TWIN_WEIGHTS],
            *[new_m[n] for n in TWIN_WEIGHTS], *[new_v[n] for n in TWIN_WEIGHTS])
```

```python
import functools
import math

import jax
import jax.numpy as jnp
from jax import lax
from jax.experimental import pallas as pl
from jax.experimental.pallas import tpu as pltpu

F32 = jnp.float32
BF16 = jnp.bfloat16
EPS = 1e-6
SEQ_CHUNK = 128
POOL_WINDOWS = (2, 4, 8, 16)
N_HEADS = 4
ADAM_LR, ADAM_B1, ADAM_B2, ADAM_EPS, ADAM_WD, ADAM_STEP = 0.001, 0.9, 0.999, 1e-08, 0.01, 10
VMEM_LIMIT = 56 * 1024 * 1024
MESH_ID = pl.DeviceIdType.MESH
N_CHIPS = 4

BIG = ("ffn1_w_in", "ffn1_w_out", "mix_w_in", "mix_w_out", "xattn_wq", "xattn_wkv", "xattn_wo",
       "ffn2_w_in", "ffn2_w_out")
COL_SHARDED = ("ffn1_w_in", "mix_w_in", "xattn_wkv", "ffn2_w_in")
SMALL_CONV = ("sconv_w", "cconv_w")
SMALL_REPL = ("norm_ffn1", "norm_mix", "sgu_norm_g", "sgu_w", "sgu_b", "cconv_ln_g", "cconv_ln_b",
              "pool_w", "pool_scale", "norm_xattn", "norm_mem", "norm_ffn2", "norm_final")
WEIGHTS = ("norm_ffn1", "ffn1_w_in", "ffn1_w_out", "norm_mix", "mix_w_in", "sconv_w", "sgu_norm_g",
           "sgu_w", "sgu_b", "cconv_w", "cconv_ln_g", "cconv_ln_b", "pool_w", "pool_scale",
           "mix_w_out", "norm_xattn", "norm_mem", "xattn_wq", "xattn_wkv", "xattn_wo", "norm_ffn2",
           "ffn2_w_in", "ffn2_w_out", "norm_final")


def _pick(n, pref, align):
    best = None
    for d in range(align, min(n, pref) + 1, align):
        if n % d == 0:
            best = d
    return best or n


def _sig(x):
    return 1.0 / (1.0 + jnp.exp(-x))


def _nt(a, b):
    return lax.dot_general(a, b, (((1,), (1,)), ((), ())), preferred_element_type=F32)


def _tn(a, b):
    return lax.dot_general(a, b, (((0,), (0,)), ((), ())), preferred_element_type=F32)


def _params(*sem):
    return pltpu.CompilerParams(dimension_semantics=sem, vmem_limit_bytes=VMEM_LIMIT)


def norm_matmul(x, g, w, name):
    T, D = x.shape
    N = w.shape[1]
    tm, tn = _pick(T, 512, 8), _pick(N, 512, 128)

    def body(x_ref, g_ref, w_ref, o_ref, h_ref):
        j = pl.program_id(1)

        @pl.when(j == 0)
        def _():
            xv = x_ref[...]
            r = lax.rsqrt(jnp.mean(xv * xv, axis=-1, keepdims=True) + EPS)
            h_ref[...] = (xv * r * g_ref[...]).astype(BF16)

        o_ref[...] = jnp.dot(h_ref[...], w_ref[...], preferred_element_type=F32)

    return pl.pallas_call(
        body, name=name, grid=(T // tm, N // tn),
        in_specs=[pl.BlockSpec((tm, D), lambda i, j: (i, 0)), pl.BlockSpec((1, D), lambda i, j: (0, 0)),
                  pl.BlockSpec((D, tn), lambda i, j: (0, j))],
        out_specs=[pl.BlockSpec((tm, tn), lambda i, j: (i, j)), pl.BlockSpec((tm, D), lambda i, j: (i, 0))],
        out_shape=[jax.ShapeDtypeStruct((T, N), F32), jax.ShapeDtypeStruct((T, D), BF16)],
        compiler_params=_params("parallel", "arbitrary"))(x, g, w)


def matmul_res(res, a, w, name):
    T, K = a.shape
    N = w.shape[1]
    tm, tn = _pick(T, 512, 8), _pick(N, 512, 128)

    def body(r_ref, a_ref, w_ref, o_ref):
        o_ref[...] = r_ref[...] + jnp.dot(a_ref[...].astype(BF16), w_ref[...], preferred_element_type=F32)

    return pl.pallas_call(
        body, name=name, grid=(T // tm, N // tn),
        in_specs=[pl.BlockSpec((tm, tn), lambda i, j: (i, j)), pl.BlockSpec((tm, K), lambda i, j: (i, 0)),
                  pl.BlockSpec((K, tn), lambda i, j: (0, j))],
        out_specs=pl.BlockSpec((tm, tn), lambda i, j: (i, j)),
        out_shape=jax.ShapeDtypeStruct((T, N), F32),
        compiler_params=_params("parallel", "parallel"))(res, a, w)


def matmul_nt(a, w, name):
    T, N = a.shape
    M = w.shape[0]
    tm, tmm = _pick(T, 512, 8), _pick(M, 512, 128)

    def body(a_ref, w_ref, o_ref):
        o_ref[...] = _nt(a_ref[...].astype(BF16), w_ref[...])

    return pl.pallas_call(
        body, name=name, grid=(T // tm, M // tmm),
        in_specs=[pl.BlockSpec((tm, N), lambda i, j: (i, 0)), pl.BlockSpec((tmm, N), lambda i, j: (j, 0))],
        out_specs=pl.BlockSpec((tm, tmm), lambda i, j: (i, j)),
        out_shape=jax.ShapeDtypeStruct((T, M), F32),
        compiler_params=_params("parallel", "parallel"))(a, w)


def matmul_tn(a, b, scale, name):
    T, M = a.shape
    N = b.shape[1]
    bm, bn, bk = _pick(M, 512, 128), _pick(N, 1408, 128), _pick(T, 512, 8)
    nk = T // bk

    def body(a_ref, b_ref, o_ref):
        k = pl.program_id(2)

        @pl.when(k == 0)
        def _():
            o_ref[...] = jnp.zeros_like(o_ref)

        o_ref[...] += _tn(a_ref[...].astype(BF16), b_ref[...].astype(BF16))

        if scale != 1.0:
            @pl.when(k == nk - 1)
            def _():
                o_ref[...] = o_ref[...] * scale

    return pl.pallas_call(
        body, name=name, grid=(M // bm, N // bn, nk),
        in_specs=[pl.BlockSpec((bk, bm), lambda i, j, k: (k, i)), pl.BlockSpec((bk, bn), lambda i, j, k: (k, j))],
        out_specs=pl.BlockSpec((bm, bn), lambda i, j, k: (i, j)),
        out_shape=jax.ShapeDtypeStruct((M, N), F32),
        compiler_params=_params("parallel", "parallel", "arbitrary"))(a, b)


def rmsnorm_bwd(dxo, dh, x, g, name):
    T, D = x.shape
    tm = _pick(T, 512, 8)
    has_res = dxo is not None

    def body(*refs):
        if has_res:
            dxo_ref, dh_ref, x_ref, g_ref, dx_ref, dg_ref = refs
        else:
            dh_ref, x_ref, g_ref, dx_ref, dg_ref = refs
        i = pl.program_id(0)

        @pl.when(i == 0)
        def _():
            dg_ref[...] = jnp.zeros_like(dg_ref)

        xv, dh_v = x_ref[...], dh_ref[...]
        r = lax.rsqrt(jnp.mean(xv * xv, axis=-1, keepdims=True) + EPS)
        xh = xv * r
        dg_ref[...] += jnp.sum(dh_v * xh, axis=0, keepdims=True)
        dxh = dh_v * g_ref[...]
        dx = r * (dxh - xh * jnp.mean(dxh * xh, axis=-1, keepdims=True))
        dx_ref[...] = dx + dxo_ref[...] if has_res else dx

    tile = pl.BlockSpec((tm, D), lambda i: (i, 0))
    vec = pl.BlockSpec((1, D), lambda i: (0, 0))
    args = ([dxo] if has_res else []) + [dh, x, g]
    return pl.pallas_call(
        body, name=name, grid=(T // tm,),
        in_specs=[tile] * (len(args) - 1) + [vec],
        out_specs=[tile, vec],
        out_shape=[jax.ShapeDtypeStruct((T, D), F32), jax.ShapeDtypeStruct((1, D), F32)],
        compiler_params=_params("arbitrary"))(*args)


def ffn_fwd(x, g, w_in, w_out, name):
    T, D = x.shape
    F = w_out.shape[0]
    tm, tf = _pick(T, 512, 8), _pick(F, 256, 128)
    nf = F // tf

    def body(x_ref, g_ref, wg_ref, wu_ref, wo_ref, o_ref, h_ref, acc_ref):
        j = pl.program_id(1)

        @pl.when(j == 0)
        def _():
            xv = x_ref[...]
            r = lax.rsqrt(jnp.mean(xv * xv, axis=-1, keepdims=True) + EPS)
            h_ref[...] = (xv * r * g_ref[...]).astype(BF16)
            acc_ref[...] = jnp.zeros_like(acc_ref)

        h = h_ref[...]
        zg = jnp.dot(h, wg_ref[...], preferred_element_type=F32)
        zu = jnp.dot(h, wu_ref[...], preferred_element_type=F32)
        a = (zg * _sig(zg) * zu).astype(BF16)
        acc_ref[...] += jnp.dot(a, wo_ref[...], preferred_element_type=F32)

        @pl.when(j == nf - 1)
        def _():
            o_ref[...] = x_ref[...] + 0.5 * acc_ref[...]

    return pl.pallas_call(
        body, name=name, grid=(T // tm, nf),
        in_specs=[pl.BlockSpec((tm, D), lambda i, j: (i, 0)), pl.BlockSpec((1, D), lambda i, j: (0, 0)),
                  pl.BlockSpec((D, tf), lambda i, j: (0, j)), pl.BlockSpec((D, tf), lambda i, j: (0, j + nf)),
                  pl.BlockSpec((tf, D), lambda i, j: (j, 0))],
        out_specs=pl.BlockSpec((tm, D), lambda i, j: (i, 0)),
        out_shape=jax.ShapeDtypeStruct((T, D), F32),
        scratch_shapes=[pltpu.VMEM((tm, D), BF16), pltpu.VMEM((tm, D), F32)],
        compiler_params=_params("parallel", "arbitrary"))(x, g, w_in, w_in, w_out)


def ffn_dgrad(x, dxo, g, w_in, w_out, name):
    T, D = x.shape
    F = w_out.shape[0]
    tm, tf = _pick(T, 512, 8), _pick(F, 256, 128)
    nf = F // tf

    def body(x_ref, dxo_ref, g_ref, wg_ref, wu_ref, wo_ref,
             dx_ref, dg_ref, h_ref, a_ref, dzg_ref, dzu_ref, do_ref, acc_ref):
        i, j = pl.program_id(0), pl.program_id(1)

        @pl.when(jnp.logical_and(i == 0, j == 0))
        def _():
            dg_ref[...] = jnp.zeros_like(dg_ref)

        @pl.when(j == 0)
        def _():
            xv = x_ref[...]
            r = lax.rsqrt(jnp.mean(xv * xv, axis=-1, keepdims=True) + EPS)
            h_ref[...] = (xv * r * g_ref[...]).astype(BF16)
            do_ref[...] = (0.5 * dxo_ref[...]).astype(BF16)
            acc_ref[...] = jnp.zeros_like(acc_ref)

        h = h_ref[...]
        zg = jnp.dot(h, wg_ref[...], preferred_element_type=F32)
        zu = jnp.dot(h, wu_ref[...], preferred_element_type=F32)
        s = _sig(zg)
        silu = zg * s
        a_ref[...] = (silu * zu).astype(BF16)
        da = _nt(do_ref[...], wo_ref[...])
        dzu = (da * silu).astype(BF16)
        dzg = (da * zu * (s * (1.0 + zg * (1.0 - s)))).astype(BF16)
        dzg_ref[...] = dzg
        dzu_ref[...] = dzu
        acc_ref[...] += _nt(dzg, wg_ref[...]) + _nt(dzu, wu_ref[...])

        @pl.when(j == nf - 1)
        def _():
            xv, dh = x_ref[...], acc_ref[...]
            r = lax.rsqrt(jnp.mean(xv * xv, axis=-1, keepdims=True) + EPS)
            xh = xv * r
            dg_ref[...] += jnp.sum(dh * xh, axis=0, keepdims=True)
            dxh = dh * g_ref[...]
            dx_ref[...] = dxo_ref[...] + r * (dxh - xh * jnp.mean(dxh * xh, axis=-1, keepdims=True))

    tile = pl.BlockSpec((tm, D), lambda i, j: (i, 0))
    vec = pl.BlockSpec((1, D), lambda i, j: (0, 0))
    fblk = pl.BlockSpec((tm, tf), lambda i, j: (i, j))
    return pl.pallas_call(
        body, name=name, grid=(T // tm, nf),
        in_specs=[tile, tile, vec, pl.BlockSpec((D, tf), lambda i, j: (0, j)),
                  pl.BlockSpec((D, tf), lambda i, j: (0, j + nf)), pl.BlockSpec((tf, D), lambda i, j: (j, 0))],
        out_specs=[tile, vec, tile, fblk, fblk, fblk],
        out_shape=[jax.ShapeDtypeStruct((T, D), F32), jax.ShapeDtypeStruct((1, D), F32),
                   jax.ShapeDtypeStruct((T, D), BF16), jax.ShapeDtypeStruct((T, F), BF16),
                   jax.ShapeDtypeStruct((T, F), BF16), jax.ShapeDtypeStruct((T, F), BF16)],
        scratch_shapes=[pltpu.VMEM((tm, D), BF16), pltpu.VMEM((tm, D), F32)],
        compiler_params=_params("arbitrary", "arbitrary"))(x, dxo, g, w_in, w_in, w_out)


def _chunks(T, fn):
    def step(c, carry):
        fn(pl.multiple_of(c * SEQ_CHUNK, SEQ_CHUNK))
        return carry
    lax.fori_loop(0, T // SEQ_CHUNK, step, 0)


def _conv_taps(win, ktaps, pad):
    return [(win if k == ktaps - 1 else pltpu.roll(win, ktaps - 1 - k, 0))[pad:, :] for k in range(ktaps)]


def _conv_taps_t(win, ktaps, pad):
    n = win.shape[0]
    return [(win if k == ktaps - 1 else pltpu.roll(win, n - (ktaps - 1 - k), 0))[:n - pad, :] for k in range(ktaps)]


def _col(T, W, idx):
    return pl.BlockSpec((T, W), lambda i, idx=idx: (0, idx))


def _full(shape):
    return pl.BlockSpec(shape, lambda i: (0,) * len(shape))


def mix_a_fwd(z, w, name):
    T, W = z.shape[0], w.shape[1]
    K, P = w.shape[0], 8

    def body(ab_ref, ac_ref, ax_ref, w_ref, y_ref, pp_ref):
        pp_ref[0:P, :] = jnp.zeros((P, W), F32)

        def chunk(s):
            rows = pl.ds(s, SEQ_CHUNK)
            pp_ref[pl.ds(s + P, SEQ_CHUNK), :] = ac_ref[rows, :] * ax_ref[rows, :]
            taps = _conv_taps(pp_ref[pl.ds(s, SEQ_CHUNK + P), :], K, P)
            q = sum(w_ref[k:k + 1, :] * taps[k] for k in range(K))
            y_ref[rows, :] = (ab_ref[rows, :] * q).astype(BF16)

        _chunks(T, chunk)

    return pl.pallas_call(
        body, name=name, grid=(1,),
        in_specs=[_col(T, W, 0), _col(T, W, 1), _col(T, W, 2), _full((K, W))],
        out_specs=_full((T, W)), out_shape=jax.ShapeDtypeStruct((T, W), BF16),
        scratch_shapes=[pltpu.VMEM((T + P, W), F32)],
        compiler_params=_params("arbitrary"))(z, z, z, w)


def mix_a_bwd(z, dy, w, name):
    T, W = z.shape[0], w.shape[1]
    K, P = w.shape[0], 8

    def body(ab_ref, ac_ref, ax_ref, dy_ref, w_ref, dab_ref, dac_ref, dax_ref, dw_ref, pp_ref, dq_ref):
        pp_ref[0:P, :] = jnp.zeros((P, W), F32)
        dq_ref[T:T + P, :] = jnp.zeros((P, W), F32)
        dw_ref[...] = jnp.zeros_like(dw_ref)

        def chunk1(s):
            rows = pl.ds(s, SEQ_CHUNK)
            pp_ref[pl.ds(s + P, SEQ_CHUNK), :] = ac_ref[rows, :] * ax_ref[rows, :]
            taps = _conv_taps(pp_ref[pl.ds(s, SEQ_CHUNK + P), :], K, P)
            q = sum(w_ref[k:k + 1, :] * taps[k] for k in range(K))
            dyv = dy_ref[rows, :]
            dab_ref[rows, :] = (dyv * q).astype(BF16)
            dq = dyv * ab_ref[rows, :]
            dq_ref[rows, :] = dq
            for k in range(K):
                dw_ref[k:k + 1, :] += jnp.sum(dq * taps[k], axis=0, keepdims=True)

        _chunks(T, chunk1)

        def chunk2(s):
            rows = pl.ds(s, SEQ_CHUNK)
            taps = _conv_taps_t(dq_ref[pl.ds(s, SEQ_CHUNK + P), :], K, P)
            dp = sum(w_ref[k:k + 1, :] * taps[k] for k in range(K))
            dac_ref[rows, :] = (dp * ax_ref[rows, :]).astype(BF16)
            dax_ref[rows, :] = (dp * ac_ref[rows, :]).astype(BF16)

        _chunks(T, chunk2)

    tw = jax.ShapeDtypeStruct((T, W), BF16)
    return pl.pallas_call(
        body, name=name, grid=(1,),
        in_specs=[_col(T, W, 0), _col(T, W, 1), _col(T, W, 2), _col(T, W, 0), _full((K, W))],
        out_specs=[_full((T, W))] * 3 + [_full((K, W))],
        out_shape=[tw, tw, tw, jax.ShapeDtypeStruct((K, W), F32)],
        scratch_shapes=[pltpu.VMEM((T + P, W), F32), pltpu.VMEM((T + P, W), F32)],
        compiler_params=_params("arbitrary"))(z, z, z, dy, w)


def _ln_stats(v):
    mu = jnp.mean(v, axis=-1, keepdims=True)
    xc = v - mu
    rstd = lax.rsqrt(jnp.mean(xc * xc, axis=-1, keepdims=True) + EPS)
    return xc * rstd, rstd


def _ln_bwd(dxh, xh, rstd):
    return rstd * (dxh - jnp.mean(dxh, axis=-1, keepdims=True) - xh * jnp.mean(dxh * xh, axis=-1, keepdims=True))


def _tril_bf16(w_ref, h):
    n = w_ref.shape[-1]
    keep = lax.broadcasted_iota(jnp.int32, (n, n), 0) >= lax.broadcasted_iota(jnp.int32, (n, n), 1)
    return jnp.where(keep, w_ref[h], 0.0).astype(BF16)


def mix_b_fwd(z, g, w_s, bias, name):
    T, W = z.shape[0], g.shape[1]
    H, C = w_s.shape[0], w_s.shape[1]
    hd = W // H

    def body(u_ref, v_ref, g_ref, w_ref, b_ref, y_ref):
        wts = [_tril_bf16(w_ref, h) for h in range(H)]
        head = lax.broadcasted_iota(jnp.int32, (C, W), 1) // hd

        def chunk(s):
            rows = pl.ds(s, C)
            xh, _ = _ln_stats(v_ref[rows, :])
            vn = (xh * g_ref[...]).astype(BF16)
            mixed = b_ref[...]
            for h in range(H):
                mixed = mixed + jnp.where(head == h, jnp.dot(wts[h], vn, preferred_element_type=F32), 0.0)
            y_ref[rows, :] = (u_ref[rows, :] * mixed).astype(BF16)

        _chunks(T, chunk)

    return pl.pallas_call(
        body, name=name, grid=(1,),
        in_specs=[_col(T, W, 3), _col(T, W, 4), _full((1, W)), _full((H, C, C)), _full((C, W))],
        out_specs=_full((T, W)), out_shape=jax.ShapeDtypeStruct((T, W), BF16),
        compiler_params=_params("arbitrary"))(z, z, g, w_s, bias)


def mix_b_bwd(z, dy, g, w_s, bias, name):
    T, W = z.shape[0], g.shape[1]
    H, C = w_s.shape[0], w_s.shape[1]
    hd = W // H

    def body(u_ref, v_ref, dy_ref, g_ref, w_ref, b_ref, du_ref, dv_ref, dw_ref, db_ref, dg_ref, dbf_ref):
        wts = [_tril_bf16(w_ref, h) for h in range(H)]
        head = lax.broadcasted_iota(jnp.int32, (C, W), 1) // hd
        dw_ref[...] = jnp.zeros_like(dw_ref)
        dg_ref[...] = jnp.zeros_like(dg_ref)
        dbf_ref[...] = jnp.zeros_like(dbf_ref)

        def chunk(s):
            rows = pl.ds(s, C)
            xh, rstd = _ln_stats(v_ref[rows, :])
            vn = (xh * g_ref[...]).astype(BF16)
            mixed = b_ref[...]
            for h in range(H):
                mixed = mixed + jnp.where(head == h, jnp.dot(wts[h], vn, preferred_element_type=F32), 0.0)
            dyv = dy_ref[rows, :]
            du_ref[rows, :] = (dyv * mixed).astype(BF16)
            dm = dyv * u_ref[rows, :]
            dbf_ref[...] += dm
            dvn = jnp.zeros((C, W), F32)
            for h in range(H):
                dmh = jnp.where(head == h, dm, 0.0).astype(BF16)
                dw_ref[h] += _nt(dmh, vn)
                dvn = dvn + _tn(wts[h], dmh)
            dg_ref[...] += jnp.sum(dvn * xh, axis=0, keepdims=True)
            dv_ref[rows, :] = _ln_bwd(dvn * g_ref[...], xh, rstd).astype(BF16)

        _chunks(T, chunk)

        keep = lax.broadcasted_iota(jnp.int32, (C, C), 0) >= lax.broadcasted_iota(jnp.int32, (C, C), 1)
        lane = lax.broadcasted_iota(jnp.int32, (C, 128), 1)
        db = jnp.zeros((C, 128), F32)
        dbf = dbf_ref[...]
        for h in range(H):
            dw_ref[h] = jnp.where(keep, dw_ref[h], 0.0)
            db = db + jnp.where(lane == h, jnp.sum(jnp.where(head == h, dbf, 0.0), axis=1, keepdims=True), 0.0)
        db_ref[...] = db

    tw = jax.ShapeDtypeStruct((T, W), BF16)
    return pl.pallas_call(
        body, name=name, grid=(1,),
        in_specs=[_col(T, W, 3), _col(T, W, 4), _col(T, W, 1), _full((1, W)), _full((H, C, C)), _full((C, W))],
        out_specs=[_full((T, W)), _full((T, W)), _full((H, C, C)), _full((C, 128)), _full((1, W))],
        out_shape=[tw, tw, jax.ShapeDtypeStruct((H, C, C), F32), jax.ShapeDtypeStruct((C, 128), F32),
                   jax.ShapeDtypeStruct((1, W), F32)],
        scratch_shapes=[pltpu.VMEM((C, W), F32)],
        compiler_params=_params("arbitrary"))(z, z, dy, g, w_s, bias)


def mix_c_fwd(z, w, ln_g, ln_b, name):
    T, W = z.shape[0], w.shape[1]
    K, P = w.shape[0], 32

    def body(a_ref, gt_ref, w_ref, g_ref, b_ref, y_ref, up_ref):
        up_ref[0:P, :] = jnp.zeros((P, W), F32)

        def chunk(s):
            rows = pl.ds(s, SEQ_CHUNK)
            up_ref[pl.ds(s + P, SEQ_CHUNK), :] = a_ref[rows, :] * _sig(gt_ref[rows, :])
            taps = _conv_taps(up_ref[pl.ds(s, SEQ_CHUNK + P), :], K, P)
            q = sum(w_ref[k:k + 1, :] * taps[k] for k in range(K))
            xh, _ = _ln_stats(q)
            r = xh * g_ref[...] + b_ref[...]
            y_ref[rows, :] = (r * _sig(r)).astype(BF16)

        _chunks(T, chunk)

    return pl.pallas_call(
        body, name=name, grid=(1,),
        in_specs=[_col(T, W, 5), _col(T, W, 6), _full((K, W)), _full((1, W)), _full((1, W))],
        out_specs=_full((T, W)), out_shape=jax.ShapeDtypeStruct((T, W), BF16),
        scratch_shapes=[pltpu.VMEM((T + P, W), F32)],
        compiler_params=_params("arbitrary"))(z, z, w, ln_g, ln_b)


def mix_c_bwd(z, dy, w, ln_g, ln_b, name):
    T, W = z.shape[0], w.shape[1]
    K, P = w.shape[0], 32

    def body(a_ref, gt_ref, dy_ref, w_ref, g_ref, b_ref, da_ref, dgt_ref, dw_ref, dg_ref, db_ref, up_ref, dq_ref):
        up_ref[0:P, :] = jnp.zeros((P, W), F32)
        dq_ref[T:T + P, :] = jnp.zeros((P, W), F32)
        dw_ref[...] = jnp.zeros_like(dw_ref)
        dg_ref[...] = jnp.zeros_like(dg_ref)
        db_ref[...] = jnp.zeros_like(db_ref)

        def chunk1(s):
            rows = pl.ds(s, SEQ_CHUNK)
            up_ref[pl.ds(s + P, SEQ_CHUNK), :] = a_ref[rows, :] * _sig(gt_ref[rows, :])
            taps = _conv_taps(up_ref[pl.ds(s, SEQ_CHUNK + P), :], K, P)
            q = sum(w_ref[k:k + 1, :] * taps[k] for k in range(K))
            xh, rstd = _ln_stats(q)
            r = xh * g_ref[...] + b_ref[...]
            sr = _sig(r)
            dr = dy_ref[rows, :] * (sr * (1.0 + r * (1.0 - sr)))
            db_ref[...] += jnp.sum(dr, axis=0, keepdims=True)
            dg_ref[...] += jnp.sum(dr * xh, axis=0, keepdims=True)
            dq = _ln_bwd(dr * g_ref[...], xh, rstd)
            dq_ref[rows, :] = dq
            for k in range(K):
                dw_ref[k:k + 1, :] += jnp.sum(dq * taps[k], axis=0, keepdims=True)

        _chunks(T, chunk1)

        def chunk2(s):
            rows = pl.ds(s, SEQ_CHUNK)
            taps = _conv_taps_t(dq_ref[pl.ds(s, SEQ_CHUNK + P), :], K, P)
            du = sum(w_ref[k:k + 1, :] * taps[k] for k in range(K))
            sg = _sig(gt_ref[rows, :])
            da_ref[rows, :] = (du * sg).astype(BF16)
            dgt_ref[rows, :] = (du * a_ref[rows, :] * sg * (1.0 - sg)).astype(BF16)

        _chunks(T, chunk2)

    tw = jax.ShapeDtypeStruct((T, W), BF16)
    vec = jax.ShapeDtypeStruct((1, W), F32)
    return pl.pallas_call(
        body, name=name, grid=(1,),
        in_specs=[_col(T, W, 5), _col(T, W, 6), _col(T, W, 2), _full((K, W)), _full((1, W)), _full((1, W))],
        out_specs=[_full((T, W)), _full((T, W)), _full((K, W)), _full((1, W)), _full((1, W))],
        out_shape=[tw, tw, jax.ShapeDtypeStruct((K, W), F32), vec, vec],
        scratch_shapes=[pltpu.VMEM((T + P, W), F32), pltpu.VMEM((T + P, W), F32)],
        compiler_params=_params("arbitrary"))(z, z, dy, w, ln_g, ln_b)


def _pool_select(levels, W, rows):
    group = lax.broadcasted_iota(jnp.int32, (rows, W), 1) // (W // len(POOL_WINDOWS))
    out = levels[-1]
    for gi in range(len(POOL_WINDOWS) - 2, -1, -1):
        out = jnp.where(group == gi, levels[gi], out)
    return out


def _pool_count(s, W):
    t = s + lax.broadcasted_iota(jnp.int32, (SEQ_CHUNK, W), 0)
    group = lax.broadcasted_iota(jnp.int32, (SEQ_CHUNK, W), 1) // (W // len(POOL_WINDOWS))
    win = jnp.full((SEQ_CHUNK, W), POOL_WINDOWS[-1], jnp.int32)
    for gi in range(len(POOL_WINDOWS) - 2, -1, -1):
        win = jnp.where(group == gi, POOL_WINDOWS[gi], win)
    return jnp.minimum(t + 1, win).astype(F32)


def _pooled(wp_ref, s, W, P):
    win = wp_ref[pl.ds(s, SEQ_CHUNK + P), :]
    levels, acc, shift = [], win, 1
    for _ in POOL_WINDOWS:
        acc = acc + pltpu.roll(acc, shift, 0)
        levels.append(acc[P:, :])
        shift *= 2
    return _pool_select(levels, W, SEQ_CHUNK) / _pool_count(s, W) - win[P:, :]


def mix_d_fwd(z, pbd, scale, name):
    T, W = z.shape[0], scale.shape[1]
    P = 16

    def body(x_ref, p_ref, s_ref, y_ref, wp_ref):
        wp_ref[0:P, :] = jnp.zeros((P, W), F32)

        def chunk(s):
            rows = pl.ds(s, SEQ_CHUNK)
            wp_ref[pl.ds(s + P, SEQ_CHUNK), :] = x_ref[rows, :]
            pooled = _pooled(wp_ref, s, W, P).astype(BF16)
            y_ref[rows, :] = (jnp.dot(pooled, p_ref[...], preferred_element_type=F32) * s_ref[...]).astype(BF16)

        _chunks(T, chunk)

    return pl.pallas_call(
        body, name=name, grid=(1,),
        in_specs=[_col(T, W, 7), _full((W, W)), _full((1, W))],
        out_specs=_full((T, W)), out_shape=jax.ShapeDtypeStruct((T, W), BF16),
        scratch_shapes=[pltpu.VMEM((T + P, W), F32)],
        compiler_params=_params("arbitrary"))(z, pbd, scale)


def mix_d_bwd(z, dy, pbd, scale, name):
    T, W = z.shape[0], scale.shape[1]
    P = 16

    def body(x_ref, dy_ref, p_ref, s_ref, dx_ref, dp_ref, ds_ref, wp_ref, e_ref, dpool_ref):
        wp_ref[0:P, :] = jnp.zeros((P, W), F32)
        e_ref[T:T + P, :] = jnp.zeros((P, W), F32)
        dp_ref[...] = jnp.zeros_like(dp_ref)
        ds_ref[...] = jnp.zeros_like(ds_ref)

        def chunk1(s):
            rows = pl.ds(s, SEQ_CHUNK)
            wp_ref[pl.ds(s + P, SEQ_CHUNK), :] = x_ref[rows, :]
            pooled = _pooled(wp_ref, s, W, P).astype(BF16)
            yl = jnp.dot(pooled, p_ref[...], preferred_element_type=F32)
            dyv = dy_ref[rows, :]
            ds_ref[...] += jnp.sum(dyv * yl, axis=0, keepdims=True)
            dyl = (dyv * s_ref[...]).astype(BF16)
            dp_ref[...] += _tn(pooled, dyl)
            dpool = _nt(dyl, p_ref[...])
            dpool_ref[rows, :] = dpool
            e_ref[rows, :] = dpool / _pool_count(s, W)

        _chunks(T, chunk1)

        def chunk2(s):
            rows = pl.ds(s, SEQ_CHUNK)
            win = e_ref[pl.ds(s, SEQ_CHUNK + P), :]
            n = SEQ_CHUNK + P
            levels, acc, shift = [], win, 1
            for _ in POOL_WINDOWS:
                acc = acc + pltpu.roll(acc, n - shift, 0)
                levels.append(acc[:SEQ_CHUNK, :])
                shift *= 2
            dx_ref[rows, :] = (_pool_select(levels, W, SEQ_CHUNK) - dpool_ref[rows, :]).astype(BF16)

        _chunks(T, chunk2)

    return pl.pallas_call(
        body, name=name, grid=(1,),
        in_specs=[_col(T, W, 7), _col(T, W, 3), _full((W, W)), _full((1, W))],
        out_specs=[_full((T, W)), _full((W, W)), _full((1, W))],
        out_shape=[jax.ShapeDtypeStruct((T, W), BF16), jax.ShapeDtypeStruct((W, W), F32),
                   jax.ShapeDtypeStruct((1, W), F32)],
        scratch_shapes=[pltpu.VMEM((T + P, W), F32), pltpu.VMEM((T + P, W), F32), pltpu.VMEM((T, W), F32)],
        compiler_params=_params("arbitrary"))(z, dy, pbd, scale)


def attn_fwd(q, kv, name):
    T, D = q.shape
    M = kv.shape[0]
    hd = D // N_HEADS
    tm = _pick(T, 512, 8)
    sc = 1.0 / math.sqrt(hd)

    def body(q_ref, k_ref, v_ref, o_ref):
        for h in range(N_HEADS):
            cols = slice(h * hd, (h + 1) * hd)
            s = _nt(q_ref[:, cols].astype(BF16), k_ref[:, cols].astype(BF16)) * sc
            p = jnp.exp(s - jnp.max(s, axis=-1, keepdims=True))
            p = p / jnp.sum(p, axis=-1, keepdims=True)
            o_ref[:, cols] = jnp.dot(p.astype(BF16), v_ref[:, cols].astype(BF16),
                                     preferred_element_type=F32).astype(BF16)

    return pl.pallas_call(
        body, name=name, grid=(T // tm,),
        in_specs=[pl.BlockSpec((tm, D), lambda i: (i, 0)), pl.BlockSpec((M, D), lambda i: (0, 0)),
                  pl.BlockSpec((M, D), lambda i: (0, 1))],
        out_specs=pl.BlockSpec((tm, D), lambda i: (i, 0)),
        out_shape=jax.ShapeDtypeStruct((T, D), BF16),
        compiler_params=_params("parallel"))(q, kv, kv)


def attn_bwd(q, kv, do, name):
    T, D = q.shape
    M = kv.shape[0]
    hd = D // N_HEADS
    tm = _pick(T, 512, 8)
    sc = 1.0 / math.sqrt(hd)

    def body(q_ref, k_ref, v_ref, do_ref, dq_ref, dk_ref, dv_ref):
        i = pl.program_id(0)

        @pl.when(i == 0)
        def _():
            dk_ref[...] = jnp.zeros_like(dk_ref)
            dv_ref[...] = jnp.zeros_like(dv_ref)

        for h in range(N_HEADS):
            cols = slice(h * hd, (h + 1) * hd)
            qh, kh = q_ref[:, cols].astype(BF16), k_ref[:, cols].astype(BF16)
            vh, doh = v_ref[:, cols].astype(BF16), do_ref[:, cols].astype(BF16)
            s = _nt(qh, kh) * sc
            p = jnp.exp(s - jnp.max(s, axis=-1, keepdims=True))
            p = p / jnp.sum(p, axis=-1, keepdims=True)
            dp = _nt(doh, vh)
            dv_ref[:, cols] += _tn(p.astype(BF16), doh)
            ds = (p * (dp - jnp.sum(dp * p, axis=-1, keepdims=True)) * sc).astype(BF16)
            dq_ref[:, cols] = jnp.dot(ds, kh, preferred_element_type=F32).astype(BF16)
            dk_ref[:, cols] += _tn(ds, qh)

    tile = pl.BlockSpec((tm, D), lambda i: (i, 0))
    mem = jax.ShapeDtypeStruct((M, D), F32)
    return pl.pallas_call(
        body, name=name, grid=(T // tm,),
        in_specs=[tile, pl.BlockSpec((M, D), lambda i: (0, 0)), pl.BlockSpec((M, D), lambda i: (0, 1)), tile],
        out_specs=[tile, pl.BlockSpec((M, D), lambda i: (0, 0)), pl.BlockSpec((M, D), lambda i: (0, 0))],
        out_shape=[jax.ShapeDtypeStruct((T, D), BF16), mem, mem],
        compiler_params=_params("arbitrary"))(q, kv, kv, do)


def loss_head(x, g, target, name):
    T, D = x.shape
    tm = _pick(T, 512, 8)

    def body(x_ref, g_ref, t_ref, l_ref, dx_ref, dg_ref):
        i = pl.program_id(0)

        @pl.when(i == 0)
        def _():
            l_ref[...] = jnp.zeros_like(l_ref)
            dg_ref[...] = jnp.zeros_like(dg_ref)

        xv = x_ref[...]
        r = lax.rsqrt(jnp.mean(xv * xv, axis=-1, keepdims=True) + EPS)
        xh = xv * r
        err = xh * g_ref[...] - t_ref[...]
        l_ref[...] += 0.5 * jnp.sum(jnp.mean(err * err, axis=-1, keepdims=True), axis=0, keepdims=True)
        dy = err * (1.0 / D)
        dg_ref[...] += jnp.sum(dy * xh, axis=0, keepdims=True)
        dxh = dy * g_ref[...]
        dx_ref[...] = r * (dxh - xh * jnp.mean(dxh * xh, axis=-1, keepdims=True))

    tile = pl.BlockSpec((tm, D), lambda i: (i, 0))
    vec = pl.BlockSpec((1, D), lambda i: (0, 0))
    return pl.pallas_call(
        body, name=name, grid=(T // tm,),
        in_specs=[tile, vec, tile],
        out_specs=[pl.BlockSpec((1, 128), lambda i: (0, 0)), tile, vec],
        out_shape=[jax.ShapeDtypeStruct((1, 128), F32), jax.ShapeDtypeStruct((T, D), F32),
                   jax.ShapeDtypeStruct((1, D), F32)],
        compiler_params=_params("arbitrary"))(x, g, target)


def _block_diag(p):
    G, gd, _ = p.shape
    rows = [jnp.concatenate([p[g] if g == c else jnp.zeros((gd, gd), p.dtype) for c in range(G)], axis=1)
            for g in range(G)]
    return jnp.concatenate(rows, axis=0)


def _local_step(x, mem, target, wb, ws):
    L = wb["ffn1_w_in"].shape[0]
    T, D = x.shape
    W = D // 4
    H = ws["sgu_w"].shape[1]
    row = lambda v: v.reshape(1, -1)
    saved = []
    for l in range(L):
        s = {"x0": x}
        x = ffn_fwd(x, row(ws["norm_ffn1"][l]), wb["ffn1_w_in"][l], wb["ffn1_w_out"][l], "ffn_fwd")
        s["x1"] = x
        z, s["h_mix"] = norm_matmul(x, row(ws["norm_mix"][l]), wb["mix_w_in"][l], "mix_in")
        s["z"] = z
        s["bias"] = jnp.repeat(ws["sgu_b"][l].T, W // H, axis=1)
        s["pbd"] = _block_diag(ws["pool_w"][l]).astype(BF16)
        y = jnp.concatenate([
            mix_a_fwd(z, ws["sconv_w"][l], "mix_a_fwd"),
            mix_b_fwd(z, row(ws["sgu_norm_g"][l]), ws["sgu_w"][l], s["bias"], "mix_b_fwd"),
            mix_c_fwd(z, ws["cconv_w"][l], row(ws["cconv_ln_g"][l]), row(ws["cconv_ln_b"][l]), "mix_c_fwd"),
            mix_d_fwd(z, s["pbd"], row(ws["pool_scale"][l]), "mix_d_fwd")], axis=1)
        s["y"] = y
        x = matmul_res(x, y, wb["mix_w_out"][l], "mix_out")
        s["x2"] = x
        s["q"], s["hq"] = norm_matmul(x, row(ws["norm_xattn"][l]), wb["xattn_wq"][l], "attn_q")
        s["kv"], s["mn"] = norm_matmul(mem, row(ws["norm_mem"][l]), wb["xattn_wkv"][l], "attn_kv")
        s["o"] = attn_fwd(s["q"], s["kv"], "attn_fwd")
        x = matmul_res(x, s["o"], wb["xattn_wo"][l], "attn_out")
        s["x3"] = x
        x = ffn_fwd(x, row(ws["norm_ffn2"][l]), wb["ffn2_w_in"][l], wb["ffn2_w_out"][l], "ffn_fwd")
        saved.append(s)

    loss, dx, dg_final = loss_head(x, row(ws["norm_final"]), target, "loss_head")
    grads = {n: [None] * L for n in WEIGHTS if n != "norm_final"}
    grads["norm_final"] = dg_final.reshape(-1)

    def ffn_back(xin, dxo, gname, win, wout, l):
        dxn, dg, h, a, dzg, dzu = ffn_dgrad(xin, dxo, row(ws[gname][l]), wb[win][l], wb[wout][l], "ffn_dgrad")
        grads[gname][l] = dg.reshape(-1)
        grads[win][l] = jnp.concatenate([matmul_tn(h, dzg, 1.0, "ffn_dwin"), matmul_tn(h, dzu, 1.0, "ffn_dwin")], axis=1)
        grads[wout][l] = matmul_tn(a, dxo, 0.5, "ffn_dwout")
        return dxn

    for l in reversed(range(L)):
        s = saved[l]
        dx = ffn_back(s["x3"], dx, "norm_ffn2", "ffn2_w_in", "ffn2_w_out", l)
        grads["xattn_wo"][l] = matmul_tn(s["o"], dx, 1.0, "dw_sq")
        do = matmul_nt(dx, wb["xattn_wo"][l], "attn_do")
        dq, dk, dv = attn_bwd(s["q"], s["kv"], do, "attn_bwd")
        grads["xattn_wq"][l] = matmul_tn(s["hq"], dq, 1.0, "dw_sq")
        dhq = matmul_nt(dq, wb["xattn_wq"][l], "attn_dhq")
        dx, dg = rmsnorm_bwd(dx, dhq, s["x2"], row(ws["norm_xattn"][l]), "norm_bwd")
        grads["norm_xattn"][l] = dg.reshape(-1)
        dkv = jnp.concatenate([dk, dv], axis=1)
        grads["xattn_wkv"][l] = matmul_tn(s["mn"], dkv, 1.0, "attn_dwkv")
        dmn = matmul_nt(dkv, wb["xattn_wkv"][l], "attn_dmn")
        _, dg = rmsnorm_bwd(None, dmn, mem, row(ws["norm_mem"][l]), "norm_mem_bwd")
        grads["norm_mem"][l] = dg.reshape(-1)
        grads["mix_w_out"][l] = matmul_tn(s["y"], dx, 1.0, "dw_sq")
        dy = matmul_nt(dx, wb["mix_w_out"][l], "mix_dy")
        z = s["z"]
        dab, dac, dax, dws = mix_a_bwd(z, dy, ws["sconv_w"][l], "mix_a_bwd")
        dbu, dbv, dwsgu, dbs, dgs = mix_b_bwd(z, dy, row(ws["sgu_norm_g"][l]), ws["sgu_w"][l], s["bias"], "mix_b_bwd")
        dca, dcg, dwc, dgc, dbc = mix_c_bwd(z, dy, ws["cconv_w"][l], row(ws["cconv_ln_g"][l]),
                                            row(ws["cconv_ln_b"][l]), "mix_c_bwd")
        ddw, dpbd, dsc = mix_d_bwd(z, dy, s["pbd"], row(ws["pool_scale"][l]), "mix_d_bwd")
        grads["sconv_w"][l], grads["cconv_w"][l] = dws, dwc
        grads["sgu_w"][l], grads["sgu_b"][l], grads["sgu_norm_g"][l] = dwsgu, dbs[:, :H].T, dgs.reshape(-1)
        grads["cconv_ln_g"][l], grads["cconv_ln_b"][l] = dgc.reshape(-1), dbc.reshape(-1)
        gd = W // len(POOL_WINDOWS)
        grads["pool_w"][l] = jnp.stack([dpbd[g * gd:(g + 1) * gd, g * gd:(g + 1) * gd] for g in range(len(POOL_WINDOWS))])
        grads["pool_scale"][l] = dsc.reshape(-1)
        dz = jnp.concatenate([dab, dac, dax, dbu, dbv, dca, dcg, ddw], axis=1)
        grads["mix_w_in"][l] = matmul_tn(s["h_mix"], dz, 1.0, "mix_dwin")
        dh = matmul_nt(dz, wb["mix_w_in"][l], "mix_dh")
        dx, dg = rmsnorm_bwd(dx, dh, s["x1"], row(ws["norm_mix"][l]), "norm_bwd")
        grads["norm_mix"][l] = dg.reshape(-1)
        dx = ffn_back(s["x0"], dx, "norm_ffn1", "ffn1_w_in", "ffn1_w_out", l)

    return loss[0, 0], dx, grads


ANY = pl.BlockSpec(memory_space=pl.ANY)


def _other_chips(x, y):
    return [(1 - x, y), (x, 1 - y), (1 - x, 1 - y)]


def _shard_slice(ref, axis, chip, size):
    idx = [slice(None)] * len(ref.shape)
    idx[axis] = pl.ds(pl.multiple_of(chip * size, size), size)
    return ref.at[tuple(idx)]


def all_gather_chips(shards, axes, name):
    n = len(shards)

    def body(*refs):
        ins, outs = refs[:n], refs[n:2 * n]
        send, recv, loc = refs[2 * n:]
        x, y, c = lax.axis_index("x"), lax.axis_index("y"), lax.axis_index("c")
        me = 2 * x + y
        chips = _other_chips(x, y)
        started = []
        for i in range(n):
            size = ins[i].shape[axes[i]]
            cp = pltpu.make_async_copy(ins[i], _shard_slice(outs[i], axes[i], me, size), loc.at[i])
            cp.start()
            started.append(cp)
        sends = []
        for i in range(n):
            size = ins[i].shape[axes[i]]
            for j, (px, py) in enumerate(chips):
                cp = pltpu.make_async_remote_copy(
                    src_ref=ins[i], dst_ref=_shard_slice(outs[i], axes[i], me, size),
                    send_sem=send.at[i, j], recv_sem=recv.at[i, j], device_id=(px, py, c), device_id_type=MESH_ID)
                cp.start()
                sends.append(cp)
        for i in range(n):
            size = ins[i].shape[axes[i]]
            for j, (px, py) in enumerate(chips):
                pltpu.make_async_remote_copy(
                    src_ref=ins[i], dst_ref=_shard_slice(outs[i], axes[i], 2 * px + py, size),
                    send_sem=send.at[i, j], recv_sem=recv.at[i, j], device_id=(px, py, c),
                    device_id_type=MESH_ID).wait_recv()
        for cp in sends:
            cp.wait_send()
        for cp in started:
            cp.wait()

    def full(a, ax):
        shape = list(a.shape)
        shape[ax] *= N_CHIPS
        return jax.ShapeDtypeStruct(tuple(shape), a.dtype)

    return pl.pallas_call(
        body, name=name, in_specs=[ANY] * n, out_specs=[ANY] * n,
        out_shape=[full(a, ax) for a, ax in zip(shards, axes)],
        scratch_shapes=[pltpu.SemaphoreType.DMA((n, 3)), pltpu.SemaphoreType.DMA((n, 3)),
                        pltpu.SemaphoreType.DMA((n,))],
        compiler_params=pltpu.CompilerParams(has_side_effects=True))(*shards)


def all_reduce_small(p, name):
    R = p.shape[0]

    def body(p_ref, o_ref, sib_ref, chip_ref, send, recv):
        x, y, c = lax.axis_index("x"), lax.axis_index("y"), lax.axis_index("c")
        me = 2 * x + y
        chips = _other_chips(x, y)
        pair = pltpu.make_async_remote_copy(src_ref=p_ref, dst_ref=sib_ref, send_sem=send.at[0], recv_sem=recv.at[0],
                                            device_id=(x, y, 1 - c), device_id_type=MESH_ID)
        pair.start()
        pair.wait()
        chip_ref[me] = p_ref[...] + sib_ref[...]
        sends = []
        for j, (px, py) in enumerate(chips):
            cp = pltpu.make_async_remote_copy(src_ref=chip_ref.at[me], dst_ref=chip_ref.at[me], send_sem=send.at[1 + j],
                                              recv_sem=recv.at[1 + j], device_id=(px, py, c), device_id_type=MESH_ID)
            cp.start()
            sends.append(cp)
        for j, (px, py) in enumerate(chips):
            pltpu.make_async_remote_copy(src_ref=chip_ref.at[me], dst_ref=chip_ref.at[2 * px + py], send_sem=send.at[1 + j],
                                         recv_sem=recv.at[1 + j], device_id=(px, py, c), device_id_type=MESH_ID).wait_recv()
        for cp in sends:
            cp.wait_send()
        o_ref[...] = ((chip_ref[0] + chip_ref[1]) + chip_ref[2]) + chip_ref[3]

    vm = pl.BlockSpec(memory_space=pltpu.VMEM)
    return pl.pallas_call(
        body, name=name, in_specs=[vm], out_specs=vm, out_shape=jax.ShapeDtypeStruct((R, 128), F32),
        scratch_shapes=[pltpu.VMEM((R, 128), F32), pltpu.VMEM((N_CHIPS, R, 128), F32),
                        pltpu.SemaphoreType.DMA((4,)), pltpu.SemaphoreType.DMA((4,))],
        compiler_params=pltpu.CompilerParams(has_side_effects=True, vmem_limit_bytes=VMEM_LIMIT))(p)


def swap_sibling(stacks, name):
    n = len(stacks)

    def body(*refs):
        ins, outs = refs[:n], refs[n:2 * n]
        send, recv = refs[2 * n:]
        x, y, c = lax.axis_index("x"), lax.axis_index("y"), lax.axis_index("c")
        cps = [pltpu.make_async_remote_copy(src_ref=ins[i].at[1 - c], dst_ref=outs[i], send_sem=send.at[i],
                                            recv_sem=recv.at[i], device_id=(x, y, 1 - c), device_id_type=MESH_ID)
               for i in range(n)]
        for cp in cps:
            cp.start()
        for cp in cps:
            cp.wait_recv()
        for cp in cps:
            cp.wait_send()

    return pl.pallas_call(
        body, name=name, in_specs=[ANY] * n, out_specs=[ANY] * n,
        out_shape=[jax.ShapeDtypeStruct(a.shape[1:], a.dtype) for a in stacks],
        scratch_shapes=[pltpu.SemaphoreType.DMA((n,)), pltpu.SemaphoreType.DMA((n,))],
        compiler_params=pltpu.CompilerParams(has_side_effects=True))(*stacks)


def scatter_chips(sums, axes, name):
    n = len(sums)

    def body(*refs):
        ins, outs = refs[:n], refs[n:2 * n]
        send, recv, loc = refs[2 * n:]
        x, y, c = lax.axis_index("x"), lax.axis_index("y"), lax.axis_index("c")
        me = 2 * x + y
        chips = _other_chips(x, y)
        started, sends = [], []
        for i in range(n):
            size = outs[i].shape[1 + axes[i]]
            cp = pltpu.make_async_copy(_shard_slice(ins[i], axes[i], me, size), outs[i].at[me], loc.at[i])
            cp.start()
            started.append(cp)
            for j, (px, py) in enumerate(chips):
                cp = pltpu.make_async_remote_copy(
                    src_ref=_shard_slice(ins[i], axes[i], 2 * px + py, size), dst_ref=outs[i].at[me],
                    send_sem=send.at[i, j], recv_sem=recv.at[i, j], device_id=(px, py, c), device_id_type=MESH_ID)
                cp.start()
                sends.append(cp)
        for i in range(n):
            size = outs[i].shape[1 + axes[i]]
            for j, (px, py) in enumerate(chips):
                pltpu.make_async_remote_copy(
                    src_ref=_shard_slice(ins[i], axes[i], me, size), dst_ref=outs[i].at[2 * px + py],
                    send_sem=send.at[i, j], recv_sem=recv.at[i, j], device_id=(px, py, c),
                    device_id_type=MESH_ID).wait_recv()
        for cp in sends:
            cp.wait_send()
        for cp in started:
            cp.wait()

    def slots(a, ax):
        shape = list(a.shape)
        shape[ax] //= N_CHIPS
        return jax.ShapeDtypeStruct((N_CHIPS,) + tuple(shape), a.dtype)

    return pl.pallas_call(
        body, name=name, in_specs=[ANY] * n, out_specs=[ANY] * n,
        out_shape=[slots(a, ax) for a, ax in zip(sums, axes)],
        scratch_shapes=[pltpu.SemaphoreType.DMA((n, 3)), pltpu.SemaphoreType.DMA((n, 3)),
                        pltpu.SemaphoreType.DMA((n,))],
        compiler_params=pltpu.CompilerParams(has_side_effects=True))(*sums)


def share_sibling(fins, name):
    n = len(fins)

    def body(*refs):
        ins, outs = refs[:n], refs[n:2 * n]
        send, recv, loc = refs[2 * n:]
        x, y, c = lax.axis_index("x"), lax.axis_index("y"), lax.axis_index("c")
        mine = [pltpu.make_async_copy(ins[i], outs[i].at[c], loc.at[i]) for i in range(n)]
        cps = [pltpu.make_async_remote_copy(src_ref=ins[i], dst_ref=outs[i].at[c], send_sem=send.at[i], recv_sem=recv.at[i],
                                            device_id=(x, y, 1 - c), device_id_type=MESH_ID) for i in range(n)]
        for cp in mine + cps:
            cp.start()
        for i in range(n):
            pltpu.make_async_remote_copy(src_ref=ins[i], dst_ref=outs[i].at[1 - c], send_sem=send.at[i], recv_sem=recv.at[i],
                                         device_id=(x, y, 1 - c), device_id_type=MESH_ID).wait_recv()
        for cp in cps:
            cp.wait_send()
        for cp in mine:
            cp.wait()

    return pl.pallas_call(
        body, name=name, in_specs=[ANY] * n, out_specs=[ANY] * n,
        out_shape=[jax.ShapeDtypeStruct((2,) + a.shape, a.dtype) for a in fins],
        scratch_shapes=[pltpu.SemaphoreType.DMA((n,)), pltpu.SemaphoreType.DMA((n,)), pltpu.SemaphoreType.DMA((n,))],
        compiler_params=pltpu.CompilerParams(has_side_effects=True))(*fins)


def add_own_layer(stack, other, core, name):
    _, K, N = stack.shape
    bm, bn = _pick(K, 256, 8), _pick(N, 1408, 128)

    def body(c_ref, s_ref, o_ref, out_ref):
        out_ref[...] = s_ref[...] + o_ref[...]

    spec = pltpu.PrefetchScalarGridSpec(
        num_scalar_prefetch=1, grid=(K // bm, N // bn),
        in_specs=[pl.BlockSpec((None, bm, bn), lambda i, j, c: (c[0], i, j)), pl.BlockSpec((bm, bn), lambda i, j, c: (i, j))],
        out_specs=pl.BlockSpec((bm, bn), lambda i, j, c: (i, j)))
    return pl.pallas_call(body, name=name, grid_spec=spec, out_shape=jax.ShapeDtypeStruct((K, N), F32),
                          compiler_params=_params("parallel", "parallel"))(core, stack, other)


def add_slots(q, name):
    _, K, N = q.shape
    bm, bn = _pick(K, 256, 8), _pick(N, 1408, 128)

    def body(q_ref, o_ref):
        o_ref[...] = ((q_ref[0] + q_ref[1]) + q_ref[2]) + q_ref[3]

    return pl.pallas_call(
        body, name=name, grid=(K // bm, N // bn),
        in_specs=[pl.BlockSpec((N_CHIPS, bm, bn), lambda i, j: (0, i, j))],
        out_specs=pl.BlockSpec((bm, bn), lambda i, j: (i, j)), out_shape=jax.ShapeDtypeStruct((K, N), F32),
        compiler_params=_params("parallel", "parallel"))(q)


def adamw(w, g, m, v, name):
    R, N = w.shape
    bm = _pick(R, 256, 8)
    c1 = 1.0 / (1.0 - ADAM_B1 ** ADAM_STEP)
    c2 = 1.0 / (1.0 - ADAM_B2 ** ADAM_STEP)

    def body(w_ref, g_ref, m_ref, v_ref, d_ref, nm_ref, nv_ref):
        gv = g_ref[...]
        nm = ADAM_B1 * m_ref[...] + (1.0 - ADAM_B1) * gv
        nv = ADAM_B2 * v_ref[...] + (1.0 - ADAM_B2) * (gv * gv)
        nm_ref[...] = nm
        nv_ref[...] = nv
        d_ref[...] = -ADAM_LR * ((nm * c1) / (jnp.sqrt(nv * c2) + ADAM_EPS) + ADAM_WD * w_ref[...])

    blk = pl.BlockSpec((bm, N), lambda i: (i, 0))
    out = jax.ShapeDtypeStruct((R, N), F32)
    return pl.pallas_call(body, name=name, grid=(R // bm,), in_specs=[blk] * 4, out_specs=[blk] * 3,
                          out_shape=[out, out, out], compiler_params=_params("parallel"))(w, g, m, v)


def _pack(arrays):
    flat = jnp.concatenate([a.reshape(-1) for a in arrays])
    rows = -(-flat.shape[0] // (8 * 128)) * 8
    return jnp.pad(flat, (0, rows * 128 - flat.shape[0])).reshape(rows, 128)


def _unpack(p, shapes):
    flat, out, at = p.reshape(-1), [], 0
    for s in shapes:
        n = math.prod(s)
        out.append(flat[at:at + n].reshape(s))
        at += n
    return out


def kernel(x, mem, norm_ffn1, ffn1_w_in, ffn1_w_out, norm_mix, mix_w_in, sconv_w, sgu_norm_g, sgu_w, sgu_b, cconv_w, cconv_ln_g, cconv_ln_b, pool_w, pool_scale, mix_w_out, norm_xattn, norm_mem, xattn_wq, xattn_wkv, xattn_wo, norm_ffn2, ffn2_w_in, ffn2_w_out, norm_final, loss_target, m_norm_ffn1, m_ffn1_w_in, m_ffn1_w_out, m_norm_mix, m_mix_w_in, m_sconv_w, m_sgu_norm_g, m_sgu_w, m_sgu_b, m_cconv_w, m_cconv_ln_g, m_cconv_ln_b, m_pool_w, m_pool_scale, m_mix_w_out, m_norm_xattn, m_norm_mem, m_xattn_wq, m_xattn_wkv, m_xattn_wo, m_norm_ffn2, m_ffn2_w_in, m_ffn2_w_out, m_norm_final, v_norm_ffn1, v_ffn1_w_in, v_ffn1_w_out, v_norm_mix, v_mix_w_in, v_sconv_w, v_sgu_norm_g, v_sgu_w, v_sgu_b, v_cconv_w, v_cconv_ln_g, v_cconv_ln_b, v_pool_w, v_pool_scale, v_mix_w_out, v_norm_xattn, v_norm_mem, v_xattn_wq, v_xattn_wkv, v_xattn_wo, v_norm_ffn2, v_ffn2_w_in, v_ffn2_w_out, v_norm_final):
    given = dict(locals())
    w = {n: given[n] for n in WEIGHTS}
    L = ffn1_w_in.shape[0]
    assert L == 2, "the reduce-scatter gives one layer to each core of a chip"
    chip = 2 * lax.axis_index("x") + lax.axis_index("y")
    core = lax.axis_index("c").astype(jnp.int32).reshape(1)

    axes = [2 if n in COL_SHARDED else 1 for n in BIG]
    wc = sconv_w.shape[-1]
    conv_rows = [w[n].reshape(-1, wc) for n in SMALL_CONV]
    n_conv = sum(r.shape[0] for r in conv_rows)
    conv_pack = jnp.pad(jnp.concatenate(conv_rows, axis=0), ((0, -n_conv % 8), (0, 128 - wc)))[None]
    gathered = all_gather_chips([w[n].astype(BF16) for n in BIG] + [conv_pack], axes + [0], "gather_weights")
    wb = dict(zip(BIG, gathered[:-1]))
    conv_full = jnp.moveaxis(gathered[-1][:, :n_conv, :wc], 0, 1).reshape(n_conv, N_CHIPS * wc)
    ws = {n: w[n] for n in SMALL_REPL}
    at = 0
    for n in SMALL_CONV:
        rows = w[n].shape[0] * w[n].shape[1]
        ws[n] = conv_full[at:at + rows].reshape(w[n].shape[0], w[n].shape[1], N_CHIPS * wc)
        at += rows

    loss_part, grad_x, grads = _local_step(x[0], mem[0], loss_target[0], wb, ws)
    loss = lax.psum(loss_part, ("x", "y", "c"))

    small = SMALL_REPL + SMALL_CONV
    small_g = [grads[n] if n == "norm_final" else jnp.stack(grads[n]) for n in small]
    total = _unpack(all_reduce_small(_pack(small_g), "reduce_small"), [g.shape for g in small_g])
    grad = dict(zip(small, total))
    for n in SMALL_CONV:
        grad[n] = lax.dynamic_slice_in_dim(grad[n], chip * wc, wc, axis=2)

    stacks = [jnp.stack(grads[n]) for n in BIG]
    from_sibling = swap_sibling(stacks, "reduce_pair")
    pair_sums = [add_own_layer(s, o, core, "add_pair") for s, o in zip(stacks, from_sibling)]
    slots = scatter_chips(pair_sums, [a - 1 for a in axes], "reduce_chips")
    finished = [add_slots(q, "add_chips") for q in slots]
    for n, g in zip(BIG, share_sibling(finished, "share_pair")):
        grad[n] = g

    delta, new_m, new_v = {}, {}, {}
    for n in BIG:
        shape = w[n].shape
        flat = lambda a: a.reshape(-1, shape[-1])
        d, nm, nv = adamw(flat(w[n]), flat(grad[n]), flat(given["m_" + n]), flat(given["v_" + n]), "adamw")
        delta[n], new_m[n], new_v[n] = d.reshape(shape), nm.reshape(shape), nv.reshape(shape)
    shapes = [w[n].shape for n in small]
    packed = [_pack([src[n] for n in small]) for src in
              (w, grad, {n: given["m_" + n] for n in small}, {n: given["v_" + n] for n in small})]
    for out, p in zip((delta, new_m, new_v), adamw(*packed, "adamw_small")):
        out.update(zip(small, _unpack(p, shapes)))

    return (loss, grad_x[None], *[grad[n] for n in WEIGHTS], *[delta[n] for n in WEIGHTS],
            *[new_m[n] for n in WEIGHTS], *[new_v[n] for n in WEIGHTS])
```

```python
import functools
import math

import jax
import jax.numpy as jnp
from jax import lax
from jax.experimental import pallas as pl
from jax.experimental.pallas import tpu as pltpu

F32 = jnp.float32
BF16 = jnp.bfloat16
EPS = 1e-6
SEQ_CHUNK = 128
POOL_WINDOWS = (2, 4, 8, 16)
N_HEADS = 4
ADAM_LR, ADAM_B1, ADAM_B2, ADAM_EPS, ADAM_WD, ADAM_STEP = 0.001, 0.9, 0.999, 1e-08, 0.01, 10
VMEM_LIMIT = 56 * 1024 * 1024
MESH_ID = pl.DeviceIdType.MESH
N_CHIPS = 4
GRAD_WIRE = BF16

BIG = ("ffn1_w_in", "ffn1_w_out", "mix_w_in", "mix_w_out", "xattn_wq", "xattn_wkv", "xattn_wo",
       "ffn2_w_in", "ffn2_w_out")
COL_SHARDED = ("ffn1_w_in", "mix_w_in", "xattn_wkv", "ffn2_w_in")
SMALL_CONV = ("sconv_w", "cconv_w")
SMALL_REPL = ("norm_ffn1", "norm_mix", "sgu_norm_g", "sgu_w", "sgu_b", "cconv_ln_g", "cconv_ln_b",
              "pool_w", "pool_scale", "norm_xattn", "norm_mem", "norm_ffn2", "norm_final")
WEIGHTS = ("norm_ffn1", "ffn1_w_in", "ffn1_w_out", "norm_mix", "mix_w_in", "sconv_w", "sgu_norm_g",
           "sgu_w", "sgu_b", "cconv_w", "cconv_ln_g", "cconv_ln_b", "pool_w", "pool_scale",
           "mix_w_out", "norm_xattn", "norm_mem", "xattn_wq", "xattn_wkv", "xattn_wo", "norm_ffn2",
           "ffn2_w_in", "ffn2_w_out", "norm_final")


def _pick(n, pref, align):
    best = None
    for d in range(align, min(n, pref) + 1, align):
        if n % d == 0:
            best = d
    return best or n


def _sig(x):
    return 1.0 / (1.0 + jnp.exp(-x))


def _nt(a, b):
    return lax.dot_general(a, b, (((1,), (1,)), ((), ())), preferred_element_type=F32)


def _tn(a, b):
    return lax.dot_general(a, b, (((0,), (0,)), ((), ())), preferred_element_type=F32)


def _params(*sem):
    return pltpu.CompilerParams(dimension_semantics=sem, vmem_limit_bytes=VMEM_LIMIT)


def norm_matmul(x, g, w, name):
    T, D = x.shape
    N = w.shape[1]
    tm, tn = _pick(T, 512, 8), _pick(N, 512, 128)

    def body(x_ref, g_ref, w_ref, o_ref, h_ref):
        j = pl.program_id(1)

        @pl.when(j == 0)
        def _():
            xv = x_ref[...]
            r = lax.rsqrt(jnp.mean(xv * xv, axis=-1, keepdims=True) + EPS)
            h_ref[...] = (xv * r * g_ref[...]).astype(BF16)

        o_ref[...] = jnp.dot(h_ref[...], w_ref[...], preferred_element_type=F32)

    return pl.pallas_call(
        body, name=name, grid=(T // tm, N // tn),
        in_specs=[pl.BlockSpec((tm, D), lambda i, j: (i, 0)), pl.BlockSpec((1, D), lambda i, j: (0, 0)),
                  pl.BlockSpec((D, tn), lambda i, j: (0, j))],
        out_specs=[pl.BlockSpec((tm, tn), lambda i, j: (i, j)), pl.BlockSpec((tm, D), lambda i, j: (i, 0))],
        out_shape=[jax.ShapeDtypeStruct((T, N), F32), jax.ShapeDtypeStruct((T, D), BF16)],
        compiler_params=_params("parallel", "arbitrary"))(x, g, w)


def matmul_res(res, a, w, name):
    T, K = a.shape
    N = w.shape[1]
    tm, tn = _pick(T, 512, 8), _pick(N, 512, 128)

    def body(r_ref, a_ref, w_ref, o_ref):
        o_ref[...] = r_ref[...] + jnp.dot(a_ref[...].astype(BF16), w_ref[...], preferred_element_type=F32)

    return pl.pallas_call(
        body, name=name, grid=(T // tm, N // tn),
        in_specs=[pl.BlockSpec((tm, tn), lambda i, j: (i, j)), pl.BlockSpec((tm, K), lambda i, j: (i, 0)),
                  pl.BlockSpec((K, tn), lambda i, j: (0, j))],
        out_specs=pl.BlockSpec((tm, tn), lambda i, j: (i, j)),
        out_shape=jax.ShapeDtypeStruct((T, N), F32),
        compiler_params=_params("parallel", "parallel"))(res, a, w)


def matmul_nt(a, w, name):
    T, N = a.shape
    M = w.shape[0]
    tm, tmm = _pick(T, 512, 8), _pick(M, 512, 128)

    def body(a_ref, w_ref, o_ref):
        o_ref[...] = _nt(a_ref[...].astype(BF16), w_ref[...])

    return pl.pallas_call(
        body, name=name, grid=(T // tm, M // tmm),
        in_specs=[pl.BlockSpec((tm, N), lambda i, j: (i, 0)), pl.BlockSpec((tmm, N), lambda i, j: (j, 0))],
        out_specs=pl.BlockSpec((tm, tmm), lambda i, j: (i, j)),
        out_shape=jax.ShapeDtypeStruct((T, M), F32),
        compiler_params=_params("parallel", "parallel"))(a, w)


def matmul_tn(a, b, scale, name):
    T, M = a.shape
    N = b.shape[1]
    bm, bn, bk = _pick(M, 1408, 128), _pick(N, 1408, 128), _pick(T, 512, 8)
    nk = T // bk

    def body(a_ref, b_ref, o_ref):
        k = pl.program_id(2)

        @pl.when(k == 0)
        def _():
            o_ref[...] = jnp.zeros_like(o_ref)

        o_ref[...] += _tn(a_ref[...].astype(BF16), b_ref[...].astype(BF16))

        if scale != 1.0:
            @pl.when(k == nk - 1)
            def _():
                o_ref[...] = o_ref[...] * scale

    return pl.pallas_call(
        body, name=name, grid=(M // bm, N // bn, nk),
        in_specs=[pl.BlockSpec((bk, bm), lambda i, j, k: (k, i)), pl.BlockSpec((bk, bn), lambda i, j, k: (k, j))],
        out_specs=pl.BlockSpec((bm, bn), lambda i, j, k: (i, j)),
        out_shape=jax.ShapeDtypeStruct((M, N), F32),
        compiler_params=_params("parallel", "parallel", "arbitrary"))(a, b)


def rmsnorm_bwd(dxo, dh, x, g, name):
    T, D = x.shape
    tm = _pick(T, 512, 8)
    has_res = dxo is not None

    def body(*refs):
        if has_res:
            dxo_ref, dh_ref, x_ref, g_ref, dx_ref, dg_ref = refs
        else:
            dh_ref, x_ref, g_ref, dx_ref, dg_ref = refs
        i = pl.program_id(0)

        @pl.when(i == 0)
        def _():
            dg_ref[...] = jnp.zeros_like(dg_ref)

        xv, dh_v = x_ref[...], dh_ref[...]
        r = lax.rsqrt(jnp.mean(xv * xv, axis=-1, keepdims=True) + EPS)
        xh = xv * r
        dg_ref[...] += jnp.sum(dh_v * xh, axis=0, keepdims=True)
        dxh = dh_v * g_ref[...]
        dx = r * (dxh - xh * jnp.mean(dxh * xh, axis=-1, keepdims=True))
        dx_ref[...] = dx + dxo_ref[...] if has_res else dx

    tile = pl.BlockSpec((tm, D), lambda i: (i, 0))
    vec = pl.BlockSpec((1, D), lambda i: (0, 0))
    args = ([dxo] if has_res else []) + [dh, x, g]
    return pl.pallas_call(
        body, name=name, grid=(T // tm,),
        in_specs=[tile] * (len(args) - 1) + [vec],
        out_specs=[tile, vec],
        out_shape=[jax.ShapeDtypeStruct((T, D), F32), jax.ShapeDtypeStruct((1, D), F32)],
        compiler_params=_params("arbitrary"))(*args)


def ffn_fwd(x, g, w_in, w_out, name):
    T, D = x.shape
    F = w_out.shape[0]
    tm, tf = _pick(T, 512, 8), _pick(F, 1408, 128)
    nf = F // tf

    def body(x_ref, g_ref, wg_ref, wu_ref, wo_ref, o_ref, h_ref, acc_ref):
        j = pl.program_id(1)

        @pl.when(j == 0)
        def _():
            xv = x_ref[...]
            r = lax.rsqrt(jnp.mean(xv * xv, axis=-1, keepdims=True) + EPS)
            h_ref[...] = (xv * r * g_ref[...]).astype(BF16)
            acc_ref[...] = jnp.zeros_like(acc_ref)

        h = h_ref[...]
        zg = jnp.dot(h, wg_ref[...], preferred_element_type=F32)
        zu = jnp.dot(h, wu_ref[...], preferred_element_type=F32)
        a = (zg * _sig(zg) * zu).astype(BF16)
        acc_ref[...] += jnp.dot(a, wo_ref[...], preferred_element_type=F32)

        @pl.when(j == nf - 1)
        def _():
            o_ref[...] = x_ref[...] + 0.5 * acc_ref[...]

    return pl.pallas_call(
        body, name=name, grid=(T // tm, nf),
        in_specs=[pl.BlockSpec((tm, D), lambda i, j: (i, 0)), pl.BlockSpec((1, D), lambda i, j: (0, 0)),
                  pl.BlockSpec((D, tf), lambda i, j: (0, j)), pl.BlockSpec((D, tf), lambda i, j: (0, j + nf)),
                  pl.BlockSpec((tf, D), lambda i, j: (j, 0))],
        out_specs=pl.BlockSpec((tm, D), lambda i, j: (i, 0)),
        out_shape=jax.ShapeDtypeStruct((T, D), F32),
        scratch_shapes=[pltpu.VMEM((tm, D), BF16), pltpu.VMEM((tm, D), F32)],
        compiler_params=_params("parallel", "arbitrary"))(x, g, w_in, w_in, w_out)


def ffn_dz(x, dxo, g, w_in, w_out, name):
    T, D = x.shape
    F = w_out.shape[0]
    tm, tf = _pick(T, 512, 8), _pick(F, 256, 128)
    nf = F // tf

    def body(x_ref, dxo_ref, g_ref, wg_ref, wu_ref, wo_ref, h_ref, a_ref, dzg_ref, dzu_ref, do_ref):
        j = pl.program_id(1)

        @pl.when(j == 0)
        def _():
            xv = x_ref[...]
            r = lax.rsqrt(jnp.mean(xv * xv, axis=-1, keepdims=True) + EPS)
            h_ref[...] = (xv * r * g_ref[...]).astype(BF16)
            do_ref[...] = (0.5 * dxo_ref[...]).astype(BF16)

        h = h_ref[...]
        zg = jnp.dot(h, wg_ref[...], preferred_element_type=F32)
        zu = jnp.dot(h, wu_ref[...], preferred_element_type=F32)
        s = _sig(zg)
        silu = zg * s
        a_ref[...] = (silu * zu).astype(BF16)
        da = _nt(do_ref[...], wo_ref[...])
        dzu_ref[...] = (da * silu).astype(BF16)
        dzg_ref[...] = (da * zu * (s * (1.0 + zg * (1.0 - s)))).astype(BF16)

    tile = pl.BlockSpec((tm, D), lambda i, j: (i, 0))
    fblk = pl.BlockSpec((tm, tf), lambda i, j: (i, j))
    hidden = jax.ShapeDtypeStruct((T, F), BF16)
    return pl.pallas_call(
        body, name=name, grid=(T // tm, nf),
        in_specs=[tile, tile, pl.BlockSpec((1, D), lambda i, j: (0, 0)), pl.BlockSpec((D, tf), lambda i, j: (0, j)),
                  pl.BlockSpec((D, tf), lambda i, j: (0, j + nf)), pl.BlockSpec((tf, D), lambda i, j: (j, 0))],
        out_specs=[tile, fblk, fblk, fblk],
        out_shape=[jax.ShapeDtypeStruct((T, D), BF16), hidden, hidden, hidden],
        scratch_shapes=[pltpu.VMEM((tm, D), BF16)],
        compiler_params=_params("parallel", "arbitrary"))(x, dxo, g, w_in, w_in, w_out)


def ffn_dh(x, dxo, g, dzg, dzu, w_in, name):
    T, D = x.shape
    F = dzg.shape[1]
    tm = _pick(T, 256, 8)

    def body(x_ref, dxo_ref, g_ref, dzg_ref, dzu_ref, wg_ref, wu_ref, dx_ref, dg_ref):
        i = pl.program_id(0)

        @pl.when(i == 0)
        def _():
            dg_ref[...] = jnp.zeros_like(dg_ref)

        dh = _nt(dzg_ref[...], wg_ref[...]) + _nt(dzu_ref[...], wu_ref[...])
        xv = x_ref[...]
        r = lax.rsqrt(jnp.mean(xv * xv, axis=-1, keepdims=True) + EPS)
        xh = xv * r
        dg_ref[...] += jnp.sum(dh * xh, axis=0, keepdims=True)
        dxh = dh * g_ref[...]
        dx_ref[...] = dxo_ref[...] + r * (dxh - xh * jnp.mean(dxh * xh, axis=-1, keepdims=True))

    tile = pl.BlockSpec((tm, D), lambda i: (i, 0))
    vec = pl.BlockSpec((1, D), lambda i: (0, 0))
    ftile = pl.BlockSpec((tm, F), lambda i: (i, 0))
    return pl.pallas_call(
        body, name=name, grid=(T // tm,),
        in_specs=[tile, tile, vec, ftile, ftile, pl.BlockSpec((D, F), lambda i: (0, 0)), pl.BlockSpec((D, F), lambda i: (0, 1))],
        out_specs=[tile, vec],
        out_shape=[jax.ShapeDtypeStruct((T, D), F32), jax.ShapeDtypeStruct((1, D), F32)],
        compiler_params=_params("arbitrary"))(x, dxo, g, dzg, dzu, w_in, w_in)


def _chunks(T, fn):
    def step(c, carry):
        fn(pl.multiple_of(c * SEQ_CHUNK, SEQ_CHUNK))
        return carry
    lax.fori_loop(0, T // SEQ_CHUNK, step, 0)


def _conv_taps(win, ktaps, pad):
    return [(win if k == ktaps - 1 else pltpu.roll(win, ktaps - 1 - k, 0))[pad:, :] for k in range(ktaps)]


def _conv_taps_t(win, ktaps, pad):
    n = win.shape[0]
    return [(win if k == ktaps - 1 else pltpu.roll(win, n - (ktaps - 1 - k), 0))[:n - pad, :] for k in range(ktaps)]


def _col(T, W, idx):
    return pl.BlockSpec((T, W), lambda i, idx=idx: (0, idx))


def _full(shape):
    return pl.BlockSpec(shape, lambda i: (0,) * len(shape))


def mix_a_fwd(z, w, name):
    T, W = z.shape[0], w.shape[1]
    K, P = w.shape[0], 8

    def body(ab_ref, ac_ref, ax_ref, w_ref, y_ref, pp_ref):
        pp_ref[0:P, :] = jnp.zeros((P, W), F32)

        def chunk(s):
            rows = pl.ds(s, SEQ_CHUNK)
            pp_ref[pl.ds(s + P, SEQ_CHUNK), :] = ac_ref[rows, :] * ax_ref[rows, :]
            taps = _conv_taps(pp_ref[pl.ds(s, SEQ_CHUNK + P), :], K, P)
            q = sum(w_ref[k:k + 1, :] * taps[k] for k in range(K))
            y_ref[rows, :] = (ab_ref[rows, :] * q).astype(BF16)

        _chunks(T, chunk)

    return pl.pallas_call(
        body, name=name, grid=(1,),
        in_specs=[_col(T, W, 0), _col(T, W, 1), _col(T, W, 2), _full((K, W))],
        out_specs=_full((T, W)), out_shape=jax.ShapeDtypeStruct((T, W), BF16),
        scratch_shapes=[pltpu.VMEM((T + P, W), F32)],
        compiler_params=_params("arbitrary"))(z, z, z, w)


def mix_a_bwd(z, dy, w, name):
    T, W = z.shape[0], w.shape[1]
    K, P = w.shape[0], 8

    def body(ab_ref, ac_ref, ax_ref, dy_ref, w_ref, dab_ref, dac_ref, dax_ref, dw_ref, pp_ref, dq_ref):
        pp_ref[0:P, :] = jnp.zeros((P, W), F32)
        dq_ref[T:T + P, :] = jnp.zeros((P, W), F32)
        dw_ref[...] = jnp.zeros_like(dw_ref)

        def chunk1(s):
            rows = pl.ds(s, SEQ_CHUNK)
            pp_ref[pl.ds(s + P, SEQ_CHUNK), :] = ac_ref[rows, :] * ax_ref[rows, :]
            taps = _conv_taps(pp_ref[pl.ds(s, SEQ_CHUNK + P), :], K, P)
            q = sum(w_ref[k:k + 1, :] * taps[k] for k in range(K))
            dyv = dy_ref[rows, :]
            dab_ref[rows, :] = (dyv * q).astype(BF16)
            dq = dyv * ab_ref[rows, :]
            dq_ref[rows, :] = dq
            for k in range(K):
                dw_ref[k:k + 1, :] += jnp.sum(dq * taps[k], axis=0, keepdims=True)

        _chunks(T, chunk1)

        def chunk2(s):
            rows = pl.ds(s, SEQ_CHUNK)
            taps = _conv_taps_t(dq_ref[pl.ds(s, SEQ_CHUNK + P), :], K, P)
            dp = sum(w_ref[k:k + 1, :] * taps[k] for k in range(K))
            dac_ref[rows, :] = (dp * ax_ref[rows, :]).astype(BF16)
            dax_ref[rows, :] = (dp * ac_ref[rows, :]).astype(BF16)

        _chunks(T, chunk2)

    tw = jax.ShapeDtypeStruct((T, W), BF16)
    return pl.pallas_call(
        body, name=name, grid=(1,),
        in_specs=[_col(T, W, 0), _col(T, W, 1), _col(T, W, 2), _col(T, W, 0), _full((K, W))],
        out_specs=[_full((T, W))] * 3 + [_full((K, W))],
        out_shape=[tw, tw, tw, jax.ShapeDtypeStruct((K, W), F32)],
        scratch_shapes=[pltpu.VMEM((T + P, W), F32), pltpu.VMEM((T + P, W), F32)],
        compiler_params=_params("arbitrary"))(z, z, z, dy, w)


def _ln_stats(v):
    mu = jnp.mean(v, axis=-1, keepdims=True)
    xc = v - mu
    rstd = lax.rsqrt(jnp.mean(xc * xc, axis=-1, keepdims=True) + EPS)
    return xc * rstd, rstd


def _ln_bwd(dxh, xh, rstd):
    return rstd * (dxh - jnp.mean(dxh, axis=-1, keepdims=True) - xh * jnp.mean(dxh * xh, axis=-1, keepdims=True))


def _tril_bf16(w_ref, h):
    n = w_ref.shape[-1]
    keep = lax.broadcasted_iota(jnp.int32, (n, n), 0) >= lax.broadcasted_iota(jnp.int32, (n, n), 1)
    return jnp.where(keep, w_ref[h], 0.0).astype(BF16)


def mix_b_fwd(z, g, w_s, bias, name):
    T, W = z.shape[0], g.shape[1]
    H, C = w_s.shape[0], w_s.shape[1]
    hd = W // H

    def body(u_ref, v_ref, g_ref, w_ref, b_ref, y_ref):
        wts = [_tril_bf16(w_ref, h) for h in range(H)]
        head = lax.broadcasted_iota(jnp.int32, (C, W), 1) // hd

        def chunk(s):
            rows = pl.ds(s, C)
            xh, _ = _ln_stats(v_ref[rows, :])
            vn = (xh * g_ref[...]).astype(BF16)
            mixed = b_ref[...]
            for h in range(H):
                mixed = mixed + jnp.where(head == h, jnp.dot(wts[h], vn, preferred_element_type=F32), 0.0)
            y_ref[rows, :] = (u_ref[rows, :] * mixed).astype(BF16)

        _chunks(T, chunk)

    return pl.pallas_call(
        body, name=name, grid=(1,),
        in_specs=[_col(T, W, 3), _col(T, W, 4), _full((1, W)), _full((H, C, C)), _full((C, W))],
        out_specs=_full((T, W)), out_shape=jax.ShapeDtypeStruct((T, W), BF16),
        compiler_params=_params("arbitrary"))(z, z, g, w_s, bias)


def mix_b_bwd(z, dy, g, w_s, bias, name):
    T, W = z.shape[0], g.shape[1]
    H, C = w_s.shape[0], w_s.shape[1]
    hd = W // H

    def body(u_ref, v_ref, dy_ref, g_ref, w_ref, b_ref, du_ref, dv_ref, dw_ref, db_ref, dg_ref, dbf_ref):
        wts = [_tril_bf16(w_ref, h) for h in range(H)]
        head = lax.broadcasted_iota(jnp.int32, (C, W), 1) // hd
        dw_ref[...] = jnp.zeros_like(dw_ref)
        dg_ref[...] = jnp.zeros_like(dg_ref)
        dbf_ref[...] = jnp.zeros_like(dbf_ref)

        def chunk(s):
            rows = pl.ds(s, C)
            xh, rstd = _ln_stats(v_ref[rows, :])
            vn = (xh * g_ref[...]).astype(BF16)
            mixed = b_ref[...]
            for h in range(H):
                mixed = mixed + jnp.where(head == h, jnp.dot(wts[h], vn, preferred_element_type=F32), 0.0)
            dyv = dy_ref[rows, :]
            du_ref[rows, :] = (dyv * mixed).astype(BF16)
            dm = dyv * u_ref[rows, :]
            dbf_ref[...] += dm
            dvn = jnp.zeros((C, W), F32)
            for h in range(H):
                dmh = jnp.where(head == h, dm, 0.0).astype(BF16)
                dw_ref[h] += _nt(dmh, vn)
                dvn = dvn + _tn(wts[h], dmh)
            dg_ref[...] += jnp.sum(dvn * xh, axis=0, keepdims=True)
            dv_ref[rows, :] = _ln_bwd(dvn * g_ref[...], xh, rstd).astype(BF16)

        _chunks(T, chunk)

        keep = lax.broadcasted_iota(jnp.int32, (C, C), 0) >= lax.broadcasted_iota(jnp.int32, (C, C), 1)
        lane = lax.broadcasted_iota(jnp.int32, (C, 128), 1)
        db = jnp.zeros((C, 128), F32)
        dbf = dbf_ref[...]
        for h in range(H):
            dw_ref[h] = jnp.where(keep, dw_ref[h], 0.0)
            db = db + jnp.where(lane == h, jnp.sum(jnp.where(head == h, dbf, 0.0), axis=1, keepdims=True), 0.0)
        db_ref[...] = db

    tw = jax.ShapeDtypeStruct((T, W), BF16)
    return pl.pallas_call(
        body, name=name, grid=(1,),
        in_specs=[_col(T, W, 3), _col(T, W, 4), _col(T, W, 1), _full((1, W)), _full((H, C, C)), _full((C, W))],
        out_specs=[_full((T, W)), _full((T, W)), _full((H, C, C)), _full((C, 128)), _full((1, W))],
        out_shape=[tw, tw, jax.ShapeDtypeStruct((H, C, C), F32), jax.ShapeDtypeStruct((C, 128), F32),
                   jax.ShapeDtypeStruct((1, W), F32)],
        scratch_shapes=[pltpu.VMEM((C, W), F32)],
        compiler_params=_params("arbitrary"))(z, z, dy, g, w_s, bias)


def mix_c_fwd(z, w, ln_g, ln_b, name):
    T, W = z.shape[0], w.shape[1]
    K, P = w.shape[0], 32

    def body(a_ref, gt_ref, w_ref, g_ref, b_ref, y_ref, up_ref):
        up_ref[0:P, :] = jnp.zeros((P, W), F32)

        def chunk(s):
            rows = pl.ds(s, SEQ_CHUNK)
            up_ref[pl.ds(s + P, SEQ_CHUNK), :] = a_ref[rows, :] * _sig(gt_ref[rows, :])
            taps = _conv_taps(up_ref[pl.ds(s, SEQ_CHUNK + P), :], K, P)
            q = sum(w_ref[k:k + 1, :] * taps[k] for k in range(K))
            xh, _ = _ln_stats(q)
            r = xh * g_ref[...] + b_ref[...]
            y_ref[rows, :] = (r * _sig(r)).astype(BF16)

        _chunks(T, chunk)

    return pl.pallas_call(
        body, name=name, grid=(1,),
        in_specs=[_col(T, W, 5), _col(T, W, 6), _full((K, W)), _full((1, W)), _full((1, W))],
        out_specs=_full((T, W)), out_shape=jax.ShapeDtypeStruct((T, W), BF16),
        scratch_shapes=[pltpu.VMEM((T + P, W), F32)],
        compiler_params=_params("arbitrary"))(z, z, w, ln_g, ln_b)


def mix_c_bwd(z, dy, w, ln_g, ln_b, name):
    T, W = z.shape[0], w.shape[1]
    K, P = w.shape[0], 32

    def body(a_ref, gt_ref, dy_ref, w_ref, g_ref, b_ref, da_ref, dgt_ref, dw_ref, dg_ref, db_ref, up_ref, dq_ref):
        up_ref[0:P, :] = jnp.zeros((P, W), F32)
        dq_ref[T:T + P, :] = jnp.zeros((P, W), F32)
        dw_ref[...] = jnp.zeros_like(dw_ref)
        dg_ref[...] = jnp.zeros_like(dg_ref)
        db_ref[...] = jnp.zeros_like(db_ref)

        def chunk1(s):
            rows = pl.ds(s, SEQ_CHUNK)
            up_ref[pl.ds(s + P, SEQ_CHUNK), :] = a_ref[rows, :] * _sig(gt_ref[rows, :])
            taps = _conv_taps(up_ref[pl.ds(s, SEQ_CHUNK + P), :], K, P)
            q = sum(w_ref[k:k + 1, :] * taps[k] for k in range(K))
            xh, rstd = _ln_stats(q)
            r = xh * g_ref[...] + b_ref[...]
            sr = _sig(r)
            dr = dy_ref[rows, :] * (sr * (1.0 + r * (1.0 - sr)))
            db_ref[...] += jnp.sum(dr, axis=0, keepdims=True)
            dg_ref[...] += jnp.sum(dr * xh, axis=0, keepdims=True)
            dq = _ln_bwd(dr * g_ref[...], xh, rstd)
            dq_ref[rows, :] = dq
            for k in range(K):
                dw_ref[k:k + 1, :] += jnp.sum(dq * taps[k], axis=0, keepdims=True)

        _chunks(T, chunk1)

        def chunk2(s):
            rows = pl.ds(s, SEQ_CHUNK)
            taps = _conv_taps_t(dq_ref[pl.ds(s, SEQ_CHUNK + P), :], K, P)
            du = sum(w_ref[k:k + 1, :] * taps[k] for k in range(K))
            sg = _sig(gt_ref[rows, :])
            da_ref[rows, :] = (du * sg).astype(BF16)
            dgt_ref[rows, :] = (du * a_ref[rows, :] * sg * (1.0 - sg)).astype(BF16)

        _chunks(T, chunk2)

    tw = jax.ShapeDtypeStruct((T, W), BF16)
    vec = jax.ShapeDtypeStruct((1, W), F32)
    return pl.pallas_call(
        body, name=name, grid=(1,),
        in_specs=[_col(T, W, 5), _col(T, W, 6), _col(T, W, 2), _full((K, W)), _full((1, W)), _full((1, W))],
        out_specs=[_full((T, W)), _full((T, W)), _full((K, W)), _full((1, W)), _full((1, W))],
        out_shape=[tw, tw, jax.ShapeDtypeStruct((K, W), F32), vec, vec],
        scratch_shapes=[pltpu.VMEM((T + P, W), F32), pltpu.VMEM((T + P, W), F32)],
        compiler_params=_params("arbitrary"))(z, z, dy, w, ln_g, ln_b)


def _pool_select(levels, W, rows):
    group = lax.broadcasted_iota(jnp.int32, (rows, W), 1) // (W // len(POOL_WINDOWS))
    out = levels[-1]
    for gi in range(len(POOL_WINDOWS) - 2, -1, -1):
        out = jnp.where(group == gi, levels[gi], out)
    return out


def _pool_count(s, W):
    t = s + lax.broadcasted_iota(jnp.int32, (SEQ_CHUNK, W), 0)
    group = lax.broadcasted_iota(jnp.int32, (SEQ_CHUNK, W), 1) // (W // len(POOL_WINDOWS))
    win = jnp.full((SEQ_CHUNK, W), POOL_WINDOWS[-1], jnp.int32)
    for gi in range(len(POOL_WINDOWS) - 2, -1, -1):
        win = jnp.where(group == gi, POOL_WINDOWS[gi], win)
    return jnp.minimum(t + 1, win).astype(F32)


def _pooled(wp_ref, s, W, P):
    win = wp_ref[pl.ds(s, SEQ_CHUNK + P), :]
    levels, acc, shift = [], win, 1
    for _ in POOL_WINDOWS:
        acc = acc + pltpu.roll(acc, shift, 0)
        levels.append(acc[P:, :])
        shift *= 2
    return _pool_select(levels, W, SEQ_CHUNK) / _pool_count(s, W) - win[P:, :]


def mix_d_fwd(z, pbd, scale, name):
    T, W = z.shape[0], scale.shape[1]
    P = 16

    def body(x_ref, p_ref, s_ref, y_ref, wp_ref):
        wp_ref[0:P, :] = jnp.zeros((P, W), F32)

        def chunk(s):
            rows = pl.ds(s, SEQ_CHUNK)
            wp_ref[pl.ds(s + P, SEQ_CHUNK), :] = x_ref[rows, :]
            pooled = _pooled(wp_ref, s, W, P).astype(BF16)
            y_ref[rows, :] = (jnp.dot(pooled, p_ref[...], preferred_element_type=F32) * s_ref[...]).astype(BF16)

        _chunks(T, chunk)

    return pl.pallas_call(
        body, name=name, grid=(1,),
        in_specs=[_col(T, W, 7), _full((W, W)), _full((1, W))],
        out_specs=_full((T, W)), out_shape=jax.ShapeDtypeStruct((T, W), BF16),
        scratch_shapes=[pltpu.VMEM((T + P, W), F32)],
        compiler_params=_params("arbitrary"))(z, pbd, scale)


def mix_d_bwd(z, dy, pbd, scale, name):
    T, W = z.shape[0], scale.shape[1]
    P = 16

    def body(x_ref, dy_ref, p_ref, s_ref, dx_ref, dp_ref, ds_ref, wp_ref, e_ref, dpool_ref):
        wp_ref[0:P, :] = jnp.zeros((P, W), F32)
        e_ref[T:T + P, :] = jnp.zeros((P, W), F32)
        dp_ref[...] = jnp.zeros_like(dp_ref)
        ds_ref[...] = jnp.zeros_like(ds_ref)

        def chunk1(s):
            rows = pl.ds(s, SEQ_CHUNK)
            wp_ref[pl.ds(s + P, SEQ_CHUNK), :] = x_ref[rows, :]
            pooled = _pooled(wp_ref, s, W, P).astype(BF16)
            yl = jnp.dot(pooled, p_ref[...], preferred_element_type=F32)
            dyv = dy_ref[rows, :]
            ds_ref[...] += jnp.sum(dyv * yl, axis=0, keepdims=True)
            dyl = (dyv * s_ref[...]).astype(BF16)
            dp_ref[...] += _tn(pooled, dyl)
            dpool = _nt(dyl, p_ref[...])
            dpool_ref[rows, :] = dpool
            e_ref[rows, :] = dpool / _pool_count(s, W)

        _chunks(T, chunk1)

        def chunk2(s):
            rows = pl.ds(s, SEQ_CHUNK)
            win = e_ref[pl.ds(s, SEQ_CHUNK + P), :]
            n = SEQ_CHUNK + P
            levels, acc, shift = [], win, 1
            for _ in POOL_WINDOWS:
                acc = acc + pltpu.roll(acc, n - shift, 0)
                levels.append(acc[:SEQ_CHUNK, :])
                shift *= 2
            dx_ref[rows, :] = (_pool_select(levels, W, SEQ_CHUNK) - dpool_ref[rows, :]).astype(BF16)

        _chunks(T, chunk2)

    return pl.pallas_call(
        body, name=name, grid=(1,),
        in_specs=[_col(T, W, 7), _col(T, W, 3), _full((W, W)), _full((1, W))],
        out_specs=[_full((T, W)), _full((W, W)), _full((1, W))],
        out_shape=[jax.ShapeDtypeStruct((T, W), BF16), jax.ShapeDtypeStruct((W, W), F32),
                   jax.ShapeDtypeStruct((1, W), F32)],
        scratch_shapes=[pltpu.VMEM((T + P, W), F32), pltpu.VMEM((T + P, W), F32), pltpu.VMEM((T, W), F32)],
        compiler_params=_params("arbitrary"))(z, dy, pbd, scale)


def attn_fwd(q, kv, name):
    T, D = q.shape
    M = kv.shape[0]
    hd = D // N_HEADS
    tm = _pick(T, 512, 8)
    sc = 1.0 / math.sqrt(hd)

    def body(q_ref, k_ref, v_ref, o_ref):
        for h in range(N_HEADS):
            cols = slice(h * hd, (h + 1) * hd)
            s = _nt(q_ref[:, cols].astype(BF16), k_ref[:, cols].astype(BF16)) * sc
            p = jnp.exp(s - jnp.max(s, axis=-1, keepdims=True))
            p = p / jnp.sum(p, axis=-1, keepdims=True)
            o_ref[:, cols] = jnp.dot(p.astype(BF16), v_ref[:, cols].astype(BF16),
                                     preferred_element_type=F32).astype(BF16)

    return pl.pallas_call(
        body, name=name, grid=(T // tm,),
        in_specs=[pl.BlockSpec((tm, D), lambda i: (i, 0)), pl.BlockSpec((M, D), lambda i: (0, 0)),
                  pl.BlockSpec((M, D), lambda i: (0, 1))],
        out_specs=pl.BlockSpec((tm, D), lambda i: (i, 0)),
        out_shape=jax.ShapeDtypeStruct((T, D), BF16),
        compiler_params=_params("parallel"))(q, kv, kv)


def attn_bwd(q, kv, do, name):
    T, D = q.shape
    M = kv.shape[0]
    hd = D // N_HEADS
    tm = _pick(T, 512, 8)
    sc = 1.0 / math.sqrt(hd)

    def body(q_ref, k_ref, v_ref, do_ref, dq_ref, dk_ref, dv_ref):
        i = pl.program_id(0)

        @pl.when(i == 0)
        def _():
            dk_ref[...] = jnp.zeros_like(dk_ref)
            dv_ref[...] = jnp.zeros_like(dv_ref)

        for h in range(N_HEADS):
            cols = slice(h * hd, (h + 1) * hd)
            qh, kh = q_ref[:, cols].astype(BF16), k_ref[:, cols].astype(BF16)
            vh, doh = v_ref[:, cols].astype(BF16), do_ref[:, cols].astype(BF16)
            s = _nt(qh, kh) * sc
            p = jnp.exp(s - jnp.max(s, axis=-1, keepdims=True))
            p = p / jnp.sum(p, axis=-1, keepdims=True)
            dp = _nt(doh, vh)
            dv_ref[:, cols] += _tn(p.astype(BF16), doh)
            ds = (p * (dp - jnp.sum(dp * p, axis=-1, keepdims=True)) * sc).astype(BF16)
            dq_ref[:, cols] = jnp.dot(ds, kh, preferred_element_type=F32).astype(BF16)
            dk_ref[:, cols] += _tn(ds, qh)

    tile = pl.BlockSpec((tm, D), lambda i: (i, 0))
    mem = jax.ShapeDtypeStruct((M, D), F32)
    return pl.pallas_call(
        body, name=name, grid=(T // tm,),
        in_specs=[tile, pl.BlockSpec((M, D), lambda i: (0, 0)), pl.BlockSpec((M, D), lambda i: (0, 1)), tile],
        out_specs=[tile, pl.BlockSpec((M, D), lambda i: (0, 0)), pl.BlockSpec((M, D), lambda i: (0, 0))],
        out_shape=[jax.ShapeDtypeStruct((T, D), BF16), mem, mem],
        compiler_params=_params("arbitrary"))(q, kv, kv, do)


def loss_head(x, g, target, name):
    T, D = x.shape
    tm = _pick(T, 512, 8)

    def body(x_ref, g_ref, t_ref, l_ref, dx_ref, dg_ref):
        i = pl.program_id(0)

        @pl.when(i == 0)
        def _():
            l_ref[...] = jnp.zeros_like(l_ref)
            dg_ref[...] = jnp.zeros_like(dg_ref)

        xv = x_ref[...]
        r = lax.rsqrt(jnp.mean(xv * xv, axis=-1, keepdims=True) + EPS)
        xh = xv * r
        err = xh * g_ref[...] - t_ref[...]
        l_ref[...] += 0.5 * jnp.sum(jnp.mean(err * err, axis=-1, keepdims=True), axis=0, keepdims=True)
        dy = err * (1.0 / D)
        dg_ref[...] += jnp.sum(dy * xh, axis=0, keepdims=True)
        dxh = dy * g_ref[...]
        dx_ref[...] = r * (dxh - xh * jnp.mean(dxh * xh, axis=-1, keepdims=True))

    tile = pl.BlockSpec((tm, D), lambda i: (i, 0))
    vec = pl.BlockSpec((1, D), lambda i: (0, 0))
    return pl.pallas_call(
        body, name=name, grid=(T // tm,),
        in_specs=[tile, vec, tile],
        out_specs=[pl.BlockSpec((1, 128), lambda i: (0, 0)), tile, vec],
        out_shape=[jax.ShapeDtypeStruct((1, 128), F32), jax.ShapeDtypeStruct((T, D), F32),
                   jax.ShapeDtypeStruct((1, D), F32)],
        compiler_params=_params("arbitrary"))(x, g, target)


def _block_diag(p):
    G, gd, _ = p.shape
    rows = [jnp.concatenate([p[g] if g == c else jnp.zeros((gd, gd), p.dtype) for c in range(G)], axis=1)
            for g in range(G)]
    return jnp.concatenate(rows, axis=0)


class _LazyWeight:
    def __init__(self, fetch, name, latest):
        self.fetch, self.name, self.latest = fetch, name, latest

    def __getitem__(self, l):
        return self.fetch(self.name, l, self.latest[0])


def _local_step(x, mem, target, fetch, ws, L):
    T, D = x.shape
    W = D // 4
    H = ws["sgu_w"].shape[1]
    row = lambda v: v.reshape(1, -1)
    latest = [x]
    wb = {n: _LazyWeight(fetch, n, latest) for n in BIG}
    saved = []
    for l in range(L):
        s = {"x0": x}
        latest[0] = x
        x = ffn_fwd(x, row(ws["norm_ffn1"][l]), wb["ffn1_w_in"][l], wb["ffn1_w_out"][l], "ffn_fwd")
        s["x1"] = x
        latest[0] = x
        z, s["h_mix"] = norm_matmul(x, row(ws["norm_mix"][l]), wb["mix_w_in"][l], "mix_in")
        s["z"] = z
        s["bias"] = jnp.repeat(ws["sgu_b"][l].T, W // H, axis=1)
        s["pbd"] = _block_diag(ws["pool_w"][l]).astype(BF16)
        y = jnp.concatenate([
            mix_a_fwd(z, ws["sconv_w"][l], "mix_a_fwd"),
            mix_b_fwd(z, row(ws["sgu_norm_g"][l]), ws["sgu_w"][l], s["bias"], "mix_b_fwd"),
            mix_c_fwd(z, ws["cconv_w"][l], row(ws["cconv_ln_g"][l]), row(ws["cconv_ln_b"][l]), "mix_c_fwd"),
            mix_d_fwd(z, s["pbd"], row(ws["pool_scale"][l]), "mix_d_fwd")], axis=1)
        s["y"] = y
        x = matmul_res(x, y, wb["mix_w_out"][l], "mix_out")
        s["x2"] = x
        s["q"], s["hq"] = norm_matmul(x, row(ws["norm_xattn"][l]), wb["xattn_wq"][l], "attn_q")
        s["kv"], s["mn"] = norm_matmul(mem, row(ws["norm_mem"][l]), wb["xattn_wkv"][l], "attn_kv")
        s["o"] = attn_fwd(s["q"], s["kv"], "attn_fwd")
        x = matmul_res(x, s["o"], wb["xattn_wo"][l], "attn_out")
        s["x3"] = x
        x = ffn_fwd(x, row(ws["norm_ffn2"][l]), wb["ffn2_w_in"][l], wb["ffn2_w_out"][l], "ffn_fwd")
        saved.append(s)

    loss, dx, dg_final = loss_head(x, row(ws["norm_final"]), target, "loss_head")
    grads = {n: [None] * L for n in WEIGHTS if n != "norm_final"}
    grads["norm_final"] = dg_final.reshape(-1)

    def ffn_back(xin, dxo, gname, win, wout, l):
        h, a, dzg, dzu = ffn_dz(xin, dxo, row(ws[gname][l]), wb[win][l], wb[wout][l], "ffn_dz")
        dxn, dg = ffn_dh(xin, dxo, row(ws[gname][l]), dzg, dzu, wb[win][l], "ffn_dh")
        grads[gname][l] = dg.reshape(-1)
        grads[win][l] = jnp.concatenate([matmul_tn(h, dzg, 1.0, "ffn_dwin"), matmul_tn(h, dzu, 1.0, "ffn_dwin")], axis=1)
        grads[wout][l] = matmul_tn(a, dxo, 0.5, "ffn_dwout")
        return dxn

    for l in reversed(range(L)):
        s = saved[l]
        dx = ffn_back(s["x3"], dx, "norm_ffn2", "ffn2_w_in", "ffn2_w_out", l)
        grads["xattn_wo"][l] = matmul_tn(s["o"], dx, 1.0, "dw_sq")
        do = matmul_nt(dx, wb["xattn_wo"][l], "attn_do")
        dq, dk, dv = attn_bwd(s["q"], s["kv"], do, "attn_bwd")
        grads["xattn_wq"][l] = matmul_tn(s["hq"], dq, 1.0, "dw_sq")
        dhq = matmul_nt(dq, wb["xattn_wq"][l], "attn_dhq")
        dx, dg = rmsnorm_bwd(dx, dhq, s["x2"], row(ws["norm_xattn"][l]), "norm_bwd")
        grads["norm_xattn"][l] = dg.reshape(-1)
        dkv = jnp.concatenate([dk, dv], axis=1)
        grads["xattn_wkv"][l] = matmul_tn(s["mn"], dkv, 1.0, "attn_dwkv")
        dmn = matmul_nt(dkv, wb["xattn_wkv"][l], "attn_dmn")
        _, dg = rmsnorm_bwd(None, dmn, mem, row(ws["norm_mem"][l]), "norm_mem_bwd")
        grads["norm_mem"][l] = dg.reshape(-1)
        grads["mix_w_out"][l] = matmul_tn(s["y"], dx, 1.0, "dw_sq")
        dy = matmul_nt(dx, wb["mix_w_out"][l], "mix_dy")
        z = s["z"]
        dab, dac, dax, dws = mix_a_bwd(z, dy, ws["sconv_w"][l], "mix_a_bwd")
        dbu, dbv, dwsgu, dbs, dgs = mix_b_bwd(z, dy, row(ws["sgu_norm_g"][l]), ws["sgu_w"][l], s["bias"], "mix_b_bwd")
        dca, dcg, dwc, dgc, dbc = mix_c_bwd(z, dy, ws["cconv_w"][l], row(ws["cconv_ln_g"][l]),
                                            row(ws["cconv_ln_b"][l]), "mix_c_bwd")
        ddw, dpbd, dsc = mix_d_bwd(z, dy, s["pbd"], row(ws["pool_scale"][l]), "mix_d_bwd")
        grads["sconv_w"][l], grads["cconv_w"][l] = dws, dwc
        grads["sgu_w"][l], grads["sgu_b"][l], grads["sgu_norm_g"][l] = dwsgu, dbs[:, :H].T, dgs.reshape(-1)
        grads["cconv_ln_g"][l], grads["cconv_ln_b"][l] = dgc.reshape(-1), dbc.reshape(-1)
        gd = W // len(POOL_WINDOWS)
        grads["pool_w"][l] = jnp.stack([dpbd[g * gd:(g + 1) * gd, g * gd:(g + 1) * gd] for g in range(len(POOL_WINDOWS))])
        grads["pool_scale"][l] = dsc.reshape(-1)
        dz = jnp.concatenate([dab, dac, dax, dbu, dbv, dca, dcg, ddw], axis=1)
        grads["mix_w_in"][l] = matmul_tn(s["h_mix"], dz, 1.0, "mix_dwin")
        dh = matmul_nt(dz, wb["mix_w_in"][l], "mix_dh")
        dx, dg = rmsnorm_bwd(dx, dh, s["x1"], row(ws["norm_mix"][l]), "norm_bwd")
        grads["norm_mix"][l] = dg.reshape(-1)
        dx = ffn_back(s["x0"], dx, "norm_ffn1", "ffn1_w_in", "ffn1_w_out", l)

    return loss[0, 0], dx, grads


ANY = pl.BlockSpec(memory_space=pl.ANY)


def _other_chips(x, y):
    return [(1 - x, y), (x, 1 - y), (1 - x, 1 - y)]


def _shard_slice(ref, axis, chip, size):
    idx = [slice(None)] * len(ref.shape)
    idx[axis] = pl.ds(pl.multiple_of(chip * size, size), size)
    return ref.at[tuple(idx)]


def all_gather_chips(shards, axes, name):
    n = len(shards)

    def body(*refs):
        ins, outs = refs[:n], refs[n:2 * n]
        send, recv, loc = refs[2 * n:]
        x, y, c = lax.axis_index("x"), lax.axis_index("y"), lax.axis_index("c")
        me = 2 * x + y
        chips = _other_chips(x, y)
        started = []
        for i in range(n):
            size = ins[i].shape[axes[i]]
            cp = pltpu.make_async_copy(ins[i], _shard_slice(outs[i], axes[i], me, size), loc.at[i])
            cp.start()
            started.append(cp)
        sends = []
        for i in range(n):
            size = ins[i].shape[axes[i]]
            for j, (px, py) in enumerate(chips):
                cp = pltpu.make_async_remote_copy(
                    src_ref=ins[i], dst_ref=_shard_slice(outs[i], axes[i], me, size),
                    send_sem=send.at[i, j], recv_sem=recv.at[i, j], device_id=(px, py, c), device_id_type=MESH_ID)
                cp.start()
                sends.append(cp)
        for i in range(n):
            size = ins[i].shape[axes[i]]
            for j, (px, py) in enumerate(chips):
                pltpu.make_async_remote_copy(
                    src_ref=ins[i], dst_ref=_shard_slice(outs[i], axes[i], 2 * px + py, size),
                    send_sem=send.at[i, j], recv_sem=recv.at[i, j], device_id=(px, py, c),
                    device_id_type=MESH_ID).wait_recv()
        for cp in sends:
            cp.wait_send()
        for cp in started:
            cp.wait()

    def full(a, ax):
        shape = list(a.shape)
        shape[ax] *= N_CHIPS
        return jax.ShapeDtypeStruct(tuple(shape), a.dtype)

    return pl.pallas_call(
        body, name=name, in_specs=[ANY] * n, out_specs=[ANY] * n,
        out_shape=[full(a, ax) for a, ax in zip(shards, axes)],
        scratch_shapes=[pltpu.SemaphoreType.DMA((n, 3)), pltpu.SemaphoreType.DMA((n, 3)),
                        pltpu.SemaphoreType.DMA((n,))],
        compiler_params=pltpu.CompilerParams(has_side_effects=True))(*shards)


def cast_into_slot(shard, axis, chip, name):
    L, K, N = shard.shape
    bm = _pick(K, 256, 16)
    full = (K * N_CHIPS, N) if axis == 1 else (K, N * N_CHIPS)
    nb = K // bm

    def body(c_ref, s_ref, *o_refs):
        for l in range(L):
            o_refs[l][...] = s_ref[l].astype(BF16)

    out_map = (lambda i, c: (c[0] * nb + i, 0)) if axis == 1 else (lambda i, c: (i, c[0]))
    spec = pltpu.PrefetchScalarGridSpec(
        num_scalar_prefetch=1, grid=(nb,),
        in_specs=[pl.BlockSpec((L, bm, N), lambda i, c: (0, i, 0))],
        out_specs=[pl.BlockSpec((bm, N), out_map)] * L)
    return pl.pallas_call(body, name=name, grid_spec=spec, out_shape=[jax.ShapeDtypeStruct(full, BF16)] * L,
                          compiler_params=_params("parallel"))(chip, shard)


HBM = pl.BlockSpec(memory_space=pltpu.HBM)
SEM = pl.BlockSpec(memory_space=pltpu.SEMAPHORE)
DATAFLOW = pltpu.SideEffectType.DATAFLOW_SIDE_EFFECTING


def _gather_copies(bufs, axes, send, recv):
    x, y, c = lax.axis_index("x"), lax.axis_index("y"), lax.axis_index("c")
    me = 2 * x + y
    out, back = [], []
    for i, (buf, ax) in enumerate(zip(bufs, axes)):
        size = buf.shape[ax] // N_CHIPS
        for j, (px, py) in enumerate(_other_chips(x, y)):
            mine = _shard_slice(buf, ax, me, size)
            kw = dict(send_sem=send.at[3 * i + j], recv_sem=recv.at[3 * i + j], device_id=(px, py, c),
                      device_id_type=MESH_ID)
            out.append(pltpu.make_async_remote_copy(src_ref=mine, dst_ref=mine, **kw))
            back.append(pltpu.make_async_remote_copy(src_ref=mine, dst_ref=_shard_slice(buf, ax, 2 * px + py, size), **kw))
    return out, back


def gather_start(bufs, axes, after, name):
    n = len(bufs)

    def body(*refs):
        out, _ = _gather_copies(refs[:n], axes, refs[n + 1], refs[n + 2])
        for cp in out:
            cp.start()
        refs[-1][...] = jnp.zeros_like(refs[-1])

    bufs = [pltpu.with_memory_space_constraint(b, pltpu.HBM) for b in bufs]
    res = pl.pallas_call(
        body, name=name,
        out_shape=(pltpu.SemaphoreType.DMA((3 * n,)), pltpu.SemaphoreType.DMA((3 * n,)),
                   *[pltpu.HBM(b.shape, b.dtype) for b in bufs], jax.ShapeDtypeStruct((8, 128), F32)),
        in_specs=[HBM] * n + [ANY], out_specs=(SEM, SEM, *[HBM] * n, pl.BlockSpec(memory_space=pltpu.VMEM)),
        input_output_aliases={i: 2 + i for i in range(n)},
        compiler_params=pltpu.CompilerParams(has_side_effects=DATAFLOW))(*bufs, after)
    return res[0], res[1], list(res[2:2 + n]), res[-1]


def gather_wait(send, recv, bufs, axes, after, name):
    n = len(bufs)

    def body(*refs):
        _, back = _gather_copies(refs[:n], axes, refs[n], refs[n + 1])
        for cp in back:
            cp.wait_send()
            cp.wait_recv()

    return list(pl.pallas_call(
        body, name=name, out_shape=tuple(pltpu.HBM(b.shape, b.dtype) for b in bufs),
        in_specs=[HBM] * n + [SEM, SEM, ANY], out_specs=tuple([HBM] * n),
        input_output_aliases={i: i for i in range(n)},
        compiler_params=pltpu.CompilerParams(has_side_effects=DATAFLOW))(*bufs, send, recv, after))


def all_reduce_small(p, name):
    R = p.shape[0]

    def body(p_ref, o_ref, sib_ref, chip_ref, send, recv):
        x, y, c = lax.axis_index("x"), lax.axis_index("y"), lax.axis_index("c")
        me = 2 * x + y
        chips = _other_chips(x, y)
        pair = pltpu.make_async_remote_copy(src_ref=p_ref, dst_ref=sib_ref, send_sem=send.at[0], recv_sem=recv.at[0],
                                            device_id=(x, y, 1 - c), device_id_type=MESH_ID)
        pair.start()
        pair.wait()
        chip_ref[me] = p_ref[...] + sib_ref[...]
        sends = []
        for j, (px, py) in enumerate(chips):
            cp = pltpu.make_async_remote_copy(src_ref=chip_ref.at[me], dst_ref=chip_ref.at[me], send_sem=send.at[1 + j],
                                              recv_sem=recv.at[1 + j], device_id=(px, py, c), device_id_type=MESH_ID)
            cp.start()
            sends.append(cp)
        for j, (px, py) in enumerate(chips):
            pltpu.make_async_remote_copy(src_ref=chip_ref.at[me], dst_ref=chip_ref.at[2 * px + py], send_sem=send.at[1 + j],
                                         recv_sem=recv.at[1 + j], device_id=(px, py, c), device_id_type=MESH_ID).wait_recv()
        for cp in sends:
            cp.wait_send()
        o_ref[...] = ((chip_ref[0] + chip_ref[1]) + chip_ref[2]) + chip_ref[3]

    vm = pl.BlockSpec(memory_space=pltpu.VMEM)
    return pl.pallas_call(
        body, name=name, in_specs=[vm], out_specs=vm, out_shape=jax.ShapeDtypeStruct((R, 128), F32),
        scratch_shapes=[pltpu.VMEM((R, 128), F32), pltpu.VMEM((N_CHIPS, R, 128), F32),
                        pltpu.SemaphoreType.DMA((4,)), pltpu.SemaphoreType.DMA((4,))],
        compiler_params=pltpu.CompilerParams(has_side_effects=True, vmem_limit_bytes=VMEM_LIMIT))(p)


def swap_sibling(layer0, layer1, name):
    n = len(layer0)

    def body(*refs):
        l0, l1, outs = refs[:n], refs[n:2 * n], refs[2 * n:3 * n]
        send, recv = refs[3 * n:]
        x, y, c = lax.axis_index("x"), lax.axis_index("y"), lax.axis_index("c")

        def copies(srcs):
            return [pltpu.make_async_remote_copy(src_ref=srcs[i], dst_ref=outs[i], send_sem=send.at[i], recv_sem=recv.at[i],
                                                 device_id=(x, y, 1 - c), device_id_type=MESH_ID) for i in range(n)]

        @pl.when(c == 0)
        def _():
            for cp in copies(l1):
                cp.start()

        @pl.when(c == 1)
        def _():
            for cp in copies(l0):
                cp.start()

        for cp in copies(l0):
            cp.wait_recv()
            cp.wait_send()

    return pl.pallas_call(
        body, name=name, in_specs=[ANY] * (2 * n), out_specs=[ANY] * n,
        out_shape=[jax.ShapeDtypeStruct(a.shape, a.dtype) for a in layer0],
        scratch_shapes=[pltpu.SemaphoreType.DMA((n,)), pltpu.SemaphoreType.DMA((n,))],
        compiler_params=pltpu.CompilerParams(has_side_effects=True))(*layer0, *layer1)


def scatter_chips(sums, axes, name):
    n = len(sums)

    def body(*refs):
        ins, outs = refs[:n], refs[n:2 * n]
        send, recv = refs[2 * n:]
        x, y, c = lax.axis_index("x"), lax.axis_index("y"), lax.axis_index("c")
        sends = []
        for i in range(n):
            size = outs[i].shape[1 + axes[i]]
            for j, (px, py) in enumerate(_other_chips(x, y)):
                cp = pltpu.make_async_remote_copy(
                    src_ref=_shard_slice(ins[i], axes[i], 2 * px + py, size), dst_ref=outs[i].at[j],
                    send_sem=send.at[i, j], recv_sem=recv.at[i, j], device_id=(px, py, c), device_id_type=MESH_ID)
                cp.start()
                sends.append(cp)
        for cp in sends:
            cp.wait_recv()
        for cp in sends:
            cp.wait_send()

    def slots(a, ax):
        shape = list(a.shape)
        shape[ax] //= N_CHIPS
        return jax.ShapeDtypeStruct((3,) + tuple(shape), a.dtype)

    return pl.pallas_call(
        body, name=name, in_specs=[ANY] * n, out_specs=[ANY] * n,
        out_shape=[slots(a, ax) for a, ax in zip(sums, axes)],
        scratch_shapes=[pltpu.SemaphoreType.DMA((n, 3)), pltpu.SemaphoreType.DMA((n, 3))],
        compiler_params=pltpu.CompilerParams(has_side_effects=True))(*sums)


def share_sibling(halves, name):
    n = len(halves)

    def body(*refs):
        ins = refs[:n]
        send, recv = refs[2 * n:]
        x, y, c = lax.axis_index("x"), lax.axis_index("y"), lax.axis_index("c")
        cps = [pltpu.make_async_remote_copy(src_ref=ins[i].at[c], dst_ref=ins[i].at[c], send_sem=send.at[i], recv_sem=recv.at[i],
                                            device_id=(x, y, 1 - c), device_id_type=MESH_ID) for i in range(n)]
        for cp in cps:
            cp.start()
        for i in range(n):
            pltpu.make_async_remote_copy(src_ref=ins[i].at[c], dst_ref=ins[i].at[1 - c], send_sem=send.at[i], recv_sem=recv.at[i],
                                         device_id=(x, y, 1 - c), device_id_type=MESH_ID).wait_recv()
        for cp in cps:
            cp.wait_send()

    return pl.pallas_call(
        body, name=name, in_specs=[ANY] * n, out_specs=[ANY] * n,
        out_shape=[jax.ShapeDtypeStruct(a.shape, a.dtype) for a in halves],
        input_output_aliases={i: i for i in range(n)},
        scratch_shapes=[pltpu.SemaphoreType.DMA((n,)), pltpu.SemaphoreType.DMA((n,))],
        compiler_params=pltpu.CompilerParams(has_side_effects=True))(*halves)


def add_own_layer(layer0, layer1, other, core, name):
    K, N = other.shape
    bm, bn = _pick(K, 256, 16), _pick(N, 1408, 128)

    def body(c_ref, a_ref, b_ref, o_ref, out_ref):
        @pl.when(c_ref[0] == 0)
        def _():
            out_ref[...] = (a_ref[...] + o_ref[...]).astype(GRAD_WIRE)

        @pl.when(c_ref[0] == 1)
        def _():
            out_ref[...] = (b_ref[...] + o_ref[...]).astype(GRAD_WIRE)

    spec = pltpu.PrefetchScalarGridSpec(
        num_scalar_prefetch=1, grid=(K // bm, N // bn),
        in_specs=[pl.BlockSpec((bm, bn), lambda i, j, c: (i * (1 - c[0]), j * (1 - c[0]))),
                  pl.BlockSpec((bm, bn), lambda i, j, c: (i * c[0], j * c[0])),
                  pl.BlockSpec((bm, bn), lambda i, j, c: (i, j))],
        out_specs=pl.BlockSpec((bm, bn), lambda i, j, c: (i, j)))
    return pl.pallas_call(body, name=name, grid_spec=spec, out_shape=jax.ShapeDtypeStruct((K, N), GRAD_WIRE),
                          compiler_params=_params("parallel", "parallel"))(core, layer0, layer1, other)


def add_slots(s, q, axis, place, name):
    _, k, n = q.shape
    bm, bn = _pick(k, 256, 16), _pick(n, 1408, 128)
    nbi, nbj = k // bm, n // bn

    def body(p_ref, s_ref, q_ref, o_ref):
        o_ref[...] = ((s_ref[...].astype(F32) + q_ref[0].astype(F32)) + q_ref[1].astype(F32)) + q_ref[2].astype(F32)

    mine = (lambda i, j, p: (p[0] * nbi + i, j)) if axis == 0 else (lambda i, j, p: (i, p[0] * nbj + j))
    spec = pltpu.PrefetchScalarGridSpec(
        num_scalar_prefetch=1, grid=(nbi, nbj),
        in_specs=[pl.BlockSpec((bm, bn), mine), pl.BlockSpec((3, bm, bn), lambda i, j, p: (0, i, j))],
        out_specs=pl.BlockSpec((None, bm, bn), lambda i, j, p: (p[1], i, j)))
    return pl.pallas_call(body, name=name, grid_spec=spec, out_shape=jax.ShapeDtypeStruct((2, k, n), F32),
                          compiler_params=_params("parallel", "parallel"))(place, s, q)


def adamw(w, g, m, v, name):
    R, N = w.shape
    bm = _pick(R, 256, 8)
    c1 = 1.0 / (1.0 - ADAM_B1 ** ADAM_STEP)
    c2 = 1.0 / (1.0 - ADAM_B2 ** ADAM_STEP)

    def body(w_ref, g_ref, m_ref, v_ref, d_ref, nm_ref, nv_ref):
        gv = g_ref[...]
        nm = ADAM_B1 * m_ref[...] + (1.0 - ADAM_B1) * gv
        nv = ADAM_B2 * v_ref[...] + (1.0 - ADAM_B2) * (gv * gv)
        nm_ref[...] = nm
        nv_ref[...] = nv
        d_ref[...] = -ADAM_LR * ((nm * c1) / (jnp.sqrt(nv * c2) + ADAM_EPS) + ADAM_WD * w_ref[...])

    blk = pl.BlockSpec((bm, N), lambda i: (i, 0))
    out = jax.ShapeDtypeStruct((R, N), F32)
    return pl.pallas_call(body, name=name, grid=(R // bm,), in_specs=[blk] * 4, out_specs=[blk] * 3,
                          out_shape=[out, out, out], compiler_params=_params("parallel"))(w, g, m, v)


def _pack(arrays):
    flat = jnp.concatenate([a.reshape(-1) for a in arrays])
    rows = -(-flat.shape[0] // (256 * 128)) * 256
    return jnp.pad(flat, (0, rows * 128 - flat.shape[0])).reshape(rows, 128)


def _unpack(p, shapes):
    flat, out, at = p.reshape(-1), [], 0
    for s in shapes:
        n = math.prod(s)
        out.append(flat[at:at + n].reshape(s))
        at += n
    return out


def kernel(x, mem, norm_ffn1, ffn1_w_in, ffn1_w_out, norm_mix, mix_w_in, sconv_w, sgu_norm_g, sgu_w, sgu_b, cconv_w, cconv_ln_g, cconv_ln_b, pool_w, pool_scale, mix_w_out, norm_xattn, norm_mem, xattn_wq, xattn_wkv, xattn_wo, norm_ffn2, ffn2_w_in, ffn2_w_out, norm_final, loss_target, m_norm_ffn1, m_ffn1_w_in, m_ffn1_w_out, m_norm_mix, m_mix_w_in, m_sconv_w, m_sgu_norm_g, m_sgu_w, m_sgu_b, m_cconv_w, m_cconv_ln_g, m_cconv_ln_b, m_pool_w, m_pool_scale, m_mix_w_out, m_norm_xattn, m_norm_mem, m_xattn_wq, m_xattn_wkv, m_xattn_wo, m_norm_ffn2, m_ffn2_w_in, m_ffn2_w_out, m_norm_final, v_norm_ffn1, v_ffn1_w_in, v_ffn1_w_out, v_norm_mix, v_mix_w_in, v_sconv_w, v_sgu_norm_g, v_sgu_w, v_sgu_b, v_cconv_w, v_cconv_ln_g, v_cconv_ln_b, v_pool_w, v_pool_scale, v_mix_w_out, v_norm_xattn, v_norm_mem, v_xattn_wq, v_xattn_wkv, v_xattn_wo, v_norm_ffn2, v_ffn2_w_in, v_ffn2_w_out, v_norm_final):
    given = dict(locals())
    w = {n: given[n] for n in WEIGHTS}
    L = ffn1_w_in.shape[0]
    assert L == 2, "the reduce-scatter gives one layer to each core of a chip"
    chip = 2 * lax.axis_index("x") + lax.axis_index("y")
    chip1 = chip.astype(jnp.int32).reshape(1)
    core = lax.axis_index("c").astype(jnp.int32).reshape(1)
    place = jnp.concatenate([chip1, core])

    axis = {n: 1 if n in COL_SHARDED else 0 for n in BIG}
    bufs = {}
    for n in BIG:
        for l, b in enumerate(cast_into_slot(w[n], axis[n] + 1, chip1, "cast_weights")):
            bufs[n, l] = b
    groups = {"a": [(n, 0) for n in BIG[:2]], "b": [(n, 0) for n in BIG[2:]], "c": [(n, 1) for n in BIG]}
    started, token = {}, jnp.zeros((8, 128), F32)
    for g, keys in groups.items():
        send, recv, thru, token = gather_start([bufs[k] for k in keys], [axis[k[0]] for k in keys], token,
                                               "gather_start_" + g)
        started[g] = (send, recv, thru)
    ready = {}

    def fetch(n, l, after):
        g = next(g for g, keys in groups.items() if (n, l) in keys)
        if g not in ready:
            send, recv, thru = started[g]
            done = gather_wait(send, recv, thru, [axis[k[0]] for k in groups[g]], token if g == "a" else after,
                               "gather_wait_" + g)
            ready[g] = dict(zip(groups[g], done))
        return ready[g][n, l]

    wc = sconv_w.shape[-1]
    conv_rows = [w[n].reshape(-1, wc) for n in SMALL_CONV]
    n_conv = sum(r.shape[0] for r in conv_rows)
    conv_pack = jnp.pad(jnp.concatenate(conv_rows, axis=0), ((0, -n_conv % 8), (0, 128 - wc)))[None]
    conv_all = all_gather_chips([conv_pack], [0], "gather_conv")[0]
    conv_full = jnp.moveaxis(conv_all[:, :n_conv, :wc], 0, 1).reshape(n_conv, N_CHIPS * wc)
    ws = {n: w[n] for n in SMALL_REPL}
    at = 0
    for n in SMALL_CONV:
        rows = w[n].shape[0] * w[n].shape[1]
        ws[n] = conv_full[at:at + rows].reshape(w[n].shape[0], w[n].shape[1], N_CHIPS * wc)
        at += rows

    loss_part, grad_x, grads = _local_step(x[0], mem[0], loss_target[0], fetch, ws, L)
    loss = lax.psum(loss_part, ("x", "y", "c"))

    small = SMALL_REPL + SMALL_CONV
    small_g = [grads[n] if n == "norm_final" else jnp.stack(grads[n]) for n in small]
    total = _unpack(all_reduce_small(_pack(small_g), "reduce_small"), [g.shape for g in small_g])
    grad = dict(zip(small, total))
    for n in SMALL_CONV:
        grad[n] = lax.dynamic_slice_in_dim(grad[n], chip * wc, wc, axis=2)

    from_sibling = swap_sibling([grads[n][0] for n in BIG], [grads[n][1] for n in BIG], "reduce_pair")
    pair_sums = [add_own_layer(grads[n][0], grads[n][1], o, core, "add_pair") for n, o in zip(BIG, from_sibling)]
    arrivals = scatter_chips(pair_sums, [axis[n] for n in BIG], "reduce_chips")
    halves = [add_slots(s, q, axis[n], place, "add_chips") for n, s, q in zip(BIG, pair_sums, arrivals)]
    for n, g in zip(BIG, share_sibling(halves, "share_pair")):
        grad[n] = g

    delta, new_m, new_v = {}, {}, {}
    for n in BIG:
        shape = w[n].shape
        flat = lambda a: a.reshape(-1, shape[-1])
        d, nm, nv = adamw(flat(w[n]), flat(grad[n]), flat(given["m_" + n]), flat(given["v_" + n]), "adamw")
        delta[n], new_m[n], new_v[n] = d.reshape(shape), nm.reshape(shape), nv.reshape(shape)
    shapes = [w[n].shape for n in small]
    packed = [_pack([src[n] for n in small]) for src in
              (w, grad, {n: given["m_" + n] for n in small}, {n: given["v_" + n] for n in small})]
    for out, p in zip((delta, new_m, new_v), adamw(*packed, "adamw_small")):
        out.update(zip(small, _unpack(p, shapes)))

    return (loss, grad_x[None], *[grad[n] for n in WEIGHTS], *[delta[n] for n in WEIGHTS],
            *[new_m[n] for n in WEIGHTS], *[new_v[n] for n in WEIGHTS])
```

```python
import functools
import math

import jax
import jax.numpy as jnp
from jax import lax
from jax.experimental import pallas as pl
from jax.experimental.pallas import tpu as pltpu

F32 = jnp.float32
BF16 = jnp.bfloat16
EPS = 1e-6
SEQ_CHUNK = 128
POOL_WINDOWS = (2, 4, 8, 16)
N_HEADS = 4
ADAM_LR, ADAM_B1, ADAM_B2, ADAM_EPS, ADAM_WD, ADAM_STEP = 0.001, 0.9, 0.999, 1e-08, 0.01, 10
VMEM_LIMIT = 56 * 1024 * 1024
MESH_ID = pl.DeviceIdType.MESH
N_CHIPS = 4
GRAD_WIRE = BF16

BIG = ("ffn1_w_in", "ffn1_w_out", "mix_w_in", "mix_w_out", "xattn_wq", "xattn_wkv", "xattn_wo",
       "ffn2_w_in", "ffn2_w_out")
COL_SHARDED = ("ffn1_w_in", "mix_w_in", "xattn_wkv", "ffn2_w_in")
SMALL_CONV = ("sconv_w", "cconv_w")
SMALL_REPL = ("norm_ffn1", "norm_mix", "sgu_norm_g", "sgu_w", "sgu_b", "cconv_ln_g", "cconv_ln_b",
              "pool_w", "pool_scale", "norm_xattn", "norm_mem", "norm_ffn2", "norm_final")
WEIGHTS = ("norm_ffn1", "ffn1_w_in", "ffn1_w_out", "norm_mix", "mix_w_in", "sconv_w", "sgu_norm_g",
           "sgu_w", "sgu_b", "cconv_w", "cconv_ln_g", "cconv_ln_b", "pool_w", "pool_scale",
           "mix_w_out", "norm_xattn", "norm_mem", "xattn_wq", "xattn_wkv", "xattn_wo", "norm_ffn2",
           "ffn2_w_in", "ffn2_w_out", "norm_final")


def _pick(n, pref, align):
    best = None
    for d in range(align, min(n, pref) + 1, align):
        if n % d == 0:
            best = d
    return best or n


def _sig(x):
    return 1.0 / (1.0 + jnp.exp(-x))


def _nt(a, b):
    return lax.dot_general(a, b, (((1,), (1,)), ((), ())), preferred_element_type=F32)


def _tn(a, b):
    return lax.dot_general(a, b, (((0,), (0,)), ((), ())), preferred_element_type=F32)


def _params(*sem):
    return pltpu.CompilerParams(dimension_semantics=sem, vmem_limit_bytes=VMEM_LIMIT)


def norm_matmul(x, g, w, name):
    T, D = x.shape
    N = w.shape[1]
    tm, tn = _pick(T, 512, 8), _pick(N, 512, 128)

    def body(x_ref, g_ref, w_ref, o_ref, h_ref):
        j = pl.program_id(1)

        @pl.when(j == 0)
        def _():
            xv = x_ref[...]
            r = lax.rsqrt(jnp.mean(xv * xv, axis=-1, keepdims=True) + EPS)
            h_ref[...] = (xv * r * g_ref[...]).astype(BF16)

        o_ref[...] = jnp.dot(h_ref[...], w_ref[...], preferred_element_type=F32)

    return pl.pallas_call(
        body, name=name, grid=(T // tm, N // tn),
        in_specs=[pl.BlockSpec((tm, D), lambda i, j: (i, 0)), pl.BlockSpec((1, D), lambda i, j: (0, 0)),
                  pl.BlockSpec((D, tn), lambda i, j: (0, j))],
        out_specs=[pl.BlockSpec((tm, tn), lambda i, j: (i, j)), pl.BlockSpec((tm, D), lambda i, j: (i, 0))],
        out_shape=[jax.ShapeDtypeStruct((T, N), F32), jax.ShapeDtypeStruct((T, D), BF16)],
        compiler_params=_params("parallel", "arbitrary"))(x, g, w)


def matmul_res(res, a, w, name):
    T, K = a.shape
    N = w.shape[1]
    tm, tn = _pick(T, 512, 8), _pick(N, 512, 128)

    def body(r_ref, a_ref, w_ref, o_ref):
        o_ref[...] = r_ref[...] + jnp.dot(a_ref[...].astype(BF16), w_ref[...], preferred_element_type=F32)

    return pl.pallas_call(
        body, name=name, grid=(T // tm, N // tn),
        in_specs=[pl.BlockSpec((tm, tn), lambda i, j: (i, j)), pl.BlockSpec((tm, K), lambda i, j: (i, 0)),
                  pl.BlockSpec((K, tn), lambda i, j: (0, j))],
        out_specs=pl.BlockSpec((tm, tn), lambda i, j: (i, j)),
        out_shape=jax.ShapeDtypeStruct((T, N), F32),
        compiler_params=_params("parallel", "parallel"))(res, a, w)


def matmul_nt(a, w, name):
    T, N = a.shape
    M = w.shape[0]
    tm, tmm = _pick(T, 512, 8), _pick(M, 512, 128)

    def body(a_ref, w_ref, o_ref):
        o_ref[...] = _nt(a_ref[...].astype(BF16), w_ref[...])

    return pl.pallas_call(
        body, name=name, grid=(T // tm, M // tmm),
        in_specs=[pl.BlockSpec((tm, N), lambda i, j: (i, 0)), pl.BlockSpec((tmm, N), lambda i, j: (j, 0))],
        out_specs=pl.BlockSpec((tm, tmm), lambda i, j: (i, j)),
        out_shape=jax.ShapeDtypeStruct((T, M), F32),
        compiler_params=_params("parallel", "parallel"))(a, w)


def matmul_tn(a, b, scale, name):
    T, M = a.shape
    N = b.shape[1]
    bm, bn, bk = _pick(M, 1408, 128), _pick(N, 1408, 128), _pick(T, 512, 8)
    nk = T // bk

    def body(a_ref, b_ref, o_ref):
        k = pl.program_id(2)

        @pl.when(k == 0)
        def _():
            o_ref[...] = jnp.zeros_like(o_ref)

        o_ref[...] += _tn(a_ref[...].astype(BF16), b_ref[...].astype(BF16))

        if scale != 1.0:
            @pl.when(k == nk - 1)
            def _():
                o_ref[...] = o_ref[...] * scale

    return pl.pallas_call(
        body, name=name, grid=(M // bm, N // bn, nk),
        in_specs=[pl.BlockSpec((bk, bm), lambda i, j, k: (k, i)), pl.BlockSpec((bk, bn), lambda i, j, k: (k, j))],
        out_specs=pl.BlockSpec((bm, bn), lambda i, j, k: (i, j)),
        out_shape=jax.ShapeDtypeStruct((M, N), F32),
        compiler_params=_params("parallel", "parallel", "arbitrary"))(a, b)


def rmsnorm_bwd(dxo, dh, x, g, name):
    T, D = x.shape
    tm = _pick(T, 512, 8)
    has_res = dxo is not None

    def body(*refs):
        if has_res:
            dxo_ref, dh_ref, x_ref, g_ref, dx_ref, dg_ref = refs
        else:
            dh_ref, x_ref, g_ref, dx_ref, dg_ref = refs
        i = pl.program_id(0)

        @pl.when(i == 0)
        def _():
            dg_ref[...] = jnp.zeros_like(dg_ref)

        xv, dh_v = x_ref[...], dh_ref[...]
        r = lax.rsqrt(jnp.mean(xv * xv, axis=-1, keepdims=True) + EPS)
        xh = xv * r
        dg_ref[...] += jnp.sum(dh_v * xh, axis=0, keepdims=True)
        dxh = dh_v * g_ref[...]
        dx = r * (dxh - xh * jnp.mean(dxh * xh, axis=-1, keepdims=True))
        dx_ref[...] = dx + dxo_ref[...] if has_res else dx

    tile = pl.BlockSpec((tm, D), lambda i: (i, 0))
    vec = pl.BlockSpec((1, D), lambda i: (0, 0))
    args = ([dxo] if has_res else []) + [dh, x, g]
    return pl.pallas_call(
        body, name=name, grid=(T // tm,),
        in_specs=[tile] * (len(args) - 1) + [vec],
        out_specs=[tile, vec],
        out_shape=[jax.ShapeDtypeStruct((T, D), F32), jax.ShapeDtypeStruct((1, D), F32)],
        compiler_params=_params("arbitrary"))(*args)


def ffn_fwd(x, g, w_in, w_out, name):
    T, D = x.shape
    F = w_out.shape[0]
    tm, tf = _pick(T, 512, 8), _pick(F, 1408, 128)
    nf = F // tf

    def body(x_ref, g_ref, wg_ref, wu_ref, wo_ref, o_ref, h_ref, acc_ref):
        j = pl.program_id(1)

        @pl.when(j == 0)
        def _():
            xv = x_ref[...]
            r = lax.rsqrt(jnp.mean(xv * xv, axis=-1, keepdims=True) + EPS)
            h_ref[...] = (xv * r * g_ref[...]).astype(BF16)
            acc_ref[...] = jnp.zeros_like(acc_ref)

        h = h_ref[...]
        zg = jnp.dot(h, wg_ref[...], preferred_element_type=F32)
        zu = jnp.dot(h, wu_ref[...], preferred_element_type=F32)
        a = (zg * _sig(zg) * zu).astype(BF16)
        acc_ref[...] += jnp.dot(a, wo_ref[...], preferred_element_type=F32)

        @pl.when(j == nf - 1)
        def _():
            o_ref[...] = x_ref[...] + 0.5 * acc_ref[...]

    return pl.pallas_call(
        body, name=name, grid=(T // tm, nf),
        in_specs=[pl.BlockSpec((tm, D), lambda i, j: (i, 0)), pl.BlockSpec((1, D), lambda i, j: (0, 0)),
                  pl.BlockSpec((D, tf), lambda i, j: (0, j)), pl.BlockSpec((D, tf), lambda i, j: (0, j + nf)),
                  pl.BlockSpec((tf, D), lambda i, j: (j, 0))],
        out_specs=pl.BlockSpec((tm, D), lambda i, j: (i, 0)),
        out_shape=jax.ShapeDtypeStruct((T, D), F32),
        scratch_shapes=[pltpu.VMEM((tm, D), BF16), pltpu.VMEM((tm, D), F32)],
        compiler_params=_params("parallel", "arbitrary"))(x, g, w_in, w_in, w_out)


def ffn_dz(x, dxo, g, w_in, w_out, name):
    T, D = x.shape
    F = w_out.shape[0]
    tm, tf = _pick(T, 512, 8), _pick(F, 256, 128)
    nf = F // tf

    def body(x_ref, dxo_ref, g_ref, wg_ref, wu_ref, wo_ref, h_ref, a_ref, dzg_ref, dzu_ref, do_ref):
        j = pl.program_id(1)

        @pl.when(j == 0)
        def _():
            xv = x_ref[...]
            r = lax.rsqrt(jnp.mean(xv * xv, axis=-1, keepdims=True) + EPS)
            h_ref[...] = (xv * r * g_ref[...]).astype(BF16)
            do_ref[...] = (0.5 * dxo_ref[...]).astype(BF16)

        h = h_ref[...]
        zg = jnp.dot(h, wg_ref[...], preferred_element_type=F32)
        zu = jnp.dot(h, wu_ref[...], preferred_element_type=F32)
        s = _sig(zg)
        silu = zg * s
        a_ref[...] = (silu * zu).astype(BF16)
        da = _nt(do_ref[...], wo_ref[...])
        dzu_ref[...] = (da * silu).astype(BF16)
        dzg_ref[...] = (da * zu * (s * (1.0 + zg * (1.0 - s)))).astype(BF16)

    tile = pl.BlockSpec((tm, D), lambda i, j: (i, 0))
    fblk = pl.BlockSpec((tm, tf), lambda i, j: (i, j))
    hidden = jax.ShapeDtypeStruct((T, F), BF16)
    return pl.pallas_call(
        body, name=name, grid=(T // tm, nf),
        in_specs=[tile, tile, pl.BlockSpec((1, D), lambda i, j: (0, 0)), pl.BlockSpec((D, tf), lambda i, j: (0, j)),
                  pl.BlockSpec((D, tf), lambda i, j: (0, j + nf)), pl.BlockSpec((tf, D), lambda i, j: (j, 0))],
        out_specs=[tile, fblk, fblk, fblk],
        out_shape=[jax.ShapeDtypeStruct((T, D), BF16), hidden, hidden, hidden],
        scratch_shapes=[pltpu.VMEM((tm, D), BF16)],
        compiler_params=_params("parallel", "arbitrary"))(x, dxo, g, w_in, w_in, w_out)


def ffn_dh(x, dxo, g, dzg, dzu, w_in, name):
    T, D = x.shape
    F = dzg.shape[1]
    tm = _pick(T, 256, 8)

    def body(x_ref, dxo_ref, g_ref, dzg_ref, dzu_ref, wg_ref, wu_ref, dx_ref, dg_ref):
        i = pl.program_id(0)

        @pl.when(i == 0)
        def _():
            dg_ref[...] = jnp.zeros_like(dg_ref)

        dh = _nt(dzg_ref[...], wg_ref[...]) + _nt(dzu_ref[...], wu_ref[...])
        xv = x_ref[...]
        r = lax.rsqrt(jnp.mean(xv * xv, axis=-1, keepdims=True) + EPS)
        xh = xv * r
        dg_ref[...] += jnp.sum(dh * xh, axis=0, keepdims=True)
        dxh = dh * g_ref[...]
        dx_ref[...] = dxo_ref[...] + r * (dxh - xh * jnp.mean(dxh * xh, axis=-1, keepdims=True))

    tile = pl.BlockSpec((tm, D), lambda i: (i, 0))
    vec = pl.BlockSpec((1, D), lambda i: (0, 0))
    ftile = pl.BlockSpec((tm, F), lambda i: (i, 0))
    return pl.pallas_call(
        body, name=name, grid=(T // tm,),
        in_specs=[tile, tile, vec, ftile, ftile, pl.BlockSpec((D, F), lambda i: (0, 0)), pl.BlockSpec((D, F), lambda i: (0, 1))],
        out_specs=[tile, vec],
        out_shape=[jax.ShapeDtypeStruct((T, D), F32), jax.ShapeDtypeStruct((1, D), F32)],
        compiler_params=_params("arbitrary"))(x, dxo, g, dzg, dzu, w_in, w_in)


def _chunks(T, fn):
    def step(c, carry):
        fn(pl.multiple_of(c * SEQ_CHUNK, SEQ_CHUNK))
        return carry
    lax.fori_loop(0, T // SEQ_CHUNK, step, 0)


def _conv_taps(win, ktaps, pad):
    return [(win if k == ktaps - 1 else pltpu.roll(win, ktaps - 1 - k, 0))[pad:, :] for k in range(ktaps)]


def _conv_taps_t(win, ktaps, pad):
    n = win.shape[0]
    return [(win if k == ktaps - 1 else pltpu.roll(win, n - (ktaps - 1 - k), 0))[:n - pad, :] for k in range(ktaps)]


def _col(T, W, idx):
    return pl.BlockSpec((T, W), lambda i, idx=idx: (0, idx))


def _full(shape):
    return pl.BlockSpec(shape, lambda i: (0,) * len(shape))


def mix_a_fwd(z, w, name):
    T, W = z.shape[0], w.shape[1]
    K, P = w.shape[0], 8

    def body(ab_ref, ac_ref, ax_ref, w_ref, y_ref, pp_ref):
        pp_ref[0:P, :] = jnp.zeros((P, W), F32)

        def chunk(s):
            rows = pl.ds(s, SEQ_CHUNK)
            pp_ref[pl.ds(s + P, SEQ_CHUNK), :] = ac_ref[rows, :] * ax_ref[rows, :]
            taps = _conv_taps(pp_ref[pl.ds(s, SEQ_CHUNK + P), :], K, P)
            q = sum(w_ref[k:k + 1, :] * taps[k] for k in range(K))
            y_ref[rows, :] = (ab_ref[rows, :] * q).astype(BF16)

        _chunks(T, chunk)

    return pl.pallas_call(
        body, name=name, grid=(1,),
        in_specs=[_col(T, W, 0), _col(T, W, 1), _col(T, W, 2), _full((K, W))],
        out_specs=_full((T, W)), out_shape=jax.ShapeDtypeStruct((T, W), BF16),
        scratch_shapes=[pltpu.VMEM((T + P, W), F32)],
        compiler_params=_params("arbitrary"))(z, z, z, w)


def mix_a_bwd(z, dy, w, name):
    T, W = z.shape[0], w.shape[1]
    K, P = w.shape[0], 8

    def body(ab_ref, ac_ref, ax_ref, dy_ref, w_ref, dab_ref, dac_ref, dax_ref, dw_ref, pp_ref, dq_ref):
        pp_ref[0:P, :] = jnp.zeros((P, W), F32)
        dq_ref[T:T + P, :] = jnp.zeros((P, W), F32)
        dw_ref[...] = jnp.zeros_like(dw_ref)

        def chunk1(s):
            rows = pl.ds(s, SEQ_CHUNK)
            pp_ref[pl.ds(s + P, SEQ_CHUNK), :] = ac_ref[rows, :] * ax_ref[rows, :]
            taps = _conv_taps(pp_ref[pl.ds(s, SEQ_CHUNK + P), :], K, P)
            q = sum(w_ref[k:k + 1, :] * taps[k] for k in range(K))
            dyv = dy_ref[rows, :]
            dab_ref[rows, :] = (dyv * q).astype(BF16)
            dq = dyv * ab_ref[rows, :]
            dq_ref[rows, :] = dq
            for k in range(K):
                dw_ref[k:k + 1, :] += jnp.sum(dq * taps[k], axis=0, keepdims=True)

        _chunks(T, chunk1)

        def chunk2(s):
            rows = pl.ds(s, SEQ_CHUNK)
            taps = _conv_taps_t(dq_ref[pl.ds(s, SEQ_CHUNK + P), :], K, P)
            dp = sum(w_ref[k:k + 1, :] * taps[k] for k in range(K))
            dac_ref[rows, :] = (dp * ax_ref[rows, :]).astype(BF16)
            dax_ref[rows, :] = (dp * ac_ref[rows, :]).astype(BF16)

        _chunks(T, chunk2)

    tw = jax.ShapeDtypeStruct((T, W), BF16)
    return pl.pallas_call(
        body, name=name, grid=(1,),
        in_specs=[_col(T, W, 0), _col(T, W, 1), _col(T, W, 2), _col(T, W, 0), _full((K, W))],
        out_specs=[_full((T, W))] * 3 + [_full((K, W))],
        out_shape=[tw, tw, tw, jax.ShapeDtypeStruct((K, W), F32)],
        scratch_shapes=[pltpu.VMEM((T + P, W), F32), pltpu.VMEM((T + P, W), F32)],
        compiler_params=_params("arbitrary"))(z, z, z, dy, w)


def _ln_stats(v):
    mu = jnp.mean(v, axis=-1, keepdims=True)
    xc = v - mu
    rstd = lax.rsqrt(jnp.mean(xc * xc, axis=-1, keepdims=True) + EPS)
    return xc * rstd, rstd


def _ln_bwd(dxh, xh, rstd):
    return rstd * (dxh - jnp.mean(dxh, axis=-1, keepdims=True) - xh * jnp.mean(dxh * xh, axis=-1, keepdims=True))


def _tril_bf16(w_ref, h):
    n = w_ref.shape[-1]
    keep = lax.broadcasted_iota(jnp.int32, (n, n), 0) >= lax.broadcasted_iota(jnp.int32, (n, n), 1)
    return jnp.where(keep, w_ref[h], 0.0).astype(BF16)


def mix_b_fwd(z, g, w_s, bias, name):
    T, W = z.shape[0], g.shape[1]
    H, C = w_s.shape[0], w_s.shape[1]
    hd = W // H

    def body(u_ref, v_ref, g_ref, w_ref, b_ref, y_ref):
        wts = [_tril_bf16(w_ref, h) for h in range(H)]
        head = lax.broadcasted_iota(jnp.int32, (C, W), 1) // hd

        def chunk(s):
            rows = pl.ds(s, C)
            xh, _ = _ln_stats(v_ref[rows, :])
            vn = (xh * g_ref[...]).astype(BF16)
            mixed = b_ref[...]
            for h in range(H):
                mixed = mixed + jnp.where(head == h, jnp.dot(wts[h], vn, preferred_element_type=F32), 0.0)
            y_ref[rows, :] = (u_ref[rows, :] * mixed).astype(BF16)

        _chunks(T, chunk)

    return pl.pallas_call(
        body, name=name, grid=(1,),
        in_specs=[_col(T, W, 3), _col(T, W, 4), _full((1, W)), _full((H, C, C)), _full((C, W))],
        out_specs=_full((T, W)), out_shape=jax.ShapeDtypeStruct((T, W), BF16),
        compiler_params=_params("arbitrary"))(z, z, g, w_s, bias)


def mix_b_bwd(z, dy, g, w_s, bias, name):
    T, W = z.shape[0], g.shape[1]
    H, C = w_s.shape[0], w_s.shape[1]
    hd = W // H

    def body(u_ref, v_ref, dy_ref, g_ref, w_ref, b_ref, du_ref, dv_ref, dw_ref, db_ref, dg_ref, dbf_ref):
        wts = [_tril_bf16(w_ref, h) for h in range(H)]
        head = lax.broadcasted_iota(jnp.int32, (C, W), 1) // hd
        dw_ref[...] = jnp.zeros_like(dw_ref)
        dg_ref[...] = jnp.zeros_like(dg_ref)
        dbf_ref[...] = jnp.zeros_like(dbf_ref)

        def chunk(s):
            rows = pl.ds(s, C)
            xh, rstd = _ln_stats(v_ref[rows, :])
            vn = (xh * g_ref[...]).astype(BF16)
            mixed = b_ref[...]
            for h in range(H):
                mixed = mixed + jnp.where(head == h, jnp.dot(wts[h], vn, preferred_element_type=F32), 0.0)
            dyv = dy_ref[rows, :]
            du_ref[rows, :] = (dyv * mixed).astype(BF16)
            dm = dyv * u_ref[rows, :]
            dbf_ref[...] += dm
            dvn = jnp.zeros((C, W), F32)
            for h in range(H):
                dmh = jnp.where(head == h, dm, 0.0).astype(BF16)
                dw_ref[h] += _nt(dmh, vn)
                dvn = dvn + _tn(wts[h], dmh)
            dg_ref[...] += jnp.sum(dvn * xh, axis=0, keepdims=True)
            dv_ref[rows, :] = _ln_bwd(dvn * g_ref[...], xh, rstd).astype(BF16)

        _chunks(T, chunk)

        keep = lax.broadcasted_iota(jnp.int32, (C, C), 0) >= lax.broadcasted_iota(jnp.int32, (C, C), 1)
        lane = lax.broadcasted_iota(jnp.int32, (C, 128), 1)
        db = jnp.zeros((C, 128), F32)
        dbf = dbf_ref[...]
        for h in range(H):
            dw_ref[h] = jnp.where(keep, dw_ref[h], 0.0)
            db = db + jnp.where(lane == h, jnp.sum(jnp.where(head == h, dbf, 0.0), axis=1, keepdims=True), 0.0)
        db_ref[...] = db

    tw = jax.ShapeDtypeStruct((T, W), BF16)
    return pl.pallas_call(
        body, name=name, grid=(1,),
        in_specs=[_col(T, W, 3), _col(T, W, 4), _col(T, W, 1), _full((1, W)), _full((H, C, C)), _full((C, W))],
        out_specs=[_full((T, W)), _full((T, W)), _full((H, C, C)), _full((C, 128)), _full((1, W))],
        out_shape=[tw, tw, jax.ShapeDtypeStruct((H, C, C), F32), jax.ShapeDtypeStruct((C, 128), F32),
                   jax.ShapeDtypeStruct((1, W), F32)],
        scratch_shapes=[pltpu.VMEM((C, W), F32)],
        compiler_params=_params("arbitrary"))(z, z, dy, g, w_s, bias)


def mix_c_fwd(z, w, ln_g, ln_b, name):
    T, W = z.shape[0], w.shape[1]
    K, P = w.shape[0], 32

    def body(a_ref, gt_ref, w_ref, g_ref, b_ref, y_ref, up_ref):
        up_ref[0:P, :] = jnp.zeros((P, W), F32)

        def chunk(s):
            rows = pl.ds(s, SEQ_CHUNK)
            up_ref[pl.ds(s + P, SEQ_CHUNK), :] = a_ref[rows, :] * _sig(gt_ref[rows, :])
            taps = _conv_taps(up_ref[pl.ds(s, SEQ_CHUNK + P), :], K, P)
            q = sum(w_ref[k:k + 1, :] * taps[k] for k in range(K))
            xh, _ = _ln_stats(q)
            r = xh * g_ref[...] + b_ref[...]
            y_ref[rows, :] = (r * _sig(r)).astype(BF16)

        _chunks(T, chunk)

    return pl.pallas_call(
        body, name=name, grid=(1,),
        in_specs=[_col(T, W, 5), _col(T, W, 6), _full((K, W)), _full((1, W)), _full((1, W))],
        out_specs=_full((T, W)), out_shape=jax.ShapeDtypeStruct((T, W), BF16),
        scratch_shapes=[pltpu.VMEM((T + P, W), F32)],
        compiler_params=_params("arbitrary"))(z, z, w, ln_g, ln_b)


def mix_c_bwd(z, dy, w, ln_g, ln_b, name):
    T, W = z.shape[0], w.shape[1]
    K, P = w.shape[0], 32

    def body(a_ref, gt_ref, dy_ref, w_ref, g_ref, b_ref, da_ref, dgt_ref, dw_ref, dg_ref, db_ref, up_ref, dq_ref):
        up_ref[0:P, :] = jnp.zeros((P, W), F32)
        dq_ref[T:T + P, :] = jnp.zeros((P, W), F32)
        dw_ref[...] = jnp.zeros_like(dw_ref)
        dg_ref[...] = jnp.zeros_like(dg_ref)
        db_ref[...] = jnp.zeros_like(db_ref)

        def chunk1(s):
            rows = pl.ds(s, SEQ_CHUNK)
            up_ref[pl.ds(s + P, SEQ_CHUNK), :] = a_ref[rows, :] * _sig(gt_ref[rows, :])
            taps = _conv_taps(up_ref[pl.ds(s, SEQ_CHUNK + P), :], K, P)
            q = sum(w_ref[k:k + 1, :] * taps[k] for k in range(K))
            xh, rstd = _ln_stats(q)
            r = xh * g_ref[...] + b_ref[...]
            sr = _sig(r)
            dr = dy_ref[rows, :] * (sr * (1.0 + r * (1.0 - sr)))
            db_ref[...] += jnp.sum(dr, axis=0, keepdims=True)
            dg_ref[...] += jnp.sum(dr * xh, axis=0, keepdims=True)
            dq = _ln_bwd(dr * g_ref[...], xh, rstd)
            dq_ref[rows, :] = dq
            for k in range(K):
                dw_ref[k:k + 1, :] += jnp.sum(dq * taps[k], axis=0, keepdims=True)

        _chunks(T, chunk1)

        def chunk2(s):
            rows = pl.ds(s, SEQ_CHUNK)
            taps = _conv_taps_t(dq_ref[pl.ds(s, SEQ_CHUNK + P), :], K, P)
            du = sum(w_ref[k:k + 1, :] * taps[k] for k in range(K))
            sg = _sig(gt_ref[rows, :])
            da_ref[rows, :] = (du * sg).astype(BF16)
            dgt_ref[rows, :] = (du * a_ref[rows, :] * sg * (1.0 - sg)).astype(BF16)

        _chunks(T, chunk2)

    tw = jax.ShapeDtypeStruct((T, W), BF16)
    vec = jax.ShapeDtypeStruct((1, W), F32)
    return pl.pallas_call(
        body, name=name, grid=(1,),
        in_specs=[_col(T, W, 5), _col(T, W, 6), _col(T, W, 2), _full((K, W)), _full((1, W)), _full((1, W))],
        out_specs=[_full((T, W)), _full((T, W)), _full((K, W)), _full((1, W)), _full((1, W))],
        out_shape=[tw, tw, jax.ShapeDtypeStruct((K, W), F32), vec, vec],
        scratch_shapes=[pltpu.VMEM((T + P, W), F32), pltpu.VMEM((T + P, W), F32)],
        compiler_params=_params("arbitrary"))(z, z, dy, w, ln_g, ln_b)


def _pool_select(levels, W, rows):
    group = lax.broadcasted_iota(jnp.int32, (rows, W), 1) // (W // len(POOL_WINDOWS))
    out = levels[-1]
    for gi in range(len(POOL_WINDOWS) - 2, -1, -1):
        out = jnp.where(group == gi, levels[gi], out)
    return out


def _pool_count(s, W):
    t = s + lax.broadcasted_iota(jnp.int32, (SEQ_CHUNK, W), 0)
    group = lax.broadcasted_iota(jnp.int32, (SEQ_CHUNK, W), 1) // (W // len(POOL_WINDOWS))
    win = jnp.full((SEQ_CHUNK, W), POOL_WINDOWS[-1], jnp.int32)
    for gi in range(len(POOL_WINDOWS) - 2, -1, -1):
        win = jnp.where(group == gi, POOL_WINDOWS[gi], win)
    return jnp.minimum(t + 1, win).astype(F32)


def _pooled(wp_ref, s, W, P):
    win = wp_ref[pl.ds(s, SEQ_CHUNK + P), :]
    levels, acc, shift = [], win, 1
    for _ in POOL_WINDOWS:
        acc = acc + pltpu.roll(acc, shift, 0)
        levels.append(acc[P:, :])
        shift *= 2
    return _pool_select(levels, W, SEQ_CHUNK) / _pool_count(s, W) - win[P:, :]


def mix_d_fwd(z, pbd, scale, name):
    T, W = z.shape[0], scale.shape[1]
    P = 16

    def body(x_ref, p_ref, s_ref, y_ref, wp_ref):
        wp_ref[0:P, :] = jnp.zeros((P, W), F32)

        def chunk(s):
            rows = pl.ds(s, SEQ_CHUNK)
            wp_ref[pl.ds(s + P, SEQ_CHUNK), :] = x_ref[rows, :]
            pooled = _pooled(wp_ref, s, W, P).astype(BF16)
            y_ref[rows, :] = (jnp.dot(pooled, p_ref[...], preferred_element_type=F32) * s_ref[...]).astype(BF16)

        _chunks(T, chunk)

    return pl.pallas_call(
        body, name=name, grid=(1,),
        in_specs=[_col(T, W, 7), _full((W, W)), _full((1, W))],
        out_specs=_full((T, W)), out_shape=jax.ShapeDtypeStruct((T, W), BF16),
        scratch_shapes=[pltpu.VMEM((T + P, W), F32)],
        compiler_params=_params("arbitrary"))(z, pbd, scale)


def mix_d_bwd(z, dy, pbd, scale, name):
    T, W = z.shape[0], scale.shape[1]
    P = 16

    def body(x_ref, dy_ref, p_ref, s_ref, dx_ref, dp_ref, ds_ref, wp_ref, e_ref, dpool_ref):
        wp_ref[0:P, :] = jnp.zeros((P, W), F32)
        e_ref[T:T + P, :] = jnp.zeros((P, W), F32)
        dp_ref[...] = jnp.zeros_like(dp_ref)
        ds_ref[...] = jnp.zeros_like(ds_ref)

        def chunk1(s):
            rows = pl.ds(s, SEQ_CHUNK)
            wp_ref[pl.ds(s + P, SEQ_CHUNK), :] = x_ref[rows, :]
            pooled = _pooled(wp_ref, s, W, P).astype(BF16)
            yl = jnp.dot(pooled, p_ref[...], preferred_element_type=F32)
            dyv = dy_ref[rows, :]
            ds_ref[...] += jnp.sum(dyv * yl, axis=0, keepdims=True)
            dyl = (dyv * s_ref[...]).astype(BF16)
            dp_ref[...] += _tn(pooled, dyl)
            dpool = _nt(dyl, p_ref[...])
            dpool_ref[rows, :] = dpool
            e_ref[rows, :] = dpool / _pool_count(s, W)

        _chunks(T, chunk1)

        def chunk2(s):
            rows = pl.ds(s, SEQ_CHUNK)
            win = e_ref[pl.ds(s, SEQ_CHUNK + P), :]
            n = SEQ_CHUNK + P
            levels, acc, shift = [], win, 1
            for _ in POOL_WINDOWS:
                acc = acc + pltpu.roll(acc, n - shift, 0)
                levels.append(acc[:SEQ_CHUNK, :])
                shift *= 2
            dx_ref[rows, :] = (_pool_select(levels, W, SEQ_CHUNK) - dpool_ref[rows, :]).astype(BF16)

        _chunks(T, chunk2)

    return pl.pallas_call(
        body, name=name, grid=(1,),
        in_specs=[_col(T, W, 7), _col(T, W, 3), _full((W, W)), _full((1, W))],
        out_specs=[_full((T, W)), _full((W, W)), _full((1, W))],
        out_shape=[jax.ShapeDtypeStruct((T, W), BF16), jax.ShapeDtypeStruct((W, W), F32),
                   jax.ShapeDtypeStruct((1, W), F32)],
        scratch_shapes=[pltpu.VMEM((T + P, W), F32), pltpu.VMEM((T + P, W), F32), pltpu.VMEM((T, W), F32)],
        compiler_params=_params("arbitrary"))(z, dy, pbd, scale)


def attn_fwd(q, kv, name):
    T, D = q.shape
    M = kv.shape[0]
    hd = D // N_HEADS
    tm = _pick(T, 512, 8)
    sc = 1.0 / math.sqrt(hd)

    def body(q_ref, k_ref, v_ref, o_ref):
        for h in range(N_HEADS):
            cols = slice(h * hd, (h + 1) * hd)
            s = _nt(q_ref[:, cols].astype(BF16), k_ref[:, cols].astype(BF16)) * sc
            p = jnp.exp(s - jnp.max(s, axis=-1, keepdims=True))
            p = p / jnp.sum(p, axis=-1, keepdims=True)
            o_ref[:, cols] = jnp.dot(p.astype(BF16), v_ref[:, cols].astype(BF16),
                                     preferred_element_type=F32).astype(BF16)

    return pl.pallas_call(
        body, name=name, grid=(T // tm,),
        in_specs=[pl.BlockSpec((tm, D), lambda i: (i, 0)), pl.BlockSpec((M, D), lambda i: (0, 0)),
                  pl.BlockSpec((M, D), lambda i: (0, 1))],
        out_specs=pl.BlockSpec((tm, D), lambda i: (i, 0)),
        out_shape=jax.ShapeDtypeStruct((T, D), BF16),
        compiler_params=_params("parallel"))(q, kv, kv)


def attn_bwd(q, kv, do, name):
    T, D = q.shape
    M = kv.shape[0]
    hd = D // N_HEADS
    tm = _pick(T, 512, 8)
    sc = 1.0 / math.sqrt(hd)

    def body(q_ref, k_ref, v_ref, do_ref, dq_ref, dk_ref, dv_ref):
        i = pl.program_id(0)

        @pl.when(i == 0)
        def _():
            dk_ref[...] = jnp.zeros_like(dk_ref)
            dv_ref[...] = jnp.zeros_like(dv_ref)

        for h in range(N_HEADS):
            cols = slice(h * hd, (h + 1) * hd)
            qh, kh = q_ref[:, cols].astype(BF16), k_ref[:, cols].astype(BF16)
            vh, doh = v_ref[:, cols].astype(BF16), do_ref[:, cols].astype(BF16)
            s = _nt(qh, kh) * sc
            p = jnp.exp(s - jnp.max(s, axis=-1, keepdims=True))
            p = p / jnp.sum(p, axis=-1, keepdims=True)
            dp = _nt(doh, vh)
            dv_ref[:, cols] += _tn(p.astype(BF16), doh)
            ds = (p * (dp - jnp.sum(dp * p, axis=-1, keepdims=True)) * sc).astype(BF16)
            dq_ref[:, cols] = jnp.dot(ds, kh, preferred_element_type=F32).astype(BF16)
            dk_ref[:, cols] += _tn(ds, qh)

    tile = pl.BlockSpec((tm, D), lambda i: (i, 0))
    mem = jax.ShapeDtypeStruct((M, D), F32)
    return pl.pallas_call(
        body, name=name, grid=(T // tm,),
        in_specs=[tile, pl.BlockSpec((M, D), lambda i: (0, 0)), pl.BlockSpec((M, D), lambda i: (0, 1)), tile],
        out_specs=[tile, pl.BlockSpec((M, D), lambda i: (0, 0)), pl.BlockSpec((M, D), lambda i: (0, 0))],
        out_shape=[jax.ShapeDtypeStruct((T, D), BF16), mem, mem],
        compiler_params=_params("arbitrary"))(q, kv, kv, do)


def loss_head(x, g, target, name):
    T, D = x.shape
    tm = _pick(T, 512, 8)

    def body(x_ref, g_ref, t_ref, l_ref, dx_ref, dg_ref):
        i = pl.program_id(0)

        @pl.when(i == 0)
        def _():
            l_ref[...] = jnp.zeros_like(l_ref)
            dg_ref[...] = jnp.zeros_like(dg_ref)

        xv = x_ref[...]
        r = lax.rsqrt(jnp.mean(xv * xv, axis=-1, keepdims=True) + EPS)
        xh = xv * r
        err = xh * g_ref[...] - t_ref[...]
        l_ref[...] += 0.5 * jnp.sum(jnp.mean(err * err, axis=-1, keepdims=True), axis=0, keepdims=True)
        dy = err * (1.0 / D)
        dg_ref[...] += jnp.sum(dy * xh, axis=0, keepdims=True)
        dxh = dy * g_ref[...]
        dx_ref[...] = r * (dxh - xh * jnp.mean(dxh * xh, axis=-1, keepdims=True))

    tile = pl.BlockSpec((tm, D), lambda i: (i, 0))
    vec = pl.BlockSpec((1, D), lambda i: (0, 0))
    return pl.pallas_call(
        body, name=name, grid=(T // tm,),
        in_specs=[tile, vec, tile],
        out_specs=[pl.BlockSpec((1, 128), lambda i: (0, 0)), tile, vec],
        out_shape=[jax.ShapeDtypeStruct((1, 128), F32), jax.ShapeDtypeStruct((T, D), F32),
                   jax.ShapeDtypeStruct((1, D), F32)],
        compiler_params=_params("arbitrary"))(x, g, target)


def _block_diag(p):
    G, gd, _ = p.shape
    rows = [jnp.concatenate([p[g] if g == c else jnp.zeros((gd, gd), p.dtype) for c in range(G)], axis=1)
            for g in range(G)]
    return jnp.concatenate(rows, axis=0)


class _LazyWeight:
    def __init__(self, fetch, name, latest):
        self.fetch, self.name, self.latest = fetch, name, latest

    def __getitem__(self, l):
        return self.fetch(self.name, l, self.latest[0])


def _local_step(x, mem, target, fetch, ws, L, progress=lambda event, l, grads, dx: None):
    T, D = x.shape
    W = D // 4
    H = ws["sgu_w"].shape[1]
    row = lambda v: v.reshape(1, -1)
    latest = [x]
    wb = {n: _LazyWeight(fetch, n, latest) for n in BIG}
    saved = []
    for l in range(L):
        s = {"x0": x}
        latest[0] = x
        x = ffn_fwd(x, row(ws["norm_ffn1"][l]), wb["ffn1_w_in"][l], wb["ffn1_w_out"][l], "ffn_fwd")
        s["x1"] = x
        latest[0] = x
        z, s["h_mix"] = norm_matmul(x, row(ws["norm_mix"][l]), wb["mix_w_in"][l], "mix_in")
        s["z"] = z
        s["bias"] = jnp.repeat(ws["sgu_b"][l].T, W // H, axis=1)
        s["pbd"] = _block_diag(ws["pool_w"][l]).astype(BF16)
        y = jnp.concatenate([
            mix_a_fwd(z, ws["sconv_w"][l], "mix_a_fwd"),
            mix_b_fwd(z, row(ws["sgu_norm_g"][l]), ws["sgu_w"][l], s["bias"], "mix_b_fwd"),
            mix_c_fwd(z, ws["cconv_w"][l], row(ws["cconv_ln_g"][l]), row(ws["cconv_ln_b"][l]), "mix_c_fwd"),
            mix_d_fwd(z, s["pbd"], row(ws["pool_scale"][l]), "mix_d_fwd")], axis=1)
        s["y"] = y
        x = matmul_res(x, y, wb["mix_w_out"][l], "mix_out")
        s["x2"] = x
        s["q"], s["hq"] = norm_matmul(x, row(ws["norm_xattn"][l]), wb["xattn_wq"][l], "attn_q")
        s["kv"], s["mn"] = norm_matmul(mem, row(ws["norm_mem"][l]), wb["xattn_wkv"][l], "attn_kv")
        s["o"] = attn_fwd(s["q"], s["kv"], "attn_fwd")
        x = matmul_res(x, s["o"], wb["xattn_wo"][l], "attn_out")
        s["x3"] = x
        x = ffn_fwd(x, row(ws["norm_ffn2"][l]), wb["ffn2_w_in"][l], wb["ffn2_w_out"][l], "ffn_fwd")
        saved.append(s)

    loss, dx, dg_final = loss_head(x, row(ws["norm_final"]), target, "loss_head")
    grads = {n: [None] * L for n in WEIGHTS if n != "norm_final"}
    grads["norm_final"] = dg_final.reshape(-1)

    def ffn_back(xin, dxo, gname, win, wout, l):
        h, a, dzg, dzu = ffn_dz(xin, dxo, row(ws[gname][l]), wb[win][l], wb[wout][l], "ffn_dz")
        dxn, dg = ffn_dh(xin, dxo, row(ws[gname][l]), dzg, dzu, wb[win][l], "ffn_dh")
        grads[gname][l] = dg.reshape(-1)
        grads[win][l] = jnp.concatenate([matmul_tn(h, dzg, 1.0, "ffn_dwin"), matmul_tn(h, dzu, 1.0, "ffn_dwin")], axis=1)
        grads[wout][l] = matmul_tn(a, dxo, 0.5, "ffn_dwout")
        return dxn

    for l in reversed(range(L)):
        s = saved[l]
        dx = ffn_back(s["x3"], dx, "norm_ffn2", "ffn2_w_in", "ffn2_w_out", l)
        progress("ffn2", l, grads, dx)
        grads["xattn_wo"][l] = matmul_tn(s["o"], dx, 1.0, "dw_sq")
        do = matmul_nt(dx, wb["xattn_wo"][l], "attn_do")
        dq, dk, dv = attn_bwd(s["q"], s["kv"], do, "attn_bwd")
        grads["xattn_wq"][l] = matmul_tn(s["hq"], dq, 1.0, "dw_sq")
        dhq = matmul_nt(dq, wb["xattn_wq"][l], "attn_dhq")
        dx, dg = rmsnorm_bwd(dx, dhq, s["x2"], row(ws["norm_xattn"][l]), "norm_bwd")
        grads["norm_xattn"][l] = dg.reshape(-1)
        dkv = jnp.concatenate([dk, dv], axis=1)
        grads["xattn_wkv"][l] = matmul_tn(s["mn"], dkv, 1.0, "attn_dwkv")
        dmn = matmul_nt(dkv, wb["xattn_wkv"][l], "attn_dmn")
        _, dg = rmsnorm_bwd(None, dmn, mem, row(ws["norm_mem"][l]), "norm_mem_bwd")
        grads["norm_mem"][l] = dg.reshape(-1)
        progress("attn", l, grads, dx)
        grads["mix_w_out"][l] = matmul_tn(s["y"], dx, 1.0, "dw_sq")
        dy = matmul_nt(dx, wb["mix_w_out"][l], "mix_dy")
        z = s["z"]
        dab, dac, dax, dws = mix_a_bwd(z, dy, ws["sconv_w"][l], "mix_a_bwd")
        dbu, dbv, dwsgu, dbs, dgs = mix_b_bwd(z, dy, row(ws["sgu_norm_g"][l]), ws["sgu_w"][l], s["bias"], "mix_b_bwd")
        dca, dcg, dwc, dgc, dbc = mix_c_bwd(z, dy, ws["cconv_w"][l], row(ws["cconv_ln_g"][l]),
                                            row(ws["cconv_ln_b"][l]), "mix_c_bwd")
        ddw, dpbd, dsc = mix_d_bwd(z, dy, s["pbd"], row(ws["pool_scale"][l]), "mix_d_bwd")
        grads["sconv_w"][l], grads["cconv_w"][l] = dws, dwc
        grads["sgu_w"][l], grads["sgu_b"][l], grads["sgu_norm_g"][l] = dwsgu, dbs[:, :H].T, dgs.reshape(-1)
        grads["cconv_ln_g"][l], grads["cconv_ln_b"][l] = dgc.reshape(-1), dbc.reshape(-1)
        gd = W // len(POOL_WINDOWS)
        grads["pool_w"][l] = jnp.stack([dpbd[g * gd:(g + 1) * gd, g * gd:(g + 1) * gd] for g in range(len(POOL_WINDOWS))])
        grads["pool_scale"][l] = dsc.reshape(-1)
        dz = jnp.concatenate([dab, dac, dax, dbu, dbv, dca, dcg, ddw], axis=1)
        grads["mix_w_in"][l] = matmul_tn(s["h_mix"], dz, 1.0, "mix_dwin")
        dh = matmul_nt(dz, wb["mix_w_in"][l], "mix_dh")
        dx, dg = rmsnorm_bwd(dx, dh, s["x1"], row(ws["norm_mix"][l]), "norm_bwd")
        grads["norm_mix"][l] = dg.reshape(-1)
        progress("mix", l, grads, dx)
        dx = ffn_back(s["x0"], dx, "norm_ffn1", "ffn1_w_in", "ffn1_w_out", l)
        progress("layer", l, grads, dx)

    return loss[0, 0], dx, grads


ANY = pl.BlockSpec(memory_space=pl.ANY)


def _other_chips(x, y):
    return [(1 - x, y), (x, 1 - y), (1 - x, 1 - y)]


def _shard_slice(ref, axis, chip, size):
    idx = [slice(None)] * len(ref.shape)
    idx[axis] = pl.ds(pl.multiple_of(chip * size, size), size)
    return ref.at[tuple(idx)]


def all_gather_chips(shards, axes, name):
    n = len(shards)

    def body(*refs):
        ins, outs = refs[:n], refs[n:2 * n]
        send, recv, loc = refs[2 * n:]
        x, y, c = lax.axis_index("x"), lax.axis_index("y"), lax.axis_index("c")
        me = 2 * x + y
        chips = _other_chips(x, y)
        started = []
        for i in range(n):
            size = ins[i].shape[axes[i]]
            cp = pltpu.make_async_copy(ins[i], _shard_slice(outs[i], axes[i], me, size), loc.at[i])
            cp.start()
            started.append(cp)
        sends = []
        for i in range(n):
            size = ins[i].shape[axes[i]]
            for j, (px, py) in enumerate(chips):
                cp = pltpu.make_async_remote_copy(
                    src_ref=ins[i], dst_ref=_shard_slice(outs[i], axes[i], me, size),
                    send_sem=send.at[i, j], recv_sem=recv.at[i, j], device_id=(px, py, c), device_id_type=MESH_ID)
                cp.start()
                sends.append(cp)
        for i in range(n):
            size = ins[i].shape[axes[i]]
            for j, (px, py) in enumerate(chips):
                pltpu.make_async_remote_copy(
                    src_ref=ins[i], dst_ref=_shard_slice(outs[i], axes[i], 2 * px + py, size),
                    send_sem=send.at[i, j], recv_sem=recv.at[i, j], device_id=(px, py, c),
                    device_id_type=MESH_ID).wait_recv()
        for cp in sends:
            cp.wait_send()
        for cp in started:
            cp.wait()

    def full(a, ax):
        shape = list(a.shape)
        shape[ax] *= N_CHIPS
        return jax.ShapeDtypeStruct(tuple(shape), a.dtype)

    return pl.pallas_call(
        body, name=name, in_specs=[ANY] * n, out_specs=[ANY] * n,
        out_shape=[full(a, ax) for a, ax in zip(shards, axes)],
        scratch_shapes=[pltpu.SemaphoreType.DMA((n, 3)), pltpu.SemaphoreType.DMA((n, 3)),
                        pltpu.SemaphoreType.DMA((n,))],
        compiler_params=pltpu.CompilerParams(has_side_effects=True))(*shards)


def cast_into_slot(shard, axis, chip, name):
    L, K, N = shard.shape
    bm = _pick(K, 256, 16)
    full = (K * N_CHIPS, N) if axis == 1 else (K, N * N_CHIPS)
    nb = K // bm

    def body(c_ref, s_ref, *o_refs):
        for l in range(L):
            o_refs[l][...] = s_ref[l].astype(BF16)

    out_map = (lambda i, c: (c[0] * nb + i, 0)) if axis == 1 else (lambda i, c: (i, c[0]))
    spec = pltpu.PrefetchScalarGridSpec(
        num_scalar_prefetch=1, grid=(nb,),
        in_specs=[pl.BlockSpec((L, bm, N), lambda i, c: (0, i, 0))],
        out_specs=[pl.BlockSpec((bm, N), out_map)] * L)
    return pl.pallas_call(body, name=name, grid_spec=spec, out_shape=[jax.ShapeDtypeStruct(full, BF16)] * L,
                          compiler_params=_params("parallel"))(chip, shard)


HBM = pl.BlockSpec(memory_space=pltpu.HBM)
SEM = pl.BlockSpec(memory_space=pltpu.SEMAPHORE)
DATAFLOW = pltpu.SideEffectType.DATAFLOW_SIDE_EFFECTING


def split_start(name, sources, landing, n_sems, make, after):
    ns, nl = len(sources), len(landing)

    def body(*refs):
        out, _ = make(refs[:ns], refs[ns:ns + nl], refs[ns + nl + 1], refs[ns + nl + 2])
        for cp in out:
            cp.start()
        refs[-1][...] = jnp.zeros_like(refs[-1])

    hbm = lambda b: pltpu.with_memory_space_constraint(b, pltpu.HBM)
    res = pl.pallas_call(
        body, name=name,
        out_shape=(pltpu.SemaphoreType.DMA((n_sems,)), pltpu.SemaphoreType.DMA((n_sems,)),
                   *[pltpu.HBM(b.shape, b.dtype) for b in landing], jax.ShapeDtypeStruct((8, 128), F32)),
        in_specs=[HBM] * (ns + nl) + [ANY], out_specs=(SEM, SEM, *[HBM] * nl, pl.BlockSpec(memory_space=pltpu.VMEM)),
        input_output_aliases={ns + i: 2 + i for i in range(nl)},
        compiler_params=pltpu.CompilerParams(has_side_effects=DATAFLOW))(
            *[hbm(b) for b in sources], *[hbm(b) for b in landing], after)
    return res[0], res[1], list(res[2:2 + nl]), res[-1]


def split_wait(name, sources, landing, send, recv, make, after):
    ns, nl = len(sources), len(landing)

    def body(*refs):
        _, back = make(refs[:ns], refs[ns:ns + nl], refs[ns + nl], refs[ns + nl + 1])
        for cp in back:
            cp.wait_send()
            cp.wait_recv()

    return list(pl.pallas_call(
        body, name=name, out_shape=tuple(pltpu.HBM(b.shape, b.dtype) for b in landing),
        in_specs=[HBM] * (ns + nl) + [SEM, SEM, ANY], out_specs=tuple([HBM] * nl),
        input_output_aliases={ns + i: i for i in range(nl)},
        compiler_params=pltpu.CompilerParams(has_side_effects=DATAFLOW))(
            *[pltpu.with_memory_space_constraint(b, pltpu.HBM) for b in sources], *landing, send, recv, after))


def _half_slot(buf, axis, chip, half):
    K, N = buf.shape
    if axis == 1:
        n = N // N_CHIPS
        return buf.at[pl.ds(pl.multiple_of(half * (K // 2), K // 2), K // 2), pl.ds(pl.multiple_of(chip * n, n), n)]
    k2 = K // N_CHIPS // 2
    return buf.at[pl.ds(pl.multiple_of((2 * chip + half) * k2, k2), k2), :]


def gather_copies(axes):
    def make(_, bufs, send, recv):
        x, y, c = lax.axis_index("x"), lax.axis_index("y"), lax.axis_index("c")
        out, back = [], []
        for i, (buf, ax) in enumerate(zip(bufs, axes)):
            mine = _half_slot(buf, ax, 2 * x + y, c)
            for j, (px, py) in enumerate(_other_chips(x, y)):
                kw = dict(send_sem=send.at[3 * i + j], recv_sem=recv.at[3 * i + j], device_id=(px, py, c),
                          device_id_type=MESH_ID)
                out.append(pltpu.make_async_remote_copy(src_ref=mine, dst_ref=mine, **kw))
                back.append(pltpu.make_async_remote_copy(src_ref=mine, dst_ref=_half_slot(buf, ax, 2 * px + py, c), **kw))
        return out, back
    return make


def forward_sibling(bufs, axes, name):
    n = len(bufs)

    def body(*refs):
        ins = refs[:n]
        send, recv = refs[2 * n:]
        x, y, c = lax.axis_index("x"), lax.axis_index("y"), lax.axis_index("c")
        out, back = [], []
        for i, ax in enumerate(axes):
            for j, (px, py) in enumerate(_other_chips(x, y)):
                have = _half_slot(ins[i], ax, 2 * px + py, c)
                kw = dict(send_sem=send.at[3 * i + j], recv_sem=recv.at[3 * i + j], device_id=(x, y, 1 - c),
                          device_id_type=MESH_ID)
                out.append(pltpu.make_async_remote_copy(src_ref=have, dst_ref=have, **kw))
                back.append(pltpu.make_async_remote_copy(src_ref=have, dst_ref=_half_slot(ins[i], ax, 2 * px + py, 1 - c), **kw))
        for cp in out:
            cp.start()
        for cp in back:
            cp.wait_recv()
        for cp in out:
            cp.wait_send()

    return pl.pallas_call(
        body, name=name, in_specs=[ANY] * n, out_specs=[ANY] * n,
        out_shape=[jax.ShapeDtypeStruct(b.shape, b.dtype) for b in bufs],
        input_output_aliases={i: i for i in range(n)},
        scratch_shapes=[pltpu.SemaphoreType.DMA((3 * n,)), pltpu.SemaphoreType.DMA((3 * n,))],
        compiler_params=pltpu.CompilerParams(has_side_effects=True))(*bufs)


def all_reduce_small(p, name):
    R = p.shape[0]

    def body(p_ref, o_ref, sib_ref, chip_ref, send, recv):
        x, y, c = lax.axis_index("x"), lax.axis_index("y"), lax.axis_index("c")
        me = 2 * x + y
        chips = _other_chips(x, y)
        pair = pltpu.make_async_remote_copy(src_ref=p_ref, dst_ref=sib_ref, send_sem=send.at[0], recv_sem=recv.at[0],
                                            device_id=(x, y, 1 - c), device_id_type=MESH_ID)
        pair.start()
        pair.wait()
        chip_ref[me] = p_ref[...] + sib_ref[...]
        sends = []
        for j, (px, py) in enumerate(chips):
            cp = pltpu.make_async_remote_copy(src_ref=chip_ref.at[me], dst_ref=chip_ref.at[me], send_sem=send.at[1 + j],
                                              recv_sem=recv.at[1 + j], device_id=(px, py, c), device_id_type=MESH_ID)
            cp.start()
            sends.append(cp)
        for j, (px, py) in enumerate(chips):
            pltpu.make_async_remote_copy(src_ref=chip_ref.at[me], dst_ref=chip_ref.at[2 * px + py], send_sem=send.at[1 + j],
                                         recv_sem=recv.at[1 + j], device_id=(px, py, c), device_id_type=MESH_ID).wait_recv()
        for cp in sends:
            cp.wait_send()
        o_ref[...] = ((chip_ref[0] + chip_ref[1]) + chip_ref[2]) + chip_ref[3]

    vm = pl.BlockSpec(memory_space=pltpu.VMEM)
    return pl.pallas_call(
        body, name=name, in_specs=[vm], out_specs=vm, out_shape=jax.ShapeDtypeStruct((R, 128), F32),
        scratch_shapes=[pltpu.VMEM((R, 128), F32), pltpu.VMEM((N_CHIPS, R, 128), F32),
                        pltpu.SemaphoreType.DMA((4,)), pltpu.SemaphoreType.DMA((4,))],
        compiler_params=pltpu.CompilerParams(has_side_effects=True, vmem_limit_bytes=VMEM_LIMIT))(p)


def _grad_view(g, axis):
    K, N = g.shape
    return g.reshape(1, 2, K // 2, N) if axis == 1 else g.reshape(N_CHIPS, 2, K // N_CHIPS // 2, N)


def pair_copies(gvs, others, send, recv):
    x, y, c = lax.axis_index("x"), lax.axis_index("y"), lax.axis_index("c")
    out = [pltpu.make_async_remote_copy(src_ref=gv.at[:, 1 - c], dst_ref=o, send_sem=send.at[i], recv_sem=recv.at[i],
                                        device_id=(x, y, 1 - c), device_id_type=MESH_ID)
           for i, (gv, o) in enumerate(zip(gvs, others))]
    return out, out


def chip_copies(axes):
    def piece(s, ax, chip):
        if ax == 1:
            n = s.shape[2] // N_CHIPS
            return s.at[0, :, pl.ds(pl.multiple_of(chip * n, n), n)]
        return s.at[chip]

    def make(sums, qs, send, recv):
        x, y, c = lax.axis_index("x"), lax.axis_index("y"), lax.axis_index("c")
        out = []
        for i, (s, q, ax) in enumerate(zip(sums, qs, axes)):
            for j, (px, py) in enumerate(_other_chips(x, y)):
                out.append(pltpu.make_async_remote_copy(
                    src_ref=piece(s, ax, 2 * px + py), dst_ref=q.at[j], send_sem=send.at[3 * i + j],
                    recv_sem=recv.at[3 * i + j], device_id=(px, py, c), device_id_type=MESH_ID))
        return out, out
    return make


def share_sibling(halves, name):
    n = len(halves)

    def body(*refs):
        ins = refs[:n]
        send, recv = refs[2 * n:]
        x, y, c = lax.axis_index("x"), lax.axis_index("y"), lax.axis_index("c")
        cps = [pltpu.make_async_remote_copy(src_ref=ins[i].at[c], dst_ref=ins[i].at[c], send_sem=send.at[i], recv_sem=recv.at[i],
                                            device_id=(x, y, 1 - c), device_id_type=MESH_ID) for i in range(n)]
        for cp in cps:
            cp.start()
        for i in range(n):
            pltpu.make_async_remote_copy(src_ref=ins[i].at[c], dst_ref=ins[i].at[1 - c], send_sem=send.at[i], recv_sem=recv.at[i],
                                         device_id=(x, y, 1 - c), device_id_type=MESH_ID).wait_recv()
        for cp in cps:
            cp.wait_send()

    return pl.pallas_call(
        body, name=name, in_specs=[ANY] * n, out_specs=[ANY] * n,
        out_shape=[jax.ShapeDtypeStruct(a.shape, a.dtype) for a in halves],
        input_output_aliases={i: i for i in range(n)},
        scratch_shapes=[pltpu.SemaphoreType.DMA((n,)), pltpu.SemaphoreType.DMA((n,))],
        compiler_params=pltpu.CompilerParams(has_side_effects=True))(*halves)


def add_pair(gv, other, place, name):
    A, _, rows, N = gv.shape
    bm, bn = _pick(rows, 256, 16), _pick(N, 1408, 128)

    def body(p_ref, g_ref, o_ref, out_ref):
        out_ref[...] = (g_ref[...] + o_ref[...]).astype(GRAD_WIRE)

    spec = pltpu.PrefetchScalarGridSpec(
        num_scalar_prefetch=1, grid=(A, rows // bm, N // bn),
        in_specs=[pl.BlockSpec((None, None, bm, bn), lambda a, i, j, p: (a, p[1], i, j)),
                  pl.BlockSpec((None, bm, bn), lambda a, i, j, p: (a, i, j))],
        out_specs=pl.BlockSpec((None, bm, bn), lambda a, i, j, p: (a, i, j)))
    return pl.pallas_call(body, name=name, grid_spec=spec, out_shape=jax.ShapeDtypeStruct((A, rows, N), GRAD_WIRE),
                          compiler_params=_params("parallel", "parallel", "parallel"))(place, gv, other)


def add_chips(s, q, axis, place, name):
    _, rows, n = q.shape
    bm, bn = _pick(rows, 256, 16), _pick(n, 1408, 128)
    nbj = n // bn

    def body(p_ref, s_ref, q_ref, o_ref):
        o_ref[...] = ((s_ref[...].astype(F32) + q_ref[0].astype(F32)) + q_ref[1].astype(F32)) + q_ref[2].astype(F32)

    mine = (lambda i, j, p: (p[0], i, j)) if axis == 0 else (lambda i, j, p: (0, i, p[0] * nbj + j))
    spec = pltpu.PrefetchScalarGridSpec(
        num_scalar_prefetch=1, grid=(rows // bm, nbj),
        in_specs=[pl.BlockSpec((None, bm, bn), mine), pl.BlockSpec((3, bm, bn), lambda i, j, p: (0, i, j))],
        out_specs=pl.BlockSpec((None, bm, bn), lambda i, j, p: (p[1], i, j)))
    return pl.pallas_call(body, name=name, grid_spec=spec, out_shape=jax.ShapeDtypeStruct((2, rows, n), F32),
                          compiler_params=_params("parallel", "parallel"))(place, s, q)


def adamw(w, g, m, v, name):
    R, N = w.shape
    bm = _pick(R, 256, 8)
    c1 = 1.0 / (1.0 - ADAM_B1 ** ADAM_STEP)
    c2 = 1.0 / (1.0 - ADAM_B2 ** ADAM_STEP)

    def body(w_ref, g_ref, m_ref, v_ref, d_ref, nm_ref, nv_ref):
        gv = g_ref[...]
        nm = ADAM_B1 * m_ref[...] + (1.0 - ADAM_B1) * gv
        nv = ADAM_B2 * v_ref[...] + (1.0 - ADAM_B2) * (gv * gv)
        nm_ref[...] = nm
        nv_ref[...] = nv
        d_ref[...] = -ADAM_LR * ((nm * c1) / (jnp.sqrt(nv * c2) + ADAM_EPS) + ADAM_WD * w_ref[...])

    blk = pl.BlockSpec((bm, N), lambda i: (i, 0))
    out = jax.ShapeDtypeStruct((R, N), F32)
    return pl.pallas_call(body, name=name, grid=(R // bm,), in_specs=[blk] * 4, out_specs=[blk] * 3,
                          out_shape=[out, out, out], compiler_params=_params("parallel"))(w, g, m, v)


def adamw_layers(w, g0, g1, m, v, name):
    _, k, n = w.shape
    bm = _pick(k, 256, 8)
    c1 = 1.0 / (1.0 - ADAM_B1 ** ADAM_STEP)
    c2 = 1.0 / (1.0 - ADAM_B2 ** ADAM_STEP)

    def body(w_ref, g0_ref, g1_ref, m_ref, v_ref, g_ref, d_ref, nm_ref, nv_ref):
        def step(gv):
            nm = ADAM_B1 * m_ref[...] + (1.0 - ADAM_B1) * gv
            nv = ADAM_B2 * v_ref[...] + (1.0 - ADAM_B2) * (gv * gv)
            g_ref[...] = gv
            nm_ref[...] = nm
            nv_ref[...] = nv
            d_ref[...] = -ADAM_LR * ((nm * c1) / (jnp.sqrt(nv * c2) + ADAM_EPS) + ADAM_WD * w_ref[...])

        @pl.when(pl.program_id(0) == 0)
        def _():
            step(g0_ref[...])

        @pl.when(pl.program_id(0) == 1)
        def _():
            step(g1_ref[...])

    blk = pl.BlockSpec((None, bm, n), lambda l, i: (l, i, 0))
    out = jax.ShapeDtypeStruct(w.shape, F32)
    return pl.pallas_call(
        body, name=name, grid=(2, k // bm),
        in_specs=[blk, pl.BlockSpec((bm, n), lambda l, i: (i * (1 - l), 0)), pl.BlockSpec((bm, n), lambda l, i: (i * l, 0)),
                  blk, blk],
        out_specs=[blk] * 4, out_shape=[out] * 4, compiler_params=_params("arbitrary", "arbitrary"))(w, g0, g1, m, v)


def _pack(arrays):
    flat = jnp.concatenate([a.reshape(-1) for a in arrays])
    rows = -(-flat.shape[0] // (256 * 128)) * 256
    return jnp.pad(flat, (0, rows * 128 - flat.shape[0])).reshape(rows, 128)


def _unpack(p, shapes):
    flat, out, at = p.reshape(-1), [], 0
    for s in shapes:
        n = math.prod(s)
        out.append(flat[at:at + n].reshape(s))
        at += n
    return out


def kernel(x, mem, norm_ffn1, ffn1_w_in, ffn1_w_out, norm_mix, mix_w_in, sconv_w, sgu_norm_g, sgu_w, sgu_b, cconv_w, cconv_ln_g, cconv_ln_b, pool_w, pool_scale, mix_w_out, norm_xattn, norm_mem, xattn_wq, xattn_wkv, xattn_wo, norm_ffn2, ffn2_w_in, ffn2_w_out, norm_final, loss_target, m_norm_ffn1, m_ffn1_w_in, m_ffn1_w_out, m_norm_mix, m_mix_w_in, m_sconv_w, m_sgu_norm_g, m_sgu_w, m_sgu_b, m_cconv_w, m_cconv_ln_g, m_cconv_ln_b, m_pool_w, m_pool_scale, m_mix_w_out, m_norm_xattn, m_norm_mem, m_xattn_wq, m_xattn_wkv, m_xattn_wo, m_norm_ffn2, m_ffn2_w_in, m_ffn2_w_out, m_norm_final, v_norm_ffn1, v_ffn1_w_in, v_ffn1_w_out, v_norm_mix, v_mix_w_in, v_sconv_w, v_sgu_norm_g, v_sgu_w, v_sgu_b, v_cconv_w, v_cconv_ln_g, v_cconv_ln_b, v_pool_w, v_pool_scale, v_mix_w_out, v_norm_xattn, v_norm_mem, v_xattn_wq, v_xattn_wkv, v_xattn_wo, v_norm_ffn2, v_ffn2_w_in, v_ffn2_w_out, v_norm_final):
    given = dict(locals())
    w = {n: given[n] for n in WEIGHTS}
    L = ffn1_w_in.shape[0]
    assert L == 2, "the reduce-scatter gives one layer to each core of a chip"
    chip = 2 * lax.axis_index("x") + lax.axis_index("y")
    chip1 = chip.astype(jnp.int32).reshape(1)
    core = lax.axis_index("c").astype(jnp.int32).reshape(1)
    place = jnp.concatenate([chip1, core])

    axis = {n: 1 if n in COL_SHARDED else 0 for n in BIG}
    bufs = {}
    for n in BIG:
        for l, b in enumerate(cast_into_slot(w[n], axis[n] + 1, chip1, "cast_weights")):
            bufs[n, l] = b
    groups = {"a": [(n, 0) for n in BIG[:2]], "b": [(n, 0) for n in BIG[2:]], "c": [(n, 1) for n in BIG]}
    started, token = {}, jnp.zeros((8, 128), F32)
    for g, keys in groups.items():
        send, recv, thru, token = split_start("gather_start_" + g, [], [bufs[k] for k in keys], 3 * len(keys),
                                              gather_copies([axis[k[0]] for k in keys]), token)
        started[g] = (send, recv, thru)
    ready = {}

    def fetch(n, l, after):
        g = next(g for g, keys in groups.items() if (n, l) in keys)
        if g not in ready:
            send, recv, thru = started[g]
            axes = [axis[k[0]] for k in groups[g]]
            done = split_wait("gather_wait_" + g, [], thru, send, recv, gather_copies(axes), token if g == "a" else after)
            ready[g] = dict(zip(groups[g], forward_sibling(done, axes, "gather_forward")))
        return ready[g][n, l]

    wc = sconv_w.shape[-1]
    conv_rows = [w[n].reshape(-1, wc) for n in SMALL_CONV]
    n_conv = sum(r.shape[0] for r in conv_rows)
    conv_pack = jnp.pad(jnp.concatenate(conv_rows, axis=0), ((0, -n_conv % 8), (0, 128 - wc)))[None]
    conv_all = all_gather_chips([conv_pack], [0], "gather_conv")[0]
    conv_full = jnp.moveaxis(conv_all[:, :n_conv, :wc], 0, 1).reshape(n_conv, N_CHIPS * wc)
    ws = {n: w[n] for n in SMALL_REPL}
    at = 0
    for n in SMALL_CONV:
        rows = w[n].shape[0] * w[n].shape[1]
        ws[n] = conv_full[at:at + rows].reshape(w[n].shape[0], w[n].shape[1], N_CHIPS * wc)
        at += rows

    axes = [axis[n] for n in BIG]
    halves, inflight = {}, {}

    def stage_pair(l, grads, after):
        gvs = [_grad_view(grads[n][l], axis[n]) for n in BIG]
        others = [lax.empty(gv.shape[:1] + gv.shape[2:], F32) for gv in gvs]
        send, recv, others, _ = split_start(f"pair_start_{l}", gvs, others, len(gvs), pair_copies, after)
        return gvs, send, recv, others

    def stage_chips(l, state, after):
        gvs, send, recv, others = state
        others = split_wait(f"pair_wait_{l}", gvs, others, send, recv, pair_copies, after)
        sums = [add_pair(gv, o, place, "add_pair") for gv, o in zip(gvs, others)]
        qs = [lax.empty((3, s.shape[1], s.shape[2] // (N_CHIPS if ax == 1 else 1)), GRAD_WIRE) for s, ax in zip(sums, axes)]
        send, recv, qs, _ = split_start(f"chips_start_{l}", sums, qs, 3 * len(sums), chip_copies(axes), after)
        return sums, send, recv, qs

    def stage_finish(l, state, after):
        sums, send, recv, qs = state
        qs = split_wait(f"chips_wait_{l}", sums, qs, send, recv, chip_copies(axes), after)
        for n, s, q in zip(BIG, sums, qs):
            halves[n, l] = add_chips(s, q, axis[n], place, "add_chips")

    def progress(event, l, grads, dx):
        if event == "layer":
            state = stage_pair(l, grads, dx)
            if l == 0:
                stage_finish(l, stage_chips(l, state, dx), dx)
            else:
                inflight["pair"] = (l, state)
        elif event == "ffn2" and "pair" in inflight:
            prev, state = inflight.pop("pair")
            inflight["chips"] = (prev, stage_chips(prev, state, dx))
        elif event == "mix" and "chips" in inflight:
            prev, state = inflight.pop("chips")
            stage_finish(prev, state, dx)

    loss_part, grad_x, grads = _local_step(x[0], mem[0], loss_target[0], fetch, ws, L, progress)
    loss = lax.psum(loss_part, ("x", "y", "c"))

    small = SMALL_REPL + SMALL_CONV
    small_g = [grads[n] if n == "norm_final" else jnp.stack(grads[n]) for n in small]
    total = _unpack(all_reduce_small(_pack(small_g), "reduce_small"), [g.shape for g in small_g])
    grad = dict(zip(small, total))
    for n in SMALL_CONV:
        grad[n] = lax.dynamic_slice_in_dim(grad[n], chip * wc, wc, axis=2)

    keys = [(n, l) for n in BIG for l in range(L)]
    shard_grad = dict(zip(keys, share_sibling([halves[k] for k in keys], "share_pair")))
    delta, new_m, new_v = {}, {}, {}
    for n in BIG:
        g0, g1 = (shard_grad[n, l].reshape(w[n].shape[1:]) for l in range(L))
        grad[n], delta[n], new_m[n], new_v[n] = adamw_layers(w[n], g0, g1, given["m_" + n], given["v_" + n], "adamw")
    shapes = [w[n].shape for n in small]
    packed = [_pack([src[n] for n in small]) for src in
              (w, grad, {n: given["m_" + n] for n in small}, {n: given["v_" + n] for n in small})]
    for out, p in zip((delta, new_m, new_v), adamw(*packed, "adamw_small")):
        out.update(zip(small, _unpack(p, shapes)))

    return (loss, grad_x[None], *[grad[n] for n in WEIGHTS], *[delta[n] for n in WEIGHTS],
            *[new_m[n] for n in WEIGHTS], *[new_v[n] for n in WEIGHTS])
```

```python
import functools
import math

import jax
import jax.numpy as jnp
from jax import lax
from jax.experimental import pallas as pl
from jax.experimental.pallas import tpu as pltpu

F32 = jnp.float32
BF16 = jnp.bfloat16
EPS = 1e-6
SEQ_CHUNK = 128
POOL_WINDOWS = (2, 4, 8, 16)
N_HEADS = 4
ADAM_LR, ADAM_B1, ADAM_B2, ADAM_EPS, ADAM_WD, ADAM_STEP = 0.001, 0.9, 0.999, 1e-08, 0.01, 10
VMEM_LIMIT = 56 * 1024 * 1024
MESH_ID = pl.DeviceIdType.MESH
N_CHIPS = 4
GRAD_WIRE = BF16

BIG = ("ffn1_w_in", "ffn1_w_out", "mix_w_in", "mix_w_out", "xattn_wq", "xattn_wkv", "xattn_wo",
       "ffn2_w_in", "ffn2_w_out")
COL_SHARDED = ("ffn1_w_in", "mix_w_in", "xattn_wkv", "ffn2_w_in")
SMALL_CONV = ("sconv_w", "cconv_w")
SMALL_REPL = ("norm_ffn1", "norm_mix", "sgu_norm_g", "sgu_w", "sgu_b", "cconv_ln_g", "cconv_ln_b",
              "pool_w", "pool_scale", "norm_xattn", "norm_mem", "norm_ffn2", "norm_final")
WEIGHTS = ("norm_ffn1", "ffn1_w_in", "ffn1_w_out", "norm_mix", "mix_w_in", "sconv_w", "sgu_norm_g",
           "sgu_w", "sgu_b", "cconv_w", "cconv_ln_g", "cconv_ln_b", "pool_w", "pool_scale",
           "mix_w_out", "norm_xattn", "norm_mem", "xattn_wq", "xattn_wkv", "xattn_wo", "norm_ffn2",
           "ffn2_w_in", "ffn2_w_out", "norm_final")


def _pick(n, pref, align):
    best = None
    for d in range(align, min(n, pref) + 1, align):
        if n % d == 0:
            best = d
    return best or n


def _sig(x):
    return 1.0 / (1.0 + jnp.exp(-x))


def _nt(a, b):
    return lax.dot_general(a, b, (((1,), (1,)), ((), ())), preferred_element_type=F32)


def _tn(a, b):
    return lax.dot_general(a, b, (((0,), (0,)), ((), ())), preferred_element_type=F32)


def _params(*sem):
    return pltpu.CompilerParams(dimension_semantics=sem, vmem_limit_bytes=VMEM_LIMIT)


def norm_matmul(x, g, w, name):
    T, D = x.shape
    N = w.shape[1]
    tm, tn = _pick(T, 512, 8), _pick(N, 512, 128)

    def body(x_ref, g_ref, w_ref, o_ref, h_ref):
        j = pl.program_id(1)

        @pl.when(j == 0)
        def _():
            xv = x_ref[...]
            r = lax.rsqrt(jnp.mean(xv * xv, axis=-1, keepdims=True) + EPS)
            h_ref[...] = (xv * r * g_ref[...]).astype(BF16)

        o_ref[...] = jnp.dot(h_ref[...], w_ref[...], preferred_element_type=F32)

    return pl.pallas_call(
        body, name=name, grid=(T // tm, N // tn),
        in_specs=[pl.BlockSpec((tm, D), lambda i, j: (i, 0)), pl.BlockSpec((1, D), lambda i, j: (0, 0)),
                  pl.BlockSpec((D, tn), lambda i, j: (0, j))],
        out_specs=[pl.BlockSpec((tm, tn), lambda i, j: (i, j)), pl.BlockSpec((tm, D), lambda i, j: (i, 0))],
        out_shape=[jax.ShapeDtypeStruct((T, N), F32), jax.ShapeDtypeStruct((T, D), BF16)],
        compiler_params=_params("parallel", "arbitrary"))(x, g, w)


def matmul_res(res, a, w, name):
    T, K = a.shape
    N = w.shape[1]
    tm, tn = _pick(T, 512, 8), _pick(N, 512, 128)

    def body(r_ref, a_ref, w_ref, o_ref):
        o_ref[...] = r_ref[...] + jnp.dot(a_ref[...].astype(BF16), w_ref[...], preferred_element_type=F32)

    return pl.pallas_call(
        body, name=name, grid=(T // tm, N // tn),
        in_specs=[pl.BlockSpec((tm, tn), lambda i, j: (i, j)), pl.BlockSpec((tm, K), lambda i, j: (i, 0)),
                  pl.BlockSpec((K, tn), lambda i, j: (0, j))],
        out_specs=pl.BlockSpec((tm, tn), lambda i, j: (i, j)),
        out_shape=jax.ShapeDtypeStruct((T, N), F32),
        compiler_params=_params("parallel", "parallel"))(res, a, w)


def matmul_nt(a, w, name):
    T, N = a.shape
    M = w.shape[0]
    tm, tmm = _pick(T, 512, 8), _pick(M, 512, 128)

    def body(a_ref, w_ref, o_ref):
        o_ref[...] = _nt(a_ref[...].astype(BF16), w_ref[...])

    return pl.pallas_call(
        body, name=name, grid=(T // tm, M // tmm),
        in_specs=[pl.BlockSpec((tm, N), lambda i, j: (i, 0)), pl.BlockSpec((tmm, N), lambda i, j: (j, 0))],
        out_specs=pl.BlockSpec((tm, tmm), lambda i, j: (i, j)),
        out_shape=jax.ShapeDtypeStruct((T, M), F32),
        compiler_params=_params("parallel", "parallel"))(a, w)


def matmul_tn(a, b, scale, name):
    T, M = a.shape
    N = b.shape[1]
    bm, bn, bk = _pick(M, 1408, 128), _pick(N, 1408, 128), _pick(T, 512, 8)
    nk = T // bk

    def body(a_ref, b_ref, o_ref):
        k = pl.program_id(2)

        @pl.when(k == 0)
        def _():
            o_ref[...] = jnp.zeros_like(o_ref)

        o_ref[...] += _tn(a_ref[...].astype(BF16), b_ref[...].astype(BF16))

        if scale != 1.0:
            @pl.when(k == nk - 1)
            def _():
                o_ref[...] = o_ref[...] * scale

    return pl.pallas_call(
        body, name=name, grid=(M // bm, N // bn, nk),
        in_specs=[pl.BlockSpec((bk, bm), lambda i, j, k: (k, i)), pl.BlockSpec((bk, bn), lambda i, j, k: (k, j))],
        out_specs=pl.BlockSpec((bm, bn), lambda i, j, k: (i, j)),
        out_shape=jax.ShapeDtypeStruct((M, N), F32),
        compiler_params=_params("parallel", "parallel", "arbitrary"))(a, b)


def rmsnorm_bwd(dxo, dh, x, g, name):
    T, D = x.shape
    tm = _pick(T, 512, 8)
    has_res = dxo is not None

    def body(*refs):
        if has_res:
            dxo_ref, dh_ref, x_ref, g_ref, dx_ref, dg_ref = refs
        else:
            dh_ref, x_ref, g_ref, dx_ref, dg_ref = refs
        i = pl.program_id(0)

        @pl.when(i == 0)
        def _():
            dg_ref[...] = jnp.zeros_like(dg_ref)

        xv, dh_v = x_ref[...], dh_ref[...]
        r = lax.rsqrt(jnp.mean(xv * xv, axis=-1, keepdims=True) + EPS)
        xh = xv * r
        dg_ref[...] += jnp.sum(dh_v * xh, axis=0, keepdims=True)
        dxh = dh_v * g_ref[...]
        dx = r * (dxh - xh * jnp.mean(dxh * xh, axis=-1, keepdims=True))
        dx_ref[...] = dx + dxo_ref[...] if has_res else dx

    tile = pl.BlockSpec((tm, D), lambda i: (i, 0))
    vec = pl.BlockSpec((1, D), lambda i: (0, 0))
    args = ([dxo] if has_res else []) + [dh, x, g]
    return pl.pallas_call(
        body, name=name, grid=(T // tm,),
        in_specs=[tile] * (len(args) - 1) + [vec],
        out_specs=[tile, vec],
        out_shape=[jax.ShapeDtypeStruct((T, D), F32), jax.ShapeDtypeStruct((1, D), F32)],
        compiler_params=_params("arbitrary"))(*args)


def ffn_fwd(x, g, w_in, w_out, name):
    T, D = x.shape
    F = w_out.shape[0]
    tm, tf = _pick(T, 512, 8), _pick(F, 1408, 128)
    nf = F // tf

    def body(x_ref, g_ref, wg_ref, wu_ref, wo_ref, o_ref, h_ref, acc_ref):
        j = pl.program_id(1)

        @pl.when(j == 0)
        def _():
            xv = x_ref[...]
            r = lax.rsqrt(jnp.mean(xv * xv, axis=-1, keepdims=True) + EPS)
            h_ref[...] = (xv * r * g_ref[...]).astype(BF16)
            acc_ref[...] = jnp.zeros_like(acc_ref)

        h = h_ref[...]
        zg = jnp.dot(h, wg_ref[...], preferred_element_type=F32)
        zu = jnp.dot(h, wu_ref[...], preferred_element_type=F32)
        a = (zg * _sig(zg) * zu).astype(BF16)
        acc_ref[...] += jnp.dot(a, wo_ref[...], preferred_element_type=F32)

        @pl.when(j == nf - 1)
        def _():
            o_ref[...] = x_ref[...] + 0.5 * acc_ref[...]

    return pl.pallas_call(
        body, name=name, grid=(T // tm, nf),
        in_specs=[pl.BlockSpec((tm, D), lambda i, j: (i, 0)), pl.BlockSpec((1, D), lambda i, j: (0, 0)),
                  pl.BlockSpec((D, tf), lambda i, j: (0, j)), pl.BlockSpec((D, tf), lambda i, j: (0, j + nf)),
                  pl.BlockSpec((tf, D), lambda i, j: (j, 0))],
        out_specs=pl.BlockSpec((tm, D), lambda i, j: (i, 0)),
        out_shape=jax.ShapeDtypeStruct((T, D), F32),
        scratch_shapes=[pltpu.VMEM((tm, D), BF16), pltpu.VMEM((tm, D), F32)],
        compiler_params=_params("parallel", "arbitrary"))(x, g, w_in, w_in, w_out)


def ffn_dz(x, dxo, g, w_in, w_out, name):
    T, D = x.shape
    F = w_out.shape[0]
    tm, tf = _pick(T, 512, 8), _pick(F, 256, 128)
    nf = F // tf

    def body(x_ref, dxo_ref, g_ref, wg_ref, wu_ref, wo_ref, h_ref, a_ref, dzg_ref, dzu_ref, do_ref):
        j = pl.program_id(1)

        @pl.when(j == 0)
        def _():
            xv = x_ref[...]
            r = lax.rsqrt(jnp.mean(xv * xv, axis=-1, keepdims=True) + EPS)
            h_ref[...] = (xv * r * g_ref[...]).astype(BF16)
            do_ref[...] = (0.5 * dxo_ref[...]).astype(BF16)

        h = h_ref[...]
        zg = jnp.dot(h, wg_ref[...], preferred_element_type=F32)
        zu = jnp.dot(h, wu_ref[...], preferred_element_type=F32)
        s = _sig(zg)
        silu = zg * s
        a_ref[...] = (silu * zu).astype(BF16)
        da = _nt(do_ref[...], wo_ref[...])
        dzu_ref[...] = (da * silu).astype(BF16)
        dzg_ref[...] = (da * zu * (s * (1.0 + zg * (1.0 - s)))).astype(BF16)

    tile = pl.BlockSpec((tm, D), lambda i, j: (i, 0))
    fblk = pl.BlockSpec((tm, tf), lambda i, j: (i, j))
    hidden = jax.ShapeDtypeStruct((T, F), BF16)
    return pl.pallas_call(
        body, name=name, grid=(T // tm, nf),
        in_specs=[tile, tile, pl.BlockSpec((1, D), lambda i, j: (0, 0)), pl.BlockSpec((D, tf), lambda i, j: (0, j)),
                  pl.BlockSpec((D, tf), lambda i, j: (0, j + nf)), pl.BlockSpec((tf, D), lambda i, j: (j, 0))],
        out_specs=[tile, fblk, fblk, fblk],
        out_shape=[jax.ShapeDtypeStruct((T, D), BF16), hidden, hidden, hidden],
        scratch_shapes=[pltpu.VMEM((tm, D), BF16)],
        compiler_params=_params("parallel", "arbitrary"))(x, dxo, g, w_in, w_in, w_out)


def ffn_dh(x, dxo, g, dzg, dzu, w_in, name):
    T, D = x.shape
    F = dzg.shape[1]
    tm = _pick(T, 256, 8)

    def body(x_ref, dxo_ref, g_ref, dzg_ref, dzu_ref, wg_ref, wu_ref, dx_ref, dg_ref):
        i = pl.program_id(0)

        @pl.when(i == 0)
        def _():
            dg_ref[...] = jnp.zeros_like(dg_ref)

        dh = _nt(dzg_ref[...], wg_ref[...]) + _nt(dzu_ref[...], wu_ref[...])
        xv = x_ref[...]
        r = lax.rsqrt(jnp.mean(xv * xv, axis=-1, keepdims=True) + EPS)
        xh = xv * r
        dg_ref[...] += jnp.sum(dh * xh, axis=0, keepdims=True)
        dxh = dh * g_ref[...]
        dx_ref[...] = dxo_ref[...] + r * (dxh - xh * jnp.mean(dxh * xh, axis=-1, keepdims=True))

    tile = pl.BlockSpec((tm, D), lambda i: (i, 0))
    vec = pl.BlockSpec((1, D), lambda i: (0, 0))
    ftile = pl.BlockSpec((tm, F), lambda i: (i, 0))
    return pl.pallas_call(
        body, name=name, grid=(T // tm,),
        in_specs=[tile, tile, vec, ftile, ftile, pl.BlockSpec((D, F), lambda i: (0, 0)), pl.BlockSpec((D, F), lambda i: (0, 1))],
        out_specs=[tile, vec],
        out_shape=[jax.ShapeDtypeStruct((T, D), F32), jax.ShapeDtypeStruct((1, D), F32)],
        compiler_params=_params("arbitrary"))(x, dxo, g, dzg, dzu, w_in, w_in)


def _chunks(T, fn):
    def step(c, carry):
        fn(pl.multiple_of(c * SEQ_CHUNK, SEQ_CHUNK))
        return carry
    lax.fori_loop(0, T // SEQ_CHUNK, step, 0)


def _conv_taps(win, ktaps, pad):
    return [(win if k == ktaps - 1 else pltpu.roll(win, ktaps - 1 - k, 0))[pad:, :] for k in range(ktaps)]


def _conv_taps_t(win, ktaps, pad):
    n = win.shape[0]
    return [(win if k == ktaps - 1 else pltpu.roll(win, n - (ktaps - 1 - k), 0))[:n - pad, :] for k in range(ktaps)]


def _col(T, W, idx):
    return pl.BlockSpec((T, W), lambda i, idx=idx: (0, idx))


def _full(shape):
    return pl.BlockSpec(shape, lambda i: (0,) * len(shape))


def mix_a_fwd(z, w, name):
    T, W = z.shape[0], w.shape[1]
    K, P = w.shape[0], 8

    def body(ab_ref, ac_ref, ax_ref, w_ref, y_ref, pp_ref):
        pp_ref[0:P, :] = jnp.zeros((P, W), F32)

        def chunk(s):
            rows = pl.ds(s, SEQ_CHUNK)
            pp_ref[pl.ds(s + P, SEQ_CHUNK), :] = ac_ref[rows, :] * ax_ref[rows, :]
            taps = _conv_taps(pp_ref[pl.ds(s, SEQ_CHUNK + P), :], K, P)
            q = sum(w_ref[k:k + 1, :] * taps[k] for k in range(K))
            y_ref[rows, :] = (ab_ref[rows, :] * q).astype(BF16)

        _chunks(T, chunk)

    return pl.pallas_call(
        body, name=name, grid=(1,),
        in_specs=[_col(T, W, 0), _col(T, W, 1), _col(T, W, 2), _full((K, W))],
        out_specs=_full((T, W)), out_shape=jax.ShapeDtypeStruct((T, W), BF16),
        scratch_shapes=[pltpu.VMEM((T + P, W), F32)],
        compiler_params=_params("arbitrary"))(z, z, z, w)


def mix_a_bwd(z, dy, w, name):
    T, W = z.shape[0], w.shape[1]
    K, P = w.shape[0], 8

    def body(ab_ref, ac_ref, ax_ref, dy_ref, w_ref, dab_ref, dac_ref, dax_ref, dw_ref, pp_ref, dq_ref):
        pp_ref[0:P, :] = jnp.zeros((P, W), F32)
        dq_ref[T:T + P, :] = jnp.zeros((P, W), F32)
        dw_ref[...] = jnp.zeros_like(dw_ref)

        def chunk1(s):
            rows = pl.ds(s, SEQ_CHUNK)
            pp_ref[pl.ds(s + P, SEQ_CHUNK), :] = ac_ref[rows, :] * ax_ref[rows, :]
            taps = _conv_taps(pp_ref[pl.ds(s, SEQ_CHUNK + P), :], K, P)
            q = sum(w_ref[k:k + 1, :] * taps[k] for k in range(K))
            dyv = dy_ref[rows, :]
            dab_ref[rows, :] = (dyv * q).astype(BF16)
            dq = dyv * ab_ref[rows, :]
            dq_ref[rows, :] = dq
            for k in range(K):
                dw_ref[k:k + 1, :] += jnp.sum(dq * taps[k], axis=0, keepdims=True)

        _chunks(T, chunk1)

        def chunk2(s):
            rows = pl.ds(s, SEQ_CHUNK)
            taps = _conv_taps_t(dq_ref[pl.ds(s, SEQ_CHUNK + P), :], K, P)
            dp = sum(w_ref[k:k + 1, :] * taps[k] for k in range(K))
            dac_ref[rows, :] = (dp * ax_ref[rows, :]).astype(BF16)
            dax_ref[rows, :] = (dp * ac_ref[rows, :]).astype(BF16)

        _chunks(T, chunk2)

    tw = jax.ShapeDtypeStruct((T, W), BF16)
    return pl.pallas_call(
        body, name=name, grid=(1,),
        in_specs=[_col(T, W, 0), _col(T, W, 1), _col(T, W, 2), _col(T, W, 0), _full((K, W))],
        out_specs=[_full((T, W))] * 3 + [_full((K, W))],
        out_shape=[tw, tw, tw, jax.ShapeDtypeStruct((K, W), F32)],
        scratch_shapes=[pltpu.VMEM((T + P, W), F32), pltpu.VMEM((T + P, W), F32)],
        compiler_params=_params("arbitrary"))(z, z, z, dy, w)


def _ln_stats(v):
    mu = jnp.mean(v, axis=-1, keepdims=True)
    xc = v - mu
    rstd = lax.rsqrt(jnp.mean(xc * xc, axis=-1, keepdims=True) + EPS)
    return xc * rstd, rstd


def _ln_bwd(dxh, xh, rstd):
    return rstd * (dxh - jnp.mean(dxh, axis=-1, keepdims=True) - xh * jnp.mean(dxh * xh, axis=-1, keepdims=True))


def _tril_bf16(w_ref, h):
    n = w_ref.shape[-1]
    keep = lax.broadcasted_iota(jnp.int32, (n, n), 0) >= lax.broadcasted_iota(jnp.int32, (n, n), 1)
    return jnp.where(keep, w_ref[h], 0.0).astype(BF16)


def mix_b_fwd(z, g, w_s, bias, name):
    T, W = z.shape[0], g.shape[1]
    H, C = w_s.shape[0], w_s.shape[1]
    hd = W // H

    def body(u_ref, v_ref, g_ref, w_ref, b_ref, y_ref):
        wts = [_tril_bf16(w_ref, h) for h in range(H)]
        head = lax.broadcasted_iota(jnp.int32, (C, W), 1) // hd

        def chunk(s):
            rows = pl.ds(s, C)
            xh, _ = _ln_stats(v_ref[rows, :])
            vn = (xh * g_ref[...]).astype(BF16)
            mixed = b_ref[...]
            for h in range(H):
                mixed = mixed + jnp.where(head == h, jnp.dot(wts[h], vn, preferred_element_type=F32), 0.0)
            y_ref[rows, :] = (u_ref[rows, :] * mixed).astype(BF16)

        _chunks(T, chunk)

    return pl.pallas_call(
        body, name=name, grid=(1,),
        in_specs=[_col(T, W, 3), _col(T, W, 4), _full((1, W)), _full((H, C, C)), _full((C, W))],
        out_specs=_full((T, W)), out_shape=jax.ShapeDtypeStruct((T, W), BF16),
        compiler_params=_params("arbitrary"))(z, z, g, w_s, bias)


def mix_b_bwd(z, dy, g, w_s, bias, name):
    T, W = z.shape[0], g.shape[1]
    H, C = w_s.shape[0], w_s.shape[1]
    hd = W // H

    def body(u_ref, v_ref, dy_ref, g_ref, w_ref, b_ref, du_ref, dv_ref, dw_ref, db_ref, dg_ref, dbf_ref):
        wts = [_tril_bf16(w_ref, h) for h in range(H)]
        head = lax.broadcasted_iota(jnp.int32, (C, W), 1) // hd
        dw_ref[...] = jnp.zeros_like(dw_ref)
        dg_ref[...] = jnp.zeros_like(dg_ref)
        dbf_ref[...] = jnp.zeros_like(dbf_ref)

        def chunk(s):
            rows = pl.ds(s, C)
            xh, rstd = _ln_stats(v_ref[rows, :])
            vn = (xh * g_ref[...]).astype(BF16)
            mixed = b_ref[...]
            for h in range(H):
                mixed = mixed + jnp.where(head == h, jnp.dot(wts[h], vn, preferred_element_type=F32), 0.0)
            dyv = dy_ref[rows, :]
            du_ref[rows, :] = (dyv * mixed).astype(BF16)
            dm = dyv * u_ref[rows, :]
            dbf_ref[...] += dm
            dvn = jnp.zeros((C, W), F32)
            for h in range(H):
                dmh = jnp.where(head == h, dm, 0.0).astype(BF16)
                dw_ref[h] += _nt(dmh, vn)
                dvn = dvn + _tn(wts[h], dmh)
            dg_ref[...] += jnp.sum(dvn * xh, axis=0, keepdims=True)
            dv_ref[rows, :] = _ln_bwd(dvn * g_ref[...], xh, rstd).astype(BF16)

        _chunks(T, chunk)

        keep = lax.broadcasted_iota(jnp.int32, (C, C), 0) >= lax.broadcasted_iota(jnp.int32, (C, C), 1)
        lane = lax.broadcasted_iota(jnp.int32, (C, 128), 1)
        db = jnp.zeros((C, 128), F32)
        dbf = dbf_ref[...]
        for h in range(H):
            dw_ref[h] = jnp.where(keep, dw_ref[h], 0.0)
            db = db + jnp.where(lane == h, jnp.sum(jnp.where(head == h, dbf, 0.0), axis=1, keepdims=True), 0.0)
        db_ref[...] = db

    tw = jax.ShapeDtypeStruct((T, W), BF16)
    return pl.pallas_call(
        body, name=name, grid=(1,),
        in_specs=[_col(T, W, 3), _col(T, W, 4), _col(T, W, 1), _full((1, W)), _full((H, C, C)), _full((C, W))],
        out_specs=[_full((T, W)), _full((T, W)), _full((H, C, C)), _full((C, 128)), _full((1, W))],
        out_shape=[tw, tw, jax.ShapeDtypeStruct((H, C, C), F32), jax.ShapeDtypeStruct((C, 128), F32),
                   jax.ShapeDtypeStruct((1, W), F32)],
        scratch_shapes=[pltpu.VMEM((C, W), F32)],
        compiler_params=_params("arbitrary"))(z, z, dy, g, w_s, bias)


def mix_c_fwd(z, w, ln_g, ln_b, name):
    T, W = z.shape[0], w.shape[1]
    K, P = w.shape[0], 32

    def body(a_ref, gt_ref, w_ref, g_ref, b_ref, y_ref, up_ref):
        up_ref[0:P, :] = jnp.zeros((P, W), F32)

        def chunk(s):
            rows = pl.ds(s, SEQ_CHUNK)
            up_ref[pl.ds(s + P, SEQ_CHUNK), :] = a_ref[rows, :] * _sig(gt_ref[rows, :])
            taps = _conv_taps(up_ref[pl.ds(s, SEQ_CHUNK + P), :], K, P)
            q = sum(w_ref[k:k + 1, :] * taps[k] for k in range(K))
            xh, _ = _ln_stats(q)
            r = xh * g_ref[...] + b_ref[...]
            y_ref[rows, :] = (r * _sig(r)).astype(BF16)

        _chunks(T, chunk)

    return pl.pallas_call(
        body, name=name, grid=(1,),
        in_specs=[_col(T, W, 5), _col(T, W, 6), _full((K, W)), _full((1, W)), _full((1, W))],
        out_specs=_full((T, W)), out_shape=jax.ShapeDtypeStruct((T, W), BF16),
        scratch_shapes=[pltpu.VMEM((T + P, W), F32)],
        compiler_params=_params("arbitrary"))(z, z, w, ln_g, ln_b)


def mix_c_bwd(z, dy, w, ln_g, ln_b, name):
    T, W = z.shape[0], w.shape[1]
    K, P = w.shape[0], 32

    def body(a_ref, gt_ref, dy_ref, w_ref, g_ref, b_ref, da_ref, dgt_ref, dw_ref, dg_ref, db_ref, up_ref, dq_ref):
        up_ref[0:P, :] = jnp.zeros((P, W), F32)
        dq_ref[T:T + P, :] = jnp.zeros((P, W), F32)
        dw_ref[...] = jnp.zeros_like(dw_ref)
        dg_ref[...] = jnp.zeros_like(dg_ref)
        db_ref[...] = jnp.zeros_like(db_ref)

        def chunk1(s):
            rows = pl.ds(s, SEQ_CHUNK)
            up_ref[pl.ds(s + P, SEQ_CHUNK), :] = a_ref[rows, :] * _sig(gt_ref[rows, :])
            taps = _conv_taps(up_ref[pl.ds(s, SEQ_CHUNK + P), :], K, P)
            q = sum(w_ref[k:k + 1, :] * taps[k] for k in range(K))
            xh, rstd = _ln_stats(q)
            r = xh * g_ref[...] + b_ref[...]
            sr = _sig(r)
            dr = dy_ref[rows, :] * (sr * (1.0 + r * (1.0 - sr)))
            db_ref[...] += jnp.sum(dr, axis=0, keepdims=True)
            dg_ref[...] += jnp.sum(dr * xh, axis=0, keepdims=True)
            dq = _ln_bwd(dr * g_ref[...], xh, rstd)
            dq_ref[rows, :] = dq
            for k in range(K):
                dw_ref[k:k + 1, :] += jnp.sum(dq * taps[k], axis=0, keepdims=True)

        _chunks(T, chunk1)

        def chunk2(s):
            rows = pl.ds(s, SEQ_CHUNK)
            taps = _conv_taps_t(dq_ref[pl.ds(s, SEQ_CHUNK + P), :], K, P)
            du = sum(w_ref[k:k + 1, :] * taps[k] for k in range(K))
            sg = _sig(gt_ref[rows, :])
            da_ref[rows, :] = (du * sg).astype(BF16)
            dgt_ref[rows, :] = (du * a_ref[rows, :] * sg * (1.0 - sg)).astype(BF16)

        _chunks(T, chunk2)

    tw = jax.ShapeDtypeStruct((T, W), BF16)
    vec = jax.ShapeDtypeStruct((1, W), F32)
    return pl.pallas_call(
        body, name=name, grid=(1,),
        in_specs=[_col(T, W, 5), _col(T, W, 6), _col(T, W, 2), _full((K, W)), _full((1, W)), _full((1, W))],
        out_specs=[_full((T, W)), _full((T, W)), _full((K, W)), _full((1, W)), _full((1, W))],
        out_shape=[tw, tw, jax.ShapeDtypeStruct((K, W), F32), vec, vec],
        scratch_shapes=[pltpu.VMEM((T + P, W), F32), pltpu.VMEM((T + P, W), F32)],
        compiler_params=_params("arbitrary"))(z, z, dy, w, ln_g, ln_b)


def _pool_select(levels, W, rows):
    group = lax.broadcasted_iota(jnp.int32, (rows, W), 1) // (W // len(POOL_WINDOWS))
    out = levels[-1]
    for gi in range(len(POOL_WINDOWS) - 2, -1, -1):
        out = jnp.where(group == gi, levels[gi], out)
    return out


def _pool_count(s, W):
    t = s + lax.broadcasted_iota(jnp.int32, (SEQ_CHUNK, W), 0)
    group = lax.broadcasted_iota(jnp.int32, (SEQ_CHUNK, W), 1) // (W // len(POOL_WINDOWS))
    win = jnp.full((SEQ_CHUNK, W), POOL_WINDOWS[-1], jnp.int32)
    for gi in range(len(POOL_WINDOWS) - 2, -1, -1):
        win = jnp.where(group == gi, POOL_WINDOWS[gi], win)
    return jnp.minimum(t + 1, win).astype(F32)


def _pooled(wp_ref, s, W, P):
    win = wp_ref[pl.ds(s, SEQ_CHUNK + P), :]
    levels, acc, shift = [], win, 1
    for _ in POOL_WINDOWS:
        acc = acc + pltpu.roll(acc, shift, 0)
        levels.append(acc[P:, :])
        shift *= 2
    return _pool_select(levels, W, SEQ_CHUNK) / _pool_count(s, W) - win[P:, :]


def mix_d_fwd(z, pbd, scale, name):
    T, W = z.shape[0], scale.shape[1]
    P = 16

    def body(x_ref, p_ref, s_ref, y_ref, wp_ref):
        wp_ref[0:P, :] = jnp.zeros((P, W), F32)

        def chunk(s):
            rows = pl.ds(s, SEQ_CHUNK)
            wp_ref[pl.ds(s + P, SEQ_CHUNK), :] = x_ref[rows, :]
            pooled = _pooled(wp_ref, s, W, P).astype(BF16)
            y_ref[rows, :] = (jnp.dot(pooled, p_ref[...], preferred_element_type=F32) * s_ref[...]).astype(BF16)

        _chunks(T, chunk)

    return pl.pallas_call(
        body, name=name, grid=(1,),
        in_specs=[_col(T, W, 7), _full((W, W)), _full((1, W))],
        out_specs=_full((T, W)), out_shape=jax.ShapeDtypeStruct((T, W), BF16),
        scratch_shapes=[pltpu.VMEM((T + P, W), F32)],
        compiler_params=_params("arbitrary"))(z, pbd, scale)


def mix_d_bwd(z, dy, pbd, scale, name):
    T, W = z.shape[0], scale.shape[1]
    P = 16

    def body(x_ref, dy_ref, p_ref, s_ref, dx_ref, dp_ref, ds_ref, wp_ref, e_ref, dpool_ref):
        wp_ref[0:P, :] = jnp.zeros((P, W), F32)
        e_ref[T:T + P, :] = jnp.zeros((P, W), F32)
        dp_ref[...] = jnp.zeros_like(dp_ref)
        ds_ref[...] = jnp.zeros_like(ds_ref)

        def chunk1(s):
            rows = pl.ds(s, SEQ_CHUNK)
            wp_ref[pl.ds(s + P, SEQ_CHUNK), :] = x_ref[rows, :]
            pooled = _pooled(wp_ref, s, W, P).astype(BF16)
            yl = jnp.dot(pooled, p_ref[...], preferred_element_type=F32)
            dyv = dy_ref[rows, :]
            ds_ref[...] += jnp.sum(dyv * yl, axis=0, keepdims=True)
            dyl = (dyv * s_ref[...]).astype(BF16)
            dp_ref[...] += _tn(pooled, dyl)
            dpool = _nt(dyl, p_ref[...])
            dpool_ref[rows, :] = dpool
            e_ref[rows, :] = dpool / _pool_count(s, W)

        _chunks(T, chunk1)

        def chunk2(s):
            rows = pl.ds(s, SEQ_CHUNK)
            win = e_ref[pl.ds(s, SEQ_CHUNK + P), :]
            n = SEQ_CHUNK + P
            levels, acc, shift = [], win, 1
            for _ in POOL_WINDOWS:
                acc = acc + pltpu.roll(acc, n - shift, 0)
                levels.append(acc[:SEQ_CHUNK, :])
                shift *= 2
            dx_ref[rows, :] = (_pool_select(levels, W, SEQ_CHUNK) - dpool_ref[rows, :]).astype(BF16)

        _chunks(T, chunk2)

    return pl.pallas_call(
        body, name=name, grid=(1,),
        in_specs=[_col(T, W, 7), _col(T, W, 3), _full((W, W)), _full((1, W))],
        out_specs=[_full((T, W)), _full((W, W)), _full((1, W))],
        out_shape=[jax.ShapeDtypeStruct((T, W), BF16), jax.ShapeDtypeStruct((W, W), F32),
                   jax.ShapeDtypeStruct((1, W), F32)],
        scratch_shapes=[pltpu.VMEM((T + P, W), F32), pltpu.VMEM((T + P, W), F32), pltpu.VMEM((T, W), F32)],
        compiler_params=_params("arbitrary"))(z, dy, pbd, scale)


def attn_fwd(q, kv, name):
    T, D = q.shape
    M = kv.shape[0]
    hd = D // N_HEADS
    tm = _pick(T, 512, 8)
    sc = 1.0 / math.sqrt(hd)

    def body(q_ref, k_ref, v_ref, o_ref):
        for h in range(N_HEADS):
            cols = slice(h * hd, (h + 1) * hd)
            s = _nt(q_ref[:, cols].astype(BF16), k_ref[:, cols].astype(BF16)) * sc
            p = jnp.exp(s - jnp.max(s, axis=-1, keepdims=True))
            p = p / jnp.sum(p, axis=-1, keepdims=True)
            o_ref[:, cols] = jnp.dot(p.astype(BF16), v_ref[:, cols].astype(BF16),
                                     preferred_element_type=F32).astype(BF16)

    return pl.pallas_call(
        body, name=name, grid=(T // tm,),
        in_specs=[pl.BlockSpec((tm, D), lambda i: (i, 0)), pl.BlockSpec((M, D), lambda i: (0, 0)),
                  pl.BlockSpec((M, D), lambda i: (0, 1))],
        out_specs=pl.BlockSpec((tm, D), lambda i: (i, 0)),
        out_shape=jax.ShapeDtypeStruct((T, D), BF16),
        compiler_params=_params("parallel"))(q, kv, kv)


def attn_bwd(q, kv, do, name):
    T, D = q.shape
    M = kv.shape[0]
    hd = D // N_HEADS
    tm = _pick(T, 512, 8)
    sc = 1.0 / math.sqrt(hd)

    def body(q_ref, k_ref, v_ref, do_ref, dq_ref, dk_ref, dv_ref):
        i = pl.program_id(0)

        @pl.when(i == 0)
        def _():
            dk_ref[...] = jnp.zeros_like(dk_ref)
            dv_ref[...] = jnp.zeros_like(dv_ref)

        for h in range(N_HEADS):
            cols = slice(h * hd, (h + 1) * hd)
            qh, kh = q_ref[:, cols].astype(BF16), k_ref[:, cols].astype(BF16)
            vh, doh = v_ref[:, cols].astype(BF16), do_ref[:, cols].astype(BF16)
            s = _nt(qh, kh) * sc
            p = jnp.exp(s - jnp.max(s, axis=-1, keepdims=True))
            p = p / jnp.sum(p, axis=-1, keepdims=True)
            dp = _nt(doh, vh)
            dv_ref[:, cols] += _tn(p.astype(BF16), doh)
            ds = (p * (dp - jnp.sum(dp * p, axis=-1, keepdims=True)) * sc).astype(BF16)
            dq_ref[:, cols] = jnp.dot(ds, kh, preferred_element_type=F32).astype(BF16)
            dk_ref[:, cols] += _tn(ds, qh)

    tile = pl.BlockSpec((tm, D), lambda i: (i, 0))
    mem = jax.ShapeDtypeStruct((M, D), F32)
    return pl.pallas_call(
        body, name=name, grid=(T // tm,),
        in_specs=[tile, pl.BlockSpec((M, D), lambda i: (0, 0)), pl.BlockSpec((M, D), lambda i: (0, 1)), tile],
        out_specs=[tile, pl.BlockSpec((M, D), lambda i: (0, 0)), pl.BlockSpec((M, D), lambda i: (0, 0))],
        out_shape=[jax.ShapeDtypeStruct((T, D), BF16), mem, mem],
        compiler_params=_params("arbitrary"))(q, kv, kv, do)


def loss_head(x, g, target, name):
    T, D = x.shape
    tm = _pick(T, 512, 8)

    def body(x_ref, g_ref, t_ref, l_ref, dx_ref, dg_ref):
        i = pl.program_id(0)

        @pl.when(i == 0)
        def _():
            l_ref[...] = jnp.zeros_like(l_ref)
            dg_ref[...] = jnp.zeros_like(dg_ref)

        xv = x_ref[...]
        r = lax.rsqrt(jnp.mean(xv * xv, axis=-1, keepdims=True) + EPS)
        xh = xv * r
        err = xh * g_ref[...] - t_ref[...]
        l_ref[...] += 0.5 * jnp.sum(jnp.mean(err * err, axis=-1, keepdims=True), axis=0, keepdims=True)
        dy = err * (1.0 / D)
        dg_ref[...] += jnp.sum(dy * xh, axis=0, keepdims=True)
        dxh = dy * g_ref[...]
        dx_ref[...] = r * (dxh - xh * jnp.mean(dxh * xh, axis=-1, keepdims=True))

    tile = pl.BlockSpec((tm, D), lambda i: (i, 0))
    vec = pl.BlockSpec((1, D), lambda i: (0, 0))
    return pl.pallas_call(
        body, name=name, grid=(T // tm,),
        in_specs=[tile, vec, tile],
        out_specs=[pl.BlockSpec((1, 128), lambda i: (0, 0)), tile, vec],
        out_shape=[jax.ShapeDtypeStruct((1, 128), F32), jax.ShapeDtypeStruct((T, D), F32),
                   jax.ShapeDtypeStruct((1, D), F32)],
        compiler_params=_params("arbitrary"))(x, g, target)


def _block_diag(p):
    G, gd, _ = p.shape
    rows = [jnp.concatenate([p[g] if g == c else jnp.zeros((gd, gd), p.dtype) for c in range(G)], axis=1)
            for g in range(G)]
    return jnp.concatenate(rows, axis=0)


class _LazyWeight:
    def __init__(self, fetch, name, latest):
        self.fetch, self.name, self.latest = fetch, name, latest

    def __getitem__(self, l):
        return self.fetch(self.name, l, self.latest[0])


def _local_step(x, mem, target, fetch, ws, L, progress=lambda event, l, grads, values: values):
    T, D = x.shape
    W = D // 4
    H = ws["sgu_w"].shape[1]
    row = lambda v: v.reshape(1, -1)
    latest = [x]
    wb = {n: _LazyWeight(fetch, n, latest) for n in BIG}
    saved = []
    for l in range(L):
        s = {"x0": x}
        latest[0] = x
        x = ffn_fwd(x, row(ws["norm_ffn1"][l]), wb["ffn1_w_in"][l], wb["ffn1_w_out"][l], "ffn_fwd")
        s["x1"] = x
        latest[0] = x
        z, s["h_mix"] = norm_matmul(x, row(ws["norm_mix"][l]), wb["mix_w_in"][l], "mix_in")
        s["z"] = z
        s["bias"] = jnp.repeat(ws["sgu_b"][l].T, W // H, axis=1)
        s["pbd"] = _block_diag(ws["pool_w"][l]).astype(BF16)
        y = jnp.concatenate([
            mix_a_fwd(z, ws["sconv_w"][l], "mix_a_fwd"),
            mix_b_fwd(z, row(ws["sgu_norm_g"][l]), ws["sgu_w"][l], s["bias"], "mix_b_fwd"),
            mix_c_fwd(z, ws["cconv_w"][l], row(ws["cconv_ln_g"][l]), row(ws["cconv_ln_b"][l]), "mix_c_fwd"),
            mix_d_fwd(z, s["pbd"], row(ws["pool_scale"][l]), "mix_d_fwd")], axis=1)
        s["y"] = y
        x = matmul_res(x, y, wb["mix_w_out"][l], "mix_out")
        s["x2"] = x
        s["q"], s["hq"] = norm_matmul(x, row(ws["norm_xattn"][l]), wb["xattn_wq"][l], "attn_q")
        s["kv"], s["mn"] = norm_matmul(mem, row(ws["norm_mem"][l]), wb["xattn_wkv"][l], "attn_kv")
        s["o"] = attn_fwd(s["q"], s["kv"], "attn_fwd")
        x = matmul_res(x, s["o"], wb["xattn_wo"][l], "attn_out")
        s["x3"] = x
        x = ffn_fwd(x, row(ws["norm_ffn2"][l]), wb["ffn2_w_in"][l], wb["ffn2_w_out"][l], "ffn_fwd")
        saved.append(s)

    loss, dx, dg_final = loss_head(x, row(ws["norm_final"]), target, "loss_head")
    grads = {n: [None] * L for n in WEIGHTS if n != "norm_final"}
    grads["norm_final"] = dg_final.reshape(-1)

    def pin(dx, names, l):
        dx, made = lax.optimization_barrier((dx, [grads[n][l] for n in names]))
        for n, g in zip(names, made):
            grads[n][l] = g
        return dx

    def after_stages(event, l, values):
        return progress(event, l, grads, values)

    def ffn_back(xin, dxo, gname, win, wout, l, event):
        h, a, dzg, dzu = ffn_dz(xin, dxo, row(ws[gname][l]), wb[win][l], wb[wout][l], "ffn_dz")
        dxn, dg = ffn_dh(xin, dxo, row(ws[gname][l]), dzg, dzu, wb[win][l], "ffn_dh")
        grads[gname][l] = dg.reshape(-1)
        dxn, h, a = after_stages(event, l, (dxn, h, a))
        grads[win][l] = jnp.concatenate([matmul_tn(h, dzg, 1.0, "ffn_dwin"), matmul_tn(h, dzu, 1.0, "ffn_dwin")], axis=1)
        grads[wout][l] = matmul_tn(a, dxo, 0.5, "ffn_dwout")
        return pin(dxn, (win, wout), l)

    for l in reversed(range(L)):
        s = saved[l]
        dx = ffn_back(s["x3"], dx, "norm_ffn2", "ffn2_w_in", "ffn2_w_out", l, "ffn2_mid")
        dx, = after_stages("ffn2", l, (dx,))
        grads["xattn_wo"][l] = matmul_tn(s["o"], dx, 1.0, "dw_sq")
        do = matmul_nt(dx, wb["xattn_wo"][l], "attn_do")
        dq, dk, dv = attn_bwd(s["q"], s["kv"], do, "attn_bwd")
        grads["xattn_wq"][l] = matmul_tn(s["hq"], dq, 1.0, "dw_sq")
        dhq = matmul_nt(dq, wb["xattn_wq"][l], "attn_dhq")
        dx, dg = rmsnorm_bwd(dx, dhq, s["x2"], row(ws["norm_xattn"][l]), "norm_bwd")
        grads["norm_xattn"][l] = dg.reshape(-1)
        dkv = jnp.concatenate([dk, dv], axis=1)
        grads["xattn_wkv"][l] = matmul_tn(s["mn"], dkv, 1.0, "attn_dwkv")
        dmn = matmul_nt(dkv, wb["xattn_wkv"][l], "attn_dmn")
        _, dg = rmsnorm_bwd(None, dmn, mem, row(ws["norm_mem"][l]), "norm_mem_bwd")
        grads["norm_mem"][l] = dg.reshape(-1)
        dx = pin(dx, ("xattn_wo", "xattn_wq", "xattn_wkv", "norm_mem"), l)
        dx, = after_stages("attn", l, (dx,))
        grads["mix_w_out"][l] = matmul_tn(s["y"], dx, 1.0, "dw_sq")
        dy = matmul_nt(dx, wb["mix_w_out"][l], "mix_dy")
        z = s["z"]
        dab, dac, dax, dws = mix_a_bwd(z, dy, ws["sconv_w"][l], "mix_a_bwd")
        dbu, dbv, dwsgu, dbs, dgs = mix_b_bwd(z, dy, row(ws["sgu_norm_g"][l]), ws["sgu_w"][l], s["bias"], "mix_b_bwd")
        dca, dcg, dwc, dgc, dbc = mix_c_bwd(z, dy, ws["cconv_w"][l], row(ws["cconv_ln_g"][l]),
                                            row(ws["cconv_ln_b"][l]), "mix_c_bwd")
        ddw, dpbd, dsc = mix_d_bwd(z, dy, s["pbd"], row(ws["pool_scale"][l]), "mix_d_bwd")
        grads["sconv_w"][l], grads["cconv_w"][l] = dws, dwc
        grads["sgu_w"][l], grads["sgu_b"][l], grads["sgu_norm_g"][l] = dwsgu, dbs[:, :H].T, dgs.reshape(-1)
        grads["cconv_ln_g"][l], grads["cconv_ln_b"][l] = dgc.reshape(-1), dbc.reshape(-1)
        gd = W // len(POOL_WINDOWS)
        grads["pool_w"][l] = jnp.stack([dpbd[g * gd:(g + 1) * gd, g * gd:(g + 1) * gd] for g in range(len(POOL_WINDOWS))])
        grads["pool_scale"][l] = dsc.reshape(-1)
        dz = jnp.concatenate([dab, dac, dax, dbu, dbv, dca, dcg, ddw], axis=1)
        grads["mix_w_in"][l] = matmul_tn(s["h_mix"], dz, 1.0, "mix_dwin")
        dh = matmul_nt(dz, wb["mix_w_in"][l], "mix_dh")
        dx, dg = rmsnorm_bwd(dx, dh, s["x1"], row(ws["norm_mix"][l]), "norm_bwd")
        grads["norm_mix"][l] = dg.reshape(-1)
        dx = pin(dx, ("mix_w_out", "mix_w_in"), l)
        dx, = after_stages("mix", l, (dx,))
        dx = ffn_back(s["x0"], dx, "norm_ffn1", "ffn1_w_in", "ffn1_w_out", l, "ffn1_mid")
        dx, = after_stages("layer", l, (dx,))

    return loss[0, 0], dx, grads


ANY = pl.BlockSpec(memory_space=pl.ANY)


def _other_chips(x, y):
    return [(1 - x, y), (x, 1 - y), (1 - x, 1 - y)]


def _shard_slice(ref, axis, chip, size):
    idx = [slice(None)] * len(ref.shape)
    idx[axis] = pl.ds(pl.multiple_of(chip * size, size), size)
    return ref.at[tuple(idx)]


def all_gather_chips(shards, axes, name):
    n = len(shards)

    def body(*refs):
        ins, outs = refs[:n], refs[n:2 * n]
        send, recv, loc = refs[2 * n:]
        x, y, c = lax.axis_index("x"), lax.axis_index("y"), lax.axis_index("c")
        me = 2 * x + y
        chips = _other_chips(x, y)
        started = []
        for i in range(n):
            size = ins[i].shape[axes[i]]
            cp = pltpu.make_async_copy(ins[i], _shard_slice(outs[i], axes[i], me, size), loc.at[i])
            cp.start()
            started.append(cp)
        sends = []
        for i in range(n):
            size = ins[i].shape[axes[i]]
            for j, (px, py) in enumerate(chips):
                cp = pltpu.make_async_remote_copy(
                    src_ref=ins[i], dst_ref=_shard_slice(outs[i], axes[i], me, size),
                    send_sem=send.at[i, j], recv_sem=recv.at[i, j], device_id=(px, py, c), device_id_type=MESH_ID)
                cp.start()
                sends.append(cp)
        for i in range(n):
            size = ins[i].shape[axes[i]]
            for j, (px, py) in enumerate(chips):
                pltpu.make_async_remote_copy(
                    src_ref=ins[i], dst_ref=_shard_slice(outs[i], axes[i], 2 * px + py, size),
                    send_sem=send.at[i, j], recv_sem=recv.at[i, j], device_id=(px, py, c),
                    device_id_type=MESH_ID).wait_recv()
        for cp in sends:
            cp.wait_send()
        for cp in started:
            cp.wait()

    def full(a, ax):
        shape = list(a.shape)
        shape[ax] *= N_CHIPS
        return jax.ShapeDtypeStruct(tuple(shape), a.dtype)

    return pl.pallas_call(
        body, name=name, in_specs=[ANY] * n, out_specs=[ANY] * n,
        out_shape=[full(a, ax) for a, ax in zip(shards, axes)],
        scratch_shapes=[pltpu.SemaphoreType.DMA((n, 3)), pltpu.SemaphoreType.DMA((n, 3)),
                        pltpu.SemaphoreType.DMA((n,))],
        compiler_params=pltpu.CompilerParams(has_side_effects=True))(*shards)


def cast_into_slot(shard, axis, chip, name):
    L, K, N = shard.shape
    bm = _pick(K, 256, 16)
    full = (K * N_CHIPS, N) if axis == 1 else (K, N * N_CHIPS)
    nb = K // bm

    def body(c_ref, s_ref, *o_refs):
        for l in range(L):
            o_refs[l][...] = s_ref[l].astype(BF16)

    out_map = (lambda i, c: (c[0] * nb + i, 0)) if axis == 1 else (lambda i, c: (i, c[0]))
    spec = pltpu.PrefetchScalarGridSpec(
        num_scalar_prefetch=1, grid=(nb,),
        in_specs=[pl.BlockSpec((L, bm, N), lambda i, c: (0, i, 0))],
        out_specs=[pl.BlockSpec((bm, N), out_map)] * L)
    return pl.pallas_call(body, name=name, grid_spec=spec, out_shape=[jax.ShapeDtypeStruct(full, BF16)] * L,
                          compiler_params=_params("parallel"))(chip, shard)


HBM = pl.BlockSpec(memory_space=pltpu.HBM)
SEM = pl.BlockSpec(memory_space=pltpu.SEMAPHORE)
DATAFLOW = pltpu.SideEffectType.DATAFLOW_SIDE_EFFECTING


def split_start(name, sources, landing, n_sems, make, after):
    ns, nl, na = len(sources), len(landing), len(after)

    def body(*refs):
        out, _ = make(refs[:ns], refs[ns:ns + nl], refs[ns + nl + na], refs[ns + nl + na + 1])
        for cp in out:
            cp.start()
        refs[-1][...] = jnp.zeros_like(refs[-1])

    hbm = lambda b: pltpu.with_memory_space_constraint(b, pltpu.HBM)
    res = pl.pallas_call(
        body, name=name,
        out_shape=(pltpu.SemaphoreType.DMA((n_sems,)), pltpu.SemaphoreType.DMA((n_sems,)),
                   *[pltpu.HBM(b.shape, b.dtype) for b in landing], jax.ShapeDtypeStruct((8, 128), F32)),
        in_specs=[HBM] * (ns + nl) + [ANY] * na, out_specs=(SEM, SEM, *[HBM] * nl, pl.BlockSpec(memory_space=pltpu.VMEM)),
        input_output_aliases={ns + i: 2 + i for i in range(nl)},
        compiler_params=pltpu.CompilerParams(has_side_effects=DATAFLOW))(
            *[hbm(b) for b in sources], *[hbm(b) for b in landing], *after)
    return res[0], res[1], list(res[2:2 + nl]), res[-1]


def split_wait(name, sources, landing, send, recv, make, after):
    ns, nl = len(sources), len(landing)

    def body(*refs):
        _, back = make(refs[:ns], refs[ns:ns + nl], refs[ns + nl], refs[ns + nl + 1])
        for cp in back:
            cp.wait_send()
            cp.wait_recv()

    return list(pl.pallas_call(
        body, name=name, out_shape=tuple(pltpu.HBM(b.shape, b.dtype) for b in landing),
        in_specs=[HBM] * (ns + nl) + [SEM, SEM] + [ANY] * len(after), out_specs=tuple([HBM] * nl),
        input_output_aliases={ns + i: i for i in range(nl)},
        compiler_params=pltpu.CompilerParams(has_side_effects=DATAFLOW))(
            *[pltpu.with_memory_space_constraint(b, pltpu.HBM) for b in sources], *landing, send, recv, *after))


def _half_slot(buf, axis, chip, half):
    K, N = buf.shape
    if axis == 1:
        n = N // N_CHIPS
        return buf.at[pl.ds(pl.multiple_of(half * (K // 2), K // 2), K // 2), pl.ds(pl.multiple_of(chip * n, n), n)]
    k2 = K // N_CHIPS // 2
    return buf.at[pl.ds(pl.multiple_of((2 * chip + half) * k2, k2), k2), :]


def gather_copies(axes):
    def make(_, bufs, send, recv):
        x, y, c = lax.axis_index("x"), lax.axis_index("y"), lax.axis_index("c")
        out, back = [], []
        for i, (buf, ax) in enumerate(zip(bufs, axes)):
            mine = _half_slot(buf, ax, 2 * x + y, c)
            for j, (px, py) in enumerate(_other_chips(x, y)):
                kw = dict(send_sem=send.at[3 * i + j], recv_sem=recv.at[3 * i + j], device_id=(px, py, c),
                          device_id_type=MESH_ID)
                out.append(pltpu.make_async_remote_copy(src_ref=mine, dst_ref=mine, **kw))
                back.append(pltpu.make_async_remote_copy(src_ref=mine, dst_ref=_half_slot(buf, ax, 2 * px + py, c), **kw))
        return out, back
    return make


def forward_sibling(bufs, axes, name):
    n = len(bufs)

    def body(*refs):
        ins = refs[:n]
        send, recv = refs[2 * n:]
        x, y, c = lax.axis_index("x"), lax.axis_index("y"), lax.axis_index("c")
        out, back = [], []
        for i, ax in enumerate(axes):
            for j, (px, py) in enumerate(_other_chips(x, y)):
                have = _half_slot(ins[i], ax, 2 * px + py, c)
                kw = dict(send_sem=send.at[3 * i + j], recv_sem=recv.at[3 * i + j], device_id=(x, y, 1 - c),
                          device_id_type=MESH_ID)
                out.append(pltpu.make_async_remote_copy(src_ref=have, dst_ref=have, **kw))
                back.append(pltpu.make_async_remote_copy(src_ref=have, dst_ref=_half_slot(ins[i], ax, 2 * px + py, 1 - c), **kw))
        for cp in out:
            cp.start()
        for cp in back:
            cp.wait_recv()
        for cp in out:
            cp.wait_send()

    return pl.pallas_call(
        body, name=name, in_specs=[ANY] * n, out_specs=[ANY] * n,
        out_shape=[jax.ShapeDtypeStruct(b.shape, b.dtype) for b in bufs],
        input_output_aliases={i: i for i in range(n)},
        scratch_shapes=[pltpu.SemaphoreType.DMA((3 * n,)), pltpu.SemaphoreType.DMA((3 * n,))],
        compiler_params=pltpu.CompilerParams(has_side_effects=True))(*bufs)


def all_reduce_small(p, name):
    R = p.shape[0]

    def body(p_ref, o_ref, sib_ref, chip_ref, send, recv):
        x, y, c = lax.axis_index("x"), lax.axis_index("y"), lax.axis_index("c")
        me = 2 * x + y
        chips = _other_chips(x, y)
        pair = pltpu.make_async_remote_copy(src_ref=p_ref, dst_ref=sib_ref, send_sem=send.at[0], recv_sem=recv.at[0],
                                            device_id=(x, y, 1 - c), device_id_type=MESH_ID)
        pair.start()
        pair.wait()
        chip_ref[me] = p_ref[...] + sib_ref[...]
        sends = []
        for j, (px, py) in enumerate(chips):
            cp = pltpu.make_async_remote_copy(src_ref=chip_ref.at[me], dst_ref=chip_ref.at[me], send_sem=send.at[1 + j],
                                              recv_sem=recv.at[1 + j], device_id=(px, py, c), device_id_type=MESH_ID)
            cp.start()
            sends.append(cp)
        for j, (px, py) in enumerate(chips):
            pltpu.make_async_remote_copy(src_ref=chip_ref.at[me], dst_ref=chip_ref.at[2 * px + py], send_sem=send.at[1 + j],
                                         recv_sem=recv.at[1 + j], device_id=(px, py, c), device_id_type=MESH_ID).wait_recv()
        for cp in sends:
            cp.wait_send()
        o_ref[...] = ((chip_ref[0] + chip_ref[1]) + chip_ref[2]) + chip_ref[3]

    vm = pl.BlockSpec(memory_space=pltpu.VMEM)
    return pl.pallas_call(
        body, name=name, in_specs=[vm], out_specs=vm, out_shape=jax.ShapeDtypeStruct((R, 128), F32),
        scratch_shapes=[pltpu.VMEM((R, 128), F32), pltpu.VMEM((N_CHIPS, R, 128), F32),
                        pltpu.SemaphoreType.DMA((4,)), pltpu.SemaphoreType.DMA((4,))],
        compiler_params=pltpu.CompilerParams(has_side_effects=True, vmem_limit_bytes=VMEM_LIMIT))(p)


def _grad_view(g, axis):
    K, N = g.shape
    return g.reshape(1, 2, K // 2, N) if axis == 1 else g.reshape(N_CHIPS, 2, K // N_CHIPS // 2, N)


def pair_copies(gvs, others, send, recv):
    x, y, c = lax.axis_index("x"), lax.axis_index("y"), lax.axis_index("c")
    out = [pltpu.make_async_remote_copy(src_ref=gv.at[:, 1 - c], dst_ref=o, send_sem=send.at[i], recv_sem=recv.at[i],
                                        device_id=(x, y, 1 - c), device_id_type=MESH_ID)
           for i, (gv, o) in enumerate(zip(gvs, others))]
    return out, out


def chip_copies(axes):
    def piece(s, ax, chip):
        if ax == 1:
            n = s.shape[2] // N_CHIPS
            return s.at[0, :, pl.ds(pl.multiple_of(chip * n, n), n)]
        return s.at[chip]

    def make(sums, qs, send, recv):
        x, y, c = lax.axis_index("x"), lax.axis_index("y"), lax.axis_index("c")
        out = []
        for i, (s, q, ax) in enumerate(zip(sums, qs, axes)):
            for j, (px, py) in enumerate(_other_chips(x, y)):
                out.append(pltpu.make_async_remote_copy(
                    src_ref=piece(s, ax, 2 * px + py), dst_ref=q.at[j], send_sem=send.at[3 * i + j],
                    recv_sem=recv.at[3 * i + j], device_id=(px, py, c), device_id_type=MESH_ID))
        return out, out
    return make


def share_sibling(halves, name):
    n = len(halves)

    def body(*refs):
        ins = refs[:n]
        send, recv = refs[2 * n:]
        x, y, c = lax.axis_index("x"), lax.axis_index("y"), lax.axis_index("c")
        cps = [pltpu.make_async_remote_copy(src_ref=ins[i].at[c], dst_ref=ins[i].at[c], send_sem=send.at[i], recv_sem=recv.at[i],
                                            device_id=(x, y, 1 - c), device_id_type=MESH_ID) for i in range(n)]
        for cp in cps:
            cp.start()
        for i in range(n):
            pltpu.make_async_remote_copy(src_ref=ins[i].at[c], dst_ref=ins[i].at[1 - c], send_sem=send.at[i], recv_sem=recv.at[i],
                                         device_id=(x, y, 1 - c), device_id_type=MESH_ID).wait_recv()
        for cp in cps:
            cp.wait_send()

    return pl.pallas_call(
        body, name=name, in_specs=[ANY] * n, out_specs=[ANY] * n,
        out_shape=[jax.ShapeDtypeStruct(a.shape, a.dtype) for a in halves],
        input_output_aliases={i: i for i in range(n)},
        scratch_shapes=[pltpu.SemaphoreType.DMA((n,)), pltpu.SemaphoreType.DMA((n,))],
        compiler_params=pltpu.CompilerParams(has_side_effects=True))(*halves)


def add_pair(gv, other, place, name):
    A, _, rows, N = gv.shape
    bm, bn = _pick(rows, 256, 16), _pick(N, 1408, 128)

    def body(p_ref, g_ref, o_ref, out_ref):
        out_ref[...] = (g_ref[...] + o_ref[...]).astype(GRAD_WIRE)

    spec = pltpu.PrefetchScalarGridSpec(
        num_scalar_prefetch=1, grid=(A, rows // bm, N // bn),
        in_specs=[pl.BlockSpec((None, None, bm, bn), lambda a, i, j, p: (a, p[1], i, j)),
                  pl.BlockSpec((None, bm, bn), lambda a, i, j, p: (a, i, j))],
        out_specs=pl.BlockSpec((None, bm, bn), lambda a, i, j, p: (a, i, j)))
    return pl.pallas_call(body, name=name, grid_spec=spec, out_shape=jax.ShapeDtypeStruct((A, rows, N), GRAD_WIRE),
                          compiler_params=_params("parallel", "parallel", "parallel"))(place, gv, other)


def add_chips(s, q, axis, place, name):
    _, rows, n = q.shape
    bm, bn = _pick(rows, 256, 16), _pick(n, 1408, 128)
    nbj = n // bn

    def body(p_ref, s_ref, q_ref, o_ref):
        o_ref[...] = ((s_ref[...].astype(F32) + q_ref[0].astype(F32)) + q_ref[1].astype(F32)) + q_ref[2].astype(F32)

    mine = (lambda i, j, p: (p[0], i, j)) if axis == 0 else (lambda i, j, p: (0, i, p[0] * nbj + j))
    spec = pltpu.PrefetchScalarGridSpec(
        num_scalar_prefetch=1, grid=(rows // bm, nbj),
        in_specs=[pl.BlockSpec((None, bm, bn), mine), pl.BlockSpec((3, bm, bn), lambda i, j, p: (0, i, j))],
        out_specs=pl.BlockSpec((None, bm, bn), lambda i, j, p: (p[1], i, j)))
    return pl.pallas_call(body, name=name, grid_spec=spec, out_shape=jax.ShapeDtypeStruct((2, rows, n), F32),
                          compiler_params=_params("parallel", "parallel"))(place, s, q)


def adamw(w, g, m, v, name):
    R, N = w.shape
    bm = _pick(R, 256, 8)
    c1 = 1.0 / (1.0 - ADAM_B1 ** ADAM_STEP)
    c2 = 1.0 / (1.0 - ADAM_B2 ** ADAM_STEP)

    def body(w_ref, g_ref, m_ref, v_ref, d_ref, nm_ref, nv_ref):
        gv = g_ref[...]
        nm = ADAM_B1 * m_ref[...] + (1.0 - ADAM_B1) * gv
        nv = ADAM_B2 * v_ref[...] + (1.0 - ADAM_B2) * (gv * gv)
        nm_ref[...] = nm
        nv_ref[...] = nv
        d_ref[...] = -ADAM_LR * ((nm * c1) / (jnp.sqrt(nv * c2) + ADAM_EPS) + ADAM_WD * w_ref[...])

    blk = pl.BlockSpec((bm, N), lambda i: (i, 0))
    out = jax.ShapeDtypeStruct((R, N), F32)
    return pl.pallas_call(body, name=name, grid=(R // bm,), in_specs=[blk] * 4, out_specs=[blk] * 3,
                          out_shape=[out, out, out], compiler_params=_params("parallel"))(w, g, m, v)


def adamw_layers(w, g0, g1, m, v, name):
    _, k, n = w.shape
    bm = _pick(k, 256, 8)
    c1 = 1.0 / (1.0 - ADAM_B1 ** ADAM_STEP)
    c2 = 1.0 / (1.0 - ADAM_B2 ** ADAM_STEP)

    def body(w_ref, g0_ref, g1_ref, m_ref, v_ref, g_ref, d_ref, nm_ref, nv_ref):
        def step(gv):
            nm = ADAM_B1 * m_ref[...] + (1.0 - ADAM_B1) * gv
            nv = ADAM_B2 * v_ref[...] + (1.0 - ADAM_B2) * (gv * gv)
            g_ref[...] = gv
            nm_ref[...] = nm
            nv_ref[...] = nv
            d_ref[...] = -ADAM_LR * ((nm * c1) / (jnp.sqrt(nv * c2) + ADAM_EPS) + ADAM_WD * w_ref[...])

        @pl.when(pl.program_id(0) == 0)
        def _():
            step(g0_ref[...])

        @pl.when(pl.program_id(0) == 1)
        def _():
            step(g1_ref[...])

    blk = pl.BlockSpec((None, bm, n), lambda l, i: (l, i, 0))
    out = jax.ShapeDtypeStruct(w.shape, F32)
    return pl.pallas_call(
        body, name=name, grid=(2, k // bm),
        in_specs=[blk, pl.BlockSpec((bm, n), lambda l, i: (i * (1 - l), 0)), pl.BlockSpec((bm, n), lambda l, i: (i * l, 0)),
                  blk, blk],
        out_specs=[blk] * 4, out_shape=[out] * 4, compiler_params=_params("arbitrary", "arbitrary"))(w, g0, g1, m, v)


def _pack(arrays):
    flat = jnp.concatenate([a.reshape(-1) for a in arrays])
    rows = -(-flat.shape[0] // (256 * 128)) * 256
    return jnp.pad(flat, (0, rows * 128 - flat.shape[0])).reshape(rows, 128)


def _unpack(p, shapes):
    flat, out, at = p.reshape(-1), [], 0
    for s in shapes:
        n = math.prod(s)
        out.append(flat[at:at + n].reshape(s))
        at += n
    return out


def kernel(x, mem, norm_ffn1, ffn1_w_in, ffn1_w_out, norm_mix, mix_w_in, sconv_w, sgu_norm_g, sgu_w, sgu_b, cconv_w, cconv_ln_g, cconv_ln_b, pool_w, pool_scale, mix_w_out, norm_xattn, norm_mem, xattn_wq, xattn_wkv, xattn_wo, norm_ffn2, ffn2_w_in, ffn2_w_out, norm_final, loss_target, m_norm_ffn1, m_ffn1_w_in, m_ffn1_w_out, m_norm_mix, m_mix_w_in, m_sconv_w, m_sgu_norm_g, m_sgu_w, m_sgu_b, m_cconv_w, m_cconv_ln_g, m_cconv_ln_b, m_pool_w, m_pool_scale, m_mix_w_out, m_norm_xattn, m_norm_mem, m_xattn_wq, m_xattn_wkv, m_xattn_wo, m_norm_ffn2, m_ffn2_w_in, m_ffn2_w_out, m_norm_final, v_norm_ffn1, v_ffn1_w_in, v_ffn1_w_out, v_norm_mix, v_mix_w_in, v_sconv_w, v_sgu_norm_g, v_sgu_w, v_sgu_b, v_cconv_w, v_cconv_ln_g, v_cconv_ln_b, v_pool_w, v_pool_scale, v_mix_w_out, v_norm_xattn, v_norm_mem, v_xattn_wq, v_xattn_wkv, v_xattn_wo, v_norm_ffn2, v_ffn2_w_in, v_ffn2_w_out, v_norm_final):
    given = dict(locals())
    w = {n: given[n] for n in WEIGHTS}
    L = ffn1_w_in.shape[0]
    assert L == 2, "the reduce-scatter gives one layer to each core of a chip"
    chip = 2 * lax.axis_index("x") + lax.axis_index("y")
    chip1 = chip.astype(jnp.int32).reshape(1)
    core = lax.axis_index("c").astype(jnp.int32).reshape(1)
    place = jnp.concatenate([chip1, core])

    axis = {n: 1 if n in COL_SHARDED else 0 for n in BIG}
    bufs = {}
    for n in BIG:
        for l, b in enumerate(cast_into_slot(w[n], axis[n] + 1, chip1, "cast_weights")):
            bufs[n, l] = b
    groups = {"a": [(n, 0) for n in BIG[:2]], "b": [(n, 0) for n in BIG[2:]], "c": [(n, 1) for n in BIG]}
    wc = sconv_w.shape[-1]
    conv_rows = [w[n].reshape(-1, wc) for n in SMALL_CONV]
    n_conv = sum(r.shape[0] for r in conv_rows)
    conv_pack = jnp.pad(jnp.concatenate(conv_rows, axis=0), ((0, -n_conv % 8), (0, 128 - wc)))[None]
    conv_all = all_gather_chips([conv_pack], [0], "gather_conv")[0]
    started, token = {}, conv_all
    for g, keys in groups.items():
        send, recv, thru, token = split_start("gather_start_" + g, [], [bufs[k] for k in keys], 3 * len(keys),
                                              gather_copies([axis[k[0]] for k in keys]), [token])
        started[g] = (send, recv, thru)
    ready = {}

    def fetch(n, l, after):
        g = next(g for g, keys in groups.items() if (n, l) in keys)
        if g not in ready:
            send, recv, thru = started[g]
            axes = [axis[k[0]] for k in groups[g]]
            done = split_wait("gather_wait_" + g, [], thru, send, recv, gather_copies(axes), [token if g == "a" else after])
            ready[g] = dict(zip(groups[g], forward_sibling(done, axes, "gather_forward")))
        return ready[g][n, l]

    conv_full = jnp.moveaxis(conv_all[:, :n_conv, :wc], 0, 1).reshape(n_conv, N_CHIPS * wc)
    ws = {n: w[n] for n in SMALL_REPL}
    at = 0
    for n in SMALL_CONV:
        rows = w[n].shape[0] * w[n].shape[1]
        ws[n] = conv_full[at:at + rows].reshape(w[n].shape[0], w[n].shape[1], N_CHIPS * wc)
        at += rows

    halves, state = {}, {}
    reduce_groups = {"r1": [(n, 1) for n in BIG], "r0a": [(n, 0) for n in BIG[2:]], "r0b": [(n, 0) for n in BIG[:2]]}
    plan = {("layer", 1): [("pair", "r1")],
            ("ffn2", 0): [("chips", "r1")],
            ("mix", 0): [("finish", "r1"), ("pair", "r0a")],
            ("ffn1_mid", 0): [("chips", "r0a")],
            ("layer", 0): [("pair", "r0b"), ("finish", "r0a"), ("chips", "r0b"), ("finish", "r0b")]}


    def stage_pair(g, keys, grads, after):
        gvs = [_grad_view(grads[n][l], axis[n]) for n, l in keys]
        others = [lax.empty(gv.shape[:1] + gv.shape[2:], F32) for gv in gvs]
        send, recv, others, token = split_start("pair_start_" + g, gvs, others, len(gvs), pair_copies, [after])
        state[g] = dict(sources=gvs, send=send, recv=recv, landing=others, token=token)
        return [(state[g], "token")]

    def stage_chips(g, keys, grads, after):
        st = state[g]
        axes = [axis[n] for n, _ in keys]
        others = split_wait("pair_wait_" + g, st["sources"], st["landing"], st["send"], st["recv"], pair_copies,
                            [after, st["token"]])
        sums = [add_pair(gv, o, place, "add_pair") for gv, o in zip(st["sources"], others)]
        qs = [lax.empty((3, s.shape[1], s.shape[2] // (N_CHIPS if ax == 1 else 1)), GRAD_WIRE) for s, ax in zip(sums, axes)]
        send, recv, qs, token = split_start("chips_start_" + g, sums, qs, 3 * len(sums), chip_copies(axes), [after])
        state[g] = dict(sources=sums, send=send, recv=recv, landing=qs, token=token)
        return [(state[g], "token")]

    def stage_finish(g, keys, grads, after):
        st = state.pop(g)
        qs = split_wait("chips_wait_" + g, st["sources"], st["landing"], st["send"], st["recv"],
                        chip_copies([axis[n] for n, _ in keys]), [after, st["token"]])
        for key, s, q in zip(keys, st["sources"], qs):
            halves[key] = add_chips(s, q, axis[key[0]], place, "add_chips")
        return [(halves, key) for key in keys]

    stages = {"pair": stage_pair, "chips": stage_chips, "finish": stage_finish}

    def progress(event, l, grads, values):
        places = []
        for stage, g in plan.get((event, l), []):
            places += stages[stage](g, reduce_groups[g], grads, values[0])
        if places:
            values, tied = lax.optimization_barrier((values, [box[k] for box, k in places]))
            for (box, k), a in zip(places, tied):
                box[k] = a
        return values

    loss_part, grad_x, grads = _local_step(x[0], mem[0], loss_target[0], fetch, ws, L, progress)
    loss = lax.psum(loss_part, ("x", "y", "c"))

    small = SMALL_REPL + SMALL_CONV
    small_g = [grads[n] if n == "norm_final" else jnp.stack(grads[n]) for n in small]
    total = _unpack(all_reduce_small(_pack(small_g), "reduce_small"), [g.shape for g in small_g])
    grad = dict(zip(small, total))
    for n in SMALL_CONV:
        grad[n] = lax.dynamic_slice_in_dim(grad[n], chip * wc, wc, axis=2)

    keys = [(n, l) for n in BIG for l in range(L)]
    shard_grad = dict(zip(keys, share_sibling([halves[k] for k in keys], "share_pair")))
    delta, new_m, new_v = {}, {}, {}
    for n in BIG:
        g0, g1 = (shard_grad[n, l].reshape(w[n].shape[1:]) for l in range(L))
        grad[n], delta[n], new_m[n], new_v[n] = adamw_layers(w[n], g0, g1, given["m_" + n], given["v_" + n], "adamw")
    shapes = [w[n].shape for n in small]
    packed = [_pack([src[n] for n in small]) for src in
              (w, grad, {n: given["m_" + n] for n in small}, {n: given["v_" + n] for n in small})]
    for out, p in zip((delta, new_m, new_v), adamw(*packed, "adamw_small")):
        out.update(zip(small, _unpack(p, shapes)))

    return (loss, grad_x[None], *[grad[n] for n in WEIGHTS], *[delta[n] for n in WEIGHTS],
            *[new_m[n] for n in WEIGHTS], *[new_v[n] for n in WEIGHTS])
```

```python
import functools
import math

import jax
import jax.numpy as jnp
from jax import lax
from jax.experimental import pallas as pl
from jax.experimental.pallas import tpu as pltpu

F32 = jnp.float32
BF16 = jnp.bfloat16
EPS = 1e-6
SEQ_CHUNK = 128
POOL_WINDOWS = (2, 4, 8, 16)
N_HEADS = 4
ADAM_LR, ADAM_B1, ADAM_B2, ADAM_EPS, ADAM_WD, ADAM_STEP = 0.001, 0.9, 0.999, 1e-08, 0.01, 10
VMEM_LIMIT = 56 * 1024 * 1024
MESH_ID = pl.DeviceIdType.MESH
N_CHIPS = 4
GRAD_WIRE = BF16

BIG = ("ffn1_w_in", "ffn1_w_out", "mix_w_in", "mix_w_out", "xattn_wq", "xattn_wkv", "xattn_wo",
       "ffn2_w_in", "ffn2_w_out")
COL_SHARDED = ("ffn1_w_in", "mix_w_in", "xattn_wkv", "ffn2_w_in")
SMALL_CONV = ("sconv_w", "cconv_w")
SMALL_REPL = ("norm_ffn1", "norm_mix", "sgu_norm_g", "sgu_w", "sgu_b", "cconv_ln_g", "cconv_ln_b",
              "pool_w", "pool_scale", "norm_xattn", "norm_mem", "norm_ffn2", "norm_final")
WEIGHTS = ("norm_ffn1", "ffn1_w_in", "ffn1_w_out", "norm_mix", "mix_w_in", "sconv_w", "sgu_norm_g",
           "sgu_w", "sgu_b", "cconv_w", "cconv_ln_g", "cconv_ln_b", "pool_w", "pool_scale",
           "mix_w_out", "norm_xattn", "norm_mem", "xattn_wq", "xattn_wkv", "xattn_wo", "norm_ffn2",
           "ffn2_w_in", "ffn2_w_out", "norm_final")


def _pick(n, pref, align):
    best = None
    for d in range(align, min(n, pref) + 1, align):
        if n % d == 0:
            best = d
    return best or n


def _sig(x):
    return 1.0 / (1.0 + jnp.exp(-x))


def _nt(a, b):
    return lax.dot_general(a, b, (((1,), (1,)), ((), ())), preferred_element_type=F32)


def _tn(a, b):
    return lax.dot_general(a, b, (((0,), (0,)), ((), ())), preferred_element_type=F32)


def _params(*sem):
    return pltpu.CompilerParams(dimension_semantics=sem, vmem_limit_bytes=VMEM_LIMIT)


def norm_matmul(x, g, w, name, out_dtype=F32):
    T, D = x.shape
    N = w.shape[1]
    tm, tn = _pick(T, 512, 8), _pick(N, 1024, 128)

    def body(x_ref, g_ref, w_ref, o_ref, h_ref):
        j = pl.program_id(1)

        @pl.when(j == 0)
        def _():
            xv = x_ref[...]
            r = lax.rsqrt(jnp.mean(xv * xv, axis=-1, keepdims=True) + EPS)
            h_ref[...] = (xv * r * g_ref[...]).astype(BF16)

        o_ref[...] = jnp.dot(h_ref[...], w_ref[...], preferred_element_type=F32).astype(out_dtype)

    return pl.pallas_call(
        body, name=name, grid=(T // tm, N // tn),
        in_specs=[pl.BlockSpec((tm, D), lambda i, j: (i, 0)), pl.BlockSpec((1, D), lambda i, j: (0, 0)),
                  pl.BlockSpec((D, tn), lambda i, j: (0, j))],
        out_specs=[pl.BlockSpec((tm, tn), lambda i, j: (i, j)), pl.BlockSpec((tm, D), lambda i, j: (i, 0))],
        out_shape=[jax.ShapeDtypeStruct((T, N), out_dtype), jax.ShapeDtypeStruct((T, D), BF16)],
        compiler_params=_params("parallel", "arbitrary"))(x, g, w)


def matmul_res(res, a, w, name):
    T, K = a.shape
    N = w.shape[1]
    tm, tn = _pick(T, 512, 8), _pick(N, 1024, 128)

    def body(r_ref, a_ref, w_ref, o_ref):
        o_ref[...] = r_ref[...] + jnp.dot(a_ref[...].astype(BF16), w_ref[...], preferred_element_type=F32)

    return pl.pallas_call(
        body, name=name, grid=(T // tm, N // tn),
        in_specs=[pl.BlockSpec((tm, tn), lambda i, j: (i, j)), pl.BlockSpec((tm, K), lambda i, j: (i, 0)),
                  pl.BlockSpec((K, tn), lambda i, j: (0, j))],
        out_specs=pl.BlockSpec((tm, tn), lambda i, j: (i, j)),
        out_shape=jax.ShapeDtypeStruct((T, N), F32),
        compiler_params=_params("parallel", "parallel"))(res, a, w)


def matmul_nt(a, w, name, out_dtype=F32):
    T, N = a.shape
    M = w.shape[0]
    tm, tmm = _pick(T, 512, 8), _pick(M, 1024, 128)

    def body(a_ref, w_ref, o_ref):
        o_ref[...] = _nt(a_ref[...].astype(BF16), w_ref[...]).astype(out_dtype)

    return pl.pallas_call(
        body, name=name, grid=(T // tm, M // tmm),
        in_specs=[pl.BlockSpec((tm, N), lambda i, j: (i, 0)), pl.BlockSpec((tmm, N), lambda i, j: (j, 0))],
        out_specs=pl.BlockSpec((tm, tmm), lambda i, j: (i, j)),
        out_shape=jax.ShapeDtypeStruct((T, M), out_dtype),
        compiler_params=_params("parallel", "parallel"))(a, w)


def matmul_tn(a, b, scale, name, b2=None):
    T, M = a.shape
    Nb = b.shape[1]
    bm, bn, bk = _pick(M, 1408, 128), _pick(Nb, 1408, 128), _pick(T, 512, 8)
    nk, nj = T // bk, Nb // bn

    def body(a_ref, b_ref, *rest):
        o_ref = rest[-1]
        j, k = pl.program_id(1), pl.program_id(2)

        @pl.when(k == 0)
        def _():
            o_ref[...] = jnp.zeros_like(o_ref)

        a_blk = a_ref[...].astype(BF16)
        if b2 is None:
            o_ref[...] += _tn(a_blk, b_ref[...].astype(BF16))
        else:
            @pl.when(j < nj)
            def _():
                o_ref[...] += _tn(a_blk, b_ref[...].astype(BF16))

            @pl.when(j >= nj)
            def _():
                o_ref[...] += _tn(a_blk, rest[0][...].astype(BF16))

        if scale != 1.0:
            @pl.when(k == nk - 1)
            def _():
                o_ref[...] = o_ref[...] * scale

    if b2 is None:
        b_specs, operands, n_out = [pl.BlockSpec((bk, bn), lambda i, j, k: (k, j))], (a, b), nj
    else:
        first = lambda i, j, k: (jnp.where(j < nj, k, 0), jnp.where(j < nj, j, 0))
        second = lambda i, j, k: (jnp.where(j >= nj, k, 0), jnp.where(j >= nj, j - nj, 0))
        b_specs, operands, n_out = [pl.BlockSpec((bk, bn), first), pl.BlockSpec((bk, bn), second)], (a, b, b2), 2 * nj
    return pl.pallas_call(
        body, name=name, grid=(M // bm, n_out, nk),
        in_specs=[pl.BlockSpec((bk, bm), lambda i, j, k: (k, i))] + b_specs,
        out_specs=pl.BlockSpec((bm, bn), lambda i, j, k: (i, j)),
        out_shape=jax.ShapeDtypeStruct((M, n_out * bn), F32),
        compiler_params=_params("parallel", "parallel", "arbitrary"))(*operands)


def rmsnorm_bwd(dxo, dh, x, g, name):
    T, D = x.shape
    tm = _pick(T, 512, 8)
    has_res = dxo is not None

    def body(*refs):
        if has_res:
            dxo_ref, dh_ref, x_ref, g_ref, dx_ref, dg_ref = refs
        else:
            dh_ref, x_ref, g_ref, dx_ref, dg_ref = refs
        i = pl.program_id(0)

        @pl.when(i == 0)
        def _():
            dg_ref[...] = jnp.zeros_like(dg_ref)

        xv, dh_v = x_ref[...], dh_ref[...]
        r = lax.rsqrt(jnp.mean(xv * xv, axis=-1, keepdims=True) + EPS)
        xh = xv * r
        dg_ref[...] += jnp.sum(dh_v * xh, axis=0, keepdims=True)
        dxh = dh_v * g_ref[...]
        dx = r * (dxh - xh * jnp.mean(dxh * xh, axis=-1, keepdims=True))
        dx_ref[...] = dx + dxo_ref[...] if has_res else dx

    tile = pl.BlockSpec((tm, D), lambda i: (i, 0))
    vec = pl.BlockSpec((1, D), lambda i: (0, 0))
    args = ([dxo] if has_res else []) + [dh, x, g]
    return pl.pallas_call(
        body, name=name, grid=(T // tm,),
        in_specs=[tile] * (len(args) - 1) + [vec],
        out_specs=[tile, vec],
        out_shape=[jax.ShapeDtypeStruct((T, D), F32), jax.ShapeDtypeStruct((1, D), F32)],
        compiler_params=_params("arbitrary"))(*args)


def ffn_fwd(x, g, w_in, w_out, name):
    T, D = x.shape
    F = w_out.shape[0]
    tm, tf = _pick(T, 512, 8), _pick(F, 1408, 128)
    nf = F // tf

    def body(x_ref, g_ref, wg_ref, wu_ref, wo_ref, o_ref, h_ref, zg_ref, zu_ref, acc_ref):
        j = pl.program_id(1)

        @pl.when(j == 0)
        def _():
            xv = x_ref[...]
            r = lax.rsqrt(jnp.mean(xv * xv, axis=-1, keepdims=True) + EPS)
            h_ref[...] = (xv * r * g_ref[...]).astype(BF16)
            acc_ref[...] = jnp.zeros_like(acc_ref)

        h = h_ref[...]
        zg = jnp.dot(h, wg_ref[...], preferred_element_type=F32)
        zu = jnp.dot(h, wu_ref[...], preferred_element_type=F32)
        zg_ref[...] = zg.astype(BF16)
        zu_ref[...] = zu.astype(BF16)
        a = (zg * _sig(zg) * zu).astype(BF16)
        acc_ref[...] += jnp.dot(a, wo_ref[...], preferred_element_type=F32)

        @pl.when(j == nf - 1)
        def _():
            o_ref[...] = x_ref[...] + 0.5 * acc_ref[...]

    tile = pl.BlockSpec((tm, D), lambda i, j: (i, 0))
    fblk = pl.BlockSpec((tm, tf), lambda i, j: (i, j))
    hidden = jax.ShapeDtypeStruct((T, F), BF16)
    return pl.pallas_call(
        body, name=name, grid=(T // tm, nf),
        in_specs=[tile, pl.BlockSpec((1, D), lambda i, j: (0, 0)),
                  pl.BlockSpec((D, tf), lambda i, j: (0, j)), pl.BlockSpec((D, tf), lambda i, j: (0, j + nf)),
                  pl.BlockSpec((tf, D), lambda i, j: (j, 0))],
        out_specs=[tile, tile, fblk, fblk],
        out_shape=[jax.ShapeDtypeStruct((T, D), F32), jax.ShapeDtypeStruct((T, D), BF16), hidden, hidden],
        scratch_shapes=[pltpu.VMEM((tm, D), F32)],
        compiler_params=_params("parallel", "arbitrary"))(x, g, w_in, w_in, w_out)


def ffn_dz(dxo, zg, zu, w_out, name):
    T, D = dxo.shape
    F = w_out.shape[0]
    tm, tf = _pick(T, 512, 8), _pick(F, 256, 128)

    def body(dxo_ref, zg_ref, zu_ref, wo_ref, a_ref, dzg_ref, dzu_ref, do_ref):
        @pl.when(pl.program_id(1) == 0)
        def _():
            do_ref[...] = (0.5 * dxo_ref[...]).astype(BF16)

        zg, zu = zg_ref[...].astype(F32), zu_ref[...].astype(F32)
        s = _sig(zg)
        silu = zg * s
        a_ref[...] = (silu * zu).astype(BF16)
        da = _nt(do_ref[...], wo_ref[...])
        dzu_ref[...] = (da * silu).astype(BF16)
        dzg_ref[...] = (da * zu * (s * (1.0 + zg * (1.0 - s)))).astype(BF16)

    fblk = pl.BlockSpec((tm, tf), lambda i, j: (i, j))
    hidden = jax.ShapeDtypeStruct((T, F), BF16)
    return pl.pallas_call(
        body, name=name, grid=(T // tm, F // tf),
        in_specs=[pl.BlockSpec((tm, D), lambda i, j: (i, 0)), fblk, fblk, pl.BlockSpec((tf, D), lambda i, j: (j, 0))],
        out_specs=[fblk, fblk, fblk], out_shape=[hidden, hidden, hidden],
        scratch_shapes=[pltpu.VMEM((tm, D), BF16)],
        compiler_params=_params("parallel", "arbitrary"))(dxo, zg, zu, w_out)


def ffn_dh(x, dxo, g, dzg, dzu, w_in, name):
    T, D = x.shape
    F = dzg.shape[1]
    tm = _pick(T, 256, 8)

    def body(x_ref, dxo_ref, g_ref, dzg_ref, dzu_ref, wg_ref, wu_ref, dx_ref, dg_ref):
        i = pl.program_id(0)

        @pl.when(i == 0)
        def _():
            dg_ref[...] = jnp.zeros_like(dg_ref)

        dh = _nt(dzg_ref[...], wg_ref[...]) + _nt(dzu_ref[...], wu_ref[...])
        xv = x_ref[...]
        r = lax.rsqrt(jnp.mean(xv * xv, axis=-1, keepdims=True) + EPS)
        xh = xv * r
        dg_ref[...] += jnp.sum(dh * xh, axis=0, keepdims=True)
        dxh = dh * g_ref[...]
        dx_ref[...] = dxo_ref[...] + r * (dxh - xh * jnp.mean(dxh * xh, axis=-1, keepdims=True))

    tile = pl.BlockSpec((tm, D), lambda i: (i, 0))
    vec = pl.BlockSpec((1, D), lambda i: (0, 0))
    ftile = pl.BlockSpec((tm, F), lambda i: (i, 0))
    return pl.pallas_call(
        body, name=name, grid=(T // tm,),
        in_specs=[tile, tile, vec, ftile, ftile, pl.BlockSpec((D, F), lambda i: (0, 0)), pl.BlockSpec((D, F), lambda i: (0, 1))],
        out_specs=[tile, vec],
        out_shape=[jax.ShapeDtypeStruct((T, D), F32), jax.ShapeDtypeStruct((1, D), F32)],
        compiler_params=_params("arbitrary"))(x, dxo, g, dzg, dzu, w_in, w_in)


def _chunks(T, fn):
    def step(c, carry):
        fn(pl.multiple_of(c * SEQ_CHUNK, SEQ_CHUNK))
        return carry
    lax.fori_loop(0, T // SEQ_CHUNK, step, 0)


def _conv_taps(win, ktaps, pad):
    return [(win if k == ktaps - 1 else pltpu.roll(win, ktaps - 1 - k, 0))[pad:, :] for k in range(ktaps)]


def _conv_taps_t(win, ktaps, pad):
    n = win.shape[0]
    return [(win if k == ktaps - 1 else pltpu.roll(win, n - (ktaps - 1 - k), 0))[:n - pad, :] for k in range(ktaps)]


def _col(T, W, idx):
    return pl.BlockSpec((T, W), lambda i, idx=idx: (0, idx))


def _full(shape):
    return pl.BlockSpec(shape, lambda i: (0,) * len(shape))


def mix_a_fwd(z, w, name):
    T, W = z.shape[0], w.shape[1]
    K, P = w.shape[0], 8

    def body(ab_ref, ac_ref, ax_ref, w_ref, y_ref, pp_ref):
        pp_ref[0:P, :] = jnp.zeros((P, W), F32)

        def chunk(s):
            rows = pl.ds(s, SEQ_CHUNK)
            pp_ref[pl.ds(s + P, SEQ_CHUNK), :] = ac_ref[rows, :] * ax_ref[rows, :]
            taps = _conv_taps(pp_ref[pl.ds(s, SEQ_CHUNK + P), :], K, P)
            q = sum(w_ref[k:k + 1, :] * taps[k] for k in range(K))
            y_ref[rows, :] = (ab_ref[rows, :] * q).astype(BF16)

        _chunks(T, chunk)

    return pl.pallas_call(
        body, name=name, grid=(1,),
        in_specs=[_col(T, W, 0), _col(T, W, 1), _col(T, W, 2), _full((K, W))],
        out_specs=_full((T, W)), out_shape=jax.ShapeDtypeStruct((T, W), BF16),
        scratch_shapes=[pltpu.VMEM((T + P, W), F32)],
        compiler_params=_params("arbitrary"))(z, z, z, w)


def mix_a_bwd(z, dy, w, name):
    T, W = z.shape[0], w.shape[1]
    K, P = w.shape[0], 8

    def body(ab_ref, ac_ref, ax_ref, dy_ref, w_ref, dab_ref, dac_ref, dax_ref, dw_ref, pp_ref, dq_ref):
        pp_ref[0:P, :] = jnp.zeros((P, W), F32)
        dq_ref[T:T + P, :] = jnp.zeros((P, W), F32)
        dw_ref[...] = jnp.zeros_like(dw_ref)

        def chunk1(s):
            rows = pl.ds(s, SEQ_CHUNK)
            pp_ref[pl.ds(s + P, SEQ_CHUNK), :] = ac_ref[rows, :] * ax_ref[rows, :]
            taps = _conv_taps(pp_ref[pl.ds(s, SEQ_CHUNK + P), :], K, P)
            q = sum(w_ref[k:k + 1, :] * taps[k] for k in range(K))
            dyv = dy_ref[rows, :]
            dab_ref[rows, :] = (dyv * q).astype(BF16)
            dq = dyv * ab_ref[rows, :]
            dq_ref[rows, :] = dq
            for k in range(K):
                dw_ref[k:k + 1, :] += jnp.sum(dq * taps[k], axis=0, keepdims=True)

        _chunks(T, chunk1)

        def chunk2(s):
            rows = pl.ds(s, SEQ_CHUNK)
            taps = _conv_taps_t(dq_ref[pl.ds(s, SEQ_CHUNK + P), :], K, P)
            dp = sum(w_ref[k:k + 1, :] * taps[k] for k in range(K))
            dac_ref[rows, :] = (dp * ax_ref[rows, :]).astype(BF16)
            dax_ref[rows, :] = (dp * ac_ref[rows, :]).astype(BF16)

        _chunks(T, chunk2)

    tw = jax.ShapeDtypeStruct((T, W), BF16)
    return pl.pallas_call(
        body, name=name, grid=(1,),
        in_specs=[_col(T, W, 0), _col(T, W, 1), _col(T, W, 2), _col(T, W, 0), _full((K, W))],
        out_specs=[_full((T, W))] * 3 + [_full((K, W))],
        out_shape=[tw, tw, tw, jax.ShapeDtypeStruct((K, W), F32)],
        scratch_shapes=[pltpu.VMEM((T + P, W), F32), pltpu.VMEM((T + P, W), F32)],
        compiler_params=_params("arbitrary"))(z, z, z, dy, w)


def _ln_stats(v):
    mu = jnp.mean(v, axis=-1, keepdims=True)
    xc = v - mu
    rstd = lax.rsqrt(jnp.mean(xc * xc, axis=-1, keepdims=True) + EPS)
    return xc * rstd, rstd


def _ln_bwd(dxh, xh, rstd):
    return rstd * (dxh - jnp.mean(dxh, axis=-1, keepdims=True) - xh * jnp.mean(dxh * xh, axis=-1, keepdims=True))


def _tril_bf16(w_ref, h):
    n = w_ref.shape[-1]
    keep = lax.broadcasted_iota(jnp.int32, (n, n), 0) >= lax.broadcasted_iota(jnp.int32, (n, n), 1)
    return jnp.where(keep, w_ref[h], 0.0).astype(BF16)


def mix_b_fwd(z, g, w_s, bias, name):
    T, W = z.shape[0], g.shape[1]
    H, C = w_s.shape[0], w_s.shape[1]
    hd = W // H

    def body(u_ref, v_ref, g_ref, w_ref, b_ref, y_ref):
        wts = [_tril_bf16(w_ref, h) for h in range(H)]
        head = lax.broadcasted_iota(jnp.int32, (C, W), 1) // hd

        def chunk(s):
            rows = pl.ds(s, C)
            xh, _ = _ln_stats(v_ref[rows, :])
            vn = (xh * g_ref[...]).astype(BF16)
            mixed = b_ref[...]
            for h in range(H):
                mixed = mixed + jnp.where(head == h, jnp.dot(wts[h], vn, preferred_element_type=F32), 0.0)
            y_ref[rows, :] = (u_ref[rows, :] * mixed).astype(BF16)

        _chunks(T, chunk)

    return pl.pallas_call(
        body, name=name, grid=(1,),
        in_specs=[_col(T, W, 3), _col(T, W, 4), _full((1, W)), _full((H, C, C)), _full((C, W))],
        out_specs=_full((T, W)), out_shape=jax.ShapeDtypeStruct((T, W), BF16),
        compiler_params=_params("arbitrary"))(z, z, g, w_s, bias)


def mix_b_bwd(z, dy, g, w_s, bias, name):
    T, W = z.shape[0], g.shape[1]
    H, C = w_s.shape[0], w_s.shape[1]
    hd = W // H

    def body(u_ref, v_ref, dy_ref, g_ref, w_ref, b_ref, du_ref, dv_ref, dw_ref, db_ref, dg_ref, dbf_ref):
        wts = [_tril_bf16(w_ref, h) for h in range(H)]
        head = lax.broadcasted_iota(jnp.int32, (C, W), 1) // hd
        dw_ref[...] = jnp.zeros_like(dw_ref)
        dg_ref[...] = jnp.zeros_like(dg_ref)
        dbf_ref[...] = jnp.zeros_like(dbf_ref)

        def chunk(s):
            rows = pl.ds(s, C)
            xh, rstd = _ln_stats(v_ref[rows, :])
            vn = (xh * g_ref[...]).astype(BF16)
            mixed = b_ref[...]
            for h in range(H):
                mixed = mixed + jnp.where(head == h, jnp.dot(wts[h], vn, preferred_element_type=F32), 0.0)
            dyv = dy_ref[rows, :]
            du_ref[rows, :] = (dyv * mixed).astype(BF16)
            dm = dyv * u_ref[rows, :]
            dbf_ref[...] += dm
            dvn = jnp.zeros((C, W), F32)
            for h in range(H):
                dmh = jnp.where(head == h, dm, 0.0).astype(BF16)
                dw_ref[h] += _nt(dmh, vn)
                dvn = dvn + _tn(wts[h], dmh)
            dg_ref[...] += jnp.sum(dvn * xh, axis=0, keepdims=True)
            dv_ref[rows, :] = _ln_bwd(dvn * g_ref[...], xh, rstd).astype(BF16)

        _chunks(T, chunk)

        keep = lax.broadcasted_iota(jnp.int32, (C, C), 0) >= lax.broadcasted_iota(jnp.int32, (C, C), 1)
        lane = lax.broadcasted_iota(jnp.int32, (C, 128), 1)
        db = jnp.zeros((C, 128), F32)
        dbf = dbf_ref[...]
        for h in range(H):
            dw_ref[h] = jnp.where(keep, dw_ref[h], 0.0)
            db = db + jnp.where(lane == h, jnp.sum(jnp.where(head == h, dbf, 0.0), axis=1, keepdims=True), 0.0)
        db_ref[...] = db

    tw = jax.ShapeDtypeStruct((T, W), BF16)
    return pl.pallas_call(
        body, name=name, grid=(1,),
        in_specs=[_col(T, W, 3), _col(T, W, 4), _col(T, W, 1), _full((1, W)), _full((H, C, C)), _full((C, W))],
        out_specs=[_full((T, W)), _full((T, W)), _full((H, C, C)), _full((C, 128)), _full((1, W))],
        out_shape=[tw, tw, jax.ShapeDtypeStruct((H, C, C), F32), jax.ShapeDtypeStruct((C, 128), F32),
                   jax.ShapeDtypeStruct((1, W), F32)],
        scratch_shapes=[pltpu.VMEM((C, W), F32)],
        compiler_params=_params("arbitrary"))(z, z, dy, g, w_s, bias)


def mix_c_fwd(z, w, ln_g, ln_b, name):
    T, W = z.shape[0], w.shape[1]
    K, P = w.shape[0], 32

    def body(a_ref, gt_ref, w_ref, g_ref, b_ref, y_ref, up_ref):
        up_ref[0:P, :] = jnp.zeros((P, W), F32)

        def chunk(s):
            rows = pl.ds(s, SEQ_CHUNK)
            up_ref[pl.ds(s + P, SEQ_CHUNK), :] = a_ref[rows, :] * _sig(gt_ref[rows, :])
            taps = _conv_taps(up_ref[pl.ds(s, SEQ_CHUNK + P), :], K, P)
            q = sum(w_ref[k:k + 1, :] * taps[k] for k in range(K))
            xh, _ = _ln_stats(q)
            r = xh * g_ref[...] + b_ref[...]
            y_ref[rows, :] = (r * _sig(r)).astype(BF16)

        _chunks(T, chunk)

    return pl.pallas_call(
        body, name=name, grid=(1,),
        in_specs=[_col(T, W, 5), _col(T, W, 6), _full((K, W)), _full((1, W)), _full((1, W))],
        out_specs=_full((T, W)), out_shape=jax.ShapeDtypeStruct((T, W), BF16),
        scratch_shapes=[pltpu.VMEM((T + P, W), F32)],
        compiler_params=_params("arbitrary"))(z, z, w, ln_g, ln_b)


def mix_c_bwd(z, dy, w, ln_g, ln_b, name):
    T, W = z.shape[0], w.shape[1]
    K, P = w.shape[0], 32

    def body(a_ref, gt_ref, dy_ref, w_ref, g_ref, b_ref, da_ref, dgt_ref, dw_ref, dg_ref, db_ref, up_ref, dq_ref):
        up_ref[0:P, :] = jnp.zeros((P, W), F32)
        dq_ref[T:T + P, :] = jnp.zeros((P, W), F32)
        dw_ref[...] = jnp.zeros_like(dw_ref)
        dg_ref[...] = jnp.zeros_like(dg_ref)
        db_ref[...] = jnp.zeros_like(db_ref)

        def chunk1(s):
            rows = pl.ds(s, SEQ_CHUNK)
            up_ref[pl.ds(s + P, SEQ_CHUNK), :] = a_ref[rows, :] * _sig(gt_ref[rows, :])
            taps = _conv_taps(up_ref[pl.ds(s, SEQ_CHUNK + P), :], K, P)
            q = sum(w_ref[k:k + 1, :] * taps[k] for k in range(K))
            xh, rstd = _ln_stats(q)
            r = xh * g_ref[...] + b_ref[...]
            sr = _sig(r)
            dr = dy_ref[rows, :] * (sr * (1.0 + r * (1.0 - sr)))
            db_ref[...] += jnp.sum(dr, axis=0, keepdims=True)
            dg_ref[...] += jnp.sum(dr * xh, axis=0, keepdims=True)
            dq = _ln_bwd(dr * g_ref[...], xh, rstd)
            dq_ref[rows, :] = dq
            for k in range(K):
                dw_ref[k:k + 1, :] += jnp.sum(dq * taps[k], axis=0, keepdims=True)

        _chunks(T, chunk1)

        def chunk2(s):
            rows = pl.ds(s, SEQ_CHUNK)
            taps = _conv_taps_t(dq_ref[pl.ds(s, SEQ_CHUNK + P), :], K, P)
            du = sum(w_ref[k:k + 1, :] * taps[k] for k in range(K))
            sg = _sig(gt_ref[rows, :])
            da_ref[rows, :] = (du * sg).astype(BF16)
            dgt_ref[rows, :] = (du * a_ref[rows, :] * sg * (1.0 - sg)).astype(BF16)

        _chunks(T, chunk2)

    tw = jax.ShapeDtypeStruct((T, W), BF16)
    vec = jax.ShapeDtypeStruct((1, W), F32)
    return pl.pallas_call(
        body, name=name, grid=(1,),
        in_specs=[_col(T, W, 5), _col(T, W, 6), _col(T, W, 2), _full((K, W)), _full((1, W)), _full((1, W))],
        out_specs=[_full((T, W)), _full((T, W)), _full((K, W)), _full((1, W)), _full((1, W))],
        out_shape=[tw, tw, jax.ShapeDtypeStruct((K, W), F32), vec, vec],
        scratch_shapes=[pltpu.VMEM((T + P, W), F32), pltpu.VMEM((T + P, W), F32)],
        compiler_params=_params("arbitrary"))(z, z, dy, w, ln_g, ln_b)


def _pool_select(levels, W, rows):
    group = lax.broadcasted_iota(jnp.int32, (rows, W), 1) // (W // len(POOL_WINDOWS))
    out = levels[-1]
    for gi in range(len(POOL_WINDOWS) - 2, -1, -1):
        out = jnp.where(group == gi, levels[gi], out)
    return out


def _pool_count(s, W):
    t = s + lax.broadcasted_iota(jnp.int32, (SEQ_CHUNK, W), 0)
    group = lax.broadcasted_iota(jnp.int32, (SEQ_CHUNK, W), 1) // (W // len(POOL_WINDOWS))
    win = jnp.full((SEQ_CHUNK, W), POOL_WINDOWS[-1], jnp.int32)
    for gi in range(len(POOL_WINDOWS) - 2, -1, -1):
        win = jnp.where(group == gi, POOL_WINDOWS[gi], win)
    return jnp.minimum(t + 1, win).astype(F32)


def _pooled(wp_ref, s, W, P):
    win = wp_ref[pl.ds(s, SEQ_CHUNK + P), :]
    levels, acc, shift = [], win, 1
    for _ in POOL_WINDOWS:
        acc = acc + pltpu.roll(acc, shift, 0)
        levels.append(acc[P:, :])
        shift *= 2
    return _pool_select(levels, W, SEQ_CHUNK) / _pool_count(s, W) - win[P:, :]


def mix_d_fwd(z, pbd, scale, name):
    T, W = z.shape[0], scale.shape[1]
    P = 16

    def body(x_ref, p_ref, s_ref, y_ref, wp_ref):
        wp_ref[0:P, :] = jnp.zeros((P, W), F32)

        def chunk(s):
            rows = pl.ds(s, SEQ_CHUNK)
            wp_ref[pl.ds(s + P, SEQ_CHUNK), :] = x_ref[rows, :]
            pooled = _pooled(wp_ref, s, W, P).astype(BF16)
            y_ref[rows, :] = (jnp.dot(pooled, p_ref[...], preferred_element_type=F32) * s_ref[...]).astype(BF16)

        _chunks(T, chunk)

    return pl.pallas_call(
        body, name=name, grid=(1,),
        in_specs=[_col(T, W, 7), _full((W, W)), _full((1, W))],
        out_specs=_full((T, W)), out_shape=jax.ShapeDtypeStruct((T, W), BF16),
        scratch_shapes=[pltpu.VMEM((T + P, W), F32)],
        compiler_params=_params("arbitrary"))(z, pbd, scale)


def mix_d_bwd(z, dy, pbd, scale, name):
    T, W = z.shape[0], scale.shape[1]
    P = 16

    def body(x_ref, dy_ref, p_ref, s_ref, dx_ref, dp_ref, ds_ref, wp_ref, e_ref, dpool_ref):
        wp_ref[0:P, :] = jnp.zeros((P, W), F32)
        e_ref[T:T + P, :] = jnp.zeros((P, W), F32)
        dp_ref[...] = jnp.zeros_like(dp_ref)
        ds_ref[...] = jnp.zeros_like(ds_ref)

        def chunk1(s):
            rows = pl.ds(s, SEQ_CHUNK)
            wp_ref[pl.ds(s + P, SEQ_CHUNK), :] = x_ref[rows, :]
            pooled = _pooled(wp_ref, s, W, P).astype(BF16)
            yl = jnp.dot(pooled, p_ref[...], preferred_element_type=F32)
            dyv = dy_ref[rows, :]
            ds_ref[...] += jnp.sum(dyv * yl, axis=0, keepdims=True)
            dyl = (dyv * s_ref[...]).astype(BF16)
            dp_ref[...] += _tn(pooled, dyl)
            dpool = _nt(dyl, p_ref[...])
            dpool_ref[rows, :] = dpool
            e_ref[rows, :] = dpool / _pool_count(s, W)

        _chunks(T, chunk1)

        def chunk2(s):
            rows = pl.ds(s, SEQ_CHUNK)
            win = e_ref[pl.ds(s, SEQ_CHUNK + P), :]
            n = SEQ_CHUNK + P
            levels, acc, shift = [], win, 1
            for _ in POOL_WINDOWS:
                acc = acc + pltpu.roll(acc, n - shift, 0)
                levels.append(acc[:SEQ_CHUNK, :])
                shift *= 2
            dx_ref[rows, :] = (_pool_select(levels, W, SEQ_CHUNK) - dpool_ref[rows, :]).astype(BF16)

        _chunks(T, chunk2)

    return pl.pallas_call(
        body, name=name, grid=(1,),
        in_specs=[_col(T, W, 7), _col(T, W, 3), _full((W, W)), _full((1, W))],
        out_specs=[_full((T, W)), _full((W, W)), _full((1, W))],
        out_shape=[jax.ShapeDtypeStruct((T, W), BF16), jax.ShapeDtypeStruct((W, W), F32),
                   jax.ShapeDtypeStruct((1, W), F32)],
        scratch_shapes=[pltpu.VMEM((T + P, W), F32), pltpu.VMEM((T + P, W), F32), pltpu.VMEM((T, W), F32)],
        compiler_params=_params("arbitrary"))(z, dy, pbd, scale)


def attn_fwd(q, kv, name):
    T, D = q.shape
    M = kv.shape[0]
    hd = D // N_HEADS
    tm = _pick(T, 512, 8)
    sc = 1.0 / math.sqrt(hd)

    def body(q_ref, k_ref, v_ref, o_ref):
        for h in range(N_HEADS):
            cols = slice(h * hd, (h + 1) * hd)
            s = _nt(q_ref[:, cols].astype(BF16), k_ref[:, cols].astype(BF16)) * sc
            p = jnp.exp(s - jnp.max(s, axis=-1, keepdims=True))
            p = p / jnp.sum(p, axis=-1, keepdims=True)
            o_ref[:, cols] = jnp.dot(p.astype(BF16), v_ref[:, cols].astype(BF16),
                                     preferred_element_type=F32).astype(BF16)

    return pl.pallas_call(
        body, name=name, grid=(T // tm,),
        in_specs=[pl.BlockSpec((tm, D), lambda i: (i, 0)), pl.BlockSpec((M, D), lambda i: (0, 0)),
                  pl.BlockSpec((M, D), lambda i: (0, 1))],
        out_specs=pl.BlockSpec((tm, D), lambda i: (i, 0)),
        out_shape=jax.ShapeDtypeStruct((T, D), BF16),
        compiler_params=_params("parallel"))(q, kv, kv)


def attn_bwd(q, kv, do, name):
    T, D = q.shape
    M = kv.shape[0]
    hd = D // N_HEADS
    tm = _pick(T, 512, 8)
    sc = 1.0 / math.sqrt(hd)

    def body(q_ref, k_ref, v_ref, do_ref, dq_ref, dk_ref, dv_ref):
        i = pl.program_id(0)

        @pl.when(i == 0)
        def _():
            dk_ref[...] = jnp.zeros_like(dk_ref)
            dv_ref[...] = jnp.zeros_like(dv_ref)

        for h in range(N_HEADS):
            cols = slice(h * hd, (h + 1) * hd)
            qh, kh = q_ref[:, cols].astype(BF16), k_ref[:, cols].astype(BF16)
            vh, doh = v_ref[:, cols].astype(BF16), do_ref[:, cols].astype(BF16)
            s = _nt(qh, kh) * sc
            p = jnp.exp(s - jnp.max(s, axis=-1, keepdims=True))
            p = p / jnp.sum(p, axis=-1, keepdims=True)
            dp = _nt(doh, vh)
            dv_ref[:, cols] += _tn(p.astype(BF16), doh)
            ds = (p * (dp - jnp.sum(dp * p, axis=-1, keepdims=True)) * sc).astype(BF16)
            dq_ref[:, cols] = jnp.dot(ds, kh, preferred_element_type=F32).astype(BF16)
            dk_ref[:, cols] += _tn(ds, qh)

    tile = pl.BlockSpec((tm, D), lambda i: (i, 0))
    mem = jax.ShapeDtypeStruct((M, D), F32)
    return pl.pallas_call(
        body, name=name, grid=(T // tm,),
        in_specs=[tile, pl.BlockSpec((M, D), lambda i: (0, 0)), pl.BlockSpec((M, D), lambda i: (0, 1)), tile],
        out_specs=[tile, pl.BlockSpec((M, D), lambda i: (0, 0)), pl.BlockSpec((M, D), lambda i: (0, 0))],
        out_shape=[jax.ShapeDtypeStruct((T, D), BF16), mem, mem],
        compiler_params=_params("arbitrary"))(q, kv, kv, do)


def loss_head(x, g, target, name):
    T, D = x.shape
    tm = _pick(T, 512, 8)

    def body(x_ref, g_ref, t_ref, l_ref, dx_ref, dg_ref):
        i = pl.program_id(0)

        @pl.when(i == 0)
        def _():
            l_ref[...] = jnp.zeros_like(l_ref)
            dg_ref[...] = jnp.zeros_like(dg_ref)

        xv = x_ref[...]
        r = lax.rsqrt(jnp.mean(xv * xv, axis=-1, keepdims=True) + EPS)
        xh = xv * r
        err = xh * g_ref[...] - t_ref[...]
        l_ref[...] += 0.5 * jnp.sum(jnp.mean(err * err, axis=-1, keepdims=True), axis=0, keepdims=True)
        dy = err * (1.0 / D)
        dg_ref[...] += jnp.sum(dy * xh, axis=0, keepdims=True)
        dxh = dy * g_ref[...]
        dx_ref[...] = r * (dxh - xh * jnp.mean(dxh * xh, axis=-1, keepdims=True))

    tile = pl.BlockSpec((tm, D), lambda i: (i, 0))
    vec = pl.BlockSpec((1, D), lambda i: (0, 0))
    return pl.pallas_call(
        body, name=name, grid=(T // tm,),
        in_specs=[tile, vec, tile],
        out_specs=[pl.BlockSpec((1, 128), lambda i: (0, 0)), tile, vec],
        out_shape=[jax.ShapeDtypeStruct((1, 128), F32), jax.ShapeDtypeStruct((T, D), F32),
                   jax.ShapeDtypeStruct((1, D), F32)],
        compiler_params=_params("arbitrary"))(x, g, target)


def _block_diag(p):
    G, gd, _ = p.shape
    rows = [jnp.concatenate([p[g] if g == c else jnp.zeros((gd, gd), p.dtype) for c in range(G)], axis=1)
            for g in range(G)]
    return jnp.concatenate(rows, axis=0)


class _LazyWeight:
    def __init__(self, fetch, name, latest):
        self.fetch, self.name, self.latest = fetch, name, latest

    def __getitem__(self, l):
        return self.fetch(self.name, l, self.latest[0])


def _local_step(x, mem, target, fetch, ws, L, progress=lambda event, l, grads, values: values):
    T, D = x.shape
    W = D // 4
    H = ws["sgu_w"].shape[1]
    row = lambda v: v.reshape(1, -1)
    latest = [x]
    wb = {n: _LazyWeight(fetch, n, latest) for n in BIG}
    saved = []
    for l in range(L):
        s = {"x0": x}
        latest[0] = x
        x, *s["ffn1"] = ffn_fwd(x, row(ws["norm_ffn1"][l]), wb["ffn1_w_in"][l], wb["ffn1_w_out"][l], "ffn_fwd")
        s["x1"] = x
        latest[0] = x
        z, s["h_mix"] = norm_matmul(x, row(ws["norm_mix"][l]), wb["mix_w_in"][l], "mix_in")
        s["z"] = z
        s["bias"] = jnp.repeat(ws["sgu_b"][l].T, W // H, axis=1)
        s["pbd"] = _block_diag(ws["pool_w"][l]).astype(BF16)
        y = jnp.concatenate([
            mix_a_fwd(z, ws["sconv_w"][l], "mix_a_fwd"),
            mix_b_fwd(z, row(ws["sgu_norm_g"][l]), ws["sgu_w"][l], s["bias"], "mix_b_fwd"),
            mix_c_fwd(z, ws["cconv_w"][l], row(ws["cconv_ln_g"][l]), row(ws["cconv_ln_b"][l]), "mix_c_fwd"),
            mix_d_fwd(z, s["pbd"], row(ws["pool_scale"][l]), "mix_d_fwd")], axis=1)
        s["y"] = y
        x = matmul_res(x, y, wb["mix_w_out"][l], "mix_out")
        s["x2"] = x
        s["q"], s["hq"] = norm_matmul(x, row(ws["norm_xattn"][l]), wb["xattn_wq"][l], "attn_q", out_dtype=BF16)
        s["kv"], s["mn"] = norm_matmul(mem, row(ws["norm_mem"][l]), wb["xattn_wkv"][l], "attn_kv")
        s["o"] = attn_fwd(s["q"], s["kv"], "attn_fwd")
        x = matmul_res(x, s["o"], wb["xattn_wo"][l], "attn_out")
        s["x3"] = x
        x, *s["ffn2"] = ffn_fwd(x, row(ws["norm_ffn2"][l]), wb["ffn2_w_in"][l], wb["ffn2_w_out"][l], "ffn_fwd")
        saved.append(s)

    loss, dx, dg_final = loss_head(x, row(ws["norm_final"]), target, "loss_head")
    grads = {n: [None] * L for n in WEIGHTS if n != "norm_final"}
    grads["norm_final"] = dg_final.reshape(-1)

    def pin(dx, names, l):
        dx, made = lax.optimization_barrier((dx, [grads[n][l] for n in names]))
        for n, g in zip(names, made):
            grads[n][l] = g
        return dx

    def after_stages(event, l, values):
        return progress(event, l, grads, values)

    def ffn_back(xin, kept, dxo, gname, win, wout, l, event):
        h, zg, zu = kept
        a, dzg, dzu = ffn_dz(dxo, zg, zu, wb[wout][l], "ffn_dz")
        dxn, dg = ffn_dh(xin, dxo, row(ws[gname][l]), dzg, dzu, wb[win][l], "ffn_dh")
        grads[gname][l] = dg.reshape(-1)
        dxn, h, a = after_stages(event, l, (dxn, h, a))
        grads[win][l] = matmul_tn(h, dzg, 1.0, "ffn_dwin", b2=dzu)
        grads[wout][l] = matmul_tn(a, dxo, 0.5, "ffn_dwout")
        return pin(dxn, (win, wout), l)

    for l in reversed(range(L)):
        s = saved[l]
        dx = ffn_back(s["x3"], s["ffn2"], dx, "norm_ffn2", "ffn2_w_in", "ffn2_w_out", l, "ffn2_mid")
        dx, = after_stages("ffn2", l, (dx,))
        grads["xattn_wo"][l] = matmul_tn(s["o"], dx, 1.0, "dw_sq")
        do = matmul_nt(dx, wb["xattn_wo"][l], "attn_do", out_dtype=BF16)
        dq, dk, dv = attn_bwd(s["q"], s["kv"], do, "attn_bwd")
        grads["xattn_wq"][l] = matmul_tn(s["hq"], dq, 1.0, "dw_sq")
        dhq = matmul_nt(dq, wb["xattn_wq"][l], "attn_dhq")
        dx, dg = rmsnorm_bwd(dx, dhq, s["x2"], row(ws["norm_xattn"][l]), "norm_bwd")
        grads["norm_xattn"][l] = dg.reshape(-1)
        dkv = jnp.concatenate([dk, dv], axis=1)
        grads["xattn_wkv"][l] = matmul_tn(s["mn"], dkv, 1.0, "attn_dwkv")
        dmn = matmul_nt(dkv, wb["xattn_wkv"][l], "attn_dmn")
        _, dg = rmsnorm_bwd(None, dmn, mem, row(ws["norm_mem"][l]), "norm_mem_bwd")
        grads["norm_mem"][l] = dg.reshape(-1)
        dx = pin(dx, ("xattn_wo", "xattn_wq", "xattn_wkv", "norm_mem"), l)
        dx, = after_stages("attn", l, (dx,))
        grads["mix_w_out"][l] = matmul_tn(s["y"], dx, 1.0, "dw_sq")
        dy = matmul_nt(dx, wb["mix_w_out"][l], "mix_dy")
        z = s["z"]
        dab, dac, dax, dws = mix_a_bwd(z, dy, ws["sconv_w"][l], "mix_a_bwd")
        dbu, dbv, dwsgu, dbs, dgs = mix_b_bwd(z, dy, row(ws["sgu_norm_g"][l]), ws["sgu_w"][l], s["bias"], "mix_b_bwd")
        dca, dcg, dwc, dgc, dbc = mix_c_bwd(z, dy, ws["cconv_w"][l], row(ws["cconv_ln_g"][l]),
                                            row(ws["cconv_ln_b"][l]), "mix_c_bwd")
        ddw, dpbd, dsc = mix_d_bwd(z, dy, s["pbd"], row(ws["pool_scale"][l]), "mix_d_bwd")
        grads["sconv_w"][l], grads["cconv_w"][l] = dws, dwc
        grads["sgu_w"][l], grads["sgu_b"][l], grads["sgu_norm_g"][l] = dwsgu, dbs[:, :H].T, dgs.reshape(-1)
        grads["cconv_ln_g"][l], grads["cconv_ln_b"][l] = dgc.reshape(-1), dbc.reshape(-1)
        gd = W // len(POOL_WINDOWS)
        grads["pool_w"][l] = jnp.stack([dpbd[g * gd:(g + 1) * gd, g * gd:(g + 1) * gd] for g in range(len(POOL_WINDOWS))])
        grads["pool_scale"][l] = dsc.reshape(-1)
        dz = jnp.concatenate([dab, dac, dax, dbu, dbv, dca, dcg, ddw], axis=1)
        grads["mix_w_in"][l] = matmul_tn(s["h_mix"], dz, 1.0, "mix_dwin")
        dh = matmul_nt(dz, wb["mix_w_in"][l], "mix_dh")
        dx, dg = rmsnorm_bwd(dx, dh, s["x1"], row(ws["norm_mix"][l]), "norm_bwd")
        grads["norm_mix"][l] = dg.reshape(-1)
        dx = pin(dx, ("mix_w_out", "mix_w_in"), l)
        dx, = after_stages("mix", l, (dx,))
        dx = ffn_back(s["x0"], s["ffn1"], dx, "norm_ffn1", "ffn1_w_in", "ffn1_w_out", l, "ffn1_mid")
        dx, = after_stages("layer", l, (dx,))

    return loss[0, 0], dx, grads


ANY = pl.BlockSpec(memory_space=pl.ANY)


def _other_chips(x, y):
    return [(1 - x, y), (x, 1 - y), (1 - x, 1 - y)]


def _shard_slice(ref, axis, chip, size):
    idx = [slice(None)] * len(ref.shape)
    idx[axis] = pl.ds(pl.multiple_of(chip * size, size), size)
    return ref.at[tuple(idx)]


def all_gather_chips(shards, axes, name):
    n = len(shards)

    def body(*refs):
        ins, outs = refs[:n], refs[n:2 * n]
        send, recv, loc = refs[2 * n:]
        x, y, c = lax.axis_index("x"), lax.axis_index("y"), lax.axis_index("c")
        me = 2 * x + y
        chips = _other_chips(x, y)
        started = []
        for i in range(n):
            size = ins[i].shape[axes[i]]
            cp = pltpu.make_async_copy(ins[i], _shard_slice(outs[i], axes[i], me, size), loc.at[i])
            cp.start()
            started.append(cp)
        sends = []
        for i in range(n):
            size = ins[i].shape[axes[i]]
            for j, (px, py) in enumerate(chips):
                cp = pltpu.make_async_remote_copy(
                    src_ref=ins[i], dst_ref=_shard_slice(outs[i], axes[i], me, size),
                    send_sem=send.at[i, j], recv_sem=recv.at[i, j], device_id=(px, py, c), device_id_type=MESH_ID)
                cp.start()
                sends.append(cp)
        for i in range(n):
            size = ins[i].shape[axes[i]]
            for j, (px, py) in enumerate(chips):
                pltpu.make_async_remote_copy(
                    src_ref=ins[i], dst_ref=_shard_slice(outs[i], axes[i], 2 * px + py, size),
                    send_sem=send.at[i, j], recv_sem=recv.at[i, j], device_id=(px, py, c),
                    device_id_type=MESH_ID).wait_recv()
        for cp in sends:
            cp.wait_send()
        for cp in started:
            cp.wait()

    def full(a, ax):
        shape = list(a.shape)
        shape[ax] *= N_CHIPS
        return jax.ShapeDtypeStruct(tuple(shape), a.dtype)

    return pl.pallas_call(
        body, name=name, in_specs=[ANY] * n, out_specs=[ANY] * n,
        out_shape=[full(a, ax) for a, ax in zip(shards, axes)],
        scratch_shapes=[pltpu.SemaphoreType.DMA((n, 3)), pltpu.SemaphoreType.DMA((n, 3)),
                        pltpu.SemaphoreType.DMA((n,))],
        compiler_params=pltpu.CompilerParams(has_side_effects=True))(*shards)


def cast_into_slot(shard, axis, chip, name):
    L, K, N = shard.shape
    bm = _pick(K, 256, 16)
    full = (K * N_CHIPS, N) if axis == 1 else (K, N * N_CHIPS)
    nb = K // bm

    def body(c_ref, s_ref, *o_refs):
        for l in range(L):
            o_refs[l][...] = s_ref[l].astype(BF16)

    out_map = (lambda i, c: (c[0] * nb + i, 0)) if axis == 1 else (lambda i, c: (i, c[0]))
    spec = pltpu.PrefetchScalarGridSpec(
        num_scalar_prefetch=1, grid=(nb,),
        in_specs=[pl.BlockSpec((L, bm, N), lambda i, c: (0, i, 0))],
        out_specs=[pl.BlockSpec((bm, N), out_map)] * L)
    return pl.pallas_call(body, name=name, grid_spec=spec, out_shape=[jax.ShapeDtypeStruct(full, BF16)] * L,
                          compiler_params=_params("parallel"))(chip, shard)


HBM = pl.BlockSpec(memory_space=pltpu.HBM)
SEM = pl.BlockSpec(memory_space=pltpu.SEMAPHORE)
DATAFLOW = pltpu.SideEffectType.DATAFLOW_SIDE_EFFECTING


def split_start(name, sources, landing, n_sems, make, after):
    ns, nl, na = len(sources), len(landing), len(after)

    def body(*refs):
        out, _ = make(refs[:ns], refs[ns:ns + nl], refs[ns + nl + na], refs[ns + nl + na + 1])
        for cp in out:
            cp.start()
        refs[-1][...] = jnp.zeros_like(refs[-1])

    hbm = lambda b: pltpu.with_memory_space_constraint(b, pltpu.HBM)
    res = pl.pallas_call(
        body, name=name,
        out_shape=(pltpu.SemaphoreType.DMA((n_sems,)), pltpu.SemaphoreType.DMA((n_sems,)),
                   *[pltpu.HBM(b.shape, b.dtype) for b in landing], jax.ShapeDtypeStruct((8, 128), F32)),
        in_specs=[HBM] * (ns + nl) + [ANY] * na, out_specs=(SEM, SEM, *[HBM] * nl, pl.BlockSpec(memory_space=pltpu.VMEM)),
        input_output_aliases={ns + i: 2 + i for i in range(nl)},
        compiler_params=pltpu.CompilerParams(has_side_effects=DATAFLOW))(
            *[hbm(b) for b in sources], *[hbm(b) for b in landing], *after)
    return res[0], res[1], list(res[2:2 + nl]), res[-1]


def split_wait(name, sources, landing, send, recv, make, after):
    ns, nl = len(sources), len(landing)

    def body(*refs):
        _, back = make(refs[:ns], refs[ns:ns + nl], refs[ns + nl], refs[ns + nl + 1])
        for cp in back:
            cp.wait_send()
            cp.wait_recv()

    return list(pl.pallas_call(
        body, name=name, out_shape=tuple(pltpu.HBM(b.shape, b.dtype) for b in landing),
        in_specs=[HBM] * (ns + nl) + [SEM, SEM] + [ANY] * len(after), out_specs=tuple([HBM] * nl),
        input_output_aliases={ns + i: i for i in range(nl)},
        compiler_params=pltpu.CompilerParams(has_side_effects=DATAFLOW))(
            *[pltpu.with_memory_space_constraint(b, pltpu.HBM) for b in sources], *landing, send, recv, *after))


def _half_slot(buf, axis, chip, half):
    K, N = buf.shape
    if axis == 1:
        n = N // N_CHIPS
        return buf.at[pl.ds(pl.multiple_of(half * (K // 2), K // 2), K // 2), pl.ds(pl.multiple_of(chip * n, n), n)]
    k2 = K // N_CHIPS // 2
    return buf.at[pl.ds(pl.multiple_of((2 * chip + half) * k2, k2), k2), :]


def gather_copies(axes):
    def make(_, bufs, send, recv):
        x, y, c = lax.axis_index("x"), lax.axis_index("y"), lax.axis_index("c")
        out, back = [], []
        for i, (buf, ax) in enumerate(zip(bufs, axes)):
            mine = _half_slot(buf, ax, 2 * x + y, c)
            for j, (px, py) in enumerate(_other_chips(x, y)):
                kw = dict(send_sem=send.at[3 * i + j], recv_sem=recv.at[3 * i + j], device_id=(px, py, c),
                          device_id_type=MESH_ID)
                out.append(pltpu.make_async_remote_copy(src_ref=mine, dst_ref=mine, **kw))
                back.append(pltpu.make_async_remote_copy(src_ref=mine, dst_ref=_half_slot(buf, ax, 2 * px + py, c), **kw))
        return out, back
    return make


def forward_sibling(bufs, axes, name):
    n = len(bufs)

    def body(*refs):
        ins = refs[:n]
        send, recv = refs[2 * n:]
        x, y, c = lax.axis_index("x"), lax.axis_index("y"), lax.axis_index("c")
        out, back = [], []
        for i, ax in enumerate(axes):
            for j, (px, py) in enumerate(_other_chips(x, y)):
                have = _half_slot(ins[i], ax, 2 * px + py, c)
                kw = dict(send_sem=send.at[3 * i + j], recv_sem=recv.at[3 * i + j], device_id=(x, y, 1 - c),
                          device_id_type=MESH_ID)
                out.append(pltpu.make_async_remote_copy(src_ref=have, dst_ref=have, **kw))
                back.append(pltpu.make_async_remote_copy(src_ref=have, dst_ref=_half_slot(ins[i], ax, 2 * px + py, 1 - c), **kw))
        for cp in out:
            cp.start()
        for cp in back:
            cp.wait_recv()
        for cp in out:
            cp.wait_send()

    return pl.pallas_call(
        body, name=name, in_specs=[ANY] * n, out_specs=[ANY] * n,
        out_shape=[jax.ShapeDtypeStruct(b.shape, b.dtype) for b in bufs],
        input_output_aliases={i: i for i in range(n)},
        scratch_shapes=[pltpu.SemaphoreType.DMA((3 * n,)), pltpu.SemaphoreType.DMA((3 * n,))],
        compiler_params=pltpu.CompilerParams(has_side_effects=True))(*bufs)


def all_reduce_small(p, name):
    R = p.shape[0]

    def body(p_ref, o_ref, sib_ref, chip_ref, send, recv):
        x, y, c = lax.axis_index("x"), lax.axis_index("y"), lax.axis_index("c")
        me = 2 * x + y
        chips = _other_chips(x, y)
        pair = pltpu.make_async_remote_copy(src_ref=p_ref, dst_ref=sib_ref, send_sem=send.at[0], recv_sem=recv.at[0],
                                            device_id=(x, y, 1 - c), device_id_type=MESH_ID)
        pair.start()
        pair.wait()
        chip_ref[me] = p_ref[...] + sib_ref[...]
        sends = []
        for j, (px, py) in enumerate(chips):
            cp = pltpu.make_async_remote_copy(src_ref=chip_ref.at[me], dst_ref=chip_ref.at[me], send_sem=send.at[1 + j],
                                              recv_sem=recv.at[1 + j], device_id=(px, py, c), device_id_type=MESH_ID)
            cp.start()
            sends.append(cp)
        for j, (px, py) in enumerate(chips):
            pltpu.make_async_remote_copy(src_ref=chip_ref.at[me], dst_ref=chip_ref.at[2 * px + py], send_sem=send.at[1 + j],
                                         recv_sem=recv.at[1 + j], device_id=(px, py, c), device_id_type=MESH_ID).wait_recv()
        for cp in sends:
            cp.wait_send()
        o_ref[...] = ((chip_ref[0] + chip_ref[1]) + chip_ref[2]) + chip_ref[3]

    vm = pl.BlockSpec(memory_space=pltpu.VMEM)
    return pl.pallas_call(
        body, name=name, in_specs=[vm], out_specs=vm, out_shape=jax.ShapeDtypeStruct((R, 128), F32),
        scratch_shapes=[pltpu.VMEM((R, 128), F32), pltpu.VMEM((N_CHIPS, R, 128), F32),
                        pltpu.SemaphoreType.DMA((4,)), pltpu.SemaphoreType.DMA((4,))],
        compiler_params=pltpu.CompilerParams(has_side_effects=True, vmem_limit_bytes=VMEM_LIMIT))(p)


def _grad_view(g, axis):
    K, N = g.shape
    return g.reshape(1, 2, K // 2, N) if axis == 1 else g.reshape(N_CHIPS, 2, K // N_CHIPS // 2, N)


def pair_copies(gvs, others, send, recv):
    x, y, c = lax.axis_index("x"), lax.axis_index("y"), lax.axis_index("c")
    out = [pltpu.make_async_remote_copy(src_ref=gv.at[:, 1 - c], dst_ref=o, send_sem=send.at[i], recv_sem=recv.at[i],
                                        device_id=(x, y, 1 - c), device_id_type=MESH_ID)
           for i, (gv, o) in enumerate(zip(gvs, others))]
    return out, out


def chip_copies(axes):
    def piece(s, ax, chip):
        if ax == 1:
            n = s.shape[2] // N_CHIPS
            return s.at[0, :, pl.ds(pl.multiple_of(chip * n, n), n)]
        return s.at[chip]

    def make(sums, qs, send, recv):
        x, y, c = lax.axis_index("x"), lax.axis_index("y"), lax.axis_index("c")
        out = []
        for i, (s, q, ax) in enumerate(zip(sums, qs, axes)):
            for j, (px, py) in enumerate(_other_chips(x, y)):
                out.append(pltpu.make_async_remote_copy(
                    src_ref=piece(s, ax, 2 * px + py), dst_ref=q.at[j], send_sem=send.at[3 * i + j],
                    recv_sem=recv.at[3 * i + j], device_id=(px, py, c), device_id_type=MESH_ID))
        return out, out
    return make


def share_sibling(halves, name):
    n = len(halves)

    def body(*refs):
        ins = refs[:n]
        send, recv = refs[2 * n:]
        x, y, c = lax.axis_index("x"), lax.axis_index("y"), lax.axis_index("c")
        cps = [pltpu.make_async_remote_copy(src_ref=ins[i].at[c], dst_ref=ins[i].at[c], send_sem=send.at[i], recv_sem=recv.at[i],
                                            device_id=(x, y, 1 - c), device_id_type=MESH_ID) for i in range(n)]
        for cp in cps:
            cp.start()
        for i in range(n):
            pltpu.make_async_remote_copy(src_ref=ins[i].at[c], dst_ref=ins[i].at[1 - c], send_sem=send.at[i], recv_sem=recv.at[i],
                                         device_id=(x, y, 1 - c), device_id_type=MESH_ID).wait_recv()
        for cp in cps:
            cp.wait_send()

    return pl.pallas_call(
        body, name=name, in_specs=[ANY] * n, out_specs=[ANY] * n,
        out_shape=[jax.ShapeDtypeStruct(a.shape, a.dtype) for a in halves],
        input_output_aliases={i: i for i in range(n)},
        scratch_shapes=[pltpu.SemaphoreType.DMA((n,)), pltpu.SemaphoreType.DMA((n,))],
        compiler_params=pltpu.CompilerParams(has_side_effects=True))(*halves)


def add_pair(gv, other, place, name):
    A, _, rows, N = gv.shape
    bm, bn = _pick(rows, 256, 16), _pick(N, 1408, 128)

    def body(p_ref, g_ref, o_ref, out_ref):
        out_ref[...] = (g_ref[...] + o_ref[...]).astype(GRAD_WIRE)

    spec = pltpu.PrefetchScalarGridSpec(
        num_scalar_prefetch=1, grid=(A, rows // bm, N // bn),
        in_specs=[pl.BlockSpec((None, None, bm, bn), lambda a, i, j, p: (a, p[1], i, j)),
                  pl.BlockSpec((None, bm, bn), lambda a, i, j, p: (a, i, j))],
        out_specs=pl.BlockSpec((None, bm, bn), lambda a, i, j, p: (a, i, j)))
    return pl.pallas_call(body, name=name, grid_spec=spec, out_shape=jax.ShapeDtypeStruct((A, rows, N), GRAD_WIRE),
                          compiler_params=_params("parallel", "parallel", "parallel"))(place, gv, other)


def add_chips(s, q, axis, place, name):
    _, rows, n = q.shape
    bm, bn = _pick(rows, 256, 16), _pick(n, 1408, 128)
    nbj = n // bn

    def body(p_ref, s_ref, q_ref, o_ref):
        o_ref[...] = ((s_ref[...].astype(F32) + q_ref[0].astype(F32)) + q_ref[1].astype(F32)) + q_ref[2].astype(F32)

    mine = (lambda i, j, p: (p[0], i, j)) if axis == 0 else (lambda i, j, p: (0, i, p[0] * nbj + j))
    spec = pltpu.PrefetchScalarGridSpec(
        num_scalar_prefetch=1, grid=(rows // bm, nbj),
        in_specs=[pl.BlockSpec((None, bm, bn), mine), pl.BlockSpec((3, bm, bn), lambda i, j, p: (0, i, j))],
        out_specs=pl.BlockSpec((None, bm, bn), lambda i, j, p: (p[1], i, j)))
    return pl.pallas_call(body, name=name, grid_spec=spec, out_shape=jax.ShapeDtypeStruct((2, rows, n), F32),
                          compiler_params=_params("parallel", "parallel"))(place, s, q)


def adamw(w, g, m, v, name):
    R, N = w.shape
    bm = _pick(R, 256, 8)
    c1 = 1.0 / (1.0 - ADAM_B1 ** ADAM_STEP)
    c2 = 1.0 / (1.0 - ADAM_B2 ** ADAM_STEP)

    def body(w_ref, g_ref, m_ref, v_ref, d_ref, nm_ref, nv_ref):
        gv = g_ref[...]
        nm = ADAM_B1 * m_ref[...] + (1.0 - ADAM_B1) * gv
        nv = ADAM_B2 * v_ref[...] + (1.0 - ADAM_B2) * (gv * gv)
        nm_ref[...] = nm
        nv_ref[...] = nv
        d_ref[...] = -ADAM_LR * ((nm * c1) / (jnp.sqrt(nv * c2) + ADAM_EPS) + ADAM_WD * w_ref[...])

    blk = pl.BlockSpec((bm, N), lambda i: (i, 0))
    out = jax.ShapeDtypeStruct((R, N), F32)
    return pl.pallas_call(body, name=name, grid=(R // bm,), in_specs=[blk] * 4, out_specs=[blk] * 3,
                          out_shape=[out, out, out], compiler_params=_params("parallel"))(w, g, m, v)


def adamw_layers(w, g0, g1, m, v, name):
    _, k, n = w.shape
    bm = _pick(k, 256, 8)
    c1 = 1.0 / (1.0 - ADAM_B1 ** ADAM_STEP)
    c2 = 1.0 / (1.0 - ADAM_B2 ** ADAM_STEP)

    def body(w_ref, g0_ref, g1_ref, m_ref, v_ref, g_ref, d_ref, nm_ref, nv_ref):
        def step(gv):
            nm = ADAM_B1 * m_ref[...] + (1.0 - ADAM_B1) * gv
            nv = ADAM_B2 * v_ref[...] + (1.0 - ADAM_B2) * (gv * gv)
            g_ref[...] = gv
            nm_ref[...] = nm
            nv_ref[...] = nv
            d_ref[...] = -ADAM_LR * ((nm * c1) / (jnp.sqrt(nv * c2) + ADAM_EPS) + ADAM_WD * w_ref[...])

        @pl.when(pl.program_id(0) == 0)
        def _():
            step(g0_ref[...])

        @pl.when(pl.program_id(0) == 1)
        def _():
            step(g1_ref[...])

    blk = pl.BlockSpec((None, bm, n), lambda l, i: (l, i, 0))
    out = jax.ShapeDtypeStruct(w.shape, F32)
    return pl.pallas_call(
        body, name=name, grid=(2, k // bm),
        in_specs=[blk, pl.BlockSpec((bm, n), lambda l, i: (i * (1 - l), 0)), pl.BlockSpec((bm, n), lambda l, i: (i * l, 0)),
                  blk, blk],
        out_specs=[blk] * 4, out_shape=[out] * 4, compiler_params=_params("arbitrary", "arbitrary"))(w, g0, g1, m, v)


def _pack(arrays):
    flat = jnp.concatenate([a.reshape(-1) for a in arrays])
    rows = -(-flat.shape[0] // (256 * 128)) * 256
    return jnp.pad(flat, (0, rows * 128 - flat.shape[0])).reshape(rows, 128)


def _unpack(p, shapes):
    flat, out, at = p.reshape(-1), [], 0
    for s in shapes:
        n = math.prod(s)
        out.append(flat[at:at + n].reshape(s))
        at += n
    return out


def kernel(x, mem, norm_ffn1, ffn1_w_in, ffn1_w_out, norm_mix, mix_w_in, sconv_w, sgu_norm_g, sgu_w, sgu_b, cconv_w, cconv_ln_g, cconv_ln_b, pool_w, pool_scale, mix_w_out, norm_xattn, norm_mem, xattn_wq, xattn_wkv, xattn_wo, norm_ffn2, ffn2_w_in, ffn2_w_out, norm_final, loss_target, m_norm_ffn1, m_ffn1_w_in, m_ffn1_w_out, m_norm_mix, m_mix_w_in, m_sconv_w, m_sgu_norm_g, m_sgu_w, m_sgu_b, m_cconv_w, m_cconv_ln_g, m_cconv_ln_b, m_pool_w, m_pool_scale, m_mix_w_out, m_norm_xattn, m_norm_mem, m_xattn_wq, m_xattn_wkv, m_xattn_wo, m_norm_ffn2, m_ffn2_w_in, m_ffn2_w_out, m_norm_final, v_norm_ffn1, v_ffn1_w_in, v_ffn1_w_out, v_norm_mix, v_mix_w_in, v_sconv_w, v_sgu_norm_g, v_sgu_w, v_sgu_b, v_cconv_w, v_cconv_ln_g, v_cconv_ln_b, v_pool_w, v_pool_scale, v_mix_w_out, v_norm_xattn, v_norm_mem, v_xattn_wq, v_xattn_wkv, v_xattn_wo, v_norm_ffn2, v_ffn2_w_in, v_ffn2_w_out, v_norm_final):
    given = dict(locals())
    w = {n: given[n] for n in WEIGHTS}
    L = ffn1_w_in.shape[0]
    assert L == 2, "the reduce-scatter gives one layer to each core of a chip"
    chip = 2 * lax.axis_index("x") + lax.axis_index("y")
    chip1 = chip.astype(jnp.int32).reshape(1)
    core = lax.axis_index("c").astype(jnp.int32).reshape(1)
    place = jnp.concatenate([chip1, core])

    axis = {n: 1 if n in COL_SHARDED else 0 for n in BIG}
    bufs = {}
    for n in BIG:
        for l, b in enumerate(cast_into_slot(w[n], axis[n] + 1, chip1, "cast_weights")):
            bufs[n, l] = b
    groups = {"a": [(n, 0) for n in BIG[:2]], "b": [(n, 0) for n in BIG[2:]], "c": [(n, 1) for n in BIG]}
    wc = sconv_w.shape[-1]
    conv_rows = [w[n].reshape(-1, wc) for n in SMALL_CONV]
    n_conv = sum(r.shape[0] for r in conv_rows)
    conv_pack = jnp.pad(jnp.concatenate(conv_rows, axis=0), ((0, -n_conv % 8), (0, 128 - wc)))[None]
    conv_all = all_gather_chips([conv_pack], [0], "gather_conv")[0]
    started, token = {}, conv_all
    for g, keys in groups.items():
        send, recv, thru, token = split_start("gather_start_" + g, [], [bufs[k] for k in keys], 3 * len(keys),
                                              gather_copies([axis[k[0]] for k in keys]), [token])
        started[g] = (send, recv, thru)
    ready = {}

    def fetch(n, l, after):
        g = next(g for g, keys in groups.items() if (n, l) in keys)
        if g not in ready:
            send, recv, thru = started[g]
            axes = [axis[k[0]] for k in groups[g]]
            done = split_wait("gather_wait_" + g, [], thru, send, recv, gather_copies(axes), [token if g == "a" else after])
            ready[g] = dict(zip(groups[g], forward_sibling(done, axes, "gather_forward")))
        return ready[g][n, l]

    conv_full = jnp.moveaxis(conv_all[:, :n_conv, :wc], 0, 1).reshape(n_conv, N_CHIPS * wc)
    ws = {n: w[n] for n in SMALL_REPL}
    at = 0
    for n in SMALL_CONV:
        rows = w[n].shape[0] * w[n].shape[1]
        ws[n] = conv_full[at:at + rows].reshape(w[n].shape[0], w[n].shape[1], N_CHIPS * wc)
        at += rows

    halves, state = {}, {}
    reduce_groups = {"r1": [(n, 1) for n in BIG], "r0a": [(n, 0) for n in BIG[2:]], "r0b": [(n, 0) for n in BIG[:2]]}
    plan = {("layer", 1): [("pair", "r1")],
            ("ffn2", 0): [("chips", "r1")],
            ("mix", 0): [("finish", "r1"), ("pair", "r0a")],
            ("ffn1_mid", 0): [("chips", "r0a")],
            ("layer", 0): [("pair", "r0b"), ("finish", "r0a"), ("chips", "r0b"), ("finish", "r0b")]}


    def stage_pair(g, keys, grads, after):
        gvs = [_grad_view(grads[n][l], axis[n]) for n, l in keys]
        others = [lax.empty(gv.shape[:1] + gv.shape[2:], F32) for gv in gvs]
        send, recv, others, token = split_start("pair_start_" + g, gvs, others, len(gvs), pair_copies, [after])
        state[g] = dict(sources=gvs, send=send, recv=recv, landing=others, token=token)
        return [(state[g], "token")]

    def stage_chips(g, keys, grads, after):
        st = state[g]
        axes = [axis[n] for n, _ in keys]
        others = split_wait("pair_wait_" + g, st["sources"], st["landing"], st["send"], st["recv"], pair_copies,
                            [after, st["token"]])
        sums = [add_pair(gv, o, place, "add_pair") for gv, o in zip(st["sources"], others)]
        qs = [lax.empty((3, s.shape[1], s.shape[2] // (N_CHIPS if ax == 1 else 1)), GRAD_WIRE) for s, ax in zip(sums, axes)]
        send, recv, qs, token = split_start("chips_start_" + g, sums, qs, 3 * len(sums), chip_copies(axes), [after])
        state[g] = dict(sources=sums, send=send, recv=recv, landing=qs, token=token)
        return [(state[g], "token")]

    def stage_finish(g, keys, grads, after):
        st = state.pop(g)
        qs = split_wait("chips_wait_" + g, st["sources"], st["landing"], st["send"], st["recv"],
                        chip_copies([axis[n] for n, _ in keys]), [after, st["token"]])
        for key, s, q in zip(keys, st["sources"], qs):
            halves[key] = add_chips(s, q, axis[key[0]], place, "add_chips")
        return [(halves, key) for key in keys]

    stages = {"pair": stage_pair, "chips": stage_chips, "finish": stage_finish}

    def progress(event, l, grads, values):
        places = []
        for stage, g in plan.get((event, l), []):
            places += stages[stage](g, reduce_groups[g], grads, values[0])
        if places:
            values, tied = lax.optimization_barrier((values, [box[k] for box, k in places]))
            for (box, k), a in zip(places, tied):
                box[k] = a
        return values

    loss_part, grad_x, grads = _local_step(x[0], mem[0], loss_target[0], fetch, ws, L, progress)
    loss = lax.psum(loss_part, ("x", "y", "c"))

    small = SMALL_REPL + SMALL_CONV
    small_g = [grads[n] if n == "norm_final" else jnp.stack(grads[n]) for n in small]
    total = _unpack(all_reduce_small(_pack(small_g), "reduce_small"), [g.shape for g in small_g])
    grad = dict(zip(small, total))
    for n in SMALL_CONV:
        grad[n] = lax.dynamic_slice_in_dim(grad[n], chip * wc, wc, axis=2)

    keys = [(n, l) for n in BIG for l in range(L)]
    shard_grad = dict(zip(keys, share_sibling([halves[k] for k in keys], "share_pair")))
    delta, new_m, new_v = {}, {}, {}
    for n in BIG:
        g0, g1 = (shard_grad[n, l].reshape(w[n].shape[1:]) for l in range(L))
        grad[n], delta[n], new_m[n], new_v[n] = adamw_layers(w[n], g0, g1, given["m_" + n], given["v_" + n], "adamw")
    shapes = [w[n].shape for n in small]
    packed = [_pack([src[n] for n in small]) for src in
              (w, grad, {n: given["m_" + n] for n in small}, {n: given["v_" + n] for n in small})]
    for out, p in zip((delta, new_m, new_v), adamw(*packed, "adamw_small")):
        out.update(zip(small, _unpack(p, shapes)))

    return (loss, grad_x[None], *[grad[n] for n in WEIGHTS], *[delta[n] for n in WEIGHTS],
            *[new_m[n] for n in WEIGHTS], *[new_v[n] for n in WEIGHTS])
```

```python
import functools
import math

import jax
import jax.numpy as jnp
from jax import lax
from jax.experimental import pallas as pl
from jax.experimental.pallas import tpu as pltpu

F32 = jnp.float32
BF16 = jnp.bfloat16
EPS = 1e-6
SEQ_CHUNK = 128
POOL_WINDOWS = (2, 4, 8, 16)
N_HEADS = 4
ADAM_LR, ADAM_B1, ADAM_B2, ADAM_EPS, ADAM_WD, ADAM_STEP = 0.001, 0.9, 0.999, 1e-08, 0.01, 10
VMEM_LIMIT = 56 * 1024 * 1024
MESH_ID = pl.DeviceIdType.MESH
N_CHIPS = 4
GRAD_WIRE = BF16

BIG = ("ffn1_w_in", "ffn1_w_out", "mix_w_in", "mix_w_out", "xattn_wq", "xattn_wkv", "xattn_wo",
       "ffn2_w_in", "ffn2_w_out")
COL_SHARDED = ("ffn1_w_in", "mix_w_in", "xattn_wkv", "ffn2_w_in")
SMALL_CONV = ("sconv_w", "cconv_w")
SMALL_REPL = ("norm_ffn1", "norm_mix", "sgu_norm_g", "sgu_w", "sgu_b", "cconv_ln_g", "cconv_ln_b",
              "pool_w", "pool_scale", "norm_xattn", "norm_mem", "norm_ffn2", "norm_final")
WEIGHTS = ("norm_ffn1", "ffn1_w_in", "ffn1_w_out", "norm_mix", "mix_w_in", "sconv_w", "sgu_norm_g",
           "sgu_w", "sgu_b", "cconv_w", "cconv_ln_g", "cconv_ln_b", "pool_w", "pool_scale",
           "mix_w_out", "norm_xattn", "norm_mem", "xattn_wq", "xattn_wkv", "xattn_wo", "norm_ffn2",
           "ffn2_w_in", "ffn2_w_out", "norm_final")


def _pick(n, pref, align):
    best = None
    for d in range(align, min(n, pref) + 1, align):
        if n % d == 0:
            best = d
    return best or n


def _sig(x):
    return 0.5 * jnp.tanh(0.5 * x) + 0.5


def _nt(a, b):
    return lax.dot_general(a, b, (((1,), (1,)), ((), ())), preferred_element_type=F32)


def _tn(a, b):
    return lax.dot_general(a, b, (((0,), (0,)), ((), ())), preferred_element_type=F32)


def _params(*sem):
    return pltpu.CompilerParams(dimension_semantics=sem, vmem_limit_bytes=VMEM_LIMIT)


def norm_matmul(x, g, w, name, out_dtype=F32):
    T, D = x.shape
    N = w.shape[1]
    tm, tn = _pick(T, 512, 8), _pick(N, 1024, 128)

    def body(x_ref, g_ref, w_ref, o_ref, h_ref):
        j = pl.program_id(1)

        @pl.when(j == 0)
        def _():
            xv = x_ref[...]
            r = lax.rsqrt(jnp.mean(xv * xv, axis=-1, keepdims=True) + EPS)
            h_ref[...] = (xv * r * g_ref[...]).astype(BF16)

        o_ref[...] = jnp.dot(h_ref[...], w_ref[...], preferred_element_type=F32).astype(out_dtype)

    return pl.pallas_call(
        body, name=name, grid=(T // tm, N // tn),
        in_specs=[pl.BlockSpec((tm, D), lambda i, j: (i, 0)), pl.BlockSpec((1, D), lambda i, j: (0, 0)),
                  pl.BlockSpec((D, tn), lambda i, j: (0, j))],
        out_specs=[pl.BlockSpec((tm, tn), lambda i, j: (i, j)), pl.BlockSpec((tm, D), lambda i, j: (i, 0))],
        out_shape=[jax.ShapeDtypeStruct((T, N), out_dtype), jax.ShapeDtypeStruct((T, D), BF16)],
        compiler_params=_params("parallel", "arbitrary"))(x, g, w)


def matmul_res(res, a, w, name):
    T, K = a.shape
    N = w.shape[1]
    tm, tn = _pick(T, 512, 8), _pick(N, 1024, 128)

    def body(r_ref, a_ref, w_ref, o_ref):
        o_ref[...] = r_ref[...] + jnp.dot(a_ref[...].astype(BF16), w_ref[...], preferred_element_type=F32)

    return pl.pallas_call(
        body, name=name, grid=(T // tm, N // tn),
        in_specs=[pl.BlockSpec((tm, tn), lambda i, j: (i, j)), pl.BlockSpec((tm, K), lambda i, j: (i, 0)),
                  pl.BlockSpec((K, tn), lambda i, j: (0, j))],
        out_specs=pl.BlockSpec((tm, tn), lambda i, j: (i, j)),
        out_shape=jax.ShapeDtypeStruct((T, N), F32),
        compiler_params=_params("parallel", "parallel"))(res, a, w)


def matmul_nt(a, w, name, out_dtype=F32):
    T, N = a.shape
    M = w.shape[0]
    tm, tmm = _pick(T, 512, 8), _pick(M, 1024, 128)

    def body(a_ref, w_ref, o_ref):
        o_ref[...] = _nt(a_ref[...].astype(BF16), w_ref[...]).astype(out_dtype)

    return pl.pallas_call(
        body, name=name, grid=(T // tm, M // tmm),
        in_specs=[pl.BlockSpec((tm, N), lambda i, j: (i, 0)), pl.BlockSpec((tmm, N), lambda i, j: (j, 0))],
        out_specs=pl.BlockSpec((tm, tmm), lambda i, j: (i, j)),
        out_shape=jax.ShapeDtypeStruct((T, M), out_dtype),
        compiler_params=_params("parallel", "parallel"))(a, w)


def matmul_tn(a, b, scale, name, b2=None):
    T, M = a.shape
    Nb = b.shape[1]
    bm, bn, bk = _pick(M, 1408, 128), _pick(Nb, 1408, 128), _pick(T, 512, 8)
    nk, nj = T // bk, Nb // bn

    def body(a_ref, b_ref, *rest):
        o_ref = rest[-1]
        j, k = pl.program_id(1), pl.program_id(2)

        @pl.when(k == 0)
        def _():
            o_ref[...] = jnp.zeros_like(o_ref)

        a_blk = a_ref[...].astype(BF16)
        if b2 is None:
            o_ref[...] += _tn(a_blk, b_ref[...].astype(BF16))
        else:
            @pl.when(j < nj)
            def _():
                o_ref[...] += _tn(a_blk, b_ref[...].astype(BF16))

            @pl.when(j >= nj)
            def _():
                o_ref[...] += _tn(a_blk, rest[0][...].astype(BF16))

        if scale != 1.0:
            @pl.when(k == nk - 1)
            def _():
                o_ref[...] = o_ref[...] * scale

    if b2 is None:
        b_specs, operands, n_out = [pl.BlockSpec((bk, bn), lambda i, j, k: (k, j))], (a, b), nj
    else:
        first = lambda i, j, k: (jnp.where(j < nj, k, 0), jnp.where(j < nj, j, 0))
        second = lambda i, j, k: (jnp.where(j >= nj, k, 0), jnp.where(j >= nj, j - nj, 0))
        b_specs, operands, n_out = [pl.BlockSpec((bk, bn), first), pl.BlockSpec((bk, bn), second)], (a, b, b2), 2 * nj
    return pl.pallas_call(
        body, name=name, grid=(M // bm, n_out, nk),
        in_specs=[pl.BlockSpec((bk, bm), lambda i, j, k: (k, i))] + b_specs,
        out_specs=pl.BlockSpec((bm, bn), lambda i, j, k: (i, j)),
        out_shape=jax.ShapeDtypeStruct((M, n_out * bn), F32),
        compiler_params=_params("parallel", "parallel", "arbitrary"))(*operands)


def rmsnorm_bwd(dxo, dh, x, g, name):
    T, D = x.shape
    tm = _pick(T, 512, 8)
    has_res = dxo is not None

    def body(*refs):
        if has_res:
            dxo_ref, dh_ref, x_ref, g_ref, dx_ref, dg_ref = refs
        else:
            dh_ref, x_ref, g_ref, dx_ref, dg_ref = refs
        i = pl.program_id(0)

        @pl.when(i == 0)
        def _():
            dg_ref[...] = jnp.zeros_like(dg_ref)

        xv, dh_v = x_ref[...], dh_ref[...]
        r = lax.rsqrt(jnp.mean(xv * xv, axis=-1, keepdims=True) + EPS)
        xh = xv * r
        dg_ref[...] += jnp.sum(dh_v * xh, axis=0, keepdims=True)
        dxh = dh_v * g_ref[...]
        dx = r * (dxh - xh * jnp.mean(dxh * xh, axis=-1, keepdims=True))
        dx_ref[...] = dx + dxo_ref[...] if has_res else dx

    tile = pl.BlockSpec((tm, D), lambda i: (i, 0))
    vec = pl.BlockSpec((1, D), lambda i: (0, 0))
    args = ([dxo] if has_res else []) + [dh, x, g]
    return pl.pallas_call(
        body, name=name, grid=(T // tm,),
        in_specs=[tile] * (len(args) - 1) + [vec],
        out_specs=[tile, vec],
        out_shape=[jax.ShapeDtypeStruct((T, D), F32), jax.ShapeDtypeStruct((1, D), F32)],
        compiler_params=_params("arbitrary"))(*args)


def ffn_fwd(x, g, w_in, w_out, name):
    T, D = x.shape
    F = w_out.shape[0]
    tm, tf = _pick(T, 512, 8), _pick(F, 1408, 128)
    nf = F // tf

    def body(x_ref, g_ref, wg_ref, wu_ref, wo_ref, o_ref, h_ref, zg_ref, zu_ref, acc_ref):
        j = pl.program_id(1)

        @pl.when(j == 0)
        def _():
            xv = x_ref[...]
            r = lax.rsqrt(jnp.mean(xv * xv, axis=-1, keepdims=True) + EPS)
            h_ref[...] = (xv * r * g_ref[...]).astype(BF16)
            acc_ref[...] = jnp.zeros_like(acc_ref)

        h = h_ref[...]
        zg = jnp.dot(h, wg_ref[...], preferred_element_type=F32)
        zu = jnp.dot(h, wu_ref[...], preferred_element_type=F32)
        zg_ref[...] = zg.astype(BF16)
        zu_ref[...] = zu.astype(BF16)
        a = (zg * _sig(zg) * zu).astype(BF16)
        acc_ref[...] += jnp.dot(a, wo_ref[...], preferred_element_type=F32)

        @pl.when(j == nf - 1)
        def _():
            o_ref[...] = x_ref[...] + 0.5 * acc_ref[...]

    tile = pl.BlockSpec((tm, D), lambda i, j: (i, 0))
    fblk = pl.BlockSpec((tm, tf), lambda i, j: (i, j))
    hidden = jax.ShapeDtypeStruct((T, F), BF16)
    return pl.pallas_call(
        body, name=name, grid=(T // tm, nf),
        in_specs=[tile, pl.BlockSpec((1, D), lambda i, j: (0, 0)),
                  pl.BlockSpec((D, tf), lambda i, j: (0, j)), pl.BlockSpec((D, tf), lambda i, j: (0, j + nf)),
                  pl.BlockSpec((tf, D), lambda i, j: (j, 0))],
        out_specs=[tile, tile, fblk, fblk],
        out_shape=[jax.ShapeDtypeStruct((T, D), F32), jax.ShapeDtypeStruct((T, D), BF16), hidden, hidden],
        scratch_shapes=[pltpu.VMEM((tm, D), F32)],
        compiler_params=_params("parallel", "arbitrary"))(x, g, w_in, w_in, w_out)


def ffn_dz(dxo, zg, zu, w_out, name):
    T, D = dxo.shape
    F = w_out.shape[0]
    tm, tf = _pick(T, 512, 8), _pick(F, 256, 128)

    def body(dxo_ref, zg_ref, zu_ref, wo_ref, a_ref, dzg_ref, dzu_ref, do_ref):
        @pl.when(pl.program_id(1) == 0)
        def _():
            do_ref[...] = (0.5 * dxo_ref[...]).astype(BF16)

        zg, zu = zg_ref[...].astype(F32), zu_ref[...].astype(F32)
        s = _sig(zg)
        silu = zg * s
        a_ref[...] = (silu * zu).astype(BF16)
        da = _nt(do_ref[...], wo_ref[...])
        dzu_ref[...] = (da * silu).astype(BF16)
        dzg_ref[...] = (da * zu * (s + silu * (1.0 - s))).astype(BF16)

    fblk = pl.BlockSpec((tm, tf), lambda i, j: (i, j))
    hidden = jax.ShapeDtypeStruct((T, F), BF16)
    return pl.pallas_call(
        body, name=name, grid=(T // tm, F // tf),
        in_specs=[pl.BlockSpec((tm, D), lambda i, j: (i, 0)), fblk, fblk, pl.BlockSpec((tf, D), lambda i, j: (j, 0))],
        out_specs=[fblk, fblk, fblk], out_shape=[hidden, hidden, hidden],
        scratch_shapes=[pltpu.VMEM((tm, D), BF16)],
        compiler_params=_params("parallel", "arbitrary"))(dxo, zg, zu, w_out)


def ffn_dh(x, dxo, g, dzg, dzu, w_in, name):
    T, D = x.shape
    F = dzg.shape[1]
    tm = _pick(T, 256, 8)

    def body(x_ref, dxo_ref, g_ref, dzg_ref, dzu_ref, wg_ref, wu_ref, dx_ref, dg_ref):
        i = pl.program_id(0)

        @pl.when(i == 0)
        def _():
            dg_ref[...] = jnp.zeros_like(dg_ref)

        dh = _nt(dzg_ref[...], wg_ref[...]) + _nt(dzu_ref[...], wu_ref[...])
        xv = x_ref[...]
        r = lax.rsqrt(jnp.mean(xv * xv, axis=-1, keepdims=True) + EPS)
        xh = xv * r
        dg_ref[...] += jnp.sum(dh * xh, axis=0, keepdims=True)
        dxh = dh * g_ref[...]
        dx_ref[...] = dxo_ref[...] + r * (dxh - xh * jnp.mean(dxh * xh, axis=-1, keepdims=True))

    tile = pl.BlockSpec((tm, D), lambda i: (i, 0))
    vec = pl.BlockSpec((1, D), lambda i: (0, 0))
    ftile = pl.BlockSpec((tm, F), lambda i: (i, 0))
    return pl.pallas_call(
        body, name=name, grid=(T // tm,),
        in_specs=[tile, tile, vec, ftile, ftile, pl.BlockSpec((D, F), lambda i: (0, 0)), pl.BlockSpec((D, F), lambda i: (0, 1))],
        out_specs=[tile, vec],
        out_shape=[jax.ShapeDtypeStruct((T, D), F32), jax.ShapeDtypeStruct((1, D), F32)],
        compiler_params=_params("arbitrary"))(x, dxo, g, dzg, dzu, w_in, w_in)


def _chunks(T, fn):
    def step(c, carry):
        fn(pl.multiple_of(c * SEQ_CHUNK, SEQ_CHUNK))
        return carry
    lax.fori_loop(0, T // SEQ_CHUNK, step, 0)


def _conv_taps(win, ktaps, pad):
    return [(win if k == ktaps - 1 else pltpu.roll(win, ktaps - 1 - k, 0))[pad:, :] for k in range(ktaps)]


def _conv_taps_t(win, ktaps, pad):
    n = win.shape[0]
    return [(win if k == ktaps - 1 else pltpu.roll(win, n - (ktaps - 1 - k), 0))[:n - pad, :] for k in range(ktaps)]


def _col(T, W, idx):
    return pl.BlockSpec((T, W), lambda i, idx=idx: (0, idx))


def _full(shape):
    return pl.BlockSpec(shape, lambda i: (0,) * len(shape))


def mix_a_fwd(z, w, name):
    T, W = z.shape[0], w.shape[1]
    K, P = w.shape[0], 8

    def body(ab_ref, ac_ref, ax_ref, w_ref, y_ref, pp_ref):
        pp_ref[0:P, :] = jnp.zeros((P, W), F32)

        def chunk(s):
            rows = pl.ds(s, SEQ_CHUNK)
            pp_ref[pl.ds(s + P, SEQ_CHUNK), :] = ac_ref[rows, :] * ax_ref[rows, :]
            taps = _conv_taps(pp_ref[pl.ds(s, SEQ_CHUNK + P), :], K, P)
            q = sum(w_ref[k:k + 1, :] * taps[k] for k in range(K))
            y_ref[rows, :] = (ab_ref[rows, :] * q).astype(BF16)

        _chunks(T, chunk)

    return pl.pallas_call(
        body, name=name, grid=(1,),
        in_specs=[_col(T, W, 0), _col(T, W, 1), _col(T, W, 2), _full((K, W))],
        out_specs=_full((T, W)), out_shape=jax.ShapeDtypeStruct((T, W), BF16),
        scratch_shapes=[pltpu.VMEM((T + P, W), F32)],
        compiler_params=_params("arbitrary"))(z, z, z, w)


def mix_a_bwd(z, dy, w, name):
    T, W = z.shape[0], w.shape[1]
    K, P = w.shape[0], 8

    def body(ab_ref, ac_ref, ax_ref, dy_ref, w_ref, dab_ref, dac_ref, dax_ref, dw_ref, pp_ref, dq_ref):
        pp_ref[0:P, :] = jnp.zeros((P, W), F32)
        dq_ref[T:T + P, :] = jnp.zeros((P, W), F32)
        dw_ref[...] = jnp.zeros_like(dw_ref)

        def chunk1(s):
            rows = pl.ds(s, SEQ_CHUNK)
            pp_ref[pl.ds(s + P, SEQ_CHUNK), :] = ac_ref[rows, :] * ax_ref[rows, :]
            taps = _conv_taps(pp_ref[pl.ds(s, SEQ_CHUNK + P), :], K, P)
            q = sum(w_ref[k:k + 1, :] * taps[k] for k in range(K))
            dyv = dy_ref[rows, :]
            dab_ref[rows, :] = (dyv * q).astype(BF16)
            dq = dyv * ab_ref[rows, :]
            dq_ref[rows, :] = dq
            for k in range(K):
                dw_ref[k:k + 1, :] += jnp.sum(dq * taps[k], axis=0, keepdims=True)

        _chunks(T, chunk1)

        def chunk2(s):
            rows = pl.ds(s, SEQ_CHUNK)
            taps = _conv_taps_t(dq_ref[pl.ds(s, SEQ_CHUNK + P), :], K, P)
            dp = sum(w_ref[k:k + 1, :] * taps[k] for k in range(K))
            dac_ref[rows, :] = (dp * ax_ref[rows, :]).astype(BF16)
            dax_ref[rows, :] = (dp * ac_ref[rows, :]).astype(BF16)

        _chunks(T, chunk2)

    tw = jax.ShapeDtypeStruct((T, W), BF16)
    return pl.pallas_call(
        body, name=name, grid=(1,),
        in_specs=[_col(T, W, 0), _col(T, W, 1), _col(T, W, 2), _col(T, W, 0), _full((K, W))],
        out_specs=[_full((T, W))] * 3 + [_full((K, W))],
        out_shape=[tw, tw, tw, jax.ShapeDtypeStruct((K, W), F32)],
        scratch_shapes=[pltpu.VMEM((T + P, W), F32), pltpu.VMEM((T + P, W), F32)],
        compiler_params=_params("arbitrary"))(z, z, z, dy, w)


def _ln_stats(v):
    mu = jnp.mean(v, axis=-1, keepdims=True)
    xc = v - mu
    rstd = lax.rsqrt(jnp.mean(xc * xc, axis=-1, keepdims=True) + EPS)
    return xc * rstd, rstd


def _ln_bwd(dxh, xh, rstd):
    return rstd * (dxh - jnp.mean(dxh, axis=-1, keepdims=True) - xh * jnp.mean(dxh * xh, axis=-1, keepdims=True))


def _tril_bf16(w_ref, h):
    n = w_ref.shape[-1]
    keep = lax.broadcasted_iota(jnp.int32, (n, n), 0) >= lax.broadcasted_iota(jnp.int32, (n, n), 1)
    return jnp.where(keep, w_ref[h], 0.0).astype(BF16)


def mix_b_fwd(z, g, w_s, bias, name):
    T, W = z.shape[0], g.shape[1]
    H, C = w_s.shape[0], w_s.shape[1]
    hd = W // H

    def body(u_ref, v_ref, g_ref, w_ref, b_ref, y_ref):
        wts = [_tril_bf16(w_ref, h) for h in range(H)]
        head = lax.broadcasted_iota(jnp.int32, (C, W), 1) // hd

        def chunk(s):
            rows = pl.ds(s, C)
            xh, _ = _ln_stats(v_ref[rows, :])
            vn = (xh * g_ref[...]).astype(BF16)
            mixed = b_ref[...]
            for h in range(H):
                mixed = mixed + jnp.where(head == h, jnp.dot(wts[h], vn, preferred_element_type=F32), 0.0)
            y_ref[rows, :] = (u_ref[rows, :] * mixed).astype(BF16)

        _chunks(T, chunk)

    return pl.pallas_call(
        body, name=name, grid=(1,),
        in_specs=[_col(T, W, 3), _col(T, W, 4), _full((1, W)), _full((H, C, C)), _full((C, W))],
        out_specs=_full((T, W)), out_shape=jax.ShapeDtypeStruct((T, W), BF16),
        compiler_params=_params("arbitrary"))(z, z, g, w_s, bias)


def mix_b_bwd(z, dy, g, w_s, bias, name):
    T, W = z.shape[0], g.shape[1]
    H, C = w_s.shape[0], w_s.shape[1]
    hd = W // H

    def body(u_ref, v_ref, dy_ref, g_ref, w_ref, b_ref, du_ref, dv_ref, dw_ref, db_ref, dg_ref, dbf_ref):
        wts = [_tril_bf16(w_ref, h) for h in range(H)]
        head = lax.broadcasted_iota(jnp.int32, (C, W), 1) // hd
        dw_ref[...] = jnp.zeros_like(dw_ref)
        dg_ref[...] = jnp.zeros_like(dg_ref)
        dbf_ref[...] = jnp.zeros_like(dbf_ref)

        def chunk(s):
            rows = pl.ds(s, C)
            xh, rstd = _ln_stats(v_ref[rows, :])
            vn = (xh * g_ref[...]).astype(BF16)
            mixed = b_ref[...]
            for h in range(H):
                mixed = mixed + jnp.where(head == h, jnp.dot(wts[h], vn, preferred_element_type=F32), 0.0)
            dyv = dy_ref[rows, :]
            du_ref[rows, :] = (dyv * mixed).astype(BF16)
            dm = dyv * u_ref[rows, :]
            dbf_ref[...] += dm
            dvn = jnp.zeros((C, W), F32)
            for h in range(H):
                dmh = jnp.where(head == h, dm, 0.0).astype(BF16)
                dw_ref[h] += _nt(dmh, vn)
                dvn = dvn + _tn(wts[h], dmh)
            dg_ref[...] += jnp.sum(dvn * xh, axis=0, keepdims=True)
            dv_ref[rows, :] = _ln_bwd(dvn * g_ref[...], xh, rstd).astype(BF16)

        _chunks(T, chunk)

        keep = lax.broadcasted_iota(jnp.int32, (C, C), 0) >= lax.broadcasted_iota(jnp.int32, (C, C), 1)
        lane = lax.broadcasted_iota(jnp.int32, (C, 128), 1)
        db = jnp.zeros((C, 128), F32)
        dbf = dbf_ref[...]
        for h in range(H):
            dw_ref[h] = jnp.where(keep, dw_ref[h], 0.0)
            db = db + jnp.where(lane == h, jnp.sum(jnp.where(head == h, dbf, 0.0), axis=1, keepdims=True), 0.0)
        db_ref[...] = db

    tw = jax.ShapeDtypeStruct((T, W), BF16)
    return pl.pallas_call(
        body, name=name, grid=(1,),
        in_specs=[_col(T, W, 3), _col(T, W, 4), _col(T, W, 1), _full((1, W)), _full((H, C, C)), _full((C, W))],
        out_specs=[_full((T, W)), _full((T, W)), _full((H, C, C)), _full((C, 128)), _full((1, W))],
        out_shape=[tw, tw, jax.ShapeDtypeStruct((H, C, C), F32), jax.ShapeDtypeStruct((C, 128), F32),
                   jax.ShapeDtypeStruct((1, W), F32)],
        scratch_shapes=[pltpu.VMEM((C, W), F32)],
        compiler_params=_params("arbitrary"))(z, z, dy, g, w_s, bias)


def mix_c_fwd(z, w, ln_g, ln_b, name):
    T, W = z.shape[0], w.shape[1]
    K, P = w.shape[0], 32

    def body(a_ref, gt_ref, w_ref, g_ref, b_ref, y_ref, up_ref):
        up_ref[0:P, :] = jnp.zeros((P, W), F32)

        def chunk(s):
            rows = pl.ds(s, SEQ_CHUNK)
            up_ref[pl.ds(s + P, SEQ_CHUNK), :] = a_ref[rows, :] * _sig(gt_ref[rows, :])
            taps = _conv_taps(up_ref[pl.ds(s, SEQ_CHUNK + P), :], K, P)
            q = sum(w_ref[k:k + 1, :] * taps[k] for k in range(K))
            xh, _ = _ln_stats(q)
            r = xh * g_ref[...] + b_ref[...]
            y_ref[rows, :] = (r * _sig(r)).astype(BF16)

        _chunks(T, chunk)

    return pl.pallas_call(
        body, name=name, grid=(1,),
        in_specs=[_col(T, W, 5), _col(T, W, 6), _full((K, W)), _full((1, W)), _full((1, W))],
        out_specs=_full((T, W)), out_shape=jax.ShapeDtypeStruct((T, W), BF16),
        scratch_shapes=[pltpu.VMEM((T + P, W), F32)],
        compiler_params=_params("arbitrary"))(z, z, w, ln_g, ln_b)


def mix_c_bwd(z, dy, w, ln_g, ln_b, name):
    T, W = z.shape[0], w.shape[1]
    K, P = w.shape[0], 32

    def body(a_ref, gt_ref, dy_ref, w_ref, g_ref, b_ref, da_ref, dgt_ref, dw_ref, dg_ref, db_ref, up_ref, dq_ref):
        up_ref[0:P, :] = jnp.zeros((P, W), F32)
        dq_ref[T:T + P, :] = jnp.zeros((P, W), F32)
        dw_ref[...] = jnp.zeros_like(dw_ref)
        dg_ref[...] = jnp.zeros_like(dg_ref)
        db_ref[...] = jnp.zeros_like(db_ref)

        def chunk1(s):
            rows = pl.ds(s, SEQ_CHUNK)
            up_ref[pl.ds(s + P, SEQ_CHUNK), :] = a_ref[rows, :] * _sig(gt_ref[rows, :])
            taps = _conv_taps(up_ref[pl.ds(s, SEQ_CHUNK + P), :], K, P)
            q = sum(w_ref[k:k + 1, :] * taps[k] for k in range(K))
            xh, rstd = _ln_stats(q)
            r = xh * g_ref[...] + b_ref[...]
            sr = _sig(r)
            dr = dy_ref[rows, :] * (sr * (1.0 + r * (1.0 - sr)))
            db_ref[...] += jnp.sum(dr, axis=0, keepdims=True)
            dg_ref[...] += jnp.sum(dr * xh, axis=0, keepdims=True)
            dq = _ln_bwd(dr * g_ref[...], xh, rstd)
            dq_ref[rows, :] = dq
            for k in range(K):
                dw_ref[k:k + 1, :] += jnp.sum(dq * taps[k], axis=0, keepdims=True)

        _chunks(T, chunk1)

        def chunk2(s):
            rows = pl.ds(s, SEQ_CHUNK)
            taps = _conv_taps_t(dq_ref[pl.ds(s, SEQ_CHUNK + P), :], K, P)
            du = sum(w_ref[k:k + 1, :] * taps[k] for k in range(K))
            sg = _sig(gt_ref[rows, :])
            da_ref[rows, :] = (du * sg).astype(BF16)
            dgt_ref[rows, :] = (du * a_ref[rows, :] * sg * (1.0 - sg)).astype(BF16)

        _chunks(T, chunk2)

    tw = jax.ShapeDtypeStruct((T, W), BF16)
    vec = jax.ShapeDtypeStruct((1, W), F32)
    return pl.pallas_call(
        body, name=name, grid=(1,),
        in_specs=[_col(T, W, 5), _col(T, W, 6), _col(T, W, 2), _full((K, W)), _full((1, W)), _full((1, W))],
        out_specs=[_full((T, W)), _full((T, W)), _full((K, W)), _full((1, W)), _full((1, W))],
        out_shape=[tw, tw, jax.ShapeDtypeStruct((K, W), F32), vec, vec],
        scratch_shapes=[pltpu.VMEM((T + P, W), F32), pltpu.VMEM((T + P, W), F32)],
        compiler_params=_params("arbitrary"))(z, z, dy, w, ln_g, ln_b)


def _pool_select(levels, W, rows):
    group = lax.broadcasted_iota(jnp.int32, (rows, W), 1) // (W // len(POOL_WINDOWS))
    out = levels[-1]
    for gi in range(len(POOL_WINDOWS) - 2, -1, -1):
        out = jnp.where(group == gi, levels[gi], out)
    return out


def _pool_count(s, W):
    t = s + lax.broadcasted_iota(jnp.int32, (SEQ_CHUNK, W), 0)
    group = lax.broadcasted_iota(jnp.int32, (SEQ_CHUNK, W), 1) // (W // len(POOL_WINDOWS))
    win = jnp.full((SEQ_CHUNK, W), POOL_WINDOWS[-1], jnp.int32)
    for gi in range(len(POOL_WINDOWS) - 2, -1, -1):
        win = jnp.where(group == gi, POOL_WINDOWS[gi], win)
    return jnp.minimum(t + 1, win).astype(F32)


def _pooled(wp_ref, s, W, P):
    win = wp_ref[pl.ds(s, SEQ_CHUNK + P), :]
    levels, acc, shift = [], win, 1
    for _ in POOL_WINDOWS:
        acc = acc + pltpu.roll(acc, shift, 0)
        levels.append(acc[P:, :])
        shift *= 2
    return _pool_select(levels, W, SEQ_CHUNK) / _pool_count(s, W) - win[P:, :]


def mix_d_fwd(z, pbd, scale, name):
    T, W = z.shape[0], scale.shape[1]
    P = 16

    def body(x_ref, p_ref, s_ref, y_ref, wp_ref):
        wp_ref[0:P, :] = jnp.zeros((P, W), F32)

        def chunk(s):
            rows = pl.ds(s, SEQ_CHUNK)
            wp_ref[pl.ds(s + P, SEQ_CHUNK), :] = x_ref[rows, :]
            pooled = _pooled(wp_ref, s, W, P).astype(BF16)
            y_ref[rows, :] = (jnp.dot(pooled, p_ref[...], preferred_element_type=F32) * s_ref[...]).astype(BF16)

        _chunks(T, chunk)

    return pl.pallas_call(
        body, name=name, grid=(1,),
        in_specs=[_col(T, W, 7), _full((W, W)), _full((1, W))],
        out_specs=_full((T, W)), out_shape=jax.ShapeDtypeStruct((T, W), BF16),
        scratch_shapes=[pltpu.VMEM((T + P, W), F32)],
        compiler_params=_params("arbitrary"))(z, pbd, scale)


def mix_d_bwd(z, dy, pbd, scale, name):
    T, W = z.shape[0], scale.shape[1]
    P = 16

    def body(x_ref, dy_ref, p_ref, s_ref, dx_ref, dp_ref, ds_ref, wp_ref, e_ref, dpool_ref):
        wp_ref[0:P, :] = jnp.zeros((P, W), F32)
        e_ref[T:T + P, :] = jnp.zeros((P, W), F32)
        dp_ref[...] = jnp.zeros_like(dp_ref)
        ds_ref[...] = jnp.zeros_like(ds_ref)

        def chunk1(s):
            rows = pl.ds(s, SEQ_CHUNK)
            wp_ref[pl.ds(s + P, SEQ_CHUNK), :] = x_ref[rows, :]
            pooled = _pooled(wp_ref, s, W, P).astype(BF16)
            yl = jnp.dot(pooled, p_ref[...], preferred_element_type=F32)
            dyv = dy_ref[rows, :]
            ds_ref[...] += jnp.sum(dyv * yl, axis=0, keepdims=True)
            dyl = (dyv * s_ref[...]).astype(BF16)
            dp_ref[...] += _tn(pooled, dyl)
            dpool = _nt(dyl, p_ref[...])
            dpool_ref[rows, :] = dpool
            e_ref[rows, :] = dpool / _pool_count(s, W)

        _chunks(T, chunk1)

        def chunk2(s):
            rows = pl.ds(s, SEQ_CHUNK)
            win = e_ref[pl.ds(s, SEQ_CHUNK + P), :]
            n = SEQ_CHUNK + P
            levels, acc, shift = [], win, 1
            for _ in POOL_WINDOWS:
                acc = acc + pltpu.roll(acc, n - shift, 0)
                levels.append(acc[:SEQ_CHUNK, :])
                shift *= 2
            dx_ref[rows, :] = (_pool_select(levels, W, SEQ_CHUNK) - dpool_ref[rows, :]).astype(BF16)

        _chunks(T, chunk2)

    return pl.pallas_call(
        body, name=name, grid=(1,),
        in_specs=[_col(T, W, 7), _col(T, W, 3), _full((W, W)), _full((1, W))],
        out_specs=[_full((T, W)), _full((W, W)), _full((1, W))],
        out_shape=[jax.ShapeDtypeStruct((T, W), BF16), jax.ShapeDtypeStruct((W, W), F32),
                   jax.ShapeDtypeStruct((1, W), F32)],
        scratch_shapes=[pltpu.VMEM((T + P, W), F32), pltpu.VMEM((T + P, W), F32), pltpu.VMEM((T, W), F32)],
        compiler_params=_params("arbitrary"))(z, dy, pbd, scale)


def attn_fwd(q, kv, name):
    T, D = q.shape
    M = kv.shape[0]
    hd = D // N_HEADS
    tm = _pick(T, 512, 8)
    sc = 1.0 / math.sqrt(hd)

    def body(q_ref, k_ref, v_ref, o_ref):
        for h in range(N_HEADS):
            cols = slice(h * hd, (h + 1) * hd)
            s = _nt(q_ref[:, cols].astype(BF16), k_ref[:, cols].astype(BF16)) * sc
            p = jnp.exp(s - jnp.max(s, axis=-1, keepdims=True))
            p = p / jnp.sum(p, axis=-1, keepdims=True)
            o_ref[:, cols] = jnp.dot(p.astype(BF16), v_ref[:, cols].astype(BF16),
                                     preferred_element_type=F32).astype(BF16)

    return pl.pallas_call(
        body, name=name, grid=(T // tm,),
        in_specs=[pl.BlockSpec((tm, D), lambda i: (i, 0)), pl.BlockSpec((M, D), lambda i: (0, 0)),
                  pl.BlockSpec((M, D), lambda i: (0, 1))],
        out_specs=pl.BlockSpec((tm, D), lambda i: (i, 0)),
        out_shape=jax.ShapeDtypeStruct((T, D), BF16),
        compiler_params=_params("parallel"))(q, kv, kv)


def attn_bwd(q, kv, do, name):
    T, D = q.shape
    M = kv.shape[0]
    hd = D // N_HEADS
    tm = _pick(T, 512, 8)
    sc = 1.0 / math.sqrt(hd)

    def body(q_ref, k_ref, v_ref, do_ref, dq_ref, dk_ref, dv_ref):
        i = pl.program_id(0)

        @pl.when(i == 0)
        def _():
            dk_ref[...] = jnp.zeros_like(dk_ref)
            dv_ref[...] = jnp.zeros_like(dv_ref)

        for h in range(N_HEADS):
            cols = slice(h * hd, (h + 1) * hd)
            qh, kh = q_ref[:, cols].astype(BF16), k_ref[:, cols].astype(BF16)
            vh, doh = v_ref[:, cols].astype(BF16), do_ref[:, cols].astype(BF16)
            s = _nt(qh, kh) * sc
            p = jnp.exp(s - jnp.max(s, axis=-1, keepdims=True))
            p = p / jnp.sum(p, axis=-1, keepdims=True)
            dp = _nt(doh, vh)
            dv_ref[:, cols] += _tn(p.astype(BF16), doh)
            ds = (p * (dp - jnp.sum(dp * p, axis=-1, keepdims=True)) * sc).astype(BF16)
            dq_ref[:, cols] = jnp.dot(ds, kh, preferred_element_type=F32).astype(BF16)
            dk_ref[:, cols] += _tn(ds, qh)

    tile = pl.BlockSpec((tm, D), lambda i: (i, 0))
    mem = jax.ShapeDtypeStruct((M, D), F32)
    return pl.pallas_call(
        body, name=name, grid=(T // tm,),
        in_specs=[tile, pl.BlockSpec((M, D), lambda i: (0, 0)), pl.BlockSpec((M, D), lambda i: (0, 1)), tile],
        out_specs=[tile, pl.BlockSpec((M, D), lambda i: (0, 0)), pl.BlockSpec((M, D), lambda i: (0, 0))],
        out_shape=[jax.ShapeDtypeStruct((T, D), BF16), mem, mem],
        compiler_params=_params("arbitrary"))(q, kv, kv, do)


def loss_head(x, g, target, name):
    T, D = x.shape
    tm = _pick(T, 512, 8)

    def body(x_ref, g_ref, t_ref, l_ref, dx_ref, dg_ref):
        i = pl.program_id(0)

        @pl.when(i == 0)
        def _():
            l_ref[...] = jnp.zeros_like(l_ref)
            dg_ref[...] = jnp.zeros_like(dg_ref)

        xv = x_ref[...]
        r = lax.rsqrt(jnp.mean(xv * xv, axis=-1, keepdims=True) + EPS)
        xh = xv * r
        err = xh * g_ref[...] - t_ref[...]
        l_ref[...] += 0.5 * jnp.sum(jnp.mean(err * err, axis=-1, keepdims=True), axis=0, keepdims=True)
        dy = err * (1.0 / D)
        dg_ref[...] += jnp.sum(dy * xh, axis=0, keepdims=True)
        dxh = dy * g_ref[...]
        dx_ref[...] = r * (dxh - xh * jnp.mean(dxh * xh, axis=-1, keepdims=True))

    tile = pl.BlockSpec((tm, D), lambda i: (i, 0))
    vec = pl.BlockSpec((1, D), lambda i: (0, 0))
    return pl.pallas_call(
        body, name=name, grid=(T // tm,),
        in_specs=[tile, vec, tile],
        out_specs=[pl.BlockSpec((1, 128), lambda i: (0, 0)), tile, vec],
        out_shape=[jax.ShapeDtypeStruct((1, 128), F32), jax.ShapeDtypeStruct((T, D), F32),
                   jax.ShapeDtypeStruct((1, D), F32)],
        compiler_params=_params("arbitrary"))(x, g, target)


def _block_diag(p):
    G, gd, _ = p.shape
    rows = [jnp.concatenate([p[g] if g == c else jnp.zeros((gd, gd), p.dtype) for c in range(G)], axis=1)
            for g in range(G)]
    return jnp.concatenate(rows, axis=0)


class _LazyWeight:
    def __init__(self, fetch, name, latest):
        self.fetch, self.name, self.latest = fetch, name, latest

    def __getitem__(self, l):
        return self.fetch(self.name, l, self.latest[0])


def _local_step(x, mem, target, fetch, ws, L, progress=lambda event, l, grads, values: values):
    T, D = x.shape
    W = D // 4
    H = ws["sgu_w"].shape[1]
    row = lambda v: v.reshape(1, -1)
    latest = [x]
    wb = {n: _LazyWeight(fetch, n, latest) for n in BIG}
    saved = []
    for l in range(L):
        s = {"x0": x}
        latest[0] = x
        x, *s["ffn1"] = ffn_fwd(x, row(ws["norm_ffn1"][l]), wb["ffn1_w_in"][l], wb["ffn1_w_out"][l], "ffn_fwd")
        s["x1"] = x
        latest[0] = x
        z, s["h_mix"] = norm_matmul(x, row(ws["norm_mix"][l]), wb["mix_w_in"][l], "mix_in")
        s["z"] = z
        s["bias"] = jnp.repeat(ws["sgu_b"][l].T, W // H, axis=1)
        s["pbd"] = _block_diag(ws["pool_w"][l]).astype(BF16)
        y = jnp.concatenate([
            mix_a_fwd(z, ws["sconv_w"][l], "mix_a_fwd"),
            mix_b_fwd(z, row(ws["sgu_norm_g"][l]), ws["sgu_w"][l], s["bias"], "mix_b_fwd"),
            mix_c_fwd(z, ws["cconv_w"][l], row(ws["cconv_ln_g"][l]), row(ws["cconv_ln_b"][l]), "mix_c_fwd"),
            mix_d_fwd(z, s["pbd"], row(ws["pool_scale"][l]), "mix_d_fwd")], axis=1)
        s["y"] = y
        x = matmul_res(x, y, wb["mix_w_out"][l], "mix_out")
        s["x2"] = x
        s["q"], s["hq"] = norm_matmul(x, row(ws["norm_xattn"][l]), wb["xattn_wq"][l], "attn_q", out_dtype=BF16)
        s["kv"], s["mn"] = norm_matmul(mem, row(ws["norm_mem"][l]), wb["xattn_wkv"][l], "attn_kv")
        s["o"] = attn_fwd(s["q"], s["kv"], "attn_fwd")
        x = matmul_res(x, s["o"], wb["xattn_wo"][l], "attn_out")
        s["x3"] = x
        x, *s["ffn2"] = ffn_fwd(x, row(ws["norm_ffn2"][l]), wb["ffn2_w_in"][l], wb["ffn2_w_out"][l], "ffn_fwd")
        saved.append(s)

    loss, dx, dg_final = loss_head(x, row(ws["norm_final"]), target, "loss_head")
    grads = {n: [None] * L for n in WEIGHTS if n != "norm_final"}
    grads["norm_final"] = dg_final.reshape(-1)

    def pin(dx, names, l):
        dx, made = lax.optimization_barrier((dx, [grads[n][l] for n in names]))
        for n, g in zip(names, made):
            grads[n][l] = g
        return dx

    def after_stages(event, l, values):
        return progress(event, l, grads, values)

    def ffn_back(xin, kept, dxo, gname, win, wout, l, event):
        h, zg, zu = kept
        a, dzg, dzu = ffn_dz(dxo, zg, zu, wb[wout][l], "ffn_dz")
        last = l == 0 and event == "ffn1_mid"
        if not last:
            dxn, dg = ffn_dh(xin, dxo, row(ws[gname][l]), dzg, dzu, wb[win][l], "ffn_dh")
            dxn, h, a = after_stages(event, l, (dxn, h, a))
        else:
            h, a = after_stages(event, l, (h, a))
        grads[win][l] = matmul_tn(h, dzg, 1.0, "ffn_dwin", b2=dzu)
        grads[wout][l] = matmul_tn(a, dxo, 0.5, "ffn_dwout")
        if last:
            dzg, dzu = after_stages("ffn1_grads", l, (dzg, dzu))
            dxn, dg = ffn_dh(xin, dxo, row(ws[gname][l]), dzg, dzu, wb[win][l], "ffn_dh")
        grads[gname][l] = dg.reshape(-1)
        return dxn if last else pin(dxn, (win, wout), l)

    for l in reversed(range(L)):
        s = saved[l]
        dx = ffn_back(s["x3"], s["ffn2"], dx, "norm_ffn2", "ffn2_w_in", "ffn2_w_out", l, "ffn2_mid")
        dx, = after_stages("ffn2", l, (dx,))
        grads["xattn_wo"][l] = matmul_tn(s["o"], dx, 1.0, "dw_sq")
        do = matmul_nt(dx, wb["xattn_wo"][l], "attn_do", out_dtype=BF16)
        dq, dk, dv = attn_bwd(s["q"], s["kv"], do, "attn_bwd")
        grads["xattn_wq"][l] = matmul_tn(s["hq"], dq, 1.0, "dw_sq")
        dhq = matmul_nt(dq, wb["xattn_wq"][l], "attn_dhq")
        dx, dg = rmsnorm_bwd(dx, dhq, s["x2"], row(ws["norm_xattn"][l]), "norm_bwd")
        grads["norm_xattn"][l] = dg.reshape(-1)
        dkv = jnp.concatenate([dk, dv], axis=1)
        grads["xattn_wkv"][l] = matmul_tn(s["mn"], dkv, 1.0, "attn_dwkv")
        dmn = matmul_nt(dkv, wb["xattn_wkv"][l], "attn_dmn")
        _, dg = rmsnorm_bwd(None, dmn, mem, row(ws["norm_mem"][l]), "norm_mem_bwd")
        grads["norm_mem"][l] = dg.reshape(-1)
        dx = pin(dx, ("xattn_wo", "xattn_wq", "xattn_wkv", "norm_mem"), l)
        dx, = after_stages("attn", l, (dx,))
        grads["mix_w_out"][l] = matmul_tn(s["y"], dx, 1.0, "dw_sq")
        dy = matmul_nt(dx, wb["mix_w_out"][l], "mix_dy")
        z = s["z"]
        dab, dac, dax, dws = mix_a_bwd(z, dy, ws["sconv_w"][l], "mix_a_bwd")
        dbu, dbv, dwsgu, dbs, dgs = mix_b_bwd(z, dy, row(ws["sgu_norm_g"][l]), ws["sgu_w"][l], s["bias"], "mix_b_bwd")
        dca, dcg, dwc, dgc, dbc = mix_c_bwd(z, dy, ws["cconv_w"][l], row(ws["cconv_ln_g"][l]),
                                            row(ws["cconv_ln_b"][l]), "mix_c_bwd")
        ddw, dpbd, dsc = mix_d_bwd(z, dy, s["pbd"], row(ws["pool_scale"][l]), "mix_d_bwd")
        grads["sconv_w"][l], grads["cconv_w"][l] = dws, dwc
        grads["sgu_w"][l], grads["sgu_b"][l], grads["sgu_norm_g"][l] = dwsgu, dbs[:, :H].T, dgs.reshape(-1)
        grads["cconv_ln_g"][l], grads["cconv_ln_b"][l] = dgc.reshape(-1), dbc.reshape(-1)
        gd = W // len(POOL_WINDOWS)
        grads["pool_w"][l] = jnp.stack([dpbd[g * gd:(g + 1) * gd, g * gd:(g + 1) * gd] for g in range(len(POOL_WINDOWS))])
        grads["pool_scale"][l] = dsc.reshape(-1)
        dz = jnp.concatenate([dab, dac, dax, dbu, dbv, dca, dcg, ddw], axis=1)
        grads["mix_w_in"][l] = matmul_tn(s["h_mix"], dz, 1.0, "mix_dwin")
        dh = matmul_nt(dz, wb["mix_w_in"][l], "mix_dh")
        dx, dg = rmsnorm_bwd(dx, dh, s["x1"], row(ws["norm_mix"][l]), "norm_bwd")
        grads["norm_mix"][l] = dg.reshape(-1)
        dx = pin(dx, ("mix_w_out", "mix_w_in"), l)
        dx, = after_stages("mix", l, (dx,))
        dx = ffn_back(s["x0"], s["ffn1"], dx, "norm_ffn1", "ffn1_w_in", "ffn1_w_out", l, "ffn1_mid")
        dx, = after_stages("layer", l, (dx,))

    return loss[0, 0], dx, grads


ANY = pl.BlockSpec(memory_space=pl.ANY)


def _other_chips(x, y):
    return [(1 - x, y), (x, 1 - y), (1 - x, 1 - y)]


def _shard_slice(ref, axis, chip, size):
    idx = [slice(None)] * len(ref.shape)
    idx[axis] = pl.ds(pl.multiple_of(chip * size, size), size)
    return ref.at[tuple(idx)]


def all_gather_chips(shards, axes, name):
    n = len(shards)

    def body(*refs):
        ins, outs = refs[:n], refs[n:2 * n]
        send, recv, loc = refs[2 * n:]
        x, y, c = lax.axis_index("x"), lax.axis_index("y"), lax.axis_index("c")
        me = 2 * x + y
        chips = _other_chips(x, y)
        started = []
        for i in range(n):
            size = ins[i].shape[axes[i]]
            cp = pltpu.make_async_copy(ins[i], _shard_slice(outs[i], axes[i], me, size), loc.at[i])
            cp.start()
            started.append(cp)
        sends = []
        for i in range(n):
            size = ins[i].shape[axes[i]]
            for j, (px, py) in enumerate(chips):
                cp = pltpu.make_async_remote_copy(
                    src_ref=ins[i], dst_ref=_shard_slice(outs[i], axes[i], me, size),
                    send_sem=send.at[i, j], recv_sem=recv.at[i, j], device_id=(px, py, c), device_id_type=MESH_ID)
                cp.start()
                sends.append(cp)
        for i in range(n):
            size = ins[i].shape[axes[i]]
            for j, (px, py) in enumerate(chips):
                pltpu.make_async_remote_copy(
                    src_ref=ins[i], dst_ref=_shard_slice(outs[i], axes[i], 2 * px + py, size),
                    send_sem=send.at[i, j], recv_sem=recv.at[i, j], device_id=(px, py, c),
                    device_id_type=MESH_ID).wait_recv()
        for cp in sends:
            cp.wait_send()
        for cp in started:
            cp.wait()

    def full(a, ax):
        shape = list(a.shape)
        shape[ax] *= N_CHIPS
        return jax.ShapeDtypeStruct(tuple(shape), a.dtype)

    return pl.pallas_call(
        body, name=name, in_specs=[ANY] * n, out_specs=[ANY] * n,
        out_shape=[full(a, ax) for a, ax in zip(shards, axes)],
        scratch_shapes=[pltpu.SemaphoreType.DMA((n, 3)), pltpu.SemaphoreType.DMA((n, 3)),
                        pltpu.SemaphoreType.DMA((n,))],
        compiler_params=pltpu.CompilerParams(has_side_effects=True))(*shards)


def cast_into_slot(shard, axis, chip, name):
    L, K, N = shard.shape
    bm = _pick(K, 256, 16)
    full = (K * N_CHIPS, N) if axis == 1 else (K, N * N_CHIPS)
    nb = K // bm

    def body(c_ref, s_ref, *o_refs):
        for l in range(L):
            o_refs[l][...] = s_ref[l].astype(BF16)

    out_map = (lambda i, c: (c[0] * nb + i, 0)) if axis == 1 else (lambda i, c: (i, c[0]))
    spec = pltpu.PrefetchScalarGridSpec(
        num_scalar_prefetch=1, grid=(nb,),
        in_specs=[pl.BlockSpec((L, bm, N), lambda i, c: (0, i, 0))],
        out_specs=[pl.BlockSpec((bm, N), out_map)] * L)
    return pl.pallas_call(body, name=name, grid_spec=spec, out_shape=[jax.ShapeDtypeStruct(full, BF16)] * L,
                          compiler_params=_params("parallel"))(chip, shard)


HBM = pl.BlockSpec(memory_space=pltpu.HBM)
SEM = pl.BlockSpec(memory_space=pltpu.SEMAPHORE)
DATAFLOW = pltpu.SideEffectType.DATAFLOW_SIDE_EFFECTING


def split_start(name, sources, landing, n_sems, make, after):
    ns, nl, na = len(sources), len(landing), len(after)

    def body(*refs):
        out, _ = make(refs[:ns], refs[ns:ns + nl], refs[ns + nl + na], refs[ns + nl + na + 1])
        for cp in out:
            cp.start()
        refs[-1][...] = jnp.zeros_like(refs[-1])

    hbm = lambda b: pltpu.with_memory_space_constraint(b, pltpu.HBM)
    res = pl.pallas_call(
        body, name=name,
        out_shape=(pltpu.SemaphoreType.DMA((n_sems,)), pltpu.SemaphoreType.DMA((n_sems,)),
                   *[pltpu.HBM(b.shape, b.dtype) for b in landing], jax.ShapeDtypeStruct((8, 128), F32)),
        in_specs=[HBM] * (ns + nl) + [ANY] * na, out_specs=(SEM, SEM, *[HBM] * nl, pl.BlockSpec(memory_space=pltpu.VMEM)),
        input_output_aliases={ns + i: 2 + i for i in range(nl)},
        compiler_params=pltpu.CompilerParams(has_side_effects=DATAFLOW))(
            *[hbm(b) for b in sources], *[hbm(b) for b in landing], *after)
    return res[0], res[1], list(res[2:2 + nl]), res[-1]


def split_wait(name, sources, landing, send, recv, make, after):
    ns, nl = len(sources), len(landing)

    def body(*refs):
        _, back = make(refs[:ns], refs[ns:ns + nl], refs[ns + nl], refs[ns + nl + 1])
        for cp in back:
            cp.wait_send()
            cp.wait_recv()

    return list(pl.pallas_call(
        body, name=name, out_shape=tuple(pltpu.HBM(b.shape, b.dtype) for b in landing),
        in_specs=[HBM] * (ns + nl) + [SEM, SEM] + [ANY] * len(after), out_specs=tuple([HBM] * nl),
        input_output_aliases={ns + i: i for i in range(nl)},
        compiler_params=pltpu.CompilerParams(has_side_effects=DATAFLOW))(
            *[pltpu.with_memory_space_constraint(b, pltpu.HBM) for b in sources], *landing, send, recv, *after))


def _half_slot(buf, axis, chip, half):
    K, N = buf.shape
    if axis == 1:
        n = N // N_CHIPS
        return buf.at[pl.ds(pl.multiple_of(half * (K // 2), K // 2), K // 2), pl.ds(pl.multiple_of(chip * n, n), n)]
    k2 = K // N_CHIPS // 2
    return buf.at[pl.ds(pl.multiple_of((2 * chip + half) * k2, k2), k2), :]


def gather_copies(axes):
    def make(_, bufs, send, recv):
        x, y, c = lax.axis_index("x"), lax.axis_index("y"), lax.axis_index("c")
        out, back = [], []
        for i, (buf, ax) in enumerate(zip(bufs, axes)):
            mine = _half_slot(buf, ax, 2 * x + y, c)
            for j, (px, py) in enumerate(_other_chips(x, y)):
                kw = dict(send_sem=send.at[3 * i + j], recv_sem=recv.at[3 * i + j], device_id=(px, py, c),
                          device_id_type=MESH_ID)
                out.append(pltpu.make_async_remote_copy(src_ref=mine, dst_ref=mine, **kw))
                back.append(pltpu.make_async_remote_copy(src_ref=mine, dst_ref=_half_slot(buf, ax, 2 * px + py, c), **kw))
        return out, back
    return make


def forward_sibling(bufs, axes, name):
    n = len(bufs)

    def body(*refs):
        ins = refs[:n]
        send, recv = refs[2 * n:]
        x, y, c = lax.axis_index("x"), lax.axis_index("y"), lax.axis_index("c")
        out, back = [], []
        for i, ax in enumerate(axes):
            for j, (px, py) in enumerate(_other_chips(x, y)):
                have = _half_slot(ins[i], ax, 2 * px + py, c)
                kw = dict(send_sem=send.at[3 * i + j], recv_sem=recv.at[3 * i + j], device_id=(x, y, 1 - c),
                          device_id_type=MESH_ID)
                out.append(pltpu.make_async_remote_copy(src_ref=have, dst_ref=have, **kw))
                back.append(pltpu.make_async_remote_copy(src_ref=have, dst_ref=_half_slot(ins[i], ax, 2 * px + py, 1 - c), **kw))
        for cp in out:
            cp.start()
        for cp in back:
            cp.wait_recv()
        for cp in out:
            cp.wait_send()

    return pl.pallas_call(
        body, name=name, in_specs=[ANY] * n, out_specs=[ANY] * n,
        out_shape=[jax.ShapeDtypeStruct(b.shape, b.dtype) for b in bufs],
        input_output_aliases={i: i for i in range(n)},
        scratch_shapes=[pltpu.SemaphoreType.DMA((3 * n,)), pltpu.SemaphoreType.DMA((3 * n,))],
        compiler_params=pltpu.CompilerParams(has_side_effects=True))(*bufs)


def all_reduce_small(p, name):
    R = p.shape[0]

    def body(p_ref, o_ref, sib_ref, chip_ref, send, recv):
        x, y, c = lax.axis_index("x"), lax.axis_index("y"), lax.axis_index("c")
        me = 2 * x + y
        chips = _other_chips(x, y)
        pair = pltpu.make_async_remote_copy(src_ref=p_ref, dst_ref=sib_ref, send_sem=send.at[0], recv_sem=recv.at[0],
                                            device_id=(x, y, 1 - c), device_id_type=MESH_ID)
        pair.start()
        pair.wait()
        chip_ref[me] = p_ref[...] + sib_ref[...]
        sends = []
        for j, (px, py) in enumerate(chips):
            cp = pltpu.make_async_remote_copy(src_ref=chip_ref.at[me], dst_ref=chip_ref.at[me], send_sem=send.at[1 + j],
                                              recv_sem=recv.at[1 + j], device_id=(px, py, c), device_id_type=MESH_ID)
            cp.start()
            sends.append(cp)
        for j, (px, py) in enumerate(chips):
            pltpu.make_async_remote_copy(src_ref=chip_ref.at[me], dst_ref=chip_ref.at[2 * px + py], send_sem=send.at[1 + j],
                                         recv_sem=recv.at[1 + j], device_id=(px, py, c), device_id_type=MESH_ID).wait_recv()
        for cp in sends:
            cp.wait_send()
        o_ref[...] = ((chip_ref[0] + chip_ref[1]) + chip_ref[2]) + chip_ref[3]

    vm = pl.BlockSpec(memory_space=pltpu.VMEM)
    return pl.pallas_call(
        body, name=name, in_specs=[vm], out_specs=vm, out_shape=jax.ShapeDtypeStruct((R, 128), F32),
        scratch_shapes=[pltpu.VMEM((R, 128), F32), pltpu.VMEM((N_CHIPS, R, 128), F32),
                        pltpu.SemaphoreType.DMA((4,)), pltpu.SemaphoreType.DMA((4,))],
        compiler_params=pltpu.CompilerParams(has_side_effects=True, vmem_limit_bytes=VMEM_LIMIT))(p)


def _grad_view(g, axis):
    K, N = g.shape
    return g.reshape(1, 2, K // 2, N) if axis == 1 else g.reshape(N_CHIPS, 2, K // N_CHIPS // 2, N)


def pair_copies(gvs, others, send, recv):
    x, y, c = lax.axis_index("x"), lax.axis_index("y"), lax.axis_index("c")
    out = [pltpu.make_async_remote_copy(src_ref=gv.at[:, 1 - c], dst_ref=o, send_sem=send.at[i], recv_sem=recv.at[i],
                                        device_id=(x, y, 1 - c), device_id_type=MESH_ID)
           for i, (gv, o) in enumerate(zip(gvs, others))]
    return out, out


def chip_copies(axes):
    def piece(s, ax, chip):
        if ax == 1:
            n = s.shape[2] // N_CHIPS
            return s.at[0, :, pl.ds(pl.multiple_of(chip * n, n), n)]
        return s.at[chip]

    def make(sums, qs, send, recv):
        x, y, c = lax.axis_index("x"), lax.axis_index("y"), lax.axis_index("c")
        out = []
        for i, (s, q, ax) in enumerate(zip(sums, qs, axes)):
            for j, (px, py) in enumerate(_other_chips(x, y)):
                out.append(pltpu.make_async_remote_copy(
                    src_ref=piece(s, ax, 2 * px + py), dst_ref=q.at[j], send_sem=send.at[3 * i + j],
                    recv_sem=recv.at[3 * i + j], device_id=(px, py, c), device_id_type=MESH_ID))
        return out, out
    return make


def share_sibling(halves, name):
    n = len(halves)

    def body(*refs):
        ins = refs[:n]
        send, recv = refs[2 * n:]
        x, y, c = lax.axis_index("x"), lax.axis_index("y"), lax.axis_index("c")
        cps = [pltpu.make_async_remote_copy(src_ref=ins[i].at[c], dst_ref=ins[i].at[c], send_sem=send.at[i], recv_sem=recv.at[i],
                                            device_id=(x, y, 1 - c), device_id_type=MESH_ID) for i in range(n)]
        for cp in cps:
            cp.start()
        for i in range(n):
            pltpu.make_async_remote_copy(src_ref=ins[i].at[c], dst_ref=ins[i].at[1 - c], send_sem=send.at[i], recv_sem=recv.at[i],
                                         device_id=(x, y, 1 - c), device_id_type=MESH_ID).wait_recv()
        for cp in cps:
            cp.wait_send()

    return pl.pallas_call(
        body, name=name, in_specs=[ANY] * n, out_specs=[ANY] * n,
        out_shape=[jax.ShapeDtypeStruct(a.shape, a.dtype) for a in halves],
        input_output_aliases={i: i for i in range(n)},
        scratch_shapes=[pltpu.SemaphoreType.DMA((n,)), pltpu.SemaphoreType.DMA((n,))],
        compiler_params=pltpu.CompilerParams(has_side_effects=True))(*halves)


def add_pair(gv, other, place, name):
    A, _, rows, N = gv.shape
    bm, bn = _pick(rows, 256, 16), _pick(N, 1408, 128)

    def body(p_ref, g_ref, o_ref, out_ref):
        out_ref[...] = (g_ref[...] + o_ref[...]).astype(GRAD_WIRE)

    spec = pltpu.PrefetchScalarGridSpec(
        num_scalar_prefetch=1, grid=(A, rows // bm, N // bn),
        in_specs=[pl.BlockSpec((None, None, bm, bn), lambda a, i, j, p: (a, p[1], i, j)),
                  pl.BlockSpec((None, bm, bn), lambda a, i, j, p: (a, i, j))],
        out_specs=pl.BlockSpec((None, bm, bn), lambda a, i, j, p: (a, i, j)))
    return pl.pallas_call(body, name=name, grid_spec=spec, out_shape=jax.ShapeDtypeStruct((A, rows, N), GRAD_WIRE),
                          compiler_params=_params("parallel", "parallel", "parallel"))(place, gv, other)


def add_chips(s, q, axis, place, name):
    _, rows, n = q.shape
    bm, bn = _pick(rows, 256, 16), _pick(n, 1408, 128)
    nbj = n // bn

    def body(p_ref, s_ref, q_ref, o_ref):
        o_ref[...] = ((s_ref[...].astype(F32) + q_ref[0].astype(F32)) + q_ref[1].astype(F32)) + q_ref[2].astype(F32)

    mine = (lambda i, j, p: (p[0], i, j)) if axis == 0 else (lambda i, j, p: (0, i, p[0] * nbj + j))
    spec = pltpu.PrefetchScalarGridSpec(
        num_scalar_prefetch=1, grid=(rows // bm, nbj),
        in_specs=[pl.BlockSpec((None, bm, bn), mine), pl.BlockSpec((3, bm, bn), lambda i, j, p: (0, i, j))],
        out_specs=pl.BlockSpec((None, bm, bn), lambda i, j, p: (p[1], i, j)))
    return pl.pallas_call(body, name=name, grid_spec=spec, out_shape=jax.ShapeDtypeStruct((2, rows, n), F32),
                          compiler_params=_params("parallel", "parallel"))(place, s, q)


def adamw(w, g, m, v, name):
    R, N = w.shape
    bm = _pick(R, 256, 8)
    c1 = 1.0 / (1.0 - ADAM_B1 ** ADAM_STEP)
    c2 = 1.0 / (1.0 - ADAM_B2 ** ADAM_STEP)

    def body(w_ref, g_ref, m_ref, v_ref, d_ref, nm_ref, nv_ref):
        gv = g_ref[...]
        nm = ADAM_B1 * m_ref[...] + (1.0 - ADAM_B1) * gv
        nv = ADAM_B2 * v_ref[...] + (1.0 - ADAM_B2) * (gv * gv)
        nm_ref[...] = nm
        nv_ref[...] = nv
        d_ref[...] = -ADAM_LR * ((nm * c1) / (jnp.sqrt(nv * c2) + ADAM_EPS) + ADAM_WD * w_ref[...])

    blk = pl.BlockSpec((bm, N), lambda i: (i, 0))
    out = jax.ShapeDtypeStruct((R, N), F32)
    return pl.pallas_call(body, name=name, grid=(R // bm,), in_specs=[blk] * 4, out_specs=[blk] * 3,
                          out_shape=[out, out, out], compiler_params=_params("parallel"))(w, g, m, v)


def adamw_layers(w, g0, g1, m, v, name):
    _, k, n = w.shape
    bm = _pick(k, 256, 8)
    c1 = 1.0 / (1.0 - ADAM_B1 ** ADAM_STEP)
    c2 = 1.0 / (1.0 - ADAM_B2 ** ADAM_STEP)

    def body(w_ref, g0_ref, g1_ref, m_ref, v_ref, g_ref, d_ref, nm_ref, nv_ref):
        def step(gv):
            nm = ADAM_B1 * m_ref[...] + (1.0 - ADAM_B1) * gv
            nv = ADAM_B2 * v_ref[...] + (1.0 - ADAM_B2) * (gv * gv)
            g_ref[...] = gv
            nm_ref[...] = nm
            nv_ref[...] = nv
            d_ref[...] = -ADAM_LR * ((nm * c1) / (jnp.sqrt(nv * c2) + ADAM_EPS) + ADAM_WD * w_ref[...])

        @pl.when(pl.program_id(0) == 0)
        def _():
            step(g0_ref[...])

        @pl.when(pl.program_id(0) == 1)
        def _():
            step(g1_ref[...])

    blk = pl.BlockSpec((None, bm, n), lambda l, i: (l, i, 0))
    out = jax.ShapeDtypeStruct(w.shape, F32)
    return pl.pallas_call(
        body, name=name, grid=(2, k // bm),
        in_specs=[blk, pl.BlockSpec((bm, n), lambda l, i: (i * (1 - l), 0)), pl.BlockSpec((bm, n), lambda l, i: (i * l, 0)),
                  blk, blk],
        out_specs=[blk] * 4, out_shape=[out] * 4, compiler_params=_params("arbitrary", "arbitrary"))(w, g0, g1, m, v)


def _pack(arrays):
    flat = jnp.concatenate([a.reshape(-1) for a in arrays])
    rows = -(-flat.shape[0] // (256 * 128)) * 256
    return jnp.pad(flat, (0, rows * 128 - flat.shape[0])).reshape(rows, 128)


def _unpack(p, shapes):
    flat, out, at = p.reshape(-1), [], 0
    for s in shapes:
        n = math.prod(s)
        out.append(flat[at:at + n].reshape(s))
        at += n
    return out


def kernel(x, mem, norm_ffn1, ffn1_w_in, ffn1_w_out, norm_mix, mix_w_in, sconv_w, sgu_norm_g, sgu_w, sgu_b, cconv_w, cconv_ln_g, cconv_ln_b, pool_w, pool_scale, mix_w_out, norm_xattn, norm_mem, xattn_wq, xattn_wkv, xattn_wo, norm_ffn2, ffn2_w_in, ffn2_w_out, norm_final, loss_target, m_norm_ffn1, m_ffn1_w_in, m_ffn1_w_out, m_norm_mix, m_mix_w_in, m_sconv_w, m_sgu_norm_g, m_sgu_w, m_sgu_b, m_cconv_w, m_cconv_ln_g, m_cconv_ln_b, m_pool_w, m_pool_scale, m_mix_w_out, m_norm_xattn, m_norm_mem, m_xattn_wq, m_xattn_wkv, m_xattn_wo, m_norm_ffn2, m_ffn2_w_in, m_ffn2_w_out, m_norm_final, v_norm_ffn1, v_ffn1_w_in, v_ffn1_w_out, v_norm_mix, v_mix_w_in, v_sconv_w, v_sgu_norm_g, v_sgu_w, v_sgu_b, v_cconv_w, v_cconv_ln_g, v_cconv_ln_b, v_pool_w, v_pool_scale, v_mix_w_out, v_norm_xattn, v_norm_mem, v_xattn_wq, v_xattn_wkv, v_xattn_wo, v_norm_ffn2, v_ffn2_w_in, v_ffn2_w_out, v_norm_final):
    given = dict(locals())
    w = {n: given[n] for n in WEIGHTS}
    L = ffn1_w_in.shape[0]
    assert L == 2, "the reduce-scatter gives one layer to each core of a chip"
    chip = 2 * lax.axis_index("x") + lax.axis_index("y")
    chip1 = chip.astype(jnp.int32).reshape(1)
    core = lax.axis_index("c").astype(jnp.int32).reshape(1)
    place = jnp.concatenate([chip1, core])

    axis = {n: 1 if n in COL_SHARDED else 0 for n in BIG}
    bufs = {}
    for n in BIG:
        for l, b in enumerate(cast_into_slot(w[n], axis[n] + 1, chip1, "cast_weights")):
            bufs[n, l] = b
    groups = {"a": [(n, 0) for n in BIG[:2]], "b": [(n, 0) for n in BIG[2:]], "c": [(n, 1) for n in BIG]}
    wc = sconv_w.shape[-1]
    conv_rows = [w[n].reshape(-1, wc) for n in SMALL_CONV]
    n_conv = sum(r.shape[0] for r in conv_rows)
    conv_pack = jnp.pad(jnp.concatenate(conv_rows, axis=0), ((0, -n_conv % 8), (0, 128 - wc)))[None]
    conv_all = all_gather_chips([conv_pack], [0], "gather_conv")[0]
    started, token = {}, conv_all
    for g, keys in groups.items():
        send, recv, thru, token = split_start("gather_start_" + g, [], [bufs[k] for k in keys], 3 * len(keys),
                                              gather_copies([axis[k[0]] for k in keys]), [token])
        started[g] = (send, recv, thru)
    ready = {}

    def fetch(n, l, after):
        g = next(g for g, keys in groups.items() if (n, l) in keys)
        if g not in ready:
            send, recv, thru = started[g]
            axes = [axis[k[0]] for k in groups[g]]
            done = split_wait("gather_wait_" + g, [], thru, send, recv, gather_copies(axes), [token if g == "a" else after])
            ready[g] = dict(zip(groups[g], forward_sibling(done, axes, "gather_forward")))
        return ready[g][n, l]

    conv_full = jnp.moveaxis(conv_all[:, :n_conv, :wc], 0, 1).reshape(n_conv, N_CHIPS * wc)
    ws = {n: w[n] for n in SMALL_REPL}
    at = 0
    for n in SMALL_CONV:
        rows = w[n].shape[0] * w[n].shape[1]
        ws[n] = conv_full[at:at + rows].reshape(w[n].shape[0], w[n].shape[1], N_CHIPS * wc)
        at += rows

    halves, state = {}, {}
    reduce_groups = {"r1": [(n, 1) for n in BIG], "r0a": [(n, 0) for n in BIG[2:]], "r0b": [(n, 0) for n in BIG[:2]]}
    plan = {("layer", 1): [("pair", "r1")],
            ("ffn2", 0): [("chips", "r1")],
            ("mix", 0): [("finish", "r1"), ("pair", "r0a")],
            ("ffn1_mid", 0): [("chips", "r0a")],
            ("ffn1_grads", 0): [("pair", "r0b")],
            ("layer", 0): [("finish", "r0a")]}


    def stage_pair(g, keys, grads, after):
        gvs = [_grad_view(grads[n][l], axis[n]) for n, l in keys]
        others = [lax.empty(gv.shape[:1] + gv.shape[2:], F32) for gv in gvs]
        send, recv, others, token = split_start("pair_start_" + g, gvs, others, len(gvs), pair_copies, [after])
        state[g] = dict(sources=gvs, send=send, recv=recv, landing=others, token=token)
        return [(state[g], "token")]

    def stage_chips(g, keys, grads, after):
        st = state[g]
        axes = [axis[n] for n, _ in keys]
        others = split_wait("pair_wait_" + g, st["sources"], st["landing"], st["send"], st["recv"], pair_copies,
                            [after, st["token"]])
        sums = [add_pair(gv, o, place, "add_pair") for gv, o in zip(st["sources"], others)]
        qs = [lax.empty((3, s.shape[1], s.shape[2] // (N_CHIPS if ax == 1 else 1)), GRAD_WIRE) for s, ax in zip(sums, axes)]
        send, recv, qs, token = split_start("chips_start_" + g, sums, qs, 3 * len(sums), chip_copies(axes), [after])
        state[g] = dict(sources=sums, send=send, recv=recv, landing=qs, token=token)
        return [(state[g], "token")]

    def stage_finish(g, keys, grads, after):
        st = state.pop(g)
        qs = split_wait("chips_wait_" + g, st["sources"], st["landing"], st["send"], st["recv"],
                        chip_copies([axis[n] for n, _ in keys]), [after, st["token"]])
        for key, s, q in zip(keys, st["sources"], qs):
            halves[key] = add_chips(s, q, axis[key[0]], place, "add_chips")
        return [(halves, key) for key in keys]

    stages = {"pair": stage_pair, "chips": stage_chips, "finish": stage_finish}

    def progress(event, l, grads, values):
        places = []
        for stage, g in plan.get((event, l), []):
            places += stages[stage](g, reduce_groups[g], grads, values[0])
        if places:
            values, tied = lax.optimization_barrier((values, [box[k] for box, k in places]))
            for (box, k), a in zip(places, tied):
                box[k] = a
        return values

    loss_part, grad_x, grads = _local_step(x[0], mem[0], loss_target[0], fetch, ws, L, progress)
    loss = lax.psum(loss_part, ("x", "y", "c"))

    small = SMALL_REPL + SMALL_CONV
    small_g = [grads[n] if n == "norm_final" else jnp.stack(grads[n]) for n in small]
    small_sum = all_reduce_small(_pack(small_g), "reduce_small")
    grad = dict(zip(small, _unpack(small_sum, [g.shape for g in small_g])))
    for n in SMALL_CONV:
        grad[n] = lax.dynamic_slice_in_dim(grad[n], chip * wc, wc, axis=2)

    delta, new_m, new_v = {}, {}, {}

    def finish_weights(names, ready):
        keys = [(n, l) for n in names for l in range(L)]
        shard_grad = dict(zip(keys, share_sibling(ready, "share_pair")))
        for n in names:
            g0, g1 = (shard_grad[n, l].reshape(w[n].shape[1:]) for l in range(L))
            grad[n], delta[n], new_m[n], new_v[n] = adamw_layers(w[n], g0, g1, given["m_" + n], given["v_" + n], "adamw")

    stage_chips("r0b", reduce_groups["r0b"], grads, small_sum)
    ready, (state["r0b"]["token"],) = lax.optimization_barrier(
        ([halves[n, l] for n in BIG[2:] for l in range(L)], [state["r0b"]["token"]]))
    finish_weights(BIG[2:], ready)
    stage_finish("r0b", reduce_groups["r0b"], grads, delta[BIG[-1]])
    finish_weights(BIG[:2], [halves[n, l] for n in BIG[:2] for l in range(L)])
    shapes = [w[n].shape for n in small]
    packed = [_pack([src[n] for n in small]) for src in
              (w, grad, {n: given["m_" + n] for n in small}, {n: given["v_" + n] for n in small})]
    for out, p in zip((delta, new_m, new_v), adamw(*packed, "adamw_small")):
        out.update(zip(small, _unpack(p, shapes)))

    return (loss, grad_x[None], *[grad[n] for n in WEIGHTS], *[delta[n] for n in WEIGHTS],
            *[new_m[n] for n in WEIGHTS], *[new_v[n] for n in WEIGHTS])
```

```python
import functools
import math

import jax
import jax.numpy as jnp
from jax import lax
from jax.experimental import pallas as pl
from jax.experimental.pallas import tpu as pltpu

F32 = jnp.float32
BF16 = jnp.bfloat16
EPS = 1e-6
SEQ_CHUNK = 128
POOL_WINDOWS = (2, 4, 8, 16)
N_HEADS = 4
ADAM_LR, ADAM_B1, ADAM_B2, ADAM_EPS, ADAM_WD, ADAM_STEP = 0.001, 0.9, 0.999, 1e-08, 0.01, 10
VMEM_LIMIT = 56 * 1024 * 1024
MESH_ID = pl.DeviceIdType.MESH
N_CHIPS = 4
GRAD_WIRE = BF16

BIG = ("ffn1_w_in", "ffn1_w_out", "mix_w_in", "mix_w_out", "xattn_wq", "xattn_wkv", "xattn_wo",
       "ffn2_w_in", "ffn2_w_out")
COL_SHARDED = ("ffn1_w_in", "mix_w_in", "xattn_wkv", "ffn2_w_in")
SMALL_CONV = ("sconv_w", "cconv_w")
SMALL_REPL = ("norm_ffn1", "norm_mix", "sgu_norm_g", "sgu_w", "sgu_b", "cconv_ln_g", "cconv_ln_b",
              "pool_w", "pool_scale", "norm_xattn", "norm_mem", "norm_ffn2", "norm_final")
WEIGHTS = ("norm_ffn1", "ffn1_w_in", "ffn1_w_out", "norm_mix", "mix_w_in", "sconv_w", "sgu_norm_g",
           "sgu_w", "sgu_b", "cconv_w", "cconv_ln_g", "cconv_ln_b", "pool_w", "pool_scale",
           "mix_w_out", "norm_xattn", "norm_mem", "xattn_wq", "xattn_wkv", "xattn_wo", "norm_ffn2",
           "ffn2_w_in", "ffn2_w_out", "norm_final")


def _pick(n, pref, align):
    best = None
    for d in range(align, min(n, pref) + 1, align):
        if n % d == 0:
            best = d
    return best or n


def _sig(x):
    return 0.5 * jnp.tanh(0.5 * x) + 0.5


def _nt(a, b):
    return lax.dot_general(a, b, (((1,), (1,)), ((), ())), preferred_element_type=F32)


def _tn(a, b):
    return lax.dot_general(a, b, (((0,), (0,)), ((), ())), preferred_element_type=F32)


def _params(*sem):
    return pltpu.CompilerParams(dimension_semantics=sem, vmem_limit_bytes=VMEM_LIMIT)


def norm_matmul(x, g, w, name, out_dtype=F32):
    T, D = x.shape
    N = w.shape[1]
    tm, tn = _pick(T, 512, 8), _pick(N, 2048, 128)

    def body(x_ref, g_ref, w_ref, o_ref, h_ref):
        j = pl.program_id(1)

        @pl.when(j == 0)
        def _():
            xv = x_ref[...]
            r = lax.rsqrt(jnp.mean(xv * xv, axis=-1, keepdims=True) + EPS)
            h_ref[...] = (xv * r * g_ref[...]).astype(BF16)

        o_ref[...] = jnp.dot(h_ref[...], w_ref[...], preferred_element_type=F32).astype(out_dtype)

    return pl.pallas_call(
        body, name=name, grid=(T // tm, N // tn),
        in_specs=[pl.BlockSpec((tm, D), lambda i, j: (i, 0)), pl.BlockSpec((1, D), lambda i, j: (0, 0)),
                  pl.BlockSpec((D, tn), lambda i, j: (0, j))],
        out_specs=[pl.BlockSpec((tm, tn), lambda i, j: (i, j)), pl.BlockSpec((tm, D), lambda i, j: (i, 0))],
        out_shape=[jax.ShapeDtypeStruct((T, N), out_dtype), jax.ShapeDtypeStruct((T, D), BF16)],
        compiler_params=_params("parallel", "arbitrary"))(x, g, w)


def matmul_res(res, a, w, name):
    T, K = a.shape
    N = w.shape[1]
    tm, tn = _pick(T, 512, 8), _pick(N, 1024, 128)

    def body(r_ref, a_ref, w_ref, o_ref):
        o_ref[...] = r_ref[...] + jnp.dot(a_ref[...].astype(BF16), w_ref[...], preferred_element_type=F32)

    return pl.pallas_call(
        body, name=name, grid=(T // tm, N // tn),
        in_specs=[pl.BlockSpec((tm, tn), lambda i, j: (i, j)), pl.BlockSpec((tm, K), lambda i, j: (i, 0)),
                  pl.BlockSpec((K, tn), lambda i, j: (0, j))],
        out_specs=pl.BlockSpec((tm, tn), lambda i, j: (i, j)),
        out_shape=jax.ShapeDtypeStruct((T, N), F32),
        compiler_params=_params("parallel", "parallel"))(res, a, w)


def matmul_nt(a, w, name, out_dtype=F32):
    T, N = a.shape
    M = w.shape[0]
    tm, tmm = _pick(T, 512, 8), _pick(M, 1024, 128)

    def body(a_ref, w_ref, o_ref):
        o_ref[...] = _nt(a_ref[...].astype(BF16), w_ref[...]).astype(out_dtype)

    return pl.pallas_call(
        body, name=name, grid=(T // tm, M // tmm),
        in_specs=[pl.BlockSpec((tm, N), lambda i, j: (i, 0)), pl.BlockSpec((tmm, N), lambda i, j: (j, 0))],
        out_specs=pl.BlockSpec((tm, tmm), lambda i, j: (i, j)),
        out_shape=jax.ShapeDtypeStruct((T, M), out_dtype),
        compiler_params=_params("parallel", "parallel"))(a, w)


def matmul_tn(a, b, scale, name, b2=None):
    T, M = a.shape
    Nb = b.shape[1]
    bm, bn, bk = _pick(M, 1408, 128), _pick(Nb, 1408, 128), _pick(T, 512, 8)
    nk, nj = T // bk, Nb // bn

    def body(a_ref, b_ref, *rest):
        o_ref = rest[-1]
        j, k = pl.program_id(1), pl.program_id(2)

        @pl.when(k == 0)
        def _():
            o_ref[...] = jnp.zeros_like(o_ref)

        a_blk = a_ref[...].astype(BF16)
        if b2 is None:
            o_ref[...] += _tn(a_blk, b_ref[...].astype(BF16))
        else:
            @pl.when(j < nj)
            def _():
                o_ref[...] += _tn(a_blk, b_ref[...].astype(BF16))

            @pl.when(j >= nj)
            def _():
                o_ref[...] += _tn(a_blk, rest[0][...].astype(BF16))

        if scale != 1.0:
            @pl.when(k == nk - 1)
            def _():
                o_ref[...] = o_ref[...] * scale

    if b2 is None:
        b_specs, operands, n_out = [pl.BlockSpec((bk, bn), lambda i, j, k: (k, j))], (a, b), nj
    else:
        first = lambda i, j, k: (jnp.where(j < nj, k, 0), jnp.where(j < nj, j, 0))
        second = lambda i, j, k: (jnp.where(j >= nj, k, 0), jnp.where(j >= nj, j - nj, 0))
        b_specs, operands, n_out = [pl.BlockSpec((bk, bn), first), pl.BlockSpec((bk, bn), second)], (a, b, b2), 2 * nj
    return pl.pallas_call(
        body, name=name, grid=(M // bm, n_out, nk),
        in_specs=[pl.BlockSpec((bk, bm), lambda i, j, k: (k, i))] + b_specs,
        out_specs=pl.BlockSpec((bm, bn), lambda i, j, k: (i, j)),
        out_shape=jax.ShapeDtypeStruct((M, n_out * bn), F32),
        compiler_params=_params("parallel", "parallel", "arbitrary"))(*operands)


def rmsnorm_bwd(dxo, dh, x, g, name):
    T, D = x.shape
    tm = _pick(T, 512, 8)
    has_res = dxo is not None

    def body(*refs):
        if has_res:
            dxo_ref, dh_ref, x_ref, g_ref, dx_ref, dg_ref = refs
        else:
            dh_ref, x_ref, g_ref, dx_ref, dg_ref = refs
        i = pl.program_id(0)

        @pl.when(i == 0)
        def _():
            dg_ref[...] = jnp.zeros_like(dg_ref)

        xv, dh_v = x_ref[...], dh_ref[...]
        r = lax.rsqrt(jnp.mean(xv * xv, axis=-1, keepdims=True) + EPS)
        xh = xv * r
        dg_ref[...] += jnp.sum(dh_v * xh, axis=0, keepdims=True)
        dxh = dh_v * g_ref[...]
        dx = r * (dxh - xh * jnp.mean(dxh * xh, axis=-1, keepdims=True))
        dx_ref[...] = dx + dxo_ref[...] if has_res else dx

    tile = pl.BlockSpec((tm, D), lambda i: (i, 0))
    vec = pl.BlockSpec((1, D), lambda i: (0, 0))
    args = ([dxo] if has_res else []) + [dh, x, g]
    return pl.pallas_call(
        body, name=name, grid=(T // tm,),
        in_specs=[tile] * (len(args) - 1) + [vec],
        out_specs=[tile, vec],
        out_shape=[jax.ShapeDtypeStruct((T, D), F32), jax.ShapeDtypeStruct((1, D), F32)],
        compiler_params=_params("arbitrary"))(*args)


def ffn_fwd(x, g, w_in, w_out, name):
    T, D = x.shape
    F = w_out.shape[0]
    tm, tf = _pick(T, 512, 8), _pick(F, 1408, 128)
    nf = F // tf

    def body(x_ref, g_ref, wg_ref, wu_ref, wo_ref, o_ref, h_ref, zg_ref, zu_ref, acc_ref):
        j = pl.program_id(1)

        @pl.when(j == 0)
        def _():
            xv = x_ref[...]
            r = lax.rsqrt(jnp.mean(xv * xv, axis=-1, keepdims=True) + EPS)
            h_ref[...] = (xv * r * g_ref[...]).astype(BF16)
            acc_ref[...] = jnp.zeros_like(acc_ref)

        h = h_ref[...]
        zg = jnp.dot(h, wg_ref[...], preferred_element_type=F32)
        zu = jnp.dot(h, wu_ref[...], preferred_element_type=F32)
        zg_ref[...] = zg.astype(BF16)
        zu_ref[...] = zu.astype(BF16)
        a = (zg * _sig(zg) * zu).astype(BF16)
        acc_ref[...] += jnp.dot(a, wo_ref[...], preferred_element_type=F32)

        @pl.when(j == nf - 1)
        def _():
            o_ref[...] = x_ref[...] + 0.5 * acc_ref[...]

    tile = pl.BlockSpec((tm, D), lambda i, j: (i, 0))
    fblk = pl.BlockSpec((tm, tf), lambda i, j: (i, j))
    hidden = jax.ShapeDtypeStruct((T, F), BF16)
    return pl.pallas_call(
        body, name=name, grid=(T // tm, nf),
        in_specs=[tile, pl.BlockSpec((1, D), lambda i, j: (0, 0)),
                  pl.BlockSpec((D, tf), lambda i, j: (0, j)), pl.BlockSpec((D, tf), lambda i, j: (0, j + nf)),
                  pl.BlockSpec((tf, D), lambda i, j: (j, 0))],
        out_specs=[tile, tile, fblk, fblk],
        out_shape=[jax.ShapeDtypeStruct((T, D), F32), jax.ShapeDtypeStruct((T, D), BF16), hidden, hidden],
        scratch_shapes=[pltpu.VMEM((tm, D), F32)],
        compiler_params=_params("parallel", "arbitrary"))(x, g, w_in, w_in, w_out)


def ffn_dz(dxo, zg, zu, w_out, name):
    T, D = dxo.shape
    F = w_out.shape[0]
    tm, tf = _pick(T, 512, 8), _pick(F, 256, 128)

    def body(dxo_ref, zg_ref, zu_ref, wo_ref, a_ref, dzg_ref, dzu_ref):
        do = (0.5 * dxo_ref[...]).astype(BF16)
        for j in range(F // tf):
            cols = slice(j * tf, (j + 1) * tf)
            zg, zu = zg_ref[:, cols].astype(F32), zu_ref[:, cols].astype(F32)
            s = _sig(zg)
            silu = zg * s
            a_ref[:, cols] = (silu * zu).astype(BF16)
            da = _nt(do, wo_ref[cols, :])
            dzu_ref[:, cols] = (da * silu).astype(BF16)
            dzg_ref[:, cols] = (da * zu * (s + silu * (1.0 - s))).astype(BF16)

    rows = pl.BlockSpec((tm, F), lambda i: (i, 0))
    hidden = jax.ShapeDtypeStruct((T, F), BF16)
    return pl.pallas_call(
        body, name=name, grid=(T // tm,),
        in_specs=[pl.BlockSpec((tm, D), lambda i: (i, 0)), rows, rows, pl.BlockSpec((F, D), lambda i: (0, 0))],
        out_specs=[rows, rows, rows], out_shape=[hidden, hidden, hidden],
        compiler_params=_params("parallel"))(dxo, zg, zu, w_out)


def dh_norm_bwd(x, dxo, g, parts, w, name):
    T, D = x.shape
    F = parts[0].shape[1]
    n = len(parts)
    tm = _pick(T, 256, 8)

    def body(x_ref, dxo_ref, g_ref, *refs):
        a_refs, w_refs, (dx_ref, dg_ref) = refs[:n], refs[n:2 * n], refs[2 * n:]
        i = pl.program_id(0)

        @pl.when(i == 0)
        def _():
            dg_ref[...] = jnp.zeros_like(dg_ref)

        dh = sum(_nt(a_ref[...], w_ref[...]) for a_ref, w_ref in zip(a_refs, w_refs))
        xv = x_ref[...]
        r = lax.rsqrt(jnp.mean(xv * xv, axis=-1, keepdims=True) + EPS)
        xh = xv * r
        dg_ref[...] += jnp.sum(dh * xh, axis=0, keepdims=True)
        dxh = dh * g_ref[...]
        dx_ref[...] = dxo_ref[...] + r * (dxh - xh * jnp.mean(dxh * xh, axis=-1, keepdims=True))

    tile = pl.BlockSpec((tm, D), lambda i: (i, 0))
    vec = pl.BlockSpec((1, D), lambda i: (0, 0))
    return pl.pallas_call(
        body, name=name, grid=(T // tm,),
        in_specs=[tile, tile, vec] + [pl.BlockSpec((tm, F), lambda i: (i, 0))] * n
                 + [pl.BlockSpec((D, F), lambda i, p=p: (0, p)) for p in range(n)],
        out_specs=[tile, vec],
        out_shape=[jax.ShapeDtypeStruct((T, D), F32), jax.ShapeDtypeStruct((1, D), F32)],
        compiler_params=_params("arbitrary"))(x, dxo, g, *parts, *([w] * n))


def _chunks(T, fn):
    def step(c, carry):
        fn(pl.multiple_of(c * SEQ_CHUNK, SEQ_CHUNK))
        return carry
    lax.fori_loop(0, T // SEQ_CHUNK, step, 0)


def _conv_taps(win, ktaps, pad):
    return [(win if k == ktaps - 1 else pltpu.roll(win, ktaps - 1 - k, 0))[pad:, :] for k in range(ktaps)]


def _conv_taps_t(win, ktaps, pad):
    n = win.shape[0]
    return [(win if k == ktaps - 1 else pltpu.roll(win, n - (ktaps - 1 - k), 0))[:n - pad, :] for k in range(ktaps)]


def _col(T, W, idx):
    return pl.BlockSpec((T, W), lambda i, idx=idx: (0, idx))


def _full(shape):
    return pl.BlockSpec(shape, lambda i: (0,) * len(shape))


def mix_a_fwd(z, w, name):
    T, W = z.shape[0], w.shape[1]
    K, P = w.shape[0], 8

    def body(ab_ref, ac_ref, ax_ref, w_ref, y_ref, pp_ref):
        pp_ref[0:P, :] = jnp.zeros((P, W), F32)

        def chunk(s):
            rows = pl.ds(s, SEQ_CHUNK)
            pp_ref[pl.ds(s + P, SEQ_CHUNK), :] = ac_ref[rows, :] * ax_ref[rows, :]
            taps = _conv_taps(pp_ref[pl.ds(s, SEQ_CHUNK + P), :], K, P)
            q = sum(w_ref[k:k + 1, :] * taps[k] for k in range(K))
            y_ref[rows, :] = (ab_ref[rows, :] * q).astype(BF16)

        _chunks(T, chunk)

    return pl.pallas_call(
        body, name=name, grid=(1,),
        in_specs=[_col(T, W, 0), _col(T, W, 1), _col(T, W, 2), _full((K, W))],
        out_specs=_full((T, W)), out_shape=jax.ShapeDtypeStruct((T, W), BF16),
        scratch_shapes=[pltpu.VMEM((T + P, W), F32)],
        compiler_params=_params("arbitrary"))(z, z, z, w)


def mix_a_bwd(z, dy, w, name):
    T, W = z.shape[0], w.shape[1]
    K, P = w.shape[0], 8

    def body(ab_ref, ac_ref, ax_ref, dy_ref, w_ref, dab_ref, dac_ref, dax_ref, dw_ref, pp_ref, dq_ref):
        pp_ref[0:P, :] = jnp.zeros((P, W), F32)
        dq_ref[T:T + P, :] = jnp.zeros((P, W), F32)
        dw_ref[...] = jnp.zeros_like(dw_ref)

        def chunk1(s):
            rows = pl.ds(s, SEQ_CHUNK)
            pp_ref[pl.ds(s + P, SEQ_CHUNK), :] = ac_ref[rows, :] * ax_ref[rows, :]
            taps = _conv_taps(pp_ref[pl.ds(s, SEQ_CHUNK + P), :], K, P)
            q = sum(w_ref[k:k + 1, :] * taps[k] for k in range(K))
            dyv = dy_ref[rows, :]
            dab_ref[rows, :] = (dyv * q).astype(BF16)
            dq = dyv * ab_ref[rows, :]
            dq_ref[rows, :] = dq
            for k in range(K):
                dw_ref[k:k + 1, :] += jnp.sum(dq * taps[k], axis=0, keepdims=True)

        _chunks(T, chunk1)

        def chunk2(s):
            rows = pl.ds(s, SEQ_CHUNK)
            taps = _conv_taps_t(dq_ref[pl.ds(s, SEQ_CHUNK + P), :], K, P)
            dp = sum(w_ref[k:k + 1, :] * taps[k] for k in range(K))
            dac_ref[rows, :] = (dp * ax_ref[rows, :]).astype(BF16)
            dax_ref[rows, :] = (dp * ac_ref[rows, :]).astype(BF16)

        _chunks(T, chunk2)

    tw = jax.ShapeDtypeStruct((T, W), BF16)
    return pl.pallas_call(
        body, name=name, grid=(1,),
        in_specs=[_col(T, W, 0), _col(T, W, 1), _col(T, W, 2), _col(T, W, 0), _full((K, W))],
        out_specs=[_full((T, W))] * 3 + [_full((K, W))],
        out_shape=[tw, tw, tw, jax.ShapeDtypeStruct((K, W), F32)],
        scratch_shapes=[pltpu.VMEM((T + P, W), F32), pltpu.VMEM((T + P, W), F32)],
        compiler_params=_params("arbitrary"))(z, z, z, dy, w)


def _ln_stats(v):
    mu = jnp.mean(v, axis=-1, keepdims=True)
    xc = v - mu
    rstd = lax.rsqrt(jnp.mean(xc * xc, axis=-1, keepdims=True) + EPS)
    return xc * rstd, rstd


def _ln_bwd(dxh, xh, rstd):
    return rstd * (dxh - jnp.mean(dxh, axis=-1, keepdims=True) - xh * jnp.mean(dxh * xh, axis=-1, keepdims=True))


def _tril_bf16(w_ref, h):
    n = w_ref.shape[-1]
    keep = lax.broadcasted_iota(jnp.int32, (n, n), 0) >= lax.broadcasted_iota(jnp.int32, (n, n), 1)
    return jnp.where(keep, w_ref[h], 0.0).astype(BF16)


def mix_b_fwd(z, g, w_s, bias, name):
    T, W = z.shape[0], g.shape[1]
    H, C = w_s.shape[0], w_s.shape[1]
    hd = W // H

    def body(u_ref, v_ref, g_ref, w_ref, b_ref, y_ref):
        wts = [_tril_bf16(w_ref, h) for h in range(H)]
        head = lax.broadcasted_iota(jnp.int32, (C, W), 1) // hd

        def chunk(s):
            rows = pl.ds(s, C)
            xh, _ = _ln_stats(v_ref[rows, :])
            vn = (xh * g_ref[...]).astype(BF16)
            mixed = b_ref[...]
            for h in range(H):
                mixed = mixed + jnp.where(head == h, jnp.dot(wts[h], vn, preferred_element_type=F32), 0.0)
            y_ref[rows, :] = (u_ref[rows, :] * mixed).astype(BF16)

        _chunks(T, chunk)

    return pl.pallas_call(
        body, name=name, grid=(1,),
        in_specs=[_col(T, W, 3), _col(T, W, 4), _full((1, W)), _full((H, C, C)), _full((C, W))],
        out_specs=_full((T, W)), out_shape=jax.ShapeDtypeStruct((T, W), BF16),
        compiler_params=_params("arbitrary"))(z, z, g, w_s, bias)


def mix_b_bwd(z, dy, g, w_s, bias, name):
    T, W = z.shape[0], g.shape[1]
    H, C = w_s.shape[0], w_s.shape[1]
    hd = W // H

    def body(u_ref, v_ref, dy_ref, g_ref, w_ref, b_ref, du_ref, dv_ref, dw_ref, db_ref, dg_ref, dbf_ref):
        wts = [_tril_bf16(w_ref, h) for h in range(H)]
        head = lax.broadcasted_iota(jnp.int32, (C, W), 1) // hd
        dw_ref[...] = jnp.zeros_like(dw_ref)
        dg_ref[...] = jnp.zeros_like(dg_ref)
        dbf_ref[...] = jnp.zeros_like(dbf_ref)

        def chunk(s):
            rows = pl.ds(s, C)
            xh, rstd = _ln_stats(v_ref[rows, :])
            vn = (xh * g_ref[...]).astype(BF16)
            mixed = b_ref[...]
            for h in range(H):
                mixed = mixed + jnp.where(head == h, jnp.dot(wts[h], vn, preferred_element_type=F32), 0.0)
            dyv = dy_ref[rows, :]
            du_ref[rows, :] = (dyv * mixed).astype(BF16)
            dm = dyv * u_ref[rows, :]
            dbf_ref[...] += dm
            dvn = jnp.zeros((C, W), F32)
            for h in range(H):
                dmh = jnp.where(head == h, dm, 0.0).astype(BF16)
                dw_ref[h] += _nt(dmh, vn)
                dvn = dvn + _tn(wts[h], dmh)
            dg_ref[...] += jnp.sum(dvn * xh, axis=0, keepdims=True)
            dv_ref[rows, :] = _ln_bwd(dvn * g_ref[...], xh, rstd).astype(BF16)

        _chunks(T, chunk)

        keep = lax.broadcasted_iota(jnp.int32, (C, C), 0) >= lax.broadcasted_iota(jnp.int32, (C, C), 1)
        lane = lax.broadcasted_iota(jnp.int32, (C, 128), 1)
        db = jnp.zeros((C, 128), F32)
        dbf = dbf_ref[...]
        for h in range(H):
            dw_ref[h] = jnp.where(keep, dw_ref[h], 0.0)
            db = db + jnp.where(lane == h, jnp.sum(jnp.where(head == h, dbf, 0.0), axis=1, keepdims=True), 0.0)
        db_ref[...] = db

    tw = jax.ShapeDtypeStruct((T, W), BF16)
    return pl.pallas_call(
        body, name=name, grid=(1,),
        in_specs=[_col(T, W, 3), _col(T, W, 4), _col(T, W, 1), _full((1, W)), _full((H, C, C)), _full((C, W))],
        out_specs=[_full((T, W)), _full((T, W)), _full((H, C, C)), _full((C, 128)), _full((1, W))],
        out_shape=[tw, tw, jax.ShapeDtypeStruct((H, C, C), F32), jax.ShapeDtypeStruct((C, 128), F32),
                   jax.ShapeDtypeStruct((1, W), F32)],
        scratch_shapes=[pltpu.VMEM((C, W), F32)],
        compiler_params=_params("arbitrary"))(z, z, dy, g, w_s, bias)


def mix_c_fwd(z, w, ln_g, ln_b, name):
    T, W = z.shape[0], w.shape[1]
    K, P = w.shape[0], 32

    def body(a_ref, gt_ref, w_ref, g_ref, b_ref, y_ref, up_ref):
        up_ref[0:P, :] = jnp.zeros((P, W), F32)

        def chunk(s):
            rows = pl.ds(s, SEQ_CHUNK)
            up_ref[pl.ds(s + P, SEQ_CHUNK), :] = a_ref[rows, :] * _sig(gt_ref[rows, :])
            taps = _conv_taps(up_ref[pl.ds(s, SEQ_CHUNK + P), :], K, P)
            q = sum(w_ref[k:k + 1, :] * taps[k] for k in range(K))
            xh, _ = _ln_stats(q)
            r = xh * g_ref[...] + b_ref[...]
            y_ref[rows, :] = (r * _sig(r)).astype(BF16)

        _chunks(T, chunk)

    return pl.pallas_call(
        body, name=name, grid=(1,),
        in_specs=[_col(T, W, 5), _col(T, W, 6), _full((K, W)), _full((1, W)), _full((1, W))],
        out_specs=_full((T, W)), out_shape=jax.ShapeDtypeStruct((T, W), BF16),
        scratch_shapes=[pltpu.VMEM((T + P, W), F32)],
        compiler_params=_params("arbitrary"))(z, z, w, ln_g, ln_b)


def mix_c_bwd(z, dy, w, ln_g, ln_b, name):
    T, W = z.shape[0], w.shape[1]
    K, P = w.shape[0], 32

    def body(a_ref, gt_ref, dy_ref, w_ref, g_ref, b_ref, da_ref, dgt_ref, dw_ref, dg_ref, db_ref, up_ref, dq_ref):
        up_ref[0:P, :] = jnp.zeros((P, W), F32)
        dq_ref[T:T + P, :] = jnp.zeros((P, W), F32)
        dw_ref[...] = jnp.zeros_like(dw_ref)
        dg_ref[...] = jnp.zeros_like(dg_ref)
        db_ref[...] = jnp.zeros_like(db_ref)

        def chunk1(s):
            rows = pl.ds(s, SEQ_CHUNK)
            up_ref[pl.ds(s + P, SEQ_CHUNK), :] = a_ref[rows, :] * _sig(gt_ref[rows, :])
            taps = _conv_taps(up_ref[pl.ds(s, SEQ_CHUNK + P), :], K, P)
            q = sum(w_ref[k:k + 1, :] * taps[k] for k in range(K))
            xh, rstd = _ln_stats(q)
            r = xh * g_ref[...] + b_ref[...]
            sr = _sig(r)
            dr = dy_ref[rows, :] * (sr * (1.0 + r * (1.0 - sr)))
            db_ref[...] += jnp.sum(dr, axis=0, keepdims=True)
            dg_ref[...] += jnp.sum(dr * xh, axis=0, keepdims=True)
            dq = _ln_bwd(dr * g_ref[...], xh, rstd)
            dq_ref[rows, :] = dq
            for k in range(K):
                dw_ref[k:k + 1, :] += jnp.sum(dq * taps[k], axis=0, keepdims=True)

        _chunks(T, chunk1)

        def chunk2(s):
            rows = pl.ds(s, SEQ_CHUNK)
            taps = _conv_taps_t(dq_ref[pl.ds(s, SEQ_CHUNK + P), :], K, P)
            du = sum(w_ref[k:k + 1, :] * taps[k] for k in range(K))
            sg = _sig(gt_ref[rows, :])
            da_ref[rows, :] = (du * sg).astype(BF16)
            dgt_ref[rows, :] = (du * a_ref[rows, :] * sg * (1.0 - sg)).astype(BF16)

        _chunks(T, chunk2)

    tw = jax.ShapeDtypeStruct((T, W), BF16)
    vec = jax.ShapeDtypeStruct((1, W), F32)
    return pl.pallas_call(
        body, name=name, grid=(1,),
        in_specs=[_col(T, W, 5), _col(T, W, 6), _col(T, W, 2), _full((K, W)), _full((1, W)), _full((1, W))],
        out_specs=[_full((T, W)), _full((T, W)), _full((K, W)), _full((1, W)), _full((1, W))],
        out_shape=[tw, tw, jax.ShapeDtypeStruct((K, W), F32), vec, vec],
        scratch_shapes=[pltpu.VMEM((T + P, W), F32), pltpu.VMEM((T + P, W), F32)],
        compiler_params=_params("arbitrary"))(z, z, dy, w, ln_g, ln_b)


def _pool_select(levels, W, rows):
    group = lax.broadcasted_iota(jnp.int32, (rows, W), 1) // (W // len(POOL_WINDOWS))
    out = levels[-1]
    for gi in range(len(POOL_WINDOWS) - 2, -1, -1):
        out = jnp.where(group == gi, levels[gi], out)
    return out


def _pool_count(s, W):
    t = s + lax.broadcasted_iota(jnp.int32, (SEQ_CHUNK, W), 0)
    group = lax.broadcasted_iota(jnp.int32, (SEQ_CHUNK, W), 1) // (W // len(POOL_WINDOWS))
    win = jnp.full((SEQ_CHUNK, W), POOL_WINDOWS[-1], jnp.int32)
    for gi in range(len(POOL_WINDOWS) - 2, -1, -1):
        win = jnp.where(group == gi, POOL_WINDOWS[gi], win)
    return jnp.minimum(t + 1, win).astype(F32)


def _pooled(wp_ref, s, W, P):
    win = wp_ref[pl.ds(s, SEQ_CHUNK + P), :]
    levels, acc, shift = [], win, 1
    for _ in POOL_WINDOWS:
        acc = acc + pltpu.roll(acc, shift, 0)
        levels.append(acc[P:, :])
        shift *= 2
    return _pool_select(levels, W, SEQ_CHUNK) / _pool_count(s, W) - win[P:, :]


def mix_d_fwd(z, pbd, scale, name):
    T, W = z.shape[0], scale.shape[1]
    P = 16

    def body(x_ref, p_ref, s_ref, y_ref, wp_ref):
        wp_ref[0:P, :] = jnp.zeros((P, W), F32)

        def chunk(s):
            rows = pl.ds(s, SEQ_CHUNK)
            wp_ref[pl.ds(s + P, SEQ_CHUNK), :] = x_ref[rows, :]
            pooled = _pooled(wp_ref, s, W, P).astype(BF16)
            y_ref[rows, :] = (jnp.dot(pooled, p_ref[...], preferred_element_type=F32) * s_ref[...]).astype(BF16)

        _chunks(T, chunk)

    return pl.pallas_call(
        body, name=name, grid=(1,),
        in_specs=[_col(T, W, 7), _full((W, W)), _full((1, W))],
        out_specs=_full((T, W)), out_shape=jax.ShapeDtypeStruct((T, W), BF16),
        scratch_shapes=[pltpu.VMEM((T + P, W), F32)],
        compiler_params=_params("arbitrary"))(z, pbd, scale)


def mix_d_bwd(z, dy, pbd, scale, name):
    T, W = z.shape[0], scale.shape[1]
    P = 16

    def body(x_ref, dy_ref, p_ref, s_ref, dx_ref, dp_ref, ds_ref, wp_ref, e_ref, dpool_ref):
        wp_ref[0:P, :] = jnp.zeros((P, W), F32)
        e_ref[T:T + P, :] = jnp.zeros((P, W), F32)
        dp_ref[...] = jnp.zeros_like(dp_ref)
        ds_ref[...] = jnp.zeros_like(ds_ref)

        def chunk1(s):
            rows = pl.ds(s, SEQ_CHUNK)
            wp_ref[pl.ds(s + P, SEQ_CHUNK), :] = x_ref[rows, :]
            pooled = _pooled(wp_ref, s, W, P).astype(BF16)
            yl = jnp.dot(pooled, p_ref[...], preferred_element_type=F32)
            dyv = dy_ref[rows, :]
            ds_ref[...] += jnp.sum(dyv * yl, axis=0, keepdims=True)
            dyl = (dyv * s_ref[...]).astype(BF16)
            dp_ref[...] += _tn(pooled, dyl)
            dpool = _nt(dyl, p_ref[...])
            dpool_ref[rows, :] = dpool
            e_ref[rows, :] = dpool / _pool_count(s, W)

        _chunks(T, chunk1)

        def chunk2(s):
            rows = pl.ds(s, SEQ_CHUNK)
            win = e_ref[pl.ds(s, SEQ_CHUNK + P), :]
            n = SEQ_CHUNK + P
            levels, acc, shift = [], win, 1
            for _ in POOL_WINDOWS:
                acc = acc + pltpu.roll(acc, n - shift, 0)
                levels.append(acc[:SEQ_CHUNK, :])
                shift *= 2
            dx_ref[rows, :] = (_pool_select(levels, W, SEQ_CHUNK) - dpool_ref[rows, :]).astype(BF16)

        _chunks(T, chunk2)

    return pl.pallas_call(
        body, name=name, grid=(1,),
        in_specs=[_col(T, W, 7), _col(T, W, 3), _full((W, W)), _full((1, W))],
        out_specs=[_full((T, W)), _full((W, W)), _full((1, W))],
        out_shape=[jax.ShapeDtypeStruct((T, W), BF16), jax.ShapeDtypeStruct((W, W), F32),
                   jax.ShapeDtypeStruct((1, W), F32)],
        scratch_shapes=[pltpu.VMEM((T + P, W), F32), pltpu.VMEM((T + P, W), F32), pltpu.VMEM((T, W), F32)],
        compiler_params=_params("arbitrary"))(z, dy, pbd, scale)


def attn_fwd(q, kv, name):
    T, D = q.shape
    M = kv.shape[0]
    hd = D // N_HEADS
    tm = _pick(T, 512, 8)
    sc = 1.0 / math.sqrt(hd)

    def body(q_ref, k_ref, v_ref, o_ref):
        for h in range(N_HEADS):
            cols = slice(h * hd, (h + 1) * hd)
            s = _nt(q_ref[:, cols].astype(BF16), k_ref[:, cols].astype(BF16)) * sc
            p = jnp.exp(s - jnp.max(s, axis=-1, keepdims=True))
            p = p / jnp.sum(p, axis=-1, keepdims=True)
            o_ref[:, cols] = jnp.dot(p.astype(BF16), v_ref[:, cols].astype(BF16),
                                     preferred_element_type=F32).astype(BF16)

    return pl.pallas_call(
        body, name=name, grid=(T // tm,),
        in_specs=[pl.BlockSpec((tm, D), lambda i: (i, 0)), pl.BlockSpec((M, D), lambda i: (0, 0)),
                  pl.BlockSpec((M, D), lambda i: (0, 1))],
        out_specs=pl.BlockSpec((tm, D), lambda i: (i, 0)),
        out_shape=jax.ShapeDtypeStruct((T, D), BF16),
        compiler_params=_params("parallel"))(q, kv, kv)


def attn_bwd(q, kv, do, name):
    T, D = q.shape
    M = kv.shape[0]
    hd = D // N_HEADS
    tm = _pick(T, 512, 8)
    sc = 1.0 / math.sqrt(hd)

    def body(q_ref, k_ref, v_ref, do_ref, dq_ref, dk_ref, dv_ref):
        i = pl.program_id(0)

        @pl.when(i == 0)
        def _():
            dk_ref[...] = jnp.zeros_like(dk_ref)
            dv_ref[...] = jnp.zeros_like(dv_ref)

        for h in range(N_HEADS):
            cols = slice(h * hd, (h + 1) * hd)
            qh, kh = q_ref[:, cols].astype(BF16), k_ref[:, cols].astype(BF16)
            vh, doh = v_ref[:, cols].astype(BF16), do_ref[:, cols].astype(BF16)
            s = _nt(qh, kh) * sc
            p = jnp.exp(s - jnp.max(s, axis=-1, keepdims=True))
            p = p / jnp.sum(p, axis=-1, keepdims=True)
            dp = _nt(doh, vh)
            dv_ref[:, cols] += _tn(p.astype(BF16), doh)
            ds = (p * (dp - jnp.sum(dp * p, axis=-1, keepdims=True)) * sc).astype(BF16)
            dq_ref[:, cols] = jnp.dot(ds, kh, preferred_element_type=F32).astype(BF16)
            dk_ref[:, cols] += _tn(ds, qh)

    tile = pl.BlockSpec((tm, D), lambda i: (i, 0))
    mem = jax.ShapeDtypeStruct((M, D), F32)
    return pl.pallas_call(
        body, name=name, grid=(T // tm,),
        in_specs=[tile, pl.BlockSpec((M, D), lambda i: (0, 0)), pl.BlockSpec((M, D), lambda i: (0, 1)), tile],
        out_specs=[tile, pl.BlockSpec((M, D), lambda i: (0, 0)), pl.BlockSpec((M, D), lambda i: (0, 0))],
        out_shape=[jax.ShapeDtypeStruct((T, D), BF16), mem, mem],
        compiler_params=_params("arbitrary"))(q, kv, kv, do)


def loss_head(x, g, target, name):
    T, D = x.shape
    tm = _pick(T, 512, 8)

    def body(x_ref, g_ref, t_ref, l_ref, dx_ref, dg_ref):
        i = pl.program_id(0)

        @pl.when(i == 0)
        def _():
            l_ref[...] = jnp.zeros_like(l_ref)
            dg_ref[...] = jnp.zeros_like(dg_ref)

        xv = x_ref[...]
        r = lax.rsqrt(jnp.mean(xv * xv, axis=-1, keepdims=True) + EPS)
        xh = xv * r
        err = xh * g_ref[...] - t_ref[...]
        l_ref[...] += 0.5 * jnp.sum(jnp.mean(err * err, axis=-1, keepdims=True), axis=0, keepdims=True)
        dy = err * (1.0 / D)
        dg_ref[...] += jnp.sum(dy * xh, axis=0, keepdims=True)
        dxh = dy * g_ref[...]
        dx_ref[...] = r * (dxh - xh * jnp.mean(dxh * xh, axis=-1, keepdims=True))

    tile = pl.BlockSpec((tm, D), lambda i: (i, 0))
    vec = pl.BlockSpec((1, D), lambda i: (0, 0))
    return pl.pallas_call(
        body, name=name, grid=(T // tm,),
        in_specs=[tile, vec, tile],
        out_specs=[pl.BlockSpec((1, 128), lambda i: (0, 0)), tile, vec],
        out_shape=[jax.ShapeDtypeStruct((1, 128), F32), jax.ShapeDtypeStruct((T, D), F32),
                   jax.ShapeDtypeStruct((1, D), F32)],
        compiler_params=_params("arbitrary"))(x, g, target)


def _block_diag(p):
    G, gd, _ = p.shape
    rows = [jnp.concatenate([p[g] if g == c else jnp.zeros((gd, gd), p.dtype) for c in range(G)], axis=1)
            for g in range(G)]
    return jnp.concatenate(rows, axis=0)


class _LazyWeight:
    def __init__(self, fetch, name, latest):
        self.fetch, self.name, self.latest = fetch, name, latest

    def __getitem__(self, l):
        return self.fetch(self.name, l, self.latest[0])


def _local_step(x, mem, target, fetch, ws, L, progress=lambda event, l, grads, values: values):
    T, D = x.shape
    W = D // 4
    H = ws["sgu_w"].shape[1]
    row = lambda v: v.reshape(1, -1)
    latest = [x]
    wb = {n: _LazyWeight(fetch, n, latest) for n in BIG}
    saved = []
    for l in range(L):
        s = {"x0": x}
        latest[0] = x
        x, *s["ffn1"] = ffn_fwd(x, row(ws["norm_ffn1"][l]), wb["ffn1_w_in"][l], wb["ffn1_w_out"][l], "ffn_fwd")
        s["x1"] = x
        latest[0] = x
        z, s["h_mix"] = norm_matmul(x, row(ws["norm_mix"][l]), wb["mix_w_in"][l], "mix_in")
        s["z"] = z
        s["bias"] = jnp.repeat(ws["sgu_b"][l].T, W // H, axis=1)
        s["pbd"] = _block_diag(ws["pool_w"][l]).astype(BF16)
        y = jnp.concatenate([
            mix_a_fwd(z, ws["sconv_w"][l], "mix_a_fwd"),
            mix_b_fwd(z, row(ws["sgu_norm_g"][l]), ws["sgu_w"][l], s["bias"], "mix_b_fwd"),
            mix_c_fwd(z, ws["cconv_w"][l], row(ws["cconv_ln_g"][l]), row(ws["cconv_ln_b"][l]), "mix_c_fwd"),
            mix_d_fwd(z, s["pbd"], row(ws["pool_scale"][l]), "mix_d_fwd")], axis=1)
        s["y"] = y
        x = matmul_res(x, y, wb["mix_w_out"][l], "mix_out")
        s["x2"] = x
        s["q"], s["hq"] = norm_matmul(x, row(ws["norm_xattn"][l]), wb["xattn_wq"][l], "attn_q", out_dtype=BF16)
        s["kv"], s["mn"] = norm_matmul(mem, row(ws["norm_mem"][l]), wb["xattn_wkv"][l], "attn_kv")
        s["o"] = attn_fwd(s["q"], s["kv"], "attn_fwd")
        x = matmul_res(x, s["o"], wb["xattn_wo"][l], "attn_out")
        s["x3"] = x
        x, *s["ffn2"] = ffn_fwd(x, row(ws["norm_ffn2"][l]), wb["ffn2_w_in"][l], wb["ffn2_w_out"][l], "ffn_fwd")
        saved.append(s)

    loss, dx, dg_final = loss_head(x, row(ws["norm_final"]), target, "loss_head")
    grads = {n: [None] * L for n in WEIGHTS if n != "norm_final"}
    grads["norm_final"] = dg_final.reshape(-1)

    def pin(dx, names, l):
        dx, made = lax.optimization_barrier((dx, [grads[n][l] for n in names]))
        for n, g in zip(names, made):
            grads[n][l] = g
        return dx

    def after_stages(event, l, values):
        return progress(event, l, grads, values)

    def ffn_back(xin, kept, dxo, gname, win, wout, l, event):
        h, zg, zu = kept
        a, dzg, dzu = ffn_dz(dxo, zg, zu, wb[wout][l], "ffn_dz")
        last = l == 0 and event == "ffn1_mid"
        if not last:
            dxn, dg = dh_norm_bwd(xin, dxo, row(ws[gname][l]), [dzg, dzu], wb[win][l], "ffn_dh")
            dxn, h, a = after_stages(event, l, (dxn, h, a))
        else:
            h, a = after_stages(event, l, (h, a))
        grads[win][l] = matmul_tn(h, dzg, 1.0, "ffn_dwin", b2=dzu)
        grads[wout][l] = matmul_tn(a, dxo, 0.5, "ffn_dwout")
        if last:
            dzg, dzu = after_stages("ffn1_grads", l, (dzg, dzu))
            dxn, dg = dh_norm_bwd(xin, dxo, row(ws[gname][l]), [dzg, dzu], wb[win][l], "ffn_dh")
        grads[gname][l] = dg.reshape(-1)
        return dxn if last else pin(dxn, (win, wout), l)

    for l in reversed(range(L)):
        s = saved[l]
        dx = ffn_back(s["x3"], s["ffn2"], dx, "norm_ffn2", "ffn2_w_in", "ffn2_w_out", l, "ffn2_mid")
        dx, = after_stages("ffn2", l, (dx,))
        grads["xattn_wo"][l] = matmul_tn(s["o"], dx, 1.0, "dw_sq")
        do = matmul_nt(dx, wb["xattn_wo"][l], "attn_do", out_dtype=BF16)
        dq, dk, dv = attn_bwd(s["q"], s["kv"], do, "attn_bwd")
        grads["xattn_wq"][l] = matmul_tn(s["hq"], dq, 1.0, "dw_sq")
        dx, dg = dh_norm_bwd(s["x2"], dx, row(ws["norm_xattn"][l]), [dq], wb["xattn_wq"][l], "attn_dh")
        grads["norm_xattn"][l] = dg.reshape(-1)
        dkv = jnp.concatenate([dk, dv], axis=1)
        grads["xattn_wkv"][l] = matmul_tn(s["mn"], dkv, 1.0, "attn_dwkv")
        dmn = matmul_nt(dkv, wb["xattn_wkv"][l], "attn_dmn")
        _, dg = rmsnorm_bwd(None, dmn, mem, row(ws["norm_mem"][l]), "norm_mem_bwd")
        grads["norm_mem"][l] = dg.reshape(-1)
        dx = pin(dx, ("xattn_wo", "xattn_wq", "xattn_wkv", "norm_mem"), l)
        dx, = after_stages("attn", l, (dx,))
        grads["mix_w_out"][l] = matmul_tn(s["y"], dx, 1.0, "dw_sq")
        dy = matmul_nt(dx, wb["mix_w_out"][l], "mix_dy")
        z = s["z"]
        dab, dac, dax, dws = mix_a_bwd(z, dy, ws["sconv_w"][l], "mix_a_bwd")
        dbu, dbv, dwsgu, dbs, dgs = mix_b_bwd(z, dy, row(ws["sgu_norm_g"][l]), ws["sgu_w"][l], s["bias"], "mix_b_bwd")
        dca, dcg, dwc, dgc, dbc = mix_c_bwd(z, dy, ws["cconv_w"][l], row(ws["cconv_ln_g"][l]),
                                            row(ws["cconv_ln_b"][l]), "mix_c_bwd")
        ddw, dpbd, dsc = mix_d_bwd(z, dy, s["pbd"], row(ws["pool_scale"][l]), "mix_d_bwd")
        grads["sconv_w"][l], grads["cconv_w"][l] = dws, dwc
        grads["sgu_w"][l], grads["sgu_b"][l], grads["sgu_norm_g"][l] = dwsgu, dbs[:, :H].T, dgs.reshape(-1)
        grads["cconv_ln_g"][l], grads["cconv_ln_b"][l] = dgc.reshape(-1), dbc.reshape(-1)
        gd = W // len(POOL_WINDOWS)
        grads["pool_w"][l] = jnp.stack([dpbd[g * gd:(g + 1) * gd, g * gd:(g + 1) * gd] for g in range(len(POOL_WINDOWS))])
        grads["pool_scale"][l] = dsc.reshape(-1)
        dz = jnp.concatenate([dab, dac, dax, dbu, dbv, dca, dcg, ddw], axis=1)
        grads["mix_w_in"][l] = matmul_tn(s["h_mix"], dz, 1.0, "mix_dwin")
        dx, dg = dh_norm_bwd(s["x1"], dx, row(ws["norm_mix"][l]), [dz], wb["mix_w_in"][l], "mix_dh")
        grads["norm_mix"][l] = dg.reshape(-1)
        dx = pin(dx, ("mix_w_out", "mix_w_in"), l)
        dx, = after_stages("mix", l, (dx,))
        dx = ffn_back(s["x0"], s["ffn1"], dx, "norm_ffn1", "ffn1_w_in", "ffn1_w_out", l, "ffn1_mid")
        dx, = after_stages("layer", l, (dx,))

    return loss[0, 0], dx, grads


ANY = pl.BlockSpec(memory_space=pl.ANY)


def _other_chips(x, y):
    return [(1 - x, y), (x, 1 - y), (1 - x, 1 - y)]


def _shard_slice(ref, axis, chip, size):
    idx = [slice(None)] * len(ref.shape)
    idx[axis] = pl.ds(pl.multiple_of(chip * size, size), size)
    return ref.at[tuple(idx)]


def all_gather_chips(shards, axes, name):
    n = len(shards)

    def body(*refs):
        ins, outs = refs[:n], refs[n:2 * n]
        send, recv, loc = refs[2 * n:]
        x, y, c = lax.axis_index("x"), lax.axis_index("y"), lax.axis_index("c")
        me = 2 * x + y
        chips = _other_chips(x, y)
        started = []
        for i in range(n):
            size = ins[i].shape[axes[i]]
            cp = pltpu.make_async_copy(ins[i], _shard_slice(outs[i], axes[i], me, size), loc.at[i])
            cp.start()
            started.append(cp)
        sends = []
        for i in range(n):
            size = ins[i].shape[axes[i]]
            for j, (px, py) in enumerate(chips):
                cp = pltpu.make_async_remote_copy(
                    src_ref=ins[i], dst_ref=_shard_slice(outs[i], axes[i], me, size),
                    send_sem=send.at[i, j], recv_sem=recv.at[i, j], device_id=(px, py, c), device_id_type=MESH_ID)
                cp.start()
                sends.append(cp)
        for i in range(n):
            size = ins[i].shape[axes[i]]
            for j, (px, py) in enumerate(chips):
                pltpu.make_async_remote_copy(
                    src_ref=ins[i], dst_ref=_shard_slice(outs[i], axes[i], 2 * px + py, size),
                    send_sem=send.at[i, j], recv_sem=recv.at[i, j], device_id=(px, py, c),
                    device_id_type=MESH_ID).wait_recv()
        for cp in sends:
            cp.wait_send()
        for cp in started:
            cp.wait()

    def full(a, ax):
        shape = list(a.shape)
        shape[ax] *= N_CHIPS
        return jax.ShapeDtypeStruct(tuple(shape), a.dtype)

    return pl.pallas_call(
        body, name=name, in_specs=[ANY] * n, out_specs=[ANY] * n,
        out_shape=[full(a, ax) for a, ax in zip(shards, axes)],
        scratch_shapes=[pltpu.SemaphoreType.DMA((n, 3)), pltpu.SemaphoreType.DMA((n, 3)),
                        pltpu.SemaphoreType.DMA((n,))],
        compiler_params=pltpu.CompilerParams(has_side_effects=True))(*shards)


def cast_into_slot(shard, axis, chip, name):
    L, K, N = shard.shape
    bm = _pick(K, 256, 16)
    full = (K * N_CHIPS, N) if axis == 1 else (K, N * N_CHIPS)
    nb = K // bm

    def body(c_ref, s_ref, *o_refs):
        for l in range(L):
            o_refs[l][...] = s_ref[l].astype(BF16)

    out_map = (lambda i, c: (c[0] * nb + i, 0)) if axis == 1 else (lambda i, c: (i, c[0]))
    spec = pltpu.PrefetchScalarGridSpec(
        num_scalar_prefetch=1, grid=(nb,),
        in_specs=[pl.BlockSpec((L, bm, N), lambda i, c: (0, i, 0))],
        out_specs=[pl.BlockSpec((bm, N), out_map)] * L)
    return pl.pallas_call(body, name=name, grid_spec=spec, out_shape=[jax.ShapeDtypeStruct(full, BF16)] * L,
                          compiler_params=_params("parallel"))(chip, shard)


HBM = pl.BlockSpec(memory_space=pltpu.HBM)
SEM = pl.BlockSpec(memory_space=pltpu.SEMAPHORE)
DATAFLOW = pltpu.SideEffectType.DATAFLOW_SIDE_EFFECTING


def split_start(name, sources, landing, n_sems, make, after):
    ns, nl, na = len(sources), len(landing), len(after)

    def body(*refs):
        out, _ = make(refs[:ns], refs[ns:ns + nl], refs[ns + nl + na], refs[ns + nl + na + 1])
        for cp in out:
            cp.start()
        refs[-1][...] = jnp.zeros_like(refs[-1])

    hbm = lambda b: pltpu.with_memory_space_constraint(b, pltpu.HBM)
    res = pl.pallas_call(
        body, name=name,
        out_shape=(pltpu.SemaphoreType.DMA((n_sems,)), pltpu.SemaphoreType.DMA((n_sems,)),
                   *[pltpu.HBM(b.shape, b.dtype) for b in landing], jax.ShapeDtypeStruct((8, 128), F32)),
        in_specs=[HBM] * (ns + nl) + [ANY] * na, out_specs=(SEM, SEM, *[HBM] * nl, pl.BlockSpec(memory_space=pltpu.VMEM)),
        input_output_aliases={ns + i: 2 + i for i in range(nl)},
        compiler_params=pltpu.CompilerParams(has_side_effects=DATAFLOW))(
            *[hbm(b) for b in sources], *[hbm(b) for b in landing], *after)
    return res[0], res[1], list(res[2:2 + nl]), res[-1]


def split_wait(name, sources, landing, send, recv, make, after):
    ns, nl = len(sources), len(landing)

    def body(*refs):
        _, back = make(refs[:ns], refs[ns:ns + nl], refs[ns + nl], refs[ns + nl + 1])
        for cp in back:
            cp.wait_send()
            cp.wait_recv()

    return list(pl.pallas_call(
        body, name=name, out_shape=tuple(pltpu.HBM(b.shape, b.dtype) for b in landing),
        in_specs=[HBM] * (ns + nl) + [SEM, SEM] + [ANY] * len(after), out_specs=tuple([HBM] * nl),
        input_output_aliases={ns + i: i for i in range(nl)},
        compiler_params=pltpu.CompilerParams(has_side_effects=DATAFLOW))(
            *[pltpu.with_memory_space_constraint(b, pltpu.HBM) for b in sources], *landing, send, recv, *after))


def _half_slot(buf, axis, chip, half):
    K, N = buf.shape
    if axis == 1:
        n = N // N_CHIPS
        return buf.at[pl.ds(pl.multiple_of(half * (K // 2), K // 2), K // 2), pl.ds(pl.multiple_of(chip * n, n), n)]
    k2 = K // N_CHIPS // 2
    return buf.at[pl.ds(pl.multiple_of((2 * chip + half) * k2, k2), k2), :]


def gather_copies(axes):
    def make(_, bufs, send, recv):
        x, y, c = lax.axis_index("x"), lax.axis_index("y"), lax.axis_index("c")
        out, back = [], []
        for i, (buf, ax) in enumerate(zip(bufs, axes)):
            mine = _half_slot(buf, ax, 2 * x + y, c)
            for j, (px, py) in enumerate(_other_chips(x, y)):
                kw = dict(send_sem=send.at[3 * i + j], recv_sem=recv.at[3 * i + j], device_id=(px, py, c),
                          device_id_type=MESH_ID)
                out.append(pltpu.make_async_remote_copy(src_ref=mine, dst_ref=mine, **kw))
                back.append(pltpu.make_async_remote_copy(src_ref=mine, dst_ref=_half_slot(buf, ax, 2 * px + py, c), **kw))
        return out, back
    return make


def forward_sibling(bufs, axes, name):
    n = len(bufs)

    def body(*refs):
        ins = refs[:n]
        send, recv = refs[2 * n:]
        x, y, c = lax.axis_index("x"), lax.axis_index("y"), lax.axis_index("c")
        out, back = [], []
        for i, ax in enumerate(axes):
            for j, (px, py) in enumerate(_other_chips(x, y)):
                have = _half_slot(ins[i], ax, 2 * px + py, c)
                kw = dict(send_sem=send.at[3 * i + j], recv_sem=recv.at[3 * i + j], device_id=(x, y, 1 - c),
                          device_id_type=MESH_ID)
                out.append(pltpu.make_async_remote_copy(src_ref=have, dst_ref=have, **kw))
                back.append(pltpu.make_async_remote_copy(src_ref=have, dst_ref=_half_slot(ins[i], ax, 2 * px + py, 1 - c), **kw))
        for cp in out:
            cp.start()
        for cp in back:
            cp.wait_recv()
        for cp in out:
            cp.wait_send()

    return pl.pallas_call(
        body, name=name, in_specs=[ANY] * n, out_specs=[ANY] * n,
        out_shape=[jax.ShapeDtypeStruct(b.shape, b.dtype) for b in bufs],
        input_output_aliases={i: i for i in range(n)},
        scratch_shapes=[pltpu.SemaphoreType.DMA((3 * n,)), pltpu.SemaphoreType.DMA((3 * n,))],
        compiler_params=pltpu.CompilerParams(has_side_effects=True))(*bufs)


def all_reduce_small(p, name):
    R = p.shape[0]

    def body(p_ref, o_ref, sib_ref, chip_ref, send, recv):
        x, y, c = lax.axis_index("x"), lax.axis_index("y"), lax.axis_index("c")
        me = 2 * x + y
        chips = _other_chips(x, y)
        pair = pltpu.make_async_remote_copy(src_ref=p_ref, dst_ref=sib_ref, send_sem=send.at[0], recv_sem=recv.at[0],
                                            device_id=(x, y, 1 - c), device_id_type=MESH_ID)
        pair.start()
        pair.wait()
        chip_ref[me] = p_ref[...] + sib_ref[...]
        sends = []
        for j, (px, py) in enumerate(chips):
            cp = pltpu.make_async_remote_copy(src_ref=chip_ref.at[me], dst_ref=chip_ref.at[me], send_sem=send.at[1 + j],
                                              recv_sem=recv.at[1 + j], device_id=(px, py, c), device_id_type=MESH_ID)
            cp.start()
            sends.append(cp)
        for j, (px, py) in enumerate(chips):
            pltpu.make_async_remote_copy(src_ref=chip_ref.at[me], dst_ref=chip_ref.at[2 * px + py], send_sem=send.at[1 + j],
                                         recv_sem=recv.at[1 + j], device_id=(px, py, c), device_id_type=MESH_ID).wait_recv()
        for cp in sends:
            cp.wait_send()
        o_ref[...] = ((chip_ref[0] + chip_ref[1]) + chip_ref[2]) + chip_ref[3]

    vm = pl.BlockSpec(memory_space=pltpu.VMEM)
    return pl.pallas_call(
        body, name=name, in_specs=[vm], out_specs=vm, out_shape=jax.ShapeDtypeStruct((R, 128), F32),
        scratch_shapes=[pltpu.VMEM((R, 128), F32), pltpu.VMEM((N_CHIPS, R, 128), F32),
                        pltpu.SemaphoreType.DMA((4,)), pltpu.SemaphoreType.DMA((4,))],
        compiler_params=pltpu.CompilerParams(has_side_effects=True, vmem_limit_bytes=VMEM_LIMIT))(p)


def _grad_view(g, axis):
    K, N = g.shape
    return g.reshape(1, 2, K // 2, N) if axis == 1 else g.reshape(N_CHIPS, 2, K // N_CHIPS // 2, N)


def pair_copies(gvs, others, send, recv):
    x, y, c = lax.axis_index("x"), lax.axis_index("y"), lax.axis_index("c")
    out = [pltpu.make_async_remote_copy(src_ref=gv.at[:, 1 - c], dst_ref=o, send_sem=send.at[i], recv_sem=recv.at[i],
                                        device_id=(x, y, 1 - c), device_id_type=MESH_ID)
           for i, (gv, o) in enumerate(zip(gvs, others))]
    return out, out


def chip_copies(axes):
    def piece(s, ax, chip):
        if ax == 1:
            n = s.shape[2] // N_CHIPS
            return s.at[0, :, pl.ds(pl.multiple_of(chip * n, n), n)]
        return s.at[chip]

    def make(sums, qs, send, recv):
        x, y, c = lax.axis_index("x"), lax.axis_index("y"), lax.axis_index("c")
        out = []
        for i, (s, q, ax) in enumerate(zip(sums, qs, axes)):
            for j, (px, py) in enumerate(_other_chips(x, y)):
                out.append(pltpu.make_async_remote_copy(
                    src_ref=piece(s, ax, 2 * px + py), dst_ref=q.at[j], send_sem=send.at[3 * i + j],
                    recv_sem=recv.at[3 * i + j], device_id=(px, py, c), device_id_type=MESH_ID))
        return out, out
    return make


def share_sibling(halves, name):
    n = len(halves)

    def body(*refs):
        ins = refs[:n]
        send, recv = refs[2 * n:]
        x, y, c = lax.axis_index("x"), lax.axis_index("y"), lax.axis_index("c")
        cps = [pltpu.make_async_remote_copy(src_ref=ins[i].at[c], dst_ref=ins[i].at[c], send_sem=send.at[i], recv_sem=recv.at[i],
                                            device_id=(x, y, 1 - c), device_id_type=MESH_ID) for i in range(n)]
        for cp in cps:
            cp.start()
        for i in range(n):
            pltpu.make_async_remote_copy(src_ref=ins[i].at[c], dst_ref=ins[i].at[1 - c], send_sem=send.at[i], recv_sem=recv.at[i],
                                         device_id=(x, y, 1 - c), device_id_type=MESH_ID).wait_recv()
        for cp in cps:
            cp.wait_send()

    return pl.pallas_call(
        body, name=name, in_specs=[ANY] * n, out_specs=[ANY] * n,
        out_shape=[jax.ShapeDtypeStruct(a.shape, a.dtype) for a in halves],
        input_output_aliases={i: i for i in range(n)},
        scratch_shapes=[pltpu.SemaphoreType.DMA((n,)), pltpu.SemaphoreType.DMA((n,))],
        compiler_params=pltpu.CompilerParams(has_side_effects=True))(*halves)


def add_pair(gv, other, place, name):
    A, _, rows, N = gv.shape
    bm, bn = _pick(rows, 256, 16), _pick(N, 1408, 128)

    def body(p_ref, g_ref, o_ref, out_ref):
        out_ref[...] = (g_ref[...] + o_ref[...]).astype(GRAD_WIRE)

    spec = pltpu.PrefetchScalarGridSpec(
        num_scalar_prefetch=1, grid=(A, rows // bm, N // bn),
        in_specs=[pl.BlockSpec((None, None, bm, bn), lambda a, i, j, p: (a, p[1], i, j)),
                  pl.BlockSpec((None, bm, bn), lambda a, i, j, p: (a, i, j))],
        out_specs=pl.BlockSpec((None, bm, bn), lambda a, i, j, p: (a, i, j)))
    return pl.pallas_call(body, name=name, grid_spec=spec, out_shape=jax.ShapeDtypeStruct((A, rows, N), GRAD_WIRE),
                          compiler_params=_params("parallel", "parallel", "parallel"))(place, gv, other)


def add_chips(s, q, axis, place, name):
    _, rows, n = q.shape
    bm, bn = _pick(rows, 256, 16), _pick(n, 1408, 128)
    nbj = n // bn

    def body(p_ref, s_ref, q_ref, o_ref):
        o_ref[...] = ((s_ref[...].astype(F32) + q_ref[0].astype(F32)) + q_ref[1].astype(F32)) + q_ref[2].astype(F32)

    mine = (lambda i, j, p: (p[0], i, j)) if axis == 0 else (lambda i, j, p: (0, i, p[0] * nbj + j))
    spec = pltpu.PrefetchScalarGridSpec(
        num_scalar_prefetch=1, grid=(rows // bm, nbj),
        in_specs=[pl.BlockSpec((None, bm, bn), mine), pl.BlockSpec((3, bm, bn), lambda i, j, p: (0, i, j))],
        out_specs=pl.BlockSpec((None, bm, bn), lambda i, j, p: (p[1], i, j)))
    return pl.pallas_call(body, name=name, grid_spec=spec, out_shape=jax.ShapeDtypeStruct((2, rows, n), F32),
                          compiler_params=_params("parallel", "parallel"))(place, s, q)


def adamw(w, g, m, v, name):
    R, N = w.shape
    bm = _pick(R, 256, 8)
    c1 = 1.0 / (1.0 - ADAM_B1 ** ADAM_STEP)
    c2 = 1.0 / (1.0 - ADAM_B2 ** ADAM_STEP)

    def body(w_ref, g_ref, m_ref, v_ref, d_ref, nm_ref, nv_ref):
        gv = g_ref[...]
        nm = ADAM_B1 * m_ref[...] + (1.0 - ADAM_B1) * gv
        nv = ADAM_B2 * v_ref[...] + (1.0 - ADAM_B2) * (gv * gv)
        nm_ref[...] = nm
        nv_ref[...] = nv
        d_ref[...] = -ADAM_LR * ((nm * c1) / (jnp.sqrt(nv * c2) + ADAM_EPS) + ADAM_WD * w_ref[...])

    blk = pl.BlockSpec((bm, N), lambda i: (i, 0))
    out = jax.ShapeDtypeStruct((R, N), F32)
    return pl.pallas_call(body, name=name, grid=(R // bm,), in_specs=[blk] * 4, out_specs=[blk] * 3,
                          out_shape=[out, out, out], compiler_params=_params("parallel"))(w, g, m, v)


def adamw_layers(w, g0, g1, m, v, name):
    _, k, n = w.shape
    bm = _pick(k, 256, 8)
    c1 = 1.0 / (1.0 - ADAM_B1 ** ADAM_STEP)
    c2 = 1.0 / (1.0 - ADAM_B2 ** ADAM_STEP)

    def body(w_ref, g0_ref, g1_ref, m_ref, v_ref, g_ref, d_ref, nm_ref, nv_ref):
        def step(gv):
            nm = ADAM_B1 * m_ref[...] + (1.0 - ADAM_B1) * gv
            nv = ADAM_B2 * v_ref[...] + (1.0 - ADAM_B2) * (gv * gv)
            g_ref[...] = gv
            nm_ref[...] = nm
            nv_ref[...] = nv
            d_ref[...] = -ADAM_LR * ((nm * c1) / (jnp.sqrt(nv * c2) + ADAM_EPS) + ADAM_WD * w_ref[...])

        @pl.when(pl.program_id(0) == 0)
        def _():
            step(g0_ref[...])

        @pl.when(pl.program_id(0) == 1)
        def _():
            step(g1_ref[...])

    blk = pl.BlockSpec((None, bm, n), lambda l, i: (l, i, 0))
    out = jax.ShapeDtypeStruct(w.shape, F32)
    return pl.pallas_call(
        body, name=name, grid=(2, k // bm),
        in_specs=[blk, pl.BlockSpec((bm, n), lambda l, i: (i * (1 - l), 0)), pl.BlockSpec((bm, n), lambda l, i: (i * l, 0)),
                  blk, blk],
        out_specs=[blk] * 4, out_shape=[out] * 4, compiler_params=_params("arbitrary", "arbitrary"))(w, g0, g1, m, v)


def _pack(arrays):
    flat = jnp.concatenate([a.reshape(-1) for a in arrays])
    rows = -(-flat.shape[0] // (256 * 128)) * 256
    return jnp.pad(flat, (0, rows * 128 - flat.shape[0])).reshape(rows, 128)


def _unpack(p, shapes):
    flat, out, at = p.reshape(-1), [], 0
    for s in shapes:
        n = math.prod(s)
        out.append(flat[at:at + n].reshape(s))
        at += n
    return out


def kernel(x, mem, norm_ffn1, ffn1_w_in, ffn1_w_out, norm_mix, mix_w_in, sconv_w, sgu_norm_g, sgu_w, sgu_b, cconv_w, cconv_ln_g, cconv_ln_b, pool_w, pool_scale, mix_w_out, norm_xattn, norm_mem, xattn_wq, xattn_wkv, xattn_wo, norm_ffn2, ffn2_w_in, ffn2_w_out, norm_final, loss_target, m_norm_ffn1, m_ffn1_w_in, m_ffn1_w_out, m_norm_mix, m_mix_w_in, m_sconv_w, m_sgu_norm_g, m_sgu_w, m_sgu_b, m_cconv_w, m_cconv_ln_g, m_cconv_ln_b, m_pool_w, m_pool_scale, m_mix_w_out, m_norm_xattn, m_norm_mem, m_xattn_wq, m_xattn_wkv, m_xattn_wo, m_norm_ffn2, m_ffn2_w_in, m_ffn2_w_out, m_norm_final, v_norm_ffn1, v_ffn1_w_in, v_ffn1_w_out, v_norm_mix, v_mix_w_in, v_sconv_w, v_sgu_norm_g, v_sgu_w, v_sgu_b, v_cconv_w, v_cconv_ln_g, v_cconv_ln_b, v_pool_w, v_pool_scale, v_mix_w_out, v_norm_xattn, v_norm_mem, v_xattn_wq, v_xattn_wkv, v_xattn_wo, v_norm_ffn2, v_ffn2_w_in, v_ffn2_w_out, v_norm_final):
    given = dict(locals())
    w = {n: given[n] for n in WEIGHTS}
    L = ffn1_w_in.shape[0]
    assert L == 2, "the reduce-scatter gives one layer to each core of a chip"
    chip = 2 * lax.axis_index("x") + lax.axis_index("y")
    chip1 = chip.astype(jnp.int32).reshape(1)
    core = lax.axis_index("c").astype(jnp.int32).reshape(1)
    place = jnp.concatenate([chip1, core])

    axis = {n: 1 if n in COL_SHARDED else 0 for n in BIG}
    bufs = {}
    for n in BIG:
        for l, b in enumerate(cast_into_slot(w[n], axis[n] + 1, chip1, "cast_weights")):
            bufs[n, l] = b
    groups = {"a": [(n, 0) for n in BIG[:2]], "b": [(n, 0) for n in BIG[2:]], "c": [(n, 1) for n in BIG]}
    wc = sconv_w.shape[-1]
    conv_rows = [w[n].reshape(-1, wc) for n in SMALL_CONV]
    n_conv = sum(r.shape[0] for r in conv_rows)
    conv_pack = jnp.pad(jnp.concatenate(conv_rows, axis=0), ((0, -n_conv % 8), (0, 128 - wc)))[None]
    conv_all = all_gather_chips([conv_pack], [0], "gather_conv")[0]
    started, token = {}, conv_all
    for g, keys in groups.items():
        send, recv, thru, token = split_start("gather_start_" + g, [], [bufs[k] for k in keys], 3 * len(keys),
                                              gather_copies([axis[k[0]] for k in keys]), [token])
        started[g] = (send, recv, thru)
    ready = {}

    def fetch(n, l, after):
        g = next(g for g, keys in groups.items() if (n, l) in keys)
        if g not in ready:
            send, recv, thru = started[g]
            axes = [axis[k[0]] for k in groups[g]]
            done = split_wait("gather_wait_" + g, [], thru, send, recv, gather_copies(axes), [token if g == "a" else after])
            ready[g] = dict(zip(groups[g], forward_sibling(done, axes, "gather_forward")))
        return ready[g][n, l]

    conv_full = jnp.moveaxis(conv_all[:, :n_conv, :wc], 0, 1).reshape(n_conv, N_CHIPS * wc)
    ws = {n: w[n] for n in SMALL_REPL}
    at = 0
    for n in SMALL_CONV:
        rows = w[n].shape[0] * w[n].shape[1]
        ws[n] = conv_full[at:at + rows].reshape(w[n].shape[0], w[n].shape[1], N_CHIPS * wc)
        at += rows

    halves, state = {}, {}
    reduce_groups = {"r1": [(n, 1) for n in BIG], "r0a": [(n, 0) for n in BIG[2:]], "r0b": [(n, 0) for n in BIG[:2]]}
    plan = {("layer", 1): [("pair", "r1")],
            ("ffn2", 0): [("chips", "r1")],
            ("mix", 0): [("finish", "r1"), ("pair", "r0a")],
            ("ffn1_mid", 0): [("chips", "r0a")],
            ("ffn1_grads", 0): [("pair", "r0b")],
            ("layer", 0): [("finish", "r0a")]}


    def stage_pair(g, keys, grads, after):
        gvs = [_grad_view(grads[n][l], axis[n]) for n, l in keys]
        others = [lax.empty(gv.shape[:1] + gv.shape[2:], F32) for gv in gvs]
        send, recv, others, token = split_start("pair_start_" + g, gvs, others, len(gvs), pair_copies, [after])
        state[g] = dict(sources=gvs, send=send, recv=recv, landing=others, token=token)
        return [(state[g], "token")]

    def stage_chips(g, keys, grads, after):
        st = state[g]
        axes = [axis[n] for n, _ in keys]
        others = split_wait("pair_wait_" + g, st["sources"], st["landing"], st["send"], st["recv"], pair_copies,
                            [after, st["token"]])
        sums = [add_pair(gv, o, place, "add_pair") for gv, o in zip(st["sources"], others)]
        qs = [lax.empty((3, s.shape[1], s.shape[2] // (N_CHIPS if ax == 1 else 1)), GRAD_WIRE) for s, ax in zip(sums, axes)]
        send, recv, qs, token = split_start("chips_start_" + g, sums, qs, 3 * len(sums), chip_copies(axes), [after])
        state[g] = dict(sources=sums, send=send, recv=recv, landing=qs, token=token)
        return [(state[g], "token")]

    def stage_finish(g, keys, grads, after):
        st = state.pop(g)
        qs = split_wait("chips_wait_" + g, st["sources"], st["landing"], st["send"], st["recv"],
                        chip_copies([axis[n] for n, _ in keys]), [after, st["token"]])
        for key, s, q in zip(keys, st["sources"], qs):
            halves[key] = add_chips(s, q, axis[key[0]], place, "add_chips")
        return [(halves, key) for key in keys]

    stages = {"pair": stage_pair, "chips": stage_chips, "finish": stage_finish}

    def progress(event, l, grads, values):
        places = []
        for stage, g in plan.get((event, l), []):
            places += stages[stage](g, reduce_groups[g], grads, values[0])
        if places:
            values, tied = lax.optimization_barrier((values, [box[k] for box, k in places]))
            for (box, k), a in zip(places, tied):
                box[k] = a
        return values

    loss_part, grad_x, grads = _local_step(x[0], mem[0], loss_target[0], fetch, ws, L, progress)
    loss = lax.psum(loss_part, ("x", "y", "c"))

    small = SMALL_REPL + SMALL_CONV
    small_g = [grads[n] if n == "norm_final" else jnp.stack(grads[n]) for n in small]
    small_sum = all_reduce_small(_pack(small_g), "reduce_small")
    grad = dict(zip(small, _unpack(small_sum, [g.shape for g in small_g])))
    for n in SMALL_CONV:
        grad[n] = lax.dynamic_slice_in_dim(grad[n], chip * wc, wc, axis=2)

    delta, new_m, new_v = {}, {}, {}

    def finish_weights(names, ready):
        keys = [(n, l) for n in names for l in range(L)]
        shard_grad = dict(zip(keys, share_sibling(ready, "share_pair")))
        for n in names:
            g0, g1 = (shard_grad[n, l].reshape(w[n].shape[1:]) for l in range(L))
            grad[n], delta[n], new_m[n], new_v[n] = adamw_layers(w[n], g0, g1, given["m_" + n], given["v_" + n], "adamw")

    stage_chips("r0b", reduce_groups["r0b"], grads, small_sum)
    ready, (state["r0b"]["token"],) = lax.optimization_barrier(
        ([halves[n, l] for n in BIG[2:] for l in range(L)], [state["r0b"]["token"]]))
    finish_weights(BIG[2:], ready)
    stage_finish("r0b", reduce_groups["r0b"], grads, delta[BIG[-1]])
    finish_weights(BIG[:2], [halves[n, l] for n in BIG[:2] for l in range(L)])
    shapes = [w[n].shape for n in small]
    packed = [_pack([src[n] for n in small]) for src in
              (w, grad, {n: given["m_" + n] for n in small}, {n: given["v_" + n] for n in small})]
    for out, p in zip((delta, new_m, new_v), adamw(*packed, "adamw_small")):
        out.update(zip(small, _unpack(p, shapes)))

    return (loss, grad_x[None], *[grad[n] for n in WEIGHTS], *[delta[n] for n in WEIGHTS],
            *[new_m[n] for n in WEIGHTS], *[new_v[n] for n in WEIGHTS])
```

```python
import functools
import math

import jax
import jax.numpy as jnp
from jax import lax
from jax.experimental import pallas as pl
from jax.experimental.pallas import tpu as pltpu

F32 = jnp.float32
BF16 = jnp.bfloat16
EPS = 1e-6
SEQ_CHUNK = 128
POOL_WINDOWS = (2, 4, 8, 16)
N_HEADS = 4
ADAM_LR, ADAM_B1, ADAM_B2, ADAM_EPS, ADAM_WD, ADAM_STEP = 0.001, 0.9, 0.999, 1e-08, 0.01, 10
VMEM_LIMIT = 56 * 1024 * 1024
MESH_ID = pl.DeviceIdType.MESH
N_CHIPS = 4
GRAD_WIRE = BF16

BIG = ("ffn1_w_in", "ffn1_w_out", "mix_w_in", "mix_w_out", "xattn_wq", "xattn_wkv", "xattn_wo",
       "ffn2_w_in", "ffn2_w_out")
COL_SHARDED = ("ffn1_w_in", "mix_w_in", "xattn_wkv", "ffn2_w_in")
SMALL_CONV = ("sconv_w", "cconv_w")
SMALL_REPL = ("norm_ffn1", "norm_mix", "sgu_norm_g", "sgu_w", "sgu_b", "cconv_ln_g", "cconv_ln_b",
              "pool_w", "pool_scale", "norm_xattn", "norm_mem", "norm_ffn2", "norm_final")
WEIGHTS = ("norm_ffn1", "ffn1_w_in", "ffn1_w_out", "norm_mix", "mix_w_in", "sconv_w", "sgu_norm_g",
           "sgu_w", "sgu_b", "cconv_w", "cconv_ln_g", "cconv_ln_b", "pool_w", "pool_scale",
           "mix_w_out", "norm_xattn", "norm_mem", "xattn_wq", "xattn_wkv", "xattn_wo", "norm_ffn2",
           "ffn2_w_in", "ffn2_w_out", "norm_final")


def _pick(n, pref, align):
    best = None
    for d in range(align, min(n, pref) + 1, align):
        if n % d == 0:
            best = d
    return best or n


def _sig(x):
    return 0.5 * jnp.tanh(0.5 * x) + 0.5


def _nt(a, b):
    return lax.dot_general(a, b, (((1,), (1,)), ((), ())), preferred_element_type=F32)


def _tn(a, b):
    return lax.dot_general(a, b, (((0,), (0,)), ((), ())), preferred_element_type=F32)


def _params(*sem):
    return pltpu.CompilerParams(dimension_semantics=sem, vmem_limit_bytes=VMEM_LIMIT)


def norm_matmul(x, g, w, name, out_dtype=F32):
    T, D = x.shape
    N = w.shape[1]
    tm, tn = _pick(T, 512, 8), _pick(N, 2048, 128)

    def body(x_ref, g_ref, w_ref, o_ref, h_ref):
        j = pl.program_id(1)

        @pl.when(j == 0)
        def _():
            xv = x_ref[...]
            r = lax.rsqrt(jnp.mean(xv * xv, axis=-1, keepdims=True) + EPS)
            h_ref[...] = (xv * r * g_ref[...]).astype(BF16)

        o_ref[...] = jnp.dot(h_ref[...], w_ref[...], preferred_element_type=F32).astype(out_dtype)

    return pl.pallas_call(
        body, name=name, grid=(T // tm, N // tn),
        in_specs=[pl.BlockSpec((tm, D), lambda i, j: (i, 0)), pl.BlockSpec((1, D), lambda i, j: (0, 0)),
                  pl.BlockSpec((D, tn), lambda i, j: (0, j))],
        out_specs=[pl.BlockSpec((tm, tn), lambda i, j: (i, j)), pl.BlockSpec((tm, D), lambda i, j: (i, 0))],
        out_shape=[jax.ShapeDtypeStruct((T, N), out_dtype), jax.ShapeDtypeStruct((T, D), BF16)],
        compiler_params=_params("parallel", "arbitrary"))(x, g, w)


def matmul_res(res, a, w, name):
    T, K = a.shape
    N = w.shape[1]
    tm, tn = _pick(T, 512, 8), _pick(N, 1024, 128)

    def body(r_ref, a_ref, w_ref, o_ref):
        o_ref[...] = r_ref[...] + jnp.dot(a_ref[...].astype(BF16), w_ref[...], preferred_element_type=F32)

    return pl.pallas_call(
        body, name=name, grid=(T // tm, N // tn),
        in_specs=[pl.BlockSpec((tm, tn), lambda i, j: (i, j)), pl.BlockSpec((tm, K), lambda i, j: (i, 0)),
                  pl.BlockSpec((K, tn), lambda i, j: (0, j))],
        out_specs=pl.BlockSpec((tm, tn), lambda i, j: (i, j)),
        out_shape=jax.ShapeDtypeStruct((T, N), F32),
        compiler_params=_params("parallel", "parallel"))(res, a, w)


def matmul_nt(a, w, name, out_dtype=F32):
    T, N = a.shape
    M = w.shape[0]
    tm, tmm = _pick(T, 512, 8), _pick(M, 1024, 128)

    def body(a_ref, w_ref, o_ref):
        o_ref[...] = _nt(a_ref[...].astype(BF16), w_ref[...]).astype(out_dtype)

    return pl.pallas_call(
        body, name=name, grid=(T // tm, M // tmm),
        in_specs=[pl.BlockSpec((tm, N), lambda i, j: (i, 0)), pl.BlockSpec((tmm, N), lambda i, j: (j, 0))],
        out_specs=pl.BlockSpec((tm, tmm), lambda i, j: (i, j)),
        out_shape=jax.ShapeDtypeStruct((T, M), out_dtype),
        compiler_params=_params("parallel", "parallel"))(a, w)


def matmul_tn(a, b, scale, name, b2=None):
    T, M = a.shape
    Nb = b.shape[1]
    bm, bn, bk = _pick(M, 1408, 128), _pick(Nb, 1408, 128), _pick(T, 512, 8)
    nk, nj = T // bk, Nb // bn

    def body(a_ref, b_ref, *rest):
        o_ref = rest[-1]
        j, k = pl.program_id(1), pl.program_id(2)

        @pl.when(k == 0)
        def _():
            o_ref[...] = jnp.zeros_like(o_ref)

        a_blk = a_ref[...].astype(BF16)
        if b2 is None:
            o_ref[...] += _tn(a_blk, b_ref[...].astype(BF16))
        else:
            @pl.when(j < nj)
            def _():
                o_ref[...] += _tn(a_blk, b_ref[...].astype(BF16))

            @pl.when(j >= nj)
            def _():
                o_ref[...] += _tn(a_blk, rest[0][...].astype(BF16))

        if scale != 1.0:
            @pl.when(k == nk - 1)
            def _():
                o_ref[...] = o_ref[...] * scale

    if b2 is None:
        b_specs, operands, n_out = [pl.BlockSpec((bk, bn), lambda i, j, k: (k, j))], (a, b), nj
    else:
        first = lambda i, j, k: (jnp.where(j < nj, k, 0), jnp.where(j < nj, j, 0))
        second = lambda i, j, k: (jnp.where(j >= nj, k, 0), jnp.where(j >= nj, j - nj, 0))
        b_specs, operands, n_out = [pl.BlockSpec((bk, bn), first), pl.BlockSpec((bk, bn), second)], (a, b, b2), 2 * nj
    return pl.pallas_call(
        body, name=name, grid=(M // bm, n_out, nk),
        in_specs=[pl.BlockSpec((bk, bm), lambda i, j, k: (k, i))] + b_specs,
        out_specs=pl.BlockSpec((bm, bn), lambda i, j, k: (i, j)),
        out_shape=jax.ShapeDtypeStruct((M, n_out * bn), F32),
        compiler_params=_params("parallel", "parallel", "arbitrary"))(*operands)


def rmsnorm_bwd(dxo, dh, x, g, name):
    T, D = x.shape
    tm = _pick(T, 512, 8)
    has_res = dxo is not None

    def body(*refs):
        if has_res:
            dxo_ref, dh_ref, x_ref, g_ref, dx_ref, dg_ref = refs
        else:
            dh_ref, x_ref, g_ref, dx_ref, dg_ref = refs
        i = pl.program_id(0)

        @pl.when(i == 0)
        def _():
            dg_ref[...] = jnp.zeros_like(dg_ref)

        xv, dh_v = x_ref[...], dh_ref[...]
        r = lax.rsqrt(jnp.mean(xv * xv, axis=-1, keepdims=True) + EPS)
        xh = xv * r
        dg_ref[...] += jnp.sum(dh_v * xh, axis=0, keepdims=True)
        dxh = dh_v * g_ref[...]
        dx = r * (dxh - xh * jnp.mean(dxh * xh, axis=-1, keepdims=True))
        dx_ref[...] = dx + dxo_ref[...] if has_res else dx

    tile = pl.BlockSpec((tm, D), lambda i: (i, 0))
    vec = pl.BlockSpec((1, D), lambda i: (0, 0))
    args = ([dxo] if has_res else []) + [dh, x, g]
    return pl.pallas_call(
        body, name=name, grid=(T // tm,),
        in_specs=[tile] * (len(args) - 1) + [vec],
        out_specs=[tile, vec],
        out_shape=[jax.ShapeDtypeStruct((T, D), F32), jax.ShapeDtypeStruct((1, D), F32)],
        compiler_params=_params("arbitrary"))(*args)


def ffn_fwd(x, g, w_in, w_out, name):
    T, D = x.shape
    F = w_out.shape[0]
    tm, tf = _pick(T, 512, 8), _pick(F, 1408, 128)
    nf = F // tf

    def body(x_ref, g_ref, wg_ref, wu_ref, wo_ref, o_ref, h_ref, zg_ref, zu_ref, acc_ref):
        j = pl.program_id(1)

        @pl.when(j == 0)
        def _():
            xv = x_ref[...]
            r = lax.rsqrt(jnp.mean(xv * xv, axis=-1, keepdims=True) + EPS)
            h_ref[...] = (xv * r * g_ref[...]).astype(BF16)
            acc_ref[...] = jnp.zeros_like(acc_ref)

        h = h_ref[...]
        zg = jnp.dot(h, wg_ref[...], preferred_element_type=F32)
        zu = jnp.dot(h, wu_ref[...], preferred_element_type=F32)
        zg_ref[...] = zg.astype(BF16)
        zu_ref[...] = zu.astype(BF16)
        a = (zg * _sig(zg) * zu).astype(BF16)
        acc_ref[...] += jnp.dot(a, wo_ref[...], preferred_element_type=F32)

        @pl.when(j == nf - 1)
        def _():
            o_ref[...] = x_ref[...] + 0.5 * acc_ref[...]

    tile = pl.BlockSpec((tm, D), lambda i, j: (i, 0))
    fblk = pl.BlockSpec((tm, tf), lambda i, j: (i, j))
    hidden = jax.ShapeDtypeStruct((T, F), BF16)
    return pl.pallas_call(
        body, name=name, grid=(T // tm, nf),
        in_specs=[tile, pl.BlockSpec((1, D), lambda i, j: (0, 0)),
                  pl.BlockSpec((D, tf), lambda i, j: (0, j)), pl.BlockSpec((D, tf), lambda i, j: (0, j + nf)),
                  pl.BlockSpec((tf, D), lambda i, j: (j, 0))],
        out_specs=[tile, tile, fblk, fblk],
        out_shape=[jax.ShapeDtypeStruct((T, D), F32), jax.ShapeDtypeStruct((T, D), BF16), hidden, hidden],
        scratch_shapes=[pltpu.VMEM((tm, D), F32)],
        compiler_params=_params("parallel", "arbitrary"))(x, g, w_in, w_in, w_out)


def ffn_dz(dxo, zg, zu, w_out, name):
    T, D = dxo.shape
    F = w_out.shape[0]
    tm, tf = _pick(T, 512, 8), _pick(F, 256, 128)

    def body(dxo_ref, zg_ref, zu_ref, wo_ref, a_ref, dzg_ref, dzu_ref):
        do = (0.5 * dxo_ref[...]).astype(BF16)
        for j in range(F // tf):
            cols = slice(j * tf, (j + 1) * tf)
            zg, zu = zg_ref[:, cols].astype(F32), zu_ref[:, cols].astype(F32)
            s = _sig(zg)
            silu = zg * s
            a_ref[:, cols] = (silu * zu).astype(BF16)
            da = _nt(do, wo_ref[cols, :])
            dzu_ref[:, cols] = (da * silu).astype(BF16)
            dzg_ref[:, cols] = (da * zu * (s + silu * (1.0 - s))).astype(BF16)

    rows = pl.BlockSpec((tm, F), lambda i: (i, 0))
    hidden = jax.ShapeDtypeStruct((T, F), BF16)
    return pl.pallas_call(
        body, name=name, grid=(T // tm,),
        in_specs=[pl.BlockSpec((tm, D), lambda i: (i, 0)), rows, rows, pl.BlockSpec((F, D), lambda i: (0, 0))],
        out_specs=[rows, rows, rows], out_shape=[hidden, hidden, hidden],
        compiler_params=_params("parallel"))(dxo, zg, zu, w_out)


def dh_norm_bwd(x, dxo, g, parts, w, name):
    T, D = x.shape
    F = parts[0].shape[1]
    n = len(parts)
    tm = _pick(T, 256, 8)

    def body(x_ref, dxo_ref, g_ref, *refs):
        a_refs, w_refs, (dx_ref, dg_ref) = refs[:n], refs[n:2 * n], refs[2 * n:]
        i = pl.program_id(0)

        @pl.when(i == 0)
        def _():
            dg_ref[...] = jnp.zeros_like(dg_ref)

        dh = sum(_nt(a_ref[...], w_ref[...]) for a_ref, w_ref in zip(a_refs, w_refs))
        xv = x_ref[...]
        r = lax.rsqrt(jnp.mean(xv * xv, axis=-1, keepdims=True) + EPS)
        xh = xv * r
        dg_ref[...] += jnp.sum(dh * xh, axis=0, keepdims=True)
        dxh = dh * g_ref[...]
        dx_ref[...] = dxo_ref[...] + r * (dxh - xh * jnp.mean(dxh * xh, axis=-1, keepdims=True))

    tile = pl.BlockSpec((tm, D), lambda i: (i, 0))
    vec = pl.BlockSpec((1, D), lambda i: (0, 0))
    return pl.pallas_call(
        body, name=name, grid=(T // tm,),
        in_specs=[tile, tile, vec] + [pl.BlockSpec((tm, F), lambda i: (i, 0))] * n
                 + [pl.BlockSpec((D, F), lambda i, p=p: (0, p)) for p in range(n)],
        out_specs=[tile, vec],
        out_shape=[jax.ShapeDtypeStruct((T, D), F32), jax.ShapeDtypeStruct((1, D), F32)],
        compiler_params=_params("arbitrary"))(x, dxo, g, *parts, *([w] * n))


def _chunks(T, fn):
    def step(c, carry):
        fn(pl.multiple_of(c * SEQ_CHUNK, SEQ_CHUNK))
        return carry
    lax.fori_loop(0, T // SEQ_CHUNK, step, 0)


def _conv_taps(win, ktaps, pad):
    return [(win if k == ktaps - 1 else pltpu.roll(win, ktaps - 1 - k, 0))[pad:, :] for k in range(ktaps)]


def _conv_taps_t(win, ktaps, pad):
    n = win.shape[0]
    return [(win if k == ktaps - 1 else pltpu.roll(win, n - (ktaps - 1 - k), 0))[:n - pad, :] for k in range(ktaps)]


def _col(T, W, idx):
    return pl.BlockSpec((T, W), lambda i, idx=idx: (0, idx))


def _full(shape):
    return pl.BlockSpec(shape, lambda i: (0,) * len(shape))


def mix_a_fwd(z, w, name):
    T, W = z.shape[0], w.shape[1]
    K, P = w.shape[0], 8

    def body(ab_ref, ac_ref, ax_ref, w_ref, y_ref, pp_ref):
        pp_ref[0:P, :] = jnp.zeros((P, W), F32)

        def chunk(s):
            rows = pl.ds(s, SEQ_CHUNK)
            pp_ref[pl.ds(s + P, SEQ_CHUNK), :] = ac_ref[rows, :] * ax_ref[rows, :]
            taps = _conv_taps(pp_ref[pl.ds(s, SEQ_CHUNK + P), :], K, P)
            q = sum(w_ref[k:k + 1, :] * taps[k] for k in range(K))
            y_ref[rows, :] = (ab_ref[rows, :] * q).astype(BF16)

        _chunks(T, chunk)

    return pl.pallas_call(
        body, name=name, grid=(1,),
        in_specs=[_col(T, W, 0), _col(T, W, 1), _col(T, W, 2), _full((K, W))],
        out_specs=_full((T, W)), out_shape=jax.ShapeDtypeStruct((T, W), BF16),
        scratch_shapes=[pltpu.VMEM((T + P, W), F32)],
        compiler_params=_params("arbitrary"))(z, z, z, w)


def mix_a_bwd(z, dy, w, name):
    T, W = z.shape[0], w.shape[1]
    K, P = w.shape[0], 8

    def body(ab_ref, ac_ref, ax_ref, dy_ref, w_ref, dab_ref, dac_ref, dax_ref, dw_ref, pp_ref, dq_ref):
        pp_ref[0:P, :] = jnp.zeros((P, W), F32)
        dq_ref[T:T + P, :] = jnp.zeros((P, W), F32)
        dw_ref[...] = jnp.zeros_like(dw_ref)

        def chunk1(s):
            rows = pl.ds(s, SEQ_CHUNK)
            pp_ref[pl.ds(s + P, SEQ_CHUNK), :] = ac_ref[rows, :] * ax_ref[rows, :]
            taps = _conv_taps(pp_ref[pl.ds(s, SEQ_CHUNK + P), :], K, P)
            q = sum(w_ref[k:k + 1, :] * taps[k] for k in range(K))
            dyv = dy_ref[rows, :]
            dab_ref[rows, :] = (dyv * q).astype(BF16)
            dq = dyv * ab_ref[rows, :]
            dq_ref[rows, :] = dq
            for k in range(K):
                dw_ref[k:k + 1, :] += jnp.sum(dq * taps[k], axis=0, keepdims=True)

        _chunks(T, chunk1)

        def chunk2(s):
            rows = pl.ds(s, SEQ_CHUNK)
            taps = _conv_taps_t(dq_ref[pl.ds(s, SEQ_CHUNK + P), :], K, P)
            dp = sum(w_ref[k:k + 1, :] * taps[k] for k in range(K))
            dac_ref[rows, :] = (dp * ax_ref[rows, :]).astype(BF16)
            dax_ref[rows, :] = (dp * ac_ref[rows, :]).astype(BF16)

        _chunks(T, chunk2)

    tw = jax.ShapeDtypeStruct((T, W), BF16)
    return pl.pallas_call(
        body, name=name, grid=(1,),
        in_specs=[_col(T, W, 0), _col(T, W, 1), _col(T, W, 2), _col(T, W, 0), _full((K, W))],
        out_specs=[_full((T, W))] * 3 + [_full((K, W))],
        out_shape=[tw, tw, tw, jax.ShapeDtypeStruct((K, W), F32)],
        scratch_shapes=[pltpu.VMEM((T + P, W), F32), pltpu.VMEM((T + P, W), F32)],
        compiler_params=_params("arbitrary"))(z, z, z, dy, w)


def _ln_stats(v):
    mu = jnp.mean(v, axis=-1, keepdims=True)
    xc = v - mu
    rstd = lax.rsqrt(jnp.mean(xc * xc, axis=-1, keepdims=True) + EPS)
    return xc * rstd, rstd


def _ln_bwd(dxh, xh, rstd):
    return rstd * (dxh - jnp.mean(dxh, axis=-1, keepdims=True) - xh * jnp.mean(dxh * xh, axis=-1, keepdims=True))


def _tril_bf16(w_ref, h):
    n = w_ref.shape[-1]
    keep = lax.broadcasted_iota(jnp.int32, (n, n), 0) >= lax.broadcasted_iota(jnp.int32, (n, n), 1)
    return jnp.where(keep, w_ref[h], 0.0).astype(BF16)


def mix_b_fwd(z, g, w_s, bias, name):
    T, W = z.shape[0], g.shape[1]
    H, C = w_s.shape[0], w_s.shape[1]
    hd = W // H

    def body(u_ref, v_ref, g_ref, w_ref, b_ref, y_ref):
        wts = [_tril_bf16(w_ref, h) for h in range(H)]
        head = lax.broadcasted_iota(jnp.int32, (C, W), 1) // hd

        def chunk(s):
            rows = pl.ds(s, C)
            xh, _ = _ln_stats(v_ref[rows, :])
            vn = (xh * g_ref[...]).astype(BF16)
            mixed = b_ref[...]
            for h in range(H):
                mixed = mixed + jnp.where(head == h, jnp.dot(wts[h], vn, preferred_element_type=F32), 0.0)
            y_ref[rows, :] = (u_ref[rows, :] * mixed).astype(BF16)

        _chunks(T, chunk)

    return pl.pallas_call(
        body, name=name, grid=(1,),
        in_specs=[_col(T, W, 3), _col(T, W, 4), _full((1, W)), _full((H, C, C)), _full((C, W))],
        out_specs=_full((T, W)), out_shape=jax.ShapeDtypeStruct((T, W), BF16),
        compiler_params=_params("arbitrary"))(z, z, g, w_s, bias)


def mix_b_bwd(z, dy, g, w_s, bias, name):
    T, W = z.shape[0], g.shape[1]
    H, C = w_s.shape[0], w_s.shape[1]
    hd = W // H

    def body(u_ref, v_ref, dy_ref, g_ref, w_ref, b_ref, du_ref, dv_ref, dw_ref, db_ref, dg_ref, dbf_ref):
        wts = [_tril_bf16(w_ref, h) for h in range(H)]
        head = lax.broadcasted_iota(jnp.int32, (C, W), 1) // hd
        dw_ref[...] = jnp.zeros_like(dw_ref)
        dg_ref[...] = jnp.zeros_like(dg_ref)
        dbf_ref[...] = jnp.zeros_like(dbf_ref)

        def chunk(s):
            rows = pl.ds(s, C)
            xh, rstd = _ln_stats(v_ref[rows, :])
            vn = (xh * g_ref[...]).astype(BF16)
            mixed = b_ref[...]
            for h in range(H):
                mixed = mixed + jnp.where(head == h, jnp.dot(wts[h], vn, preferred_element_type=F32), 0.0)
            dyv = dy_ref[rows, :]
            du_ref[rows, :] = (dyv * mixed).astype(BF16)
            dm = dyv * u_ref[rows, :]
            dbf_ref[...] += dm
            dvn = jnp.zeros((C, W), F32)
            for h in range(H):
                dmh = jnp.where(head == h, dm, 0.0).astype(BF16)
                dw_ref[h] += _nt(dmh, vn)
                dvn = dvn + _tn(wts[h], dmh)
            dg_ref[...] += jnp.sum(dvn * xh, axis=0, keepdims=True)
            dv_ref[rows, :] = _ln_bwd(dvn * g_ref[...], xh, rstd).astype(BF16)

        _chunks(T, chunk)

        keep = lax.broadcasted_iota(jnp.int32, (C, C), 0) >= lax.broadcasted_iota(jnp.int32, (C, C), 1)
        lane = lax.broadcasted_iota(jnp.int32, (C, 128), 1)
        db = jnp.zeros((C, 128), F32)
        dbf = dbf_ref[...]
        for h in range(H):
            dw_ref[h] = jnp.where(keep, dw_ref[h], 0.0)
            db = db + jnp.where(lane == h, jnp.sum(jnp.where(head == h, dbf, 0.0), axis=1, keepdims=True), 0.0)
        db_ref[...] = db

    tw = jax.ShapeDtypeStruct((T, W), BF16)
    return pl.pallas_call(
        body, name=name, grid=(1,),
        in_specs=[_col(T, W, 3), _col(T, W, 4), _col(T, W, 1), _full((1, W)), _full((H, C, C)), _full((C, W))],
        out_specs=[_full((T, W)), _full((T, W)), _full((H, C, C)), _full((C, 128)), _full((1, W))],
        out_shape=[tw, tw, jax.ShapeDtypeStruct((H, C, C), F32), jax.ShapeDtypeStruct((C, 128), F32),
                   jax.ShapeDtypeStruct((1, W), F32)],
        scratch_shapes=[pltpu.VMEM((C, W), F32)],
        compiler_params=_params("arbitrary"))(z, z, dy, g, w_s, bias)


def mix_c_fwd(z, w, ln_g, ln_b, name):
    T, W = z.shape[0], w.shape[1]
    K, P = w.shape[0], 32

    def body(a_ref, gt_ref, w_ref, g_ref, b_ref, y_ref, up_ref):
        up_ref[0:P, :] = jnp.zeros((P, W), F32)

        def chunk(s):
            rows = pl.ds(s, SEQ_CHUNK)
            up_ref[pl.ds(s + P, SEQ_CHUNK), :] = a_ref[rows, :] * _sig(gt_ref[rows, :])
            taps = _conv_taps(up_ref[pl.ds(s, SEQ_CHUNK + P), :], K, P)
            q = sum(w_ref[k:k + 1, :] * taps[k] for k in range(K))
            xh, _ = _ln_stats(q)
            r = xh * g_ref[...] + b_ref[...]
            y_ref[rows, :] = (r * _sig(r)).astype(BF16)

        _chunks(T, chunk)

    return pl.pallas_call(
        body, name=name, grid=(1,),
        in_specs=[_col(T, W, 5), _col(T, W, 6), _full((K, W)), _full((1, W)), _full((1, W))],
        out_specs=_full((T, W)), out_shape=jax.ShapeDtypeStruct((T, W), BF16),
        scratch_shapes=[pltpu.VMEM((T + P, W), F32)],
        compiler_params=_params("arbitrary"))(z, z, w, ln_g, ln_b)


def mix_c_bwd(z, dy, w, ln_g, ln_b, name):
    T, W = z.shape[0], w.shape[1]
    K, P = w.shape[0], 32

    def body(a_ref, gt_ref, dy_ref, w_ref, g_ref, b_ref, da_ref, dgt_ref, dw_ref, dg_ref, db_ref, up_ref, dq_ref):
        up_ref[0:P, :] = jnp.zeros((P, W), F32)
        dq_ref[T:T + P, :] = jnp.zeros((P, W), F32)
        dw_ref[...] = jnp.zeros_like(dw_ref)
        dg_ref[...] = jnp.zeros_like(dg_ref)
        db_ref[...] = jnp.zeros_like(db_ref)

        def chunk1(s):
            rows = pl.ds(s, SEQ_CHUNK)
            up_ref[pl.ds(s + P, SEQ_CHUNK), :] = a_ref[rows, :] * _sig(gt_ref[rows, :])
            taps = _conv_taps(up_ref[pl.ds(s, SEQ_CHUNK + P), :], K, P)
            q = sum(w_ref[k:k + 1, :] * taps[k] for k in range(K))
            xh, rstd = _ln_stats(q)
            r = xh * g_ref[...] + b_ref[...]
            sr = _sig(r)
            dr = dy_ref[rows, :] * (sr * (1.0 + r * (1.0 - sr)))
            db_ref[...] += jnp.sum(dr, axis=0, keepdims=True)
            dg_ref[...] += jnp.sum(dr * xh, axis=0, keepdims=True)
            dq = _ln_bwd(dr * g_ref[...], xh, rstd)
            dq_ref[rows, :] = dq
            for k in range(K):
                dw_ref[k:k + 1, :] += jnp.sum(dq * taps[k], axis=0, keepdims=True)

        _chunks(T, chunk1)

        def chunk2(s):
            rows = pl.ds(s, SEQ_CHUNK)
            taps = _conv_taps_t(dq_ref[pl.ds(s, SEQ_CHUNK + P), :], K, P)
            du = sum(w_ref[k:k + 1, :] * taps[k] for k in range(K))
            sg = _sig(gt_ref[rows, :])
            da_ref[rows, :] = (du * sg).astype(BF16)
            dgt_ref[rows, :] = (du * a_ref[rows, :] * sg * (1.0 - sg)).astype(BF16)

        _chunks(T, chunk2)

    tw = jax.ShapeDtypeStruct((T, W), BF16)
    vec = jax.ShapeDtypeStruct((1, W), F32)
    return pl.pallas_call(
        body, name=name, grid=(1,),
        in_specs=[_col(T, W, 5), _col(T, W, 6), _col(T, W, 2), _full((K, W)), _full((1, W)), _full((1, W))],
        out_specs=[_full((T, W)), _full((T, W)), _full((K, W)), _full((1, W)), _full((1, W))],
        out_shape=[tw, tw, jax.ShapeDtypeStruct((K, W), F32), vec, vec],
        scratch_shapes=[pltpu.VMEM((T + P, W), F32), pltpu.VMEM((T + P, W), F32)],
        compiler_params=_params("arbitrary"))(z, z, dy, w, ln_g, ln_b)


def _pool_select(levels, W, rows):
    group = lax.broadcasted_iota(jnp.int32, (rows, W), 1) // (W // len(POOL_WINDOWS))
    out = levels[-1]
    for gi in range(len(POOL_WINDOWS) - 2, -1, -1):
        out = jnp.where(group == gi, levels[gi], out)
    return out


def _pool_count(s, W):
    t = s + lax.broadcasted_iota(jnp.int32, (SEQ_CHUNK, W), 0)
    group = lax.broadcasted_iota(jnp.int32, (SEQ_CHUNK, W), 1) // (W // len(POOL_WINDOWS))
    win = jnp.full((SEQ_CHUNK, W), POOL_WINDOWS[-1], jnp.int32)
    for gi in range(len(POOL_WINDOWS) - 2, -1, -1):
        win = jnp.where(group == gi, POOL_WINDOWS[gi], win)
    return jnp.minimum(t + 1, win).astype(F32)


def _pooled(wp_ref, s, W, P):
    win = wp_ref[pl.ds(s, SEQ_CHUNK + P), :]
    levels, acc, shift = [], win, 1
    for _ in POOL_WINDOWS:
        acc = acc + pltpu.roll(acc, shift, 0)
        levels.append(acc[P:, :])
        shift *= 2
    return _pool_select(levels, W, SEQ_CHUNK) / _pool_count(s, W) - win[P:, :]


def mix_d_fwd(z, pbd, scale, name):
    T, W = z.shape[0], scale.shape[1]
    P = 16

    def body(x_ref, p_ref, s_ref, y_ref, wp_ref):
        wp_ref[0:P, :] = jnp.zeros((P, W), F32)

        def chunk(s):
            rows = pl.ds(s, SEQ_CHUNK)
            wp_ref[pl.ds(s + P, SEQ_CHUNK), :] = x_ref[rows, :]
            pooled = _pooled(wp_ref, s, W, P).astype(BF16)
            y_ref[rows, :] = (jnp.dot(pooled, p_ref[...], preferred_element_type=F32) * s_ref[...]).astype(BF16)

        _chunks(T, chunk)

    return pl.pallas_call(
        body, name=name, grid=(1,),
        in_specs=[_col(T, W, 7), _full((W, W)), _full((1, W))],
        out_specs=_full((T, W)), out_shape=jax.ShapeDtypeStruct((T, W), BF16),
        scratch_shapes=[pltpu.VMEM((T + P, W), F32)],
        compiler_params=_params("arbitrary"))(z, pbd, scale)


def mix_d_bwd(z, dy, pbd, scale, name):
    T, W = z.shape[0], scale.shape[1]
    P = 16

    def body(x_ref, dy_ref, p_ref, s_ref, dx_ref, dp_ref, ds_ref, wp_ref, e_ref, dpool_ref):
        wp_ref[0:P, :] = jnp.zeros((P, W), F32)
        e_ref[T:T + P, :] = jnp.zeros((P, W), F32)
        dp_ref[...] = jnp.zeros_like(dp_ref)
        ds_ref[...] = jnp.zeros_like(ds_ref)

        def chunk1(s):
            rows = pl.ds(s, SEQ_CHUNK)
            wp_ref[pl.ds(s + P, SEQ_CHUNK), :] = x_ref[rows, :]
            pooled = _pooled(wp_ref, s, W, P).astype(BF16)
            yl = jnp.dot(pooled, p_ref[...], preferred_element_type=F32)
            dyv = dy_ref[rows, :]
            ds_ref[...] += jnp.sum(dyv * yl, axis=0, keepdims=True)
            dyl = (dyv * s_ref[...]).astype(BF16)
            dp_ref[...] += _tn(pooled, dyl)
            dpool = _nt(dyl, p_ref[...])
            dpool_ref[rows, :] = dpool
            e_ref[rows, :] = dpool / _pool_count(s, W)

        _chunks(T, chunk1)

        def chunk2(s):
            rows = pl.ds(s, SEQ_CHUNK)
            win = e_ref[pl.ds(s, SEQ_CHUNK + P), :]
            n = SEQ_CHUNK + P
            levels, acc, shift = [], win, 1
            for _ in POOL_WINDOWS:
                acc = acc + pltpu.roll(acc, n - shift, 0)
                levels.append(acc[:SEQ_CHUNK, :])
                shift *= 2
            dx_ref[rows, :] = (_pool_select(levels, W, SEQ_CHUNK) - dpool_ref[rows, :]).astype(BF16)

        _chunks(T, chunk2)

    return pl.pallas_call(
        body, name=name, grid=(1,),
        in_specs=[_col(T, W, 7), _col(T, W, 3), _full((W, W)), _full((1, W))],
        out_specs=[_full((T, W)), _full((W, W)), _full((1, W))],
        out_shape=[jax.ShapeDtypeStruct((T, W), BF16), jax.ShapeDtypeStruct((W, W), F32),
                   jax.ShapeDtypeStruct((1, W), F32)],
        scratch_shapes=[pltpu.VMEM((T + P, W), F32), pltpu.VMEM((T + P, W), F32), pltpu.VMEM((T, W), F32)],
        compiler_params=_params("arbitrary"))(z, dy, pbd, scale)


def attn_fwd(q, kv, name):
    T, D = q.shape
    M = kv.shape[0]
    hd = D // N_HEADS
    tm = _pick(T, 512, 8)
    sc = 1.0 / math.sqrt(hd)

    def body(q_ref, k_ref, v_ref, o_ref):
        for h in range(N_HEADS):
            cols = slice(h * hd, (h + 1) * hd)
            s = _nt(q_ref[:, cols].astype(BF16), k_ref[:, cols].astype(BF16)) * sc
            p = jnp.exp(s - jnp.max(s, axis=-1, keepdims=True))
            p = p / jnp.sum(p, axis=-1, keepdims=True)
            o_ref[:, cols] = jnp.dot(p.astype(BF16), v_ref[:, cols].astype(BF16),
                                     preferred_element_type=F32).astype(BF16)

    return pl.pallas_call(
        body, name=name, grid=(T // tm,),
        in_specs=[pl.BlockSpec((tm, D), lambda i: (i, 0)), pl.BlockSpec((M, D), lambda i: (0, 0)),
                  pl.BlockSpec((M, D), lambda i: (0, 1))],
        out_specs=pl.BlockSpec((tm, D), lambda i: (i, 0)),
        out_shape=jax.ShapeDtypeStruct((T, D), BF16),
        compiler_params=_params("parallel"))(q, kv, kv)


def attn_bwd(q, kv, do, name):
    T, D = q.shape
    M = kv.shape[0]
    hd = D // N_HEADS
    tm = _pick(T, 512, 8)
    sc = 1.0 / math.sqrt(hd)

    def body(q_ref, k_ref, v_ref, do_ref, dq_ref, dk_ref, dv_ref):
        i = pl.program_id(0)

        @pl.when(i == 0)
        def _():
            dk_ref[...] = jnp.zeros_like(dk_ref)
            dv_ref[...] = jnp.zeros_like(dv_ref)

        for h in range(N_HEADS):
            cols = slice(h * hd, (h + 1) * hd)
            qh, kh = q_ref[:, cols].astype(BF16), k_ref[:, cols].astype(BF16)
            vh, doh = v_ref[:, cols].astype(BF16), do_ref[:, cols].astype(BF16)
            s = _nt(qh, kh) * sc
            p = jnp.exp(s - jnp.max(s, axis=-1, keepdims=True))
            p = p / jnp.sum(p, axis=-1, keepdims=True)
            dp = _nt(doh, vh)
            dv_ref[:, cols] += _tn(p.astype(BF16), doh)
            ds = (p * (dp - jnp.sum(dp * p, axis=-1, keepdims=True)) * sc).astype(BF16)
            dq_ref[:, cols] = jnp.dot(ds, kh, preferred_element_type=F32).astype(BF16)
            dk_ref[:, cols] += _tn(ds, qh)

    tile = pl.BlockSpec((tm, D), lambda i: (i, 0))
    mem = jax.ShapeDtypeStruct((M, D), F32)
    return pl.pallas_call(
        body, name=name, grid=(T // tm,),
        in_specs=[tile, pl.BlockSpec((M, D), lambda i: (0, 0)), pl.BlockSpec((M, D), lambda i: (0, 1)), tile],
        out_specs=[tile, pl.BlockSpec((M, D), lambda i: (0, 0)), pl.BlockSpec((M, D), lambda i: (0, 0))],
        out_shape=[jax.ShapeDtypeStruct((T, D), BF16), mem, mem],
        compiler_params=_params("arbitrary"))(q, kv, kv, do)


def loss_head(x, g, target, name):
    T, D = x.shape
    tm = _pick(T, 512, 8)

    def body(x_ref, g_ref, t_ref, l_ref, dx_ref, dg_ref):
        i = pl.program_id(0)

        @pl.when(i == 0)
        def _():
            l_ref[...] = jnp.zeros_like(l_ref)
            dg_ref[...] = jnp.zeros_like(dg_ref)

        xv = x_ref[...]
        r = lax.rsqrt(jnp.mean(xv * xv, axis=-1, keepdims=True) + EPS)
        xh = xv * r
        err = xh * g_ref[...] - t_ref[...]
        l_ref[...] += 0.5 * jnp.sum(jnp.mean(err * err, axis=-1, keepdims=True), axis=0, keepdims=True)
        dy = err * (1.0 / D)
        dg_ref[...] += jnp.sum(dy * xh, axis=0, keepdims=True)
        dxh = dy * g_ref[...]
        dx_ref[...] = r * (dxh - xh * jnp.mean(dxh * xh, axis=-1, keepdims=True))

    tile = pl.BlockSpec((tm, D), lambda i: (i, 0))
    vec = pl.BlockSpec((1, D), lambda i: (0, 0))
    return pl.pallas_call(
        body, name=name, grid=(T // tm,),
        in_specs=[tile, vec, tile],
        out_specs=[pl.BlockSpec((1, 128), lambda i: (0, 0)), tile, vec],
        out_shape=[jax.ShapeDtypeStruct((1, 128), F32), jax.ShapeDtypeStruct((T, D), F32),
                   jax.ShapeDtypeStruct((1, D), F32)],
        compiler_params=_params("arbitrary"))(x, g, target)


def _block_diag(p):
    G, gd, _ = p.shape
    rows = [jnp.concatenate([p[g] if g == c else jnp.zeros((gd, gd), p.dtype) for c in range(G)], axis=1)
            for g in range(G)]
    return jnp.concatenate(rows, axis=0)


class _LazyWeight:
    def __init__(self, fetch, name, latest):
        self.fetch, self.name, self.latest = fetch, name, latest

    def __getitem__(self, l):
        return self.fetch(self.name, l, self.latest[0])


def _local_step(x, mem, target, fetch, ws, L, progress=lambda event, l, grads, values: values):
    T, D = x.shape
    W = D // 4
    H = ws["sgu_w"].shape[1]
    row = lambda v: v.reshape(1, -1)
    latest = [x]
    wb = {n: _LazyWeight(fetch, n, latest) for n in BIG}
    saved = []
    for l in range(L):
        s = {"x0": x}
        latest[0] = x
        x, *s["ffn1"] = ffn_fwd(x, row(ws["norm_ffn1"][l]), wb["ffn1_w_in"][l], wb["ffn1_w_out"][l], "ffn_fwd")
        s["x1"] = x
        latest[0] = x
        z, s["h_mix"] = norm_matmul(x, row(ws["norm_mix"][l]), wb["mix_w_in"][l], "mix_in")
        s["z"] = z
        s["bias"] = jnp.repeat(ws["sgu_b"][l].T, W // H, axis=1)
        s["pbd"] = _block_diag(ws["pool_w"][l]).astype(BF16)
        y = jnp.concatenate([
            mix_a_fwd(z, ws["sconv_w"][l], "mix_a_fwd"),
            mix_b_fwd(z, row(ws["sgu_norm_g"][l]), ws["sgu_w"][l], s["bias"], "mix_b_fwd"),
            mix_c_fwd(z, ws["cconv_w"][l], row(ws["cconv_ln_g"][l]), row(ws["cconv_ln_b"][l]), "mix_c_fwd"),
            mix_d_fwd(z, s["pbd"], row(ws["pool_scale"][l]), "mix_d_fwd")], axis=1)
        s["y"] = y
        x = matmul_res(x, y, wb["mix_w_out"][l], "mix_out")
        s["x2"] = x
        latest[0] = x
        s["q"], s["hq"] = norm_matmul(x, row(ws["norm_xattn"][l]), wb["xattn_wq"][l], "attn_q", out_dtype=BF16)
        s["kv"], s["mn"] = norm_matmul(mem, row(ws["norm_mem"][l]), wb["xattn_wkv"][l], "attn_kv")
        s["o"] = attn_fwd(s["q"], s["kv"], "attn_fwd")
        x = matmul_res(x, s["o"], wb["xattn_wo"][l], "attn_out")
        s["x3"] = x
        latest[0] = x
        x, *s["ffn2"] = ffn_fwd(x, row(ws["norm_ffn2"][l]), wb["ffn2_w_in"][l], wb["ffn2_w_out"][l], "ffn_fwd")
        saved.append(s)

    loss, dx, dg_final = loss_head(x, row(ws["norm_final"]), target, "loss_head")
    grads = {n: [None] * L for n in WEIGHTS if n != "norm_final"}
    grads["norm_final"] = dg_final.reshape(-1)

    def pin(dx, names, l):
        dx, made = lax.optimization_barrier((dx, [grads[n][l] for n in names]))
        for n, g in zip(names, made):
            grads[n][l] = g
        return dx

    def after_stages(event, l, values):
        return progress(event, l, grads, values)

    def ffn_back(xin, kept, dxo, gname, win, wout, l, event):
        h, zg, zu = kept
        a, dzg, dzu = ffn_dz(dxo, zg, zu, wb[wout][l], "ffn_dz")
        last = l == 0 and event == "ffn1_mid"
        if not last:
            dxn, dg = dh_norm_bwd(xin, dxo, row(ws[gname][l]), [dzg, dzu], wb[win][l], "ffn_dh")
            dxn, h, a = after_stages(event, l, (dxn, h, a))
        else:
            h, a = after_stages(event, l, (h, a))
        grads[win][l] = matmul_tn(h, dzg, 1.0, "ffn_dwin", b2=dzu)
        grads[wout][l] = matmul_tn(a, dxo, 0.5, "ffn_dwout")
        if last:
            dzg, dzu = after_stages("ffn1_grads", l, (dzg, dzu))
            dxn, dg = dh_norm_bwd(xin, dxo, row(ws[gname][l]), [dzg, dzu], wb[win][l], "ffn_dh")
        grads[gname][l] = dg.reshape(-1)
        return dxn if last else pin(dxn, (win, wout), l)

    for l in reversed(range(L)):
        s = saved[l]
        dx = ffn_back(s["x3"], s["ffn2"], dx, "norm_ffn2", "ffn2_w_in", "ffn2_w_out", l, "ffn2_mid")
        dx, = after_stages("ffn2", l, (dx,))
        grads["xattn_wo"][l] = matmul_tn(s["o"], dx, 1.0, "dw_sq")
        do = matmul_nt(dx, wb["xattn_wo"][l], "attn_do", out_dtype=BF16)
        dq, dk, dv = attn_bwd(s["q"], s["kv"], do, "attn_bwd")
        grads["xattn_wq"][l] = matmul_tn(s["hq"], dq, 1.0, "dw_sq")
        dx, dg = dh_norm_bwd(s["x2"], dx, row(ws["norm_xattn"][l]), [dq], wb["xattn_wq"][l], "attn_dh")
        grads["norm_xattn"][l] = dg.reshape(-1)
        dkv = jnp.concatenate([dk, dv], axis=1)
        grads["xattn_wkv"][l] = matmul_tn(s["mn"], dkv, 1.0, "attn_dwkv")
        dmn = matmul_nt(dkv, wb["xattn_wkv"][l], "attn_dmn")
        _, dg = rmsnorm_bwd(None, dmn, mem, row(ws["norm_mem"][l]), "norm_mem_bwd")
        grads["norm_mem"][l] = dg.reshape(-1)
        dx = pin(dx, ("xattn_wo", "xattn_wq", "xattn_wkv", "norm_mem"), l)
        dx, = after_stages("attn", l, (dx,))
        grads["mix_w_out"][l] = matmul_tn(s["y"], dx, 1.0, "dw_sq")
        dy = matmul_nt(dx, wb["mix_w_out"][l], "mix_dy")
        z = s["z"]
        dab, dac, dax, dws = mix_a_bwd(z, dy, ws["sconv_w"][l], "mix_a_bwd")
        dbu, dbv, dwsgu, dbs, dgs = mix_b_bwd(z, dy, row(ws["sgu_norm_g"][l]), ws["sgu_w"][l], s["bias"], "mix_b_bwd")
        dca, dcg, dwc, dgc, dbc = mix_c_bwd(z, dy, ws["cconv_w"][l], row(ws["cconv_ln_g"][l]),
                                            row(ws["cconv_ln_b"][l]), "mix_c_bwd")
        ddw, dpbd, dsc = mix_d_bwd(z, dy, s["pbd"], row(ws["pool_scale"][l]), "mix_d_bwd")
        grads["sconv_w"][l], grads["cconv_w"][l] = dws, dwc
        grads["sgu_w"][l], grads["sgu_b"][l], grads["sgu_norm_g"][l] = dwsgu, dbs[:, :H].T, dgs.reshape(-1)
        grads["cconv_ln_g"][l], grads["cconv_ln_b"][l] = dgc.reshape(-1), dbc.reshape(-1)
        gd = W // len(POOL_WINDOWS)
        grads["pool_w"][l] = jnp.stack([dpbd[g * gd:(g + 1) * gd, g * gd:(g + 1) * gd] for g in range(len(POOL_WINDOWS))])
        grads["pool_scale"][l] = dsc.reshape(-1)
        dz = jnp.concatenate([dab, dac, dax, dbu, dbv, dca, dcg, ddw], axis=1)
        grads["mix_w_in"][l] = matmul_tn(s["h_mix"], dz, 1.0, "mix_dwin")
        dx, dg = dh_norm_bwd(s["x1"], dx, row(ws["norm_mix"][l]), [dz], wb["mix_w_in"][l], "mix_dh")
        grads["norm_mix"][l] = dg.reshape(-1)
        dx = pin(dx, ("mix_w_out", "mix_w_in"), l)
        dx, = after_stages("mix", l, (dx,))
        dx = ffn_back(s["x0"], s["ffn1"], dx, "norm_ffn1", "ffn1_w_in", "ffn1_w_out", l, "ffn1_mid")
        dx, = after_stages("layer", l, (dx,))

    return loss[0, 0], dx, grads


ANY = pl.BlockSpec(memory_space=pl.ANY)


def _other_chips(x, y):
    return [(1 - x, y), (x, 1 - y), (1 - x, 1 - y)]


def _shard_slice(ref, axis, chip, size):
    idx = [slice(None)] * len(ref.shape)
    idx[axis] = pl.ds(pl.multiple_of(chip * size, size), size)
    return ref.at[tuple(idx)]


def all_gather_chips(shards, axes, name):
    n = len(shards)

    def body(*refs):
        ins, outs = refs[:n], refs[n:2 * n]
        send, recv, loc = refs[2 * n:]
        x, y, c = lax.axis_index("x"), lax.axis_index("y"), lax.axis_index("c")
        me = 2 * x + y
        chips = _other_chips(x, y)
        started = []
        for i in range(n):
            size = ins[i].shape[axes[i]]
            cp = pltpu.make_async_copy(ins[i], _shard_slice(outs[i], axes[i], me, size), loc.at[i])
            cp.start()
            started.append(cp)
        sends = []
        for i in range(n):
            size = ins[i].shape[axes[i]]
            for j, (px, py) in enumerate(chips):
                cp = pltpu.make_async_remote_copy(
                    src_ref=ins[i], dst_ref=_shard_slice(outs[i], axes[i], me, size),
                    send_sem=send.at[i, j], recv_sem=recv.at[i, j], device_id=(px, py, c), device_id_type=MESH_ID)
                cp.start()
                sends.append(cp)
        for i in range(n):
            size = ins[i].shape[axes[i]]
            for j, (px, py) in enumerate(chips):
                pltpu.make_async_remote_copy(
                    src_ref=ins[i], dst_ref=_shard_slice(outs[i], axes[i], 2 * px + py, size),
                    send_sem=send.at[i, j], recv_sem=recv.at[i, j], device_id=(px, py, c),
                    device_id_type=MESH_ID).wait_recv()
        for cp in sends:
            cp.wait_send()
        for cp in started:
            cp.wait()

    def full(a, ax):
        shape = list(a.shape)
        shape[ax] *= N_CHIPS
        return jax.ShapeDtypeStruct(tuple(shape), a.dtype)

    return pl.pallas_call(
        body, name=name, in_specs=[ANY] * n, out_specs=[ANY] * n,
        out_shape=[full(a, ax) for a, ax in zip(shards, axes)],
        scratch_shapes=[pltpu.SemaphoreType.DMA((n, 3)), pltpu.SemaphoreType.DMA((n, 3)),
                        pltpu.SemaphoreType.DMA((n,))],
        compiler_params=pltpu.CompilerParams(has_side_effects=True))(*shards)


def cast_into_slot(shard, axis, chip, name):
    L, K, N = shard.shape
    bm = _pick(K, 512, 16)
    full = (K * N_CHIPS, N) if axis == 1 else (K, N * N_CHIPS)
    nb = K // bm

    def body(c_ref, s_ref, *o_refs):
        for l in range(L):
            o_refs[l][...] = s_ref[l].astype(BF16)

    out_map = (lambda i, c: (c[0] * nb + i, 0)) if axis == 1 else (lambda i, c: (i, c[0]))
    spec = pltpu.PrefetchScalarGridSpec(
        num_scalar_prefetch=1, grid=(nb,),
        in_specs=[pl.BlockSpec((L, bm, N), lambda i, c: (0, i, 0))],
        out_specs=[pl.BlockSpec((bm, N), out_map)] * L)
    return pl.pallas_call(body, name=name, grid_spec=spec, out_shape=[jax.ShapeDtypeStruct(full, BF16)] * L,
                          compiler_params=_params("parallel"))(chip, shard)


HBM = pl.BlockSpec(memory_space=pltpu.HBM)
SEM = pl.BlockSpec(memory_space=pltpu.SEMAPHORE)
DATAFLOW = pltpu.SideEffectType.DATAFLOW_SIDE_EFFECTING


def split_start(name, sources, landing, n_sems, make, after):
    ns, nl, na = len(sources), len(landing), len(after)

    def body(*refs):
        out, _ = make(refs[:ns], refs[ns:ns + nl], refs[ns + nl + na], refs[ns + nl + na + 1])
        for cp in out:
            cp.start()
        refs[-1][...] = jnp.zeros_like(refs[-1])

    hbm = lambda b: pltpu.with_memory_space_constraint(b, pltpu.HBM)
    res = pl.pallas_call(
        body, name=name,
        out_shape=(pltpu.SemaphoreType.DMA((n_sems,)), pltpu.SemaphoreType.DMA((n_sems,)),
                   *[pltpu.HBM(b.shape, b.dtype) for b in landing], jax.ShapeDtypeStruct((8, 128), F32)),
        in_specs=[HBM] * (ns + nl) + [ANY] * na, out_specs=(SEM, SEM, *[HBM] * nl, pl.BlockSpec(memory_space=pltpu.VMEM)),
        input_output_aliases={ns + i: 2 + i for i in range(nl)},
        compiler_params=pltpu.CompilerParams(has_side_effects=DATAFLOW))(
            *[hbm(b) for b in sources], *[hbm(b) for b in landing], *after)
    return res[0], res[1], list(res[2:2 + nl]), res[-1]


def split_wait(name, sources, landing, send, recv, make, after):
    ns, nl = len(sources), len(landing)

    def body(*refs):
        _, back = make(refs[:ns], refs[ns:ns + nl], refs[ns + nl], refs[ns + nl + 1])
        for cp in back:
            cp.wait_send()
            cp.wait_recv()

    return list(pl.pallas_call(
        body, name=name, out_shape=tuple(pltpu.HBM(b.shape, b.dtype) for b in landing),
        in_specs=[HBM] * (ns + nl) + [SEM, SEM] + [ANY] * len(after), out_specs=tuple([HBM] * nl),
        input_output_aliases={ns + i: i for i in range(nl)},
        compiler_params=pltpu.CompilerParams(has_side_effects=DATAFLOW))(
            *[pltpu.with_memory_space_constraint(b, pltpu.HBM) for b in sources], *landing, send, recv, *after))


def _half_slot(buf, axis, chip, half):
    K, N = buf.shape
    if axis == 1:
        n = N // N_CHIPS
        return buf.at[pl.ds(pl.multiple_of(half * (K // 2), K // 2), K // 2), pl.ds(pl.multiple_of(chip * n, n), n)]
    k2 = K // N_CHIPS // 2
    return buf.at[pl.ds(pl.multiple_of((2 * chip + half) * k2, k2), k2), :]


def gather_copies(axes):
    def make(_, bufs, send, recv):
        x, y, c = lax.axis_index("x"), lax.axis_index("y"), lax.axis_index("c")
        out, back = [], []
        for i, (buf, ax) in enumerate(zip(bufs, axes)):
            mine = _half_slot(buf, ax, 2 * x + y, c)
            for j, (px, py) in enumerate(_other_chips(x, y)):
                kw = dict(send_sem=send.at[3 * i + j], recv_sem=recv.at[3 * i + j], device_id=(px, py, c),
                          device_id_type=MESH_ID)
                out.append(pltpu.make_async_remote_copy(src_ref=mine, dst_ref=mine, **kw))
                back.append(pltpu.make_async_remote_copy(src_ref=mine, dst_ref=_half_slot(buf, ax, 2 * px + py, c), **kw))
        return out, back
    return make


def forward_sibling(bufs, axes, name):
    n = len(bufs)

    def body(*refs):
        ins = refs[:n]
        send, recv = refs[2 * n:]
        x, y, c = lax.axis_index("x"), lax.axis_index("y"), lax.axis_index("c")
        out, back = [], []
        for i, ax in enumerate(axes):
            for j, (px, py) in enumerate(_other_chips(x, y)):
                have = _half_slot(ins[i], ax, 2 * px + py, c)
                kw = dict(send_sem=send.at[3 * i + j], recv_sem=recv.at[3 * i + j], device_id=(x, y, 1 - c),
                          device_id_type=MESH_ID)
                out.append(pltpu.make_async_remote_copy(src_ref=have, dst_ref=have, **kw))
                back.append(pltpu.make_async_remote_copy(src_ref=have, dst_ref=_half_slot(ins[i], ax, 2 * px + py, 1 - c), **kw))
        for cp in out:
            cp.start()
        for cp in back:
            cp.wait_recv()
        for cp in out:
            cp.wait_send()

    return pl.pallas_call(
        body, name=name, in_specs=[ANY] * n, out_specs=[ANY] * n,
        out_shape=[jax.ShapeDtypeStruct(b.shape, b.dtype) for b in bufs],
        input_output_aliases={i: i for i in range(n)},
        scratch_shapes=[pltpu.SemaphoreType.DMA((3 * n,)), pltpu.SemaphoreType.DMA((3 * n,))],
        compiler_params=pltpu.CompilerParams(has_side_effects=True))(*bufs)


def all_reduce_small(p, name):
    R = p.shape[0]

    def body(p_ref, o_ref, sib_ref, chip_ref, send, recv):
        x, y, c = lax.axis_index("x"), lax.axis_index("y"), lax.axis_index("c")
        me = 2 * x + y
        chips = _other_chips(x, y)
        pair = pltpu.make_async_remote_copy(src_ref=p_ref, dst_ref=sib_ref, send_sem=send.at[0], recv_sem=recv.at[0],
                                            device_id=(x, y, 1 - c), device_id_type=MESH_ID)
        pair.start()
        pair.wait()
        chip_ref[me] = p_ref[...] + sib_ref[...]
        sends = []
        for j, (px, py) in enumerate(chips):
            cp = pltpu.make_async_remote_copy(src_ref=chip_ref.at[me], dst_ref=chip_ref.at[me], send_sem=send.at[1 + j],
                                              recv_sem=recv.at[1 + j], device_id=(px, py, c), device_id_type=MESH_ID)
            cp.start()
            sends.append(cp)
        for j, (px, py) in enumerate(chips):
            pltpu.make_async_remote_copy(src_ref=chip_ref.at[me], dst_ref=chip_ref.at[2 * px + py], send_sem=send.at[1 + j],
                                         recv_sem=recv.at[1 + j], device_id=(px, py, c), device_id_type=MESH_ID).wait_recv()
        for cp in sends:
            cp.wait_send()
        o_ref[...] = ((chip_ref[0] + chip_ref[1]) + chip_ref[2]) + chip_ref[3]

    vm = pl.BlockSpec(memory_space=pltpu.VMEM)
    return pl.pallas_call(
        body, name=name, in_specs=[vm], out_specs=vm, out_shape=jax.ShapeDtypeStruct((R, 128), F32),
        scratch_shapes=[pltpu.VMEM((R, 128), F32), pltpu.VMEM((N_CHIPS, R, 128), F32),
                        pltpu.SemaphoreType.DMA((4,)), pltpu.SemaphoreType.DMA((4,))],
        compiler_params=pltpu.CompilerParams(has_side_effects=True, vmem_limit_bytes=VMEM_LIMIT))(p)


def _grad_view(g, axis):
    K, N = g.shape
    return g.reshape(1, 2, K // 2, N) if axis == 1 else g.reshape(N_CHIPS, 2, K // N_CHIPS // 2, N)


def pair_copies(gvs, others, send, recv):
    x, y, c = lax.axis_index("x"), lax.axis_index("y"), lax.axis_index("c")
    out = [pltpu.make_async_remote_copy(src_ref=gv.at[:, 1 - c], dst_ref=o, send_sem=send.at[i], recv_sem=recv.at[i],
                                        device_id=(x, y, 1 - c), device_id_type=MESH_ID)
           for i, (gv, o) in enumerate(zip(gvs, others))]
    return out, out


def chip_copies(axes):
    def piece(s, ax, chip):
        if ax == 1:
            n = s.shape[2] // N_CHIPS
            return s.at[0, :, pl.ds(pl.multiple_of(chip * n, n), n)]
        return s.at[chip]

    def make(sums, qs, send, recv):
        x, y, c = lax.axis_index("x"), lax.axis_index("y"), lax.axis_index("c")
        out = []
        for i, (s, q, ax) in enumerate(zip(sums, qs, axes)):
            for j, (px, py) in enumerate(_other_chips(x, y)):
                out.append(pltpu.make_async_remote_copy(
                    src_ref=piece(s, ax, 2 * px + py), dst_ref=q.at[j], send_sem=send.at[3 * i + j],
                    recv_sem=recv.at[3 * i + j], device_id=(px, py, c), device_id_type=MESH_ID))
        return out, out
    return make


def share_sibling(halves, name):
    n = len(halves)

    def body(*refs):
        ins = refs[:n]
        send, recv = refs[2 * n:]
        x, y, c = lax.axis_index("x"), lax.axis_index("y"), lax.axis_index("c")
        cps = [pltpu.make_async_remote_copy(src_ref=ins[i].at[c], dst_ref=ins[i].at[c], send_sem=send.at[i], recv_sem=recv.at[i],
                                            device_id=(x, y, 1 - c), device_id_type=MESH_ID) for i in range(n)]
        for cp in cps:
            cp.start()
        for i in range(n):
            pltpu.make_async_remote_copy(src_ref=ins[i].at[c], dst_ref=ins[i].at[1 - c], send_sem=send.at[i], recv_sem=recv.at[i],
                                         device_id=(x, y, 1 - c), device_id_type=MESH_ID).wait_recv()
        for cp in cps:
            cp.wait_send()

    return pl.pallas_call(
        body, name=name, in_specs=[ANY] * n, out_specs=[ANY] * n,
        out_shape=[jax.ShapeDtypeStruct(a.shape, a.dtype) for a in halves],
        input_output_aliases={i: i for i in range(n)},
        scratch_shapes=[pltpu.SemaphoreType.DMA((n,)), pltpu.SemaphoreType.DMA((n,))],
        compiler_params=pltpu.CompilerParams(has_side_effects=True))(*halves)


def add_pair(gv, other, place, name):
    A, _, rows, N = gv.shape
    bm, bn = _pick(rows, 512, 16), _pick(N, 1408, 128)

    def body(p_ref, g_ref, o_ref, out_ref):
        out_ref[...] = (g_ref[...] + o_ref[...]).astype(GRAD_WIRE)

    spec = pltpu.PrefetchScalarGridSpec(
        num_scalar_prefetch=1, grid=(A, rows // bm, N // bn),
        in_specs=[pl.BlockSpec((None, None, bm, bn), lambda a, i, j, p: (a, p[1], i, j)),
                  pl.BlockSpec((None, bm, bn), lambda a, i, j, p: (a, i, j))],
        out_specs=pl.BlockSpec((None, bm, bn), lambda a, i, j, p: (a, i, j)))
    return pl.pallas_call(body, name=name, grid_spec=spec, out_shape=jax.ShapeDtypeStruct((A, rows, N), GRAD_WIRE),
                          compiler_params=_params("parallel", "parallel", "parallel"))(place, gv, other)


def add_chips(s, q, axis, place, name):
    _, rows, n = q.shape
    bm, bn = _pick(rows, 512, 16), _pick(n, 1408, 128)
    nbj = n // bn

    def body(p_ref, s_ref, q_ref, o_ref):
        o_ref[...] = ((s_ref[...].astype(F32) + q_ref[0].astype(F32)) + q_ref[1].astype(F32)) + q_ref[2].astype(F32)

    mine = (lambda i, j, p: (p[0], i, j)) if axis == 0 else (lambda i, j, p: (0, i, p[0] * nbj + j))
    spec = pltpu.PrefetchScalarGridSpec(
        num_scalar_prefetch=1, grid=(rows // bm, nbj),
        in_specs=[pl.BlockSpec((None, bm, bn), mine), pl.BlockSpec((3, bm, bn), lambda i, j, p: (0, i, j))],
        out_specs=pl.BlockSpec((None, bm, bn), lambda i, j, p: (p[1], i, j)))
    return pl.pallas_call(body, name=name, grid_spec=spec, out_shape=jax.ShapeDtypeStruct((2, rows, n), F32),
                          compiler_params=_params("parallel", "parallel"))(place, s, q)


def adamw(w, g, m, v, name):
    R, N = w.shape
    bm = _pick(R, 512, 8)
    c1 = 1.0 / (1.0 - ADAM_B1 ** ADAM_STEP)
    c2 = 1.0 / (1.0 - ADAM_B2 ** ADAM_STEP)

    def body(w_ref, g_ref, m_ref, v_ref, d_ref, nm_ref, nv_ref):
        gv = g_ref[...]
        nm = ADAM_B1 * m_ref[...] + (1.0 - ADAM_B1) * gv
        nv = ADAM_B2 * v_ref[...] + (1.0 - ADAM_B2) * (gv * gv)
        nm_ref[...] = nm
        nv_ref[...] = nv
        d_ref[...] = -ADAM_LR * ((nm * c1) / (jnp.sqrt(nv * c2) + ADAM_EPS) + ADAM_WD * w_ref[...])

    blk = pl.BlockSpec((bm, N), lambda i: (i, 0))
    out = jax.ShapeDtypeStruct((R, N), F32)
    return pl.pallas_call(body, name=name, grid=(R // bm,), in_specs=[blk] * 4, out_specs=[blk] * 3,
                          out_shape=[out, out, out], compiler_params=_params("parallel"))(w, g, m, v)


def adamw_layers(w, g0, g1, m, v, name):
    _, k, n = w.shape
    bm = _pick(k, 256, 8)
    c1 = 1.0 / (1.0 - ADAM_B1 ** ADAM_STEP)
    c2 = 1.0 / (1.0 - ADAM_B2 ** ADAM_STEP)

    def body(w_ref, g0_ref, g1_ref, m_ref, v_ref, g_ref, d_ref, nm_ref, nv_ref):
        def step(gv):
            nm = ADAM_B1 * m_ref[...] + (1.0 - ADAM_B1) * gv
            nv = ADAM_B2 * v_ref[...] + (1.0 - ADAM_B2) * (gv * gv)
            g_ref[...] = gv
            nm_ref[...] = nm
            nv_ref[...] = nv
            d_ref[...] = -ADAM_LR * ((nm * c1) / (jnp.sqrt(nv * c2) + ADAM_EPS) + ADAM_WD * w_ref[...])

        @pl.when(pl.program_id(0) == 0)
        def _():
            step(g0_ref[...])

        @pl.when(pl.program_id(0) == 1)
        def _():
            step(g1_ref[...])

    blk = pl.BlockSpec((None, bm, n), lambda l, i: (l, i, 0))
    out = jax.ShapeDtypeStruct(w.shape, F32)
    return pl.pallas_call(
        body, name=name, grid=(2, k // bm),
        in_specs=[blk, pl.BlockSpec((bm, n), lambda l, i: (i * (1 - l), 0)), pl.BlockSpec((bm, n), lambda l, i: (i * l, 0)),
                  blk, blk],
        out_specs=[blk] * 4, out_shape=[out] * 4, compiler_params=_params("arbitrary", "arbitrary"))(w, g0, g1, m, v)


def _pack(arrays):
    flat = jnp.concatenate([a.reshape(-1) for a in arrays])
    rows = -(-flat.shape[0] // (256 * 128)) * 256
    return jnp.pad(flat, (0, rows * 128 - flat.shape[0])).reshape(rows, 128)


def _unpack(p, shapes):
    flat, out, at = p.reshape(-1), [], 0
    for s in shapes:
        n = math.prod(s)
        out.append(flat[at:at + n].reshape(s))
        at += n
    return out


def kernel(x, mem, norm_ffn1, ffn1_w_in, ffn1_w_out, norm_mix, mix_w_in, sconv_w, sgu_norm_g, sgu_w, sgu_b, cconv_w, cconv_ln_g, cconv_ln_b, pool_w, pool_scale, mix_w_out, norm_xattn, norm_mem, xattn_wq, xattn_wkv, xattn_wo, norm_ffn2, ffn2_w_in, ffn2_w_out, norm_final, loss_target, m_norm_ffn1, m_ffn1_w_in, m_ffn1_w_out, m_norm_mix, m_mix_w_in, m_sconv_w, m_sgu_norm_g, m_sgu_w, m_sgu_b, m_cconv_w, m_cconv_ln_g, m_cconv_ln_b, m_pool_w, m_pool_scale, m_mix_w_out, m_norm_xattn, m_norm_mem, m_xattn_wq, m_xattn_wkv, m_xattn_wo, m_norm_ffn2, m_ffn2_w_in, m_ffn2_w_out, m_norm_final, v_norm_ffn1, v_ffn1_w_in, v_ffn1_w_out, v_norm_mix, v_mix_w_in, v_sconv_w, v_sgu_norm_g, v_sgu_w, v_sgu_b, v_cconv_w, v_cconv_ln_g, v_cconv_ln_b, v_pool_w, v_pool_scale, v_mix_w_out, v_norm_xattn, v_norm_mem, v_xattn_wq, v_xattn_wkv, v_xattn_wo, v_norm_ffn2, v_ffn2_w_in, v_ffn2_w_out, v_norm_final):
    given = dict(locals())
    w = {n: given[n] for n in WEIGHTS}
    L = ffn1_w_in.shape[0]
    assert L == 2, "the reduce-scatter gives one layer to each core of a chip"
    chip = 2 * lax.axis_index("x") + lax.axis_index("y")
    chip1 = chip.astype(jnp.int32).reshape(1)
    core = lax.axis_index("c").astype(jnp.int32).reshape(1)
    place = jnp.concatenate([chip1, core])

    axis = {n: 1 if n in COL_SHARDED else 0 for n in BIG}
    bufs = {}
    for n in BIG:
        for l, b in enumerate(cast_into_slot(w[n], axis[n] + 1, chip1, "cast_weights")):
            bufs[n, l] = b
    groups = {"a": [(n, 0) for n in BIG[:2]], "b": [(n, 0) for n in BIG[2:4]], "c": [(n, 0) for n in BIG[4:7]],
              "d": [(n, 0) for n in BIG[7:]], "e": [(n, 1) for n in BIG]}
    wc = sconv_w.shape[-1]
    conv_rows = [w[n].reshape(-1, wc) for n in SMALL_CONV]
    n_conv = sum(r.shape[0] for r in conv_rows)
    conv_pack = jnp.pad(jnp.concatenate(conv_rows, axis=0), ((0, -n_conv % 8), (0, 128 - wc)))[None]
    conv_all = all_gather_chips([conv_pack], [0], "gather_conv")[0]
    started, token = {}, conv_all
    for g, keys in groups.items():
        send, recv, thru, token = split_start("gather_start_" + g, [], [bufs[k] for k in keys], 3 * len(keys),
                                              gather_copies([axis[k[0]] for k in keys]), [token])
        started[g] = (send, recv, thru)
    ready = {}

    def fetch(n, l, after):
        g = next(g for g, keys in groups.items() if (n, l) in keys)
        if g not in ready:
            send, recv, thru = started[g]
            axes = [axis[k[0]] for k in groups[g]]
            done = split_wait("gather_wait_" + g, [], thru, send, recv, gather_copies(axes), [token if g == "a" else after])
            ready[g] = dict(zip(groups[g], forward_sibling(done, axes, "gather_forward")))
        return ready[g][n, l]

    conv_full = jnp.moveaxis(conv_all[:, :n_conv, :wc], 0, 1).reshape(n_conv, N_CHIPS * wc)
    ws = {n: w[n] for n in SMALL_REPL}
    at = 0
    for n in SMALL_CONV:
        rows = w[n].shape[0] * w[n].shape[1]
        ws[n] = conv_full[at:at + rows].reshape(w[n].shape[0], w[n].shape[1], N_CHIPS * wc)
        at += rows

    halves, state = {}, {}
    reduce_groups = {"r1": [(n, 1) for n in BIG], "r0a": [(n, 0) for n in BIG[2:]], "r0b": [(n, 0) for n in BIG[:2]]}
    plan = {("layer", 1): [("pair", "r1")],
            ("ffn2", 0): [("chips", "r1")],
            ("mix", 0): [("finish", "r1"), ("pair", "r0a")],
            ("ffn1_mid", 0): [("chips", "r0a")],
            ("ffn1_grads", 0): [("pair", "r0b")],
            ("layer", 0): [("finish", "r0a")]}


    def stage_pair(g, keys, grads, after):
        gvs = [_grad_view(grads[n][l], axis[n]) for n, l in keys]
        others = [lax.empty(gv.shape[:1] + gv.shape[2:], F32) for gv in gvs]
        send, recv, others, token = split_start("pair_start_" + g, gvs, others, len(gvs), pair_copies, [after])
        state[g] = dict(sources=gvs, send=send, recv=recv, landing=others, token=token)
        return [(state[g], "token")]

    def stage_chips(g, keys, grads, after):
        st = state[g]
        axes = [axis[n] for n, _ in keys]
        others = split_wait("pair_wait_" + g, st["sources"], st["landing"], st["send"], st["recv"], pair_copies,
                            [after, st["token"]])
        sums = [add_pair(gv, o, place, "add_pair") for gv, o in zip(st["sources"], others)]
        qs = [lax.empty((3, s.shape[1], s.shape[2] // (N_CHIPS if ax == 1 else 1)), GRAD_WIRE) for s, ax in zip(sums, axes)]
        send, recv, qs, token = split_start("chips_start_" + g, sums, qs, 3 * len(sums), chip_copies(axes), [after])
        state[g] = dict(sources=sums, send=send, recv=recv, landing=qs, token=token)
        return [(state[g], "token")]

    def stage_finish(g, keys, grads, after):
        st = state.pop(g)
        qs = split_wait("chips_wait_" + g, st["sources"], st["landing"], st["send"], st["recv"],
                        chip_copies([axis[n] for n, _ in keys]), [after, st["token"]])
        for key, s, q in zip(keys, st["sources"], qs):
            halves[key] = add_chips(s, q, axis[key[0]], place, "add_chips")
        return [(halves, key) for key in keys]

    stages = {"pair": stage_pair, "chips": stage_chips, "finish": stage_finish}

    def progress(event, l, grads, values):
        places = []
        for stage, g in plan.get((event, l), []):
            places += stages[stage](g, reduce_groups[g], grads, values[0])
        if places:
            values, tied = lax.optimization_barrier((values, [box[k] for box, k in places]))
            for (box, k), a in zip(places, tied):
                box[k] = a
        return values

    loss_part, grad_x, grads = _local_step(x[0], mem[0], loss_target[0], fetch, ws, L, progress)
    loss = lax.psum(loss_part, ("x", "y", "c"))

    small = SMALL_REPL + SMALL_CONV
    small_g = [grads[n] if n == "norm_final" else jnp.stack(grads[n]) for n in small]
    small_sum = all_reduce_small(_pack(small_g), "reduce_small")
    grad = dict(zip(small, _unpack(small_sum, [g.shape for g in small_g])))
    for n in SMALL_CONV:
        grad[n] = lax.dynamic_slice_in_dim(grad[n], chip * wc, wc, axis=2)

    delta, new_m, new_v = {}, {}, {}

    def finish_weights(names, ready):
        keys = [(n, l) for n in names for l in range(L)]
        shard_grad = dict(zip(keys, share_sibling(ready, "share_pair")))
        for n in names:
            g0, g1 = (shard_grad[n, l].reshape(w[n].shape[1:]) for l in range(L))
            grad[n], delta[n], new_m[n], new_v[n] = adamw_layers(w[n], g0, g1, given["m_" + n], given["v_" + n], "adamw")

    stage_chips("r0b", reduce_groups["r0b"], grads, small_sum)
    ready, (state["r0b"]["token"],) = lax.optimization_barrier(
        ([halves[n, l] for n in BIG[2:] for l in range(L)], [state["r0b"]["token"]]))
    finish_weights(BIG[2:], ready)
    stage_finish("r0b", reduce_groups["r0b"], grads, delta[BIG[-1]])
    finish_weights(BIG[:2], [halves[n, l] for n in BIG[:2] for l in range(L)])
    shapes = [w[n].shape for n in small]
    packed = [_pack([src[n] for n in small]) for src in
              (w, grad, {n: given["m_" + n] for n in small}, {n: given["v_" + n] for n in small})]
    for out, p in zip((delta, new_m, new_v), adamw(*packed, "adamw_small")):
        out.update(zip(small, _unpack(p, shapes)))

    return (loss, grad_x[None], *[grad[n] for n in WEIGHTS], *[delta[n] for n in WEIGHTS],
            *[new_m[n] for n in WEIGHTS], *[new_v[n] for n in WEIGHTS])
```

```python
import functools
import math

import jax
import jax.numpy as jnp
from jax import lax
from jax.experimental import pallas as pl
from jax.experimental.pallas import tpu as pltpu

F32 = jnp.float32
BF16 = jnp.bfloat16
EPS = 1e-6
SEQ_CHUNK = 128
POOL_WINDOWS = (2, 4, 8, 16)
N_HEADS = 4
ADAM_LR, ADAM_B1, ADAM_B2, ADAM_EPS, ADAM_WD, ADAM_STEP = 0.001, 0.9, 0.999, 1e-08, 0.01, 10
VMEM_LIMIT = 56 * 1024 * 1024
MESH_ID = pl.DeviceIdType.MESH
N_CHIPS = 4
GRAD_WIRE = BF16

BIG = ("ffn1_w_in", "ffn1_w_out", "mix_w_in", "mix_w_out", "xattn_wq", "xattn_wkv", "xattn_wo",
       "ffn2_w_in", "ffn2_w_out")
COL_SHARDED = ("ffn1_w_in", "mix_w_in", "xattn_wkv", "ffn2_w_in")
SMALL_CONV = ("sconv_w", "cconv_w")
SMALL_REPL = ("norm_ffn1", "norm_mix", "sgu_norm_g", "sgu_w", "sgu_b", "cconv_ln_g", "cconv_ln_b",
              "pool_w", "pool_scale", "norm_xattn", "norm_mem", "norm_ffn2", "norm_final")
WEIGHTS = ("norm_ffn1", "ffn1_w_in", "ffn1_w_out", "norm_mix", "mix_w_in", "sconv_w", "sgu_norm_g",
           "sgu_w", "sgu_b", "cconv_w", "cconv_ln_g", "cconv_ln_b", "pool_w", "pool_scale",
           "mix_w_out", "norm_xattn", "norm_mem", "xattn_wq", "xattn_wkv", "xattn_wo", "norm_ffn2",
           "ffn2_w_in", "ffn2_w_out", "norm_final")


def _pick(n, pref, align):
    best = None
    for d in range(align, min(n, pref) + 1, align):
        if n % d == 0:
            best = d
    return best or n


def _sig(x):
    return 0.5 * jnp.tanh(0.5 * x) + 0.5


def _nt(a, b):
    return lax.dot_general(a, b, (((1,), (1,)), ((), ())), preferred_element_type=F32)


def _tn(a, b):
    return lax.dot_general(a, b, (((0,), (0,)), ((), ())), preferred_element_type=F32)


def _params(*sem):
    return pltpu.CompilerParams(dimension_semantics=sem, vmem_limit_bytes=VMEM_LIMIT)


def norm_matmul(x, g, w, name, out_dtype=F32):
    T, D = x.shape
    N = w.shape[1]
    tm, tn = _pick(T, 512, 8), _pick(N, 2048, 128)

    def body(x_ref, g_ref, w_ref, o_ref, h_ref):
        j = pl.program_id(1)

        @pl.when(j == 0)
        def _():
            xv = x_ref[...]
            r = lax.rsqrt(jnp.mean(xv * xv, axis=-1, keepdims=True) + EPS)
            h_ref[...] = (xv * r * g_ref[...]).astype(BF16)

        o_ref[...] = jnp.dot(h_ref[...], w_ref[...], preferred_element_type=F32).astype(out_dtype)

    return pl.pallas_call(
        body, name=name, grid=(T // tm, N // tn),
        in_specs=[pl.BlockSpec((tm, D), lambda i, j: (i, 0)), pl.BlockSpec((1, D), lambda i, j: (0, 0)),
                  pl.BlockSpec((D, tn), lambda i, j: (0, j))],
        out_specs=[pl.BlockSpec((tm, tn), lambda i, j: (i, j)), pl.BlockSpec((tm, D), lambda i, j: (i, 0))],
        out_shape=[jax.ShapeDtypeStruct((T, N), out_dtype), jax.ShapeDtypeStruct((T, D), BF16)],
        compiler_params=_params("parallel", "arbitrary"))(x, g, w)


def matmul_res(res, a, w, name):
    T, K = a.shape
    N = w.shape[1]
    tm, tn = _pick(T, 512, 8), _pick(N, 1024, 128)

    def body(r_ref, a_ref, w_ref, o_ref):
        o_ref[...] = r_ref[...] + jnp.dot(a_ref[...].astype(BF16), w_ref[...], preferred_element_type=F32)

    return pl.pallas_call(
        body, name=name, grid=(T // tm, N // tn),
        in_specs=[pl.BlockSpec((tm, tn), lambda i, j: (i, j)), pl.BlockSpec((tm, K), lambda i, j: (i, 0)),
                  pl.BlockSpec((K, tn), lambda i, j: (0, j))],
        out_specs=pl.BlockSpec((tm, tn), lambda i, j: (i, j)),
        out_shape=jax.ShapeDtypeStruct((T, N), F32),
        compiler_params=_params("parallel", "parallel"))(res, a, w)


def matmul_nt(a, w, name, out_dtype=F32):
    T, N = a.shape
    M = w.shape[0]
    tm, tmm = _pick(T, 512, 8), _pick(M, 1024, 128)

    def body(a_ref, w_ref, o_ref):
        o_ref[...] = _nt(a_ref[...].astype(BF16), w_ref[...]).astype(out_dtype)

    return pl.pallas_call(
        body, name=name, grid=(T // tm, M // tmm),
        in_specs=[pl.BlockSpec((tm, N), lambda i, j: (i, 0)), pl.BlockSpec((tmm, N), lambda i, j: (j, 0))],
        out_specs=pl.BlockSpec((tm, tmm), lambda i, j: (i, j)),
        out_shape=jax.ShapeDtypeStruct((T, M), out_dtype),
        compiler_params=_params("parallel", "parallel"))(a, w)


def matmul_tn(a, b, scale, name, b2=None):
    T, M = a.shape
    Nb = b.shape[1]
    bm, bn, bk = _pick(M, 1408, 128), _pick(Nb, 1408, 128), _pick(T, 512, 8)
    nk, nj = T // bk, Nb // bn

    def body(a_ref, b_ref, *rest):
        o_ref = rest[-1]
        j, k = pl.program_id(1), pl.program_id(2)

        @pl.when(k == 0)
        def _():
            o_ref[...] = jnp.zeros_like(o_ref)

        a_blk = a_ref[...].astype(BF16)
        if b2 is None:
            o_ref[...] += _tn(a_blk, b_ref[...].astype(BF16))
        else:
            @pl.when(j < nj)
            def _():
                o_ref[...] += _tn(a_blk, b_ref[...].astype(BF16))

            @pl.when(j >= nj)
            def _():
                o_ref[...] += _tn(a_blk, rest[0][...].astype(BF16))

        if scale != 1.0:
            @pl.when(k == nk - 1)
            def _():
                o_ref[...] = o_ref[...] * scale

    if b2 is None:
        b_specs, operands, n_out = [pl.BlockSpec((bk, bn), lambda i, j, k: (k, j))], (a, b), nj
    else:
        first = lambda i, j, k: (jnp.where(j < nj, k, 0), jnp.where(j < nj, j, 0))
        second = lambda i, j, k: (jnp.where(j >= nj, k, 0), jnp.where(j >= nj, j - nj, 0))
        b_specs, operands, n_out = [pl.BlockSpec((bk, bn), first), pl.BlockSpec((bk, bn), second)], (a, b, b2), 2 * nj
    return pl.pallas_call(
        body, name=name, grid=(M // bm, n_out, nk),
        in_specs=[pl.BlockSpec((bk, bm), lambda i, j, k: (k, i))] + b_specs,
        out_specs=pl.BlockSpec((bm, bn), lambda i, j, k: (i, j)),
        out_shape=jax.ShapeDtypeStruct((M, n_out * bn), F32),
        compiler_params=_params("parallel", "parallel", "arbitrary"))(*operands)


def rmsnorm_bwd(dxo, dh, x, g, name):
    T, D = x.shape
    tm = _pick(T, 512, 8)
    has_res = dxo is not None

    def body(*refs):
        if has_res:
            dxo_ref, dh_ref, x_ref, g_ref, dx_ref, dg_ref = refs
        else:
            dh_ref, x_ref, g_ref, dx_ref, dg_ref = refs
        i = pl.program_id(0)

        @pl.when(i == 0)
        def _():
            dg_ref[...] = jnp.zeros_like(dg_ref)

        xv, dh_v = x_ref[...], dh_ref[...]
        r = lax.rsqrt(jnp.mean(xv * xv, axis=-1, keepdims=True) + EPS)
        xh = xv * r
        dg_ref[...] += jnp.sum(dh_v * xh, axis=0, keepdims=True)
        dxh = dh_v * g_ref[...]
        dx = r * (dxh - xh * jnp.mean(dxh * xh, axis=-1, keepdims=True))
        dx_ref[...] = dx + dxo_ref[...] if has_res else dx

    tile = pl.BlockSpec((tm, D), lambda i: (i, 0))
    vec = pl.BlockSpec((1, D), lambda i: (0, 0))
    args = ([dxo] if has_res else []) + [dh, x, g]
    return pl.pallas_call(
        body, name=name, grid=(T // tm,),
        in_specs=[tile] * (len(args) - 1) + [vec],
        out_specs=[tile, vec],
        out_shape=[jax.ShapeDtypeStruct((T, D), F32), jax.ShapeDtypeStruct((1, D), F32)],
        compiler_params=_params("arbitrary"))(*args)


def ffn_fwd(x, g, w_in, w_out, name):
    T, D = x.shape
    F = w_out.shape[0]
    tm, tf = _pick(T, 512, 8), _pick(F, 1408, 128)
    nf = F // tf

    def body(x_ref, g_ref, wg_ref, wu_ref, wo_ref, o_ref, h_ref, zg_ref, zu_ref, acc_ref):
        j = pl.program_id(1)

        @pl.when(j == 0)
        def _():
            xv = x_ref[...]
            r = lax.rsqrt(jnp.mean(xv * xv, axis=-1, keepdims=True) + EPS)
            h_ref[...] = (xv * r * g_ref[...]).astype(BF16)
            acc_ref[...] = jnp.zeros_like(acc_ref)

        h = h_ref[...]
        zg = jnp.dot(h, wg_ref[...], preferred_element_type=F32)
        zu = jnp.dot(h, wu_ref[...], preferred_element_type=F32)
        zg_ref[...] = zg.astype(BF16)
        zu_ref[...] = zu.astype(BF16)
        a = (zg * _sig(zg) * zu).astype(BF16)
        acc_ref[...] += jnp.dot(a, wo_ref[...], preferred_element_type=F32)

        @pl.when(j == nf - 1)
        def _():
            o_ref[...] = x_ref[...] + 0.5 * acc_ref[...]

    tile = pl.BlockSpec((tm, D), lambda i, j: (i, 0))
    fblk = pl.BlockSpec((tm, tf), lambda i, j: (i, j))
    hidden = jax.ShapeDtypeStruct((T, F), BF16)
    return pl.pallas_call(
        body, name=name, grid=(T // tm, nf),
        in_specs=[tile, pl.BlockSpec((1, D), lambda i, j: (0, 0)),
                  pl.BlockSpec((D, tf), lambda i, j: (0, j)), pl.BlockSpec((D, tf), lambda i, j: (0, j + nf)),
                  pl.BlockSpec((tf, D), lambda i, j: (j, 0))],
        out_specs=[tile, tile, fblk, fblk],
        out_shape=[jax.ShapeDtypeStruct((T, D), F32), jax.ShapeDtypeStruct((T, D), BF16), hidden, hidden],
        scratch_shapes=[pltpu.VMEM((tm, D), F32)],
        compiler_params=_params("parallel", "arbitrary"))(x, g, w_in, w_in, w_out)


def ffn_dz(dxo, zg, zu, w_out, name):
    T, D = dxo.shape
    F = w_out.shape[0]
    tm, tf = _pick(T, 512, 8), _pick(F, 256, 128)

    def body(dxo_ref, zg_ref, zu_ref, wo_ref, a_ref, dzg_ref, dzu_ref):
        do = (0.5 * dxo_ref[...]).astype(BF16)
        for j in range(F // tf):
            cols = slice(j * tf, (j + 1) * tf)
            zg, zu = zg_ref[:, cols].astype(F32), zu_ref[:, cols].astype(F32)
            s = _sig(zg)
            silu = zg * s
            a_ref[:, cols] = (silu * zu).astype(BF16)
            da = _nt(do, wo_ref[cols, :])
            dzu_ref[:, cols] = (da * silu).astype(BF16)
            dzg_ref[:, cols] = (da * zu * (s + silu * (1.0 - s))).astype(BF16)

    rows = pl.BlockSpec((tm, F), lambda i: (i, 0))
    hidden = jax.ShapeDtypeStruct((T, F), BF16)
    return pl.pallas_call(
        body, name=name, grid=(T // tm,),
        in_specs=[pl.BlockSpec((tm, D), lambda i: (i, 0)), rows, rows, pl.BlockSpec((F, D), lambda i: (0, 0))],
        out_specs=[rows, rows, rows], out_shape=[hidden, hidden, hidden],
        compiler_params=_params("parallel"))(dxo, zg, zu, w_out)


def dh_norm_bwd(x, dxo, g, parts, w, name):
    T, D = x.shape
    F = parts[0].shape[1]
    n = len(parts)
    tm = _pick(T, 256, 8)

    def body(x_ref, dxo_ref, g_ref, *refs):
        a_refs, w_refs, (dx_ref, dg_ref) = refs[:n], refs[n:2 * n], refs[2 * n:]
        i = pl.program_id(0)

        @pl.when(i == 0)
        def _():
            dg_ref[...] = jnp.zeros_like(dg_ref)

        dh = sum(_nt(a_ref[...], w_ref[...]) for a_ref, w_ref in zip(a_refs, w_refs))
        xv = x_ref[...]
        r = lax.rsqrt(jnp.mean(xv * xv, axis=-1, keepdims=True) + EPS)
        xh = xv * r
        dg_ref[...] += jnp.sum(dh * xh, axis=0, keepdims=True)
        dxh = dh * g_ref[...]
        dx_ref[...] = dxo_ref[...] + r * (dxh - xh * jnp.mean(dxh * xh, axis=-1, keepdims=True))

    tile = pl.BlockSpec((tm, D), lambda i: (i, 0))
    vec = pl.BlockSpec((1, D), lambda i: (0, 0))
    return pl.pallas_call(
        body, name=name, grid=(T // tm,),
        in_specs=[tile, tile, vec] + [pl.BlockSpec((tm, F), lambda i: (i, 0))] * n
                 + [pl.BlockSpec((D, F), lambda i, p=p: (0, p)) for p in range(n)],
        out_specs=[tile, vec],
        out_shape=[jax.ShapeDtypeStruct((T, D), F32), jax.ShapeDtypeStruct((1, D), F32)],
        compiler_params=_params("arbitrary"))(x, dxo, g, *parts, *([w] * n))


def _chunks(T, fn):
    def step(c, carry):
        fn(pl.multiple_of(c * SEQ_CHUNK, SEQ_CHUNK))
        return carry
    lax.fori_loop(0, T // SEQ_CHUNK, step, 0)


def _conv_taps(win, ktaps, pad):
    return [(win if k == ktaps - 1 else pltpu.roll(win, ktaps - 1 - k, 0))[pad:, :] for k in range(ktaps)]


def _conv_taps_t(win, ktaps, pad):
    n = win.shape[0]
    return [(win if k == ktaps - 1 else pltpu.roll(win, n - (ktaps - 1 - k), 0))[:n - pad, :] for k in range(ktaps)]


def _col(T, W, idx):
    return pl.BlockSpec((T, W), lambda i, idx=idx: (0, idx))


def _full(shape):
    return pl.BlockSpec(shape, lambda i: (0,) * len(shape))


def mix_a_fwd(z, w, name):
    T, W = z.shape[0], w.shape[1]
    K, P = w.shape[0], 8

    def body(ab_ref, ac_ref, ax_ref, w_ref, y_ref, pp_ref):
        pp_ref[0:P, :] = jnp.zeros((P, W), F32)

        def chunk(s):
            rows = pl.ds(s, SEQ_CHUNK)
            pp_ref[pl.ds(s + P, SEQ_CHUNK), :] = ac_ref[rows, :] * ax_ref[rows, :]
            taps = _conv_taps(pp_ref[pl.ds(s, SEQ_CHUNK + P), :], K, P)
            q = sum(w_ref[k:k + 1, :] * taps[k] for k in range(K))
            y_ref[rows, :] = (ab_ref[rows, :] * q).astype(BF16)

        _chunks(T, chunk)

    return pl.pallas_call(
        body, name=name, grid=(1,),
        in_specs=[_col(T, W, 0), _col(T, W, 1), _col(T, W, 2), _full((K, W))],
        out_specs=_full((T, W)), out_shape=jax.ShapeDtypeStruct((T, W), BF16),
        scratch_shapes=[pltpu.VMEM((T + P, W), F32)],
        compiler_params=_params("arbitrary"))(z, z, z, w)


def mix_a_bwd(z, dy, w, name):
    T, W = z.shape[0], w.shape[1]
    K, P = w.shape[0], 8

    def body(ab_ref, ac_ref, ax_ref, dy_ref, w_ref, dab_ref, dac_ref, dax_ref, dw_ref, pp_ref, dq_ref):
        pp_ref[0:P, :] = jnp.zeros((P, W), F32)
        dq_ref[T:T + P, :] = jnp.zeros((P, W), F32)
        dw_ref[...] = jnp.zeros_like(dw_ref)

        def chunk1(s):
            rows = pl.ds(s, SEQ_CHUNK)
            pp_ref[pl.ds(s + P, SEQ_CHUNK), :] = ac_ref[rows, :] * ax_ref[rows, :]
            taps = _conv_taps(pp_ref[pl.ds(s, SEQ_CHUNK + P), :], K, P)
            q = sum(w_ref[k:k + 1, :] * taps[k] for k in range(K))
            dyv = dy_ref[rows, :]
            dab_ref[rows, :] = (dyv * q).astype(BF16)
            dq = dyv * ab_ref[rows, :]
            dq_ref[rows, :] = dq
            for k in range(K):
                dw_ref[k:k + 1, :] += jnp.sum(dq * taps[k], axis=0, keepdims=True)

        _chunks(T, chunk1)

        def chunk2(s):
            rows = pl.ds(s, SEQ_CHUNK)
            taps = _conv_taps_t(dq_ref[pl.ds(s, SEQ_CHUNK + P), :], K, P)
            dp = sum(w_ref[k:k + 1, :] * taps[k] for k in range(K))
            dac_ref[rows, :] = (dp * ax_ref[rows, :]).astype(BF16)
            dax_ref[rows, :] = (dp * ac_ref[rows, :]).astype(BF16)

        _chunks(T, chunk2)

    tw = jax.ShapeDtypeStruct((T, W), BF16)
    return pl.pallas_call(
        body, name=name, grid=(1,),
        in_specs=[_col(T, W, 0), _col(T, W, 1), _col(T, W, 2), _col(T, W, 0), _full((K, W))],
        out_specs=[_full((T, W))] * 3 + [_full((K, W))],
        out_shape=[tw, tw, tw, jax.ShapeDtypeStruct((K, W), F32)],
        scratch_shapes=[pltpu.VMEM((T + P, W), F32), pltpu.VMEM((T + P, W), F32)],
        compiler_params=_params("arbitrary"))(z, z, z, dy, w)


def _ln_stats(v):
    mu = jnp.mean(v, axis=-1, keepdims=True)
    xc = v - mu
    rstd = lax.rsqrt(jnp.mean(xc * xc, axis=-1, keepdims=True) + EPS)
    return xc * rstd, rstd


def _ln_bwd(dxh, xh, rstd):
    return rstd * (dxh - jnp.mean(dxh, axis=-1, keepdims=True) - xh * jnp.mean(dxh * xh, axis=-1, keepdims=True))


def _tril_bf16(w_ref, h):
    n = w_ref.shape[-1]
    keep = lax.broadcasted_iota(jnp.int32, (n, n), 0) >= lax.broadcasted_iota(jnp.int32, (n, n), 1)
    return jnp.where(keep, w_ref[h], 0.0).astype(BF16)


def mix_b_fwd(z, g, w_s, bias, name):
    T, W = z.shape[0], g.shape[1]
    H, C = w_s.shape[0], w_s.shape[1]
    hd = W // H

    def body(u_ref, v_ref, g_ref, w_ref, b_ref, y_ref):
        wts = [_tril_bf16(w_ref, h) for h in range(H)]
        head = lax.broadcasted_iota(jnp.int32, (C, W), 1) // hd

        def chunk(s):
            rows = pl.ds(s, C)
            xh, _ = _ln_stats(v_ref[rows, :])
            vn = (xh * g_ref[...]).astype(BF16)
            mixed = b_ref[...]
            for h in range(H):
                mixed = mixed + jnp.where(head == h, jnp.dot(wts[h], vn, preferred_element_type=F32), 0.0)
            y_ref[rows, :] = (u_ref[rows, :] * mixed).astype(BF16)

        _chunks(T, chunk)

    return pl.pallas_call(
        body, name=name, grid=(1,),
        in_specs=[_col(T, W, 3), _col(T, W, 4), _full((1, W)), _full((H, C, C)), _full((C, W))],
        out_specs=_full((T, W)), out_shape=jax.ShapeDtypeStruct((T, W), BF16),
        compiler_params=_params("arbitrary"))(z, z, g, w_s, bias)


def mix_b_bwd(z, dy, g, w_s, bias, name):
    T, W = z.shape[0], g.shape[1]
    H, C = w_s.shape[0], w_s.shape[1]
    hd = W // H

    def body(u_ref, v_ref, dy_ref, g_ref, w_ref, b_ref, du_ref, dv_ref, dw_ref, db_ref, dg_ref, dbf_ref):
        wts = [_tril_bf16(w_ref, h) for h in range(H)]
        head = lax.broadcasted_iota(jnp.int32, (C, W), 1) // hd
        dw_ref[...] = jnp.zeros_like(dw_ref)
        dg_ref[...] = jnp.zeros_like(dg_ref)
        dbf_ref[...] = jnp.zeros_like(dbf_ref)

        def chunk(s):
            rows = pl.ds(s, C)
            xh, rstd = _ln_stats(v_ref[rows, :])
            vn = (xh * g_ref[...]).astype(BF16)
            mixed = b_ref[...]
            for h in range(H):
                mixed = mixed + jnp.where(head == h, jnp.dot(wts[h], vn, preferred_element_type=F32), 0.0)
            dyv = dy_ref[rows, :]
            du_ref[rows, :] = (dyv * mixed).astype(BF16)
            dm = dyv * u_ref[rows, :]
            dbf_ref[...] += dm
            dvn = jnp.zeros((C, W), F32)
            for h in range(H):
                dmh = jnp.where(head == h, dm, 0.0).astype(BF16)
                dw_ref[h] += _nt(dmh, vn)
                dvn = dvn + _tn(wts[h], dmh)
            dg_ref[...] += jnp.sum(dvn * xh, axis=0, keepdims=True)
            dv_ref[rows, :] = _ln_bwd(dvn * g_ref[...], xh, rstd).astype(BF16)

        _chunks(T, chunk)

        keep = lax.broadcasted_iota(jnp.int32, (C, C), 0) >= lax.broadcasted_iota(jnp.int32, (C, C), 1)
        lane = lax.broadcasted_iota(jnp.int32, (C, 128), 1)
        db = jnp.zeros((C, 128), F32)
        dbf = dbf_ref[...]
        for h in range(H):
            dw_ref[h] = jnp.where(keep, dw_ref[h], 0.0)
            db = db + jnp.where(lane == h, jnp.sum(jnp.where(head == h, dbf, 0.0), axis=1, keepdims=True), 0.0)
        db_ref[...] = db

    tw = jax.ShapeDtypeStruct((T, W), BF16)
    return pl.pallas_call(
        body, name=name, grid=(1,),
        in_specs=[_col(T, W, 3), _col(T, W, 4), _col(T, W, 1), _full((1, W)), _full((H, C, C)), _full((C, W))],
        out_specs=[_full((T, W)), _full((T, W)), _full((H, C, C)), _full((C, 128)), _full((1, W))],
        out_shape=[tw, tw, jax.ShapeDtypeStruct((H, C, C), F32), jax.ShapeDtypeStruct((C, 128), F32),
                   jax.ShapeDtypeStruct((1, W), F32)],
        scratch_shapes=[pltpu.VMEM((C, W), F32)],
        compiler_params=_params("arbitrary"))(z, z, dy, g, w_s, bias)


def mix_c_fwd(z, w, ln_g, ln_b, name):
    T, W = z.shape[0], w.shape[1]
    K, P = w.shape[0], 32

    def body(a_ref, gt_ref, w_ref, g_ref, b_ref, y_ref, up_ref):
        up_ref[0:P, :] = jnp.zeros((P, W), F32)

        def chunk(s):
            rows = pl.ds(s, SEQ_CHUNK)
            up_ref[pl.ds(s + P, SEQ_CHUNK), :] = a_ref[rows, :] * _sig(gt_ref[rows, :])
            taps = _conv_taps(up_ref[pl.ds(s, SEQ_CHUNK + P), :], K, P)
            q = sum(w_ref[k:k + 1, :] * taps[k] for k in range(K))
            xh, _ = _ln_stats(q)
            r = xh * g_ref[...] + b_ref[...]
            y_ref[rows, :] = (r * _sig(r)).astype(BF16)

        _chunks(T, chunk)

    return pl.pallas_call(
        body, name=name, grid=(1,),
        in_specs=[_col(T, W, 5), _col(T, W, 6), _full((K, W)), _full((1, W)), _full((1, W))],
        out_specs=_full((T, W)), out_shape=jax.ShapeDtypeStruct((T, W), BF16),
        scratch_shapes=[pltpu.VMEM((T + P, W), F32)],
        compiler_params=_params("arbitrary"))(z, z, w, ln_g, ln_b)


def mix_c_bwd(z, dy, w, ln_g, ln_b, name):
    T, W = z.shape[0], w.shape[1]
    K, P = w.shape[0], 32

    def body(a_ref, gt_ref, dy_ref, w_ref, g_ref, b_ref, da_ref, dgt_ref, dw_ref, dg_ref, db_ref, up_ref, dq_ref):
        up_ref[0:P, :] = jnp.zeros((P, W), F32)
        dq_ref[T:T + P, :] = jnp.zeros((P, W), F32)
        dw_ref[...] = jnp.zeros_like(dw_ref)
        dg_ref[...] = jnp.zeros_like(dg_ref)
        db_ref[...] = jnp.zeros_like(db_ref)

        def chunk1(s):
            rows = pl.ds(s, SEQ_CHUNK)
            up_ref[pl.ds(s + P, SEQ_CHUNK), :] = a_ref[rows, :] * _sig(gt_ref[rows, :])
            taps = _conv_taps(up_ref[pl.ds(s, SEQ_CHUNK + P), :], K, P)
            q = sum(w_ref[k:k + 1, :] * taps[k] for k in range(K))
            xh, rstd = _ln_stats(q)
            r = xh * g_ref[...] + b_ref[...]
            sr = _sig(r)
            dr = dy_ref[rows, :] * (sr * (1.0 + r * (1.0 - sr)))
            db_ref[...] += jnp.sum(dr, axis=0, keepdims=True)
            dg_ref[...] += jnp.sum(dr * xh, axis=0, keepdims=True)
            dq = _ln_bwd(dr * g_ref[...], xh, rstd)
            dq_ref[rows, :] = dq
            for k in range(K):
                dw_ref[k:k + 1, :] += jnp.sum(dq * taps[k], axis=0, keepdims=True)

        _chunks(T, chunk1)

        def chunk2(s):
            rows = pl.ds(s, SEQ_CHUNK)
            taps = _conv_taps_t(dq_ref[pl.ds(s, SEQ_CHUNK + P), :], K, P)
            du = sum(w_ref[k:k + 1, :] * taps[k] for k in range(K))
            sg = _sig(gt_ref[rows, :])
            da_ref[rows, :] = (du * sg).astype(BF16)
            dgt_ref[rows, :] = (du * a_ref[rows, :] * sg * (1.0 - sg)).astype(BF16)

        _chunks(T, chunk2)

    tw = jax.ShapeDtypeStruct((T, W), BF16)
    vec = jax.ShapeDtypeStruct((1, W), F32)
    return pl.pallas_call(
        body, name=name, grid=(1,),
        in_specs=[_col(T, W, 5), _col(T, W, 6), _col(T, W, 2), _full((K, W)), _full((1, W)), _full((1, W))],
        out_specs=[_full((T, W)), _full((T, W)), _full((K, W)), _full((1, W)), _full((1, W))],
        out_shape=[tw, tw, jax.ShapeDtypeStruct((K, W), F32), vec, vec],
        scratch_shapes=[pltpu.VMEM((T + P, W), F32), pltpu.VMEM((T + P, W), F32)],
        compiler_params=_params("arbitrary"))(z, z, dy, w, ln_g, ln_b)


def _pool_select(levels, W, rows):
    group = lax.broadcasted_iota(jnp.int32, (rows, W), 1) // (W // len(POOL_WINDOWS))
    out = levels[-1]
    for gi in range(len(POOL_WINDOWS) - 2, -1, -1):
        out = jnp.where(group == gi, levels[gi], out)
    return out


def _pool_count(s, W):
    t = s + lax.broadcasted_iota(jnp.int32, (SEQ_CHUNK, W), 0)
    group = lax.broadcasted_iota(jnp.int32, (SEQ_CHUNK, W), 1) // (W // len(POOL_WINDOWS))
    win = jnp.full((SEQ_CHUNK, W), POOL_WINDOWS[-1], jnp.int32)
    for gi in range(len(POOL_WINDOWS) - 2, -1, -1):
        win = jnp.where(group == gi, POOL_WINDOWS[gi], win)
    return jnp.minimum(t + 1, win).astype(F32)


def _pooled(wp_ref, s, W, P):
    win = wp_ref[pl.ds(s, SEQ_CHUNK + P), :]
    levels, acc, shift = [], win, 1
    for _ in POOL_WINDOWS:
        acc = acc + pltpu.roll(acc, shift, 0)
        levels.append(acc[P:, :])
        shift *= 2
    return _pool_select(levels, W, SEQ_CHUNK) / _pool_count(s, W) - win[P:, :]


def mix_d_fwd(z, pbd, scale, name):
    T, W = z.shape[0], scale.shape[1]
    P = 16

    def body(x_ref, p_ref, s_ref, y_ref, wp_ref):
        wp_ref[0:P, :] = jnp.zeros((P, W), F32)

        def chunk(s):
            rows = pl.ds(s, SEQ_CHUNK)
            wp_ref[pl.ds(s + P, SEQ_CHUNK), :] = x_ref[rows, :]
            pooled = _pooled(wp_ref, s, W, P).astype(BF16)
            y_ref[rows, :] = (jnp.dot(pooled, p_ref[...], preferred_element_type=F32) * s_ref[...]).astype(BF16)

        _chunks(T, chunk)

    return pl.pallas_call(
        body, name=name, grid=(1,),
        in_specs=[_col(T, W, 7), _full((W, W)), _full((1, W))],
        out_specs=_full((T, W)), out_shape=jax.ShapeDtypeStruct((T, W), BF16),
        scratch_shapes=[pltpu.VMEM((T + P, W), F32)],
        compiler_params=_params("arbitrary"))(z, pbd, scale)


def mix_d_bwd(z, dy, pbd, scale, name):
    T, W = z.shape[0], scale.shape[1]
    P = 16

    def body(x_ref, dy_ref, p_ref, s_ref, dx_ref, dp_ref, ds_ref, wp_ref, e_ref, dpool_ref):
        wp_ref[0:P, :] = jnp.zeros((P, W), F32)
        e_ref[T:T + P, :] = jnp.zeros((P, W), F32)
        dp_ref[...] = jnp.zeros_like(dp_ref)
        ds_ref[...] = jnp.zeros_like(ds_ref)

        def chunk1(s):
            rows = pl.ds(s, SEQ_CHUNK)
            wp_ref[pl.ds(s + P, SEQ_CHUNK), :] = x_ref[rows, :]
            pooled = _pooled(wp_ref, s, W, P).astype(BF16)
            yl = jnp.dot(pooled, p_ref[...], preferred_element_type=F32)
            dyv = dy_ref[rows, :]
            ds_ref[...] += jnp.sum(dyv * yl, axis=0, keepdims=True)
            dyl = (dyv * s_ref[...]).astype(BF16)
            dp_ref[...] += _tn(pooled, dyl)
            dpool = _nt(dyl, p_ref[...])
            dpool_ref[rows, :] = dpool
            e_ref[rows, :] = dpool / _pool_count(s, W)

        _chunks(T, chunk1)

        def chunk2(s):
            rows = pl.ds(s, SEQ_CHUNK)
            win = e_ref[pl.ds(s, SEQ_CHUNK + P), :]
            n = SEQ_CHUNK + P
            levels, acc, shift = [], win, 1
            for _ in POOL_WINDOWS:
                acc = acc + pltpu.roll(acc, n - shift, 0)
                levels.append(acc[:SEQ_CHUNK, :])
                shift *= 2
            dx_ref[rows, :] = (_pool_select(levels, W, SEQ_CHUNK) - dpool_ref[rows, :]).astype(BF16)

        _chunks(T, chunk2)

    return pl.pallas_call(
        body, name=name, grid=(1,),
        in_specs=[_col(T, W, 7), _col(T, W, 3), _full((W, W)), _full((1, W))],
        out_specs=[_full((T, W)), _full((W, W)), _full((1, W))],
        out_shape=[jax.ShapeDtypeStruct((T, W), BF16), jax.ShapeDtypeStruct((W, W), F32),
                   jax.ShapeDtypeStruct((1, W), F32)],
        scratch_shapes=[pltpu.VMEM((T + P, W), F32), pltpu.VMEM((T + P, W), F32), pltpu.VMEM((T, W), F32)],
        compiler_params=_params("arbitrary"))(z, dy, pbd, scale)


def attn_fwd(q, kv, name):
    T, D = q.shape
    M = kv.shape[0]
    hd = D // N_HEADS
    tm = _pick(T, 512, 8)
    sc = 1.0 / math.sqrt(hd)

    def body(q_ref, k_ref, v_ref, o_ref):
        for h in range(N_HEADS):
            cols = slice(h * hd, (h + 1) * hd)
            s = _nt(q_ref[:, cols].astype(BF16), k_ref[:, cols].astype(BF16)) * sc
            p = jnp.exp(s - jnp.max(s, axis=-1, keepdims=True))
            p = p / jnp.sum(p, axis=-1, keepdims=True)
            o_ref[:, cols] = jnp.dot(p.astype(BF16), v_ref[:, cols].astype(BF16),
                                     preferred_element_type=F32).astype(BF16)

    return pl.pallas_call(
        body, name=name, grid=(T // tm,),
        in_specs=[pl.BlockSpec((tm, D), lambda i: (i, 0)), pl.BlockSpec((M, D), lambda i: (0, 0)),
                  pl.BlockSpec((M, D), lambda i: (0, 1))],
        out_specs=pl.BlockSpec((tm, D), lambda i: (i, 0)),
        out_shape=jax.ShapeDtypeStruct((T, D), BF16),
        compiler_params=_params("parallel"))(q, kv, kv)


def attn_bwd(q, kv, do, name):
    T, D = q.shape
    M = kv.shape[0]
    hd = D // N_HEADS
    tm = _pick(T, 512, 8)
    sc = 1.0 / math.sqrt(hd)

    def body(q_ref, k_ref, v_ref, do_ref, dq_ref, dk_ref, dv_ref):
        i = pl.program_id(0)

        @pl.when(i == 0)
        def _():
            dk_ref[...] = jnp.zeros_like(dk_ref)
            dv_ref[...] = jnp.zeros_like(dv_ref)

        for h in range(N_HEADS):
            cols = slice(h * hd, (h + 1) * hd)
            qh, kh = q_ref[:, cols].astype(BF16), k_ref[:, cols].astype(BF16)
            vh, doh = v_ref[:, cols].astype(BF16), do_ref[:, cols].astype(BF16)
            s = _nt(qh, kh) * sc
            p = jnp.exp(s - jnp.max(s, axis=-1, keepdims=True))
            p = p / jnp.sum(p, axis=-1, keepdims=True)
            dp = _nt(doh, vh)
            dv_ref[:, cols] += _tn(p.astype(BF16), doh)
            ds = (p * (dp - jnp.sum(dp * p, axis=-1, keepdims=True)) * sc).astype(BF16)
            dq_ref[:, cols] = jnp.dot(ds, kh, preferred_element_type=F32).astype(BF16)
            dk_ref[:, cols] += _tn(ds, qh)

    tile = pl.BlockSpec((tm, D), lambda i: (i, 0))
    mem = jax.ShapeDtypeStruct((M, D), F32)
    return pl.pallas_call(
        body, name=name, grid=(T // tm,),
        in_specs=[tile, pl.BlockSpec((M, D), lambda i: (0, 0)), pl.BlockSpec((M, D), lambda i: (0, 1)), tile],
        out_specs=[tile, pl.BlockSpec((M, D), lambda i: (0, 0)), pl.BlockSpec((M, D), lambda i: (0, 0))],
        out_shape=[jax.ShapeDtypeStruct((T, D), BF16), mem, mem],
        compiler_params=_params("arbitrary"))(q, kv, kv, do)


def loss_head(x, g, target, name):
    T, D = x.shape
    tm = _pick(T, 512, 8)

    def body(x_ref, g_ref, t_ref, l_ref, dx_ref, dg_ref):
        i = pl.program_id(0)

        @pl.when(i == 0)
        def _():
            l_ref[...] = jnp.zeros_like(l_ref)
            dg_ref[...] = jnp.zeros_like(dg_ref)

        xv = x_ref[...]
        r = lax.rsqrt(jnp.mean(xv * xv, axis=-1, keepdims=True) + EPS)
        xh = xv * r
        err = xh * g_ref[...] - t_ref[...]
        l_ref[...] += 0.5 * jnp.sum(jnp.mean(err * err, axis=-1, keepdims=True), axis=0, keepdims=True)
        dy = err * (1.0 / D)
        dg_ref[...] += jnp.sum(dy * xh, axis=0, keepdims=True)
        dxh = dy * g_ref[...]
        dx_ref[...] = r * (dxh - xh * jnp.mean(dxh * xh, axis=-1, keepdims=True))

    tile = pl.BlockSpec((tm, D), lambda i: (i, 0))
    vec = pl.BlockSpec((1, D), lambda i: (0, 0))
    return pl.pallas_call(
        body, name=name, grid=(T // tm,),
        in_specs=[tile, vec, tile],
        out_specs=[pl.BlockSpec((1, 128), lambda i: (0, 0)), tile, vec],
        out_shape=[jax.ShapeDtypeStruct((1, 128), F32), jax.ShapeDtypeStruct((T, D), F32),
                   jax.ShapeDtypeStruct((1, D), F32)],
        compiler_params=_params("arbitrary"))(x, g, target)


def _block_diag(p):
    G, gd, _ = p.shape
    rows = [jnp.concatenate([p[g] if g == c else jnp.zeros((gd, gd), p.dtype) for c in range(G)], axis=1)
            for g in range(G)]
    return jnp.concatenate(rows, axis=0)


class _LazyWeight:
    def __init__(self, fetch, name, latest):
        self.fetch, self.name, self.latest = fetch, name, latest

    def __getitem__(self, l):
        return self.fetch(self.name, l, self.latest[0])


def _local_step(x, mem, target, fetch, ws, L, progress=lambda event, l, grads, values: values):
    T, D = x.shape
    W = D // 4
    H = ws["sgu_w"].shape[1]
    row = lambda v: v.reshape(1, -1)
    latest = [x]
    wb = {n: _LazyWeight(fetch, n, latest) for n in BIG}
    saved = []
    for l in range(L):
        s = {"x0": x}
        latest[0] = x
        x, *s["ffn1"] = ffn_fwd(x, row(ws["norm_ffn1"][l]), wb["ffn1_w_in"][l], wb["ffn1_w_out"][l], "ffn_fwd")
        s["x1"] = x
        latest[0] = x
        z, s["h_mix"] = norm_matmul(x, row(ws["norm_mix"][l]), wb["mix_w_in"][l], "mix_in")
        s["z"] = z
        s["bias"] = jnp.repeat(ws["sgu_b"][l].T, W // H, axis=1)
        s["pbd"] = _block_diag(ws["pool_w"][l]).astype(BF16)
        y = jnp.concatenate([
            mix_a_fwd(z, ws["sconv_w"][l], "mix_a_fwd"),
            mix_b_fwd(z, row(ws["sgu_norm_g"][l]), ws["sgu_w"][l], s["bias"], "mix_b_fwd"),
            mix_c_fwd(z, ws["cconv_w"][l], row(ws["cconv_ln_g"][l]), row(ws["cconv_ln_b"][l]), "mix_c_fwd"),
            mix_d_fwd(z, s["pbd"], row(ws["pool_scale"][l]), "mix_d_fwd")], axis=1)
        s["y"] = y
        x = matmul_res(x, y, wb["mix_w_out"][l], "mix_out")
        s["x2"] = x
        latest[0] = x
        s["q"], s["hq"] = norm_matmul(x, row(ws["norm_xattn"][l]), wb["xattn_wq"][l], "attn_q", out_dtype=BF16)
        s["kv"], s["mn"] = norm_matmul(mem, row(ws["norm_mem"][l]), wb["xattn_wkv"][l], "attn_kv")
        s["o"] = attn_fwd(s["q"], s["kv"], "attn_fwd")
        x = matmul_res(x, s["o"], wb["xattn_wo"][l], "attn_out")
        s["x3"] = x
        latest[0] = x
        x, *s["ffn2"] = ffn_fwd(x, row(ws["norm_ffn2"][l]), wb["ffn2_w_in"][l], wb["ffn2_w_out"][l], "ffn_fwd")
        saved.append(s)

    loss, dx, dg_final = loss_head(x, row(ws["norm_final"]), target, "loss_head")
    grads = {n: [None] * L for n in WEIGHTS if n != "norm_final"}
    grads["norm_final"] = dg_final.reshape(-1)

    def pin(dx, names, l):
        dx, made = lax.optimization_barrier((dx, [grads[n][l] for n in names]))
        for n, g in zip(names, made):
            grads[n][l] = g
        return dx

    def after_stages(event, l, values):
        return progress(event, l, grads, values)

    def ffn_back(xin, kept, dxo, gname, win, wout, l, event):
        h, zg, zu = kept
        a, dzg, dzu = ffn_dz(dxo, zg, zu, wb[wout][l], "ffn_dz")
        last = l == 0 and event == "ffn1_mid"
        if not last:
            dxn, dg = dh_norm_bwd(xin, dxo, row(ws[gname][l]), [dzg, dzu], wb[win][l], "ffn_dh")
            dxn, h, a = after_stages(event, l, (dxn, h, a))
        else:
            h, a = after_stages(event, l, (h, a))
        grads[win][l] = matmul_tn(h, dzg, 1.0, "ffn_dwin", b2=dzu)
        grads[wout][l] = matmul_tn(a, dxo, 0.5, "ffn_dwout")
        if last:
            dzg, dzu = after_stages("ffn1_grads", l, (dzg, dzu))
            dxn, dg = dh_norm_bwd(xin, dxo, row(ws[gname][l]), [dzg, dzu], wb[win][l], "ffn_dh")
        grads[gname][l] = dg.reshape(-1)
        return dxn if last else pin(dxn, (win, wout), l)

    for l in reversed(range(L)):
        s = saved[l]
        dx = ffn_back(s["x3"], s["ffn2"], dx, "norm_ffn2", "ffn2_w_in", "ffn2_w_out", l, "ffn2_mid")
        dx, = after_stages("ffn2", l, (dx,))
        grads["xattn_wo"][l] = matmul_tn(s["o"], dx, 1.0, "dw_sq")
        do = matmul_nt(dx, wb["xattn_wo"][l], "attn_do", out_dtype=BF16)
        dq, dk, dv = attn_bwd(s["q"], s["kv"], do, "attn_bwd")
        grads["xattn_wq"][l] = matmul_tn(s["hq"], dq, 1.0, "dw_sq")
        dx, dg = dh_norm_bwd(s["x2"], dx, row(ws["norm_xattn"][l]), [dq], wb["xattn_wq"][l], "attn_dh")
        grads["norm_xattn"][l] = dg.reshape(-1)
        dkv = jnp.concatenate([dk, dv], axis=1)
        grads["xattn_wkv"][l] = matmul_tn(s["mn"], dkv, 1.0, "attn_dwkv")
        dmn = matmul_nt(dkv, wb["xattn_wkv"][l], "attn_dmn")
        _, dg = rmsnorm_bwd(None, dmn, mem, row(ws["norm_mem"][l]), "norm_mem_bwd")
        grads["norm_mem"][l] = dg.reshape(-1)
        dx = pin(dx, ("xattn_wo", "xattn_wq", "xattn_wkv", "norm_mem"), l)
        dx, = after_stages("attn", l, (dx,))
        grads["mix_w_out"][l] = matmul_tn(s["y"], dx, 1.0, "dw_sq")
        dy = matmul_nt(dx, wb["mix_w_out"][l], "mix_dy")
        z = s["z"]
        dab, dac, dax, dws = mix_a_bwd(z, dy, ws["sconv_w"][l], "mix_a_bwd")
        dbu, dbv, dwsgu, dbs, dgs = mix_b_bwd(z, dy, row(ws["sgu_norm_g"][l]), ws["sgu_w"][l], s["bias"], "mix_b_bwd")
        dca, dcg, dwc, dgc, dbc = mix_c_bwd(z, dy, ws["cconv_w"][l], row(ws["cconv_ln_g"][l]),
                                            row(ws["cconv_ln_b"][l]), "mix_c_bwd")
        ddw, dpbd, dsc = mix_d_bwd(z, dy, s["pbd"], row(ws["pool_scale"][l]), "mix_d_bwd")
        grads["sconv_w"][l], grads["cconv_w"][l] = dws, dwc
        grads["sgu_w"][l], grads["sgu_b"][l], grads["sgu_norm_g"][l] = dwsgu, dbs[:, :H].T, dgs.reshape(-1)
        grads["cconv_ln_g"][l], grads["cconv_ln_b"][l] = dgc.reshape(-1), dbc.reshape(-1)
        gd = W // len(POOL_WINDOWS)
        grads["pool_w"][l] = jnp.stack([dpbd[g * gd:(g + 1) * gd, g * gd:(g + 1) * gd] for g in range(len(POOL_WINDOWS))])
        grads["pool_scale"][l] = dsc.reshape(-1)
        dz = jnp.concatenate([dab, dac, dax, dbu, dbv, dca, dcg, ddw], axis=1)
        grads["mix_w_in"][l] = matmul_tn(s["h_mix"], dz, 1.0, "mix_dwin")
        dx, dg = dh_norm_bwd(s["x1"], dx, row(ws["norm_mix"][l]), [dz], wb["mix_w_in"][l], "mix_dh")
        grads["norm_mix"][l] = dg.reshape(-1)
        dx = pin(dx, ("mix_w_out", "mix_w_in"), l)
        dx, = after_stages("mix", l, (dx,))
        dx = ffn_back(s["x0"], s["ffn1"], dx, "norm_ffn1", "ffn1_w_in", "ffn1_w_out", l, "ffn1_mid")
        dx, = after_stages("layer", l, (dx,))

    return loss[0, 0], dx, grads


ANY = pl.BlockSpec(memory_space=pl.ANY)


def _other_chips(x, y):
    return [(1 - x, y), (x, 1 - y), (1 - x, 1 - y)]


def _shard_slice(ref, axis, chip, size):
    idx = [slice(None)] * len(ref.shape)
    idx[axis] = pl.ds(pl.multiple_of(chip * size, size), size)
    return ref.at[tuple(idx)]


def all_gather_chips(shards, axes, name):
    n = len(shards)

    def body(*refs):
        ins, outs = refs[:n], refs[n:2 * n]
        send, recv, loc = refs[2 * n:]
        x, y, c = lax.axis_index("x"), lax.axis_index("y"), lax.axis_index("c")
        me = 2 * x + y
        chips = _other_chips(x, y)
        started = []
        for i in range(n):
            size = ins[i].shape[axes[i]]
            cp = pltpu.make_async_copy(ins[i], _shard_slice(outs[i], axes[i], me, size), loc.at[i])
            cp.start()
            started.append(cp)
        sends = []
        for i in range(n):
            size = ins[i].shape[axes[i]]
            for j, (px, py) in enumerate(chips):
                cp = pltpu.make_async_remote_copy(
                    src_ref=ins[i], dst_ref=_shard_slice(outs[i], axes[i], me, size),
                    send_sem=send.at[i, j], recv_sem=recv.at[i, j], device_id=(px, py, c), device_id_type=MESH_ID)
                cp.start()
                sends.append(cp)
        for i in range(n):
            size = ins[i].shape[axes[i]]
            for j, (px, py) in enumerate(chips):
                pltpu.make_async_remote_copy(
                    src_ref=ins[i], dst_ref=_shard_slice(outs[i], axes[i], 2 * px + py, size),
                    send_sem=send.at[i, j], recv_sem=recv.at[i, j], device_id=(px, py, c),
                    device_id_type=MESH_ID).wait_recv()
        for cp in sends:
            cp.wait_send()
        for cp in started:
            cp.wait()

    def full(a, ax):
        shape = list(a.shape)
        shape[ax] *= N_CHIPS
        return jax.ShapeDtypeStruct(tuple(shape), a.dtype)

    return pl.pallas_call(
        body, name=name, in_specs=[ANY] * n, out_specs=[ANY] * n,
        out_shape=[full(a, ax) for a, ax in zip(shards, axes)],
        scratch_shapes=[pltpu.SemaphoreType.DMA((n, 3)), pltpu.SemaphoreType.DMA((n, 3)),
                        pltpu.SemaphoreType.DMA((n,))],
        compiler_params=pltpu.CompilerParams(has_side_effects=True))(*shards)


def cast_into_slot(shard, axis, chip, name):
    L, K, N = shard.shape
    bm = _pick(K, 512, 16)
    full = (K * N_CHIPS, N) if axis == 1 else (K, N * N_CHIPS)
    nb = K // bm

    def body(c_ref, s_ref, *o_refs):
        for l in range(L):
            o_refs[l][...] = s_ref[l].astype(BF16)

    out_map = (lambda i, c: (c[0] * nb + i, 0)) if axis == 1 else (lambda i, c: (i, c[0]))
    spec = pltpu.PrefetchScalarGridSpec(
        num_scalar_prefetch=1, grid=(nb,),
        in_specs=[pl.BlockSpec((L, bm, N), lambda i, c: (0, i, 0))],
        out_specs=[pl.BlockSpec((bm, N), out_map)] * L)
    return pl.pallas_call(body, name=name, grid_spec=spec, out_shape=[jax.ShapeDtypeStruct(full, BF16)] * L,
                          compiler_params=_params("parallel"))(chip, shard)


HBM = pl.BlockSpec(memory_space=pltpu.HBM)
SEM = pl.BlockSpec(memory_space=pltpu.SEMAPHORE)
DATAFLOW = pltpu.SideEffectType.DATAFLOW_SIDE_EFFECTING


def split_start(name, sources, landing, n_sems, make, after):
    ns, nl, na = len(sources), len(landing), len(after)

    def body(*refs):
        out, _ = make(refs[:ns], refs[ns:ns + nl], refs[ns + nl + na], refs[ns + nl + na + 1])
        for cp in out:
            cp.start()
        refs[-1][...] = jnp.zeros_like(refs[-1])

    hbm = lambda b: pltpu.with_memory_space_constraint(b, pltpu.HBM)
    res = pl.pallas_call(
        body, name=name,
        out_shape=(pltpu.SemaphoreType.DMA((n_sems,)), pltpu.SemaphoreType.DMA((n_sems,)),
                   *[pltpu.HBM(b.shape, b.dtype) for b in landing], jax.ShapeDtypeStruct((8, 128), F32)),
        in_specs=[HBM] * (ns + nl) + [ANY] * na, out_specs=(SEM, SEM, *[HBM] * nl, pl.BlockSpec(memory_space=pltpu.VMEM)),
        input_output_aliases={ns + i: 2 + i for i in range(nl)},
        compiler_params=pltpu.CompilerParams(has_side_effects=DATAFLOW))(
            *[hbm(b) for b in sources], *[hbm(b) for b in landing], *after)
    return res[0], res[1], list(res[2:2 + nl]), res[-1]


def split_wait(name, sources, landing, send, recv, make, after):
    ns, nl = len(sources), len(landing)

    def body(*refs):
        _, back = make(refs[:ns], refs[ns:ns + nl], refs[ns + nl], refs[ns + nl + 1])
        for cp in back:
            cp.wait_send()
            cp.wait_recv()

    return list(pl.pallas_call(
        body, name=name, out_shape=tuple(pltpu.HBM(b.shape, b.dtype) for b in landing),
        in_specs=[HBM] * (ns + nl) + [SEM, SEM] + [ANY] * len(after), out_specs=tuple([HBM] * nl),
        input_output_aliases={ns + i: i for i in range(nl)},
        compiler_params=pltpu.CompilerParams(has_side_effects=DATAFLOW))(
            *[pltpu.with_memory_space_constraint(b, pltpu.HBM) for b in sources], *landing, send, recv, *after))


def _half_slot(buf, axis, chip, half):
    K, N = buf.shape
    if axis == 1:
        n = N // N_CHIPS
        return buf.at[pl.ds(pl.multiple_of(half * (K // 2), K // 2), K // 2), pl.ds(pl.multiple_of(chip * n, n), n)]
    k2 = K // N_CHIPS // 2
    return buf.at[pl.ds(pl.multiple_of((2 * chip + half) * k2, k2), k2), :]


def gather_copies(axes):
    def make(_, bufs, send, recv):
        x, y, c = lax.axis_index("x"), lax.axis_index("y"), lax.axis_index("c")
        out, back = [], []
        for i, (buf, ax) in enumerate(zip(bufs, axes)):
            mine = _half_slot(buf, ax, 2 * x + y, c)
            for j, (px, py) in enumerate(_other_chips(x, y)):
                kw = dict(send_sem=send.at[3 * i + j], recv_sem=recv.at[3 * i + j], device_id=(px, py, c),
                          device_id_type=MESH_ID)
                out.append(pltpu.make_async_remote_copy(src_ref=mine, dst_ref=mine, **kw))
                back.append(pltpu.make_async_remote_copy(src_ref=mine, dst_ref=_half_slot(buf, ax, 2 * px + py, c), **kw))
        return out, back
    return make


def forward_copies(axes):
    def make(_, bufs, send, recv):
        x, y, c = lax.axis_index("x"), lax.axis_index("y"), lax.axis_index("c")
        out, back = [], []
        for i, (buf, ax) in enumerate(zip(bufs, axes)):
            for j, (px, py) in enumerate(_other_chips(x, y)):
                have = _half_slot(buf, ax, 2 * px + py, c)
                kw = dict(send_sem=send.at[3 * i + j], recv_sem=recv.at[3 * i + j], device_id=(x, y, 1 - c),
                          device_id_type=MESH_ID)
                out.append(pltpu.make_async_remote_copy(src_ref=have, dst_ref=have, **kw))
                back.append(pltpu.make_async_remote_copy(src_ref=have, dst_ref=_half_slot(buf, ax, 2 * px + py, 1 - c), **kw))
        return out, back
    return make


def forward_sibling(bufs, axes, name):
    n = len(bufs)

    def body(*refs):
        out, back = forward_copies(axes)(None, refs[:n], *refs[2 * n:])
        for cp in out:
            cp.start()
        for cp in back:
            cp.wait_recv()
        for cp in out:
            cp.wait_send()

    return pl.pallas_call(
        body, name=name, in_specs=[ANY] * n, out_specs=[ANY] * n,
        out_shape=[jax.ShapeDtypeStruct(b.shape, b.dtype) for b in bufs],
        input_output_aliases={i: i for i in range(n)},
        scratch_shapes=[pltpu.SemaphoreType.DMA((3 * n,)), pltpu.SemaphoreType.DMA((3 * n,))],
        compiler_params=pltpu.CompilerParams(has_side_effects=True))(*bufs)


def all_reduce_small(p, name):
    R = p.shape[0]

    def body(p_ref, o_ref, sib_ref, chip_ref, send, recv):
        x, y, c = lax.axis_index("x"), lax.axis_index("y"), lax.axis_index("c")
        me = 2 * x + y
        chips = _other_chips(x, y)
        pair = pltpu.make_async_remote_copy(src_ref=p_ref, dst_ref=sib_ref, send_sem=send.at[0], recv_sem=recv.at[0],
                                            device_id=(x, y, 1 - c), device_id_type=MESH_ID)
        pair.start()
        pair.wait()
        chip_ref[me] = p_ref[...] + sib_ref[...]
        sends = []
        for j, (px, py) in enumerate(chips):
            cp = pltpu.make_async_remote_copy(src_ref=chip_ref.at[me], dst_ref=chip_ref.at[me], send_sem=send.at[1 + j],
                                              recv_sem=recv.at[1 + j], device_id=(px, py, c), device_id_type=MESH_ID)
            cp.start()
            sends.append(cp)
        for j, (px, py) in enumerate(chips):
            pltpu.make_async_remote_copy(src_ref=chip_ref.at[me], dst_ref=chip_ref.at[2 * px + py], send_sem=send.at[1 + j],
                                         recv_sem=recv.at[1 + j], device_id=(px, py, c), device_id_type=MESH_ID).wait_recv()
        for cp in sends:
            cp.wait_send()
        o_ref[...] = ((chip_ref[0] + chip_ref[1]) + chip_ref[2]) + chip_ref[3]

    vm = pl.BlockSpec(memory_space=pltpu.VMEM)
    return pl.pallas_call(
        body, name=name, in_specs=[vm], out_specs=vm, out_shape=jax.ShapeDtypeStruct((R, 128), F32),
        scratch_shapes=[pltpu.VMEM((R, 128), F32), pltpu.VMEM((N_CHIPS, R, 128), F32),
                        pltpu.SemaphoreType.DMA((4,)), pltpu.SemaphoreType.DMA((4,))],
        compiler_params=pltpu.CompilerParams(has_side_effects=True, vmem_limit_bytes=VMEM_LIMIT))(p)


def _grad_view(g, axis):
    K, N = g.shape
    return g.reshape(1, 2, K // 2, N) if axis == 1 else g.reshape(N_CHIPS, 2, K // N_CHIPS // 2, N)


def pair_copies(gvs, others, send, recv):
    x, y, c = lax.axis_index("x"), lax.axis_index("y"), lax.axis_index("c")
    out = [pltpu.make_async_remote_copy(src_ref=gv.at[:, 1 - c], dst_ref=o, send_sem=send.at[i], recv_sem=recv.at[i],
                                        device_id=(x, y, 1 - c), device_id_type=MESH_ID)
           for i, (gv, o) in enumerate(zip(gvs, others))]
    return out, out


def chip_copies(axes):
    def piece(s, ax, chip):
        if ax == 1:
            n = s.shape[2] // N_CHIPS
            return s.at[0, :, pl.ds(pl.multiple_of(chip * n, n), n)]
        return s.at[chip]

    def make(sums, qs, send, recv):
        x, y, c = lax.axis_index("x"), lax.axis_index("y"), lax.axis_index("c")
        out = []
        for i, (s, q, ax) in enumerate(zip(sums, qs, axes)):
            for j, (px, py) in enumerate(_other_chips(x, y)):
                out.append(pltpu.make_async_remote_copy(
                    src_ref=piece(s, ax, 2 * px + py), dst_ref=q.at[j], send_sem=send.at[3 * i + j],
                    recv_sem=recv.at[3 * i + j], device_id=(px, py, c), device_id_type=MESH_ID))
        return out, out
    return make


def share_copies(_, halves, send, recv):
    x, y, c = lax.axis_index("x"), lax.axis_index("y"), lax.axis_index("c")
    kw = lambda i: dict(send_sem=send.at[i], recv_sem=recv.at[i], device_id=(x, y, 1 - c), device_id_type=MESH_ID)
    out = [pltpu.make_async_remote_copy(src_ref=h.at[c], dst_ref=h.at[c], **kw(i)) for i, h in enumerate(halves)]
    back = [pltpu.make_async_remote_copy(src_ref=h.at[c], dst_ref=h.at[1 - c], **kw(i)) for i, h in enumerate(halves)]
    return out, back


def share_sibling(halves, name):
    n = len(halves)

    def body(*refs):
        out, back = share_copies(None, refs[:n], *refs[2 * n:])
        for cp in out:
            cp.start()
        for cp in back:
            cp.wait_recv()
        for cp in out:
            cp.wait_send()

    return pl.pallas_call(
        body, name=name, in_specs=[ANY] * n, out_specs=[ANY] * n,
        out_shape=[jax.ShapeDtypeStruct(a.shape, a.dtype) for a in halves],
        input_output_aliases={i: i for i in range(n)},
        scratch_shapes=[pltpu.SemaphoreType.DMA((n,)), pltpu.SemaphoreType.DMA((n,))],
        compiler_params=pltpu.CompilerParams(has_side_effects=True))(*halves)


def add_pair(gv, other, place, name):
    A, _, rows, N = gv.shape
    bm, bn = _pick(rows, 512, 16), _pick(N, 1408, 128)

    def body(p_ref, g_ref, o_ref, out_ref):
        out_ref[...] = (g_ref[...] + o_ref[...]).astype(GRAD_WIRE)

    spec = pltpu.PrefetchScalarGridSpec(
        num_scalar_prefetch=1, grid=(A, rows // bm, N // bn),
        in_specs=[pl.BlockSpec((None, None, bm, bn), lambda a, i, j, p: (a, p[1], i, j)),
                  pl.BlockSpec((None, bm, bn), lambda a, i, j, p: (a, i, j))],
        out_specs=pl.BlockSpec((None, bm, bn), lambda a, i, j, p: (a, i, j)))
    return pl.pallas_call(body, name=name, grid_spec=spec, out_shape=jax.ShapeDtypeStruct((A, rows, N), GRAD_WIRE),
                          compiler_params=_params("parallel", "parallel", "parallel"))(place, gv, other)


def add_chips(s, q, axis, place, name):
    _, rows, n = q.shape
    bm, bn = _pick(rows, 512, 16), _pick(n, 1408, 128)
    nbj = n // bn

    def body(p_ref, s_ref, q_ref, o_ref):
        o_ref[...] = ((s_ref[...].astype(F32) + q_ref[0].astype(F32)) + q_ref[1].astype(F32)) + q_ref[2].astype(F32)

    mine = (lambda i, j, p: (p[0], i, j)) if axis == 0 else (lambda i, j, p: (0, i, p[0] * nbj + j))
    spec = pltpu.PrefetchScalarGridSpec(
        num_scalar_prefetch=1, grid=(rows // bm, nbj),
        in_specs=[pl.BlockSpec((None, bm, bn), mine), pl.BlockSpec((3, bm, bn), lambda i, j, p: (0, i, j))],
        out_specs=pl.BlockSpec((None, bm, bn), lambda i, j, p: (p[1], i, j)))
    return pl.pallas_call(body, name=name, grid_spec=spec, out_shape=jax.ShapeDtypeStruct((2, rows, n), F32),
                          compiler_params=_params("parallel", "parallel"))(place, s, q)


def adamw(w, g, m, v, name):
    R, N = w.shape
    bm = _pick(R, 512, 8)
    c1 = 1.0 / (1.0 - ADAM_B1 ** ADAM_STEP)
    c2 = 1.0 / (1.0 - ADAM_B2 ** ADAM_STEP)

    def body(w_ref, g_ref, m_ref, v_ref, d_ref, nm_ref, nv_ref):
        gv = g_ref[...]
        nm = ADAM_B1 * m_ref[...] + (1.0 - ADAM_B1) * gv
        nv = ADAM_B2 * v_ref[...] + (1.0 - ADAM_B2) * (gv * gv)
        nm_ref[...] = nm
        nv_ref[...] = nv
        d_ref[...] = -ADAM_LR * ((nm * c1) / (jnp.sqrt(nv * c2) + ADAM_EPS) + ADAM_WD * w_ref[...])

    blk = pl.BlockSpec((bm, N), lambda i: (i, 0))
    out = jax.ShapeDtypeStruct((R, N), F32)
    return pl.pallas_call(body, name=name, grid=(R // bm,), in_specs=[blk] * 4, out_specs=[blk] * 3,
                          out_shape=[out, out, out], compiler_params=_params("parallel"))(w, g, m, v)


def adamw_layers(w, g0, g1, m, v, name):
    _, k, n = w.shape
    bm = _pick(k, 256, 8)
    c1 = 1.0 / (1.0 - ADAM_B1 ** ADAM_STEP)
    c2 = 1.0 / (1.0 - ADAM_B2 ** ADAM_STEP)

    def body(w_ref, g0_ref, g1_ref, m_ref, v_ref, g_ref, d_ref, nm_ref, nv_ref):
        def step(gv):
            nm = ADAM_B1 * m_ref[...] + (1.0 - ADAM_B1) * gv
            nv = ADAM_B2 * v_ref[...] + (1.0 - ADAM_B2) * (gv * gv)
            g_ref[...] = gv
            nm_ref[...] = nm
            nv_ref[...] = nv
            d_ref[...] = -ADAM_LR * ((nm * c1) / (jnp.sqrt(nv * c2) + ADAM_EPS) + ADAM_WD * w_ref[...])

        @pl.when(pl.program_id(0) == 0)
        def _():
            step(g0_ref[...])

        @pl.when(pl.program_id(0) == 1)
        def _():
            step(g1_ref[...])

    blk = pl.BlockSpec((None, bm, n), lambda l, i: (l, i, 0))
    out = jax.ShapeDtypeStruct(w.shape, F32)
    return pl.pallas_call(
        body, name=name, grid=(2, k // bm),
        in_specs=[blk, pl.BlockSpec((bm, n), lambda l, i: (i * (1 - l), 0)), pl.BlockSpec((bm, n), lambda l, i: (i * l, 0)),
                  blk, blk],
        out_specs=[blk] * 4, out_shape=[out] * 4, compiler_params=_params("arbitrary", "arbitrary"))(w, g0, g1, m, v)


def _pack(arrays):
    flat = jnp.concatenate([a.reshape(-1) for a in arrays])
    rows = -(-flat.shape[0] // (256 * 128)) * 256
    return jnp.pad(flat, (0, rows * 128 - flat.shape[0])).reshape(rows, 128)


def _unpack(p, shapes):
    flat, out, at = p.reshape(-1), [], 0
    for s in shapes:
        n = math.prod(s)
        out.append(flat[at:at + n].reshape(s))
        at += n
    return out


def kernel(x, mem, norm_ffn1, ffn1_w_in, ffn1_w_out, norm_mix, mix_w_in, sconv_w, sgu_norm_g, sgu_w, sgu_b, cconv_w, cconv_ln_g, cconv_ln_b, pool_w, pool_scale, mix_w_out, norm_xattn, norm_mem, xattn_wq, xattn_wkv, xattn_wo, norm_ffn2, ffn2_w_in, ffn2_w_out, norm_final, loss_target, m_norm_ffn1, m_ffn1_w_in, m_ffn1_w_out, m_norm_mix, m_mix_w_in, m_sconv_w, m_sgu_norm_g, m_sgu_w, m_sgu_b, m_cconv_w, m_cconv_ln_g, m_cconv_ln_b, m_pool_w, m_pool_scale, m_mix_w_out, m_norm_xattn, m_norm_mem, m_xattn_wq, m_xattn_wkv, m_xattn_wo, m_norm_ffn2, m_ffn2_w_in, m_ffn2_w_out, m_norm_final, v_norm_ffn1, v_ffn1_w_in, v_ffn1_w_out, v_norm_mix, v_mix_w_in, v_sconv_w, v_sgu_norm_g, v_sgu_w, v_sgu_b, v_cconv_w, v_cconv_ln_g, v_cconv_ln_b, v_pool_w, v_pool_scale, v_mix_w_out, v_norm_xattn, v_norm_mem, v_xattn_wq, v_xattn_wkv, v_xattn_wo, v_norm_ffn2, v_ffn2_w_in, v_ffn2_w_out, v_norm_final):
    given = dict(locals())
    w = {n: given[n] for n in WEIGHTS}
    L = ffn1_w_in.shape[0]
    assert L == 2, "the reduce-scatter gives one layer to each core of a chip"
    chip = 2 * lax.axis_index("x") + lax.axis_index("y")
    chip1 = chip.astype(jnp.int32).reshape(1)
    core = lax.axis_index("c").astype(jnp.int32).reshape(1)
    place = jnp.concatenate([chip1, core])

    axis = {n: 1 if n in COL_SHARDED else 0 for n in BIG}
    bufs = {}
    for n in BIG:
        for l, b in enumerate(cast_into_slot(w[n], axis[n] + 1, chip1, "cast_weights")):
            bufs[n, l] = b
    groups = {"a": [(n, 0) for n in BIG[:2]], "b": [(n, 0) for n in BIG[2:4]], "c": [(n, 0) for n in BIG[4:7]],
              "d": [(n, 0) for n in BIG[7:]], "e": [(n, 1) for n in BIG]}
    wc = sconv_w.shape[-1]
    conv_rows = [w[n].reshape(-1, wc) for n in SMALL_CONV]
    n_conv = sum(r.shape[0] for r in conv_rows)
    conv_pack = jnp.pad(jnp.concatenate(conv_rows, axis=0), ((0, -n_conv % 8), (0, 128 - wc)))[None]
    conv_all = all_gather_chips([conv_pack], [0], "gather_conv")[0]
    started, token = {}, conv_all
    for g, keys in groups.items():
        send, recv, thru, token = split_start("gather_start_" + g, [], [bufs[k] for k in keys], 3 * len(keys),
                                              gather_copies([axis[k[0]] for k in keys]), [token])
        started[g] = (send, recv, thru)
    ready, ahead = {}, {}
    small = SMALL_REPL + SMALL_CONV
    packed = [_pack([src[n] for n in small]) for src in
              (w, {n: given["m_" + n] for n in small}, {n: given["v_" + n] for n in small})]

    def arrived(g, after):
        send, recv, thru = started[g]
        axes = [axis[k[0]] for k in groups[g]]
        return axes, split_wait("gather_wait_" + g, [], thru, send, recv, gather_copies(axes), after)

    def fetch(n, l, after):
        g = next(g for g, keys in groups.items() if (n, l) in keys)
        if g not in ready:
            if g in ahead:
                axes, send, recv, thru, token_g = ahead.pop(g)
                done = split_wait("forward_wait_" + g, [], thru, send, recv, forward_copies(axes), [after, token_g])
            else:
                axes, done = arrived(g, [token] + packed if g == "a" else [after])
                done = forward_sibling(done, axes, "gather_forward")
            if g == "d":
                axes_e, landed = arrived("e", [after])
                send, recv, thru, token_e = split_start("forward_start_e", [], landed, 3 * len(landed),
                                                        forward_copies(axes_e), [after])
                done, (token_e,) = lax.optimization_barrier((done, [token_e]))
                ahead["e"] = (axes_e, send, recv, thru, token_e)
            ready[g] = dict(zip(groups[g], done))
        return ready[g][n, l]

    conv_full = jnp.moveaxis(conv_all[:, :n_conv, :wc], 0, 1).reshape(n_conv, N_CHIPS * wc)
    ws = {n: w[n] for n in SMALL_REPL}
    at = 0
    for n in SMALL_CONV:
        rows = w[n].shape[0] * w[n].shape[1]
        ws[n] = conv_full[at:at + rows].reshape(w[n].shape[0], w[n].shape[1], N_CHIPS * wc)
        at += rows

    halves, state = {}, {}
    reduce_groups = {"r1": [(n, 1) for n in BIG], "r0a": [(n, 0) for n in BIG[2:]], "r0b": [(n, 0) for n in BIG[:2]]}
    plan = {("layer", 1): [("pair", "r1")],
            ("ffn2", 0): [("chips", "r1")],
            ("mix", 0): [("finish", "r1"), ("share", "r1"), ("pair", "r0a")],
            ("ffn1_mid", 0): [("chips", "r0a")],
            ("ffn1_grads", 0): [("pair", "r0b")],
            ("layer", 0): [("finish", "r0a")]}


    def stage_pair(g, keys, grads, after):
        gvs = [_grad_view(grads[n][l], axis[n]) for n, l in keys]
        others = [lax.empty(gv.shape[:1] + gv.shape[2:], F32) for gv in gvs]
        send, recv, others, token = split_start("pair_start_" + g, gvs, others, len(gvs), pair_copies, [after])
        state[g] = dict(sources=gvs, send=send, recv=recv, landing=others, token=token)
        return [(state[g], "token")]

    def stage_chips(g, keys, grads, after):
        st = state[g]
        axes = [axis[n] for n, _ in keys]
        others = split_wait("pair_wait_" + g, st["sources"], st["landing"], st["send"], st["recv"], pair_copies,
                            [after, st["token"]])
        sums = [add_pair(gv, o, place, "add_pair") for gv, o in zip(st["sources"], others)]
        qs = [lax.empty((3, s.shape[1], s.shape[2] // (N_CHIPS if ax == 1 else 1)), GRAD_WIRE) for s, ax in zip(sums, axes)]
        send, recv, qs, token = split_start("chips_start_" + g, sums, qs, 3 * len(sums), chip_copies(axes), [after])
        state[g] = dict(sources=sums, send=send, recv=recv, landing=qs, token=token)
        return [(state[g], "token")]

    def stage_finish(g, keys, grads, after):
        st = state.pop(g)
        qs = split_wait("chips_wait_" + g, st["sources"], st["landing"], st["send"], st["recv"],
                        chip_copies([axis[n] for n, _ in keys]), [after, st["token"]])
        for key, s, q in zip(keys, st["sources"], qs):
            halves[key] = add_chips(s, q, axis[key[0]], place, "add_chips")
        return [(halves, key) for key in keys]

    def stage_share(g, keys, grads, after):
        send, recv, thru, token = split_start("share_start_" + g, [], [halves.pop(k) for k in keys], len(keys),
                                              share_copies, [after])
        state["share_" + g] = dict(keys=keys, send=send, recv=recv, landing=thru, token=token)
        return [(state["share_" + g], "token")]

    stages = {"pair": stage_pair, "chips": stage_chips, "finish": stage_finish, "share": stage_share}

    def progress(event, l, grads, values):
        places = []
        for stage, g in plan.get((event, l), []):
            places += stages[stage](g, reduce_groups[g], grads, values[0])
        places = [(box, k) for box, k in places if k in box]
        if places:
            values, tied = lax.optimization_barrier((values, [box[k] for box, k in places]))
            for (box, k), a in zip(places, tied):
                box[k] = a
        return values

    loss_part, grad_x, grads = _local_step(x[0], mem[0], loss_target[0], fetch, ws, L, progress)
    loss = lax.psum(loss_part, ("x", "y", "c"))

    small_g = [grads[n] if n == "norm_final" else jnp.stack(grads[n]) for n in small]
    small_sum = all_reduce_small(_pack(small_g), "reduce_small")
    grad = dict(zip(small, _unpack(small_sum, [g.shape for g in small_g])))
    for n in SMALL_CONV:
        grad[n] = lax.dynamic_slice_in_dim(grad[n], chip * wc, wc, axis=2)

    delta, new_m, new_v = {}, {}, {}

    sent = state.pop("share_r1")
    layer1 = dict(zip(sent["keys"], split_wait("share_wait_r1", [], sent["landing"], sent["send"], sent["recv"],
                                               share_copies, [small_sum, sent["token"]])))

    def finish_weights(names, layer0_halves):
        layer0 = dict(zip(names, share_sibling(layer0_halves, "share_pair")))
        for n in names:
            g0, g1 = layer0[n].reshape(w[n].shape[1:]), layer1[n, 1].reshape(w[n].shape[1:])
            grad[n], delta[n], new_m[n], new_v[n] = adamw_layers(w[n], g0, g1, given["m_" + n], given["v_" + n], "adamw")

    stage_chips("r0b", reduce_groups["r0b"], grads, small_sum)
    early, (state["r0b"]["token"],) = lax.optimization_barrier(([halves[n, 0] for n in BIG[2:]], [state["r0b"]["token"]]))
    finish_weights(BIG[2:], early)
    stage_finish("r0b", reduce_groups["r0b"], grads, delta[BIG[-1]])
    finish_weights(BIG[:2], [halves[n, 0] for n in BIG[:2]])
    shapes = [w[n].shape for n in small]
    packed.insert(1, _pack([grad[n] for n in small]))
    for out, p in zip((delta, new_m, new_v), adamw(*packed, "adamw_small")):
        out.update(zip(small, _unpack(p, shapes)))

    return (loss, grad_x[None], *[grad[n] for n in WEIGHTS], *[delta[n] for n in WEIGHTS],
            *[new_m[n] for n in WEIGHTS], *[new_v[n] for n in WEIGHTS])
```

```python
import functools
import math

import jax
import jax.numpy as jnp
from jax import lax
from jax.experimental import pallas as pl
from jax.experimental.pallas import tpu as pltpu

F32 = jnp.float32
BF16 = jnp.bfloat16
EPS = 1e-6
SEQ_CHUNK = 128
POOL_WINDOWS = (2, 4, 8, 16)
N_HEADS = 4
ADAM_LR, ADAM_B1, ADAM_B2, ADAM_EPS, ADAM_WD, ADAM_STEP = 0.001, 0.9, 0.999, 1e-08, 0.01, 10
VMEM_LIMIT = 56 * 1024 * 1024
MESH_ID = pl.DeviceIdType.MESH
N_CHIPS = 4
GRAD_WIRE = BF16

BIG = ("ffn1_w_in", "ffn1_w_out", "mix_w_in", "mix_w_out", "xattn_wq", "xattn_wkv", "xattn_wo",
       "ffn2_w_in", "ffn2_w_out")
COL_SHARDED = ("ffn1_w_in", "mix_w_in", "xattn_wkv", "ffn2_w_in")
SMALL_CONV = ("sconv_w", "cconv_w")
SMALL_REPL = ("norm_ffn1", "norm_mix", "sgu_norm_g", "sgu_w", "sgu_b", "cconv_ln_g", "cconv_ln_b",
              "pool_w", "pool_scale", "norm_xattn", "norm_mem", "norm_ffn2", "norm_final")
WEIGHTS = ("norm_ffn1", "ffn1_w_in", "ffn1_w_out", "norm_mix", "mix_w_in", "sconv_w", "sgu_norm_g",
           "sgu_w", "sgu_b", "cconv_w", "cconv_ln_g", "cconv_ln_b", "pool_w", "pool_scale",
           "mix_w_out", "norm_xattn", "norm_mem", "xattn_wq", "xattn_wkv", "xattn_wo", "norm_ffn2",
           "ffn2_w_in", "ffn2_w_out", "norm_final")


def _pick(n, pref, align):
    best = None
    for d in range(align, min(n, pref) + 1, align):
        if n % d == 0:
            best = d
    return best or n


def _sig(x):
    return 0.5 * jnp.tanh(0.5 * x) + 0.5


def _nt(a, b):
    return lax.dot_general(a, b, (((1,), (1,)), ((), ())), preferred_element_type=F32)


def _tn(a, b):
    return lax.dot_general(a, b, (((0,), (0,)), ((), ())), preferred_element_type=F32)


def _params(*sem):
    return pltpu.CompilerParams(dimension_semantics=sem, vmem_limit_bytes=VMEM_LIMIT)


def norm_matmul(x, g, w, name, out_dtype=F32):
    T, D = x.shape
    N = w.shape[1]
    tm, tn = _pick(T, 512, 8), _pick(N, 2048, 128)

    def body(x_ref, g_ref, w_ref, o_ref, h_ref):
        j = pl.program_id(1)

        @pl.when(j == 0)
        def _():
            xv = x_ref[...]
            r = lax.rsqrt(jnp.mean(xv * xv, axis=-1, keepdims=True) + EPS)
            h_ref[...] = (xv * r * g_ref[...]).astype(BF16)

        o_ref[...] = jnp.dot(h_ref[...], w_ref[...], preferred_element_type=F32).astype(out_dtype)

    return pl.pallas_call(
        body, name=name, grid=(T // tm, N // tn),
        in_specs=[pl.BlockSpec((tm, D), lambda i, j: (i, 0)), pl.BlockSpec((1, D), lambda i, j: (0, 0)),
                  pl.BlockSpec((D, tn), lambda i, j: (0, j))],
        out_specs=[pl.BlockSpec((tm, tn), lambda i, j: (i, j)), pl.BlockSpec((tm, D), lambda i, j: (i, 0))],
        out_shape=[jax.ShapeDtypeStruct((T, N), out_dtype), jax.ShapeDtypeStruct((T, D), BF16)],
        compiler_params=_params("parallel", "arbitrary"))(x, g, w)


def matmul_res(res, a, w, name):
    T, K = a.shape
    N = w.shape[1]
    tm, tn = _pick(T, 512, 8), _pick(N, 1024, 128)

    def body(r_ref, a_ref, w_ref, o_ref):
        o_ref[...] = r_ref[...] + jnp.dot(a_ref[...].astype(BF16), w_ref[...], preferred_element_type=F32)

    return pl.pallas_call(
        body, name=name, grid=(T // tm, N // tn),
        in_specs=[pl.BlockSpec((tm, tn), lambda i, j: (i, j)), pl.BlockSpec((tm, K), lambda i, j: (i, 0)),
                  pl.BlockSpec((K, tn), lambda i, j: (0, j))],
        out_specs=pl.BlockSpec((tm, tn), lambda i, j: (i, j)),
        out_shape=jax.ShapeDtypeStruct((T, N), F32),
        compiler_params=_params("parallel", "parallel"))(res, a, w)


def matmul_nt(a, w, name, out_dtype=F32):
    T, N = a.shape
    M = w.shape[0]
    tm, tmm = _pick(T, 512, 8), _pick(M, 1024, 128)

    def body(a_ref, w_ref, o_ref):
        o_ref[...] = _nt(a_ref[...].astype(BF16), w_ref[...]).astype(out_dtype)

    return pl.pallas_call(
        body, name=name, grid=(T // tm, M // tmm),
        in_specs=[pl.BlockSpec((tm, N), lambda i, j: (i, 0)), pl.BlockSpec((tmm, N), lambda i, j: (j, 0))],
        out_specs=pl.BlockSpec((tm, tmm), lambda i, j: (i, j)),
        out_shape=jax.ShapeDtypeStruct((T, M), out_dtype),
        compiler_params=_params("parallel", "parallel"))(a, w)


def matmul_tn(a, b, scale, name, b2=None):
    T, M = a.shape
    Nb = b.shape[1]
    bm, bn, bk = _pick(M, 1408, 128), _pick(Nb, 1408, 128), _pick(T, 512, 8)
    nk, nj = T // bk, Nb // bn

    def body(a_ref, b_ref, *rest):
        o_ref = rest[-1]
        j, k = pl.program_id(1), pl.program_id(2)

        @pl.when(k == 0)
        def _():
            o_ref[...] = jnp.zeros_like(o_ref)

        a_blk = a_ref[...].astype(BF16)
        if b2 is None:
            o_ref[...] += _tn(a_blk, b_ref[...].astype(BF16))
        else:
            @pl.when(j < nj)
            def _():
                o_ref[...] += _tn(a_blk, b_ref[...].astype(BF16))

            @pl.when(j >= nj)
            def _():
                o_ref[...] += _tn(a_blk, rest[0][...].astype(BF16))

        if scale != 1.0:
            @pl.when(k == nk - 1)
            def _():
                o_ref[...] = o_ref[...] * scale

    if b2 is None:
        b_specs, operands, n_out = [pl.BlockSpec((bk, bn), lambda i, j, k: (k, j))], (a, b), nj
    else:
        first = lambda i, j, k: (jnp.where(j < nj, k, 0), jnp.where(j < nj, j, 0))
        second = lambda i, j, k: (jnp.where(j >= nj, k, 0), jnp.where(j >= nj, j - nj, 0))
        b_specs, operands, n_out = [pl.BlockSpec((bk, bn), first), pl.BlockSpec((bk, bn), second)], (a, b, b2), 2 * nj
    return pl.pallas_call(
        body, name=name, grid=(M // bm, n_out, nk),
        in_specs=[pl.BlockSpec((bk, bm), lambda i, j, k: (k, i))] + b_specs,
        out_specs=pl.BlockSpec((bm, bn), lambda i, j, k: (i, j)),
        out_shape=jax.ShapeDtypeStruct((M, n_out * bn), F32),
        compiler_params=_params("parallel", "parallel", "arbitrary"))(*operands)


def rmsnorm_bwd(dxo, dh, x, g, name):
    T, D = x.shape
    tm = _pick(T, 512, 8)
    has_res = dxo is not None

    def body(*refs):
        if has_res:
            dxo_ref, dh_ref, x_ref, g_ref, dx_ref, dg_ref = refs
        else:
            dh_ref, x_ref, g_ref, dx_ref, dg_ref = refs
        i = pl.program_id(0)

        @pl.when(i == 0)
        def _():
            dg_ref[...] = jnp.zeros_like(dg_ref)

        xv, dh_v = x_ref[...], dh_ref[...]
        r = lax.rsqrt(jnp.mean(xv * xv, axis=-1, keepdims=True) + EPS)
        xh = xv * r
        dg_ref[...] += jnp.sum(dh_v * xh, axis=0, keepdims=True)
        dxh = dh_v * g_ref[...]
        dx = r * (dxh - xh * jnp.mean(dxh * xh, axis=-1, keepdims=True))
        dx_ref[...] = dx + dxo_ref[...] if has_res else dx

    tile = pl.BlockSpec((tm, D), lambda i: (i, 0))
    vec = pl.BlockSpec((1, D), lambda i: (0, 0))
    args = ([dxo] if has_res else []) + [dh, x, g]
    return pl.pallas_call(
        body, name=name, grid=(T // tm,),
        in_specs=[tile] * (len(args) - 1) + [vec],
        out_specs=[tile, vec],
        out_shape=[jax.ShapeDtypeStruct((T, D), F32), jax.ShapeDtypeStruct((1, D), F32)],
        compiler_params=_params("arbitrary"))(*args)


def ffn_fwd(x, g, w_in, w_out, name):
    T, D = x.shape
    F = w_out.shape[0]
    tm, tf = _pick(T, 512, 8), _pick(F, 1408, 128)
    nf = F // tf

    def body(x_ref, g_ref, wg_ref, wu_ref, wo_ref, o_ref, h_ref, zg_ref, zu_ref, acc_ref):
        j = pl.program_id(1)

        @pl.when(j == 0)
        def _():
            xv = x_ref[...]
            r = lax.rsqrt(jnp.mean(xv * xv, axis=-1, keepdims=True) + EPS)
            h_ref[...] = (xv * r * g_ref[...]).astype(BF16)
            acc_ref[...] = jnp.zeros_like(acc_ref)

        h = h_ref[...]
        zg = jnp.dot(h, wg_ref[...], preferred_element_type=F32)
        zu = jnp.dot(h, wu_ref[...], preferred_element_type=F32)
        zg_ref[...] = zg.astype(BF16)
        zu_ref[...] = zu.astype(BF16)
        a = (zg * _sig(zg) * zu).astype(BF16)
        acc_ref[...] += jnp.dot(a, wo_ref[...], preferred_element_type=F32)

        @pl.when(j == nf - 1)
        def _():
            o_ref[...] = x_ref[...] + 0.5 * acc_ref[...]

    tile = pl.BlockSpec((tm, D), lambda i, j: (i, 0))
    fblk = pl.BlockSpec((tm, tf), lambda i, j: (i, j))
    hidden = jax.ShapeDtypeStruct((T, F), BF16)
    return pl.pallas_call(
        body, name=name, grid=(T // tm, nf),
        in_specs=[tile, pl.BlockSpec((1, D), lambda i, j: (0, 0)),
                  pl.BlockSpec((D, tf), lambda i, j: (0, j)), pl.BlockSpec((D, tf), lambda i, j: (0, j + nf)),
                  pl.BlockSpec((tf, D), lambda i, j: (j, 0))],
        out_specs=[tile, tile, fblk, fblk],
        out_shape=[jax.ShapeDtypeStruct((T, D), F32), jax.ShapeDtypeStruct((T, D), BF16), hidden, hidden],
        scratch_shapes=[pltpu.VMEM((tm, D), F32)],
        compiler_params=_params("parallel", "arbitrary"))(x, g, w_in, w_in, w_out)


def ffn_dz(dxo, zg, zu, w_out, name):
    T, D = dxo.shape
    F = w_out.shape[0]
    tm, tf = _pick(T, 512, 8), _pick(F, 256, 128)

    def body(dxo_ref, zg_ref, zu_ref, wo_ref, a_ref, dzg_ref, dzu_ref):
        do = (0.5 * dxo_ref[...]).astype(BF16)
        for j in range(F // tf):
            cols = slice(j * tf, (j + 1) * tf)
            zg, zu = zg_ref[:, cols].astype(F32), zu_ref[:, cols].astype(F32)
            s = _sig(zg)
            silu = zg * s
            a_ref[:, cols] = (silu * zu).astype(BF16)
            da = _nt(do, wo_ref[cols, :])
            dzu_ref[:, cols] = (da * silu).astype(BF16)
            dzg_ref[:, cols] = (da * zu * (s + silu * (1.0 - s))).astype(BF16)

    rows = pl.BlockSpec((tm, F), lambda i: (i, 0))
    hidden = jax.ShapeDtypeStruct((T, F), BF16)
    return pl.pallas_call(
        body, name=name, grid=(T // tm,),
        in_specs=[pl.BlockSpec((tm, D), lambda i: (i, 0)), rows, rows, pl.BlockSpec((F, D), lambda i: (0, 0))],
        out_specs=[rows, rows, rows], out_shape=[hidden, hidden, hidden],
        compiler_params=_params("parallel"))(dxo, zg, zu, w_out)


def dh_norm_bwd(x, dxo, g, parts, w, name):
    T, D = x.shape
    F = parts[0].shape[1]
    n = len(parts)
    tm = _pick(T, 256, 8)

    def body(x_ref, dxo_ref, g_ref, *refs):
        a_refs, w_refs, (dx_ref, dg_ref) = refs[:n], refs[n:2 * n], refs[2 * n:]
        i = pl.program_id(0)

        @pl.when(i == 0)
        def _():
            dg_ref[...] = jnp.zeros_like(dg_ref)

        dh = sum(_nt(a_ref[...], w_ref[...]) for a_ref, w_ref in zip(a_refs, w_refs))
        xv = x_ref[...]
        r = lax.rsqrt(jnp.mean(xv * xv, axis=-1, keepdims=True) + EPS)
        xh = xv * r
        dg_ref[...] += jnp.sum(dh * xh, axis=0, keepdims=True)
        dxh = dh * g_ref[...]
        dx_ref[...] = dxo_ref[...] + r * (dxh - xh * jnp.mean(dxh * xh, axis=-1, keepdims=True))

    tile = pl.BlockSpec((tm, D), lambda i: (i, 0))
    vec = pl.BlockSpec((1, D), lambda i: (0, 0))
    return pl.pallas_call(
        body, name=name, grid=(T // tm,),
        in_specs=[tile, tile, vec] + [pl.BlockSpec((tm, F), lambda i: (i, 0))] * n
                 + [pl.BlockSpec((D, F), lambda i, p=p: (0, p)) for p in range(n)],
        out_specs=[tile, vec],
        out_shape=[jax.ShapeDtypeStruct((T, D), F32), jax.ShapeDtypeStruct((1, D), F32)],
        compiler_params=_params("arbitrary"))(x, dxo, g, *parts, *([w] * n))


def _chunks(T, fn):
    def step(c, carry):
        fn(pl.multiple_of(c * SEQ_CHUNK, SEQ_CHUNK))
        return carry
    lax.fori_loop(0, T // SEQ_CHUNK, step, 0)


def _conv_taps(win, ktaps, pad):
    return [(win if k == ktaps - 1 else pltpu.roll(win, ktaps - 1 - k, 0))[pad:, :] for k in range(ktaps)]


def _conv_taps_t(win, ktaps, pad):
    n = win.shape[0]
    return [(win if k == ktaps - 1 else pltpu.roll(win, n - (ktaps - 1 - k), 0))[:n - pad, :] for k in range(ktaps)]


def _col(T, W, idx):
    return pl.BlockSpec((T, W), lambda i, idx=idx: (0, idx))


def _full(shape):
    return pl.BlockSpec(shape, lambda i: (0,) * len(shape))


def mix_a_fwd(z, w, name):
    T, W = z.shape[0], w.shape[1]
    K, P = w.shape[0], 8

    def body(ab_ref, ac_ref, ax_ref, w_ref, y_ref, pp_ref):
        pp_ref[0:P, :] = jnp.zeros((P, W), F32)

        def chunk(s):
            rows = pl.ds(s, SEQ_CHUNK)
            pp_ref[pl.ds(s + P, SEQ_CHUNK), :] = ac_ref[rows, :] * ax_ref[rows, :]
            taps = _conv_taps(pp_ref[pl.ds(s, SEQ_CHUNK + P), :], K, P)
            q = sum(w_ref[k:k + 1, :] * taps[k] for k in range(K))
            y_ref[rows, :] = (ab_ref[rows, :] * q).astype(BF16)

        _chunks(T, chunk)

    return pl.pallas_call(
        body, name=name, grid=(1,),
        in_specs=[_col(T, W, 0), _col(T, W, 1), _col(T, W, 2), _full((K, W))],
        out_specs=_full((T, W)), out_shape=jax.ShapeDtypeStruct((T, W), BF16),
        scratch_shapes=[pltpu.VMEM((T + P, W), F32)],
        compiler_params=_params("arbitrary"))(z, z, z, w)


def mix_a_bwd(z, dy, w, name):
    T, W = z.shape[0], w.shape[1]
    K, P = w.shape[0], 8

    def body(ab_ref, ac_ref, ax_ref, dy_ref, w_ref, dab_ref, dac_ref, dax_ref, dw_ref, pp_ref, dq_ref):
        pp_ref[0:P, :] = jnp.zeros((P, W), F32)
        dq_ref[T:T + P, :] = jnp.zeros((P, W), F32)
        dw_ref[...] = jnp.zeros_like(dw_ref)

        def chunk1(s):
            rows = pl.ds(s, SEQ_CHUNK)
            pp_ref[pl.ds(s + P, SEQ_CHUNK), :] = ac_ref[rows, :] * ax_ref[rows, :]
            taps = _conv_taps(pp_ref[pl.ds(s, SEQ_CHUNK + P), :], K, P)
            q = sum(w_ref[k:k + 1, :] * taps[k] for k in range(K))
            dyv = dy_ref[rows, :]
            dab_ref[rows, :] = (dyv * q).astype(BF16)
            dq = dyv * ab_ref[rows, :]
            dq_ref[rows, :] = dq
            for k in range(K):
                dw_ref[k:k + 1, :] += jnp.sum(dq * taps[k], axis=0, keepdims=True)

        _chunks(T, chunk1)

        def chunk2(s):
            rows = pl.ds(s, SEQ_CHUNK)
            taps = _conv_taps_t(dq_ref[pl.ds(s, SEQ_CHUNK + P), :], K, P)
            dp = sum(w_ref[k:k + 1, :] * taps[k] for k in range(K))
            dac_ref[rows, :] = (dp * ax_ref[rows, :]).astype(BF16)
            dax_ref[rows, :] = (dp * ac_ref[rows, :]).astype(BF16)

        _chunks(T, chunk2)

    tw = jax.ShapeDtypeStruct((T, W), BF16)
    return pl.pallas_call(
        body, name=name, grid=(1,),
        in_specs=[_col(T, W, 0), _col(T, W, 1), _col(T, W, 2), _col(T, W, 0), _full((K, W))],
        out_specs=[_full((T, W))] * 3 + [_full((K, W))],
        out_shape=[tw, tw, tw, jax.ShapeDtypeStruct((K, W), F32)],
        scratch_shapes=[pltpu.VMEM((T + P, W), F32), pltpu.VMEM((T + P, W), F32)],
        compiler_params=_params("arbitrary"))(z, z, z, dy, w)


def _ln_stats(v):
    mu = jnp.mean(v, axis=-1, keepdims=True)
    xc = v - mu
    rstd = lax.rsqrt(jnp.mean(xc * xc, axis=-1, keepdims=True) + EPS)
    return xc * rstd, rstd


def _ln_bwd(dxh, xh, rstd):
    return rstd * (dxh - jnp.mean(dxh, axis=-1, keepdims=True) - xh * jnp.mean(dxh * xh, axis=-1, keepdims=True))


def _tril_bf16(w_ref, h):
    n = w_ref.shape[-1]
    keep = lax.broadcasted_iota(jnp.int32, (n, n), 0) >= lax.broadcasted_iota(jnp.int32, (n, n), 1)
    return jnp.where(keep, w_ref[h], 0.0).astype(BF16)


def mix_b_fwd(z, g, w_s, bias, name):
    T, W = z.shape[0], g.shape[1]
    H, C = w_s.shape[0], w_s.shape[1]
    hd = W // H

    def body(u_ref, v_ref, g_ref, w_ref, b_ref, y_ref):
        wts = [_tril_bf16(w_ref, h) for h in range(H)]
        head = lax.broadcasted_iota(jnp.int32, (C, W), 1) // hd

        def chunk(s):
            rows = pl.ds(s, C)
            xh, _ = _ln_stats(v_ref[rows, :])
            vn = (xh * g_ref[...]).astype(BF16)
            mixed = b_ref[...]
            for h in range(H):
                mixed = mixed + jnp.where(head == h, jnp.dot(wts[h], vn, preferred_element_type=F32), 0.0)
            y_ref[rows, :] = (u_ref[rows, :] * mixed).astype(BF16)

        _chunks(T, chunk)

    return pl.pallas_call(
        body, name=name, grid=(1,),
        in_specs=[_col(T, W, 3), _col(T, W, 4), _full((1, W)), _full((H, C, C)), _full((C, W))],
        out_specs=_full((T, W)), out_shape=jax.ShapeDtypeStruct((T, W), BF16),
        compiler_params=_params("arbitrary"))(z, z, g, w_s, bias)


def mix_b_bwd(z, dy, g, w_s, bias, name):
    T, W = z.shape[0], g.shape[1]
    H, C = w_s.shape[0], w_s.shape[1]
    hd = W // H

    def body(u_ref, v_ref, dy_ref, g_ref, w_ref, b_ref, du_ref, dv_ref, dw_ref, db_ref, dg_ref, dbf_ref):
        wts = [_tril_bf16(w_ref, h) for h in range(H)]
        head = lax.broadcasted_iota(jnp.int32, (C, W), 1) // hd
        dw_ref[...] = jnp.zeros_like(dw_ref)
        dg_ref[...] = jnp.zeros_like(dg_ref)
        dbf_ref[...] = jnp.zeros_like(dbf_ref)

        def chunk(s):
            rows = pl.ds(s, C)
            xh, rstd = _ln_stats(v_ref[rows, :])
            vn = (xh * g_ref[...]).astype(BF16)
            mixed = b_ref[...]
            for h in range(H):
                mixed = mixed + jnp.where(head == h, jnp.dot(wts[h], vn, preferred_element_type=F32), 0.0)
            dyv = dy_ref[rows, :]
            du_ref[rows, :] = (dyv * mixed).astype(BF16)
            dm = dyv * u_ref[rows, :]
            dbf_ref[...] += dm
            dvn = jnp.zeros((C, W), F32)
            for h in range(H):
                dmh = jnp.where(head == h, dm, 0.0).astype(BF16)
                dw_ref[h] += _nt(dmh, vn)
                dvn = dvn + _tn(wts[h], dmh)
            dg_ref[...] += jnp.sum(dvn * xh, axis=0, keepdims=True)
            dv_ref[rows, :] = _ln_bwd(dvn * g_ref[...], xh, rstd).astype(BF16)

        _chunks(T, chunk)

        keep = lax.broadcasted_iota(jnp.int32, (C, C), 0) >= lax.broadcasted_iota(jnp.int32, (C, C), 1)
        lane = lax.broadcasted_iota(jnp.int32, (C, 128), 1)
        db = jnp.zeros((C, 128), F32)
        dbf = dbf_ref[...]
        for h in range(H):
            dw_ref[h] = jnp.where(keep, dw_ref[h], 0.0)
            db = db + jnp.where(lane == h, jnp.sum(jnp.where(head == h, dbf, 0.0), axis=1, keepdims=True), 0.0)
        db_ref[...] = db

    tw = jax.ShapeDtypeStruct((T, W), BF16)
    return pl.pallas_call(
        body, name=name, grid=(1,),
        in_specs=[_col(T, W, 3), _col(T, W, 4), _col(T, W, 1), _full((1, W)), _full((H, C, C)), _full((C, W))],
        out_specs=[_full((T, W)), _full((T, W)), _full((H, C, C)), _full((C, 128)), _full((1, W))],
        out_shape=[tw, tw, jax.ShapeDtypeStruct((H, C, C), F32), jax.ShapeDtypeStruct((C, 128), F32),
                   jax.ShapeDtypeStruct((1, W), F32)],
        scratch_shapes=[pltpu.VMEM((C, W), F32)],
        compiler_params=_params("arbitrary"))(z, z, dy, g, w_s, bias)


def mix_c_fwd(z, w, ln_g, ln_b, name):
    T, W = z.shape[0], w.shape[1]
    K, P = w.shape[0], 32

    def body(a_ref, gt_ref, w_ref, g_ref, b_ref, y_ref, up_ref):
        up_ref[0:P, :] = jnp.zeros((P, W), F32)

        def chunk(s):
            rows = pl.ds(s, SEQ_CHUNK)
            up_ref[pl.ds(s + P, SEQ_CHUNK), :] = a_ref[rows, :] * _sig(gt_ref[rows, :])
            taps = _conv_taps(up_ref[pl.ds(s, SEQ_CHUNK + P), :], K, P)
            q = sum(w_ref[k:k + 1, :] * taps[k] for k in range(K))
            xh, _ = _ln_stats(q)
            r = xh * g_ref[...] + b_ref[...]
            y_ref[rows, :] = (r * _sig(r)).astype(BF16)

        _chunks(T, chunk)

    return pl.pallas_call(
        body, name=name, grid=(1,),
        in_specs=[_col(T, W, 5), _col(T, W, 6), _full((K, W)), _full((1, W)), _full((1, W))],
        out_specs=_full((T, W)), out_shape=jax.ShapeDtypeStruct((T, W), BF16),
        scratch_shapes=[pltpu.VMEM((T + P, W), F32)],
        compiler_params=_params("arbitrary"))(z, z, w, ln_g, ln_b)


def mix_c_bwd(z, dy, w, ln_g, ln_b, name):
    T, W = z.shape[0], w.shape[1]
    K, P = w.shape[0], 32

    def body(a_ref, gt_ref, dy_ref, w_ref, g_ref, b_ref, da_ref, dgt_ref, dw_ref, dg_ref, db_ref, up_ref, dq_ref):
        up_ref[0:P, :] = jnp.zeros((P, W), F32)
        dq_ref[T:T + P, :] = jnp.zeros((P, W), F32)
        dw_ref[...] = jnp.zeros_like(dw_ref)
        dg_ref[...] = jnp.zeros_like(dg_ref)
        db_ref[...] = jnp.zeros_like(db_ref)

        def chunk1(s):
            rows = pl.ds(s, SEQ_CHUNK)
            up_ref[pl.ds(s + P, SEQ_CHUNK), :] = a_ref[rows, :] * _sig(gt_ref[rows, :])
            taps = _conv_taps(up_ref[pl.ds(s, SEQ_CHUNK + P), :], K, P)
            q = sum(w_ref[k:k + 1, :] * taps[k] for k in range(K))
            xh, rstd = _ln_stats(q)
            r = xh * g_ref[...] + b_ref[...]
            sr = _sig(r)
            dr = dy_ref[rows, :] * (sr * (1.0 + r * (1.0 - sr)))
            db_ref[...] += jnp.sum(dr, axis=0, keepdims=True)
            dg_ref[...] += jnp.sum(dr * xh, axis=0, keepdims=True)
            dq = _ln_bwd(dr * g_ref[...], xh, rstd)
            dq_ref[rows, :] = dq
            for k in range(K):
                dw_ref[k:k + 1, :] += jnp.sum(dq * taps[k], axis=0, keepdims=True)

        _chunks(T, chunk1)

        def chunk2(s):
            rows = pl.ds(s, SEQ_CHUNK)
            taps = _conv_taps_t(dq_ref[pl.ds(s, SEQ_CHUNK + P), :], K, P)
            du = sum(w_ref[k:k + 1, :] * taps[k] for k in range(K))
            sg = _sig(gt_ref[rows, :])
            da_ref[rows, :] = (du * sg).astype(BF16)
            dgt_ref[rows, :] = (du * a_ref[rows, :] * sg * (1.0 - sg)).astype(BF16)

        _chunks(T, chunk2)

    tw = jax.ShapeDtypeStruct((T, W), BF16)
    vec = jax.ShapeDtypeStruct((1, W), F32)
    return pl.pallas_call(
        body, name=name, grid=(1,),
        in_specs=[_col(T, W, 5), _col(T, W, 6), _col(T, W, 2), _full((K, W)), _full((1, W)), _full((1, W))],
        out_specs=[_full((T, W)), _full((T, W)), _full((K, W)), _full((1, W)), _full((1, W))],
        out_shape=[tw, tw, jax.ShapeDtypeStruct((K, W), F32), vec, vec],
        scratch_shapes=[pltpu.VMEM((T + P, W), F32), pltpu.VMEM((T + P, W), F32)],
        compiler_params=_params("arbitrary"))(z, z, dy, w, ln_g, ln_b)


def _pool_select(levels, W, rows):
    group = lax.broadcasted_iota(jnp.int32, (rows, W), 1) // (W // len(POOL_WINDOWS))
    out = levels[-1]
    for gi in range(len(POOL_WINDOWS) - 2, -1, -1):
        out = jnp.where(group == gi, levels[gi], out)
    return out


def _pool_count(s, W):
    t = s + lax.broadcasted_iota(jnp.int32, (SEQ_CHUNK, W), 0)
    group = lax.broadcasted_iota(jnp.int32, (SEQ_CHUNK, W), 1) // (W // len(POOL_WINDOWS))
    win = jnp.full((SEQ_CHUNK, W), POOL_WINDOWS[-1], jnp.int32)
    for gi in range(len(POOL_WINDOWS) - 2, -1, -1):
        win = jnp.where(group == gi, POOL_WINDOWS[gi], win)
    return jnp.minimum(t + 1, win).astype(F32)


def _pooled(wp_ref, s, W, P):
    win = wp_ref[pl.ds(s, SEQ_CHUNK + P), :]
    levels, acc, shift = [], win, 1
    for _ in POOL_WINDOWS:
        acc = acc + pltpu.roll(acc, shift, 0)
        levels.append(acc[P:, :])
        shift *= 2
    return _pool_select(levels, W, SEQ_CHUNK) / _pool_count(s, W) - win[P:, :]


def mix_d_fwd(z, pbd, scale, name):
    T, W = z.shape[0], scale.shape[1]
    P = 16

    def body(x_ref, p_ref, s_ref, y_ref, wp_ref):
        wp_ref[0:P, :] = jnp.zeros((P, W), F32)

        def chunk(s):
            rows = pl.ds(s, SEQ_CHUNK)
            wp_ref[pl.ds(s + P, SEQ_CHUNK), :] = x_ref[rows, :]
            pooled = _pooled(wp_ref, s, W, P).astype(BF16)
            y_ref[rows, :] = (jnp.dot(pooled, p_ref[...], preferred_element_type=F32) * s_ref[...]).astype(BF16)

        _chunks(T, chunk)

    return pl.pallas_call(
        body, name=name, grid=(1,),
        in_specs=[_col(T, W, 7), _full((W, W)), _full((1, W))],
        out_specs=_full((T, W)), out_shape=jax.ShapeDtypeStruct((T, W), BF16),
        scratch_shapes=[pltpu.VMEM((T + P, W), F32)],
        compiler_params=_params("arbitrary"))(z, pbd, scale)


def mix_d_bwd(z, dy, pbd, scale, name):
    T, W = z.shape[0], scale.shape[1]
    P = 16

    def body(x_ref, dy_ref, p_ref, s_ref, dx_ref, dp_ref, ds_ref, wp_ref, e_ref, dpool_ref):
        wp_ref[0:P, :] = jnp.zeros((P, W), F32)
        e_ref[T:T + P, :] = jnp.zeros((P, W), F32)
        dp_ref[...] = jnp.zeros_like(dp_ref)
        ds_ref[...] = jnp.zeros_like(ds_ref)

        def chunk1(s):
            rows = pl.ds(s, SEQ_CHUNK)
            wp_ref[pl.ds(s + P, SEQ_CHUNK), :] = x_ref[rows, :]
            pooled = _pooled(wp_ref, s, W, P).astype(BF16)
            yl = jnp.dot(pooled, p_ref[...], preferred_element_type=F32)
            dyv = dy_ref[rows, :]
            ds_ref[...] += jnp.sum(dyv * yl, axis=0, keepdims=True)
            dyl = (dyv * s_ref[...]).astype(BF16)
            dp_ref[...] += _tn(pooled, dyl)
            dpool = _nt(dyl, p_ref[...])
            dpool_ref[rows, :] = dpool
            e_ref[rows, :] = dpool / _pool_count(s, W)

        _chunks(T, chunk1)

        def chunk2(s):
            rows = pl.ds(s, SEQ_CHUNK)
            win = e_ref[pl.ds(s, SEQ_CHUNK + P), :]
            n = SEQ_CHUNK + P
            levels, acc, shift = [], win, 1
            for _ in POOL_WINDOWS:
                acc = acc + pltpu.roll(acc, n - shift, 0)
                levels.append(acc[:SEQ_CHUNK, :])
                shift *= 2
            dx_ref[rows, :] = (_pool_select(levels, W, SEQ_CHUNK) - dpool_ref[rows, :]).astype(BF16)

        _chunks(T, chunk2)

    return pl.pallas_call(
        body, name=name, grid=(1,),
        in_specs=[_col(T, W, 7), _col(T, W, 3), _full((W, W)), _full((1, W))],
        out_specs=[_full((T, W)), _full((W, W)), _full((1, W))],
        out_shape=[jax.ShapeDtypeStruct((T, W), BF16), jax.ShapeDtypeStruct((W, W), F32),
                   jax.ShapeDtypeStruct((1, W), F32)],
        scratch_shapes=[pltpu.VMEM((T + P, W), F32), pltpu.VMEM((T + P, W), F32), pltpu.VMEM((T, W), F32)],
        compiler_params=_params("arbitrary"))(z, dy, pbd, scale)


def attn_fwd(q, kv, name):
    T, D = q.shape
    M = kv.shape[0]
    hd = D // N_HEADS
    tm = _pick(T, 512, 8)
    sc = 1.0 / math.sqrt(hd)

    def body(q_ref, k_ref, v_ref, o_ref):
        for h in range(N_HEADS):
            cols = slice(h * hd, (h + 1) * hd)
            s = _nt(q_ref[:, cols].astype(BF16), k_ref[:, cols].astype(BF16)) * sc
            p = jnp.exp(s - jnp.max(s, axis=-1, keepdims=True))
            p = p / jnp.sum(p, axis=-1, keepdims=True)
            o_ref[:, cols] = jnp.dot(p.astype(BF16), v_ref[:, cols].astype(BF16),
                                     preferred_element_type=F32).astype(BF16)

    return pl.pallas_call(
        body, name=name, grid=(T // tm,),
        in_specs=[pl.BlockSpec((tm, D), lambda i: (i, 0)), pl.BlockSpec((M, D), lambda i: (0, 0)),
                  pl.BlockSpec((M, D), lambda i: (0, 1))],
        out_specs=pl.BlockSpec((tm, D), lambda i: (i, 0)),
        out_shape=jax.ShapeDtypeStruct((T, D), BF16),
        compiler_params=_params("parallel"))(q, kv, kv)


def attn_bwd(q, kv, do, name):
    T, D = q.shape
    M = kv.shape[0]
    hd = D // N_HEADS
    tm = _pick(T, 512, 8)
    sc = 1.0 / math.sqrt(hd)

    def body(q_ref, k_ref, v_ref, do_ref, dq_ref, dk_ref, dv_ref):
        i = pl.program_id(0)

        @pl.when(i == 0)
        def _():
            dk_ref[...] = jnp.zeros_like(dk_ref)
            dv_ref[...] = jnp.zeros_like(dv_ref)

        for h in range(N_HEADS):
            cols = slice(h * hd, (h + 1) * hd)
            qh, kh = q_ref[:, cols].astype(BF16), k_ref[:, cols].astype(BF16)
            vh, doh = v_ref[:, cols].astype(BF16), do_ref[:, cols].astype(BF16)
            s = _nt(qh, kh) * sc
            p = jnp.exp(s - jnp.max(s, axis=-1, keepdims=True))
            p = p / jnp.sum(p, axis=-1, keepdims=True)
            dp = _nt(doh, vh)
            dv_ref[:, cols] += _tn(p.astype(BF16), doh)
            ds = (p * (dp - jnp.sum(dp * p, axis=-1, keepdims=True)) * sc).astype(BF16)
            dq_ref[:, cols] = jnp.dot(ds, kh, preferred_element_type=F32).astype(BF16)
            dk_ref[:, cols] += _tn(ds, qh)

    tile = pl.BlockSpec((tm, D), lambda i: (i, 0))
    mem = jax.ShapeDtypeStruct((M, D), F32)
    return pl.pallas_call(
        body, name=name, grid=(T // tm,),
        in_specs=[tile, pl.BlockSpec((M, D), lambda i: (0, 0)), pl.BlockSpec((M, D), lambda i: (0, 1)), tile],
        out_specs=[tile, pl.BlockSpec((M, D), lambda i: (0, 0)), pl.BlockSpec((M, D), lambda i: (0, 0))],
        out_shape=[jax.ShapeDtypeStruct((T, D), BF16), mem, mem],
        compiler_params=_params("arbitrary"))(q, kv, kv, do)


def loss_head(x, g, target, name):
    T, D = x.shape
    tm = _pick(T, 512, 8)

    def body(x_ref, g_ref, t_ref, l_ref, dx_ref, dg_ref):
        i = pl.program_id(0)

        @pl.when(i == 0)
        def _():
            l_ref[...] = jnp.zeros_like(l_ref)
            dg_ref[...] = jnp.zeros_like(dg_ref)

        xv = x_ref[...]
        r = lax.rsqrt(jnp.mean(xv * xv, axis=-1, keepdims=True) + EPS)
        xh = xv * r
        err = xh * g_ref[...] - t_ref[...]
        l_ref[...] += 0.5 * jnp.sum(jnp.mean(err * err, axis=-1, keepdims=True), axis=0, keepdims=True)
        dy = err * (1.0 / D)
        dg_ref[...] += jnp.sum(dy * xh, axis=0, keepdims=True)
        dxh = dy * g_ref[...]
        dx_ref[...] = r * (dxh - xh * jnp.mean(dxh * xh, axis=-1, keepdims=True))

    tile = pl.BlockSpec((tm, D), lambda i: (i, 0))
    vec = pl.BlockSpec((1, D), lambda i: (0, 0))
    return pl.pallas_call(
        body, name=name, grid=(T // tm,),
        in_specs=[tile, vec, tile],
        out_specs=[pl.BlockSpec((1, 128), lambda i: (0, 0)), tile, vec],
        out_shape=[jax.ShapeDtypeStruct((1, 128), F32), jax.ShapeDtypeStruct((T, D), F32),
                   jax.ShapeDtypeStruct((1, D), F32)],
        compiler_params=_params("arbitrary"))(x, g, target)


def _block_diag(p):
    G, gd, _ = p.shape
    rows = [jnp.concatenate([p[g] if g == c else jnp.zeros((gd, gd), p.dtype) for c in range(G)], axis=1)
            for g in range(G)]
    return jnp.concatenate(rows, axis=0)


class _LazyWeight:
    def __init__(self, fetch, name, latest):
        self.fetch, self.name, self.latest = fetch, name, latest

    def __getitem__(self, l):
        return self.fetch(self.name, l, self.latest[0])


def _local_step(x, mem, target, fetch, ws, L, progress=lambda event, l, grads, values: values):
    T, D = x.shape
    W = D // 4
    H = ws["sgu_w"].shape[1]
    row = lambda v: v.reshape(1, -1)
    latest = [x]
    wb = {n: _LazyWeight(fetch, n, latest) for n in BIG}
    saved = []
    for l in range(L):
        s = {"x0": x}
        latest[0] = x
        x, *s["ffn1"] = ffn_fwd(x, row(ws["norm_ffn1"][l]), wb["ffn1_w_in"][l], wb["ffn1_w_out"][l], "ffn_fwd")
        s["x1"] = x
        latest[0] = x
        z, s["h_mix"] = norm_matmul(x, row(ws["norm_mix"][l]), wb["mix_w_in"][l], "mix_in")
        s["z"] = z
        s["bias"] = jnp.repeat(ws["sgu_b"][l].T, W // H, axis=1)
        s["pbd"] = _block_diag(ws["pool_w"][l]).astype(BF16)
        y = jnp.concatenate([
            mix_a_fwd(z, ws["sconv_w"][l], "mix_a_fwd"),
            mix_b_fwd(z, row(ws["sgu_norm_g"][l]), ws["sgu_w"][l], s["bias"], "mix_b_fwd"),
            mix_c_fwd(z, ws["cconv_w"][l], row(ws["cconv_ln_g"][l]), row(ws["cconv_ln_b"][l]), "mix_c_fwd"),
            mix_d_fwd(z, s["pbd"], row(ws["pool_scale"][l]), "mix_d_fwd")], axis=1)
        s["y"] = y
        x = matmul_res(x, y, wb["mix_w_out"][l], "mix_out")
        s["x2"] = x
        latest[0] = x
        s["q"], s["hq"] = norm_matmul(x, row(ws["norm_xattn"][l]), wb["xattn_wq"][l], "attn_q", out_dtype=BF16)
        s["kv"], s["mn"] = norm_matmul(mem, row(ws["norm_mem"][l]), wb["xattn_wkv"][l], "attn_kv")
        s["o"] = attn_fwd(s["q"], s["kv"], "attn_fwd")
        x = matmul_res(x, s["o"], wb["xattn_wo"][l], "attn_out")
        s["x3"] = x
        latest[0] = x
        x, *s["ffn2"] = ffn_fwd(x, row(ws["norm_ffn2"][l]), wb["ffn2_w_in"][l], wb["ffn2_w_out"][l], "ffn_fwd")
        saved.append(s)

    loss, dx, dg_final = loss_head(x, row(ws["norm_final"]), target, "loss_head")
    grads = {n: [None] * L for n in WEIGHTS if n != "norm_final"}
    grads["norm_final"] = dg_final.reshape(-1)

    def pin(dx, names, l):
        dx, made = lax.optimization_barrier((dx, [grads[n][l] for n in names]))
        for n, g in zip(names, made):
            grads[n][l] = g
        return dx

    def after_stages(event, l, values):
        return progress(event, l, grads, values)

    def ffn_back(xin, kept, dxo, gname, win, wout, l, event):
        h, zg, zu = kept
        a, dzg, dzu = ffn_dz(dxo, zg, zu, wb[wout][l], "ffn_dz")
        last = l == 0 and event == "ffn1_mid"
        if not last:
            dxn, dg = dh_norm_bwd(xin, dxo, row(ws[gname][l]), [dzg, dzu], wb[win][l], "ffn_dh")
            dxn, h, a = after_stages(event, l, (dxn, h, a))
        else:
            h, a = after_stages(event, l, (h, a))
        grads[win][l] = matmul_tn(h, dzg, 1.0, "ffn_dwin", b2=dzu)
        grads[wout][l] = matmul_tn(a, dxo, 0.5, "ffn_dwout")
        if last:
            dzg, dzu = after_stages("ffn1_grads", l, (dzg, dzu))
            dxn, dg = dh_norm_bwd(xin, dxo, row(ws[gname][l]), [dzg, dzu], wb[win][l], "ffn_dh")
        grads[gname][l] = dg.reshape(-1)
        return dxn if last else pin(dxn, (win, wout), l)

    for l in reversed(range(L)):
        s = saved[l]
        dx = ffn_back(s["x3"], s["ffn2"], dx, "norm_ffn2", "ffn2_w_in", "ffn2_w_out", l, "ffn2_mid")
        dx, = after_stages("ffn2", l, (dx,))
        grads["xattn_wo"][l] = matmul_tn(s["o"], dx, 1.0, "dw_sq")
        do = matmul_nt(dx, wb["xattn_wo"][l], "attn_do", out_dtype=BF16)
        dq, dk, dv = attn_bwd(s["q"], s["kv"], do, "attn_bwd")
        grads["xattn_wq"][l] = matmul_tn(s["hq"], dq, 1.0, "dw_sq")
        dx, dg = dh_norm_bwd(s["x2"], dx, row(ws["norm_xattn"][l]), [dq], wb["xattn_wq"][l], "attn_dh")
        grads["norm_xattn"][l] = dg.reshape(-1)
        dkv = jnp.concatenate([dk, dv], axis=1)
        grads["xattn_wkv"][l] = matmul_tn(s["mn"], dkv, 1.0, "attn_dwkv")
        dmn = matmul_nt(dkv, wb["xattn_wkv"][l], "attn_dmn")
        _, dg = rmsnorm_bwd(None, dmn, mem, row(ws["norm_mem"][l]), "norm_mem_bwd")
        grads["norm_mem"][l] = dg.reshape(-1)
        dx = pin(dx, ("xattn_wo", "xattn_wq", "xattn_wkv", "norm_mem"), l)
        dx, = after_stages("attn", l, (dx,))
        grads["mix_w_out"][l] = matmul_tn(s["y"], dx, 1.0, "dw_sq")
        dy = matmul_nt(dx, wb["mix_w_out"][l], "mix_dy")
        z = s["z"]
        dab, dac, dax, dws = mix_a_bwd(z, dy, ws["sconv_w"][l], "mix_a_bwd")
        dbu, dbv, dwsgu, dbs, dgs = mix_b_bwd(z, dy, row(ws["sgu_norm_g"][l]), ws["sgu_w"][l], s["bias"], "mix_b_bwd")
        dca, dcg, dwc, dgc, dbc = mix_c_bwd(z, dy, ws["cconv_w"][l], row(ws["cconv_ln_g"][l]),
                                            row(ws["cconv_ln_b"][l]), "mix_c_bwd")
        ddw, dpbd, dsc = mix_d_bwd(z, dy, s["pbd"], row(ws["pool_scale"][l]), "mix_d_bwd")
        grads["sconv_w"][l], grads["cconv_w"][l] = dws, dwc
        grads["sgu_w"][l], grads["sgu_b"][l], grads["sgu_norm_g"][l] = dwsgu, dbs[:, :H].T, dgs.reshape(-1)
        grads["cconv_ln_g"][l], grads["cconv_ln_b"][l] = dgc.reshape(-1), dbc.reshape(-1)
        gd = W // len(POOL_WINDOWS)
        grads["pool_w"][l] = jnp.stack([dpbd[g * gd:(g + 1) * gd, g * gd:(g + 1) * gd] for g in range(len(POOL_WINDOWS))])
        grads["pool_scale"][l] = dsc.reshape(-1)
        dz = jnp.concatenate([dab, dac, dax, dbu, dbv, dca, dcg, ddw], axis=1)
        grads["mix_w_in"][l] = matmul_tn(s["h_mix"], dz, 1.0, "mix_dwin")
        dx, dg = dh_norm_bwd(s["x1"], dx, row(ws["norm_mix"][l]), [dz], wb["mix_w_in"][l], "mix_dh")
        grads["norm_mix"][l] = dg.reshape(-1)
        dx = pin(dx, ("mix_w_out", "mix_w_in"), l)
        dx, = after_stages("mix", l, (dx,))
        dx = ffn_back(s["x0"], s["ffn1"], dx, "norm_ffn1", "ffn1_w_in", "ffn1_w_out", l, "ffn1_mid")
        dx, = after_stages("layer", l, (dx,))

    return loss[0, 0], dx, grads


ANY = pl.BlockSpec(memory_space=pl.ANY)


def _other_chips(x, y):
    return [(1 - x, y), (x, 1 - y), (1 - x, 1 - y)]


def _shard_slice(ref, axis, chip, size):
    idx = [slice(None)] * len(ref.shape)
    idx[axis] = pl.ds(pl.multiple_of(chip * size, size), size)
    return ref.at[tuple(idx)]


def all_gather_chips(shards, axes, name):
    n = len(shards)

    def body(*refs):
        ins, outs = refs[:n], refs[n:2 * n]
        send, recv, loc = refs[2 * n:]
        x, y, c = lax.axis_index("x"), lax.axis_index("y"), lax.axis_index("c")
        me = 2 * x + y
        chips = _other_chips(x, y)
        started = []
        for i in range(n):
            size = ins[i].shape[axes[i]]
            cp = pltpu.make_async_copy(ins[i], _shard_slice(outs[i], axes[i], me, size), loc.at[i])
            cp.start()
            started.append(cp)
        sends = []
        for i in range(n):
            size = ins[i].shape[axes[i]]
            for j, (px, py) in enumerate(chips):
                cp = pltpu.make_async_remote_copy(
                    src_ref=ins[i], dst_ref=_shard_slice(outs[i], axes[i], me, size),
                    send_sem=send.at[i, j], recv_sem=recv.at[i, j], device_id=(px, py, c), device_id_type=MESH_ID)
                cp.start()
                sends.append(cp)
        for i in range(n):
            size = ins[i].shape[axes[i]]
            for j, (px, py) in enumerate(chips):
                pltpu.make_async_remote_copy(
                    src_ref=ins[i], dst_ref=_shard_slice(outs[i], axes[i], 2 * px + py, size),
                    send_sem=send.at[i, j], recv_sem=recv.at[i, j], device_id=(px, py, c),
                    device_id_type=MESH_ID).wait_recv()
        for cp in sends:
            cp.wait_send()
        for cp in started:
            cp.wait()

    def full(a, ax):
        shape = list(a.shape)
        shape[ax] *= N_CHIPS
        return jax.ShapeDtypeStruct(tuple(shape), a.dtype)

    return pl.pallas_call(
        body, name=name, in_specs=[ANY] * n, out_specs=[ANY] * n,
        out_shape=[full(a, ax) for a, ax in zip(shards, axes)],
        scratch_shapes=[pltpu.SemaphoreType.DMA((n, 3)), pltpu.SemaphoreType.DMA((n, 3)),
                        pltpu.SemaphoreType.DMA((n,))],
        compiler_params=pltpu.CompilerParams(has_side_effects=True))(*shards)


def cast_into_slot(shard, axis, chip, name):
    L, K, N = shard.shape
    bm = _pick(K, 512, 16)
    full = (K * N_CHIPS, N) if axis == 1 else (K, N * N_CHIPS)
    nb = K // bm

    def body(c_ref, s_ref, *o_refs):
        for l in range(L):
            o_refs[l][...] = s_ref[l].astype(BF16)

    out_map = (lambda i, c: (c[0] * nb + i, 0)) if axis == 1 else (lambda i, c: (i, c[0]))
    spec = pltpu.PrefetchScalarGridSpec(
        num_scalar_prefetch=1, grid=(nb,),
        in_specs=[pl.BlockSpec((L, bm, N), lambda i, c: (0, i, 0))],
        out_specs=[pl.BlockSpec((bm, N), out_map)] * L)
    return pl.pallas_call(body, name=name, grid_spec=spec, out_shape=[jax.ShapeDtypeStruct(full, BF16)] * L,
                          compiler_params=_params("parallel"))(chip, shard)


HBM = pl.BlockSpec(memory_space=pltpu.HBM)
SEM = pl.BlockSpec(memory_space=pltpu.SEMAPHORE)
DATAFLOW = pltpu.SideEffectType.DATAFLOW_SIDE_EFFECTING


def split_start(name, sources, landing, n_sems, make, after):
    ns, nl, na = len(sources), len(landing), len(after)

    def body(*refs):
        out, _ = make(refs[:ns], refs[ns:ns + nl], refs[ns + nl + na], refs[ns + nl + na + 1])
        for cp in out:
            cp.start()
        refs[-1][...] = jnp.zeros_like(refs[-1])

    hbm = lambda b: pltpu.with_memory_space_constraint(b, pltpu.HBM)
    res = pl.pallas_call(
        body, name=name,
        out_shape=(pltpu.SemaphoreType.DMA((n_sems,)), pltpu.SemaphoreType.DMA((n_sems,)),
                   *[pltpu.HBM(b.shape, b.dtype) for b in landing], jax.ShapeDtypeStruct((8, 128), F32)),
        in_specs=[HBM] * (ns + nl) + [ANY] * na, out_specs=(SEM, SEM, *[HBM] * nl, pl.BlockSpec(memory_space=pltpu.VMEM)),
        input_output_aliases={ns + i: 2 + i for i in range(nl)},
        compiler_params=pltpu.CompilerParams(has_side_effects=DATAFLOW))(
            *[hbm(b) for b in sources], *[hbm(b) for b in landing], *after)
    return res[0], res[1], list(res[2:2 + nl]), res[-1]


def split_wait(name, sources, landing, send, recv, make, after):
    ns, nl = len(sources), len(landing)

    def body(*refs):
        _, back = make(refs[:ns], refs[ns:ns + nl], refs[ns + nl], refs[ns + nl + 1])
        for cp in back:
            cp.wait_send()
            cp.wait_recv()

    return list(pl.pallas_call(
        body, name=name, out_shape=tuple(pltpu.HBM(b.shape, b.dtype) for b in landing),
        in_specs=[HBM] * (ns + nl) + [SEM, SEM] + [ANY] * len(after), out_specs=tuple([HBM] * nl),
        input_output_aliases={ns + i: i for i in range(nl)},
        compiler_params=pltpu.CompilerParams(has_side_effects=DATAFLOW))(
            *[pltpu.with_memory_space_constraint(b, pltpu.HBM) for b in sources], *landing, send, recv, *after))


def _half_slot(buf, axis, chip, half):
    K, N = buf.shape
    if axis == 1:
        n = N // N_CHIPS
        return buf.at[pl.ds(pl.multiple_of(half * (K // 2), K // 2), K // 2), pl.ds(pl.multiple_of(chip * n, n), n)]
    k2 = K // N_CHIPS // 2
    return buf.at[pl.ds(pl.multiple_of((2 * chip + half) * k2, k2), k2), :]


def gather_copies(axes):
    def make(_, bufs, send, recv):
        x, y, c = lax.axis_index("x"), lax.axis_index("y"), lax.axis_index("c")
        out, back = [], []
        for i, (buf, ax) in enumerate(zip(bufs, axes)):
            mine = _half_slot(buf, ax, 2 * x + y, c)
            for j, (px, py) in enumerate(_other_chips(x, y)):
                kw = dict(send_sem=send.at[3 * i + j], recv_sem=recv.at[3 * i + j], device_id=(px, py, c),
                          device_id_type=MESH_ID)
                out.append(pltpu.make_async_remote_copy(src_ref=mine, dst_ref=mine, **kw))
                back.append(pltpu.make_async_remote_copy(src_ref=mine, dst_ref=_half_slot(buf, ax, 2 * px + py, c), **kw))
        return out, back
    return make


def forward_copies(axes):
    def make(_, bufs, send, recv):
        x, y, c = lax.axis_index("x"), lax.axis_index("y"), lax.axis_index("c")
        out, back = [], []
        for i, (buf, ax) in enumerate(zip(bufs, axes)):
            for j, (px, py) in enumerate(_other_chips(x, y)):
                have = _half_slot(buf, ax, 2 * px + py, c)
                kw = dict(send_sem=send.at[3 * i + j], recv_sem=recv.at[3 * i + j], device_id=(x, y, 1 - c),
                          device_id_type=MESH_ID)
                out.append(pltpu.make_async_remote_copy(src_ref=have, dst_ref=have, **kw))
                back.append(pltpu.make_async_remote_copy(src_ref=have, dst_ref=_half_slot(buf, ax, 2 * px + py, 1 - c), **kw))
        return out, back
    return make


def forward_sibling(bufs, axes, name):
    n = len(bufs)

    def body(*refs):
        out, back = forward_copies(axes)(None, refs[:n], *refs[2 * n:])
        for cp in out:
            cp.start()
        for cp in back:
            cp.wait_recv()
        for cp in out:
            cp.wait_send()

    return pl.pallas_call(
        body, name=name, in_specs=[ANY] * n, out_specs=[ANY] * n,
        out_shape=[jax.ShapeDtypeStruct(b.shape, b.dtype) for b in bufs],
        input_output_aliases={i: i for i in range(n)},
        scratch_shapes=[pltpu.SemaphoreType.DMA((3 * n,)), pltpu.SemaphoreType.DMA((3 * n,))],
        compiler_params=pltpu.CompilerParams(has_side_effects=True))(*bufs)


def all_reduce_small(p, name):
    R = p.shape[0]

    def body(p_ref, o_ref, sib_ref, chip_ref, send, recv):
        x, y, c = lax.axis_index("x"), lax.axis_index("y"), lax.axis_index("c")
        me = 2 * x + y
        chips = _other_chips(x, y)
        pair = pltpu.make_async_remote_copy(src_ref=p_ref, dst_ref=sib_ref, send_sem=send.at[0], recv_sem=recv.at[0],
                                            device_id=(x, y, 1 - c), device_id_type=MESH_ID)
        pair.start()
        pair.wait()
        chip_ref[me] = p_ref[...] + sib_ref[...]
        sends = []
        for j, (px, py) in enumerate(chips):
            cp = pltpu.make_async_remote_copy(src_ref=chip_ref.at[me], dst_ref=chip_ref.at[me], send_sem=send.at[1 + j],
                                              recv_sem=recv.at[1 + j], device_id=(px, py, c), device_id_type=MESH_ID)
            cp.start()
            sends.append(cp)
        for j, (px, py) in enumerate(chips):
            pltpu.make_async_remote_copy(src_ref=chip_ref.at[me], dst_ref=chip_ref.at[2 * px + py], send_sem=send.at[1 + j],
                                         recv_sem=recv.at[1 + j], device_id=(px, py, c), device_id_type=MESH_ID).wait_recv()
        for cp in sends:
            cp.wait_send()
        o_ref[...] = ((chip_ref[0] + chip_ref[1]) + chip_ref[2]) + chip_ref[3]

    vm = pl.BlockSpec(memory_space=pltpu.VMEM)
    return pl.pallas_call(
        body, name=name, in_specs=[vm], out_specs=vm, out_shape=jax.ShapeDtypeStruct((R, 128), F32),
        scratch_shapes=[pltpu.VMEM((R, 128), F32), pltpu.VMEM((N_CHIPS, R, 128), F32),
                        pltpu.SemaphoreType.DMA((4,)), pltpu.SemaphoreType.DMA((4,))],
        compiler_params=pltpu.CompilerParams(has_side_effects=True, vmem_limit_bytes=VMEM_LIMIT))(p)


def _grad_view(g, axis):
    K, N = g.shape
    return g.reshape(1, 2, K // 2, N) if axis == 1 else g.reshape(N_CHIPS, 2, K // N_CHIPS // 2, N)


def pair_copies(gvs, others, send, recv):
    x, y, c = lax.axis_index("x"), lax.axis_index("y"), lax.axis_index("c")
    out = [pltpu.make_async_remote_copy(src_ref=gv.at[:, 1 - c], dst_ref=o, send_sem=send.at[i], recv_sem=recv.at[i],
                                        device_id=(x, y, 1 - c), device_id_type=MESH_ID)
           for i, (gv, o) in enumerate(zip(gvs, others))]
    return out, out


def chip_copies(axes):
    def piece(s, ax, chip):
        if ax == 1:
            n = s.shape[2] // N_CHIPS
            return s.at[0, :, pl.ds(pl.multiple_of(chip * n, n), n)]
        return s.at[chip]

    def make(sums, qs, send, recv):
        x, y, c = lax.axis_index("x"), lax.axis_index("y"), lax.axis_index("c")
        out = []
        for i, (s, q, ax) in enumerate(zip(sums, qs, axes)):
            for j, (px, py) in enumerate(_other_chips(x, y)):
                out.append(pltpu.make_async_remote_copy(
                    src_ref=piece(s, ax, 2 * px + py), dst_ref=q.at[j], send_sem=send.at[3 * i + j],
                    recv_sem=recv.at[3 * i + j], device_id=(px, py, c), device_id_type=MESH_ID))
        return out, out
    return make


def share_copies(_, halves, send, recv):
    x, y, c = lax.axis_index("x"), lax.axis_index("y"), lax.axis_index("c")
    kw = lambda i: dict(send_sem=send.at[i], recv_sem=recv.at[i], device_id=(x, y, 1 - c), device_id_type=MESH_ID)
    out = [pltpu.make_async_remote_copy(src_ref=h.at[c], dst_ref=h.at[c], **kw(i)) for i, h in enumerate(halves)]
    back = [pltpu.make_async_remote_copy(src_ref=h.at[c], dst_ref=h.at[1 - c], **kw(i)) for i, h in enumerate(halves)]
    return out, back


def share_sibling(halves, name):
    n = len(halves)

    def body(*refs):
        out, back = share_copies(None, refs[:n], *refs[2 * n:])
        for cp in out:
            cp.start()
        for cp in back:
            cp.wait_recv()
        for cp in out:
            cp.wait_send()

    return pl.pallas_call(
        body, name=name, in_specs=[ANY] * n, out_specs=[ANY] * n,
        out_shape=[jax.ShapeDtypeStruct(a.shape, a.dtype) for a in halves],
        input_output_aliases={i: i for i in range(n)},
        scratch_shapes=[pltpu.SemaphoreType.DMA((n,)), pltpu.SemaphoreType.DMA((n,))],
        compiler_params=pltpu.CompilerParams(has_side_effects=True))(*halves)


def add_pair(gv, other, place, name):
    A, _, rows, N = gv.shape
    bm, bn = _pick(rows, 512, 16), _pick(N, 1408, 128)

    def body(p_ref, g_ref, o_ref, out_ref):
        out_ref[...] = (g_ref[...] + o_ref[...]).astype(GRAD_WIRE)

    spec = pltpu.PrefetchScalarGridSpec(
        num_scalar_prefetch=1, grid=(A, rows // bm, N // bn),
        in_specs=[pl.BlockSpec((None, None, bm, bn), lambda a, i, j, p: (a, p[1], i, j)),
                  pl.BlockSpec((None, bm, bn), lambda a, i, j, p: (a, i, j))],
        out_specs=pl.BlockSpec((None, bm, bn), lambda a, i, j, p: (a, i, j)))
    return pl.pallas_call(body, name=name, grid_spec=spec, out_shape=jax.ShapeDtypeStruct((A, rows, N), GRAD_WIRE),
                          compiler_params=_params("parallel", "parallel", "parallel"))(place, gv, other)


def add_chips(s, q, axis, place, name):
    _, rows, n = q.shape
    bm, bn = _pick(rows, 512, 16), _pick(n, 1408, 128)
    nbj = n // bn

    def body(p_ref, s_ref, q_ref, o_ref):
        o_ref[...] = ((s_ref[...].astype(F32) + q_ref[0].astype(F32)) + q_ref[1].astype(F32)) + q_ref[2].astype(F32)

    mine = (lambda i, j, p: (p[0], i, j)) if axis == 0 else (lambda i, j, p: (0, i, p[0] * nbj + j))
    spec = pltpu.PrefetchScalarGridSpec(
        num_scalar_prefetch=1, grid=(rows // bm, nbj),
        in_specs=[pl.BlockSpec((None, bm, bn), mine), pl.BlockSpec((3, bm, bn), lambda i, j, p: (0, i, j))],
        out_specs=pl.BlockSpec((None, bm, bn), lambda i, j, p: (p[1], i, j)))
    return pl.pallas_call(body, name=name, grid_spec=spec, out_shape=jax.ShapeDtypeStruct((2, rows, n), F32),
                          compiler_params=_params("parallel", "parallel"))(place, s, q)


def adamw(w, g, m, v, name):
    R, N = w.shape
    bm = _pick(R, 512, 8)
    c1 = 1.0 / (1.0 - ADAM_B1 ** ADAM_STEP)
    c2 = 1.0 / (1.0 - ADAM_B2 ** ADAM_STEP)

    def body(w_ref, g_ref, m_ref, v_ref, d_ref, nm_ref, nv_ref):
        gv = g_ref[...]
        nm = ADAM_B1 * m_ref[...] + (1.0 - ADAM_B1) * gv
        nv = ADAM_B2 * v_ref[...] + (1.0 - ADAM_B2) * (gv * gv)
        nm_ref[...] = nm
        nv_ref[...] = nv
        d_ref[...] = -ADAM_LR * ((nm * c1) / (jnp.sqrt(nv * c2) + ADAM_EPS) + ADAM_WD * w_ref[...])

    blk = pl.BlockSpec((bm, N), lambda i: (i, 0))
    out = jax.ShapeDtypeStruct((R, N), F32)
    return pl.pallas_call(body, name=name, grid=(R // bm,), in_specs=[blk] * 4, out_specs=[blk] * 3,
                          out_shape=[out, out, out], compiler_params=_params("parallel"))(w, g, m, v)


def adamw_layers(w, g0, g1, m, v, name):
    _, k, n = w.shape
    bm = _pick(k, 256, 8)
    c1 = 1.0 / (1.0 - ADAM_B1 ** ADAM_STEP)
    c2 = 1.0 / (1.0 - ADAM_B2 ** ADAM_STEP)

    def body(w_ref, g0_ref, g1_ref, m_ref, v_ref, g_ref, d_ref, nm_ref, nv_ref):
        def step(gv):
            nm = ADAM_B1 * m_ref[...] + (1.0 - ADAM_B1) * gv
            nv = ADAM_B2 * v_ref[...] + (1.0 - ADAM_B2) * (gv * gv)
            g_ref[...] = gv
            nm_ref[...] = nm
            nv_ref[...] = nv
            d_ref[...] = -ADAM_LR * ((nm * c1) / (jnp.sqrt(nv * c2) + ADAM_EPS) + ADAM_WD * w_ref[...])

        @pl.when(pl.program_id(0) == 0)
        def _():
            step(g0_ref[...])

        @pl.when(pl.program_id(0) == 1)
        def _():
            step(g1_ref[...])

    blk = pl.BlockSpec((None, bm, n), lambda l, i: (l, i, 0))
    out = jax.ShapeDtypeStruct(w.shape, F32)
    return pl.pallas_call(
        body, name=name, grid=(2, k // bm),
        in_specs=[blk, pl.BlockSpec((bm, n), lambda l, i: (i * (1 - l), 0)), pl.BlockSpec((bm, n), lambda l, i: (i * l, 0)),
                  blk, blk],
        out_specs=[blk] * 4, out_shape=[out] * 4, compiler_params=_params("arbitrary", "arbitrary"))(w, g0, g1, m, v)


def _pack(arrays):
    flat = jnp.concatenate([a.reshape(-1) for a in arrays])
    rows = -(-flat.shape[0] // (256 * 128)) * 256
    return jnp.pad(flat, (0, rows * 128 - flat.shape[0])).reshape(rows, 128)


def _unpack(p, shapes):
    flat, out, at = p.reshape(-1), [], 0
    for s in shapes:
        n = math.prod(s)
        out.append(flat[at:at + n].reshape(s))
        at += n
    return out


def kernel(x, mem, norm_ffn1, ffn1_w_in, ffn1_w_out, norm_mix, mix_w_in, sconv_w, sgu_norm_g, sgu_w, sgu_b, cconv_w, cconv_ln_g, cconv_ln_b, pool_w, pool_scale, mix_w_out, norm_xattn, norm_mem, xattn_wq, xattn_wkv, xattn_wo, norm_ffn2, ffn2_w_in, ffn2_w_out, norm_final, loss_target, m_norm_ffn1, m_ffn1_w_in, m_ffn1_w_out, m_norm_mix, m_mix_w_in, m_sconv_w, m_sgu_norm_g, m_sgu_w, m_sgu_b, m_cconv_w, m_cconv_ln_g, m_cconv_ln_b, m_pool_w, m_pool_scale, m_mix_w_out, m_norm_xattn, m_norm_mem, m_xattn_wq, m_xattn_wkv, m_xattn_wo, m_norm_ffn2, m_ffn2_w_in, m_ffn2_w_out, m_norm_final, v_norm_ffn1, v_ffn1_w_in, v_ffn1_w_out, v_norm_mix, v_mix_w_in, v_sconv_w, v_sgu_norm_g, v_sgu_w, v_sgu_b, v_cconv_w, v_cconv_ln_g, v_cconv_ln_b, v_pool_w, v_pool_scale, v_mix_w_out, v_norm_xattn, v_norm_mem, v_xattn_wq, v_xattn_wkv, v_xattn_wo, v_norm_ffn2, v_ffn2_w_in, v_ffn2_w_out, v_norm_final):
    given = dict(locals())
    w = {n: given[n] for n in WEIGHTS}
    L = ffn1_w_in.shape[0]
    assert L == 2, "the reduce-scatter gives one layer to each core of a chip"
    chip = 2 * lax.axis_index("x") + lax.axis_index("y")
    chip1 = chip.astype(jnp.int32).reshape(1)
    core = lax.axis_index("c").astype(jnp.int32).reshape(1)
    place = jnp.concatenate([chip1, core])

    axis = {n: 1 if n in COL_SHARDED else 0 for n in BIG}
    bufs = {}
    for n in BIG:
        for l, b in enumerate(cast_into_slot(w[n], axis[n] + 1, chip1, "cast_weights")):
            bufs[n, l] = b
    groups = {"a": [(n, 0) for n in BIG[:2]], "b": [(n, 0) for n in BIG[2:4]], "c": [(n, 0) for n in BIG[4:7]],
              "d": [(n, 0) for n in BIG[7:]], "e": [(n, 1) for n in BIG[:2]], "f": [(n, 1) for n in BIG[2:]]}
    wc = sconv_w.shape[-1]
    conv_rows = [w[n].reshape(-1, wc) for n in SMALL_CONV]
    n_conv = sum(r.shape[0] for r in conv_rows)
    conv_pack = jnp.pad(jnp.concatenate(conv_rows, axis=0), ((0, -n_conv % 8), (0, 128 - wc)))[None]
    conv_all = all_gather_chips([conv_pack], [0], "gather_conv")[0]
    started, token = {}, conv_all
    for g, keys in groups.items():
        send, recv, thru, token = split_start("gather_start_" + g, [], [bufs[k] for k in keys], 3 * len(keys),
                                              gather_copies([axis[k[0]] for k in keys]), [token])
        started[g] = (send, recv, thru)
    ready, ahead = {}, {}
    small = SMALL_REPL + SMALL_CONV
    packed = [_pack([src[n] for n in small]) for src in
              (w, {n: given["m_" + n] for n in small}, {n: given["v_" + n] for n in small})]

    def arrived(g, after):
        send, recv, thru = started[g]
        axes = [axis[k[0]] for k in groups[g]]
        return axes, split_wait("gather_wait_" + g, [], thru, send, recv, gather_copies(axes), after)

    def fetch(n, l, after):
        g = next(g for g, keys in groups.items() if (n, l) in keys)
        if g not in ready:
            if g in ahead:
                axes, send, recv, thru, token_g = ahead.pop(g)
                done = split_wait("forward_wait_" + g, [], thru, send, recv, forward_copies(axes), [after, token_g])
            else:
                axes, done = arrived(g, [token] + packed if g == "a" else [after])
                done = forward_sibling(done, axes, "gather_forward")
            if g == "e":
                axes_f, landed = arrived("f", [after])
                send, recv, thru, token_f = split_start("forward_start_f", [], landed, 3 * len(landed),
                                                        forward_copies(axes_f), [after])
                done, (token_f,) = lax.optimization_barrier((done, [token_f]))
                ahead["f"] = (axes_f, send, recv, thru, token_f)
            ready[g] = dict(zip(groups[g], done))
        return ready[g][n, l]

    conv_full = jnp.moveaxis(conv_all[:, :n_conv, :wc], 0, 1).reshape(n_conv, N_CHIPS * wc)
    ws = {n: w[n] for n in SMALL_REPL}
    at = 0
    for n in SMALL_CONV:
        rows = w[n].shape[0] * w[n].shape[1]
        ws[n] = conv_full[at:at + rows].reshape(w[n].shape[0], w[n].shape[1], N_CHIPS * wc)
        at += rows

    halves, state = {}, {}
    reduce_groups = {"r1": [(n, 1) for n in BIG], "r0a": [(n, 0) for n in BIG[2:]], "r0b": [(n, 0) for n in BIG[:2]]}
    plan = {("layer", 1): [("pair", "r1")],
            ("ffn2", 0): [("chips", "r1")],
            ("mix", 0): [("finish", "r1"), ("share", "r1"), ("pair", "r0a")],
            ("ffn1_mid", 0): [("chips", "r0a")],
            ("ffn1_grads", 0): [("pair", "r0b")],
            ("layer", 0): [("finish", "r0a")]}


    def stage_pair(g, keys, grads, after):
        gvs = [_grad_view(grads[n][l], axis[n]) for n, l in keys]
        others = [lax.empty(gv.shape[:1] + gv.shape[2:], F32) for gv in gvs]
        send, recv, others, token = split_start("pair_start_" + g, gvs, others, len(gvs), pair_copies, [after])
        state[g] = dict(sources=gvs, send=send, recv=recv, landing=others, token=token)
        return [(state[g], "token")]

    def stage_chips(g, keys, grads, after):
        st = state[g]
        axes = [axis[n] for n, _ in keys]
        others = split_wait("pair_wait_" + g, st["sources"], st["landing"], st["send"], st["recv"], pair_copies,
                            [after, st["token"]])
        sums = [add_pair(gv, o, place, "add_pair") for gv, o in zip(st["sources"], others)]
        qs = [lax.empty((3, s.shape[1], s.shape[2] // (N_CHIPS if ax == 1 else 1)), GRAD_WIRE) for s, ax in zip(sums, axes)]
        send, recv, qs, token = split_start("chips_start_" + g, sums, qs, 3 * len(sums), chip_copies(axes), [after])
        state[g] = dict(sources=sums, send=send, recv=recv, landing=qs, token=token)
        return [(state[g], "token")]

    def stage_finish(g, keys, grads, after):
        st = state.pop(g)
        qs = split_wait("chips_wait_" + g, st["sources"], st["landing"], st["send"], st["recv"],
                        chip_copies([axis[n] for n, _ in keys]), [after, st["token"]])
        for key, s, q in zip(keys, st["sources"], qs):
            halves[key] = add_chips(s, q, axis[key[0]], place, "add_chips")
        return [(halves, key) for key in keys]

    def stage_share(g, keys, grads, after):
        send, recv, thru, token = split_start("share_start_" + g, [], [halves.pop(k) for k in keys], len(keys),
                                              share_copies, [after])
        state["share_" + g] = dict(keys=keys, send=send, recv=recv, landing=thru, token=token)
        return [(state["share_" + g], "token")]

    stages = {"pair": stage_pair, "chips": stage_chips, "finish": stage_finish, "share": stage_share}

    def progress(event, l, grads, values):
        places = []
        for stage, g in plan.get((event, l), []):
            places += stages[stage](g, reduce_groups[g], grads, values[0])
        places = [(box, k) for box, k in places if k in box]
        if places:
            values, tied = lax.optimization_barrier((values, [box[k] for box, k in places]))
            for (box, k), a in zip(places, tied):
                box[k] = a
        return values

    loss_part, grad_x, grads = _local_step(x[0], mem[0], loss_target[0], fetch, ws, L, progress)
    loss = lax.psum(loss_part, ("x", "y", "c"))

    small_g = [grads[n] if n == "norm_final" else jnp.stack(grads[n]) for n in small]
    small_sum = all_reduce_small(_pack(small_g), "reduce_small")
    grad = dict(zip(small, _unpack(small_sum, [g.shape for g in small_g])))
    for n in SMALL_CONV:
        grad[n] = lax.dynamic_slice_in_dim(grad[n], chip * wc, wc, axis=2)

    delta, new_m, new_v = {}, {}, {}

    sent = state.pop("share_r1")
    layer1 = dict(zip(sent["keys"], split_wait("share_wait_r1", [], sent["landing"], sent["send"], sent["recv"],
                                               share_copies, [small_sum, sent["token"]])))

    def finish_weights(names, layer0_halves):
        layer0 = dict(zip(names, share_sibling(layer0_halves, "share_pair")))
        for n in names:
            g0, g1 = layer0[n].reshape(w[n].shape[1:]), layer1[n, 1].reshape(w[n].shape[1:])
            grad[n], delta[n], new_m[n], new_v[n] = adamw_layers(w[n], g0, g1, given["m_" + n], given["v_" + n], "adamw")

    stage_chips("r0b", reduce_groups["r0b"], grads, small_sum)
    early, (state["r0b"]["token"],) = lax.optimization_barrier(([halves[n, 0] for n in BIG[2:]], [state["r0b"]["token"]]))
    finish_weights(BIG[2:], early)
    stage_finish("r0b", reduce_groups["r0b"], grads, delta[BIG[-1]])
    finish_weights(BIG[:2], [halves[n, 0] for n in BIG[:2]])
    shapes = [w[n].shape for n in small]
    packed.insert(1, _pack([grad[n] for n in small]))
    for out, p in zip((delta, new_m, new_v), adamw(*packed, "adamw_small")):
        out.update(zip(small, _unpack(p, shapes)))

    return (loss, grad_x[None], *[grad[n] for n in WEIGHTS], *[delta[n] for n in WEIGHTS],
            *[new_m[n] for n in WEIGHTS], *[new_v[n] for n in WEIGHTS])
```

```python
import functools
import math

import jax
import jax.numpy as jnp
from jax import lax
from jax.experimental import pallas as pl
from jax.experimental.pallas import tpu as pltpu

F32 = jnp.float32
BF16 = jnp.bfloat16
EPS = 1e-6
SEQ_CHUNK = 128
POOL_WINDOWS = (2, 4, 8, 16)
N_HEADS = 4
ADAM_LR, ADAM_B1, ADAM_B2, ADAM_EPS, ADAM_WD, ADAM_STEP = 0.001, 0.9, 0.999, 1e-08, 0.01, 10
VMEM_LIMIT = 56 * 1024 * 1024
MESH_ID = pl.DeviceIdType.MESH
N_CHIPS = 4
GRAD_WIRE = BF16

BIG = ("ffn1_w_in", "ffn1_w_out", "mix_w_in", "mix_w_out", "xattn_wq", "xattn_wkv", "xattn_wo",
       "ffn2_w_in", "ffn2_w_out")
COL_SHARDED = ("ffn1_w_in", "mix_w_in", "xattn_wkv", "ffn2_w_in")
SMALL_CONV = ("sconv_w", "cconv_w")
SMALL_REPL = ("norm_ffn1", "norm_mix", "sgu_norm_g", "sgu_w", "sgu_b", "cconv_ln_g", "cconv_ln_b",
              "pool_w", "pool_scale", "norm_xattn", "norm_mem", "norm_ffn2", "norm_final")
WEIGHTS = ("norm_ffn1", "ffn1_w_in", "ffn1_w_out", "norm_mix", "mix_w_in", "sconv_w", "sgu_norm_g",
           "sgu_w", "sgu_b", "cconv_w", "cconv_ln_g", "cconv_ln_b", "pool_w", "pool_scale",
           "mix_w_out", "norm_xattn", "norm_mem", "xattn_wq", "xattn_wkv", "xattn_wo", "norm_ffn2",
           "ffn2_w_in", "ffn2_w_out", "norm_final")


def _pick(n, pref, align):
    best = None
    for d in range(align, min(n, pref) + 1, align):
        if n % d == 0:
            best = d
    return best or n


def _sig(x):
    return 0.5 * jnp.tanh(0.5 * x) + 0.5


def _nt(a, b):
    return lax.dot_general(a, b, (((1,), (1,)), ((), ())), preferred_element_type=F32)


def _tn(a, b):
    return lax.dot_general(a, b, (((0,), (0,)), ((), ())), preferred_element_type=F32)


def _params(*sem):
    return pltpu.CompilerParams(dimension_semantics=sem, vmem_limit_bytes=VMEM_LIMIT)


def norm_matmul(x, g, w, name, out_dtype=F32):
    T, D = x.shape
    N = w.shape[1]
    tm, tn = _pick(T, 512, 8), _pick(N, 2048, 128)

    def body(x_ref, g_ref, w_ref, o_ref, h_ref):
        j = pl.program_id(1)

        @pl.when(j == 0)
        def _():
            xv = x_ref[...]
            r = lax.rsqrt(jnp.mean(xv * xv, axis=-1, keepdims=True) + EPS)
            h_ref[...] = (xv * r * g_ref[...]).astype(BF16)

        o_ref[...] = jnp.dot(h_ref[...], w_ref[...], preferred_element_type=F32).astype(out_dtype)

    return pl.pallas_call(
        body, name=name, grid=(T // tm, N // tn),
        in_specs=[pl.BlockSpec((tm, D), lambda i, j: (i, 0)), pl.BlockSpec((1, D), lambda i, j: (0, 0)),
                  pl.BlockSpec((D, tn), lambda i, j: (0, j))],
        out_specs=[pl.BlockSpec((tm, tn), lambda i, j: (i, j)), pl.BlockSpec((tm, D), lambda i, j: (i, 0))],
        out_shape=[jax.ShapeDtypeStruct((T, N), out_dtype), jax.ShapeDtypeStruct((T, D), BF16)],
        compiler_params=_params("parallel", "arbitrary"))(x, g, w)


def matmul_res(res, a, w, name):
    T, K = a.shape
    N = w.shape[1]
    tm, tn = _pick(T, 512, 8), _pick(N, 1024, 128)

    def body(r_ref, a_ref, w_ref, o_ref):
        o_ref[...] = r_ref[...] + jnp.dot(a_ref[...].astype(BF16), w_ref[...], preferred_element_type=F32)

    return pl.pallas_call(
        body, name=name, grid=(T // tm, N // tn),
        in_specs=[pl.BlockSpec((tm, tn), lambda i, j: (i, j)), pl.BlockSpec((tm, K), lambda i, j: (i, 0)),
                  pl.BlockSpec((K, tn), lambda i, j: (0, j))],
        out_specs=pl.BlockSpec((tm, tn), lambda i, j: (i, j)),
        out_shape=jax.ShapeDtypeStruct((T, N), F32),
        compiler_params=_params("parallel", "parallel"))(res, a, w)


def matmul_nt(a, w, name, out_dtype=F32):
    T, N = a.shape
    M = w.shape[0]
    tm, tmm = _pick(T, 512, 8), _pick(M, 1024, 128)

    def body(a_ref, w_ref, o_ref):
        o_ref[...] = _nt(a_ref[...].astype(BF16), w_ref[...]).astype(out_dtype)

    return pl.pallas_call(
        body, name=name, grid=(T // tm, M // tmm),
        in_specs=[pl.BlockSpec((tm, N), lambda i, j: (i, 0)), pl.BlockSpec((tmm, N), lambda i, j: (j, 0))],
        out_specs=pl.BlockSpec((tm, tmm), lambda i, j: (i, j)),
        out_shape=jax.ShapeDtypeStruct((T, M), out_dtype),
        compiler_params=_params("parallel", "parallel"))(a, w)


def matmul_tn(a, b, scale, name, b2=None):
    T, M = a.shape
    Nb = b.shape[1]
    bm, bn, bk = _pick(M, 1408, 128), _pick(Nb, 2816, 128), _pick(T, 512, 8)
    nk, nj = T // bk, Nb // bn

    def body(a_ref, b_ref, *rest):
        o_ref = rest[-1]
        j, k = pl.program_id(1), pl.program_id(2)

        @pl.when(k == 0)
        def _():
            o_ref[...] = jnp.zeros_like(o_ref)

        a_blk = a_ref[...].astype(BF16)
        if b2 is None:
            o_ref[...] += _tn(a_blk, b_ref[...].astype(BF16))
        else:
            @pl.when(j < nj)
            def _():
                o_ref[...] += _tn(a_blk, b_ref[...].astype(BF16))

            @pl.when(j >= nj)
            def _():
                o_ref[...] += _tn(a_blk, rest[0][...].astype(BF16))

        if scale != 1.0:
            @pl.when(k == nk - 1)
            def _():
                o_ref[...] = o_ref[...] * scale

    if b2 is None:
        b_specs, operands, n_out = [pl.BlockSpec((bk, bn), lambda i, j, k: (k, j))], (a, b), nj
    else:
        first = lambda i, j, k: (jnp.where(j < nj, k, 0), jnp.where(j < nj, j, 0))
        second = lambda i, j, k: (jnp.where(j >= nj, k, 0), jnp.where(j >= nj, j - nj, 0))
        b_specs, operands, n_out = [pl.BlockSpec((bk, bn), first), pl.BlockSpec((bk, bn), second)], (a, b, b2), 2 * nj
    return pl.pallas_call(
        body, name=name, grid=(M // bm, n_out, nk),
        in_specs=[pl.BlockSpec((bk, bm), lambda i, j, k: (k, i))] + b_specs,
        out_specs=pl.BlockSpec((bm, bn), lambda i, j, k: (i, j)),
        out_shape=jax.ShapeDtypeStruct((M, n_out * bn), F32),
        compiler_params=_params("parallel", "parallel", "arbitrary"))(*operands)


def rmsnorm_bwd(dxo, dh, x, g, name):
    T, D = x.shape
    tm = _pick(T, 512, 8)
    has_res = dxo is not None

    def body(*refs):
        if has_res:
            dxo_ref, dh_ref, x_ref, g_ref, dx_ref, dg_ref = refs
        else:
            dh_ref, x_ref, g_ref, dx_ref, dg_ref = refs
        i = pl.program_id(0)

        @pl.when(i == 0)
        def _():
            dg_ref[...] = jnp.zeros_like(dg_ref)

        xv, dh_v = x_ref[...], dh_ref[...]
        r = lax.rsqrt(jnp.mean(xv * xv, axis=-1, keepdims=True) + EPS)
        xh = xv * r
        dg_ref[...] += jnp.sum(dh_v * xh, axis=0, keepdims=True)
        dxh = dh_v * g_ref[...]
        dx = r * (dxh - xh * jnp.mean(dxh * xh, axis=-1, keepdims=True))
        dx_ref[...] = dx + dxo_ref[...] if has_res else dx

    tile = pl.BlockSpec((tm, D), lambda i: (i, 0))
    vec = pl.BlockSpec((1, D), lambda i: (0, 0))
    args = ([dxo] if has_res else []) + [dh, x, g]
    return pl.pallas_call(
        body, name=name, grid=(T // tm,),
        in_specs=[tile] * (len(args) - 1) + [vec],
        out_specs=[tile, vec],
        out_shape=[jax.ShapeDtypeStruct((T, D), F32), jax.ShapeDtypeStruct((1, D), F32)],
        compiler_params=_params("arbitrary"))(*args)


def ffn_fwd(x, g, w_in, w_out, name):
    T, D = x.shape
    F = w_out.shape[0]
    tm, tf = _pick(T, 512, 8), _pick(F, 1408, 128)
    nf = F // tf

    def body(x_ref, g_ref, wg_ref, wu_ref, wo_ref, o_ref, h_ref, zg_ref, zu_ref, acc_ref):
        j = pl.program_id(1)

        @pl.when(j == 0)
        def _():
            xv = x_ref[...]
            r = lax.rsqrt(jnp.mean(xv * xv, axis=-1, keepdims=True) + EPS)
            h_ref[...] = (xv * r * g_ref[...]).astype(BF16)
            acc_ref[...] = jnp.zeros_like(acc_ref)

        h = h_ref[...]
        zg = jnp.dot(h, wg_ref[...], preferred_element_type=F32)
        zu = jnp.dot(h, wu_ref[...], preferred_element_type=F32)
        zg_ref[...] = zg.astype(BF16)
        zu_ref[...] = zu.astype(BF16)
        a = (zg * _sig(zg) * zu).astype(BF16)
        acc_ref[...] += jnp.dot(a, wo_ref[...], preferred_element_type=F32)

        @pl.when(j == nf - 1)
        def _():
            o_ref[...] = x_ref[...] + 0.5 * acc_ref[...]

    tile = pl.BlockSpec((tm, D), lambda i, j: (i, 0))
    fblk = pl.BlockSpec((tm, tf), lambda i, j: (i, j))
    hidden = jax.ShapeDtypeStruct((T, F), BF16)
    return pl.pallas_call(
        body, name=name, grid=(T // tm, nf),
        in_specs=[tile, pl.BlockSpec((1, D), lambda i, j: (0, 0)),
                  pl.BlockSpec((D, tf), lambda i, j: (0, j)), pl.BlockSpec((D, tf), lambda i, j: (0, j + nf)),
                  pl.BlockSpec((tf, D), lambda i, j: (j, 0))],
        out_specs=[tile, tile, fblk, fblk],
        out_shape=[jax.ShapeDtypeStruct((T, D), F32), jax.ShapeDtypeStruct((T, D), BF16), hidden, hidden],
        scratch_shapes=[pltpu.VMEM((tm, D), F32)],
        compiler_params=_params("parallel", "arbitrary"))(x, g, w_in, w_in, w_out)


def ffn_dz(dxo, zg, zu, w_out, name):
    T, D = dxo.shape
    F = w_out.shape[0]
    tm, tf = _pick(T, 512, 8), _pick(F, 256, 128)

    def body(dxo_ref, zg_ref, zu_ref, wo_ref, a_ref, dzg_ref, dzu_ref):
        do = (0.5 * dxo_ref[...]).astype(BF16)
        for j in range(F // tf):
            cols = slice(j * tf, (j + 1) * tf)
            zg, zu = zg_ref[:, cols].astype(F32), zu_ref[:, cols].astype(F32)
            s = _sig(zg)
            silu = zg * s
            a_ref[:, cols] = (silu * zu).astype(BF16)
            da = _nt(do, wo_ref[cols, :])
            dzu_ref[:, cols] = (da * silu).astype(BF16)
            dzg_ref[:, cols] = (da * zu * (s + silu * (1.0 - s))).astype(BF16)

    rows = pl.BlockSpec((tm, F), lambda i: (i, 0))
    hidden = jax.ShapeDtypeStruct((T, F), BF16)
    return pl.pallas_call(
        body, name=name, grid=(T // tm,),
        in_specs=[pl.BlockSpec((tm, D), lambda i: (i, 0)), rows, rows, pl.BlockSpec((F, D), lambda i: (0, 0))],
        out_specs=[rows, rows, rows], out_shape=[hidden, hidden, hidden],
        compiler_params=_params("parallel"))(dxo, zg, zu, w_out)


def dh_norm_bwd(x, dxo, g, parts, w, name):
    T, D = x.shape
    F = parts[0].shape[1]
    n = len(parts)
    tm = _pick(T, 256, 8)

    def body(x_ref, dxo_ref, g_ref, *refs):
        a_refs, w_refs, (dx_ref, dg_ref) = refs[:n], refs[n:2 * n], refs[2 * n:]
        i = pl.program_id(0)

        @pl.when(i == 0)
        def _():
            dg_ref[...] = jnp.zeros_like(dg_ref)

        dh = sum(_nt(a_ref[...], w_ref[...]) for a_ref, w_ref in zip(a_refs, w_refs))
        xv = x_ref[...]
        r = lax.rsqrt(jnp.mean(xv * xv, axis=-1, keepdims=True) + EPS)
        xh = xv * r
        dg_ref[...] += jnp.sum(dh * xh, axis=0, keepdims=True)
        dxh = dh * g_ref[...]
        dx_ref[...] = dxo_ref[...] + r * (dxh - xh * jnp.mean(dxh * xh, axis=-1, keepdims=True))

    tile = pl.BlockSpec((tm, D), lambda i: (i, 0))
    vec = pl.BlockSpec((1, D), lambda i: (0, 0))
    return pl.pallas_call(
        body, name=name, grid=(T // tm,),
        in_specs=[tile, tile, vec] + [pl.BlockSpec((tm, F), lambda i: (i, 0))] * n
                 + [pl.BlockSpec((D, F), lambda i, p=p: (0, p)) for p in range(n)],
        out_specs=[tile, vec],
        out_shape=[jax.ShapeDtypeStruct((T, D), F32), jax.ShapeDtypeStruct((1, D), F32)],
        compiler_params=_params("arbitrary"))(x, dxo, g, *parts, *([w] * n))


def _chunks(T, fn):
    def step(c, carry):
        fn(pl.multiple_of(c * SEQ_CHUNK, SEQ_CHUNK))
        return carry
    lax.fori_loop(0, T // SEQ_CHUNK, step, 0)


def _conv_taps(win, ktaps, pad):
    return [(win if k == ktaps - 1 else pltpu.roll(win, ktaps - 1 - k, 0))[pad:, :] for k in range(ktaps)]


def _conv_taps_t(win, ktaps, pad):
    n = win.shape[0]
    return [(win if k == ktaps - 1 else pltpu.roll(win, n - (ktaps - 1 - k), 0))[:n - pad, :] for k in range(ktaps)]


def _col(T, W, idx):
    return pl.BlockSpec((T, W), lambda i, idx=idx: (0, idx))


def _full(shape):
    return pl.BlockSpec(shape, lambda i: (0,) * len(shape))


def mix_a_fwd(z, w, name):
    T, W = z.shape[0], w.shape[1]
    K, P = w.shape[0], 8

    def body(ab_ref, ac_ref, ax_ref, w_ref, y_ref, pp_ref):
        pp_ref[0:P, :] = jnp.zeros((P, W), F32)

        def chunk(s):
            rows = pl.ds(s, SEQ_CHUNK)
            pp_ref[pl.ds(s + P, SEQ_CHUNK), :] = ac_ref[rows, :] * ax_ref[rows, :]
            taps = _conv_taps(pp_ref[pl.ds(s, SEQ_CHUNK + P), :], K, P)
            q = sum(w_ref[k:k + 1, :] * taps[k] for k in range(K))
            y_ref[rows, :] = (ab_ref[rows, :] * q).astype(BF16)

        _chunks(T, chunk)

    return pl.pallas_call(
        body, name=name, grid=(1,),
        in_specs=[_col(T, W, 0), _col(T, W, 1), _col(T, W, 2), _full((K, W))],
        out_specs=_full((T, W)), out_shape=jax.ShapeDtypeStruct((T, W), BF16),
        scratch_shapes=[pltpu.VMEM((T + P, W), F32)],
        compiler_params=_params("arbitrary"))(z, z, z, w)


def mix_a_bwd(z, dy, w, name):
    T, W = z.shape[0], w.shape[1]
    K, P = w.shape[0], 8

    def body(ab_ref, ac_ref, ax_ref, dy_ref, w_ref, dab_ref, dac_ref, dax_ref, dw_ref, pp_ref, dq_ref):
        pp_ref[0:P, :] = jnp.zeros((P, W), F32)
        dq_ref[T:T + P, :] = jnp.zeros((P, W), F32)
        dw_ref[...] = jnp.zeros_like(dw_ref)

        def chunk1(s):
            rows = pl.ds(s, SEQ_CHUNK)
            pp_ref[pl.ds(s + P, SEQ_CHUNK), :] = ac_ref[rows, :] * ax_ref[rows, :]
            taps = _conv_taps(pp_ref[pl.ds(s, SEQ_CHUNK + P), :], K, P)
            q = sum(w_ref[k:k + 1, :] * taps[k] for k in range(K))
            dyv = dy_ref[rows, :]
            dab_ref[rows, :] = (dyv * q).astype(BF16)
            dq = dyv * ab_ref[rows, :]
            dq_ref[rows, :] = dq
            for k in range(K):
                dw_ref[k:k + 1, :] += jnp.sum(dq * taps[k], axis=0, keepdims=True)

        _chunks(T, chunk1)

        def chunk2(s):
            rows = pl.ds(s, SEQ_CHUNK)
            taps = _conv_taps_t(dq_ref[pl.ds(s, SEQ_CHUNK + P), :], K, P)
            dp = sum(w_ref[k:k + 1, :] * taps[k] for k in range(K))
            dac_ref[rows, :] = (dp * ax_ref[rows, :]).astype(BF16)
            dax_ref[rows, :] = (dp * ac_ref[rows, :]).astype(BF16)

        _chunks(T, chunk2)

    tw = jax.ShapeDtypeStruct((T, W), BF16)
    return pl.pallas_call(
        body, name=name, grid=(1,),
        in_specs=[_col(T, W, 0), _col(T, W, 1), _col(T, W, 2), _col(T, W, 0), _full((K, W))],
        out_specs=[_full((T, W))] * 3 + [_full((K, W))],
        out_shape=[tw, tw, tw, jax.ShapeDtypeStruct((K, W), F32)],
        scratch_shapes=[pltpu.VMEM((T + P, W), F32), pltpu.VMEM((T + P, W), F32)],
        compiler_params=_params("arbitrary"))(z, z, z, dy, w)


def _ln_stats(v):
    mu = jnp.mean(v, axis=-1, keepdims=True)
    xc = v - mu
    rstd = lax.rsqrt(jnp.mean(xc * xc, axis=-1, keepdims=True) + EPS)
    return xc * rstd, rstd


def _ln_bwd(dxh, xh, rstd):
    return rstd * (dxh - jnp.mean(dxh, axis=-1, keepdims=True) - xh * jnp.mean(dxh * xh, axis=-1, keepdims=True))


def _tril_bf16(w_ref, h):
    n = w_ref.shape[-1]
    keep = lax.broadcasted_iota(jnp.int32, (n, n), 0) >= lax.broadcasted_iota(jnp.int32, (n, n), 1)
    return jnp.where(keep, w_ref[h], 0.0).astype(BF16)


def mix_b_fwd(z, g, w_s, bias, name):
    T, W = z.shape[0], g.shape[1]
    H, C = w_s.shape[0], w_s.shape[1]
    hd = W // H

    def body(u_ref, v_ref, g_ref, w_ref, b_ref, y_ref):
        wts = [_tril_bf16(w_ref, h) for h in range(H)]
        head = lax.broadcasted_iota(jnp.int32, (C, W), 1) // hd

        def chunk(s):
            rows = pl.ds(s, C)
            xh, _ = _ln_stats(v_ref[rows, :])
            vn = (xh * g_ref[...]).astype(BF16)
            mixed = b_ref[...]
            for h in range(H):
                mixed = mixed + jnp.where(head == h, jnp.dot(wts[h], vn, preferred_element_type=F32), 0.0)
            y_ref[rows, :] = (u_ref[rows, :] * mixed).astype(BF16)

        _chunks(T, chunk)

    return pl.pallas_call(
        body, name=name, grid=(1,),
        in_specs=[_col(T, W, 3), _col(T, W, 4), _full((1, W)), _full((H, C, C)), _full((C, W))],
        out_specs=_full((T, W)), out_shape=jax.ShapeDtypeStruct((T, W), BF16),
        compiler_params=_params("arbitrary"))(z, z, g, w_s, bias)


def mix_b_bwd(z, dy, g, w_s, bias, name):
    T, W = z.shape[0], g.shape[1]
    H, C = w_s.shape[0], w_s.shape[1]
    hd = W // H

    def body(u_ref, v_ref, dy_ref, g_ref, w_ref, b_ref, du_ref, dv_ref, dw_ref, db_ref, dg_ref, dbf_ref):
        wts = [_tril_bf16(w_ref, h) for h in range(H)]
        head = lax.broadcasted_iota(jnp.int32, (C, W), 1) // hd
        dw_ref[...] = jnp.zeros_like(dw_ref)
        dg_ref[...] = jnp.zeros_like(dg_ref)
        dbf_ref[...] = jnp.zeros_like(dbf_ref)

        def chunk(s):
            rows = pl.ds(s, C)
            xh, rstd = _ln_stats(v_ref[rows, :])
            vn = (xh * g_ref[...]).astype(BF16)
            mixed = b_ref[...]
            for h in range(H):
                mixed = mixed + jnp.where(head == h, jnp.dot(wts[h], vn, preferred_element_type=F32), 0.0)
            dyv = dy_ref[rows, :]
            du_ref[rows, :] = (dyv * mixed).astype(BF16)
            dm = dyv * u_ref[rows, :]
            dbf_ref[...] += dm
            dvn = jnp.zeros((C, W), F32)
            for h in range(H):
                dmh = jnp.where(head == h, dm, 0.0).astype(BF16)
                dw_ref[h] += _nt(dmh, vn)
                dvn = dvn + _tn(wts[h], dmh)
            dg_ref[...] += jnp.sum(dvn * xh, axis=0, keepdims=True)
            dv_ref[rows, :] = _ln_bwd(dvn * g_ref[...], xh, rstd).astype(BF16)

        _chunks(T, chunk)

        keep = lax.broadcasted_iota(jnp.int32, (C, C), 0) >= lax.broadcasted_iota(jnp.int32, (C, C), 1)
        lane = lax.broadcasted_iota(jnp.int32, (C, 128), 1)
        db = jnp.zeros((C, 128), F32)
        dbf = dbf_ref[...]
        for h in range(H):
            dw_ref[h] = jnp.where(keep, dw_ref[h], 0.0)
            db = db + jnp.where(lane == h, jnp.sum(jnp.where(head == h, dbf, 0.0), axis=1, keepdims=True), 0.0)
        db_ref[...] = db

    tw = jax.ShapeDtypeStruct((T, W), BF16)
    return pl.pallas_call(
        body, name=name, grid=(1,),
        in_specs=[_col(T, W, 3), _col(T, W, 4), _col(T, W, 1), _full((1, W)), _full((H, C, C)), _full((C, W))],
        out_specs=[_full((T, W)), _full((T, W)), _full((H, C, C)), _full((C, 128)), _full((1, W))],
        out_shape=[tw, tw, jax.ShapeDtypeStruct((H, C, C), F32), jax.ShapeDtypeStruct((C, 128), F32),
                   jax.ShapeDtypeStruct((1, W), F32)],
        scratch_shapes=[pltpu.VMEM((C, W), F32)],
        compiler_params=_params("arbitrary"))(z, z, dy, g, w_s, bias)


def mix_c_fwd(z, w, ln_g, ln_b, name):
    T, W = z.shape[0], w.shape[1]
    K, P = w.shape[0], 32

    def body(a_ref, gt_ref, w_ref, g_ref, b_ref, y_ref, up_ref):
        up_ref[0:P, :] = jnp.zeros((P, W), F32)

        def chunk(s):
            rows = pl.ds(s, SEQ_CHUNK)
            up_ref[pl.ds(s + P, SEQ_CHUNK), :] = a_ref[rows, :] * _sig(gt_ref[rows, :])
            taps = _conv_taps(up_ref[pl.ds(s, SEQ_CHUNK + P), :], K, P)
            q = sum(w_ref[k:k + 1, :] * taps[k] for k in range(K))
            xh, _ = _ln_stats(q)
            r = xh * g_ref[...] + b_ref[...]
            y_ref[rows, :] = (r * _sig(r)).astype(BF16)

        _chunks(T, chunk)

    return pl.pallas_call(
        body, name=name, grid=(1,),
        in_specs=[_col(T, W, 5), _col(T, W, 6), _full((K, W)), _full((1, W)), _full((1, W))],
        out_specs=_full((T, W)), out_shape=jax.ShapeDtypeStruct((T, W), BF16),
        scratch_shapes=[pltpu.VMEM((T + P, W), F32)],
        compiler_params=_params("arbitrary"))(z, z, w, ln_g, ln_b)


def mix_c_bwd(z, dy, w, ln_g, ln_b, name):
    T, W = z.shape[0], w.shape[1]
    K, P = w.shape[0], 32

    def body(a_ref, gt_ref, dy_ref, w_ref, g_ref, b_ref, da_ref, dgt_ref, dw_ref, dg_ref, db_ref, up_ref, dq_ref):
        up_ref[0:P, :] = jnp.zeros((P, W), F32)
        dq_ref[T:T + P, :] = jnp.zeros((P, W), F32)
        dw_ref[...] = jnp.zeros_like(dw_ref)
        dg_ref[...] = jnp.zeros_like(dg_ref)
        db_ref[...] = jnp.zeros_like(db_ref)

        def chunk1(s):
            rows = pl.ds(s, SEQ_CHUNK)
            up_ref[pl.ds(s + P, SEQ_CHUNK), :] = a_ref[rows, :] * _sig(gt_ref[rows, :])
            taps = _conv_taps(up_ref[pl.ds(s, SEQ_CHUNK + P), :], K, P)
            q = sum(w_ref[k:k + 1, :] * taps[k] for k in range(K))
            xh, rstd = _ln_stats(q)
            r = xh * g_ref[...] + b_ref[...]
            sr = _sig(r)
            dr = dy_ref[rows, :] * (sr * (1.0 + r * (1.0 - sr)))
            db_ref[...] += jnp.sum(dr, axis=0, keepdims=True)
            dg_ref[...] += jnp.sum(dr * xh, axis=0, keepdims=True)
            dq = _ln_bwd(dr * g_ref[...], xh, rstd)
            dq_ref[rows, :] = dq
            for k in range(K):
                dw_ref[k:k + 1, :] += jnp.sum(dq * taps[k], axis=0, keepdims=True)

        _chunks(T, chunk1)

        def chunk2(s):
            rows = pl.ds(s, SEQ_CHUNK)
            taps = _conv_taps_t(dq_ref[pl.ds(s, SEQ_CHUNK + P), :], K, P)
            du = sum(w_ref[k:k + 1, :] * taps[k] for k in range(K))
            sg = _sig(gt_ref[rows, :])
            da_ref[rows, :] = (du * sg).astype(BF16)
            dgt_ref[rows, :] = (du * a_ref[rows, :] * sg * (1.0 - sg)).astype(BF16)

        _chunks(T, chunk2)

    tw = jax.ShapeDtypeStruct((T, W), BF16)
    vec = jax.ShapeDtypeStruct((1, W), F32)
    return pl.pallas_call(
        body, name=name, grid=(1,),
        in_specs=[_col(T, W, 5), _col(T, W, 6), _col(T, W, 2), _full((K, W)), _full((1, W)), _full((1, W))],
        out_specs=[_full((T, W)), _full((T, W)), _full((K, W)), _full((1, W)), _full((1, W))],
        out_shape=[tw, tw, jax.ShapeDtypeStruct((K, W), F32), vec, vec],
        scratch_shapes=[pltpu.VMEM((T + P, W), F32), pltpu.VMEM((T + P, W), F32)],
        compiler_params=_params("arbitrary"))(z, z, dy, w, ln_g, ln_b)


def _pool_select(levels, W, rows):
    group = lax.broadcasted_iota(jnp.int32, (rows, W), 1) // (W // len(POOL_WINDOWS))
    out = levels[-1]
    for gi in range(len(POOL_WINDOWS) - 2, -1, -1):
        out = jnp.where(group == gi, levels[gi], out)
    return out


def _pool_count(s, W):
    t = s + lax.broadcasted_iota(jnp.int32, (SEQ_CHUNK, W), 0)
    group = lax.broadcasted_iota(jnp.int32, (SEQ_CHUNK, W), 1) // (W // len(POOL_WINDOWS))
    win = jnp.full((SEQ_CHUNK, W), POOL_WINDOWS[-1], jnp.int32)
    for gi in range(len(POOL_WINDOWS) - 2, -1, -1):
        win = jnp.where(group == gi, POOL_WINDOWS[gi], win)
    return jnp.minimum(t + 1, win).astype(F32)


def _pooled(wp_ref, s, W, P):
    win = wp_ref[pl.ds(s, SEQ_CHUNK + P), :]
    levels, acc, shift = [], win, 1
    for _ in POOL_WINDOWS:
        acc = acc + pltpu.roll(acc, shift, 0)
        levels.append(acc[P:, :])
        shift *= 2
    return _pool_select(levels, W, SEQ_CHUNK) / _pool_count(s, W) - win[P:, :]


def mix_d_fwd(z, pbd, scale, name):
    T, W = z.shape[0], scale.shape[1]
    P = 16

    def body(x_ref, p_ref, s_ref, y_ref, wp_ref):
        wp_ref[0:P, :] = jnp.zeros((P, W), F32)

        def chunk(s):
            rows = pl.ds(s, SEQ_CHUNK)
            wp_ref[pl.ds(s + P, SEQ_CHUNK), :] = x_ref[rows, :]
            pooled = _pooled(wp_ref, s, W, P).astype(BF16)
            y_ref[rows, :] = (jnp.dot(pooled, p_ref[...], preferred_element_type=F32) * s_ref[...]).astype(BF16)

        _chunks(T, chunk)

    return pl.pallas_call(
        body, name=name, grid=(1,),
        in_specs=[_col(T, W, 7), _full((W, W)), _full((1, W))],
        out_specs=_full((T, W)), out_shape=jax.ShapeDtypeStruct((T, W), BF16),
        scratch_shapes=[pltpu.VMEM((T + P, W), F32)],
        compiler_params=_params("arbitrary"))(z, pbd, scale)


def mix_d_bwd(z, dy, pbd, scale, name):
    T, W = z.shape[0], scale.shape[1]
    P = 16

    def body(x_ref, dy_ref, p_ref, s_ref, dx_ref, dp_ref, ds_ref, wp_ref, e_ref, dpool_ref):
        wp_ref[0:P, :] = jnp.zeros((P, W), F32)
        e_ref[T:T + P, :] = jnp.zeros((P, W), F32)
        dp_ref[...] = jnp.zeros_like(dp_ref)
        ds_ref[...] = jnp.zeros_like(ds_ref)

        def chunk1(s):
            rows = pl.ds(s, SEQ_CHUNK)
            wp_ref[pl.ds(s + P, SEQ_CHUNK), :] = x_ref[rows, :]
            pooled = _pooled(wp_ref, s, W, P).astype(BF16)
            yl = jnp.dot(pooled, p_ref[...], preferred_element_type=F32)
            dyv = dy_ref[rows, :]
            ds_ref[...] += jnp.sum(dyv * yl, axis=0, keepdims=True)
            dyl = (dyv * s_ref[...]).astype(BF16)
            dp_ref[...] += _tn(pooled, dyl)
            dpool = _nt(dyl, p_ref[...])
            dpool_ref[rows, :] = dpool
            e_ref[rows, :] = dpool / _pool_count(s, W)

        _chunks(T, chunk1)

        def chunk2(s):
            rows = pl.ds(s, SEQ_CHUNK)
            win = e_ref[pl.ds(s, SEQ_CHUNK + P), :]
            n = SEQ_CHUNK + P
            levels, acc, shift = [], win, 1
            for _ in POOL_WINDOWS:
                acc = acc + pltpu.roll(acc, n - shift, 0)
                levels.append(acc[:SEQ_CHUNK, :])
                shift *= 2
            dx_ref[rows, :] = (_pool_select(levels, W, SEQ_CHUNK) - dpool_ref[rows, :]).astype(BF16)

        _chunks(T, chunk2)

    return pl.pallas_call(
        body, name=name, grid=(1,),
        in_specs=[_col(T, W, 7), _col(T, W, 3), _full((W, W)), _full((1, W))],
        out_specs=[_full((T, W)), _full((W, W)), _full((1, W))],
        out_shape=[jax.ShapeDtypeStruct((T, W), BF16), jax.ShapeDtypeStruct((W, W), F32),
                   jax.ShapeDtypeStruct((1, W), F32)],
        scratch_shapes=[pltpu.VMEM((T + P, W), F32), pltpu.VMEM((T + P, W), F32), pltpu.VMEM((T, W), F32)],
        compiler_params=_params("arbitrary"))(z, dy, pbd, scale)


def attn_fwd(q, kv, name):
    T, D = q.shape
    M = kv.shape[0]
    hd = D // N_HEADS
    tm = _pick(T, 512, 8)
    sc = 1.0 / math.sqrt(hd)

    def body(q_ref, k_ref, v_ref, o_ref):
        for h in range(N_HEADS):
            cols = slice(h * hd, (h + 1) * hd)
            s = _nt(q_ref[:, cols].astype(BF16), k_ref[:, cols].astype(BF16)) * sc
            p = jnp.exp(s - jnp.max(s, axis=-1, keepdims=True))
            p = p / jnp.sum(p, axis=-1, keepdims=True)
            o_ref[:, cols] = jnp.dot(p.astype(BF16), v_ref[:, cols].astype(BF16),
                                     preferred_element_type=F32).astype(BF16)

    return pl.pallas_call(
        body, name=name, grid=(T // tm,),
        in_specs=[pl.BlockSpec((tm, D), lambda i: (i, 0)), pl.BlockSpec((M, D), lambda i: (0, 0)),
                  pl.BlockSpec((M, D), lambda i: (0, 1))],
        out_specs=pl.BlockSpec((tm, D), lambda i: (i, 0)),
        out_shape=jax.ShapeDtypeStruct((T, D), BF16),
        compiler_params=_params("parallel"))(q, kv, kv)


def attn_bwd(q, kv, do, name):
    T, D = q.shape
    M = kv.shape[0]
    hd = D // N_HEADS
    tm = _pick(T, 512, 8)
    sc = 1.0 / math.sqrt(hd)

    def body(q_ref, k_ref, v_ref, do_ref, dq_ref, dk_ref, dv_ref):
        i = pl.program_id(0)

        @pl.when(i == 0)
        def _():
            dk_ref[...] = jnp.zeros_like(dk_ref)
            dv_ref[...] = jnp.zeros_like(dv_ref)

        for h in range(N_HEADS):
            cols = slice(h * hd, (h + 1) * hd)
            qh, kh = q_ref[:, cols].astype(BF16), k_ref[:, cols].astype(BF16)
            vh, doh = v_ref[:, cols].astype(BF16), do_ref[:, cols].astype(BF16)
            s = _nt(qh, kh) * sc
            p = jnp.exp(s - jnp.max(s, axis=-1, keepdims=True))
            p = p / jnp.sum(p, axis=-1, keepdims=True)
            dp = _nt(doh, vh)
            dv_ref[:, cols] += _tn(p.astype(BF16), doh)
            ds = (p * (dp - jnp.sum(dp * p, axis=-1, keepdims=True)) * sc).astype(BF16)
            dq_ref[:, cols] = jnp.dot(ds, kh, preferred_element_type=F32).astype(BF16)
            dk_ref[:, cols] += _tn(ds, qh)

    tile = pl.BlockSpec((tm, D), lambda i: (i, 0))
    mem = jax.ShapeDtypeStruct((M, D), F32)
    return pl.pallas_call(
        body, name=name, grid=(T // tm,),
        in_specs=[tile, pl.BlockSpec((M, D), lambda i: (0, 0)), pl.BlockSpec((M, D), lambda i: (0, 1)), tile],
        out_specs=[tile, pl.BlockSpec((M, D), lambda i: (0, 0)), pl.BlockSpec((M, D), lambda i: (0, 0))],
        out_shape=[jax.ShapeDtypeStruct((T, D), BF16), mem, mem],
        compiler_params=_params("arbitrary"))(q, kv, kv, do)


def loss_head(x, g, target, name):
    T, D = x.shape
    tm = _pick(T, 512, 8)

    def body(x_ref, g_ref, t_ref, l_ref, dx_ref, dg_ref):
        i = pl.program_id(0)

        @pl.when(i == 0)
        def _():
            l_ref[...] = jnp.zeros_like(l_ref)
            dg_ref[...] = jnp.zeros_like(dg_ref)

        xv = x_ref[...]
        r = lax.rsqrt(jnp.mean(xv * xv, axis=-1, keepdims=True) + EPS)
        xh = xv * r
        err = xh * g_ref[...] - t_ref[...]
        l_ref[...] += 0.5 * jnp.sum(jnp.mean(err * err, axis=-1, keepdims=True), axis=0, keepdims=True)
        dy = err * (1.0 / D)
        dg_ref[...] += jnp.sum(dy * xh, axis=0, keepdims=True)
        dxh = dy * g_ref[...]
        dx_ref[...] = r * (dxh - xh * jnp.mean(dxh * xh, axis=-1, keepdims=True))

    tile = pl.BlockSpec((tm, D), lambda i: (i, 0))
    vec = pl.BlockSpec((1, D), lambda i: (0, 0))
    return pl.pallas_call(
        body, name=name, grid=(T // tm,),
        in_specs=[tile, vec, tile],
        out_specs=[pl.BlockSpec((1, 128), lambda i: (0, 0)), tile, vec],
        out_shape=[jax.ShapeDtypeStruct((1, 128), F32), jax.ShapeDtypeStruct((T, D), F32),
                   jax.ShapeDtypeStruct((1, D), F32)],
        compiler_params=_params("arbitrary"))(x, g, target)


def _block_diag(p):
    G, gd, _ = p.shape
    rows = [jnp.concatenate([p[g] if g == c else jnp.zeros((gd, gd), p.dtype) for c in range(G)], axis=1)
            for g in range(G)]
    return jnp.concatenate(rows, axis=0)


class _LazyWeight:
    def __init__(self, fetch, name, latest):
        self.fetch, self.name, self.latest = fetch, name, latest

    def __getitem__(self, l):
        return self.fetch(self.name, l, self.latest[0])


def _local_step(x, mem, target, fetch, ws, L, progress=lambda event, l, grads, values: values):
    T, D = x.shape
    W = D // 4
    H = ws["sgu_w"].shape[1]
    row = lambda v: v.reshape(1, -1)
    latest = [x]
    wb = {n: _LazyWeight(fetch, n, latest) for n in BIG}
    saved = []
    for l in range(L):
        s = {"x0": x}
        latest[0] = x
        x, *s["ffn1"] = ffn_fwd(x, row(ws["norm_ffn1"][l]), wb["ffn1_w_in"][l], wb["ffn1_w_out"][l], "ffn_fwd")
        s["x1"] = x
        latest[0] = x
        z, s["h_mix"] = norm_matmul(x, row(ws["norm_mix"][l]), wb["mix_w_in"][l], "mix_in")
        s["z"] = z
        s["bias"] = jnp.repeat(ws["sgu_b"][l].T, W // H, axis=1)
        s["pbd"] = _block_diag(ws["pool_w"][l]).astype(BF16)
        y = jnp.concatenate([
            mix_a_fwd(z, ws["sconv_w"][l], "mix_a_fwd"),
            mix_b_fwd(z, row(ws["sgu_norm_g"][l]), ws["sgu_w"][l], s["bias"], "mix_b_fwd"),
            mix_c_fwd(z, ws["cconv_w"][l], row(ws["cconv_ln_g"][l]), row(ws["cconv_ln_b"][l]), "mix_c_fwd"),
            mix_d_fwd(z, s["pbd"], row(ws["pool_scale"][l]), "mix_d_fwd")], axis=1)
        s["y"] = y
        x = matmul_res(x, y, wb["mix_w_out"][l], "mix_out")
        s["x2"] = x
        latest[0] = x
        s["q"], s["hq"] = norm_matmul(x, row(ws["norm_xattn"][l]), wb["xattn_wq"][l], "attn_q", out_dtype=BF16)
        s["kv"], s["mn"] = norm_matmul(mem, row(ws["norm_mem"][l]), wb["xattn_wkv"][l], "attn_kv")
        s["o"] = attn_fwd(s["q"], s["kv"], "attn_fwd")
        x = matmul_res(x, s["o"], wb["xattn_wo"][l], "attn_out")
        s["x3"] = x
        latest[0] = x
        x, *s["ffn2"] = ffn_fwd(x, row(ws["norm_ffn2"][l]), wb["ffn2_w_in"][l], wb["ffn2_w_out"][l], "ffn_fwd")
        saved.append(s)

    loss, dx, dg_final = loss_head(x, row(ws["norm_final"]), target, "loss_head")
    grads = {n: [None] * L for n in WEIGHTS if n != "norm_final"}
    grads["norm_final"] = dg_final.reshape(-1)

    def pin(dx, names, l):
        dx, made = lax.optimization_barrier((dx, [grads[n][l] for n in names]))
        for n, g in zip(names, made):
            grads[n][l] = g
        return dx

    def after_stages(event, l, values):
        return progress(event, l, grads, values)

    def ffn_back(xin, kept, dxo, gname, win, wout, l, event):
        h, zg, zu = kept
        a, dzg, dzu = ffn_dz(dxo, zg, zu, wb[wout][l], "ffn_dz")
        last = l == 0 and event == "ffn1_mid"
        if not last:
            dxn, dg = dh_norm_bwd(xin, dxo, row(ws[gname][l]), [dzg, dzu], wb[win][l], "ffn_dh")
            dxn, h, a = after_stages(event, l, (dxn, h, a))
        else:
            h, a = after_stages(event, l, (h, a))
        grads[win][l] = matmul_tn(h, dzg, 1.0, "ffn_dwin", b2=dzu)
        grads[wout][l] = matmul_tn(a, dxo, 0.5, "ffn_dwout")
        if last:
            dzg, dzu = after_stages("ffn1_grads", l, (dzg, dzu))
            dxn, dg = dh_norm_bwd(xin, dxo, row(ws[gname][l]), [dzg, dzu], wb[win][l], "ffn_dh")
        grads[gname][l] = dg.reshape(-1)
        return dxn if last else pin(dxn, (win, wout), l)

    for l in reversed(range(L)):
        s = saved[l]
        dx = ffn_back(s["x3"], s["ffn2"], dx, "norm_ffn2", "ffn2_w_in", "ffn2_w_out", l, "ffn2_mid")
        dx, = after_stages("ffn2", l, (dx,))
        grads["xattn_wo"][l] = matmul_tn(s["o"], dx, 1.0, "dw_sq")
        do = matmul_nt(dx, wb["xattn_wo"][l], "attn_do", out_dtype=BF16)
        dq, dk, dv = attn_bwd(s["q"], s["kv"], do, "attn_bwd")
        grads["xattn_wq"][l] = matmul_tn(s["hq"], dq, 1.0, "dw_sq")
        dx, dg = dh_norm_bwd(s["x2"], dx, row(ws["norm_xattn"][l]), [dq], wb["xattn_wq"][l], "attn_dh")
        grads["norm_xattn"][l] = dg.reshape(-1)
        dkv = jnp.concatenate([dk, dv], axis=1)
        grads["xattn_wkv"][l] = matmul_tn(s["mn"], dkv, 1.0, "attn_dwkv")
        dmn = matmul_nt(dkv, wb["xattn_wkv"][l], "attn_dmn")
        _, dg = rmsnorm_bwd(None, dmn, mem, row(ws["norm_mem"][l]), "norm_mem_bwd")
        grads["norm_mem"][l] = dg.reshape(-1)
        dx = pin(dx, ("xattn_wo", "xattn_wq", "xattn_wkv", "norm_mem"), l)
        dx, = after_stages("attn", l, (dx,))
        grads["mix_w_out"][l] = matmul_tn(s["y"], dx, 1.0, "dw_sq")
        dy = matmul_nt(dx, wb["mix_w_out"][l], "mix_dy")
        z = s["z"]
        dab, dac, dax, dws = mix_a_bwd(z, dy, ws["sconv_w"][l], "mix_a_bwd")
        dbu, dbv, dwsgu, dbs, dgs = mix_b_bwd(z, dy, row(ws["sgu_norm_g"][l]), ws["sgu_w"][l], s["bias"], "mix_b_bwd")
        dca, dcg, dwc, dgc, dbc = mix_c_bwd(z, dy, ws["cconv_w"][l], row(ws["cconv_ln_g"][l]),
                                            row(ws["cconv_ln_b"][l]), "mix_c_bwd")
        ddw, dpbd, dsc = mix_d_bwd(z, dy, s["pbd"], row(ws["pool_scale"][l]), "mix_d_bwd")
        grads["sconv_w"][l], grads["cconv_w"][l] = dws, dwc
        grads["sgu_w"][l], grads["sgu_b"][l], grads["sgu_norm_g"][l] = dwsgu, dbs[:, :H].T, dgs.reshape(-1)
        grads["cconv_ln_g"][l], grads["cconv_ln_b"][l] = dgc.reshape(-1), dbc.reshape(-1)
        gd = W // len(POOL_WINDOWS)
        grads["pool_w"][l] = jnp.stack([dpbd[g * gd:(g + 1) * gd, g * gd:(g + 1) * gd] for g in range(len(POOL_WINDOWS))])
        grads["pool_scale"][l] = dsc.reshape(-1)
        dz = jnp.concatenate([dab, dac, dax, dbu, dbv, dca, dcg, ddw], axis=1)
        grads["mix_w_in"][l] = matmul_tn(s["h_mix"], dz, 1.0, "mix_dwin")
        dx, dg = dh_norm_bwd(s["x1"], dx, row(ws["norm_mix"][l]), [dz], wb["mix_w_in"][l], "mix_dh")
        grads["norm_mix"][l] = dg.reshape(-1)
        dx = pin(dx, ("mix_w_out", "mix_w_in"), l)
        dx, = after_stages("mix", l, (dx,))
        dx = ffn_back(s["x0"], s["ffn1"], dx, "norm_ffn1", "ffn1_w_in", "ffn1_w_out", l, "ffn1_mid")
        dx, = after_stages("layer", l, (dx,))

    return loss[0, 0], dx, grads


ANY = pl.BlockSpec(memory_space=pl.ANY)


def _other_chips(x, y):
    return [(1 - x, y), (x, 1 - y), (1 - x, 1 - y)]


def _shard_slice(ref, axis, chip, size):
    idx = [slice(None)] * len(ref.shape)
    idx[axis] = pl.ds(pl.multiple_of(chip * size, size), size)
    return ref.at[tuple(idx)]


def all_gather_chips(shards, axes, name):
    n = len(shards)

    def body(*refs):
        ins, outs = refs[:n], refs[n:2 * n]
        send, recv, loc = refs[2 * n:]
        x, y, c = lax.axis_index("x"), lax.axis_index("y"), lax.axis_index("c")
        me = 2 * x + y
        chips = _other_chips(x, y)
        started = []
        for i in range(n):
            size = ins[i].shape[axes[i]]
            cp = pltpu.make_async_copy(ins[i], _shard_slice(outs[i], axes[i], me, size), loc.at[i])
            cp.start()
            started.append(cp)
        sends = []
        for i in range(n):
            size = ins[i].shape[axes[i]]
            for j, (px, py) in enumerate(chips):
                cp = pltpu.make_async_remote_copy(
                    src_ref=ins[i], dst_ref=_shard_slice(outs[i], axes[i], me, size),
                    send_sem=send.at[i, j], recv_sem=recv.at[i, j], device_id=(px, py, c), device_id_type=MESH_ID)
                cp.start()
                sends.append(cp)
        for i in range(n):
            size = ins[i].shape[axes[i]]
            for j, (px, py) in enumerate(chips):
                pltpu.make_async_remote_copy(
                    src_ref=ins[i], dst_ref=_shard_slice(outs[i], axes[i], 2 * px + py, size),
                    send_sem=send.at[i, j], recv_sem=recv.at[i, j], device_id=(px, py, c),
                    device_id_type=MESH_ID).wait_recv()
        for cp in sends:
            cp.wait_send()
        for cp in started:
            cp.wait()

    def full(a, ax):
        shape = list(a.shape)
        shape[ax] *= N_CHIPS
        return jax.ShapeDtypeStruct(tuple(shape), a.dtype)

    return pl.pallas_call(
        body, name=name, in_specs=[ANY] * n, out_specs=[ANY] * n,
        out_shape=[full(a, ax) for a, ax in zip(shards, axes)],
        scratch_shapes=[pltpu.SemaphoreType.DMA((n, 3)), pltpu.SemaphoreType.DMA((n, 3)),
                        pltpu.SemaphoreType.DMA((n,))],
        compiler_params=pltpu.CompilerParams(has_side_effects=True))(*shards)


def cast_into_slot(shard, axis, chip, name):
    L, K, N = shard.shape
    bm = _pick(K, 512, 16)
    full = (K * N_CHIPS, N) if axis == 1 else (K, N * N_CHIPS)
    nb = K // bm

    def body(c_ref, s_ref, *o_refs):
        for l in range(L):
            o_refs[l][...] = s_ref[l].astype(BF16)

    out_map = (lambda i, c: (c[0] * nb + i, 0)) if axis == 1 else (lambda i, c: (i, c[0]))
    spec = pltpu.PrefetchScalarGridSpec(
        num_scalar_prefetch=1, grid=(nb,),
        in_specs=[pl.BlockSpec((L, bm, N), lambda i, c: (0, i, 0))],
        out_specs=[pl.BlockSpec((bm, N), out_map)] * L)
    return pl.pallas_call(body, name=name, grid_spec=spec, out_shape=[jax.ShapeDtypeStruct(full, BF16)] * L,
                          compiler_params=_params("parallel"))(chip, shard)


HBM = pl.BlockSpec(memory_space=pltpu.HBM)
SEM = pl.BlockSpec(memory_space=pltpu.SEMAPHORE)
DATAFLOW = pltpu.SideEffectType.DATAFLOW_SIDE_EFFECTING


def split_start(name, sources, landing, n_sems, make, after):
    ns, nl, na = len(sources), len(landing), len(after)

    def body(*refs):
        out, _ = make(refs[:ns], refs[ns:ns + nl], refs[ns + nl + na], refs[ns + nl + na + 1])
        for cp in out:
            cp.start()
        refs[-1][...] = jnp.zeros_like(refs[-1])

    hbm = lambda b: pltpu.with_memory_space_constraint(b, pltpu.HBM)
    res = pl.pallas_call(
        body, name=name,
        out_shape=(pltpu.SemaphoreType.DMA((n_sems,)), pltpu.SemaphoreType.DMA((n_sems,)),
                   *[pltpu.HBM(b.shape, b.dtype) for b in landing], jax.ShapeDtypeStruct((8, 128), F32)),
        in_specs=[HBM] * (ns + nl) + [ANY] * na, out_specs=(SEM, SEM, *[HBM] * nl, pl.BlockSpec(memory_space=pltpu.VMEM)),
        input_output_aliases={ns + i: 2 + i for i in range(nl)},
        compiler_params=pltpu.CompilerParams(has_side_effects=DATAFLOW))(
            *[hbm(b) for b in sources], *[hbm(b) for b in landing], *after)
    return res[0], res[1], list(res[2:2 + nl]), res[-1]


def split_wait(name, sources, landing, send, recv, make, after):
    ns, nl = len(sources), len(landing)

    def body(*refs):
        _, back = make(refs[:ns], refs[ns:ns + nl], refs[ns + nl], refs[ns + nl + 1])
        for cp in back:
            cp.wait_send()
            cp.wait_recv()

    return list(pl.pallas_call(
        body, name=name, out_shape=tuple(pltpu.HBM(b.shape, b.dtype) for b in landing),
        in_specs=[HBM] * (ns + nl) + [SEM, SEM] + [ANY] * len(after), out_specs=tuple([HBM] * nl),
        input_output_aliases={ns + i: i for i in range(nl)},
        compiler_params=pltpu.CompilerParams(has_side_effects=DATAFLOW))(
            *[pltpu.with_memory_space_constraint(b, pltpu.HBM) for b in sources], *landing, send, recv, *after))


def _half_slot(buf, axis, chip, half):
    K, N = buf.shape
    if axis == 1:
        n = N // N_CHIPS
        return buf.at[pl.ds(pl.multiple_of(half * (K // 2), K // 2), K // 2), pl.ds(pl.multiple_of(chip * n, n), n)]
    k2 = K // N_CHIPS // 2
    return buf.at[pl.ds(pl.multiple_of((2 * chip + half) * k2, k2), k2), :]


def gather_copies(axes):
    def make(_, bufs, send, recv):
        x, y, c = lax.axis_index("x"), lax.axis_index("y"), lax.axis_index("c")
        out, back = [], []
        for i, (buf, ax) in enumerate(zip(bufs, axes)):
            mine = _half_slot(buf, ax, 2 * x + y, c)
            for j, (px, py) in enumerate(_other_chips(x, y)):
                kw = dict(send_sem=send.at[3 * i + j], recv_sem=recv.at[3 * i + j], device_id=(px, py, c),
                          device_id_type=MESH_ID)
                out.append(pltpu.make_async_remote_copy(src_ref=mine, dst_ref=mine, **kw))
                back.append(pltpu.make_async_remote_copy(src_ref=mine, dst_ref=_half_slot(buf, ax, 2 * px + py, c), **kw))
        return out, back
    return make


def forward_copies(axes):
    def make(_, bufs, send, recv):
        x, y, c = lax.axis_index("x"), lax.axis_index("y"), lax.axis_index("c")
        out, back = [], []
        for i, (buf, ax) in enumerate(zip(bufs, axes)):
            for j, (px, py) in enumerate(_other_chips(x, y)):
                have = _half_slot(buf, ax, 2 * px + py, c)
                kw = dict(send_sem=send.at[3 * i + j], recv_sem=recv.at[3 * i + j], device_id=(x, y, 1 - c),
                          device_id_type=MESH_ID)
                out.append(pltpu.make_async_remote_copy(src_ref=have, dst_ref=have, **kw))
                back.append(pltpu.make_async_remote_copy(src_ref=have, dst_ref=_half_slot(buf, ax, 2 * px + py, 1 - c), **kw))
        return out, back
    return make


def forward_sibling(bufs, axes, name):
    n = len(bufs)

    def body(*refs):
        out, back = forward_copies(axes)(None, refs[:n], *refs[2 * n:])
        for cp in out:
            cp.start()
        for cp in back:
            cp.wait_recv()
        for cp in out:
            cp.wait_send()

    return pl.pallas_call(
        body, name=name, in_specs=[ANY] * n, out_specs=[ANY] * n,
        out_shape=[jax.ShapeDtypeStruct(b.shape, b.dtype) for b in bufs],
        input_output_aliases={i: i for i in range(n)},
        scratch_shapes=[pltpu.SemaphoreType.DMA((3 * n,)), pltpu.SemaphoreType.DMA((3 * n,))],
        compiler_params=pltpu.CompilerParams(has_side_effects=True))(*bufs)


def all_reduce_small(p, name):
    R = p.shape[0]

    def body(p_ref, o_ref, sib_ref, chip_ref, send, recv):
        x, y, c = lax.axis_index("x"), lax.axis_index("y"), lax.axis_index("c")
        me = 2 * x + y
        chips = _other_chips(x, y)
        pair = pltpu.make_async_remote_copy(src_ref=p_ref, dst_ref=sib_ref, send_sem=send.at[0], recv_sem=recv.at[0],
                                            device_id=(x, y, 1 - c), device_id_type=MESH_ID)
        pair.start()
        pair.wait()
        chip_ref[me] = p_ref[...] + sib_ref[...]
        sends = []
        for j, (px, py) in enumerate(chips):
            cp = pltpu.make_async_remote_copy(src_ref=chip_ref.at[me], dst_ref=chip_ref.at[me], send_sem=send.at[1 + j],
                                              recv_sem=recv.at[1 + j], device_id=(px, py, c), device_id_type=MESH_ID)
            cp.start()
            sends.append(cp)
        for j, (px, py) in enumerate(chips):
            pltpu.make_async_remote_copy(src_ref=chip_ref.at[me], dst_ref=chip_ref.at[2 * px + py], send_sem=send.at[1 + j],
                                         recv_sem=recv.at[1 + j], device_id=(px, py, c), device_id_type=MESH_ID).wait_recv()
        for cp in sends:
            cp.wait_send()
        o_ref[...] = ((chip_ref[0] + chip_ref[1]) + chip_ref[2]) + chip_ref[3]

    vm = pl.BlockSpec(memory_space=pltpu.VMEM)
    return pl.pallas_call(
        body, name=name, in_specs=[vm], out_specs=vm, out_shape=jax.ShapeDtypeStruct((R, 128), F32),
        scratch_shapes=[pltpu.VMEM((R, 128), F32), pltpu.VMEM((N_CHIPS, R, 128), F32),
                        pltpu.SemaphoreType.DMA((4,)), pltpu.SemaphoreType.DMA((4,))],
        compiler_params=pltpu.CompilerParams(has_side_effects=True, vmem_limit_bytes=VMEM_LIMIT))(p)


def _grad_view(g, axis):
    K, N = g.shape
    return g.reshape(1, 2, K // 2, N) if axis == 1 else g.reshape(N_CHIPS, 2, K // N_CHIPS // 2, N)


def pair_copies(gvs, others, send, recv):
    x, y, c = lax.axis_index("x"), lax.axis_index("y"), lax.axis_index("c")
    out = [pltpu.make_async_remote_copy(src_ref=gv.at[:, 1 - c], dst_ref=o, send_sem=send.at[i], recv_sem=recv.at[i],
                                        device_id=(x, y, 1 - c), device_id_type=MESH_ID)
           for i, (gv, o) in enumerate(zip(gvs, others))]
    return out, out


def chip_copies(axes):
    def piece(s, ax, chip):
        if ax == 1:
            n = s.shape[2] // N_CHIPS
            return s.at[0, :, pl.ds(pl.multiple_of(chip * n, n), n)]
        return s.at[chip]

    def make(sums, qs, send, recv):
        x, y, c = lax.axis_index("x"), lax.axis_index("y"), lax.axis_index("c")
        out = []
        for i, (s, q, ax) in enumerate(zip(sums, qs, axes)):
            for j, (px, py) in enumerate(_other_chips(x, y)):
                out.append(pltpu.make_async_remote_copy(
                    src_ref=piece(s, ax, 2 * px + py), dst_ref=q.at[j], send_sem=send.at[3 * i + j],
                    recv_sem=recv.at[3 * i + j], device_id=(px, py, c), device_id_type=MESH_ID))
        return out, out
    return make


def share_copies(_, halves, send, recv):
    x, y, c = lax.axis_index("x"), lax.axis_index("y"), lax.axis_index("c")
    kw = lambda i: dict(send_sem=send.at[i], recv_sem=recv.at[i], device_id=(x, y, 1 - c), device_id_type=MESH_ID)
    out = [pltpu.make_async_remote_copy(src_ref=h.at[c], dst_ref=h.at[c], **kw(i)) for i, h in enumerate(halves)]
    back = [pltpu.make_async_remote_copy(src_ref=h.at[c], dst_ref=h.at[1 - c], **kw(i)) for i, h in enumerate(halves)]
    return out, back


def share_sibling(halves, name):
    n = len(halves)

    def body(*refs):
        out, back = share_copies(None, refs[:n], *refs[2 * n:])
        for cp in out:
            cp.start()
        for cp in back:
            cp.wait_recv()
        for cp in out:
            cp.wait_send()

    return pl.pallas_call(
        body, name=name, in_specs=[ANY] * n, out_specs=[ANY] * n,
        out_shape=[jax.ShapeDtypeStruct(a.shape, a.dtype) for a in halves],
        input_output_aliases={i: i for i in range(n)},
        scratch_shapes=[pltpu.SemaphoreType.DMA((n,)), pltpu.SemaphoreType.DMA((n,))],
        compiler_params=pltpu.CompilerParams(has_side_effects=True))(*halves)


def add_pair(gv, other, place, name):
    A, _, rows, N = gv.shape
    bm, bn = _pick(rows, 512, 16), _pick(N, 1408, 128)

    def body(p_ref, g_ref, o_ref, out_ref):
        out_ref[...] = (g_ref[...] + o_ref[...]).astype(GRAD_WIRE)

    spec = pltpu.PrefetchScalarGridSpec(
        num_scalar_prefetch=1, grid=(A, rows // bm, N // bn),
        in_specs=[pl.BlockSpec((None, None, bm, bn), lambda a, i, j, p: (a, p[1], i, j)),
                  pl.BlockSpec((None, bm, bn), lambda a, i, j, p: (a, i, j))],
        out_specs=pl.BlockSpec((None, bm, bn), lambda a, i, j, p: (a, i, j)))
    return pl.pallas_call(body, name=name, grid_spec=spec, out_shape=jax.ShapeDtypeStruct((A, rows, N), GRAD_WIRE),
                          compiler_params=_params("parallel", "parallel", "parallel"))(place, gv, other)


def add_chips(s, q, axis, place, name):
    _, rows, n = q.shape
    bm, bn = _pick(rows, 512, 16), _pick(n, 1408, 128)
    nbj = n // bn

    def body(p_ref, s_ref, q_ref, o_ref):
        o_ref[...] = ((s_ref[...].astype(F32) + q_ref[0].astype(F32)) + q_ref[1].astype(F32)) + q_ref[2].astype(F32)

    mine = (lambda i, j, p: (p[0], i, j)) if axis == 0 else (lambda i, j, p: (0, i, p[0] * nbj + j))
    spec = pltpu.PrefetchScalarGridSpec(
        num_scalar_prefetch=1, grid=(rows // bm, nbj),
        in_specs=[pl.BlockSpec((None, bm, bn), mine), pl.BlockSpec((3, bm, bn), lambda i, j, p: (0, i, j))],
        out_specs=pl.BlockSpec((None, bm, bn), lambda i, j, p: (p[1], i, j)))
    return pl.pallas_call(body, name=name, grid_spec=spec, out_shape=jax.ShapeDtypeStruct((2, rows, n), F32),
                          compiler_params=_params("parallel", "parallel"))(place, s, q)


def adamw(w, g, m, v, name):
    R, N = w.shape
    bm = _pick(R, 512, 8)
    c1 = 1.0 / (1.0 - ADAM_B1 ** ADAM_STEP)
    c2 = 1.0 / (1.0 - ADAM_B2 ** ADAM_STEP)

    def body(w_ref, g_ref, m_ref, v_ref, d_ref, nm_ref, nv_ref):
        gv = g_ref[...]
        nm = ADAM_B1 * m_ref[...] + (1.0 - ADAM_B1) * gv
        nv = ADAM_B2 * v_ref[...] + (1.0 - ADAM_B2) * (gv * gv)
        nm_ref[...] = nm
        nv_ref[...] = nv
        d_ref[...] = -ADAM_LR * ((nm * c1) / (jnp.sqrt(nv * c2) + ADAM_EPS) + ADAM_WD * w_ref[...])

    blk = pl.BlockSpec((bm, N), lambda i: (i, 0))
    out = jax.ShapeDtypeStruct((R, N), F32)
    return pl.pallas_call(body, name=name, grid=(R // bm,), in_specs=[blk] * 4, out_specs=[blk] * 3,
                          out_shape=[out, out, out], compiler_params=_params("parallel"))(w, g, m, v)


def adamw_layers(w, g0, g1, m, v, name):
    _, k, n = w.shape
    bm = _pick(k, 256, 8)
    c1 = 1.0 / (1.0 - ADAM_B1 ** ADAM_STEP)
    c2 = 1.0 / (1.0 - ADAM_B2 ** ADAM_STEP)

    def body(w_ref, g0_ref, g1_ref, m_ref, v_ref, g_ref, d_ref, nm_ref, nv_ref):
        def step(gv):
            nm = ADAM_B1 * m_ref[...] + (1.0 - ADAM_B1) * gv
            nv = ADAM_B2 * v_ref[...] + (1.0 - ADAM_B2) * (gv * gv)
            g_ref[...] = gv
            nm_ref[...] = nm
            nv_ref[...] = nv
            d_ref[...] = -ADAM_LR * ((nm * c1) / (jnp.sqrt(nv * c2) + ADAM_EPS) + ADAM_WD * w_ref[...])

        @pl.when(pl.program_id(0) == 0)
        def _():
            step(g0_ref[...])

        @pl.when(pl.program_id(0) == 1)
        def _():
            step(g1_ref[...])

    blk = pl.BlockSpec((None, bm, n), lambda l, i: (l, i, 0))
    out = jax.ShapeDtypeStruct(w.shape, F32)
    return pl.pallas_call(
        body, name=name, grid=(2, k // bm),
        in_specs=[blk, pl.BlockSpec((bm, n), lambda l, i: (i * (1 - l), 0)), pl.BlockSpec((bm, n), lambda l, i: (i * l, 0)),
                  blk, blk],
        out_specs=[blk] * 4, out_shape=[out] * 4, compiler_params=_params("arbitrary", "arbitrary"))(w, g0, g1, m, v)


def _pack(arrays):
    flat = jnp.concatenate([a.reshape(-1) for a in arrays])
    rows = -(-flat.shape[0] // (256 * 128)) * 256
    return jnp.pad(flat, (0, rows * 128 - flat.shape[0])).reshape(rows, 128)


def _unpack(p, shapes):
    flat, out, at = p.reshape(-1), [], 0
    for s in shapes:
        n = math.prod(s)
        out.append(flat[at:at + n].reshape(s))
        at += n
    return out


def kernel(x, mem, norm_ffn1, ffn1_w_in, ffn1_w_out, norm_mix, mix_w_in, sconv_w, sgu_norm_g, sgu_w, sgu_b, cconv_w, cconv_ln_g, cconv_ln_b, pool_w, pool_scale, mix_w_out, norm_xattn, norm_mem, xattn_wq, xattn_wkv, xattn_wo, norm_ffn2, ffn2_w_in, ffn2_w_out, norm_final, loss_target, m_norm_ffn1, m_ffn1_w_in, m_ffn1_w_out, m_norm_mix, m_mix_w_in, m_sconv_w, m_sgu_norm_g, m_sgu_w, m_sgu_b, m_cconv_w, m_cconv_ln_g, m_cconv_ln_b, m_pool_w, m_pool_scale, m_mix_w_out, m_norm_xattn, m_norm_mem, m_xattn_wq, m_xattn_wkv, m_xattn_wo, m_norm_ffn2, m_ffn2_w_in, m_ffn2_w_out, m_norm_final, v_norm_ffn1, v_ffn1_w_in, v_ffn1_w_out, v_norm_mix, v_mix_w_in, v_sconv_w, v_sgu_norm_g, v_sgu_w, v_sgu_b, v_cconv_w, v_cconv_ln_g, v_cconv_ln_b, v_pool_w, v_pool_scale, v_mix_w_out, v_norm_xattn, v_norm_mem, v_xattn_wq, v_xattn_wkv, v_xattn_wo, v_norm_ffn2, v_ffn2_w_in, v_ffn2_w_out, v_norm_final):
    given = dict(locals())
    w = {n: given[n] for n in WEIGHTS}
    L = ffn1_w_in.shape[0]
    assert L == 2, "the reduce-scatter gives one layer to each core of a chip"
    chip = 2 * lax.axis_index("x") + lax.axis_index("y")
    chip1 = chip.astype(jnp.int32).reshape(1)
    core = lax.axis_index("c").astype(jnp.int32).reshape(1)
    place = jnp.concatenate([chip1, core])

    axis = {n: 1 if n in COL_SHARDED else 0 for n in BIG}
    bufs = {}
    for n in BIG:
        for l, b in enumerate(cast_into_slot(w[n], axis[n] + 1, chip1, "cast_weights")):
            bufs[n, l] = b
    groups = {"a": [(n, 0) for n in BIG[:2]], "b": [(n, 0) for n in BIG[2:4]], "c": [(n, 0) for n in BIG[4:7]],
              "d": [(n, 0) for n in BIG[7:]], "e": [(n, 1) for n in BIG[:2]], "f": [(n, 1) for n in BIG[2:]]}
    wc = sconv_w.shape[-1]
    conv_rows = [w[n].reshape(-1, wc) for n in SMALL_CONV]
    n_conv = sum(r.shape[0] for r in conv_rows)
    conv_pack = jnp.pad(jnp.concatenate(conv_rows, axis=0), ((0, -n_conv % 8), (0, 128 - wc)))[None]
    conv_all = all_gather_chips([conv_pack], [0], "gather_conv")[0]
    started, token = {}, conv_all
    for g, keys in groups.items():
        send, recv, thru, token = split_start("gather_start_" + g, [], [bufs[k] for k in keys], 3 * len(keys),
                                              gather_copies([axis[k[0]] for k in keys]), [token])
        started[g] = (send, recv, thru)
    ready, ahead = {}, {}
    small = SMALL_REPL + SMALL_CONV
    packed = [_pack([src[n] for n in small]) for src in
              (w, {n: given["m_" + n] for n in small}, {n: given["v_" + n] for n in small})]

    def arrived(g, after):
        send, recv, thru = started[g]
        axes = [axis[k[0]] for k in groups[g]]
        return axes, split_wait("gather_wait_" + g, [], thru, send, recv, gather_copies(axes), after)

    def fetch(n, l, after):
        g = next(g for g, keys in groups.items() if (n, l) in keys)
        if g not in ready:
            if g in ahead:
                axes, send, recv, thru, token_g = ahead.pop(g)
                done = split_wait("forward_wait_" + g, [], thru, send, recv, forward_copies(axes), [after, token_g])
            else:
                axes, done = arrived(g, [token] + packed if g == "a" else [after])
                done = forward_sibling(done, axes, "gather_forward")
            if g == "e":
                axes_f, landed = arrived("f", [after])
                send, recv, thru, token_f = split_start("forward_start_f", [], landed, 3 * len(landed),
                                                        forward_copies(axes_f), [after])
                done, (token_f,) = lax.optimization_barrier((done, [token_f]))
                ahead["f"] = (axes_f, send, recv, thru, token_f)
            ready[g] = dict(zip(groups[g], done))
        return ready[g][n, l]

    conv_full = jnp.moveaxis(conv_all[:, :n_conv, :wc], 0, 1).reshape(n_conv, N_CHIPS * wc)
    ws = {n: w[n] for n in SMALL_REPL}
    at = 0
    for n in SMALL_CONV:
        rows = w[n].shape[0] * w[n].shape[1]
        ws[n] = conv_full[at:at + rows].reshape(w[n].shape[0], w[n].shape[1], N_CHIPS * wc)
        at += rows

    halves, state = {}, {}
    reduce_groups = {"r1": [(n, 1) for n in BIG], "r0a": [(n, 0) for n in BIG[2:]], "r0b": [(n, 0) for n in BIG[:2]]}
    plan = {("layer", 1): [("pair", "r1")],
            ("ffn2", 0): [("chips", "r1")],
            ("mix", 0): [("finish", "r1"), ("share", "r1"), ("pair", "r0a")],
            ("ffn1_mid", 0): [("chips", "r0a")],
            ("ffn1_grads", 0): [("pair", "r0b")],
            ("layer", 0): [("finish", "r0a")]}


    def stage_pair(g, keys, grads, after):
        gvs = [_grad_view(grads[n][l], axis[n]) for n, l in keys]
        others = [lax.empty(gv.shape[:1] + gv.shape[2:], F32) for gv in gvs]
        send, recv, others, token = split_start("pair_start_" + g, gvs, others, len(gvs), pair_copies, [after])
        state[g] = dict(sources=gvs, send=send, recv=recv, landing=others, token=token)
        return [(state[g], "token")]

    def stage_chips(g, keys, grads, after):
        st = state[g]
        axes = [axis[n] for n, _ in keys]
        others = split_wait("pair_wait_" + g, st["sources"], st["landing"], st["send"], st["recv"], pair_copies,
                            [after, st["token"]])
        sums = [add_pair(gv, o, place, "add_pair") for gv, o in zip(st["sources"], others)]
        qs = [lax.empty((3, s.shape[1], s.shape[2] // (N_CHIPS if ax == 1 else 1)), GRAD_WIRE) for s, ax in zip(sums, axes)]
        send, recv, qs, token = split_start("chips_start_" + g, sums, qs, 3 * len(sums), chip_copies(axes), [after])
        state[g] = dict(sources=sums, send=send, recv=recv, landing=qs, token=token)
        return [(state[g], "token")]

    def stage_finish(g, keys, grads, after):
        st = state.pop(g)
        qs = split_wait("chips_wait_" + g, st["sources"], st["landing"], st["send"], st["recv"],
                        chip_copies([axis[n] for n, _ in keys]), [after, st["token"]])
        for key, s, q in zip(keys, st["sources"], qs):
            halves[key] = add_chips(s, q, axis[key[0]], place, "add_chips")
        return [(halves, key) for key in keys]

    def stage_share(g, keys, grads, after):
        send, recv, thru, token = split_start("share_start_" + g, [], [halves.pop(k) for k in keys], len(keys),
                                              share_copies, [after])
        state["share_" + g] = dict(keys=keys, send=send, recv=recv, landing=thru, token=token)
        return [(state["share_" + g], "token")]

    stages = {"pair": stage_pair, "chips": stage_chips, "finish": stage_finish, "share": stage_share}

    def progress(event, l, grads, values):
        places = []
        for stage, g in plan.get((event, l), []):
            places += stages[stage](g, reduce_groups[g], grads, values[0])
        places = [(box, k) for box, k in places if k in box]
        if places:
            values, tied = lax.optimization_barrier((values, [box[k] for box, k in places]))
            for (box, k), a in zip(places, tied):
                box[k] = a
        return values

    loss_part, grad_x, grads = _local_step(x[0], mem[0], loss_target[0], fetch, ws, L, progress)
    loss = lax.psum(loss_part, ("x", "y", "c"))

    small_g = [grads[n] if n == "norm_final" else jnp.stack(grads[n]) for n in small]
    small_sum = all_reduce_small(_pack(small_g), "reduce_small")
    grad = dict(zip(small, _unpack(small_sum, [g.shape for g in small_g])))
    for n in SMALL_CONV:
        grad[n] = lax.dynamic_slice_in_dim(grad[n], chip * wc, wc, axis=2)

    delta, new_m, new_v = {}, {}, {}

    sent = state.pop("share_r1")
    layer1 = dict(zip(sent["keys"], split_wait("share_wait_r1", [], sent["landing"], sent["send"], sent["recv"],
                                               share_copies, [small_sum, sent["token"]])))

    def finish_weights(names, layer0_halves):
        layer0 = dict(zip(names, share_sibling(layer0_halves, "share_pair")))
        for n in names:
            g0, g1 = layer0[n].reshape(w[n].shape[1:]), layer1[n, 1].reshape(w[n].shape[1:])
            grad[n], delta[n], new_m[n], new_v[n] = adamw_layers(w[n], g0, g1, given["m_" + n], given["v_" + n], "adamw")

    stage_chips("r0b", reduce_groups["r0b"], grads, small_sum)
    early, (state["r0b"]["token"],) = lax.optimization_barrier(([halves[n, 0] for n in BIG[2:]], [state["r0b"]["token"]]))
    finish_weights(BIG[2:], early)
    stage_finish("r0b", reduce_groups["r0b"], grads, delta[BIG[-1]])
    finish_weights(BIG[:2], [halves[n, 0] for n in BIG[:2]])
    shapes = [w[n].shape for n in small]
    packed.insert(1, _pack([grad[n] for n in small]))
    for out, p in zip((delta, new_m, new_v), adamw(*packed, "adamw_small")):
        out.update(zip(small, _unpack(p, shapes)))

    return (loss, grad_x[None], *[grad[n] for n in WEIGHTS], *[delta[n] for n in WEIGHTS],
            *[new_m[n] for n in WEIGHTS], *[new_v[n] for n in WEIGHTS])
```

```python
import functools
import math

import jax
import jax.numpy as jnp
from jax import lax
from jax.experimental import pallas as pl
from jax.experimental.pallas import tpu as pltpu

F32 = jnp.float32
BF16 = jnp.bfloat16
EPS = 1e-6
SEQ_CHUNK = 128
POOL_WINDOWS = (2, 4, 8, 16)
N_HEADS = 4
ADAM_LR, ADAM_B1, ADAM_B2, ADAM_EPS, ADAM_WD, ADAM_STEP = 0.001, 0.9, 0.999, 1e-08, 0.01, 10
VMEM_LIMIT = 56 * 1024 * 1024
MESH_ID = pl.DeviceIdType.MESH
N_CHIPS = 4
GRAD_WIRE = BF16

BIG = ("ffn1_w_in", "ffn1_w_out", "mix_w_in", "mix_w_out", "xattn_wq", "xattn_wkv", "xattn_wo",
       "ffn2_w_in", "ffn2_w_out")
COL_SHARDED = ("ffn1_w_in", "mix_w_in", "xattn_wkv", "ffn2_w_in")
SMALL_CONV = ("sconv_w", "cconv_w")
SMALL_REPL = ("norm_ffn1", "norm_mix", "sgu_norm_g", "sgu_w", "sgu_b", "cconv_ln_g", "cconv_ln_b",
              "pool_w", "pool_scale", "norm_xattn", "norm_mem", "norm_ffn2", "norm_final")
WEIGHTS = ("norm_ffn1", "ffn1_w_in", "ffn1_w_out", "norm_mix", "mix_w_in", "sconv_w", "sgu_norm_g",
           "sgu_w", "sgu_b", "cconv_w", "cconv_ln_g", "cconv_ln_b", "pool_w", "pool_scale",
           "mix_w_out", "norm_xattn", "norm_mem", "xattn_wq", "xattn_wkv", "xattn_wo", "norm_ffn2",
           "ffn2_w_in", "ffn2_w_out", "norm_final")


def _pick(n, pref, align):
    best = None
    for d in range(align, min(n, pref) + 1, align):
        if n % d == 0:
            best = d
    return best or n


def _sig(x):
    return 0.5 * jnp.tanh(0.5 * x) + 0.5


def _nt(a, b):
    return lax.dot_general(a, b, (((1,), (1,)), ((), ())), preferred_element_type=F32)


def _tn(a, b):
    return lax.dot_general(a, b, (((0,), (0,)), ((), ())), preferred_element_type=F32)


def _params(*sem):
    return pltpu.CompilerParams(dimension_semantics=sem, vmem_limit_bytes=VMEM_LIMIT)


def norm_matmul(x, g, w, name, out_dtype=F32):
    T, D = x.shape
    N = w.shape[1]
    tm, tn = _pick(T, 512, 8), _pick(N, 2048, 128)

    def body(x_ref, g_ref, w_ref, o_ref, h_ref):
        j = pl.program_id(1)

        @pl.when(j == 0)
        def _():
            xv = x_ref[...]
            r = lax.rsqrt(jnp.mean(xv * xv, axis=-1, keepdims=True) + EPS)
            h_ref[...] = (xv * r * g_ref[...]).astype(BF16)

        o_ref[...] = jnp.dot(h_ref[...], w_ref[...], preferred_element_type=F32).astype(out_dtype)

    return pl.pallas_call(
        body, name=name, grid=(T // tm, N // tn),
        in_specs=[pl.BlockSpec((tm, D), lambda i, j: (i, 0)), pl.BlockSpec((1, D), lambda i, j: (0, 0)),
                  pl.BlockSpec((D, tn), lambda i, j: (0, j))],
        out_specs=[pl.BlockSpec((tm, tn), lambda i, j: (i, j)), pl.BlockSpec((tm, D), lambda i, j: (i, 0))],
        out_shape=[jax.ShapeDtypeStruct((T, N), out_dtype), jax.ShapeDtypeStruct((T, D), BF16)],
        compiler_params=_params("parallel", "arbitrary"))(x, g, w)


def matmul_res(res, a, w, name):
    T, K = a.shape
    N = w.shape[1]
    tm, tn = _pick(T, 512, 8), _pick(N, 1024, 128)

    def body(r_ref, a_ref, w_ref, o_ref):
        o_ref[...] = r_ref[...] + jnp.dot(a_ref[...].astype(BF16), w_ref[...], preferred_element_type=F32)

    return pl.pallas_call(
        body, name=name, grid=(T // tm, N // tn),
        in_specs=[pl.BlockSpec((tm, tn), lambda i, j: (i, j)), pl.BlockSpec((tm, K), lambda i, j: (i, 0)),
                  pl.BlockSpec((K, tn), lambda i, j: (0, j))],
        out_specs=pl.BlockSpec((tm, tn), lambda i, j: (i, j)),
        out_shape=jax.ShapeDtypeStruct((T, N), F32),
        compiler_params=_params("parallel", "parallel"))(res, a, w)


def matmul_nt(a, w, name, out_dtype=F32):
    T, N = a.shape
    M = w.shape[0]
    tm, tmm = _pick(T, 512, 8), _pick(M, 1024, 128)

    def body(a_ref, w_ref, o_ref):
        o_ref[...] = _nt(a_ref[...].astype(BF16), w_ref[...]).astype(out_dtype)

    return pl.pallas_call(
        body, name=name, grid=(T // tm, M // tmm),
        in_specs=[pl.BlockSpec((tm, N), lambda i, j: (i, 0)), pl.BlockSpec((tmm, N), lambda i, j: (j, 0))],
        out_specs=pl.BlockSpec((tm, tmm), lambda i, j: (i, j)),
        out_shape=jax.ShapeDtypeStruct((T, M), out_dtype),
        compiler_params=_params("parallel", "parallel"))(a, w)


def matmul_tn(a, b, scale, name, b2=None):
    T, M = a.shape
    Nb = b.shape[1]
    bm, bn, bk = _pick(M, 1408, 128), _pick(Nb, 2816, 128), _pick(T, 512, 8)
    nk, nj = T // bk, Nb // bn

    def body(a_ref, b_ref, *rest):
        o_ref = rest[-1]
        j, k = pl.program_id(1), pl.program_id(2)

        @pl.when(k == 0)
        def _():
            o_ref[...] = jnp.zeros_like(o_ref)

        a_blk = a_ref[...].astype(BF16)
        if b2 is None:
            o_ref[...] += _tn(a_blk, b_ref[...].astype(BF16))
        else:
            @pl.when(j < nj)
            def _():
                o_ref[...] += _tn(a_blk, b_ref[...].astype(BF16))

            @pl.when(j >= nj)
            def _():
                o_ref[...] += _tn(a_blk, rest[0][...].astype(BF16))

        if scale != 1.0:
            @pl.when(k == nk - 1)
            def _():
                o_ref[...] = o_ref[...] * scale

    if b2 is None:
        b_specs, operands, n_out = [pl.BlockSpec((bk, bn), lambda i, j, k: (k, j))], (a, b), nj
    else:
        first = lambda i, j, k: (jnp.where(j < nj, k, 0), jnp.where(j < nj, j, 0))
        second = lambda i, j, k: (jnp.where(j >= nj, k, 0), jnp.where(j >= nj, j - nj, 0))
        b_specs, operands, n_out = [pl.BlockSpec((bk, bn), first), pl.BlockSpec((bk, bn), second)], (a, b, b2), 2 * nj
    return pl.pallas_call(
        body, name=name, grid=(M // bm, n_out, nk),
        in_specs=[pl.BlockSpec((bk, bm), lambda i, j, k: (k, i))] + b_specs,
        out_specs=pl.BlockSpec((bm, bn), lambda i, j, k: (i, j)),
        out_shape=jax.ShapeDtypeStruct((M, n_out * bn), F32),
        compiler_params=_params("parallel", "parallel", "arbitrary"))(*operands)


def rmsnorm_bwd(dxo, dh, x, g, name):
    T, D = x.shape
    tm = _pick(T, 512, 8)
    has_res = dxo is not None

    def body(*refs):
        if has_res:
            dxo_ref, dh_ref, x_ref, g_ref, dx_ref, dg_ref = refs
        else:
            dh_ref, x_ref, g_ref, dx_ref, dg_ref = refs
        i = pl.program_id(0)

        @pl.when(i == 0)
        def _():
            dg_ref[...] = jnp.zeros_like(dg_ref)

        xv, dh_v = x_ref[...], dh_ref[...]
        r = lax.rsqrt(jnp.mean(xv * xv, axis=-1, keepdims=True) + EPS)
        xh = xv * r
        dg_ref[...] += jnp.sum(dh_v * xh, axis=0, keepdims=True)
        dxh = dh_v * g_ref[...]
        dx = r * (dxh - xh * jnp.mean(dxh * xh, axis=-1, keepdims=True))
        dx_ref[...] = dx + dxo_ref[...] if has_res else dx

    tile = pl.BlockSpec((tm, D), lambda i: (i, 0))
    vec = pl.BlockSpec((1, D), lambda i: (0, 0))
    args = ([dxo] if has_res else []) + [dh, x, g]
    return pl.pallas_call(
        body, name=name, grid=(T // tm,),
        in_specs=[tile] * (len(args) - 1) + [vec],
        out_specs=[tile, vec],
        out_shape=[jax.ShapeDtypeStruct((T, D), F32), jax.ShapeDtypeStruct((1, D), F32)],
        compiler_params=_params("arbitrary"))(*args)


def ffn_fwd(x, g, w_in, w_out, name):
    T, D = x.shape
    F = w_out.shape[0]
    tm, tf = _pick(T, 512, 8), _pick(F, 1408, 128)
    nf = F // tf

    def body(x_ref, g_ref, wg_ref, wu_ref, wo_ref, o_ref, h_ref, zg_ref, zu_ref, acc_ref):
        j = pl.program_id(1)

        @pl.when(j == 0)
        def _():
            xv = x_ref[...]
            r = lax.rsqrt(jnp.mean(xv * xv, axis=-1, keepdims=True) + EPS)
            h_ref[...] = (xv * r * g_ref[...]).astype(BF16)
            acc_ref[...] = jnp.zeros_like(acc_ref)

        h = h_ref[...]
        zg = jnp.dot(h, wg_ref[...], preferred_element_type=F32)
        zu = jnp.dot(h, wu_ref[...], preferred_element_type=F32)
        zg_ref[...] = zg.astype(BF16)
        zu_ref[...] = zu.astype(BF16)
        a = (zg * _sig(zg) * zu).astype(BF16)
        acc_ref[...] += jnp.dot(a, wo_ref[...], preferred_element_type=F32)

        @pl.when(j == nf - 1)
        def _():
            o_ref[...] = x_ref[...] + 0.5 * acc_ref[...]

    tile = pl.BlockSpec((tm, D), lambda i, j: (i, 0))
    fblk = pl.BlockSpec((tm, tf), lambda i, j: (i, j))
    hidden = jax.ShapeDtypeStruct((T, F), BF16)
    return pl.pallas_call(
        body, name=name, grid=(T // tm, nf),
        in_specs=[tile, pl.BlockSpec((1, D), lambda i, j: (0, 0)),
                  pl.BlockSpec((D, tf), lambda i, j: (0, j)), pl.BlockSpec((D, tf), lambda i, j: (0, j + nf)),
                  pl.BlockSpec((tf, D), lambda i, j: (j, 0))],
        out_specs=[tile, tile, fblk, fblk],
        out_shape=[jax.ShapeDtypeStruct((T, D), F32), jax.ShapeDtypeStruct((T, D), BF16), hidden, hidden],
        scratch_shapes=[pltpu.VMEM((tm, D), F32)],
        compiler_params=_params("parallel", "arbitrary"))(x, g, w_in, w_in, w_out)


def ffn_dz(dxo, zg, zu, w_out, name):
    T, D = dxo.shape
    F = w_out.shape[0]
    tm, tf = _pick(T, 512, 8), _pick(F, 256, 128)

    def body(dxo_ref, zg_ref, zu_ref, wo_ref, a_ref, dzg_ref, dzu_ref):
        do = (0.5 * dxo_ref[...]).astype(BF16)
        for j in range(F // tf):
            cols = slice(j * tf, (j + 1) * tf)
            zg, zu = zg_ref[:, cols].astype(F32), zu_ref[:, cols].astype(F32)
            s = _sig(zg)
            silu = zg * s
            a_ref[:, cols] = (silu * zu).astype(BF16)
            da = _nt(do, wo_ref[cols, :])
            dzu_ref[:, cols] = (da * silu).astype(BF16)
            dzg_ref[:, cols] = (da * zu * (s + silu * (1.0 - s))).astype(BF16)

    rows = pl.BlockSpec((tm, F), lambda i: (i, 0))
    hidden = jax.ShapeDtypeStruct((T, F), BF16)
    return pl.pallas_call(
        body, name=name, grid=(T // tm,),
        in_specs=[pl.BlockSpec((tm, D), lambda i: (i, 0)), rows, rows, pl.BlockSpec((F, D), lambda i: (0, 0))],
        out_specs=[rows, rows, rows], out_shape=[hidden, hidden, hidden],
        compiler_params=_params("parallel"))(dxo, zg, zu, w_out)


def dh_norm_bwd(x, dxo, g, parts, w, name):
    T, D = x.shape
    F = parts[0].shape[1]
    n = len(parts)
    tm = _pick(T, 512, 8)

    def body(x_ref, dxo_ref, g_ref, *refs):
        a_refs, w_refs, (dx_ref, dg_ref) = refs[:n], refs[n:2 * n], refs[2 * n:]
        i = pl.program_id(0)

        @pl.when(i == 0)
        def _():
            dg_ref[...] = jnp.zeros_like(dg_ref)

        dh = sum(_nt(a_ref[...], w_ref[...]) for a_ref, w_ref in zip(a_refs, w_refs))
        xv = x_ref[...]
        r = lax.rsqrt(jnp.mean(xv * xv, axis=-1, keepdims=True) + EPS)
        xh = xv * r
        dg_ref[...] += jnp.sum(dh * xh, axis=0, keepdims=True)
        dxh = dh * g_ref[...]
        dx_ref[...] = dxo_ref[...] + r * (dxh - xh * jnp.mean(dxh * xh, axis=-1, keepdims=True))

    tile = pl.BlockSpec((tm, D), lambda i: (i, 0))
    vec = pl.BlockSpec((1, D), lambda i: (0, 0))
    return pl.pallas_call(
        body, name=name, grid=(T // tm,),
        in_specs=[tile, tile, vec] + [pl.BlockSpec((tm, F), lambda i: (i, 0))] * n
                 + [pl.BlockSpec((D, F), lambda i, p=p: (0, p)) for p in range(n)],
        out_specs=[tile, vec],
        out_shape=[jax.ShapeDtypeStruct((T, D), F32), jax.ShapeDtypeStruct((1, D), F32)],
        compiler_params=_params("arbitrary"))(x, dxo, g, *parts, *([w] * n))


def _chunks(T, fn):
    def step(c, carry):
        fn(pl.multiple_of(c * SEQ_CHUNK, SEQ_CHUNK))
        return carry
    lax.fori_loop(0, T // SEQ_CHUNK, step, 0)


def _conv_taps(win, ktaps, pad):
    return [(win if k == ktaps - 1 else pltpu.roll(win, ktaps - 1 - k, 0))[pad:, :] for k in range(ktaps)]


def _conv_taps_t(win, ktaps, pad):
    n = win.shape[0]
    return [(win if k == ktaps - 1 else pltpu.roll(win, n - (ktaps - 1 - k), 0))[:n - pad, :] for k in range(ktaps)]


def _col(T, W, idx):
    return pl.BlockSpec((T, W), lambda i, idx=idx: (0, idx))


def _full(shape):
    return pl.BlockSpec(shape, lambda i: (0,) * len(shape))


def mix_a_fwd(z, w, name):
    T, W = z.shape[0], w.shape[1]
    K, P = w.shape[0], 8

    def body(ab_ref, ac_ref, ax_ref, w_ref, y_ref, pp_ref):
        pp_ref[0:P, :] = jnp.zeros((P, W), F32)

        def chunk(s):
            rows = pl.ds(s, SEQ_CHUNK)
            pp_ref[pl.ds(s + P, SEQ_CHUNK), :] = ac_ref[rows, :] * ax_ref[rows, :]
            taps = _conv_taps(pp_ref[pl.ds(s, SEQ_CHUNK + P), :], K, P)
            q = sum(w_ref[k:k + 1, :] * taps[k] for k in range(K))
            y_ref[rows, :] = (ab_ref[rows, :] * q).astype(BF16)

        _chunks(T, chunk)

    return pl.pallas_call(
        body, name=name, grid=(1,),
        in_specs=[_col(T, W, 0), _col(T, W, 1), _col(T, W, 2), _full((K, W))],
        out_specs=_full((T, W)), out_shape=jax.ShapeDtypeStruct((T, W), BF16),
        scratch_shapes=[pltpu.VMEM((T + P, W), F32)],
        compiler_params=_params("arbitrary"))(z, z, z, w)


def mix_a_bwd(z, dy, w, name):
    T, W = z.shape[0], w.shape[1]
    K, P = w.shape[0], 8

    def body(ab_ref, ac_ref, ax_ref, dy_ref, w_ref, dab_ref, dac_ref, dax_ref, dw_ref, pp_ref, dq_ref):
        pp_ref[0:P, :] = jnp.zeros((P, W), F32)
        dq_ref[T:T + P, :] = jnp.zeros((P, W), F32)
        dw_ref[...] = jnp.zeros_like(dw_ref)

        def chunk1(s):
            rows = pl.ds(s, SEQ_CHUNK)
            pp_ref[pl.ds(s + P, SEQ_CHUNK), :] = ac_ref[rows, :] * ax_ref[rows, :]
            taps = _conv_taps(pp_ref[pl.ds(s, SEQ_CHUNK + P), :], K, P)
            q = sum(w_ref[k:k + 1, :] * taps[k] for k in range(K))
            dyv = dy_ref[rows, :]
            dab_ref[rows, :] = (dyv * q).astype(BF16)
            dq = dyv * ab_ref[rows, :]
            dq_ref[rows, :] = dq
            for k in range(K):
                dw_ref[k:k + 1, :] += jnp.sum(dq * taps[k], axis=0, keepdims=True)

        _chunks(T, chunk1)

        def chunk2(s):
            rows = pl.ds(s, SEQ_CHUNK)
            taps = _conv_taps_t(dq_ref[pl.ds(s, SEQ_CHUNK + P), :], K, P)
            dp = sum(w_ref[k:k + 1, :] * taps[k] for k in range(K))
            dac_ref[rows, :] = (dp * ax_ref[rows, :]).astype(BF16)
            dax_ref[rows, :] = (dp * ac_ref[rows, :]).astype(BF16)

        _chunks(T, chunk2)

    tw = jax.ShapeDtypeStruct((T, W), BF16)
    return pl.pallas_call(
        body, name=name, grid=(1,),
        in_specs=[_col(T, W, 0), _col(T, W, 1), _col(T, W, 2), _col(T, W, 0), _full((K, W))],
        out_specs=[_full((T, W))] * 3 + [_full((K, W))],
        out_shape=[tw, tw, tw, jax.ShapeDtypeStruct((K, W), F32)],
        scratch_shapes=[pltpu.VMEM((T + P, W), F32), pltpu.VMEM((T + P, W), F32)],
        compiler_params=_params("arbitrary"))(z, z, z, dy, w)


def _ln_stats(v):
    mu = jnp.mean(v, axis=-1, keepdims=True)
    xc = v - mu
    rstd = lax.rsqrt(jnp.mean(xc * xc, axis=-1, keepdims=True) + EPS)
    return xc * rstd, rstd


def _ln_bwd(dxh, xh, rstd):
    return rstd * (dxh - jnp.mean(dxh, axis=-1, keepdims=True) - xh * jnp.mean(dxh * xh, axis=-1, keepdims=True))


def _tril_bf16(w_ref, h):
    n = w_ref.shape[-1]
    keep = lax.broadcasted_iota(jnp.int32, (n, n), 0) >= lax.broadcasted_iota(jnp.int32, (n, n), 1)
    return jnp.where(keep, w_ref[h], 0.0).astype(BF16)


def mix_b_fwd(z, g, w_s, bias, name):
    T, W = z.shape[0], g.shape[1]
    H, C = w_s.shape[0], w_s.shape[1]
    hd = W // H

    def body(u_ref, v_ref, g_ref, w_ref, b_ref, y_ref):
        wts = [_tril_bf16(w_ref, h) for h in range(H)]
        head = lax.broadcasted_iota(jnp.int32, (C, W), 1) // hd

        def chunk(s):
            rows = pl.ds(s, C)
            xh, _ = _ln_stats(v_ref[rows, :])
            vn = (xh * g_ref[...]).astype(BF16)
            mixed = b_ref[...]
            for h in range(H):
                mixed = mixed + jnp.where(head == h, jnp.dot(wts[h], vn, preferred_element_type=F32), 0.0)
            y_ref[rows, :] = (u_ref[rows, :] * mixed).astype(BF16)

        _chunks(T, chunk)

    return pl.pallas_call(
        body, name=name, grid=(1,),
        in_specs=[_col(T, W, 3), _col(T, W, 4), _full((1, W)), _full((H, C, C)), _full((C, W))],
        out_specs=_full((T, W)), out_shape=jax.ShapeDtypeStruct((T, W), BF16),
        compiler_params=_params("arbitrary"))(z, z, g, w_s, bias)


def mix_b_bwd(z, dy, g, w_s, bias, name):
    T, W = z.shape[0], g.shape[1]
    H, C = w_s.shape[0], w_s.shape[1]
    hd = W // H

    def body(u_ref, v_ref, dy_ref, g_ref, w_ref, b_ref, du_ref, dv_ref, dw_ref, db_ref, dg_ref, dbf_ref):
        wts = [_tril_bf16(w_ref, h) for h in range(H)]
        head = lax.broadcasted_iota(jnp.int32, (C, W), 1) // hd
        dw_ref[...] = jnp.zeros_like(dw_ref)
        dg_ref[...] = jnp.zeros_like(dg_ref)
        dbf_ref[...] = jnp.zeros_like(dbf_ref)

        def chunk(s):
            rows = pl.ds(s, C)
            xh, rstd = _ln_stats(v_ref[rows, :])
            vn = (xh * g_ref[...]).astype(BF16)
            mixed = b_ref[...]
            for h in range(H):
                mixed = mixed + jnp.where(head == h, jnp.dot(wts[h], vn, preferred_element_type=F32), 0.0)
            dyv = dy_ref[rows, :]
            du_ref[rows, :] = (dyv * mixed).astype(BF16)
            dm = dyv * u_ref[rows, :]
            dbf_ref[...] += dm
            dvn = jnp.zeros((C, W), F32)
            for h in range(H):
                dmh = jnp.where(head == h, dm, 0.0).astype(BF16)
                dw_ref[h] += _nt(dmh, vn)
                dvn = dvn + _tn(wts[h], dmh)
            dg_ref[...] += jnp.sum(dvn * xh, axis=0, keepdims=True)
            dv_ref[rows, :] = _ln_bwd(dvn * g_ref[...], xh, rstd).astype(BF16)

        _chunks(T, chunk)

        keep = lax.broadcasted_iota(jnp.int32, (C, C), 0) >= lax.broadcasted_iota(jnp.int32, (C, C), 1)
        lane = lax.broadcasted_iota(jnp.int32, (C, 128), 1)
        db = jnp.zeros((C, 128), F32)
        dbf = dbf_ref[...]
        for h in range(H):
            dw_ref[h] = jnp.where(keep, dw_ref[h], 0.0)
            db = db + jnp.where(lane == h, jnp.sum(jnp.where(head == h, dbf, 0.0), axis=1, keepdims=True), 0.0)
        db_ref[...] = db

    tw = jax.ShapeDtypeStruct((T, W), BF16)
    return pl.pallas_call(
        body, name=name, grid=(1,),
        in_specs=[_col(T, W, 3), _col(T, W, 4), _col(T, W, 1), _full((1, W)), _full((H, C, C)), _full((C, W))],
        out_specs=[_full((T, W)), _full((T, W)), _full((H, C, C)), _full((C, 128)), _full((1, W))],
        out_shape=[tw, tw, jax.ShapeDtypeStruct((H, C, C), F32), jax.ShapeDtypeStruct((C, 128), F32),
                   jax.ShapeDtypeStruct((1, W), F32)],
        scratch_shapes=[pltpu.VMEM((C, W), F32)],
        compiler_params=_params("arbitrary"))(z, z, dy, g, w_s, bias)


def mix_c_fwd(z, w, ln_g, ln_b, name):
    T, W = z.shape[0], w.shape[1]
    K, P = w.shape[0], 32

    def body(a_ref, gt_ref, w_ref, g_ref, b_ref, y_ref, up_ref):
        up_ref[0:P, :] = jnp.zeros((P, W), F32)

        def chunk(s):
            rows = pl.ds(s, SEQ_CHUNK)
            up_ref[pl.ds(s + P, SEQ_CHUNK), :] = a_ref[rows, :] * _sig(gt_ref[rows, :])
            taps = _conv_taps(up_ref[pl.ds(s, SEQ_CHUNK + P), :], K, P)
            q = sum(w_ref[k:k + 1, :] * taps[k] for k in range(K))
            xh, _ = _ln_stats(q)
            r = xh * g_ref[...] + b_ref[...]
            y_ref[rows, :] = (r * _sig(r)).astype(BF16)

        _chunks(T, chunk)

    return pl.pallas_call(
        body, name=name, grid=(1,),
        in_specs=[_col(T, W, 5), _col(T, W, 6), _full((K, W)), _full((1, W)), _full((1, W))],
        out_specs=_full((T, W)), out_shape=jax.ShapeDtypeStruct((T, W), BF16),
        scratch_shapes=[pltpu.VMEM((T + P, W), F32)],
        compiler_params=_params("arbitrary"))(z, z, w, ln_g, ln_b)


def mix_c_bwd(z, dy, w, ln_g, ln_b, name):
    T, W = z.shape[0], w.shape[1]
    K, P = w.shape[0], 32

    def body(a_ref, gt_ref, dy_ref, w_ref, g_ref, b_ref, da_ref, dgt_ref, dw_ref, dg_ref, db_ref, up_ref, dq_ref):
        up_ref[0:P, :] = jnp.zeros((P, W), F32)
        dq_ref[T:T + P, :] = jnp.zeros((P, W), F32)
        dw_ref[...] = jnp.zeros_like(dw_ref)
        dg_ref[...] = jnp.zeros_like(dg_ref)
        db_ref[...] = jnp.zeros_like(db_ref)

        def chunk1(s):
            rows = pl.ds(s, SEQ_CHUNK)
            up_ref[pl.ds(s + P, SEQ_CHUNK), :] = a_ref[rows, :] * _sig(gt_ref[rows, :])
            taps = _conv_taps(up_ref[pl.ds(s, SEQ_CHUNK + P), :], K, P)
            q = sum(w_ref[k:k + 1, :] * taps[k] for k in range(K))
            xh, rstd = _ln_stats(q)
            r = xh * g_ref[...] + b_ref[...]
            sr = _sig(r)
            dr = dy_ref[rows, :] * (sr * (1.0 + r * (1.0 - sr)))
            db_ref[...] += jnp.sum(dr, axis=0, keepdims=True)
            dg_ref[...] += jnp.sum(dr * xh, axis=0, keepdims=True)
            dq = _ln_bwd(dr * g_ref[...], xh, rstd)
            dq_ref[rows, :] = dq
            for k in range(K):
                dw_ref[k:k + 1, :] += jnp.sum(dq * taps[k], axis=0, keepdims=True)

        _chunks(T, chunk1)

        def chunk2(s):
            rows = pl.ds(s, SEQ_CHUNK)
            taps = _conv_taps_t(dq_ref[pl.ds(s, SEQ_CHUNK + P), :], K, P)
            du = sum(w_ref[k:k + 1, :] * taps[k] for k in range(K))
            sg = _sig(gt_ref[rows, :])
            da_ref[rows, :] = (du * sg).astype(BF16)
            dgt_ref[rows, :] = (du * a_ref[rows, :] * sg * (1.0 - sg)).astype(BF16)

        _chunks(T, chunk2)

    tw = jax.ShapeDtypeStruct((T, W), BF16)
    vec = jax.ShapeDtypeStruct((1, W), F32)
    return pl.pallas_call(
        body, name=name, grid=(1,),
        in_specs=[_col(T, W, 5), _col(T, W, 6), _col(T, W, 2), _full((K, W)), _full((1, W)), _full((1, W))],
        out_specs=[_full((T, W)), _full((T, W)), _full((K, W)), _full((1, W)), _full((1, W))],
        out_shape=[tw, tw, jax.ShapeDtypeStruct((K, W), F32), vec, vec],
        scratch_shapes=[pltpu.VMEM((T + P, W), F32), pltpu.VMEM((T + P, W), F32)],
        compiler_params=_params("arbitrary"))(z, z, dy, w, ln_g, ln_b)


def _pool_select(levels, W, rows):
    group = lax.broadcasted_iota(jnp.int32, (rows, W), 1) // (W // len(POOL_WINDOWS))
    out = levels[-1]
    for gi in range(len(POOL_WINDOWS) - 2, -1, -1):
        out = jnp.where(group == gi, levels[gi], out)
    return out


def _pool_count(s, W):
    t = s + lax.broadcasted_iota(jnp.int32, (SEQ_CHUNK, W), 0)
    group = lax.broadcasted_iota(jnp.int32, (SEQ_CHUNK, W), 1) // (W // len(POOL_WINDOWS))
    win = jnp.full((SEQ_CHUNK, W), POOL_WINDOWS[-1], jnp.int32)
    for gi in range(len(POOL_WINDOWS) - 2, -1, -1):
        win = jnp.where(group == gi, POOL_WINDOWS[gi], win)
    return jnp.minimum(t + 1, win).astype(F32)


def _pooled(wp_ref, s, W, P):
    win = wp_ref[pl.ds(s, SEQ_CHUNK + P), :]
    levels, acc, shift = [], win, 1
    for _ in POOL_WINDOWS:
        acc = acc + pltpu.roll(acc, shift, 0)
        levels.append(acc[P:, :])
        shift *= 2
    return _pool_select(levels, W, SEQ_CHUNK) / _pool_count(s, W) - win[P:, :]


def mix_d_fwd(z, pbd, scale, name):
    T, W = z.shape[0], scale.shape[1]
    P = 16

    def body(x_ref, p_ref, s_ref, y_ref, wp_ref):
        wp_ref[0:P, :] = jnp.zeros((P, W), F32)

        def chunk(s):
            rows = pl.ds(s, SEQ_CHUNK)
            wp_ref[pl.ds(s + P, SEQ_CHUNK), :] = x_ref[rows, :]
            pooled = _pooled(wp_ref, s, W, P).astype(BF16)
            y_ref[rows, :] = (jnp.dot(pooled, p_ref[...], preferred_element_type=F32) * s_ref[...]).astype(BF16)

        _chunks(T, chunk)

    return pl.pallas_call(
        body, name=name, grid=(1,),
        in_specs=[_col(T, W, 7), _full((W, W)), _full((1, W))],
        out_specs=_full((T, W)), out_shape=jax.ShapeDtypeStruct((T, W), BF16),
        scratch_shapes=[pltpu.VMEM((T + P, W), F32)],
        compiler_params=_params("arbitrary"))(z, pbd, scale)


def mix_d_bwd(z, dy, pbd, scale, name):
    T, W = z.shape[0], scale.shape[1]
    P = 16

    def body(x_ref, dy_ref, p_ref, s_ref, dx_ref, dp_ref, ds_ref, wp_ref, e_ref, dpool_ref):
        wp_ref[0:P, :] = jnp.zeros((P, W), F32)
        e_ref[T:T + P, :] = jnp.zeros((P, W), F32)
        dp_ref[...] = jnp.zeros_like(dp_ref)
        ds_ref[...] = jnp.zeros_like(ds_ref)

        def chunk1(s):
            rows = pl.ds(s, SEQ_CHUNK)
            wp_ref[pl.ds(s + P, SEQ_CHUNK), :] = x_ref[rows, :]
            pooled = _pooled(wp_ref, s, W, P).astype(BF16)
            yl = jnp.dot(pooled, p_ref[...], preferred_element_type=F32)
            dyv = dy_ref[rows, :]
            ds_ref[...] += jnp.sum(dyv * yl, axis=0, keepdims=True)
            dyl = (dyv * s_ref[...]).astype(BF16)
            dp_ref[...] += _tn(pooled, dyl)
            dpool = _nt(dyl, p_ref[...])
            dpool_ref[rows, :] = dpool
            e_ref[rows, :] = dpool / _pool_count(s, W)

        _chunks(T, chunk1)

        def chunk2(s):
            rows = pl.ds(s, SEQ_CHUNK)
            win = e_ref[pl.ds(s, SEQ_CHUNK + P), :]
            n = SEQ_CHUNK + P
            levels, acc, shift = [], win, 1
            for _ in POOL_WINDOWS:
                acc = acc + pltpu.roll(acc, n - shift, 0)
                levels.append(acc[:SEQ_CHUNK, :])
                shift *= 2
            dx_ref[rows, :] = (_pool_select(levels, W, SEQ_CHUNK) - dpool_ref[rows, :]).astype(BF16)

        _chunks(T, chunk2)

    return pl.pallas_call(
        body, name=name, grid=(1,),
        in_specs=[_col(T, W, 7), _col(T, W, 3), _full((W, W)), _full((1, W))],
        out_specs=[_full((T, W)), _full((W, W)), _full((1, W))],
        out_shape=[jax.ShapeDtypeStruct((T, W), BF16), jax.ShapeDtypeStruct((W, W), F32),
                   jax.ShapeDtypeStruct((1, W), F32)],
        scratch_shapes=[pltpu.VMEM((T + P, W), F32), pltpu.VMEM((T + P, W), F32), pltpu.VMEM((T, W), F32)],
        compiler_params=_params("arbitrary"))(z, dy, pbd, scale)


def attn_fwd(q, kv, name):
    T, D = q.shape
    M = kv.shape[0]
    hd = D // N_HEADS
    tm = _pick(T, 512, 8)
    sc = 1.0 / math.sqrt(hd)

    def body(q_ref, k_ref, v_ref, o_ref):
        for h in range(N_HEADS):
            cols = slice(h * hd, (h + 1) * hd)
            s = _nt(q_ref[:, cols].astype(BF16), k_ref[:, cols].astype(BF16)) * sc
            p = jnp.exp(s - jnp.max(s, axis=-1, keepdims=True))
            p = p / jnp.sum(p, axis=-1, keepdims=True)
            o_ref[:, cols] = jnp.dot(p.astype(BF16), v_ref[:, cols].astype(BF16),
                                     preferred_element_type=F32).astype(BF16)

    return pl.pallas_call(
        body, name=name, grid=(T // tm,),
        in_specs=[pl.BlockSpec((tm, D), lambda i: (i, 0)), pl.BlockSpec((M, D), lambda i: (0, 0)),
                  pl.BlockSpec((M, D), lambda i: (0, 1))],
        out_specs=pl.BlockSpec((tm, D), lambda i: (i, 0)),
        out_shape=jax.ShapeDtypeStruct((T, D), BF16),
        compiler_params=_params("parallel"))(q, kv, kv)


def attn_bwd(q, kv, do, name):
    T, D = q.shape
    M = kv.shape[0]
    hd = D // N_HEADS
    tm = _pick(T, 512, 8)
    sc = 1.0 / math.sqrt(hd)

    def body(q_ref, k_ref, v_ref, do_ref, dq_ref, dk_ref, dv_ref):
        i = pl.program_id(0)

        @pl.when(i == 0)
        def _():
            dk_ref[...] = jnp.zeros_like(dk_ref)
            dv_ref[...] = jnp.zeros_like(dv_ref)

        for h in range(N_HEADS):
            cols = slice(h * hd, (h + 1) * hd)
            qh, kh = q_ref[:, cols].astype(BF16), k_ref[:, cols].astype(BF16)
            vh, doh = v_ref[:, cols].astype(BF16), do_ref[:, cols].astype(BF16)
            s = _nt(qh, kh) * sc
            p = jnp.exp(s - jnp.max(s, axis=-1, keepdims=True))
            p = p / jnp.sum(p, axis=-1, keepdims=True)
            dp = _nt(doh, vh)
            dv_ref[:, cols] += _tn(p.astype(BF16), doh)
            ds = (p * (dp - jnp.sum(dp * p, axis=-1, keepdims=True)) * sc).astype(BF16)
            dq_ref[:, cols] = jnp.dot(ds, kh, preferred_element_type=F32).astype(BF16)
            dk_ref[:, cols] += _tn(ds, qh)

    tile = pl.BlockSpec((tm, D), lambda i: (i, 0))
    mem = jax.ShapeDtypeStruct((M, D), F32)
    return pl.pallas_call(
        body, name=name, grid=(T // tm,),
        in_specs=[tile, pl.BlockSpec((M, D), lambda i: (0, 0)), pl.BlockSpec((M, D), lambda i: (0, 1)), tile],
        out_specs=[tile, pl.BlockSpec((M, D), lambda i: (0, 0)), pl.BlockSpec((M, D), lambda i: (0, 0))],
        out_shape=[jax.ShapeDtypeStruct((T, D), BF16), mem, mem],
        compiler_params=_params("arbitrary"))(q, kv, kv, do)


def loss_head(x, g, target, name):
    T, D = x.shape
    tm = _pick(T, 512, 8)

    def body(x_ref, g_ref, t_ref, l_ref, dx_ref, dg_ref):
        i = pl.program_id(0)

        @pl.when(i == 0)
        def _():
            l_ref[...] = jnp.zeros_like(l_ref)
            dg_ref[...] = jnp.zeros_like(dg_ref)

        xv = x_ref[...]
        r = lax.rsqrt(jnp.mean(xv * xv, axis=-1, keepdims=True) + EPS)
        xh = xv * r
        err = xh * g_ref[...] - t_ref[...]
        l_ref[...] += 0.5 * jnp.sum(jnp.mean(err * err, axis=-1, keepdims=True), axis=0, keepdims=True)
        dy = err * (1.0 / D)
        dg_ref[...] += jnp.sum(dy * xh, axis=0, keepdims=True)
        dxh = dy * g_ref[...]
        dx_ref[...] = r * (dxh - xh * jnp.mean(dxh * xh, axis=-1, keepdims=True))

    tile = pl.BlockSpec((tm, D), lambda i: (i, 0))
    vec = pl.BlockSpec((1, D), lambda i: (0, 0))
    return pl.pallas_call(
        body, name=name, grid=(T // tm,),
        in_specs=[tile, vec, tile],
        out_specs=[pl.BlockSpec((1, 128), lambda i: (0, 0)), tile, vec],
        out_shape=[jax.ShapeDtypeStruct((1, 128), F32), jax.ShapeDtypeStruct((T, D), F32),
                   jax.ShapeDtypeStruct((1, D), F32)],
        compiler_params=_params("arbitrary"))(x, g, target)


def _block_diag(p):
    G, gd, _ = p.shape
    rows = [jnp.concatenate([p[g] if g == c else jnp.zeros((gd, gd), p.dtype) for c in range(G)], axis=1)
            for g in range(G)]
    return jnp.concatenate(rows, axis=0)


class _LazyWeight:
    def __init__(self, fetch, name, latest):
        self.fetch, self.name, self.latest = fetch, name, latest

    def __getitem__(self, l):
        return self.fetch(self.name, l, self.latest[0])


def _local_step(x, mem, target, fetch, ws, L, progress=lambda event, l, grads, values: values):
    T, D = x.shape
    W = D // 4
    H = ws["sgu_w"].shape[1]
    row = lambda v: v.reshape(1, -1)
    latest = [x]
    wb = {n: _LazyWeight(fetch, n, latest) for n in BIG}
    saved = []
    for l in range(L):
        s = {"x0": x}
        latest[0] = x
        x, *s["ffn1"] = ffn_fwd(x, row(ws["norm_ffn1"][l]), wb["ffn1_w_in"][l], wb["ffn1_w_out"][l], "ffn_fwd")
        s["x1"] = x
        latest[0] = x
        z, s["h_mix"] = norm_matmul(x, row(ws["norm_mix"][l]), wb["mix_w_in"][l], "mix_in")
        s["z"] = z
        s["bias"] = jnp.repeat(ws["sgu_b"][l].T, W // H, axis=1)
        s["pbd"] = _block_diag(ws["pool_w"][l]).astype(BF16)
        y = jnp.concatenate([
            mix_a_fwd(z, ws["sconv_w"][l], "mix_a_fwd"),
            mix_b_fwd(z, row(ws["sgu_norm_g"][l]), ws["sgu_w"][l], s["bias"], "mix_b_fwd"),
            mix_c_fwd(z, ws["cconv_w"][l], row(ws["cconv_ln_g"][l]), row(ws["cconv_ln_b"][l]), "mix_c_fwd"),
            mix_d_fwd(z, s["pbd"], row(ws["pool_scale"][l]), "mix_d_fwd")], axis=1)
        s["y"] = y
        x = matmul_res(x, y, wb["mix_w_out"][l], "mix_out")
        s["x2"] = x
        latest[0] = x
        s["q"], s["hq"] = norm_matmul(x, row(ws["norm_xattn"][l]), wb["xattn_wq"][l], "attn_q", out_dtype=BF16)
        s["kv"], s["mn"] = norm_matmul(mem, row(ws["norm_mem"][l]), wb["xattn_wkv"][l], "attn_kv")
        s["o"] = attn_fwd(s["q"], s["kv"], "attn_fwd")
        x = matmul_res(x, s["o"], wb["xattn_wo"][l], "attn_out")
        s["x3"] = x
        latest[0] = x
        x, *s["ffn2"] = ffn_fwd(x, row(ws["norm_ffn2"][l]), wb["ffn2_w_in"][l], wb["ffn2_w_out"][l], "ffn_fwd")
        saved.append(s)

    loss, dx, dg_final = loss_head(x, row(ws["norm_final"]), target, "loss_head")
    grads = {n: [None] * L for n in WEIGHTS if n != "norm_final"}
    grads["norm_final"] = dg_final.reshape(-1)

    def pin(dx, names, l):
        dx, made = lax.optimization_barrier((dx, [grads[n][l] for n in names]))
        for n, g in zip(names, made):
            grads[n][l] = g
        return dx

    def after_stages(event, l, values):
        return progress(event, l, grads, values)

    def ffn_back(xin, kept, dxo, gname, win, wout, l, event):
        h, zg, zu = kept
        a, dzg, dzu = ffn_dz(dxo, zg, zu, wb[wout][l], "ffn_dz")
        last = l == 0 and event == "ffn1_mid"
        if not last:
            dxn, dg = dh_norm_bwd(xin, dxo, row(ws[gname][l]), [dzg, dzu], wb[win][l], "ffn_dh")
            dxn, h, a = after_stages(event, l, (dxn, h, a))
        else:
            h, a = after_stages(event, l, (h, a))
        grads[win][l] = matmul_tn(h, dzg, 1.0, "ffn_dwin", b2=dzu)
        grads[wout][l] = matmul_tn(a, dxo, 0.5, "ffn_dwout")
        if last:
            dzg, dzu = after_stages("ffn1_grads", l, (dzg, dzu))
            dxn, dg = dh_norm_bwd(xin, dxo, row(ws[gname][l]), [dzg, dzu], wb[win][l], "ffn_dh")
        grads[gname][l] = dg.reshape(-1)
        return dxn if last else pin(dxn, (win, wout), l)

    for l in reversed(range(L)):
        s = saved[l]
        dx = ffn_back(s["x3"], s["ffn2"], dx, "norm_ffn2", "ffn2_w_in", "ffn2_w_out", l, "ffn2_mid")
        dx, = after_stages("ffn2", l, (dx,))
        grads["xattn_wo"][l] = matmul_tn(s["o"], dx, 1.0, "dw_sq")
        do = matmul_nt(dx, wb["xattn_wo"][l], "attn_do", out_dtype=BF16)
        dq, dk, dv = attn_bwd(s["q"], s["kv"], do, "attn_bwd")
        grads["xattn_wq"][l] = matmul_tn(s["hq"], dq, 1.0, "dw_sq")
        dx, dg = dh_norm_bwd(s["x2"], dx, row(ws["norm_xattn"][l]), [dq], wb["xattn_wq"][l], "attn_dh")
        grads["norm_xattn"][l] = dg.reshape(-1)
        dkv = jnp.concatenate([dk, dv], axis=1)
        grads["xattn_wkv"][l] = matmul_tn(s["mn"], dkv, 1.0, "attn_dwkv")
        dmn = matmul_nt(dkv, wb["xattn_wkv"][l], "attn_dmn")
        _, dg = rmsnorm_bwd(None, dmn, mem, row(ws["norm_mem"][l]), "norm_mem_bwd")
        grads["norm_mem"][l] = dg.reshape(-1)
        dx = pin(dx, ("xattn_wo", "xattn_wq", "xattn_wkv", "norm_mem"), l)
        dx, = after_stages("attn", l, (dx,))
        grads["mix_w_out"][l] = matmul_tn(s["y"], dx, 1.0, "dw_sq")
        dy = matmul_nt(dx, wb["mix_w_out"][l], "mix_dy")
        z = s["z"]
        dab, dac, dax, dws = mix_a_bwd(z, dy, ws["sconv_w"][l], "mix_a_bwd")
        dbu, dbv, dwsgu, dbs, dgs = mix_b_bwd(z, dy, row(ws["sgu_norm_g"][l]), ws["sgu_w"][l], s["bias"], "mix_b_bwd")
        dca, dcg, dwc, dgc, dbc = mix_c_bwd(z, dy, ws["cconv_w"][l], row(ws["cconv_ln_g"][l]),
                                            row(ws["cconv_ln_b"][l]), "mix_c_bwd")
        ddw, dpbd, dsc = mix_d_bwd(z, dy, s["pbd"], row(ws["pool_scale"][l]), "mix_d_bwd")
        grads["sconv_w"][l], grads["cconv_w"][l] = dws, dwc
        grads["sgu_w"][l], grads["sgu_b"][l], grads["sgu_norm_g"][l] = dwsgu, dbs[:, :H].T, dgs.reshape(-1)
        grads["cconv_ln_g"][l], grads["cconv_ln_b"][l] = dgc.reshape(-1), dbc.reshape(-1)
        gd = W // len(POOL_WINDOWS)
        grads["pool_w"][l] = jnp.stack([dpbd[g * gd:(g + 1) * gd, g * gd:(g + 1) * gd] for g in range(len(POOL_WINDOWS))])
        grads["pool_scale"][l] = dsc.reshape(-1)
        dz = jnp.concatenate([dab, dac, dax, dbu, dbv, dca, dcg, ddw], axis=1)
        grads["mix_w_in"][l] = matmul_tn(s["h_mix"], dz, 1.0, "mix_dwin")
        dx, dg = dh_norm_bwd(s["x1"], dx, row(ws["norm_mix"][l]), [dz], wb["mix_w_in"][l], "mix_dh")
        grads["norm_mix"][l] = dg.reshape(-1)
        dx = pin(dx, ("mix_w_out", "mix_w_in"), l)
        dx, = after_stages("mix", l, (dx,))
        dx = ffn_back(s["x0"], s["ffn1"], dx, "norm_ffn1", "ffn1_w_in", "ffn1_w_out", l, "ffn1_mid")
        dx, = after_stages("layer", l, (dx,))

    return loss[0, 0], dx, grads


ANY = pl.BlockSpec(memory_space=pl.ANY)


def _other_chips(x, y):
    return [(1 - x, y), (x, 1 - y), (1 - x, 1 - y)]


def _shard_slice(ref, axis, chip, size):
    idx = [slice(None)] * len(ref.shape)
    idx[axis] = pl.ds(pl.multiple_of(chip * size, size), size)
    return ref.at[tuple(idx)]


def all_gather_chips(shards, axes, name):
    n = len(shards)

    def body(*refs):
        ins, outs = refs[:n], refs[n:2 * n]
        send, recv, loc = refs[2 * n:]
        x, y, c = lax.axis_index("x"), lax.axis_index("y"), lax.axis_index("c")
        me = 2 * x + y
        chips = _other_chips(x, y)
        started = []
        for i in range(n):
            size = ins[i].shape[axes[i]]
            cp = pltpu.make_async_copy(ins[i], _shard_slice(outs[i], axes[i], me, size), loc.at[i])
            cp.start()
            started.append(cp)
        sends = []
        for i in range(n):
            size = ins[i].shape[axes[i]]
            for j, (px, py) in enumerate(chips):
                cp = pltpu.make_async_remote_copy(
                    src_ref=ins[i], dst_ref=_shard_slice(outs[i], axes[i], me, size),
                    send_sem=send.at[i, j], recv_sem=recv.at[i, j], device_id=(px, py, c), device_id_type=MESH_ID)
                cp.start()
                sends.append(cp)
        for i in range(n):
            size = ins[i].shape[axes[i]]
            for j, (px, py) in enumerate(chips):
                pltpu.make_async_remote_copy(
                    src_ref=ins[i], dst_ref=_shard_slice(outs[i], axes[i], 2 * px + py, size),
                    send_sem=send.at[i, j], recv_sem=recv.at[i, j], device_id=(px, py, c),
                    device_id_type=MESH_ID).wait_recv()
        for cp in sends:
            cp.wait_send()
        for cp in started:
            cp.wait()

    def full(a, ax):
        shape = list(a.shape)
        shape[ax] *= N_CHIPS
        return jax.ShapeDtypeStruct(tuple(shape), a.dtype)

    return pl.pallas_call(
        body, name=name, in_specs=[ANY] * n, out_specs=[ANY] * n,
        out_shape=[full(a, ax) for a, ax in zip(shards, axes)],
        scratch_shapes=[pltpu.SemaphoreType.DMA((n, 3)), pltpu.SemaphoreType.DMA((n, 3)),
                        pltpu.SemaphoreType.DMA((n,))],
        compiler_params=pltpu.CompilerParams(has_side_effects=True))(*shards)


def cast_into_slot(shard, axis, chip, name):
    L, K, N = shard.shape
    bm = _pick(K, 512, 16)
    full = (K * N_CHIPS, N) if axis == 1 else (K, N * N_CHIPS)
    nb = K // bm

    def body(c_ref, s_ref, *o_refs):
        for l in range(L):
            o_refs[l][...] = s_ref[l].astype(BF16)

    out_map = (lambda i, c: (c[0] * nb + i, 0)) if axis == 1 else (lambda i, c: (i, c[0]))
    spec = pltpu.PrefetchScalarGridSpec(
        num_scalar_prefetch=1, grid=(nb,),
        in_specs=[pl.BlockSpec((L, bm, N), lambda i, c: (0, i, 0))],
        out_specs=[pl.BlockSpec((bm, N), out_map)] * L)
    return pl.pallas_call(body, name=name, grid_spec=spec, out_shape=[jax.ShapeDtypeStruct(full, BF16)] * L,
                          compiler_params=_params("parallel"))(chip, shard)


HBM = pl.BlockSpec(memory_space=pltpu.HBM)
SEM = pl.BlockSpec(memory_space=pltpu.SEMAPHORE)
DATAFLOW = pltpu.SideEffectType.DATAFLOW_SIDE_EFFECTING


def split_start(name, sources, landing, n_sems, make, after):
    ns, nl, na = len(sources), len(landing), len(after)

    def body(*refs):
        out, _ = make(refs[:ns], refs[ns:ns + nl], refs[ns + nl + na], refs[ns + nl + na + 1])
        for cp in out:
            cp.start()
        refs[-1][...] = jnp.zeros_like(refs[-1])

    hbm = lambda b: pltpu.with_memory_space_constraint(b, pltpu.HBM)
    res = pl.pallas_call(
        body, name=name,
        out_shape=(pltpu.SemaphoreType.DMA((n_sems,)), pltpu.SemaphoreType.DMA((n_sems,)),
                   *[pltpu.HBM(b.shape, b.dtype) for b in landing], jax.ShapeDtypeStruct((8, 128), F32)),
        in_specs=[HBM] * (ns + nl) + [ANY] * na, out_specs=(SEM, SEM, *[HBM] * nl, pl.BlockSpec(memory_space=pltpu.VMEM)),
        input_output_aliases={ns + i: 2 + i for i in range(nl)},
        compiler_params=pltpu.CompilerParams(has_side_effects=DATAFLOW))(
            *[hbm(b) for b in sources], *[hbm(b) for b in landing], *after)
    return res[0], res[1], list(res[2:2 + nl]), res[-1]


def split_wait(name, sources, landing, send, recv, make, after):
    ns, nl = len(sources), len(landing)

    def body(*refs):
        _, back = make(refs[:ns], refs[ns:ns + nl], refs[ns + nl], refs[ns + nl + 1])
        for cp in back:
            cp.wait_send()
            cp.wait_recv()

    return list(pl.pallas_call(
        body, name=name, out_shape=tuple(pltpu.HBM(b.shape, b.dtype) for b in landing),
        in_specs=[HBM] * (ns + nl) + [SEM, SEM] + [ANY] * len(after), out_specs=tuple([HBM] * nl),
        input_output_aliases={ns + i: i for i in range(nl)},
        compiler_params=pltpu.CompilerParams(has_side_effects=DATAFLOW))(
            *[pltpu.with_memory_space_constraint(b, pltpu.HBM) for b in sources], *landing, send, recv, *after))


def _half_slot(buf, axis, chip, half):
    K, N = buf.shape
    if axis == 1:
        n = N // N_CHIPS
        return buf.at[pl.ds(pl.multiple_of(half * (K // 2), K // 2), K // 2), pl.ds(pl.multiple_of(chip * n, n), n)]
    k2 = K // N_CHIPS // 2
    return buf.at[pl.ds(pl.multiple_of((2 * chip + half) * k2, k2), k2), :]


def gather_copies(axes):
    def make(_, bufs, send, recv):
        x, y, c = lax.axis_index("x"), lax.axis_index("y"), lax.axis_index("c")
        out, back = [], []
        for i, (buf, ax) in enumerate(zip(bufs, axes)):
            mine = _half_slot(buf, ax, 2 * x + y, c)
            for j, (px, py) in enumerate(_other_chips(x, y)):
                kw = dict(send_sem=send.at[3 * i + j], recv_sem=recv.at[3 * i + j], device_id=(px, py, c),
                          device_id_type=MESH_ID)
                out.append(pltpu.make_async_remote_copy(src_ref=mine, dst_ref=mine, **kw))
                back.append(pltpu.make_async_remote_copy(src_ref=mine, dst_ref=_half_slot(buf, ax, 2 * px + py, c), **kw))
        return out, back
    return make


def forward_copies(axes):
    def make(_, bufs, send, recv):
        x, y, c = lax.axis_index("x"), lax.axis_index("y"), lax.axis_index("c")
        out, back = [], []
        for i, (buf, ax) in enumerate(zip(bufs, axes)):
            for j, (px, py) in enumerate(_other_chips(x, y)):
                have = _half_slot(buf, ax, 2 * px + py, c)
                kw = dict(send_sem=send.at[3 * i + j], recv_sem=recv.at[3 * i + j], device_id=(x, y, 1 - c),
                          device_id_type=MESH_ID)
                out.append(pltpu.make_async_remote_copy(src_ref=have, dst_ref=have, **kw))
                back.append(pltpu.make_async_remote_copy(src_ref=have, dst_ref=_half_slot(buf, ax, 2 * px + py, 1 - c), **kw))
        return out, back
    return make


def forward_sibling(bufs, axes, name):
    n = len(bufs)

    def body(*refs):
        out, back = forward_copies(axes)(None, refs[:n], *refs[2 * n:])
        for cp in out:
            cp.start()
        for cp in back:
            cp.wait_recv()
        for cp in out:
            cp.wait_send()

    return pl.pallas_call(
        body, name=name, in_specs=[ANY] * n, out_specs=[ANY] * n,
        out_shape=[jax.ShapeDtypeStruct(b.shape, b.dtype) for b in bufs],
        input_output_aliases={i: i for i in range(n)},
        scratch_shapes=[pltpu.SemaphoreType.DMA((3 * n,)), pltpu.SemaphoreType.DMA((3 * n,))],
        compiler_params=pltpu.CompilerParams(has_side_effects=True))(*bufs)


def all_reduce_small(p, name):
    R = p.shape[0]

    def body(p_ref, o_ref, sib_ref, chip_ref, send, recv):
        x, y, c = lax.axis_index("x"), lax.axis_index("y"), lax.axis_index("c")
        me = 2 * x + y
        chips = _other_chips(x, y)
        pair = pltpu.make_async_remote_copy(src_ref=p_ref, dst_ref=sib_ref, send_sem=send.at[0], recv_sem=recv.at[0],
                                            device_id=(x, y, 1 - c), device_id_type=MESH_ID)
        pair.start()
        pair.wait()
        chip_ref[me] = p_ref[...] + sib_ref[...]
        sends = []
        for j, (px, py) in enumerate(chips):
            cp = pltpu.make_async_remote_copy(src_ref=chip_ref.at[me], dst_ref=chip_ref.at[me], send_sem=send.at[1 + j],
                                              recv_sem=recv.at[1 + j], device_id=(px, py, c), device_id_type=MESH_ID)
            cp.start()
            sends.append(cp)
        for j, (px, py) in enumerate(chips):
            pltpu.make_async_remote_copy(src_ref=chip_ref.at[me], dst_ref=chip_ref.at[2 * px + py], send_sem=send.at[1 + j],
                                         recv_sem=recv.at[1 + j], device_id=(px, py, c), device_id_type=MESH_ID).wait_recv()
        for cp in sends:
            cp.wait_send()
        o_ref[...] = ((chip_ref[0] + chip_ref[1]) + chip_ref[2]) + chip_ref[3]

    vm = pl.BlockSpec(memory_space=pltpu.VMEM)
    return pl.pallas_call(
        body, name=name, in_specs=[vm], out_specs=vm, out_shape=jax.ShapeDtypeStruct((R, 128), F32),
        scratch_shapes=[pltpu.VMEM((R, 128), F32), pltpu.VMEM((N_CHIPS, R, 128), F32),
                        pltpu.SemaphoreType.DMA((4,)), pltpu.SemaphoreType.DMA((4,))],
        compiler_params=pltpu.CompilerParams(has_side_effects=True, vmem_limit_bytes=VMEM_LIMIT))(p)


def _grad_view(g, axis):
    K, N = g.shape
    return g.reshape(1, 2, K // 2, N) if axis == 1 else g.reshape(N_CHIPS, 2, K // N_CHIPS // 2, N)


def pair_copies(gvs, others, send, recv):
    x, y, c = lax.axis_index("x"), lax.axis_index("y"), lax.axis_index("c")
    out = [pltpu.make_async_remote_copy(src_ref=gv.at[:, 1 - c], dst_ref=o, send_sem=send.at[i], recv_sem=recv.at[i],
                                        device_id=(x, y, 1 - c), device_id_type=MESH_ID)
           for i, (gv, o) in enumerate(zip(gvs, others))]
    return out, out


def chip_copies(axes):
    def piece(s, ax, chip):
        if ax == 1:
            n = s.shape[2] // N_CHIPS
            return s.at[0, :, pl.ds(pl.multiple_of(chip * n, n), n)]
        return s.at[chip]

    def make(sums, qs, send, recv):
        x, y, c = lax.axis_index("x"), lax.axis_index("y"), lax.axis_index("c")
        out = []
        for i, (s, q, ax) in enumerate(zip(sums, qs, axes)):
            for j, (px, py) in enumerate(_other_chips(x, y)):
                out.append(pltpu.make_async_remote_copy(
                    src_ref=piece(s, ax, 2 * px + py), dst_ref=q.at[j], send_sem=send.at[3 * i + j],
                    recv_sem=recv.at[3 * i + j], device_id=(px, py, c), device_id_type=MESH_ID))
        return out, out
    return make


def share_copies(_, halves, send, recv):
    x, y, c = lax.axis_index("x"), lax.axis_index("y"), lax.axis_index("c")
    kw = lambda i: dict(send_sem=send.at[i], recv_sem=recv.at[i], device_id=(x, y, 1 - c), device_id_type=MESH_ID)
    out = [pltpu.make_async_remote_copy(src_ref=h.at[c], dst_ref=h.at[c], **kw(i)) for i, h in enumerate(halves)]
    back = [pltpu.make_async_remote_copy(src_ref=h.at[c], dst_ref=h.at[1 - c], **kw(i)) for i, h in enumerate(halves)]
    return out, back


def share_sibling(halves, name):
    n = len(halves)

    def body(*refs):
        out, back = share_copies(None, refs[:n], *refs[2 * n:])
        for cp in out:
            cp.start()
        for cp in back:
            cp.wait_recv()
        for cp in out:
            cp.wait_send()

    return pl.pallas_call(
        body, name=name, in_specs=[ANY] * n, out_specs=[ANY] * n,
        out_shape=[jax.ShapeDtypeStruct(a.shape, a.dtype) for a in halves],
        input_output_aliases={i: i for i in range(n)},
        scratch_shapes=[pltpu.SemaphoreType.DMA((n,)), pltpu.SemaphoreType.DMA((n,))],
        compiler_params=pltpu.CompilerParams(has_side_effects=True))(*halves)


def add_pair(gv, other, place, name):
    A, _, rows, N = gv.shape
    bm, bn = _pick(rows, 512, 16), _pick(N, 1408, 128)

    def body(p_ref, g_ref, o_ref, out_ref):
        out_ref[...] = (g_ref[...] + o_ref[...]).astype(GRAD_WIRE)

    spec = pltpu.PrefetchScalarGridSpec(
        num_scalar_prefetch=1, grid=(A, rows // bm, N // bn),
        in_specs=[pl.BlockSpec((None, None, bm, bn), lambda a, i, j, p: (a, p[1], i, j)),
                  pl.BlockSpec((None, bm, bn), lambda a, i, j, p: (a, i, j))],
        out_specs=pl.BlockSpec((None, bm, bn), lambda a, i, j, p: (a, i, j)))
    return pl.pallas_call(body, name=name, grid_spec=spec, out_shape=jax.ShapeDtypeStruct((A, rows, N), GRAD_WIRE),
                          compiler_params=_params("parallel", "parallel", "parallel"))(place, gv, other)


def add_chips(s, q, axis, place, name):
    _, rows, n = q.shape
    bm, bn = _pick(rows, 256, 16), _pick(n, 1408, 128)
    nbj = n // bn

    def body(p_ref, s_ref, q_ref, o_ref):
        o_ref[...] = ((s_ref[...].astype(F32) + q_ref[0].astype(F32)) + q_ref[1].astype(F32)) + q_ref[2].astype(F32)

    mine = (lambda i, j, p: (p[0], i, j)) if axis == 0 else (lambda i, j, p: (0, i, p[0] * nbj + j))
    spec = pltpu.PrefetchScalarGridSpec(
        num_scalar_prefetch=1, grid=(rows // bm, nbj),
        in_specs=[pl.BlockSpec((None, bm, bn), mine), pl.BlockSpec((3, bm, bn), lambda i, j, p: (0, i, j))],
        out_specs=pl.BlockSpec((None, bm, bn), lambda i, j, p: (p[1], i, j)))
    return pl.pallas_call(body, name=name, grid_spec=spec, out_shape=jax.ShapeDtypeStruct((2, rows, n), F32),
                          compiler_params=_params("parallel", "parallel"))(place, s, q)


def adamw(w, g, m, v, name):
    R, N = w.shape
    bm = _pick(R, 512, 8)
    c1 = 1.0 / (1.0 - ADAM_B1 ** ADAM_STEP)
    c2 = 1.0 / (1.0 - ADAM_B2 ** ADAM_STEP)

    def body(w_ref, g_ref, m_ref, v_ref, d_ref, nm_ref, nv_ref):
        gv = g_ref[...]
        nm = ADAM_B1 * m_ref[...] + (1.0 - ADAM_B1) * gv
        nv = ADAM_B2 * v_ref[...] + (1.0 - ADAM_B2) * (gv * gv)
        nm_ref[...] = nm
        nv_ref[...] = nv
        d_ref[...] = -ADAM_LR * ((nm * c1) / (jnp.sqrt(nv * c2) + ADAM_EPS) + ADAM_WD * w_ref[...])

    blk = pl.BlockSpec((bm, N), lambda i: (i, 0))
    out = jax.ShapeDtypeStruct((R, N), F32)
    return pl.pallas_call(body, name=name, grid=(R // bm,), in_specs=[blk] * 4, out_specs=[blk] * 3,
                          out_shape=[out, out, out], compiler_params=_params("parallel"))(w, g, m, v)


def adamw_layers(w, g0, g1, m, v, name):
    _, k, n = w.shape
    bm = _pick(k, 128, 8)
    c1 = 1.0 / (1.0 - ADAM_B1 ** ADAM_STEP)
    c2 = 1.0 / (1.0 - ADAM_B2 ** ADAM_STEP)

    def body(w_ref, g0_ref, g1_ref, m_ref, v_ref, g_ref, d_ref, nm_ref, nv_ref):
        def step(gv):
            nm = ADAM_B1 * m_ref[...] + (1.0 - ADAM_B1) * gv
            nv = ADAM_B2 * v_ref[...] + (1.0 - ADAM_B2) * (gv * gv)
            g_ref[...] = gv
            nm_ref[...] = nm
            nv_ref[...] = nv
            d_ref[...] = -ADAM_LR * ((nm * c1) / (jnp.sqrt(nv * c2) + ADAM_EPS) + ADAM_WD * w_ref[...])

        @pl.when(pl.program_id(0) == 0)
        def _():
            step(g0_ref[...])

        @pl.when(pl.program_id(0) == 1)
        def _():
            step(g1_ref[...])

    blk = pl.BlockSpec((None, bm, n), lambda l, i: (l, i, 0))
    out = jax.ShapeDtypeStruct(w.shape, F32)
    return pl.pallas_call(
        body, name=name, grid=(2, k // bm),
        in_specs=[blk, pl.BlockSpec((bm, n), lambda l, i: (i * (1 - l), 0)), pl.BlockSpec((bm, n), lambda l, i: (i * l, 0)),
                  blk, blk],
        out_specs=[blk] * 4, out_shape=[out] * 4, compiler_params=_params("arbitrary", "arbitrary"))(w, g0, g1, m, v)


def _pack(arrays):
    flat = jnp.concatenate([a.reshape(-1) for a in arrays])
    rows = -(-flat.shape[0] // (256 * 128)) * 256
    return jnp.pad(flat, (0, rows * 128 - flat.shape[0])).reshape(rows, 128)


def _unpack(p, shapes):
    flat, out, at = p.reshape(-1), [], 0
    for s in shapes:
        n = math.prod(s)
        out.append(flat[at:at + n].reshape(s))
        at += n
    return out


def kernel(x, mem, norm_ffn1, ffn1_w_in, ffn1_w_out, norm_mix, mix_w_in, sconv_w, sgu_norm_g, sgu_w, sgu_b, cconv_w, cconv_ln_g, cconv_ln_b, pool_w, pool_scale, mix_w_out, norm_xattn, norm_mem, xattn_wq, xattn_wkv, xattn_wo, norm_ffn2, ffn2_w_in, ffn2_w_out, norm_final, loss_target, m_norm_ffn1, m_ffn1_w_in, m_ffn1_w_out, m_norm_mix, m_mix_w_in, m_sconv_w, m_sgu_norm_g, m_sgu_w, m_sgu_b, m_cconv_w, m_cconv_ln_g, m_cconv_ln_b, m_pool_w, m_pool_scale, m_mix_w_out, m_norm_xattn, m_norm_mem, m_xattn_wq, m_xattn_wkv, m_xattn_wo, m_norm_ffn2, m_ffn2_w_in, m_ffn2_w_out, m_norm_final, v_norm_ffn1, v_ffn1_w_in, v_ffn1_w_out, v_norm_mix, v_mix_w_in, v_sconv_w, v_sgu_norm_g, v_sgu_w, v_sgu_b, v_cconv_w, v_cconv_ln_g, v_cconv_ln_b, v_pool_w, v_pool_scale, v_mix_w_out, v_norm_xattn, v_norm_mem, v_xattn_wq, v_xattn_wkv, v_xattn_wo, v_norm_ffn2, v_ffn2_w_in, v_ffn2_w_out, v_norm_final):
    given = dict(locals())
    w = {n: given[n] for n in WEIGHTS}
    L = ffn1_w_in.shape[0]
    assert L == 2, "the reduce-scatter gives one layer to each core of a chip"
    chip = 2 * lax.axis_index("x") + lax.axis_index("y")
    chip1 = chip.astype(jnp.int32).reshape(1)
    core = lax.axis_index("c").astype(jnp.int32).reshape(1)
    place = jnp.concatenate([chip1, core])

    axis = {n: 1 if n in COL_SHARDED else 0 for n in BIG}
    bufs = {}
    for n in BIG:
        for l, b in enumerate(cast_into_slot(w[n], axis[n] + 1, chip1, "cast_weights")):
            bufs[n, l] = b
    groups = {"a": [(n, 0) for n in BIG[:2]], "b": [(n, 0) for n in BIG[2:4]], "c": [(n, 0) for n in BIG[4:7]],
              "d": [(n, 0) for n in BIG[7:]], "e": [(n, 1) for n in BIG[:2]], "f": [(n, 1) for n in BIG[2:]]}
    wc = sconv_w.shape[-1]
    conv_rows = [w[n].reshape(-1, wc) for n in SMALL_CONV]
    n_conv = sum(r.shape[0] for r in conv_rows)
    conv_pack = jnp.pad(jnp.concatenate(conv_rows, axis=0), ((0, -n_conv % 8), (0, 128 - wc)))[None]
    conv_all = all_gather_chips([conv_pack], [0], "gather_conv")[0]
    started, token = {}, conv_all
    for g, keys in groups.items():
        send, recv, thru, token = split_start("gather_start_" + g, [], [bufs[k] for k in keys], 3 * len(keys),
                                              gather_copies([axis[k[0]] for k in keys]), [token])
        started[g] = (send, recv, thru)
    ready, ahead = {}, {}
    small = SMALL_REPL + SMALL_CONV
    packed = [_pack([src[n] for n in small]) for src in
              (w, {n: given["m_" + n] for n in small}, {n: given["v_" + n] for n in small})]

    def arrived(g, after):
        send, recv, thru = started[g]
        axes = [axis[k[0]] for k in groups[g]]
        return axes, split_wait("gather_wait_" + g, [], thru, send, recv, gather_copies(axes), after)

    def fetch(n, l, after):
        g = next(g for g, keys in groups.items() if (n, l) in keys)
        if g not in ready:
            if g in ahead:
                axes, send, recv, thru, token_g = ahead.pop(g)
                done = split_wait("forward_wait_" + g, [], thru, send, recv, forward_copies(axes), [after, token_g])
            else:
                axes, done = arrived(g, [token] + packed if g == "a" else [after])
                done = forward_sibling(done, axes, "gather_forward")
            if g == "e":
                axes_f, landed = arrived("f", [after])
                send, recv, thru, token_f = split_start("forward_start_f", [], landed, 3 * len(landed),
                                                        forward_copies(axes_f), [after])
                done, (token_f,) = lax.optimization_barrier((done, [token_f]))
                ahead["f"] = (axes_f, send, recv, thru, token_f)
            ready[g] = dict(zip(groups[g], done))
        return ready[g][n, l]

    conv_full = jnp.moveaxis(conv_all[:, :n_conv, :wc], 0, 1).reshape(n_conv, N_CHIPS * wc)
    ws = {n: w[n] for n in SMALL_REPL}
    at = 0
    for n in SMALL_CONV:
        rows = w[n].shape[0] * w[n].shape[1]
        ws[n] = conv_full[at:at + rows].reshape(w[n].shape[0], w[n].shape[1], N_CHIPS * wc)
        at += rows

    halves, state = {}, {}
    reduce_groups = {"r1": [(n, 1) for n in BIG], "r0a": [(n, 0) for n in BIG[2:]], "r0b": [(n, 0) for n in BIG[:2]]}
    plan = {("layer", 1): [("pair", "r1")],
            ("ffn2", 0): [("chips", "r1")],
            ("mix", 0): [("finish", "r1"), ("share", "r1"), ("pair", "r0a")],
            ("ffn1_mid", 0): [("chips", "r0a")],
            ("ffn1_grads", 0): [("pair", "r0b")],
            ("layer", 0): [("finish", "r0a")]}


    def stage_pair(g, keys, grads, after):
        gvs = [_grad_view(grads[n][l], axis[n]) for n, l in keys]
        others = [lax.empty(gv.shape[:1] + gv.shape[2:], F32) for gv in gvs]
        send, recv, others, token = split_start("pair_start_" + g, gvs, others, len(gvs), pair_copies, [after])
        state[g] = dict(sources=gvs, send=send, recv=recv, landing=others, token=token)
        return [(state[g], "token")]

    def stage_chips(g, keys, grads, after):
        st = state[g]
        axes = [axis[n] for n, _ in keys]
        others = split_wait("pair_wait_" + g, st["sources"], st["landing"], st["send"], st["recv"], pair_copies,
                            [after, st["token"]])
        sums = [add_pair(gv, o, place, "add_pair") for gv, o in zip(st["sources"], others)]
        qs = [lax.empty((3, s.shape[1], s.shape[2] // (N_CHIPS if ax == 1 else 1)), GRAD_WIRE) for s, ax in zip(sums, axes)]
        send, recv, qs, token = split_start("chips_start_" + g, sums, qs, 3 * len(sums), chip_copies(axes), [after])
        state[g] = dict(sources=sums, send=send, recv=recv, landing=qs, token=token)
        return [(state[g], "token")]

    def stage_finish(g, keys, grads, after):
        st = state.pop(g)
        qs = split_wait("chips_wait_" + g, st["sources"], st["landing"], st["send"], st["recv"],
                        chip_copies([axis[n] for n, _ in keys]), [after, st["token"]])
        for key, s, q in zip(keys, st["sources"], qs):
            halves[key] = add_chips(s, q, axis[key[0]], place, "add_chips")
        return [(halves, key) for key in keys]

    def stage_share(g, keys, grads, after):
        send, recv, thru, token = split_start("share_start_" + g, [], [halves.pop(k) for k in keys], len(keys),
                                              share_copies, [after])
        state["share_" + g] = dict(keys=keys, send=send, recv=recv, landing=thru, token=token)
        return [(state["share_" + g], "token")]

    stages = {"pair": stage_pair, "chips": stage_chips, "finish": stage_finish, "share": stage_share}

    def progress(event, l, grads, values):
        places = []
        for stage, g in plan.get((event, l), []):
            places += stages[stage](g, reduce_groups[g], grads, values[0])
        places = [(box, k) for box, k in places if k in box]
        if places:
            values, tied = lax.optimization_barrier((values, [box[k] for box, k in places]))
            for (box, k), a in zip(places, tied):
                box[k] = a
        return values

    loss_part, grad_x, grads = _local_step(x[0], mem[0], loss_target[0], fetch, ws, L, progress)
    loss = lax.psum(loss_part, ("x", "y", "c"))

    small_g = [grads[n] if n == "norm_final" else jnp.stack(grads[n]) for n in small]
    small_sum = all_reduce_small(_pack(small_g), "reduce_small")
    grad = dict(zip(small, _unpack(small_sum, [g.shape for g in small_g])))
    for n in SMALL_CONV:
        grad[n] = lax.dynamic_slice_in_dim(grad[n], chip * wc, wc, axis=2)

    delta, new_m, new_v = {}, {}, {}

    sent = state.pop("share_r1")
    layer1 = dict(zip(sent["keys"], split_wait("share_wait_r1", [], sent["landing"], sent["send"], sent["recv"],
                                               share_copies, [small_sum, sent["token"]])))

    def finish_weights(names, layer0_halves):
        layer0 = dict(zip(names, share_sibling(layer0_halves, "share_pair")))
        for n in names:
            g0, g1 = layer0[n].reshape(w[n].shape[1:]), layer1[n, 1].reshape(w[n].shape[1:])
            grad[n], delta[n], new_m[n], new_v[n] = adamw_layers(w[n], g0, g1, given["m_" + n], given["v_" + n], "adamw")

    stage_chips("r0b", reduce_groups["r0b"], grads, small_sum)
    early, (state["r0b"]["token"],) = lax.optimization_barrier(([halves[n, 0] for n in BIG[2:]], [state["r0b"]["token"]]))
    finish_weights(BIG[2:], early)
    stage_finish("r0b", reduce_groups["r0b"], grads, delta[BIG[-1]])
    finish_weights(BIG[:2], [halves[n, 0] for n in BIG[:2]])
    shapes = [w[n].shape for n in small]
    packed.insert(1, _pack([grad[n] for n in small]))
    for out, p in zip((delta, new_m, new_v), adamw(*packed, "adamw_small")):
        out.update(zip(small, _unpack(p, shapes)))

    return (loss, grad_x[None], *[grad[n] for n in WEIGHTS], *[delta[n] for n in WEIGHTS],
            *[new_m[n] for n in WEIGHTS], *[new_v[n] for n in WEIGHTS])
```
